```python
import math
import jax, jax.numpy as jnp
from jax import lax
import numpy as np

D_MODEL = 2048
BATCH = 8
SEQ = 2048
DEPTH = 2

CHUNK = 64

CONV_WIDTH = D_MODEL
CONV_K = 3
SSM_WIDTH = D_MODEL // 2
SSM_GROUP = 16
SSM_GROUPS = SSM_WIDTH // SSM_GROUP
SSM_STATE = 64
DT_MIN = 0.001
DT_MAX = 0.1
RMS_EPS = 1e-6

_SIZES = (CONV_WIDTH, CONV_WIDTH, CONV_WIDTH, CONV_WIDTH,
          SSM_WIDTH, SSM_WIDTH,
          D_MODEL, D_MODEL)
N_IN = int(sum(_SIZES))
SPLITS = tuple(int(s) for s in np.cumsum(_SIZES)[:-1])

kernel_name = "hybrid_conv_s5_gated_encoder"


def rmsnorm(x, g):
    x32 = x.astype(jnp.float32)
    y = x32 * lax.rsqrt(jnp.mean(x32 * x32, axis=-1, keepdims=True) + RMS_EPS)
    return (y * g.astype(jnp.float32)).astype(x.dtype)


def causal_depthwise_conv(v, w):
    L = v.shape[1]
    vp = jnp.pad(v, ((0, 0), (CONV_K - 1, 0), (0, 0)))
    out = w[0] * vp[:, 0:L]
    for k in range(1, CONV_K):
        out = out + w[k] * vp[:, k:k + L]
    return out


def s5_scan(u, a_re, a_im, log_dt, b_re, b_im, c_re, c_im, d_skip):
    bsz, L, _ = u.shape
    f32 = jnp.float32
    ug = u.astype(f32).reshape(bsz, L, SSM_GROUPS, SSM_GROUP)
    lam = lax.complex(a_re.astype(f32), a_im.astype(f32))
    dt = jnp.exp(log_dt.astype(f32))[:, None]
    lam_bar = jnp.exp(lam * dt)
    b = lax.complex(b_re.astype(f32), b_im.astype(f32))
    b_bar = ((lam_bar - 1.0) / lam)[..., None] * b
    bu = jnp.einsum('gpc,blgc->blgp', b_bar, ug.astype(jnp.complex64))
    a = jnp.broadcast_to(lam_bar, bu.shape)

    def combine(e1, e2):
        a1, h1 = e1
        a2, h2 = e2
        return a1 * a2, a2 * h1 + h2

    _, h = lax.associative_scan(combine, (a, bu), axis=1)
    c = lax.complex(c_re.astype(f32), c_im.astype(f32))
    y = jnp.einsum('gcp,blgp->blgc', c, h).real + d_skip.astype(f32) * ug
    return y.reshape(bsz, L, SSM_WIDTH)


def hybrid_layer(x, norm_g, w_in, conv_w, w_out_a, a_re, a_im, log_dt, b_re, b_im,
                 c_re, c_im, d_skip, w_glu, b_glu, w_out_b, w_o):
    h = rmsnorm(x, norm_g)
    proj = jnp.einsum('bld,dn->bln', h, w_in)
    v, bg, cg, za, u, zb, ga, gb = jnp.split(proj, SPLITS, axis=-1)
    ya = bg * causal_depthwise_conv(cg * v, conv_w)
    ya = jnp.einsum('blc,cd->bld', ya * jax.nn.silu(za), w_out_a)
    yb = jax.nn.gelu(s5_scan(u, a_re, a_im, log_dt, b_re, b_im, c_re, c_im, d_skip))
    yb = yb * jax.nn.sigmoid(jnp.einsum('blc,ce->ble', yb, w_glu.astype(jnp.float32))
                             + b_glu.astype(jnp.float32))
    yb = yb.astype(x.dtype)
    yb = jnp.einsum('blc,cd->bld', yb * jax.nn.silu(zb), w_out_b)
    m = jax.nn.sigmoid(ga) * ya + jax.nn.sigmoid(gb) * yb
    return x + jnp.einsum('bld,de->ble', m, w_o)


def _fwd_setup_inputs(seed: int = 0) -> dict:
    key = jax.random.key(seed)
    ks = jax.random.split(key, 20)
    f32 = jnp.float32
    n = lambda k, shape, s: jax.random.normal(k, shape, f32) * s
    res_scale = 1.0 / math.sqrt(2.0 * DEPTH)
    G, P, c = SSM_GROUPS, SSM_STATE, SSM_GROUP
    a_im_base = math.pi * jnp.arange(P, dtype=f32)
    return {
        "x": n(ks[0], (BATCH, SEQ, D_MODEL), 1.0),
        "norm_g": 1.0 + n(ks[1], (DEPTH, D_MODEL), 0.02),
        "w_in": n(ks[2], (DEPTH, D_MODEL, N_IN), D_MODEL ** -0.5),
        "conv_w": n(ks[3], (DEPTH, CONV_K, CONV_WIDTH), CONV_K ** -0.5),
        "w_out_a": n(ks[4], (DEPTH, CONV_WIDTH, D_MODEL), CONV_WIDTH ** -0.5),
        "a_re": -0.5 + n(ks[5], (DEPTH, G, P), 0.01),
        "a_im": a_im_base + n(ks[6], (DEPTH, G, P), 0.01),
        "log_dt": jax.random.uniform(ks[7], (DEPTH, G), f32,
                                     math.log(DT_MIN), math.log(DT_MAX)),
        "b_re": n(ks[8], (DEPTH, G, P, c), (2.0 * c) ** -0.5),
        "b_im": n(ks[9], (DEPTH, G, P, c), (2.0 * c) ** -0.5),
        "c_re": n(ks[10], (DEPTH, G, c, P), (2.0 * P) ** -0.5),
        "c_im": n(ks[11], (DEPTH, G, c, P), (2.0 * P) ** -0.5),
        "d_skip": n(ks[12], (DEPTH, G, c), 1.0),
        "w_glu": n(ks[13], (DEPTH, SSM_WIDTH, SSM_WIDTH), SSM_WIDTH ** -0.5),
        "b_glu": n(ks[14], (DEPTH, SSM_WIDTH), 0.02),
        "w_out_b": n(ks[15], (DEPTH, SSM_WIDTH, D_MODEL), SSM_WIDTH ** -0.5),
        "w_o": n(ks[16], (DEPTH, D_MODEL, D_MODEL), D_MODEL ** -0.5 * res_scale),
        "final_g": 1.0 + n(ks[17], (D_MODEL,), 0.02),
    }


def _fwd_reference(x, norm_g, w_in, conv_w, w_out_a, a_re, a_im, log_dt, b_re, b_im,
              c_re, c_im, d_skip, w_glu, b_glu, w_out_b, w_o, final_g):
    for i in range(DEPTH):
        x = hybrid_layer(x, norm_g[i], w_in[i], conv_w[i], w_out_a[i], a_re[i], a_im[i],
                         log_dt[i], b_re[i], b_im[i], c_re[i], c_im[i], d_skip[i],
                         w_glu[i], b_glu[i], w_out_b[i], w_o[i])
    return rmsnorm(x, final_g)


import jax as _jax
import jax.numpy as _jnp

TWIN_FORMAT = 'train_step'
FWD_PARAMS = ['x', 'norm_g', 'w_in', 'conv_w', 'w_out_a', 'a_re', 'a_im', 'log_dt', 'b_re', 'b_im', 'c_re', 'c_im', 'd_skip', 'w_glu', 'b_glu', 'w_out_b', 'w_o', 'final_g']
TWIN_WEIGHTS = ['norm_g', 'w_in', 'conv_w', 'w_out_a', 'a_re', 'a_im', 'log_dt', 'b_re', 'b_im', 'c_re', 'c_im', 'd_skip', 'w_glu', 'b_glu', 'w_out_b', 'w_o', 'final_g']
TWIN_DIFF_INPUT = 'x'
TWIN_INPUTS = ['x', 'norm_g', 'w_in', 'conv_w', 'w_out_a', 'a_re', 'a_im', 'log_dt', 'b_re', 'b_im', 'c_re', 'c_im', 'd_skip', 'w_glu', 'b_glu', 'w_out_b', 'w_o', 'final_g', 'loss_target', 'm_norm_g', 'm_w_in', 'm_conv_w', 'm_w_out_a', 'm_a_re', 'm_a_im', 'm_log_dt', 'm_b_re', 'm_b_im', 'm_c_re', 'm_c_im', 'm_d_skip', 'm_w_glu', 'm_b_glu', 'm_w_out_b', 'm_w_o', 'm_final_g', 'v_norm_g', 'v_w_in', 'v_conv_w', 'v_w_out_a', 'v_a_re', 'v_a_im', 'v_log_dt', 'v_b_re', 'v_b_im', 'v_c_re', 'v_c_im', 'v_d_skip', 'v_w_glu', 'v_b_glu', 'v_w_out_b', 'v_w_o', 'v_final_g']
TWIN_OUTPUTS = ['loss', 'grad_x', 'grad_norm_g', 'grad_w_in', 'grad_conv_w', 'grad_w_out_a', 'grad_a_re', 'grad_a_im', 'grad_log_dt', 'grad_b_re', 'grad_b_im', 'grad_c_re', 'grad_c_im', 'grad_d_skip', 'grad_w_glu', 'grad_b_glu', 'grad_w_out_b', 'grad_w_o', 'grad_final_g', 'delta_norm_g', 'delta_w_in', 'delta_conv_w', 'delta_w_out_a', 'delta_a_re', 'delta_a_im', 'delta_log_dt', 'delta_b_re', 'delta_b_im', 'delta_c_re', 'delta_c_im', 'delta_d_skip', 'delta_w_glu', 'delta_b_glu', 'delta_w_out_b', 'delta_w_o', 'delta_final_g', 'new_m_norm_g', 'new_m_w_in', 'new_m_conv_w', 'new_m_w_out_a', 'new_m_a_re', 'new_m_a_im', 'new_m_log_dt', 'new_m_b_re', 'new_m_b_im', 'new_m_c_re', 'new_m_c_im', 'new_m_d_skip', 'new_m_w_glu', 'new_m_b_glu', 'new_m_w_out_b', 'new_m_w_o', 'new_m_final_g', 'new_v_norm_g', 'new_v_w_in', 'new_v_conv_w', 'new_v_w_out_a', 'new_v_a_re', 'new_v_a_im', 'new_v_log_dt', 'new_v_b_re', 'new_v_b_im', 'new_v_c_re', 'new_v_c_im', 'new_v_d_skip', 'new_v_w_glu', 'new_v_b_glu', 'new_v_w_out_b', 'new_v_w_o', 'new_v_final_g']
TWIN_LEAF_KINDS = {'loss': 'loss', 'grad_x': 'grad_x', 'grad_norm_g': 'grad_w', 'grad_w_in': 'grad_w', 'grad_conv_w': 'grad_w', 'grad_w_out_a': 'grad_w', 'grad_a_re': 'grad_w', 'grad_a_im': 'grad_w', 'grad_log_dt': 'grad_w', 'grad_b_re': 'grad_w', 'grad_b_im': 'grad_w', 'grad_c_re': 'grad_w', 'grad_c_im': 'grad_w', 'grad_d_skip': 'grad_w', 'grad_w_glu': 'grad_w', 'grad_b_glu': 'grad_w', 'grad_w_out_b': 'grad_w', 'grad_w_o': 'grad_w', 'grad_final_g': 'grad_w', 'delta_norm_g': 'delta_w', 'delta_w_in': 'delta_w', 'delta_conv_w': 'delta_w', 'delta_w_out_a': 'delta_w', 'delta_a_re': 'delta_w', 'delta_a_im': 'delta_w', 'delta_log_dt': 'delta_w', 'delta_b_re': 'delta_w', 'delta_b_im': 'delta_w', 'delta_c_re': 'delta_w', 'delta_c_im': 'delta_w', 'delta_d_skip': 'delta_w', 'delta_w_glu': 'delta_w', 'delta_b_glu': 'delta_w', 'delta_w_out_b': 'delta_w', 'delta_w_o': 'delta_w', 'delta_final_g': 'delta_w', 'new_m_norm_g': 'new_m', 'new_m_w_in': 'new_m', 'new_m_conv_w': 'new_m', 'new_m_w_out_a': 'new_m', 'new_m_a_re': 'new_m', 'new_m_a_im': 'new_m', 'new_m_log_dt': 'new_m', 'new_m_b_re': 'new_m', 'new_m_b_im': 'new_m', 'new_m_c_re': 'new_m', 'new_m_c_im': 'new_m', 'new_m_d_skip': 'new_m', 'new_m_w_glu': 'new_m', 'new_m_b_glu': 'new_m', 'new_m_w_out_b': 'new_m', 'new_m_w_o': 'new_m', 'new_m_final_g': 'new_m', 'new_v_norm_g': 'new_v', 'new_v_w_in': 'new_v', 'new_v_conv_w': 'new_v', 'new_v_w_out_a': 'new_v', 'new_v_a_re': 'new_v', 'new_v_a_im': 'new_v', 'new_v_log_dt': 'new_v', 'new_v_b_re': 'new_v', 'new_v_b_im': 'new_v', 'new_v_c_re': 'new_v', 'new_v_c_im': 'new_v', 'new_v_d_skip': 'new_v', 'new_v_w_glu': 'new_v', 'new_v_b_glu': 'new_v', 'new_v_w_out_b': 'new_v', 'new_v_w_o': 'new_v', 'new_v_final_g': 'new_v'}


def _forward(args):
    return _fwd_reference(*[args[k] for k in FWD_PARAMS])


def _output_shape():
    out = _jax.eval_shape(lambda: _forward(_fwd_setup_inputs(0)))
    return out.shape, out.dtype

N_MICROBATCH = 1
ADAM_LR = 0.001
ADAM_B1 = 0.9
ADAM_B2 = 0.999
ADAM_EPS = 1e-08
ADAM_WD = 0.01
ADAM_STEP = 10
PER_EXAMPLE_BATCH_AXIS = {'x': 0, 'loss_target': 0}
SHARED_INPUTS = []
_WEIGHT_DTYPES = {'norm_g': _jnp.float32, 'w_in': _jnp.float32, 'conv_w': _jnp.float32, 'w_out_a': _jnp.float32, 'a_re': _jnp.float32, 'a_im': _jnp.float32, 'log_dt': _jnp.float32, 'b_re': _jnp.float32, 'b_im': _jnp.float32, 'c_re': _jnp.float32, 'c_im': _jnp.float32, 'd_skip': _jnp.float32, 'w_glu': _jnp.float32, 'b_glu': _jnp.float32, 'w_out_b': _jnp.float32, 'w_o': _jnp.float32, 'final_g': _jnp.float32}
MOMENT_SCALE = {'norm_g': 2.146479e-02, 'w_in': 8.102874e-03, 'conv_w': 1.012641e-02, 'w_out_a': 1.015104e-02, 'a_re': 2.603692e-04, 'a_im': 2.571272e-04, 'log_dt': 2.315835e-01, 'b_re': 1.660823e-04, 'b_im': 1.676337e-04, 'c_re': 3.310938e-04, 'c_im': 3.290397e-04, 'd_skip': 5.524779e-03, 'w_glu': 1.434714e-03, 'b_glu': 2.204780e-03, 'w_out_b': 3.510106e-03, 'w_o': 2.151568e-02, 'final_g': 7.999222e+00}


def _to_microbatches(a, axis):
    t = _jnp.moveaxis(a, axis, 0)
    t = t.reshape((N_MICROBATCH, t.shape[0] // N_MICROBATCH) + t.shape[1:])
    return _jnp.moveaxis(t, 1, axis + 1)


def setup_inputs(seed: int = 0) -> dict:
    inp = _fwd_setup_inputs(seed)
    key = _jax.random.fold_in(_jax.random.key(seed), 7919)
    shape, _ = _output_shape()
    out = dict(inp)
    out["loss_target"] = _jax.random.normal(_jax.random.fold_in(key, 0), shape, _jnp.float32)
    for i, name in enumerate(TWIN_WEIGHTS):
        w = inp[name].astype(_jnp.float32)
        if MOMENT_SCALE is None:
            s = _jnp.sqrt(_jnp.mean(_jnp.square(w)) + 1e-30)
        else:
            s = MOMENT_SCALE[name]
        km, kv = _jax.random.split(_jax.random.fold_in(key, i + 1))
        out[name] = w
        out["m_" + name] = s * _jax.random.normal(km, w.shape, _jnp.float32)
        out["v_" + name] = (s * s) * _jax.random.uniform(kv, w.shape, _jnp.float32, 0.5, 1.5)
    if N_MICROBATCH > 1:
        for name, axis in PER_EXAMPLE_BATCH_AXIS.items():
            out[name] = _to_microbatches(out[name], axis)
    return {'x': out['x'], 'norm_g': out['norm_g'], 'w_in': out['w_in'], 'conv_w': out['conv_w'], 'w_out_a': out['w_out_a'], 'a_re': out['a_re'], 'a_im': out['a_im'], 'log_dt': out['log_dt'], 'b_re': out['b_re'], 'b_im': out['b_im'], 'c_re': out['c_re'], 'c_im': out['c_im'], 'd_skip': out['d_skip'], 'w_glu': out['w_glu'], 'b_glu': out['b_glu'], 'w_out_b': out['w_out_b'], 'w_o': out['w_o'], 'final_g': out['final_g'], 'loss_target': out['loss_target'], 'm_norm_g': out['m_norm_g'], 'm_w_in': out['m_w_in'], 'm_conv_w': out['m_conv_w'], 'm_w_out_a': out['m_w_out_a'], 'm_a_re': out['m_a_re'], 'm_a_im': out['m_a_im'], 'm_log_dt': out['m_log_dt'], 'm_b_re': out['m_b_re'], 'm_b_im': out['m_b_im'], 'm_c_re': out['m_c_re'], 'm_c_im': out['m_c_im'], 'm_d_skip': out['m_d_skip'], 'm_w_glu': out['m_w_glu'], 'm_b_glu': out['m_b_glu'], 'm_w_out_b': out['m_w_out_b'], 'm_w_o': out['m_w_o'], 'm_final_g': out['m_final_g'], 'v_norm_g': out['v_norm_g'], 'v_w_in': out['v_w_in'], 'v_conv_w': out['v_conv_w'], 'v_w_out_a': out['v_w_out_a'], 'v_a_re': out['v_a_re'], 'v_a_im': out['v_a_im'], 'v_log_dt': out['v_log_dt'], 'v_b_re': out['v_b_re'], 'v_b_im': out['v_b_im'], 'v_c_re': out['v_c_re'], 'v_c_im': out['v_c_im'], 'v_d_skip': out['v_d_skip'], 'v_w_glu': out['v_w_glu'], 'v_b_glu': out['v_b_glu'], 'v_w_out_b': out['v_w_out_b'], 'v_w_o': out['v_w_o'], 'v_final_g': out['v_final_g']}


def _loss(weights, diff, rest, loss_target):
    with _jax.named_scope("forward"):
        args = {**rest, TWIN_DIFF_INPUT: diff, **{k: w.astype(_WEIGHT_DTYPES[k]) for k, w in weights.items()}}
        y = _forward(args)
    with _jax.named_scope("loss_head"):
        err = _jnp.square(y.astype(_jnp.float32) - loss_target)
        return 0.5 * _jnp.sum(_jnp.mean(err, axis=-1)) if err.ndim else 0.5 * err


def _adamw(w, g, m, v):
    m = ADAM_B1 * m + (1.0 - ADAM_B1) * g
    v = ADAM_B2 * v + (1.0 - ADAM_B2) * _jnp.square(g)
    m_hat = m / (1.0 - ADAM_B1 ** ADAM_STEP)
    v_hat = v / (1.0 - ADAM_B2 ** ADAM_STEP)
    delta = -ADAM_LR * (m_hat / (_jnp.sqrt(v_hat) + ADAM_EPS) + ADAM_WD * w)
    return delta, m, v


def reference(x, norm_g, w_in, conv_w, w_out_a, a_re, a_im, log_dt, b_re, b_im, c_re, c_im, d_skip, w_glu, b_glu, w_out_b, w_o, final_g, loss_target, m_norm_g, m_w_in, m_conv_w, m_w_out_a, m_a_re, m_a_im, m_log_dt, m_b_re, m_b_im, m_c_re, m_c_im, m_d_skip, m_w_glu, m_b_glu, m_w_out_b, m_w_o, m_final_g, v_norm_g, v_w_in, v_conv_w, v_w_out_a, v_a_re, v_a_im, v_log_dt, v_b_re, v_b_im, v_c_re, v_c_im, v_d_skip, v_w_glu, v_b_glu, v_w_out_b, v_w_o, v_final_g):
    given = dict(x=x, norm_g=norm_g, w_in=w_in, conv_w=conv_w, w_out_a=w_out_a, a_re=a_re, a_im=a_im, log_dt=log_dt, b_re=b_re, b_im=b_im, c_re=c_re, c_im=c_im, d_skip=d_skip, w_glu=w_glu, b_glu=b_glu, w_out_b=w_out_b, w_o=w_o, final_g=final_g, loss_target=loss_target, m_norm_g=m_norm_g, m_w_in=m_w_in, m_conv_w=m_conv_w, m_w_out_a=m_w_out_a, m_a_re=m_a_re, m_a_im=m_a_im, m_log_dt=m_log_dt, m_b_re=m_b_re, m_b_im=m_b_im, m_c_re=m_c_re, m_c_im=m_c_im, m_d_skip=m_d_skip, m_w_glu=m_w_glu, m_b_glu=m_b_glu, m_w_out_b=m_w_out_b, m_w_o=m_w_o, m_final_g=m_final_g, v_norm_g=v_norm_g, v_w_in=v_w_in, v_conv_w=v_conv_w, v_w_out_a=v_w_out_a, v_a_re=v_a_re, v_a_im=v_a_im, v_log_dt=v_log_dt, v_b_re=v_b_re, v_b_im=v_b_im, v_c_re=v_c_re, v_c_im=v_c_im, v_d_skip=v_d_skip, v_w_glu=v_w_glu, v_b_glu=v_b_glu, v_w_out_b=v_w_out_b, v_w_o=v_w_o, v_final_g=v_final_g)
    weights = {n: given[n] for n in TWIN_WEIGHTS}
    shared = {n: given[n] for n in SHARED_INPUTS}
    per_example = {n: given[n] for n in ['x']}
    grad_fn = _jax.value_and_grad(_loss, argnums=(0, 1))

    def one_microbatch(ex, loss_target):
        ex = dict(ex)
        diff = ex.pop(TWIN_DIFF_INPUT)
        return grad_fn(weights, diff, {**shared, **ex}, loss_target)

    if N_MICROBATCH == 1:
        loss, (grad_w, grad_x) = one_microbatch(per_example, given["loss_target"])
    else:
        def body(carry, xs):
            loss_sum, grad_sum = carry
            l_k, (gw_k, gx_k) = one_microbatch(xs[0], xs[1])
            with _jax.named_scope("update"):
                return (loss_sum + l_k, _jax.tree.map(_jnp.add, grad_sum, gw_k)), gx_k

        init = (_jnp.zeros((), _jnp.float32), _jax.tree.map(_jnp.zeros_like, weights))
        (loss, grad_w), grad_x = _jax.lax.scan(body, init, (per_example, given["loss_target"]))
    with _jax.named_scope("update"):
        delta_w, new_m, new_v = {}, {}, {}
        for n in TWIN_WEIGHTS:
            delta_w[n], new_m[n], new_v[n] = _adamw(weights[n], grad_w[n], given["m_" + n], given["v_" + n])
    return (loss, grad_x, *[grad_w[n] for n in TWIN_WEIGHTS], *[delta_w[n] for n in TWIN_WEIGHTS],
            *[new_m[n] for n in TWIN_WEIGHTS], *[new_v[n] for n in TWIN_WEIGHTS])
```

```python
import functools
import math

import jax
import jax.numpy as jnp
from jax import lax
from jax.experimental import pallas as pl
from jax.experimental.pallas import tpu as pltpu

F32 = jnp.float32
BF16 = jnp.bfloat16
HIGHEST = lax.Precision.HIGHEST

N_DEV = 8
MESH_AXES = ("x", "y", "c")
LANES = 128
SUBLANES = 8
VMEM_LIMIT_BYTES = 56 * 1024 * 1024

RMS_EPS = 1e-6
ADAM_LR = 0.001
ADAM_B1 = 0.9
ADAM_B2 = 0.999
ADAM_EPS = 1e-08
ADAM_WD = 0.01
ADAM_STEP = 10
GELU_C0 = math.sqrt(2.0 / math.pi)
GELU_C1 = 0.044715

ADAMW_BLOCK_ELEMS = 1 << 17
PACK_ROWS = 512

S5_GROUP = 16
S5_GB = LANES // S5_GROUP


def _params(*semantics):
    return pltpu.CompilerParams(dimension_semantics=semantics, vmem_limit_bytes=VMEM_LIMIT_BYTES)


def _tile(n, pref):
    t = min(n, pref)
    while n % t:
        assert t % 2 == 0, (n, pref)
        t //= 2
    return t


def _sigmoid(z):
    return 1.0 / (1.0 + jnp.exp(-z))


def _gelu(y):
    return 0.5 * y * (1.0 + jnp.tanh(GELU_C0 * (y + GELU_C1 * y * y * y)))


def _gelu_grad(y):
    t = jnp.tanh(GELU_C0 * (y + GELU_C1 * y * y * y))
    return 0.5 * (1.0 + t) + 0.5 * y * (1.0 - t * t) * GELU_C0 * (1.0 + 3.0 * GELU_C1 * y * y)


def _all_gather(shard, name):
    r, c_ = shard.shape

    def body(x_ref, out_ref, send_sems, recv_sems, local_sem):
        x, y, c = lax.axis_index("x"), lax.axis_index("y"), lax.axis_index("c")
        me, sibling = (x, y, c), (x, y, 1 - c)
        chips = [(1 - x, y), (x, 1 - y), (1 - x, 1 - y)]

        def slot(px, py, pc):
            return out_ref.at[4 * px + 2 * py + pc]

        def copy(k, block, to, src=None):
            return pltpu.make_async_remote_copy(
                src_ref=slot(*block) if src is None else src, dst_ref=slot(*block),
                send_sem=send_sems.at[k], recv_sem=recv_sems.at[k],
                device_id=to, device_id_type=pl.DeviceIdType.MESH)

        mine = pltpu.make_async_copy(x_ref, slot(*me), local_sem)
        mine.start()
        first = [copy(0, me, sibling, src=x_ref)]
        first += [copy(1 + j, me, (*chip, c), src=x_ref) for j, chip in enumerate(chips)]
        for cp in first:
            cp.start()
        passed = [copy(4 + j, (*chip, c), sibling) for j, chip in enumerate(chips)]
        for j, chip in enumerate(chips):
            copy(1 + j, (*chip, c), me).wait_recv()
            passed[j].start()
        copy(0, sibling, me).wait_recv()
        for j, chip in enumerate(chips):
            copy(4 + j, (*chip, 1 - c), me).wait_recv()
        for cp in first + passed:
            cp.wait_send()
        mine.wait()

    return pl.pallas_call(
        body, name=name,
        out_shape=jax.ShapeDtypeStruct((N_DEV, r, c_), shard.dtype),
        in_specs=[pl.BlockSpec(memory_space=pl.ANY)],
        out_specs=pl.BlockSpec(memory_space=pl.ANY),
        scratch_shapes=[pltpu.SemaphoreType.DMA((7,)), pltpu.SemaphoreType.DMA((7,)),
                        pltpu.SemaphoreType.DMA],
    )(shard)


def _all_to_all(pieces, name):
    def body(in_ref, out_ref, send_sems, recv_sems, local_sem):
        x, y, c = lax.axis_index("x"), lax.axis_index("y"), lax.axis_index("c")
        me = 4 * x + 2 * y + c

        def copy(k):
            px, py, pc = x ^ ((k >> 2) & 1), y ^ ((k >> 1) & 1), c ^ (k & 1)
            peer = 4 * px + 2 * py + pc
            return pltpu.make_async_remote_copy(
                src_ref=in_ref.at[peer], dst_ref=out_ref.at[me],
                send_sem=send_sems.at[k - 1], recv_sem=recv_sems.at[k - 1],
                device_id=(px, py, pc), device_id_type=pl.DeviceIdType.MESH), peer

        mine = pltpu.make_async_copy(in_ref.at[me], out_ref.at[me], local_sem)
        mine.start()
        copies = [copy(k) for k in range(1, N_DEV)]
        for cp, _ in copies:
            cp.start()
        for k, (cp, peer) in enumerate(copies):
            pltpu.make_async_remote_copy(
                src_ref=in_ref.at[peer], dst_ref=out_ref.at[peer],
                send_sem=send_sems.at[k], recv_sem=recv_sems.at[k],
                device_id=(x, y, c), device_id_type=pl.DeviceIdType.MESH).wait_recv()
        for cp, _ in copies:
            cp.wait_send()
        mine.wait()

    return pl.pallas_call(
        body, name=name,
        out_shape=jax.ShapeDtypeStruct(pieces.shape, pieces.dtype),
        in_specs=[pl.BlockSpec(memory_space=pl.ANY)],
        out_specs=pl.BlockSpec(memory_space=pl.ANY),
        scratch_shapes=[pltpu.SemaphoreType.DMA((7,)), pltpu.SemaphoreType.DMA((7,)),
                        pltpu.SemaphoreType.DMA],
    )(pieces)


def _mm(a, b, *, name, nt=False, out_dtype=F32, add=None, split_n=None, tm=512, tn=1024):
    m, k = a.shape
    n = b.shape[0] if nt else b.shape[1]
    tm = _tile(m, tm)
    tn = n // split_n if split_n else _tile(n, tn)
    dims = (((1,), (1,)), ((), ())) if nt else (((1,), (0,)), ((), ()))

    def body(*refs):
        a_ref, b_ref = refs[0], refs[1]
        o_ref = refs[-1]
        acc = lax.dot_general(a_ref[...], b_ref[...], dims, preferred_element_type=F32)
        if add is not None:
            acc = acc + refs[2][...]
        o_ref[...] = acc.astype(o_ref.dtype)

    in_specs = [pl.BlockSpec((tm, k), lambda i, j: (i, 0)),
                pl.BlockSpec((tn, k), lambda i, j: (j, 0)) if nt
                else pl.BlockSpec((k, tn), lambda i, j: (0, j))]
    args = [a, b]
    if add is not None:
        in_specs.append(pl.BlockSpec((tm, tn), lambda i, j: (i, j)))
        args.append(add)
    if split_n:
        out_shape = jax.ShapeDtypeStruct((split_n, m, tn), out_dtype)
        out_spec = pl.BlockSpec((None, tm, tn), lambda i, j: (j, i, 0))
    else:
        out_shape = jax.ShapeDtypeStruct((m, n), out_dtype)
        out_spec = pl.BlockSpec((tm, tn), lambda i, j: (i, j))
    return pl.pallas_call(
        body, name=name, grid=(m // tm, n // tn), in_specs=in_specs, out_specs=out_spec,
        out_shape=out_shape, compiler_params=_params("parallel", "parallel"),
    )(*args)


def _mm_win_fwd(h, w_g, name):
    m, k = h.shape
    nj = w_g.shape[2]
    tm = _tile(m, 512)

    def body(a_ref, b_ref, o_ref):
        o_ref[...] = jnp.dot(a_ref[...], b_ref[...], preferred_element_type=F32)

    return pl.pallas_call(
        body, name=name, grid=(N_DEV, m // tm),
        in_specs=[pl.BlockSpec((tm, k), lambda j, i: (i, 0)),
                  pl.BlockSpec((None, k, nj), lambda j, i: (j, 0, 0))],
        out_specs=pl.BlockSpec((tm, nj), lambda j, i: (i, j)),
        out_shape=jax.ShapeDtypeStruct((m, N_DEV * nj), F32),
        compiler_params=_params("parallel", "parallel"),
    )(h, w_g)


def _mm_win_bwd(dproj, w_g, name):
    m = dproj.shape[0]
    d, nj = w_g.shape[1], w_g.shape[2]
    tm = _tile(m, 512)
    tn = _tile(d, 1024)

    def body(a_ref, b_ref, o_ref, acc_ref):
        j = pl.program_id(2)

        @pl.when(j == 0)
        def _():
            acc_ref[...] = jnp.zeros_like(acc_ref)

        acc_ref[...] += lax.dot_general(a_ref[...], b_ref[...], (((1,), (1,)), ((), ())),
                                        preferred_element_type=F32)

        @pl.when(j == N_DEV - 1)
        def _():
            o_ref[...] = acc_ref[...]

    return pl.pallas_call(
        body, name=name, grid=(m // tm, d // tn, N_DEV),
        in_specs=[pl.BlockSpec((tm, nj), lambda i, n, j: (i, j)),
                  pl.BlockSpec((None, tn, nj), lambda i, n, j: (j, n, 0))],
        out_specs=pl.BlockSpec((tm, tn), lambda i, n, j: (i, n)),
        out_shape=jax.ShapeDtypeStruct((m, d), F32),
        scratch_shapes=[pltpu.VMEM((tm, tn), F32)],
        compiler_params=_params("parallel", "parallel", "arbitrary"),
    )(dproj, w_g)


def _row_spec(tr, w, col):
    return pl.BlockSpec((tr, w), lambda i: (i, col))


def _full_spec(shape):
    return pl.BlockSpec(shape, lambda i: (0,) * len(shape))


def _rmsnorm_fwd(x, g, name):
    l, d = x.shape
    tr = _tile(l, 256)

    def body(x_ref, g_ref, o_ref):
        xv = x_ref[...]
        rstd = lax.rsqrt(jnp.mean(xv * xv, axis=-1, keepdims=True) + RMS_EPS)
        o_ref[...] = (xv * rstd * g_ref[...]).astype(o_ref.dtype)

    return pl.pallas_call(
        body, name=name, grid=(l // tr,),
        in_specs=[_row_spec(tr, d, 0), _full_spec((1, d))],
        out_specs=_row_spec(tr, d, 0),
        out_shape=jax.ShapeDtypeStruct((l, d), BF16),
        compiler_params=_params("parallel"),
    )(x, g.reshape(1, d))


def _rmsnorm_bwd(x, g, dh, dxo, name):
    l, d = x.shape
    tr = _tile(l, 256)

    def body(x_ref, g_ref, dh_ref, dxo_ref, dx_ref, dg_ref):
        xv = x_ref[...]
        rstd = lax.rsqrt(jnp.mean(xv * xv, axis=-1, keepdims=True) + RMS_EPS)
        dhv = dh_ref[...]
        gdy = dhv * g_ref[...]
        dot = jnp.mean(gdy * xv, axis=-1, keepdims=True)
        dx_ref[...] = dxo_ref[...] + rstd * gdy - xv * (rstd * rstd * rstd * dot)

        @pl.when(pl.program_id(0) == 0)
        def _():
            dg_ref[...] = jnp.zeros_like(dg_ref)

        dg_ref[...] += jnp.sum(dhv * xv * rstd, axis=0, keepdims=True)

    return pl.pallas_call(
        body, name=name, grid=(l // tr,),
        in_specs=[_row_spec(tr, d, 0), _full_spec((1, d)), _row_spec(tr, d, 0), _row_spec(tr, d, 0)],
        out_specs=[_row_spec(tr, d, 0), _full_spec((1, d))],
        out_shape=[jax.ShapeDtypeStruct((l, d), F32), jax.ShapeDtypeStruct((1, d), F32)],
        compiler_params=_params("arbitrary"),
    )(x, g.reshape(1, d), dh, dxo)


def _final_loss(x, g, tgt, name):
    l, d = x.shape
    tr = _tile(l, 256)

    def body(x_ref, g_ref, t_ref, dx_ref, dg_ref, loss_ref):
        xv = x_ref[...]
        gv = g_ref[...]
        rstd = lax.rsqrt(jnp.mean(xv * xv, axis=-1, keepdims=True) + RMS_EPS)
        xn = xv * rstd
        err = xn * gv - t_ref[...]
        dy = err * (1.0 / d)
        gdy = dy * gv
        dot = jnp.mean(gdy * xv, axis=-1, keepdims=True)
        dx_ref[...] = rstd * gdy - xv * (rstd * rstd * rstd * dot)

        @pl.when(pl.program_id(0) == 0)
        def _():
            dg_ref[...] = jnp.zeros_like(dg_ref)
            loss_ref[...] = jnp.zeros_like(loss_ref)

        dg_ref[...] += jnp.sum(dy * xn, axis=0, keepdims=True)
        loss_ref[...] += (0.5 / d) * jnp.sum(err * err)

    return pl.pallas_call(
        body, name=name, grid=(l // tr,),
        in_specs=[_row_spec(tr, d, 0), _full_spec((1, d)), _row_spec(tr, d, 0)],
        out_specs=[_row_spec(tr, d, 0), _full_spec((1, d)), _full_spec((SUBLANES, LANES))],
        out_shape=[jax.ShapeDtypeStruct((l, d), F32), jax.ShapeDtypeStruct((1, d), F32),
                   jax.ShapeDtypeStruct((SUBLANES, LANES), F32)],
        compiler_params=_params("arbitrary"),
    )(x, g.reshape(1, d), tgt)


def _halo_spec(tr, w, col, nblk8, before):
    step = tr // SUBLANES
    if before:
        return pl.BlockSpec((SUBLANES, w), lambda i: (jnp.maximum(i * step - 1, 0), col))
    return pl.BlockSpec((SUBLANES, w), lambda i: (jnp.minimum((i + 1) * step, nblk8 - 1), col))


def _shift_down(cur, before, k):
    ext = jnp.concatenate([before, cur], axis=0)
    return pltpu.roll(ext, k, axis=0)[SUBLANES:, :]


def _shift_up(cur, after, k):
    tr = cur.shape[0]
    ext = jnp.concatenate([cur, after], axis=0)
    return pltpu.roll(ext, tr + SUBLANES - k, axis=0)[:tr, :]


def _branch_a_fwd(proj, conv_w, d, name):
    l = proj.shape[0]
    tr = _tile(l, 256)
    nblk8 = l // SUBLANES

    def body(v_ref, bg_ref, cg_ref, za_ref, vh_ref, cgh_ref, w_ref, o_ref):
        first = pl.program_id(0) == 0
        cv = cg_ref[...] * v_ref[...]
        cvh = jnp.where(first, 0.0, cgh_ref[...] * vh_ref[...])
        w0, w1, w2 = w_ref[0:1, :], w_ref[1:2, :], w_ref[2:3, :]
        q = w2 * cv + w1 * _shift_down(cv, cvh, 1) + w0 * _shift_down(cv, cvh, 2)
        za = za_ref[...]
        o_ref[...] = (bg_ref[...] * q * (za * _sigmoid(za))).astype(o_ref.dtype)

    return pl.pallas_call(
        body, name=name, grid=(l // tr,),
        in_specs=[_row_spec(tr, d, 0), _row_spec(tr, d, 1), _row_spec(tr, d, 2), _row_spec(tr, d, 3),
                  _halo_spec(tr, d, 0, nblk8, True), _halo_spec(tr, d, 2, nblk8, True),
                  _full_spec((SUBLANES, d))],
        out_specs=_row_spec(tr, d, 0),
        out_shape=jax.ShapeDtypeStruct((l, d), BF16),
        compiler_params=_params("parallel"),
    )(proj, proj, proj, proj, proj, proj, conv_w)


def _branch_a_bwd(proj, dpa, conv_w, d, name):
    l = proj.shape[0]
    tr = _tile(l, 128)
    nblk8 = l // SUBLANES
    ntiles = l // tr

    def body(v_ref, bg_ref, cg_ref, za_ref, dpa_ref, vh_ref, cgh_ref, bgn_ref, zan_ref, dpan_ref,
             w_ref, dv_ref, dbg_ref, dcg_ref, dza_ref, dw0_ref, dw1_ref, dw2_ref):
        i = pl.program_id(0)
        v, bg, cg, za, dpa_v = v_ref[...], bg_ref[...], cg_ref[...], za_ref[...], dpa_ref[...]
        w0, w1, w2 = w_ref[0:1, :], w_ref[1:2, :], w_ref[2:3, :]
        cv = cg * v
        cvh = jnp.where(i == 0, 0.0, cgh_ref[...] * vh_ref[...])
        cv1 = _shift_down(cv, cvh, 1)
        cv2 = _shift_down(cv, cvh, 2)
        q = w2 * cv + w1 * cv1 + w0 * cv2
        sg = _sigmoid(za)
        s = za * sg
        dbg_ref[...] = (dpa_v * q * s).astype(dbg_ref.dtype)
        dza_ref[...] = (dpa_v * bg * q * (sg * (1.0 + za * (1.0 - sg)))).astype(dza_ref.dtype)
        dq = dpa_v * bg * s
        zan = zan_ref[...]
        dqn = jnp.where(i == ntiles - 1, 0.0, dpan_ref[...] * bgn_ref[...] * (zan * _sigmoid(zan)))
        dcv = w2 * dq + w1 * _shift_up(dq, dqn, 1) + w0 * _shift_up(dq, dqn, 2)
        dcg_ref[...] = (dcv * v).astype(dcg_ref.dtype)
        dv_ref[...] = (dcv * cg).astype(dv_ref.dtype)

        @pl.when(i == 0)
        def _():
            dw0_ref[...] = jnp.zeros_like(dw0_ref)
            dw1_ref[...] = jnp.zeros_like(dw1_ref)
            dw2_ref[...] = jnp.zeros_like(dw2_ref)

        dw0_ref[...] += jnp.sum(dq * cv2, axis=0, keepdims=True)
        dw1_ref[...] += jnp.sum(dq * cv1, axis=0, keepdims=True)
        dw2_ref[...] += jnp.sum(dq * cv, axis=0, keepdims=True)

    act = jax.ShapeDtypeStruct((l, d), BF16)
    wsum = jax.ShapeDtypeStruct((1, d), F32)
    return pl.pallas_call(
        body, name=name, grid=(ntiles,),
        in_specs=[_row_spec(tr, d, 0), _row_spec(tr, d, 1), _row_spec(tr, d, 2), _row_spec(tr, d, 3),
                  _row_spec(tr, d, 0),
                  _halo_spec(tr, d, 0, nblk8, True), _halo_spec(tr, d, 2, nblk8, True),
                  _halo_spec(tr, d, 1, nblk8, False), _halo_spec(tr, d, 3, nblk8, False),
                  _halo_spec(tr, d, 0, nblk8, False),
                  _full_spec((SUBLANES, d))],
        out_specs=[_row_spec(tr, d, 0)] * 4 + [_full_spec((1, d))] * 3,
        out_shape=[act] * 4 + [wsum] * 3,
        compiler_params=_params("arbitrary"),
    )(proj, proj, proj, proj, dpa, proj, proj, proj, proj, dpa, conv_w)


def _gelu_cast(y, name):
    l, w = y.shape
    tr = _tile(l, 512)

    def body(y_ref, o_ref):
        o_ref[...] = _gelu(y_ref[...]).astype(o_ref.dtype)

    return pl.pallas_call(
        body, name=name, grid=(l // tr,), in_specs=[_row_spec(tr, w, 0)],
        out_specs=_row_spec(tr, w, 0), out_shape=jax.ShapeDtypeStruct((l, w), BF16),
        compiler_params=_params("parallel"),
    )(y)


def _glu_post(y, gl, proj, b_glu, zb_col, name):
    l, w = y.shape
    tr = _tile(l, 512)

    def body(y_ref, gl_ref, zb_ref, b_ref, o_ref):
        zb = zb_ref[...]
        o_ref[...] = (_gelu(y_ref[...]) * _sigmoid(gl_ref[...] + b_ref[...])
                      * (zb * _sigmoid(zb))).astype(o_ref.dtype)

    return pl.pallas_call(
        body, name=name, grid=(l // tr,),
        in_specs=[_row_spec(tr, w, 0), _row_spec(tr, w, 0), _row_spec(tr, w, zb_col), _full_spec((1, w))],
        out_specs=_row_spec(tr, w, 0), out_shape=jax.ShapeDtypeStruct((l, w), BF16),
        compiler_params=_params("parallel"),
    )(y, gl, proj, b_glu.reshape(1, w))


def _glu_bwd1(y, gl, proj, b_glu, dpb, zb_col, name):
    l, w = y.shape
    tr = _tile(l, 512)

    def body(y_ref, gl_ref, zb_ref, b_ref, dpb_ref, dzb_ref, dgl_ref, t_ref, db_ref):
        zb = zb_ref[...]
        dpb_v = dpb_ref[...]
        yg = _gelu(y_ref[...])
        sgl = _sigmoid(gl_ref[...] + b_ref[...])
        szb = _sigmoid(zb)
        dzb_ref[...] = (dpb_v * yg * sgl * (szb * (1.0 + zb * (1.0 - szb)))).astype(dzb_ref.dtype)
        e = dpb_v * (zb * szb)
        dgl = e * yg * sgl * (1.0 - sgl)
        dgl_ref[...] = dgl.astype(dgl_ref.dtype)
        t_ref[...] = e * sgl

        @pl.when(pl.program_id(0) == 0)
        def _():
            db_ref[...] = jnp.zeros_like(db_ref)

        db_ref[...] += jnp.sum(dgl, axis=0, keepdims=True)

    return pl.pallas_call(
        body, name=name, grid=(l // tr,),
        in_specs=[_row_spec(tr, w, 0), _row_spec(tr, w, 0), _row_spec(tr, w, zb_col), _full_spec((1, w)),
                  _row_spec(tr, w, 0)],
        out_specs=[_row_spec(tr, w, 0)] * 3 + [_full_spec((1, w))],
        out_shape=[jax.ShapeDtypeStruct((l, w), BF16), jax.ShapeDtypeStruct((l, w), BF16),
                   jax.ShapeDtypeStruct((l, w), F32), jax.ShapeDtypeStruct((1, w), F32)],
        compiler_params=_params("arbitrary"),
    )(y, gl, proj, b_glu.reshape(1, w), dpb)


def _glu_bwd2(y, t1, dyg2, name):
    l, w = y.shape
    tr = _tile(l, 512)

    def body(y_ref, t_ref, d_ref, o_ref):
        o_ref[...] = (t_ref[...] + d_ref[...]) * _gelu_grad(y_ref[...])

    return pl.pallas_call(
        body, name=name, grid=(l // tr,), in_specs=[_row_spec(tr, w, 0)] * 3,
        out_specs=_row_spec(tr, w, 0), out_shape=jax.ShapeDtypeStruct((l, w), F32),
        compiler_params=_params("parallel"),
    )(y, t1, dyg2)


def _merge_fwd(proj, ya, yb, d, ga_col, gb_col, name):
    l = proj.shape[0]
    tr = _tile(l, 256)

    def body(ga_ref, gb_ref, ya_ref, yb_ref, o_ref):
        o_ref[...] = (_sigmoid(ga_ref[...]) * ya_ref[...]
                      + _sigmoid(gb_ref[...]) * yb_ref[...]).astype(o_ref.dtype)

    return pl.pallas_call(
        body, name=name, grid=(l // tr,),
        in_specs=[_row_spec(tr, d, ga_col), _row_spec(tr, d, gb_col), _row_spec(tr, d, 0), _row_spec(tr, d, 0)],
        out_specs=_row_spec(tr, d, 0), out_shape=jax.ShapeDtypeStruct((l, d), BF16),
        compiler_params=_params("parallel"),
    )(proj, proj, ya, yb)


def _merge_bwd(proj, ya, yb, dm, d, ga_col, gb_col, name):
    l = proj.shape[0]
    tr = _tile(l, 256)

    def body(ga_ref, gb_ref, ya_ref, yb_ref, dm_ref, dya_ref, dyb_ref, dga_ref, dgb_ref):
        dmv = dm_ref[...]
        sa = _sigmoid(ga_ref[...])
        sb = _sigmoid(gb_ref[...])
        dya_ref[...] = (dmv * sa).astype(dya_ref.dtype)
        dyb_ref[...] = (dmv * sb).astype(dyb_ref.dtype)
        dga_ref[...] = (dmv * ya_ref[...] * sa * (1.0 - sa)).astype(dga_ref.dtype)
        dgb_ref[...] = (dmv * yb_ref[...] * sb * (1.0 - sb)).astype(dgb_ref.dtype)

    act = jax.ShapeDtypeStruct((l, d), BF16)
    return pl.pallas_call(
        body, name=name, grid=(l // tr,),
        in_specs=[_row_spec(tr, d, ga_col), _row_spec(tr, d, gb_col), _row_spec(tr, d, 0), _row_spec(tr, d, 0),
                  _row_spec(tr, d, 0)],
        out_specs=[_row_spec(tr, d, 0)] * 4, out_shape=[act] * 4,
        compiler_params=_params("parallel"),
    )(proj, proj, ya, yb, dm)


def _to_segments(a):
    l, w = a.shape
    return a.reshape(SUBLANES, l // SUBLANES, w).transpose(1, 0, 2).reshape(l, w)


def _from_segments(a):
    l, w = a.shape
    return a.reshape(l // SUBLANES, SUBLANES, w).transpose(1, 0, 2).reshape(l, w)


def _dense(z, shape):
    return jnp.broadcast_to(z, shape).reshape(-1, LANES)


def _s5_disc(are, aim, ldt):
    dt = jnp.exp(ldt)
    er = jnp.exp(are * dt)
    lbr = er * jnp.cos(aim * dt)
    lbi = er * jnp.sin(aim * dt)
    inv = 1.0 / (are * are + aim * aim)
    fr = ((lbr - 1.0) * are + lbi * aim) * inv
    fi = (lbi * are - (lbr - 1.0) * aim) * inv
    return dt, lbr, lbi, inv, fr, fi


def _s5_params(are, aim, ldt, bre, bim, name):
    shape = are.shape

    def body(are_ref, aim_ref, ldt_ref, bre_ref, bim_ref, lbr_ref, lbi_ref, bbr_ref, bbi_ref):
        _, lbr, lbi, _, fr, fi = _s5_disc(are_ref[...], aim_ref[...], ldt_ref[...])
        lbr_ref[...] = lbr
        lbi_ref[...] = lbi
        bbr_ref[...] = fr * bre_ref[...] - fi * bim_ref[...]
        bbi_ref[...] = fr * bim_ref[...] + fi * bre_ref[...]

    out = jax.ShapeDtypeStruct(shape, F32)
    return pl.pallas_call(body, name=name, out_shape=[out] * 4,
                          compiler_params=pltpu.CompilerParams(vmem_limit_bytes=VMEM_LIMIT_BYTES),
                          )(are, aim, ldt, bre, bim)


def _s5_params_bwd(are, aim, ldt, bre, bim, glbr, glbi, gbbr, gbbi, n_groups, name):
    shape = are.shape
    rows_per_group = shape[0] // n_groups

    def body(are_ref, aim_ref, ldt_ref, bre_ref, bim_ref, glbr_ref, glbi_ref, gbbr_ref, gbbi_ref,
             gar_ref, gai_ref, gdt_ref, gbr_ref, gbi_ref):
        are_v, aim_v = are_ref[...], aim_ref[...]
        bre_v, bim_v = bre_ref[...], bim_ref[...]
        gbbr_v, gbbi_v = gbbr_ref[...], gbbi_ref[...]
        dt, lbr, lbi, inv, fr, fi = _s5_disc(are_v, aim_v, ldt_ref[...])
        gbr_ref[...] = fr * gbbr_v + fi * gbbi_v
        gbi_ref[...] = fr * gbbi_v - fi * gbbr_v
        lane_group = lax.broadcasted_iota(jnp.int32, (LANES, LANES), 0) // S5_GROUP
        same_group = (lane_group == lax.broadcasted_iota(jnp.int32, (LANES, LANES), 1) // S5_GROUP)
        ones = same_group.astype(F32)
        gfr = jnp.dot(bre_v * gbbr_v + bim_v * gbbi_v, ones, precision=HIGHEST, preferred_element_type=F32)
        gfi = jnp.dot(bre_v * gbbi_v - bim_v * gbbr_v, ones, precision=HIGHEST, preferred_element_type=F32)
        glr = glbr_ref[...] + (are_v * gfr - aim_v * gfi) * inv
        gli = glbi_ref[...] + (are_v * gfi + aim_v * gfr) * inv
        qr = (fr * are_v + fi * aim_v) * inv
        qi = (fi * are_v - fr * aim_v) * inv
        gzr = lbr * glr + lbi * gli
        gzi = lbr * gli - lbi * glr
        gar_ref[...] = dt * gzr - (qr * gfr + qi * gfi)
        gai_ref[...] = dt * gzi - (qr * gfi - qi * gfr)
        e = dt * (are_v * gzr + aim_v * gzi)
        per_group = jnp.sum(e.reshape(n_groups, rows_per_group, LANES), axis=1)
        total = jnp.sum(per_group, axis=1, keepdims=True) * (1.0 / S5_GROUP)
        gdt_ref[...] = jnp.broadcast_to(total, gdt_ref.shape)

    out = jax.ShapeDtypeStruct(shape, F32)
    return pl.pallas_call(
        body, name=name,
        out_shape=[out, out, jax.ShapeDtypeStruct((n_groups, LANES), F32), out, out],
        compiler_params=pltpu.CompilerParams(vmem_limit_bytes=VMEM_LIMIT_BYTES),
    )(are, aim, ldt, bre, bim, glbr, glbi, gbbr, gbbi)


def _s5_up(xs, m_re, m_im, name):
    l = xs.shape[0]
    nb, kin, kout = m_re.shape
    tl = _tile(l, 512)

    def body(x_ref, mr_ref, mi_ref, or_ref, oi_ref):
        xv = x_ref[...]
        or_ref[...] = jnp.dot(xv, mr_ref[...], precision=HIGHEST, preferred_element_type=F32)
        oi_ref[...] = jnp.dot(xv, mi_ref[...], precision=HIGHEST, preferred_element_type=F32)

    out = jax.ShapeDtypeStruct((l, nb * kout), F32)
    return pl.pallas_call(
        body, name=name, grid=(l // tl, nb),
        in_specs=[pl.BlockSpec((tl, kin), lambda i, b: (i, b)),
                  pl.BlockSpec((None, kin, kout), lambda i, b: (b, 0, 0)),
                  pl.BlockSpec((None, kin, kout), lambda i, b: (b, 0, 0))],
        out_specs=[pl.BlockSpec((tl, kout), lambda i, b: (i, b))] * 2,
        out_shape=[out, out], compiler_params=_params("parallel", "parallel"),
    )(xs, m_re, m_im)


def _s5_down(a_re, a_im, m1, m2, xs, dvec, name):
    l = xs.shape[0]
    nb, kin, kout = m1.shape
    tl = _tile(l, 512)

    def body(ar_ref, ai_ref, m1_ref, m2_ref, x_ref, d_ref, o_ref):
        o_ref[...] = (jnp.dot(ar_ref[...], m1_ref[...], precision=HIGHEST, preferred_element_type=F32)
                      - jnp.dot(ai_ref[...], m2_ref[...], precision=HIGHEST, preferred_element_type=F32)
                      + d_ref[...] * x_ref[...])

    return pl.pallas_call(
        body, name=name, grid=(l // tl, nb),
        in_specs=[pl.BlockSpec((tl, kin), lambda i, b: (i, b)), pl.BlockSpec((tl, kin), lambda i, b: (i, b)),
                  pl.BlockSpec((None, kin, kout), lambda i, b: (b, 0, 0)),
                  pl.BlockSpec((None, kin, kout), lambda i, b: (b, 0, 0)),
                  pl.BlockSpec((tl, kout), lambda i, b: (i, b)),
                  pl.BlockSpec((1, kout), lambda i, b: (0, b))],
        out_specs=pl.BlockSpec((tl, kout), lambda i, b: (i, b)),
        out_shape=jax.ShapeDtypeStruct((l, nb * kout), F32),
        compiler_params=_params("parallel", "parallel"),
    )(a_re, a_im, m1, m2, xs, dvec)


def _s5_outer(xt, h_re, h_im, name):
    l = xt.shape[1]
    nb = xt.shape[0] // LANES
    kout = h_re.shape[1] // nb
    tl = _tile(l, 512)

    def body(x_ref, hr_ref, hi_ref, or_ref, oi_ref):
        @pl.when(pl.program_id(1) == 0)
        def _():
            or_ref[...] = jnp.zeros_like(or_ref)
            oi_ref[...] = jnp.zeros_like(oi_ref)

        xv = x_ref[...]
        or_ref[...] += jnp.dot(xv, hr_ref[...], precision=HIGHEST, preferred_element_type=F32)
        oi_ref[...] += jnp.dot(xv, hi_ref[...], precision=HIGHEST, preferred_element_type=F32)

    out = jax.ShapeDtypeStruct((nb, LANES, kout), F32)
    return pl.pallas_call(
        body, name=name, grid=(nb, l // tl),
        in_specs=[pl.BlockSpec((LANES, tl), lambda b, t: (b, t)),
                  pl.BlockSpec((tl, kout), lambda b, t: (t, b)),
                  pl.BlockSpec((tl, kout), lambda b, t: (t, b))],
        out_specs=[pl.BlockSpec((None, LANES, kout), lambda b, t: (b, 0, 0))] * 2,
        out_shape=[out, out], compiler_params=_params("parallel", "arbitrary"),
    )(xt, h_re, h_im)


def _cmul(ar, ai, br, bi):
    return ar * br - ai * bi, ar * bi + ai * br


def _s5_scan(x_re, x_im, lam_re, lam_im, reverse, name):
    l, n = x_re.shape
    wb = _tile(n, 256)
    nt = l // SUBLANES
    shift = SUBLANES - 1 if reverse else 1
    unroll = 8 if nt % 8 == 0 else 1

    def rows(k):
        t = (nt - 1 - k) if reverse else k
        return pl.ds(pl.multiple_of(t * SUBLANES, SUBLANES), SUBLANES)

    def body(xr_ref, xi_ref, lr_ref, li_ref, hr_ref, hi_ref):
        lr = jnp.broadcast_to(lr_ref[...], (SUBLANES, wb))
        li = jnp.broadcast_to(li_ref[...], (SUBLANES, wb))
        zero = jnp.zeros((SUBLANES, wb), F32)
        one = jnp.ones((SUBLANES, wb), F32)

        def local_step(k, carry):
            hr, hi, pr, pi = carry
            r = rows(k)
            tr_, ti_ = _cmul(lr, li, hr, hi)
            hr, hi = tr_ + xr_ref[r, :], ti_ + xi_ref[r, :]
            hr_ref[r, :] = hr
            hi_ref[r, :] = hi
            pr, pi = _cmul(lr, li, pr, pi)
            return hr, hi, pr, pi

        er, ei, lnr, lni = lax.fori_loop(0, nt, local_step, (zero, zero, one, zero), unroll=unroll)

        row = lax.broadcasted_iota(jnp.int32, (SUBLANES, wb), 0)
        tr_, ti_ = er, ei
        for j in range(1, SUBLANES):
            pr_, pi_ = _cmul(lnr, lni, pltpu.roll(tr_, shift, axis=0), pltpu.roll(ti_, shift, axis=0))
            at = row == ((SUBLANES - 1 - j) if reverse else j)
            tr_ = jnp.where(at, er + pr_, tr_)
            ti_ = jnp.where(at, ei + pi_, ti_)
        edge = row == ((SUBLANES - 1) if reverse else 0)
        cr = jnp.where(edge, 0.0, pltpu.roll(tr_, shift, axis=0))
        ci = jnp.where(edge, 0.0, pltpu.roll(ti_, shift, axis=0))

        def fix_step(k, carry):
            pr, pi = carry
            pr, pi = _cmul(lr, li, pr, pi)
            r = rows(k)
            ar_, ai_ = _cmul(pr, pi, cr, ci)
            hr_ref[r, :] = hr_ref[r, :] + ar_
            hi_ref[r, :] = hi_ref[r, :] + ai_
            return pr, pi

        lax.fori_loop(0, nt, fix_step, (one, zero), unroll=unroll)

    out = jax.ShapeDtypeStruct((l, n), F32)
    col = lambda b: (0, b)
    return pl.pallas_call(
        body, name=name, grid=(n // wb,),
        in_specs=[pl.BlockSpec((l, wb), col), pl.BlockSpec((l, wb), col),
                  pl.BlockSpec((1, wb), col), pl.BlockSpec((1, wb), col)],
        out_specs=[pl.BlockSpec((l, wb), col)] * 2, out_shape=[out, out],
        compiler_params=_params("parallel"),
    )(x_re, x_im, lam_re, lam_im)


def _s5_dlam(g_re, g_im, h_re, h_im, name):
    l, n = g_re.shape
    wb = _tile(n, 256)
    nt = l // SUBLANES

    def body(gr_ref, gi_ref, hr_ref, hi_ref, or_ref, oi_ref):
        row = lax.broadcasted_iota(jnp.int32, (SUBLANES, wb), 0)
        last = pl.ds((nt - 1) * SUBLANES, SUBLANES)
        first = pl.ds(0, SUBLANES)
        pr = jnp.where(row == 0, 0.0, pltpu.roll(hr_ref[last, :], 1, axis=0))
        pi = jnp.where(row == 0, 0.0, pltpu.roll(hi_ref[last, :], 1, axis=0))
        gr, gi = gr_ref[first, :], gi_ref[first, :]
        acc_r = gr * pr + gi * pi
        acc_i = gi * pr - gr * pi

        def step(t, carry):
            acc_r, acc_i = carry
            cur = pl.ds(pl.multiple_of(t * SUBLANES, SUBLANES), SUBLANES)
            prev = pl.ds(pl.multiple_of((t - 1) * SUBLANES, SUBLANES), SUBLANES)
            gr, gi = gr_ref[cur, :], gi_ref[cur, :]
            pr, pi = hr_ref[prev, :], hi_ref[prev, :]
            return acc_r + gr * pr + gi * pi, acc_i + gi * pr - gr * pi

        acc_r, acc_i = lax.fori_loop(1, nt, step, (acc_r, acc_i))
        or_ref[...] = jnp.sum(acc_r, axis=0, keepdims=True)
        oi_ref[...] = jnp.sum(acc_i, axis=0, keepdims=True)

    col = lambda b: (0, b)
    out = jax.ShapeDtypeStruct((1, n), F32)
    return pl.pallas_call(
        body, name=name, grid=(n // wb,),
        in_specs=[pl.BlockSpec((l, wb), col)] * 4,
        out_specs=[pl.BlockSpec((1, wb), col)] * 2, out_shape=[out, out],
        compiler_params=_params("parallel"),
    )(g_re, g_im, h_re, h_im)


def _colsum_prod(a, b, name):
    l, w = a.shape
    tr = _tile(l, 512)

    def body(a_ref, b_ref, o_ref):
        @pl.when(pl.program_id(0) == 0)
        def _():
            o_ref[...] = jnp.zeros_like(o_ref)

        o_ref[...] += jnp.sum(a_ref[...] * b_ref[...], axis=0, keepdims=True)

    return pl.pallas_call(
        body, name=name, grid=(l // tr,), in_specs=[_row_spec(tr, w, 0)] * 2,
        out_specs=_full_spec((1, w)), out_shape=jax.ShapeDtypeStruct((1, w), F32),
        compiler_params=_params("arbitrary"),
    )(a, b)


def _block_diag(m, nb):
    g, r, s = m.shape
    gb = g // nb
    eye = jnp.eye(gb, dtype=m.dtype)
    out = m.reshape(nb, gb, r, 1, s) * eye[None, :, None, :, None]
    return out.reshape(nb, gb * r, gb * s)


def _block_diag_extract(mat, g, r, s):
    nb = mat.shape[0]
    gb = g // nb
    eye = jnp.eye(gb, dtype=mat.dtype)
    m5 = mat.reshape(nb, gb, r, gb, s) * eye[None, :, None, :, None]
    return jnp.sum(m5, axis=3).reshape(g, r, s)


def _adamw(w, m, v, gslots, name):
    r, c = w.shape
    s = gslots.shape[0]
    tr = _tile(r, max(SUBLANES, 1 << int(math.log2(ADAMW_BLOCK_ELEMS // c))))
    bc1 = 1.0 / (1.0 - ADAM_B1 ** ADAM_STEP)
    bc2 = 1.0 / (1.0 - ADAM_B2 ** ADAM_STEP)

    def body(w_ref, m_ref, v_ref, g_ref, go_ref, d_ref, mo_ref, vo_ref):
        g = g_ref[0]
        for k in range(1, s):
            g = g + g_ref[k]
        mn = ADAM_B1 * m_ref[...] + (1.0 - ADAM_B1) * g
        vn = ADAM_B2 * v_ref[...] + (1.0 - ADAM_B2) * (g * g)
        go_ref[...] = g
        mo_ref[...] = mn
        vo_ref[...] = vn
        d_ref[...] = -ADAM_LR * ((mn * bc1) / (jnp.sqrt(vn * bc2) + ADAM_EPS) + ADAM_WD * w_ref[...])

    spec = pl.BlockSpec((tr, c), lambda i: (i, 0))
    out = jax.ShapeDtypeStruct((r, c), F32)
    return pl.pallas_call(
        body, name=name, grid=(r // tr,),
        in_specs=[spec, spec, spec, pl.BlockSpec((s, tr, c), lambda i: (0, i, 0))],
        out_specs=[spec] * 4, out_shape=[out] * 4, compiler_params=_params("parallel"),
    )(w, m, v, gslots)


def _pack(parts):
    flat = jnp.concatenate([p.reshape(-1) for p in parts])
    pad = (-flat.shape[0]) % (PACK_ROWS * LANES)
    return jnp.pad(flat, (0, pad)).reshape(-1, LANES)


def _unpack(packed, shapes):
    flat = packed.reshape(-1)
    out, off = [], 0
    for shp in shapes:
        size = math.prod(shp)
        out.append(flat[off:off + size].reshape(shp))
        off += size
    return out


def kernel(x, norm_g, w_in, conv_w, w_out_a, a_re, a_im, log_dt, b_re, b_im, c_re, c_im, d_skip, w_glu, b_glu, w_out_b, w_o, final_g, loss_target, m_norm_g, m_w_in, m_conv_w, m_w_out_a, m_a_re, m_a_im, m_log_dt, m_b_re, m_b_im, m_c_re, m_c_im, m_d_skip, m_w_glu, m_b_glu, m_w_out_b, m_w_o, m_final_g, v_norm_g, v_w_in, v_conv_w, v_w_out_a, v_a_re, v_a_im, v_log_dt, v_b_re, v_b_im, v_c_re, v_c_im, v_d_skip, v_w_glu, v_b_glu, v_w_out_b, v_w_o, v_final_g):
    depth = norm_g.shape[0]
    l, d = x.shape[1], x.shape[2]
    ws = w_glu.shape[2]
    n_groups, n_state = a_re.shape[1], a_re.shape[2]
    nb = ws // LANES
    assert S5_GROUP == b_re.shape[3] and n_state * S5_GB == 4 * LANES
    u_col, zb_col = 4 * d // ws, 4 * d // ws + 1
    ga_col, gb_col = (4 * d + 2 * ws) // d, (4 * d + 2 * ws) // d + 1
    me = 4 * lax.axis_index("x") + 2 * lax.axis_index("y") + lax.axis_index("c")

    xs = [x[0]]
    tgt = loss_target[0]

    conv_full = _all_gather(jnp.pad(conv_w.reshape(depth * 3, -1), ((0, SUBLANES - depth * 3), (0, 0))),
                            "ag_conv_w")
    conv_full = conv_full.transpose(1, 0, 2).reshape(SUBLANES, d)[:depth * 3].reshape(depth, 3, d)
    wg = []
    for i in range(depth):
        wg.append(dict(
            w_in=_all_gather(w_in[i].astype(BF16), f"ag_w_in_{i}"),
            w_a=_all_gather(w_out_a[i].astype(BF16), f"ag_w_a_{i}").reshape(d, d),
            w_glu=_all_gather(w_glu[i].astype(BF16), f"ag_w_glu_{i}").reshape(ws, ws),
            w_b=_all_gather(w_out_b[i].astype(BF16), f"ag_w_b_{i}"),
            w_o=_all_gather(w_o[i].astype(BF16), f"ag_w_o_{i}").reshape(d, d),
        ))

    saved = []
    for i in range(depth):
        xi = xs[-1]
        g = wg[i]
        conv8 = jnp.pad(conv_full[i], ((0, SUBLANES - 3), (0, 0)))
        h = _rmsnorm_fwd(xi, norm_g[i], f"rmsnorm_fwd_{i}")
        proj = _mm_win_fwd(h, g["w_in"], f"mm_proj_{i}")
        pa = _branch_a_fwd(proj, conv8, d, f"branch_a_fwd_{i}")
        ya = _mm(pa, g["w_a"], name=f"mm_ya_{i}")
        shape3 = (n_groups, n_state, S5_GROUP)
        dense_in = (_dense(a_re[i][:, :, None], shape3), _dense(a_im[i][:, :, None], shape3),
                    _dense(log_dt[i][:, None, None], shape3), b_re[i].reshape(-1, LANES), b_im[i].reshape(-1, LANES))
        lbr, lbi, bbr, bbi = _s5_params(*dense_in, f"s5_params_{i}")
        lam_re = lbr.reshape(shape3)[:, :, 0].reshape(1, -1)
        lam_im = lbi.reshape(shape3)[:, :, 0].reshape(1, -1)
        bbr3, bbi3 = bbr.reshape(shape3), bbi.reshape(shape3)
        u_seg = _to_segments(proj[:, 4 * d:4 * d + ws])
        bu_re, bu_im = _s5_up(u_seg, _block_diag(bbr3.transpose(0, 2, 1), nb),
                              _block_diag(bbi3.transpose(0, 2, 1), nb), f"s5_bu_{i}")
        h_re, h_im = _s5_scan(bu_re, bu_im, lam_re, lam_im, False, f"s5_scan_fwd_{i}")
        y_seg = _s5_down(h_re, h_im, _block_diag(c_re[i].transpose(0, 2, 1), nb),
                         _block_diag(c_im[i].transpose(0, 2, 1), nb), u_seg,
                         d_skip[i].reshape(1, ws), f"s5_y_{i}")
        y = _from_segments(y_seg)
        yg = _gelu_cast(y, f"gelu_{i}")
        gl = _mm(yg, g["w_glu"], name=f"mm_glu_{i}")
        pb = _glu_post(y, gl, proj, b_glu[i], zb_col, f"glu_post_{i}")
        w_b2d = g["w_b"].transpose(1, 0, 2).reshape(ws, d)
        yb = _mm(pb, w_b2d, name=f"mm_yb_{i}")
        mrg = _merge_fwd(proj, ya, yb, d, ga_col, gb_col, f"merge_fwd_{i}")
        xs.append(_mm(mrg, g["w_o"], name=f"mm_out_{i}", add=xi))
        saved.append(dict(h=h, proj=proj, pa=pa, ya=ya, yb=yb, y=y, yg=yg, gl=gl, pb=pb, mrg=mrg,
                          u_seg=u_seg, h_re=h_re, h_im=h_im, lam_re=lam_re, lam_im=lam_im,
                          bbr3=bbr3, bbi3=bbi3, dense_in=dense_in, conv8=conv8, w_b2d=w_b2d))

    dx, g_final, loss_part = _final_loss(xs[-1], final_g, tgt, "final_loss")
    loss = lax.psum(loss_part[0, 0], MESH_AXES)

    big_grads = [None] * depth
    small = {k: [None] * depth for k in ("norm_g", "a_re", "a_im", "log_dt", "b_re", "b_im", "c_re", "c_im",
                                         "d_skip", "b_glu", "conv_w")}
    for i in reversed(range(depth)):
        s, g = saved[i], wg[i]
        proj = s["proj"]
        dxo_b = dx.astype(BF16)
        dm = _mm(dxo_b, g["w_o"], name=f"mm_dm_{i}", nt=True)
        gw_o = _mm(s["mrg"].T, dxo_b, name=f"mm_gw_o_{i}")
        dya, dyb, dga, dgb = _merge_bwd(proj, s["ya"], s["yb"], dm, d, ga_col, gb_col, f"merge_bwd_{i}")
        dpa = _mm(dya, g["w_a"], name=f"mm_dpa_{i}", nt=True)
        gw_a = _mm(s["pa"].T, dya, name=f"mm_gw_a_{i}")
        dpb = _mm(dyb, s["w_b2d"], name=f"mm_dpb_{i}", nt=True)
        gw_b = _mm(s["pb"].T, dyb, name=f"mm_gw_b_{i}", split_n=N_DEV)
        dv, dbg, dcg, dza, dw0, dw1, dw2 = _branch_a_bwd(proj, dpa, s["conv8"], d, f"branch_a_bwd_{i}")
        small["conv_w"][i] = jnp.concatenate([dw0, dw1, dw2], axis=0)
        dzb, dgl, t1, db_glu = _glu_bwd1(s["y"], s["gl"], proj, b_glu[i], dpb, zb_col, f"glu_bwd1_{i}")
        small["b_glu"][i] = db_glu.reshape(ws)
        dyg2 = _mm(dgl, g["w_glu"], name=f"mm_dyg_{i}", nt=True)
        gw_glu = _mm(s["yg"].T, dgl, name=f"mm_gw_glu_{i}")
        dy = _glu_bwd2(s["y"], t1, dyg2, f"glu_bwd2_{i}")
        dy_seg = _to_segments(dy)
        u_seg = s["u_seg"]
        small["d_skip"][i] = _colsum_prod(dy_seg, u_seg, f"s5_dskip_{i}").reshape(n_groups, S5_GROUP)
        gh_re, gh_im = _s5_up(dy_seg, _block_diag(c_re[i], nb), _block_diag(-c_im[i], nb), f"s5_gh_{i}")
        q_re, q_im = _s5_scan(gh_re, gh_im, s["lam_re"], -s["lam_im"], True, f"s5_scan_bwd_{i}")
        du_seg = _s5_down(q_re, q_im, _block_diag(s["bbr3"], nb), _block_diag(-s["bbi3"], nb),
                          dy_seg, d_skip[i].reshape(1, ws), f"s5_du_{i}")
        gc_re, gc_im = _s5_outer(dy_seg.T, s["h_re"], s["h_im"], f"s5_gc_{i}")
        small["c_re"][i] = _block_diag_extract(gc_re, n_groups, S5_GROUP, n_state)
        small["c_im"][i] = -_block_diag_extract(gc_im, n_groups, S5_GROUP, n_state)
        gbb_re, gbb_im = _s5_outer(u_seg.T, q_re, q_im, f"s5_gbb_{i}")
        gbb_re = _block_diag_extract(gbb_re, n_groups, S5_GROUP, n_state).transpose(0, 2, 1)
        gbb_im = _block_diag_extract(gbb_im, n_groups, S5_GROUP, n_state).transpose(0, 2, 1)
        glam_re, glam_im = _s5_dlam(q_re, q_im, s["h_re"], s["h_im"], f"s5_dlam_{i}")
        shape3 = (n_groups, n_state, S5_GROUP)
        gar, gai, gdt, gbr, gbi = _s5_params_bwd(
            *s["dense_in"], _dense(glam_re.reshape(n_groups, n_state, 1), shape3),
            _dense(glam_im.reshape(n_groups, n_state, 1), shape3),
            gbb_re.reshape(-1, LANES), gbb_im.reshape(-1, LANES), n_groups, f"s5_params_bwd_{i}")
        small["a_re"][i] = gar.reshape(shape3)[:, :, 0]
        small["a_im"][i] = gai.reshape(shape3)[:, :, 0]
        small["log_dt"][i] = gdt[:, 0]
        small["b_re"][i] = gbr.reshape(shape3)
        small["b_im"][i] = gbi.reshape(shape3)
        du = _from_segments(du_seg).astype(BF16)
        dproj = jnp.concatenate([dv, dbg, dcg, dza, du, dzb, dga, dgb], axis=1)
        dh = _mm_win_bwd(dproj, g["w_in"], f"mm_dh_{i}")
        gw_in = _mm(s["h"].T, dproj, name=f"mm_gw_in_{i}", split_n=N_DEV, tm=1024)
        dx, dng = _rmsnorm_bwd(xs[i], norm_g[i], dh, dx, f"rmsnorm_bwd_{i}")
        small["norm_g"][i] = dng.reshape(d)
        big_grads[i] = dict(w_in=gw_in, w_out_a=gw_a.reshape(N_DEV, d // N_DEV, d),
                            w_glu=gw_glu.reshape(N_DEV, ws // N_DEV, ws), w_out_b=gw_b,
                            w_o=gw_o.reshape(N_DEV, d // N_DEV, d))

    big = dict(w_in=(w_in, m_w_in, v_w_in), w_out_a=(w_out_a, m_w_out_a, v_w_out_a),
               w_glu=(w_glu, m_w_glu, v_w_glu), w_out_b=(w_out_b, m_w_out_b, v_w_out_b),
               w_o=(w_o, m_w_o, v_w_o))
    results = {}
    for name, (w_, m_, v_) in big.items():
        per_layer = []
        for i in range(depth):
            slots = _all_to_all(big_grads[i][name], f"a2a_{name}_{i}")
            r_, c_ = w_.shape[1], w_.shape[2]
            per_layer.append(_adamw(w_[i], m_[i], v_[i], slots, f"adamw_{name}_{i}"))
        results[name] = [jnp.stack([per_layer[i][k] for i in range(depth)]) for k in range(4)]

    small_names = ("norm_g", "a_re", "a_im", "log_dt", "b_re", "b_im", "c_re", "c_im", "d_skip", "b_glu")
    small_w = dict(norm_g=(norm_g, m_norm_g, v_norm_g), a_re=(a_re, m_a_re, v_a_re), a_im=(a_im, m_a_im, v_a_im),
                   log_dt=(log_dt, m_log_dt, v_log_dt), b_re=(b_re, m_b_re, v_b_re), b_im=(b_im, m_b_im, v_b_im),
                   c_re=(c_re, m_c_re, v_c_re), c_im=(c_im, m_c_im, v_c_im), d_skip=(d_skip, m_d_skip, v_d_skip),
                   b_glu=(b_glu, m_b_glu, v_b_glu), final_g=(final_g, m_final_g, v_final_g))
    order = small_names + ("final_g", "conv_w")
    part = {k: jnp.stack(small[k]) for k in small_names + ("conv_w",)}
    part["final_g"] = g_final.reshape(d)
    shapes = [part[k].shape for k in order]
    gpack = _all_gather(_pack([part[k] for k in order]), "ag_small_grads")
    zeros_conv = jnp.zeros(part["conv_w"].shape, F32)
    wpack = _pack([small_w[k][0] for k in order[:-1]] + [zeros_conv])
    mpack = _pack([small_w[k][1] for k in order[:-1]] + [zeros_conv])
    vpack = _pack([small_w[k][2] for k in order[:-1]] + [zeros_conv])
    sres = [_unpack(p, shapes) for p in _adamw(wpack, mpack, vpack, gpack, "adamw_small")]
    for j, k in enumerate(order[:-1]):
        results[k] = [sres[q][j] for q in range(4)]
    dc = d // N_DEV
    gconv = lax.dynamic_slice_in_dim(sres[0][-1], me * dc, dc, axis=2)
    pad8 = lambda a: jnp.pad(a.reshape(depth * 3, dc), ((0, SUBLANES - depth * 3), (0, 0)))
    cres = _adamw(pad8(conv_w), pad8(m_conv_w), pad8(v_conv_w), pad8(gconv)[None], "adamw_conv_w")
    results["conv_w"] = [r_[:depth * 3].reshape(depth, 3, dc) for r_ in cres]

    names = ("norm_g", "w_in", "conv_w", "w_out_a", "a_re", "a_im", "log_dt", "b_re", "b_im", "c_re", "c_im",
             "d_skip", "w_glu", "b_glu", "w_out_b", "w_o", "final_g")
    outs = [loss, dx[None]]
    for q in range(4):
        outs += [results[k][q] for k in names]
    return tuple(outs)
```

```python
import functools
import math

import jax
import jax.numpy as jnp
from jax import lax
from jax.experimental import pallas as pl
from jax.experimental.pallas import tpu as pltpu

F32 = jnp.float32
BF16 = jnp.bfloat16
HIGHEST = lax.Precision.HIGHEST

N_DEV = 8
MESH_AXES = ("x", "y", "c")
LANES = 128
SUBLANES = 8
VMEM_LIMIT_BYTES = 56 * 1024 * 1024

RMS_EPS = 1e-6
ADAM_LR = 0.001
ADAM_B1 = 0.9
ADAM_B2 = 0.999
ADAM_EPS = 1e-08
ADAM_WD = 0.01
ADAM_STEP = 10
GELU_C0 = math.sqrt(2.0 / math.pi)
GELU_C1 = 0.044715

ADAMW_BLOCK_ELEMS = 1 << 17
PACK_ROWS = 512

S5_GROUP = 16
S5_GB = LANES // S5_GROUP


def _params(*semantics):
    return pltpu.CompilerParams(dimension_semantics=semantics, vmem_limit_bytes=VMEM_LIMIT_BYTES)


def _tile(n, pref):
    t = min(n, pref)
    while n % t:
        assert t % 2 == 0, (n, pref)
        t //= 2
    return t


def _sigmoid(z):
    return 1.0 / (1.0 + jnp.exp(-z))


def _gelu(y):
    return 0.5 * y * (1.0 + jnp.tanh(GELU_C0 * (y + GELU_C1 * y * y * y)))


def _gelu_grad(y):
    t = jnp.tanh(GELU_C0 * (y + GELU_C1 * y * y * y))
    return 0.5 * (1.0 + t) + 0.5 * y * (1.0 - t * t) * GELU_C0 * (1.0 + 3.0 * GELU_C1 * y * y)


def _all_gather(shard, name):
    r, c_ = shard.shape

    def body(x_ref, out_ref, send_sems, recv_sems, local_sem):
        x, y, c = lax.axis_index("x"), lax.axis_index("y"), lax.axis_index("c")
        me, sibling = (x, y, c), (x, y, 1 - c)
        chips = [(1 - x, y), (x, 1 - y), (1 - x, 1 - y)]

        def slot(px, py, pc):
            return out_ref.at[4 * px + 2 * py + pc]

        def copy(k, block, to, src=None):
            return pltpu.make_async_remote_copy(
                src_ref=slot(*block) if src is None else src, dst_ref=slot(*block),
                send_sem=send_sems.at[k], recv_sem=recv_sems.at[k],
                device_id=to, device_id_type=pl.DeviceIdType.MESH)

        mine = pltpu.make_async_copy(x_ref, slot(*me), local_sem)
        mine.start()
        first = [copy(0, me, sibling, src=x_ref)]
        first += [copy(1 + j, me, (*chip, c), src=x_ref) for j, chip in enumerate(chips)]
        for cp in first:
            cp.start()
        passed = [copy(4 + j, (*chip, c), sibling) for j, chip in enumerate(chips)]
        for j, chip in enumerate(chips):
            copy(1 + j, (*chip, c), me).wait_recv()
            passed[j].start()
        copy(0, sibling, me).wait_recv()
        for j, chip in enumerate(chips):
            copy(4 + j, (*chip, 1 - c), me).wait_recv()
        for cp in first + passed:
            cp.wait_send()
        mine.wait()

    return pl.pallas_call(
        body, name=name,
        out_shape=jax.ShapeDtypeStruct((N_DEV, r, c_), shard.dtype),
        in_specs=[pl.BlockSpec(memory_space=pl.ANY)],
        out_specs=pl.BlockSpec(memory_space=pl.ANY),
        scratch_shapes=[pltpu.SemaphoreType.DMA((7,)), pltpu.SemaphoreType.DMA((7,)),
                        pltpu.SemaphoreType.DMA],
    )(shard)


HBM_SPEC = pl.BlockSpec(memory_space=pltpu.HBM)
SEM_SPEC = pl.BlockSpec(memory_space=pltpu.SEMAPHORE)
DATAFLOW_EFFECT = pltpu.SideEffectType.DATAFLOW_SIDE_EFFECTING
N_PEERS = N_DEV - 1


def _exchange_copies(src_refs, land_refs, send_sems, recv_sems, gather):
    x, y, c = lax.axis_index("x"), lax.axis_index("y"), lax.axis_index("c")
    me = 4 * x + 2 * y + c
    pairs = []
    for b, (src_ref, land_ref) in enumerate(zip(src_refs, land_refs)):
        for k in range(1, N_DEV):
            px, py, pc = x ^ ((k >> 2) & 1), y ^ ((k >> 1) & 1), c ^ (k & 1)
            peer = 4 * px + 2 * py + pc
            sem = b * N_PEERS + k - 1
            src = src_ref if gather else src_ref.at[peer]
            out = pltpu.make_async_remote_copy(
                src_ref=src, dst_ref=land_ref.at[me], send_sem=send_sems.at[sem], recv_sem=recv_sems.at[sem],
                device_id=(px, py, pc), device_id_type=pl.DeviceIdType.MESH)
            inc = pltpu.make_async_remote_copy(
                src_ref=src, dst_ref=land_ref.at[peer], send_sem=send_sems.at[sem], recv_sem=recv_sems.at[sem],
                device_id=(px, py, pc), device_id_type=pl.DeviceIdType.MESH)
            pairs.append((out, inc))
    return pairs


def _exchange_start(srcs, lands, gather, name):
    n = len(srcs)

    def body(*refs):
        pairs = _exchange_copies(refs[:n], refs[n:2 * n], refs[2 * n], refs[2 * n + 1], gather)
        for out, _ in pairs:
            out.start()
        token = refs[-1]
        token[...] = jnp.zeros_like(token)

    sems = pltpu.SemaphoreType.DMA((N_PEERS * n,))
    bufs = list(srcs) + list(lands)
    outs = pl.pallas_call(
        body, name=name,
        out_shape=(sems, sems, *[pltpu.HBM(a.shape, a.dtype) for a in bufs],
                   jax.ShapeDtypeStruct((SUBLANES, LANES), F32)),
        in_specs=[HBM_SPEC] * (2 * n),
        out_specs=(SEM_SPEC, SEM_SPEC, *[HBM_SPEC] * (2 * n), pl.BlockSpec(memory_space=pltpu.VMEM)),
        input_output_aliases={i: 2 + i for i in range(2 * n)},
        compiler_params=pltpu.CompilerParams(has_side_effects=DATAFLOW_EFFECT),
    )(*[pltpu.with_memory_space_constraint(a, pltpu.HBM) for a in bufs])
    return (outs[0], outs[1]), outs[2:2 + n], outs[2 + n:2 + 2 * n], outs[-1]


def _exchange_wait(sems, srcs, lands, after, gather, name):
    n = len(srcs)

    def body(*refs):
        pairs = _exchange_copies(refs[:n], refs[n:2 * n], refs[2 * n], refs[2 * n + 1], gather)
        for out, inc in pairs:
            out.wait_send()
            inc.wait_recv()

    bufs = list(srcs) + list(lands)
    outs = pl.pallas_call(
        body, name=name,
        out_shape=[pltpu.HBM(a.shape, a.dtype) for a in bufs],
        in_specs=[HBM_SPEC] * (2 * n) + [SEM_SPEC, SEM_SPEC, pl.BlockSpec(memory_space=pl.ANY)],
        out_specs=[HBM_SPEC] * (2 * n),
        input_output_aliases={i: i for i in range(2 * n)},
        compiler_params=pltpu.CompilerParams(has_side_effects=DATAFLOW_EFFECT),
    )(*bufs, sems[0], sems[1], after)
    return outs[n:]


def _landing(own):
    me = 4 * lax.axis_index("x") + 2 * lax.axis_index("y") + lax.axis_index("c")
    land = lax.empty((N_DEV,) + own.shape, own.dtype)
    return lax.dynamic_update_slice(land, own[None], (me,) + (0,) * own.ndim)


def _after(a, dep):
    return lax.optimization_barrier((a, dep))[0]


def _mm(a, b, *, name, nt=False, out_dtype=F32, add=None, split_n=None, tm=512, tn=1024):
    m, k = a.shape
    n = b.shape[0] if nt else b.shape[1]
    tm = _tile(m, tm)
    tn = n // split_n if split_n else _tile(n, tn)
    dims = (((1,), (1,)), ((), ())) if nt else (((1,), (0,)), ((), ()))

    def body(*refs):
        a_ref, b_ref = refs[0], refs[1]
        o_ref = refs[-1]
        acc = lax.dot_general(a_ref[...], b_ref[...], dims, preferred_element_type=F32)
        if add is not None:
            acc = acc + refs[2][...]
        o_ref[...] = acc.astype(o_ref.dtype)

    in_specs = [pl.BlockSpec((tm, k), lambda i, j: (i, 0)),
                pl.BlockSpec((tn, k), lambda i, j: (j, 0)) if nt
                else pl.BlockSpec((k, tn), lambda i, j: (0, j))]
    args = [a, b]
    if add is not None:
        in_specs.append(pl.BlockSpec((tm, tn), lambda i, j: (i, j)))
        args.append(add)
    if split_n:
        out_shape = jax.ShapeDtypeStruct((split_n, m, tn), out_dtype)
        out_spec = pl.BlockSpec((None, tm, tn), lambda i, j: (j, i, 0))
    else:
        out_shape = jax.ShapeDtypeStruct((m, n), out_dtype)
        out_spec = pl.BlockSpec((tm, tn), lambda i, j: (i, j))
    return pl.pallas_call(
        body, name=name, grid=(m // tm, n // tn), in_specs=in_specs, out_specs=out_spec,
        out_shape=out_shape, compiler_params=_params("parallel", "parallel"),
    )(*args)


def _mm_win_fwd(h, w_g, name):
    m, k = h.shape
    nj = w_g.shape[2]
    tm = _tile(m, 512)

    def body(a_ref, b_ref, o_ref):
        o_ref[...] = jnp.dot(a_ref[...], b_ref[...], preferred_element_type=F32)

    return pl.pallas_call(
        body, name=name, grid=(N_DEV, m // tm),
        in_specs=[pl.BlockSpec((tm, k), lambda j, i: (i, 0)),
                  pl.BlockSpec((None, k, nj), lambda j, i: (j, 0, 0))],
        out_specs=pl.BlockSpec((tm, nj), lambda j, i: (i, j)),
        out_shape=jax.ShapeDtypeStruct((m, N_DEV * nj), F32),
        compiler_params=_params("parallel", "parallel"),
    )(h, w_g)


def _mm_win_bwd(dproj, w_g, name):
    m = dproj.shape[0]
    d, nj = w_g.shape[1], w_g.shape[2]
    tm = _tile(m, 512)
    tn = _tile(d, 1024)

    def body(a_ref, b_ref, o_ref, acc_ref):
        j = pl.program_id(2)

        @pl.when(j == 0)
        def _():
            acc_ref[...] = jnp.zeros_like(acc_ref)

        acc_ref[...] += lax.dot_general(a_ref[...], b_ref[...], (((1,), (1,)), ((), ())),
                                        preferred_element_type=F32)

        @pl.when(j == N_DEV - 1)
        def _():
            o_ref[...] = acc_ref[...]

    return pl.pallas_call(
        body, name=name, grid=(m // tm, d // tn, N_DEV),
        in_specs=[pl.BlockSpec((tm, nj), lambda i, n, j: (i, j)),
                  pl.BlockSpec((None, tn, nj), lambda i, n, j: (j, n, 0))],
        out_specs=pl.BlockSpec((tm, tn), lambda i, n, j: (i, n)),
        out_shape=jax.ShapeDtypeStruct((m, d), F32),
        scratch_shapes=[pltpu.VMEM((tm, tn), F32)],
        compiler_params=_params("parallel", "parallel", "arbitrary"),
    )(dproj, w_g)


def _row_spec(tr, w, col):
    return pl.BlockSpec((tr, w), lambda i: (i, col))


def _full_spec(shape):
    return pl.BlockSpec(shape, lambda i: (0,) * len(shape))


def _rmsnorm_fwd(x, g, name):
    l, d = x.shape
    tr = _tile(l, 256)

    def body(x_ref, g_ref, o_ref):
        xv = x_ref[...]
        rstd = lax.rsqrt(jnp.mean(xv * xv, axis=-1, keepdims=True) + RMS_EPS)
        o_ref[...] = (xv * rstd * g_ref[...]).astype(o_ref.dtype)

    return pl.pallas_call(
        body, name=name, grid=(l // tr,),
        in_specs=[_row_spec(tr, d, 0), _full_spec((1, d))],
        out_specs=_row_spec(tr, d, 0),
        out_shape=jax.ShapeDtypeStruct((l, d), BF16),
        compiler_params=_params("parallel"),
    )(x, g.reshape(1, d))


def _rmsnorm_bwd(x, g, dh, dxo, name):
    l, d = x.shape
    tr = _tile(l, 256)

    def body(x_ref, g_ref, dh_ref, dxo_ref, dx_ref, dg_ref):
        xv = x_ref[...]
        rstd = lax.rsqrt(jnp.mean(xv * xv, axis=-1, keepdims=True) + RMS_EPS)
        dhv = dh_ref[...]
        gdy = dhv * g_ref[...]
        dot = jnp.mean(gdy * xv, axis=-1, keepdims=True)
        dx_ref[...] = dxo_ref[...] + rstd * gdy - xv * (rstd * rstd * rstd * dot)

        @pl.when(pl.program_id(0) == 0)
        def _():
            dg_ref[...] = jnp.zeros_like(dg_ref)

        dg_ref[...] += jnp.sum(dhv * xv * rstd, axis=0, keepdims=True)

    return pl.pallas_call(
        body, name=name, grid=(l // tr,),
        in_specs=[_row_spec(tr, d, 0), _full_spec((1, d)), _row_spec(tr, d, 0), _row_spec(tr, d, 0)],
        out_specs=[_row_spec(tr, d, 0), _full_spec((1, d))],
        out_shape=[jax.ShapeDtypeStruct((l, d), F32), jax.ShapeDtypeStruct((1, d), F32)],
        compiler_params=_params("arbitrary"),
    )(x, g.reshape(1, d), dh, dxo)


def _final_loss(x, g, tgt, name):
    l, d = x.shape
    tr = _tile(l, 256)

    def body(x_ref, g_ref, t_ref, dx_ref, dg_ref, loss_ref):
        xv = x_ref[...]
        gv = g_ref[...]
        rstd = lax.rsqrt(jnp.mean(xv * xv, axis=-1, keepdims=True) + RMS_EPS)
        xn = xv * rstd
        err = xn * gv - t_ref[...]
        dy = err * (1.0 / d)
        gdy = dy * gv
        dot = jnp.mean(gdy * xv, axis=-1, keepdims=True)
        dx_ref[...] = rstd * gdy - xv * (rstd * rstd * rstd * dot)

        @pl.when(pl.program_id(0) == 0)
        def _():
            dg_ref[...] = jnp.zeros_like(dg_ref)
            loss_ref[...] = jnp.zeros_like(loss_ref)

        dg_ref[...] += jnp.sum(dy * xn, axis=0, keepdims=True)
        loss_ref[...] += (0.5 / d) * jnp.sum(err * err)

    return pl.pallas_call(
        body, name=name, grid=(l // tr,),
        in_specs=[_row_spec(tr, d, 0), _full_spec((1, d)), _row_spec(tr, d, 0)],
        out_specs=[_row_spec(tr, d, 0), _full_spec((1, d)), _full_spec((SUBLANES, LANES))],
        out_shape=[jax.ShapeDtypeStruct((l, d), F32), jax.ShapeDtypeStruct((1, d), F32),
                   jax.ShapeDtypeStruct((SUBLANES, LANES), F32)],
        compiler_params=_params("arbitrary"),
    )(x, g.reshape(1, d), tgt)


def _halo_spec(tr, w, col, nblk8, before):
    step = tr // SUBLANES
    if before:
        return pl.BlockSpec((SUBLANES, w), lambda i: (jnp.maximum(i * step - 1, 0), col))
    return pl.BlockSpec((SUBLANES, w), lambda i: (jnp.minimum((i + 1) * step, nblk8 - 1), col))


def _shift_down(cur, before, k):
    ext = jnp.concatenate([before, cur], axis=0)
    return pltpu.roll(ext, k, axis=0)[SUBLANES:, :]


def _shift_up(cur, after, k):
    tr = cur.shape[0]
    ext = jnp.concatenate([cur, after], axis=0)
    return pltpu.roll(ext, tr + SUBLANES - k, axis=0)[:tr, :]


def _branch_a_fwd(proj, conv_w, d, name):
    l = proj.shape[0]
    tr = _tile(l, 256)
    nblk8 = l // SUBLANES

    def body(v_ref, bg_ref, cg_ref, za_ref, vh_ref, cgh_ref, w_ref, o_ref):
        first = pl.program_id(0) == 0
        cv = cg_ref[...] * v_ref[...]
        cvh = jnp.where(first, 0.0, cgh_ref[...] * vh_ref[...])
        w0, w1, w2 = w_ref[0:1, :], w_ref[1:2, :], w_ref[2:3, :]
        q = w2 * cv + w1 * _shift_down(cv, cvh, 1) + w0 * _shift_down(cv, cvh, 2)
        za = za_ref[...]
        o_ref[...] = (bg_ref[...] * q * (za * _sigmoid(za))).astype(o_ref.dtype)

    return pl.pallas_call(
        body, name=name, grid=(l // tr,),
        in_specs=[_row_spec(tr, d, 0), _row_spec(tr, d, 1), _row_spec(tr, d, 2), _row_spec(tr, d, 3),
                  _halo_spec(tr, d, 0, nblk8, True), _halo_spec(tr, d, 2, nblk8, True),
                  _full_spec((SUBLANES, d))],
        out_specs=_row_spec(tr, d, 0),
        out_shape=jax.ShapeDtypeStruct((l, d), BF16),
        compiler_params=_params("parallel"),
    )(proj, proj, proj, proj, proj, proj, conv_w)


def _branch_a_bwd(proj, dpa, conv_w, d, name):
    l = proj.shape[0]
    tr = _tile(l, 128)
    nblk8 = l // SUBLANES
    ntiles = l // tr

    def body(v_ref, bg_ref, cg_ref, za_ref, dpa_ref, vh_ref, cgh_ref, bgn_ref, zan_ref, dpan_ref,
             w_ref, dv_ref, dbg_ref, dcg_ref, dza_ref, dw0_ref, dw1_ref, dw2_ref):
        i = pl.program_id(0)
        v, bg, cg, za, dpa_v = v_ref[...], bg_ref[...], cg_ref[...], za_ref[...], dpa_ref[...]
        w0, w1, w2 = w_ref[0:1, :], w_ref[1:2, :], w_ref[2:3, :]
        cv = cg * v
        cvh = jnp.where(i == 0, 0.0, cgh_ref[...] * vh_ref[...])
        cv1 = _shift_down(cv, cvh, 1)
        cv2 = _shift_down(cv, cvh, 2)
        q = w2 * cv + w1 * cv1 + w0 * cv2
        sg = _sigmoid(za)
        s = za * sg
        dbg_ref[...] = (dpa_v * q * s).astype(dbg_ref.dtype)
        dza_ref[...] = (dpa_v * bg * q * (sg * (1.0 + za * (1.0 - sg)))).astype(dza_ref.dtype)
        dq = dpa_v * bg * s
        zan = zan_ref[...]
        dqn = jnp.where(i == ntiles - 1, 0.0, dpan_ref[...] * bgn_ref[...] * (zan * _sigmoid(zan)))
        dcv = w2 * dq + w1 * _shift_up(dq, dqn, 1) + w0 * _shift_up(dq, dqn, 2)
        dcg_ref[...] = (dcv * v).astype(dcg_ref.dtype)
        dv_ref[...] = (dcv * cg).astype(dv_ref.dtype)

        @pl.when(i == 0)
        def _():
            dw0_ref[...] = jnp.zeros_like(dw0_ref)
            dw1_ref[...] = jnp.zeros_like(dw1_ref)
            dw2_ref[...] = jnp.zeros_like(dw2_ref)

        dw0_ref[...] += jnp.sum(dq * cv2, axis=0, keepdims=True)
        dw1_ref[...] += jnp.sum(dq * cv1, axis=0, keepdims=True)
        dw2_ref[...] += jnp.sum(dq * cv, axis=0, keepdims=True)

    act = jax.ShapeDtypeStruct((l, d), BF16)
    wsum = jax.ShapeDtypeStruct((1, d), F32)
    return pl.pallas_call(
        body, name=name, grid=(ntiles,),
        in_specs=[_row_spec(tr, d, 0), _row_spec(tr, d, 1), _row_spec(tr, d, 2), _row_spec(tr, d, 3),
                  _row_spec(tr, d, 0),
                  _halo_spec(tr, d, 0, nblk8, True), _halo_spec(tr, d, 2, nblk8, True),
                  _halo_spec(tr, d, 1, nblk8, False), _halo_spec(tr, d, 3, nblk8, False),
                  _halo_spec(tr, d, 0, nblk8, False),
                  _full_spec((SUBLANES, d))],
        out_specs=[_row_spec(tr, d, 0)] * 4 + [_full_spec((1, d))] * 3,
        out_shape=[act] * 4 + [wsum] * 3,
        compiler_params=_params("arbitrary"),
    )(proj, proj, proj, proj, dpa, proj, proj, proj, proj, dpa, conv_w)


def _gelu_cast(y, name):
    l, w = y.shape
    tr = _tile(l, 512)

    def body(y_ref, o_ref):
        o_ref[...] = _gelu(y_ref[...]).astype(o_ref.dtype)

    return pl.pallas_call(
        body, name=name, grid=(l // tr,), in_specs=[_row_spec(tr, w, 0)],
        out_specs=_row_spec(tr, w, 0), out_shape=jax.ShapeDtypeStruct((l, w), BF16),
        compiler_params=_params("parallel"),
    )(y)


def _glu_post(y, gl, proj, b_glu, zb_col, name):
    l, w = y.shape
    tr = _tile(l, 512)

    def body(y_ref, gl_ref, zb_ref, b_ref, o_ref):
        zb = zb_ref[...]
        o_ref[...] = (_gelu(y_ref[...]) * _sigmoid(gl_ref[...] + b_ref[...])
                      * (zb * _sigmoid(zb))).astype(o_ref.dtype)

    return pl.pallas_call(
        body, name=name, grid=(l // tr,),
        in_specs=[_row_spec(tr, w, 0), _row_spec(tr, w, 0), _row_spec(tr, w, zb_col), _full_spec((1, w))],
        out_specs=_row_spec(tr, w, 0), out_shape=jax.ShapeDtypeStruct((l, w), BF16),
        compiler_params=_params("parallel"),
    )(y, gl, proj, b_glu.reshape(1, w))


def _glu_bwd1(y, gl, proj, b_glu, dpb, zb_col, name):
    l, w = y.shape
    tr = _tile(l, 512)

    def body(y_ref, gl_ref, zb_ref, b_ref, dpb_ref, dzb_ref, dgl_ref, t_ref, db_ref):
        zb = zb_ref[...]
        dpb_v = dpb_ref[...]
        yg = _gelu(y_ref[...])
        sgl = _sigmoid(gl_ref[...] + b_ref[...])
        szb = _sigmoid(zb)
        dzb_ref[...] = (dpb_v * yg * sgl * (szb * (1.0 + zb * (1.0 - szb)))).astype(dzb_ref.dtype)
        e = dpb_v * (zb * szb)
        dgl = e * yg * sgl * (1.0 - sgl)
        dgl_ref[...] = dgl.astype(dgl_ref.dtype)
        t_ref[...] = e * sgl

        @pl.when(pl.program_id(0) == 0)
        def _():
            db_ref[...] = jnp.zeros_like(db_ref)

        db_ref[...] += jnp.sum(dgl, axis=0, keepdims=True)

    return pl.pallas_call(
        body, name=name, grid=(l // tr,),
        in_specs=[_row_spec(tr, w, 0), _row_spec(tr, w, 0), _row_spec(tr, w, zb_col), _full_spec((1, w)),
                  _row_spec(tr, w, 0)],
        out_specs=[_row_spec(tr, w, 0)] * 3 + [_full_spec((1, w))],
        out_shape=[jax.ShapeDtypeStruct((l, w), BF16), jax.ShapeDtypeStruct((l, w), BF16),
                   jax.ShapeDtypeStruct((l, w), F32), jax.ShapeDtypeStruct((1, w), F32)],
        compiler_params=_params("arbitrary"),
    )(y, gl, proj, b_glu.reshape(1, w), dpb)


def _glu_bwd2(y, t1, dyg2, name):
    l, w = y.shape
    tr = _tile(l, 512)

    def body(y_ref, t_ref, d_ref, o_ref):
        o_ref[...] = (t_ref[...] + d_ref[...]) * _gelu_grad(y_ref[...])

    return pl.pallas_call(
        body, name=name, grid=(l // tr,), in_specs=[_row_spec(tr, w, 0)] * 3,
        out_specs=_row_spec(tr, w, 0), out_shape=jax.ShapeDtypeStruct((l, w), F32),
        compiler_params=_params("parallel"),
    )(y, t1, dyg2)


def _merge_fwd(proj, ya, yb, d, ga_col, gb_col, name):
    l = proj.shape[0]
    tr = _tile(l, 256)

    def body(ga_ref, gb_ref, ya_ref, yb_ref, o_ref):
        o_ref[...] = (_sigmoid(ga_ref[...]) * ya_ref[...]
                      + _sigmoid(gb_ref[...]) * yb_ref[...]).astype(o_ref.dtype)

    return pl.pallas_call(
        body, name=name, grid=(l // tr,),
        in_specs=[_row_spec(tr, d, ga_col), _row_spec(tr, d, gb_col), _row_spec(tr, d, 0), _row_spec(tr, d, 0)],
        out_specs=_row_spec(tr, d, 0), out_shape=jax.ShapeDtypeStruct((l, d), BF16),
        compiler_params=_params("parallel"),
    )(proj, proj, ya, yb)


def _merge_bwd(proj, ya, yb, dm, d, ga_col, gb_col, name):
    l = proj.shape[0]
    tr = _tile(l, 256)

    def body(ga_ref, gb_ref, ya_ref, yb_ref, dm_ref, dya_ref, dyb_ref, dga_ref, dgb_ref):
        dmv = dm_ref[...]
        sa = _sigmoid(ga_ref[...])
        sb = _sigmoid(gb_ref[...])
        dya_ref[...] = (dmv * sa).astype(dya_ref.dtype)
        dyb_ref[...] = (dmv * sb).astype(dyb_ref.dtype)
        dga_ref[...] = (dmv * ya_ref[...] * sa * (1.0 - sa)).astype(dga_ref.dtype)
        dgb_ref[...] = (dmv * yb_ref[...] * sb * (1.0 - sb)).astype(dgb_ref.dtype)

    act = jax.ShapeDtypeStruct((l, d), BF16)
    return pl.pallas_call(
        body, name=name, grid=(l // tr,),
        in_specs=[_row_spec(tr, d, ga_col), _row_spec(tr, d, gb_col), _row_spec(tr, d, 0), _row_spec(tr, d, 0),
                  _row_spec(tr, d, 0)],
        out_specs=[_row_spec(tr, d, 0)] * 4, out_shape=[act] * 4,
        compiler_params=_params("parallel"),
    )(proj, proj, ya, yb, dm)


def _to_segments(a):
    l, w = a.shape
    return a.reshape(SUBLANES, l // SUBLANES, w).transpose(1, 0, 2).reshape(l, w)


def _from_segments(a):
    l, w = a.shape
    return a.reshape(l // SUBLANES, SUBLANES, w).transpose(1, 0, 2).reshape(l, w)


def _dense(z, shape):
    return jnp.broadcast_to(z, shape).reshape(-1, LANES)


def _s5_disc(are, aim, ldt):
    dt = jnp.exp(ldt)
    er = jnp.exp(are * dt)
    lbr = er * jnp.cos(aim * dt)
    lbi = er * jnp.sin(aim * dt)
    inv = 1.0 / (are * are + aim * aim)
    fr = ((lbr - 1.0) * are + lbi * aim) * inv
    fi = (lbi * are - (lbr - 1.0) * aim) * inv
    return dt, lbr, lbi, inv, fr, fi


def _s5_params(are, aim, ldt, bre, bim, name):
    shape = are.shape

    def body(are_ref, aim_ref, ldt_ref, bre_ref, bim_ref, lbr_ref, lbi_ref, bbr_ref, bbi_ref):
        _, lbr, lbi, _, fr, fi = _s5_disc(are_ref[...], aim_ref[...], ldt_ref[...])
        lbr_ref[...] = lbr
        lbi_ref[...] = lbi
        bbr_ref[...] = fr * bre_ref[...] - fi * bim_ref[...]
        bbi_ref[...] = fr * bim_ref[...] + fi * bre_ref[...]

    out = jax.ShapeDtypeStruct(shape, F32)
    return pl.pallas_call(body, name=name, out_shape=[out] * 4,
                          compiler_params=pltpu.CompilerParams(vmem_limit_bytes=VMEM_LIMIT_BYTES),
                          )(are, aim, ldt, bre, bim)


def _s5_params_bwd(are, aim, ldt, bre, bim, glbr, glbi, gbbr, gbbi, n_groups, name):
    shape = are.shape
    rows_per_group = shape[0] // n_groups

    def body(are_ref, aim_ref, ldt_ref, bre_ref, bim_ref, glbr_ref, glbi_ref, gbbr_ref, gbbi_ref,
             gar_ref, gai_ref, gdt_ref, gbr_ref, gbi_ref):
        are_v, aim_v = are_ref[...], aim_ref[...]
        bre_v, bim_v = bre_ref[...], bim_ref[...]
        gbbr_v, gbbi_v = gbbr_ref[...], gbbi_ref[...]
        dt, lbr, lbi, inv, fr, fi = _s5_disc(are_v, aim_v, ldt_ref[...])
        gbr_ref[...] = fr * gbbr_v + fi * gbbi_v
        gbi_ref[...] = fr * gbbi_v - fi * gbbr_v
        lane_group = lax.broadcasted_iota(jnp.int32, (LANES, LANES), 0) // S5_GROUP
        same_group = (lane_group == lax.broadcasted_iota(jnp.int32, (LANES, LANES), 1) // S5_GROUP)
        ones = same_group.astype(F32)
        gfr = jnp.dot(bre_v * gbbr_v + bim_v * gbbi_v, ones, precision=HIGHEST, preferred_element_type=F32)
        gfi = jnp.dot(bre_v * gbbi_v - bim_v * gbbr_v, ones, precision=HIGHEST, preferred_element_type=F32)
        glr = glbr_ref[...] + (are_v * gfr - aim_v * gfi) * inv
        gli = glbi_ref[...] + (are_v * gfi + aim_v * gfr) * inv
        qr = (fr * are_v + fi * aim_v) * inv
        qi = (fi * are_v - fr * aim_v) * inv
        gzr = lbr * glr + lbi * gli
        gzi = lbr * gli - lbi * glr
        gar_ref[...] = dt * gzr - (qr * gfr + qi * gfi)
        gai_ref[...] = dt * gzi - (qr * gfi - qi * gfr)
        e = dt * (are_v * gzr + aim_v * gzi)
        per_group = jnp.sum(e.reshape(n_groups, rows_per_group, LANES), axis=1)
        total = jnp.sum(per_group, axis=1, keepdims=True) * (1.0 / S5_GROUP)
        gdt_ref[...] = jnp.broadcast_to(total, gdt_ref.shape)

    out = jax.ShapeDtypeStruct(shape, F32)
    return pl.pallas_call(
        body, name=name,
        out_shape=[out, out, jax.ShapeDtypeStruct((n_groups, LANES), F32), out, out],
        compiler_params=pltpu.CompilerParams(vmem_limit_bytes=VMEM_LIMIT_BYTES),
    )(are, aim, ldt, bre, bim, glbr, glbi, gbbr, gbbi)


def _s5_up(xs, m_re, m_im, name):
    l = xs.shape[0]
    nb, kin, kout = m_re.shape
    tl = _tile(l, 512)

    def body(x_ref, mr_ref, mi_ref, or_ref, oi_ref):
        xv = x_ref[...]
        or_ref[...] = jnp.dot(xv, mr_ref[...], precision=HIGHEST, preferred_element_type=F32)
        oi_ref[...] = jnp.dot(xv, mi_ref[...], precision=HIGHEST, preferred_element_type=F32)

    out = jax.ShapeDtypeStruct((l, nb * kout), F32)
    return pl.pallas_call(
        body, name=name, grid=(l // tl, nb),
        in_specs=[pl.BlockSpec((tl, kin), lambda i, b: (i, b)),
                  pl.BlockSpec((None, kin, kout), lambda i, b: (b, 0, 0)),
                  pl.BlockSpec((None, kin, kout), lambda i, b: (b, 0, 0))],
        out_specs=[pl.BlockSpec((tl, kout), lambda i, b: (i, b))] * 2,
        out_shape=[out, out], compiler_params=_params("parallel", "parallel"),
    )(xs, m_re, m_im)


def _s5_down(a_re, a_im, m1, m2, xs, dvec, name):
    l = xs.shape[0]
    nb, kin, kout = m1.shape
    tl = _tile(l, 512)

    def body(ar_ref, ai_ref, m1_ref, m2_ref, x_ref, d_ref, o_ref):
        o_ref[...] = (jnp.dot(ar_ref[...], m1_ref[...], precision=HIGHEST, preferred_element_type=F32)
                      - jnp.dot(ai_ref[...], m2_ref[...], precision=HIGHEST, preferred_element_type=F32)
                      + d_ref[...] * x_ref[...])

    return pl.pallas_call(
        body, name=name, grid=(l // tl, nb),
        in_specs=[pl.BlockSpec((tl, kin), lambda i, b: (i, b)), pl.BlockSpec((tl, kin), lambda i, b: (i, b)),
                  pl.BlockSpec((None, kin, kout), lambda i, b: (b, 0, 0)),
                  pl.BlockSpec((None, kin, kout), lambda i, b: (b, 0, 0)),
                  pl.BlockSpec((tl, kout), lambda i, b: (i, b)),
                  pl.BlockSpec((1, kout), lambda i, b: (0, b))],
        out_specs=pl.BlockSpec((tl, kout), lambda i, b: (i, b)),
        out_shape=jax.ShapeDtypeStruct((l, nb * kout), F32),
        compiler_params=_params("parallel", "parallel"),
    )(a_re, a_im, m1, m2, xs, dvec)


def _s5_outer(xt, h_re, h_im, name):
    l = xt.shape[1]
    nb = xt.shape[0] // LANES
    kout = h_re.shape[1] // nb
    tl = _tile(l, 512)

    def body(x_ref, hr_ref, hi_ref, or_ref, oi_ref):
        @pl.when(pl.program_id(1) == 0)
        def _():
            or_ref[...] = jnp.zeros_like(or_ref)
            oi_ref[...] = jnp.zeros_like(oi_ref)

        xv = x_ref[...]
        or_ref[...] += jnp.dot(xv, hr_ref[...], precision=HIGHEST, preferred_element_type=F32)
        oi_ref[...] += jnp.dot(xv, hi_ref[...], precision=HIGHEST, preferred_element_type=F32)

    out = jax.ShapeDtypeStruct((nb, LANES, kout), F32)
    return pl.pallas_call(
        body, name=name, grid=(nb, l // tl),
        in_specs=[pl.BlockSpec((LANES, tl), lambda b, t: (b, t)),
                  pl.BlockSpec((tl, kout), lambda b, t: (t, b)),
                  pl.BlockSpec((tl, kout), lambda b, t: (t, b))],
        out_specs=[pl.BlockSpec((None, LANES, kout), lambda b, t: (b, 0, 0))] * 2,
        out_shape=[out, out], compiler_params=_params("parallel", "arbitrary"),
    )(xt, h_re, h_im)


def _cmul(ar, ai, br, bi):
    return ar * br - ai * bi, ar * bi + ai * br


def _s5_scan(x_re, x_im, lam_re, lam_im, reverse, name):
    l, n = x_re.shape
    wb = _tile(n, 256)
    nt = l // SUBLANES
    shift = SUBLANES - 1 if reverse else 1
    unroll = 8 if nt % 8 == 0 else 1

    def rows(k):
        t = (nt - 1 - k) if reverse else k
        return pl.ds(pl.multiple_of(t * SUBLANES, SUBLANES), SUBLANES)

    def body(xr_ref, xi_ref, lr_ref, li_ref, hr_ref, hi_ref):
        lr = jnp.broadcast_to(lr_ref[...], (SUBLANES, wb))
        li = jnp.broadcast_to(li_ref[...], (SUBLANES, wb))
        zero = jnp.zeros((SUBLANES, wb), F32)
        one = jnp.ones((SUBLANES, wb), F32)

        def local_step(k, carry):
            hr, hi, pr, pi = carry
            r = rows(k)
            tr_, ti_ = _cmul(lr, li, hr, hi)
            hr, hi = tr_ + xr_ref[r, :], ti_ + xi_ref[r, :]
            hr_ref[r, :] = hr
            hi_ref[r, :] = hi
            pr, pi = _cmul(lr, li, pr, pi)
            return hr, hi, pr, pi

        er, ei, lnr, lni = lax.fori_loop(0, nt, local_step, (zero, zero, one, zero), unroll=unroll)

        row = lax.broadcasted_iota(jnp.int32, (SUBLANES, wb), 0)
        tr_, ti_ = er, ei
        for j in range(1, SUBLANES):
            pr_, pi_ = _cmul(lnr, lni, pltpu.roll(tr_, shift, axis=0), pltpu.roll(ti_, shift, axis=0))
            at = row == ((SUBLANES - 1 - j) if reverse else j)
            tr_ = jnp.where(at, er + pr_, tr_)
            ti_ = jnp.where(at, ei + pi_, ti_)
        edge = row == ((SUBLANES - 1) if reverse else 0)
        cr = jnp.where(edge, 0.0, pltpu.roll(tr_, shift, axis=0))
        ci = jnp.where(edge, 0.0, pltpu.roll(ti_, shift, axis=0))

        def fix_step(k, carry):
            pr, pi = carry
            pr, pi = _cmul(lr, li, pr, pi)
            r = rows(k)
            ar_, ai_ = _cmul(pr, pi, cr, ci)
            hr_ref[r, :] = hr_ref[r, :] + ar_
            hi_ref[r, :] = hi_ref[r, :] + ai_
            return pr, pi

        lax.fori_loop(0, nt, fix_step, (one, zero), unroll=unroll)

    out = jax.ShapeDtypeStruct((l, n), F32)
    col = lambda b: (0, b)
    return pl.pallas_call(
        body, name=name, grid=(n // wb,),
        in_specs=[pl.BlockSpec((l, wb), col), pl.BlockSpec((l, wb), col),
                  pl.BlockSpec((1, wb), col), pl.BlockSpec((1, wb), col)],
        out_specs=[pl.BlockSpec((l, wb), col)] * 2, out_shape=[out, out],
        compiler_params=_params("parallel"),
    )(x_re, x_im, lam_re, lam_im)


def _s5_dlam(g_re, g_im, h_re, h_im, name):
    l, n = g_re.shape
    wb = _tile(n, 256)
    nt = l // SUBLANES

    def body(gr_ref, gi_ref, hr_ref, hi_ref, or_ref, oi_ref):
        row = lax.broadcasted_iota(jnp.int32, (SUBLANES, wb), 0)
        last = pl.ds((nt - 1) * SUBLANES, SUBLANES)
        first = pl.ds(0, SUBLANES)
        pr = jnp.where(row == 0, 0.0, pltpu.roll(hr_ref[last, :], 1, axis=0))
        pi = jnp.where(row == 0, 0.0, pltpu.roll(hi_ref[last, :], 1, axis=0))
        gr, gi = gr_ref[first, :], gi_ref[first, :]
        acc_r = gr * pr + gi * pi
        acc_i = gi * pr - gr * pi

        def step(t, carry):
            acc_r, acc_i = carry
            cur = pl.ds(pl.multiple_of(t * SUBLANES, SUBLANES), SUBLANES)
            prev = pl.ds(pl.multiple_of((t - 1) * SUBLANES, SUBLANES), SUBLANES)
            gr, gi = gr_ref[cur, :], gi_ref[cur, :]
            pr, pi = hr_ref[prev, :], hi_ref[prev, :]
            return acc_r + gr * pr + gi * pi, acc_i + gi * pr - gr * pi

        acc_r, acc_i = lax.fori_loop(1, nt, step, (acc_r, acc_i))
        or_ref[...] = jnp.sum(acc_r, axis=0, keepdims=True)
        oi_ref[...] = jnp.sum(acc_i, axis=0, keepdims=True)

    col = lambda b: (0, b)
    out = jax.ShapeDtypeStruct((1, n), F32)
    return pl.pallas_call(
        body, name=name, grid=(n // wb,),
        in_specs=[pl.BlockSpec((l, wb), col)] * 4,
        out_specs=[pl.BlockSpec((1, wb), col)] * 2, out_shape=[out, out],
        compiler_params=_params("parallel"),
    )(g_re, g_im, h_re, h_im)


def _colsum_prod(a, b, name):
    l, w = a.shape
    tr = _tile(l, 512)

    def body(a_ref, b_ref, o_ref):
        @pl.when(pl.program_id(0) == 0)
        def _():
            o_ref[...] = jnp.zeros_like(o_ref)

        o_ref[...] += jnp.sum(a_ref[...] * b_ref[...], axis=0, keepdims=True)

    return pl.pallas_call(
        body, name=name, grid=(l // tr,), in_specs=[_row_spec(tr, w, 0)] * 2,
        out_specs=_full_spec((1, w)), out_shape=jax.ShapeDtypeStruct((1, w), F32),
        compiler_params=_params("arbitrary"),
    )(a, b)


def _block_diag(m, nb):
    g, r, s = m.shape
    gb = g // nb
    eye = jnp.eye(gb, dtype=m.dtype)
    out = m.reshape(nb, gb, r, 1, s) * eye[None, :, None, :, None]
    return out.reshape(nb, gb * r, gb * s)


def _block_diag_extract(mat, g, r, s):
    nb = mat.shape[0]
    gb = g // nb
    eye = jnp.eye(gb, dtype=mat.dtype)
    m5 = mat.reshape(nb, gb, r, gb, s) * eye[None, :, None, :, None]
    return jnp.sum(m5, axis=3).reshape(g, r, s)


def _adamw(w, m, v, gslots, name, layer=0, prev=None):
    layers, r, c = w.shape
    s = gslots.shape[0]
    tr = _tile(r, max(SUBLANES, 1 << int(math.log2(ADAMW_BLOCK_ELEMS // c))))
    bc1 = 1.0 / (1.0 - ADAM_B1 ** ADAM_STEP)
    bc2 = 1.0 / (1.0 - ADAM_B2 ** ADAM_STEP)

    def body(w_ref, m_ref, v_ref, g_ref, *rest):
        go_ref, d_ref, mo_ref, vo_ref = rest[-4:]
        g = g_ref[0].astype(F32)
        for k in range(1, s):
            g = g + g_ref[k].astype(F32)
        mn = ADAM_B1 * m_ref[...] + (1.0 - ADAM_B1) * g
        vn = ADAM_B2 * v_ref[...] + (1.0 - ADAM_B2) * (g * g)
        go_ref[...] = g
        mo_ref[...] = mn
        vo_ref[...] = vn
        d_ref[...] = -ADAM_LR * ((mn * bc1) / (jnp.sqrt(vn * bc2) + ADAM_EPS) + ADAM_WD * w_ref[...])

    spec = pl.BlockSpec((None, tr, c), lambda i: (layer, i, 0))
    out = jax.ShapeDtypeStruct((layers, r, c), F32)
    in_specs = [spec, spec, spec, pl.BlockSpec((s, tr, c), lambda i: (0, i, 0))]
    args = [w, m, v, gslots]
    aliases = {}
    if prev is not None:
        in_specs += [pl.BlockSpec(memory_space=pl.ANY)] * 4
        args += list(prev)
        aliases = {4 + q: q for q in range(4)}
    return pl.pallas_call(
        body, name=name, grid=(r // tr,), in_specs=in_specs,
        out_specs=[spec] * 4, out_shape=[out] * 4, input_output_aliases=aliases,
        compiler_params=_params("parallel"),
    )(*args)


def _pack(parts):
    flat = jnp.concatenate([p.reshape(-1) for p in parts])
    pad = (-flat.shape[0]) % (PACK_ROWS * LANES)
    return jnp.pad(flat, (0, pad)).reshape(-1, LANES)


def _unpack(packed, shapes):
    flat = packed.reshape(-1)
    out, off = [], 0
    for shp in shapes:
        size = math.prod(shp)
        out.append(flat[off:off + size].reshape(shp))
        off += size
    return out


def kernel(x, norm_g, w_in, conv_w, w_out_a, a_re, a_im, log_dt, b_re, b_im, c_re, c_im, d_skip, w_glu, b_glu, w_out_b, w_o, final_g, loss_target, m_norm_g, m_w_in, m_conv_w, m_w_out_a, m_a_re, m_a_im, m_log_dt, m_b_re, m_b_im, m_c_re, m_c_im, m_d_skip, m_w_glu, m_b_glu, m_w_out_b, m_w_o, m_final_g, v_norm_g, v_w_in, v_conv_w, v_w_out_a, v_a_re, v_a_im, v_log_dt, v_b_re, v_b_im, v_c_re, v_c_im, v_d_skip, v_w_glu, v_b_glu, v_w_out_b, v_w_o, v_final_g):
    depth = norm_g.shape[0]
    l, d = x.shape[1], x.shape[2]
    ws = w_glu.shape[2]
    n_groups, n_state = a_re.shape[1], a_re.shape[2]
    nb = ws // LANES
    assert S5_GROUP == b_re.shape[3] and n_state * S5_GB == 4 * LANES
    u_col, zb_col = 4 * d // ws, 4 * d // ws + 1
    ga_col, gb_col = (4 * d + 2 * ws) // d, (4 * d + 2 * ws) // d + 1
    me = 4 * lax.axis_index("x") + 2 * lax.axis_index("y") + lax.axis_index("c")

    xs = [x[0]]
    tgt = loss_target[0]

    conv_full = _all_gather(jnp.pad(conv_w.reshape(depth * 3, -1), ((0, SUBLANES - depth * 3), (0, 0))),
                            "ag_conv_w")
    conv_full = conv_full.transpose(1, 0, 2).reshape(SUBLANES, d)[:depth * 3].reshape(depth, 3, d)
    big_names = ("w_in", "w_out_a", "w_glu", "w_out_b", "w_o")
    big = dict(w_in=(w_in, m_w_in, v_w_in), w_out_a=(w_out_a, m_w_out_a, v_w_out_a),
               w_glu=(w_glu, m_w_glu, v_w_glu), w_out_b=(w_out_b, m_w_out_b, v_w_out_b),
               w_o=(w_o, m_w_o, v_w_o))

    def shards_bf16(i):
        return [big[k][0][i].astype(BF16) for k in big_names]

    gathered = {0: [_all_gather(s_, f"ag_{k}_0") for k, s_ in zip(big_names, shards_bf16(0))]}
    ag_pending = {}
    dep = gathered[0][-1]
    for i in range(1, depth):
        shards = _after(shards_bf16(i), dep)
        sems, srcs, lands, dep = _exchange_start(shards, [_landing(s_) for s_ in shards], True, f"ag_start_{i}")
        ag_pending[i] = (sems, srcs, lands)
    norm_g_first = _after(norm_g[0], dep)

    def as_operands(gath):
        return dict(w_in=gath[0], w_a=gath[1].reshape(d, d), w_glu=gath[2].reshape(ws, ws),
                    w_b=gath[3], w_o=gath[4].reshape(d, d))

    saved = []
    wg = [None] * depth
    for i in range(depth):
        xi = xs[-1]
        if i in ag_pending:
            sems, srcs, lands = ag_pending[i]
            gathered[i] = _exchange_wait(sems, srcs, lands, xi, True, f"ag_wait_{i}")
        wg[i] = g = as_operands(gathered[i])
        conv8 = jnp.pad(conv_full[i], ((0, SUBLANES - 3), (0, 0)))
        h = _rmsnorm_fwd(xi, norm_g_first if i == 0 else norm_g[i], f"rmsnorm_fwd_{i}")
        proj = _mm_win_fwd(h, g["w_in"], f"mm_proj_{i}")
        pa = _branch_a_fwd(proj, conv8, d, f"branch_a_fwd_{i}")
        ya = _mm(pa, g["w_a"], name=f"mm_ya_{i}")
        shape3 = (n_groups, n_state, S5_GROUP)
        dense_in = (_dense(a_re[i][:, :, None], shape3), _dense(a_im[i][:, :, None], shape3),
                    _dense(log_dt[i][:, None, None], shape3), b_re[i].reshape(-1, LANES), b_im[i].reshape(-1, LANES))
        lbr, lbi, bbr, bbi = _s5_params(*dense_in, f"s5_params_{i}")
        lam_re = lbr.reshape(shape3)[:, :, 0].reshape(1, -1)
        lam_im = lbi.reshape(shape3)[:, :, 0].reshape(1, -1)
        bbr3, bbi3 = bbr.reshape(shape3), bbi.reshape(shape3)
        u_seg = _to_segments(proj[:, 4 * d:4 * d + ws])
        bu_re, bu_im = _s5_up(u_seg, _block_diag(bbr3.transpose(0, 2, 1), nb),
                              _block_diag(bbi3.transpose(0, 2, 1), nb), f"s5_bu_{i}")
        h_re, h_im = _s5_scan(bu_re, bu_im, lam_re, lam_im, False, f"s5_scan_fwd_{i}")
        y_seg = _s5_down(h_re, h_im, _block_diag(c_re[i].transpose(0, 2, 1), nb),
                         _block_diag(c_im[i].transpose(0, 2, 1), nb), u_seg,
                         d_skip[i].reshape(1, ws), f"s5_y_{i}")
        y = _from_segments(y_seg)
        yg = _gelu_cast(y, f"gelu_{i}")
        gl = _mm(yg, g["w_glu"], name=f"mm_glu_{i}")
        pb = _glu_post(y, gl, proj, b_glu[i], zb_col, f"glu_post_{i}")
        w_b2d = g["w_b"].transpose(1, 0, 2).reshape(ws, d)
        yb = _mm(pb, w_b2d, name=f"mm_yb_{i}")
        mrg = _merge_fwd(proj, ya, yb, d, ga_col, gb_col, f"merge_fwd_{i}")
        xs.append(_mm(mrg, g["w_o"], name=f"mm_out_{i}", add=xi))
        saved.append(dict(h=h, proj=proj, pa=pa, ya=ya, yb=yb, y=y, yg=yg, gl=gl, pb=pb, mrg=mrg,
                          u_seg=u_seg, h_re=h_re, h_im=h_im, lam_re=lam_re, lam_im=lam_im,
                          bbr3=bbr3, bbi3=bbi3, dense_in=dense_in, conv8=conv8, w_b2d=w_b2d))

    dx, g_final, loss_part = _final_loss(xs[-1], final_g, tgt, "final_loss")
    loss = lax.psum(loss_part[0, 0], MESH_AXES)

    rs_pending = []
    small = {k: [None] * depth for k in ("norm_g", "a_re", "a_im", "log_dt", "b_re", "b_im", "c_re", "c_im",
                                         "d_skip", "b_glu", "conv_w")}

    def start_exchange(names_, pieces, layer, tag):
        lands = [_landing(lax.dynamic_index_in_dim(p, me, 0, keepdims=False)) for p in pieces]
        sems, srcs, lands, token = _exchange_start(pieces, lands, False, f"rs_start_{tag}_{layer}")
        rs_pending.append((names_, layer, sems, srcs, lands, f"rs_wait_{tag}_{layer}"))
        return token

    for i in reversed(range(depth)):
        s, g = saved[i], wg[i]
        proj = s["proj"]
        dxo_b = dx.astype(BF16)
        dm = _mm(dxo_b, g["w_o"], name=f"mm_dm_{i}", nt=True)
        gw_o = _mm(s["mrg"].T, dxo_b, name=f"mm_gw_o_{i}", out_dtype=BF16)
        dya, dyb, dga, dgb = _merge_bwd(proj, s["ya"], s["yb"], dm, d, ga_col, gb_col, f"merge_bwd_{i}")
        dpa = _mm(dya, g["w_a"], name=f"mm_dpa_{i}", nt=True)
        gw_a = _mm(s["pa"].T, dya, name=f"mm_gw_a_{i}", out_dtype=BF16)
        dpb = _mm(dyb, s["w_b2d"], name=f"mm_dpb_{i}", nt=True)
        gw_b = _mm(s["pb"].T, dyb, name=f"mm_gw_b_{i}", split_n=N_DEV, out_dtype=BF16)
        dv, dbg, dcg, dza, dw0, dw1, dw2 = _branch_a_bwd(proj, dpa, s["conv8"], d, f"branch_a_bwd_{i}")
        small["conv_w"][i] = jnp.concatenate([dw0, dw1, dw2], axis=0)
        dzb, dgl, t1, db_glu = _glu_bwd1(s["y"], s["gl"], proj, b_glu[i], dpb, zb_col, f"glu_bwd1_{i}")
        small["b_glu"][i] = db_glu.reshape(ws)
        dyg2 = _mm(dgl, g["w_glu"], name=f"mm_dyg_{i}", nt=True)
        gw_glu = _mm(s["yg"].T, dgl, name=f"mm_gw_glu_{i}", out_dtype=BF16)
        token = start_exchange(
            ("w_out_a", "w_glu", "w_out_b", "w_o"),
            [gw_a.reshape(N_DEV, d // N_DEV, d), gw_glu.reshape(N_DEV, ws // N_DEV, ws), gw_b,
             gw_o.reshape(N_DEV, d // N_DEV, d)], i, "small")
        dy = _glu_bwd2(s["y"], t1, dyg2, f"glu_bwd2_{i}")
        dy_seg = _to_segments(dy)
        u_seg = s["u_seg"]
        small["d_skip"][i] = _colsum_prod(dy_seg, u_seg, f"s5_dskip_{i}").reshape(n_groups, S5_GROUP)
        gh_re, gh_im = _s5_up(dy_seg, _block_diag(_after(c_re[i], token), nb), _block_diag(-c_im[i], nb),
                              f"s5_gh_{i}")
        q_re, q_im = _s5_scan(gh_re, gh_im, s["lam_re"], -s["lam_im"], True, f"s5_scan_bwd_{i}")
        du_seg = _s5_down(q_re, q_im, _block_diag(s["bbr3"], nb), _block_diag(-s["bbi3"], nb),
                          dy_seg, d_skip[i].reshape(1, ws), f"s5_du_{i}")
        gc_re, gc_im = _s5_outer(dy_seg.T, s["h_re"], s["h_im"], f"s5_gc_{i}")
        small["c_re"][i] = _block_diag_extract(gc_re, n_groups, S5_GROUP, n_state)
        small["c_im"][i] = -_block_diag_extract(gc_im, n_groups, S5_GROUP, n_state)
        gbb_re, gbb_im = _s5_outer(u_seg.T, q_re, q_im, f"s5_gbb_{i}")
        gbb_re = _block_diag_extract(gbb_re, n_groups, S5_GROUP, n_state).transpose(0, 2, 1)
        gbb_im = _block_diag_extract(gbb_im, n_groups, S5_GROUP, n_state).transpose(0, 2, 1)
        glam_re, glam_im = _s5_dlam(q_re, q_im, s["h_re"], s["h_im"], f"s5_dlam_{i}")
        shape3 = (n_groups, n_state, S5_GROUP)
        gar, gai, gdt, gbr, gbi = _s5_params_bwd(
            *s["dense_in"], _dense(glam_re.reshape(n_groups, n_state, 1), shape3),
            _dense(glam_im.reshape(n_groups, n_state, 1), shape3),
            gbb_re.reshape(-1, LANES), gbb_im.reshape(-1, LANES), n_groups, f"s5_params_bwd_{i}")
        small["a_re"][i] = gar.reshape(shape3)[:, :, 0]
        small["a_im"][i] = gai.reshape(shape3)[:, :, 0]
        small["log_dt"][i] = gdt[:, 0]
        small["b_re"][i] = gbr.reshape(shape3)
        small["b_im"][i] = gbi.reshape(shape3)
        du = _from_segments(du_seg).astype(BF16)
        dproj = jnp.concatenate([dv, dbg, dcg, dza, du, dzb, dga, dgb], axis=1)
        gw_in = _mm(s["h"].T, dproj, name=f"mm_gw_in_{i}", split_n=N_DEV, tm=1024,
                    out_dtype=BF16)
        token = start_exchange(("w_in",), [gw_in], i, "w_in")
        dh = _mm_win_bwd(_after(dproj, token), g["w_in"], f"mm_dh_{i}")
        dx, dng = _rmsnorm_bwd(xs[i], norm_g[i], dh, dx, f"rmsnorm_bwd_{i}")
        small["norm_g"][i] = dng.reshape(d)

    results = {}

    small_names = ("norm_g", "a_re", "a_im", "log_dt", "b_re", "b_im", "c_re", "c_im", "d_skip", "b_glu")
    small_w = dict(norm_g=(norm_g, m_norm_g, v_norm_g), a_re=(a_re, m_a_re, v_a_re), a_im=(a_im, m_a_im, v_a_im),
                   log_dt=(log_dt, m_log_dt, v_log_dt), b_re=(b_re, m_b_re, v_b_re), b_im=(b_im, m_b_im, v_b_im),
                   c_re=(c_re, m_c_re, v_c_re), c_im=(c_im, m_c_im, v_c_im), d_skip=(d_skip, m_d_skip, v_d_skip),
                   b_glu=(b_glu, m_b_glu, v_b_glu), final_g=(final_g, m_final_g, v_final_g))
    order = small_names + ("final_g", "conv_w")
    part = {k: jnp.stack(small[k]) for k in small_names + ("conv_w",)}
    part["final_g"] = g_final.reshape(d)
    shapes = [part[k].shape for k in order]
    gpack = _all_gather(_pack([part[k] for k in order]), "ag_small_grads")
    zeros_conv = jnp.zeros(part["conv_w"].shape, F32)
    wpack = _pack([small_w[k][0] for k in order[:-1]] + [zeros_conv])
    mpack = _pack([small_w[k][1] for k in order[:-1]] + [zeros_conv])
    vpack = _pack([small_w[k][2] for k in order[:-1]] + [zeros_conv])
    sres = [_unpack(p[0], shapes)
            for p in _adamw(wpack[None], mpack[None], vpack[None], gpack, "adamw_small")]
    for j, k in enumerate(order[:-1]):
        results[k] = [sres[q][j] for q in range(4)]
    dc = d // N_DEV
    gconv = lax.dynamic_slice_in_dim(sres[0][-1], me * dc, dc, axis=2)
    pad8 = lambda a: jnp.pad(a.reshape(depth * 3, dc), ((0, SUBLANES - depth * 3), (0, 0)))[None]
    cres = _adamw(pad8(conv_w), pad8(m_conv_w), pad8(v_conv_w), pad8(gconv), "adamw_conv_w")
    results["conv_w"] = [r_[0, :depth * 3].reshape(depth, 3, dc) for r_ in cres]

    after = cres[0]
    for names_, layer, sems, srcs, lands, wait_name in rs_pending:
        slots = _exchange_wait(sems, srcs, lands, after, False, wait_name)
        for k, land in zip(names_, slots):
            w_, m_, v_ = big[k]
            results[k] = _adamw(w_, m_, v_, land, f"adamw_{k}_{layer}", layer=layer, prev=results.get(k))
            after = results[k][0]

    names = ("norm_g", "w_in", "conv_w", "w_out_a", "a_re", "a_im", "log_dt", "b_re", "b_im", "c_re", "c_im",
             "d_skip", "w_glu", "b_glu", "w_out_b", "w_o", "final_g")
    outs = [loss, dx[None]]
    for q in range(4):
        outs += [results[k][q] for k in names]
    return tuple(outs)
```

```python
import functools
import math

import jax
import jax.numpy as jnp
from jax import lax
from jax.experimental import pallas as pl
from jax.experimental.pallas import tpu as pltpu

F32 = jnp.float32
BF16 = jnp.bfloat16
HIGHEST = lax.Precision.HIGHEST

N_DEV = 8
MESH_AXES = ("x", "y", "c")
LANES = 128
SUBLANES = 8
VMEM_LIMIT_BYTES = 56 * 1024 * 1024

RMS_EPS = 1e-6
ADAM_LR = 0.001
ADAM_B1 = 0.9
ADAM_B2 = 0.999
ADAM_EPS = 1e-08
ADAM_WD = 0.01
ADAM_STEP = 10
GELU_C0 = math.sqrt(2.0 / math.pi)
GELU_C1 = 0.044715

ADAMW_BLOCK_ELEMS = 1 << 17
PACK_ROWS = 512

S5_GROUP = 16
S5_GB = LANES // S5_GROUP


def _params(*semantics):
    return pltpu.CompilerParams(dimension_semantics=semantics, vmem_limit_bytes=VMEM_LIMIT_BYTES)


def _tile(n, pref):
    t = min(n, pref)
    while n % t:
        assert t % 2 == 0, (n, pref)
        t //= 2
    return t


def _sigmoid(z):
    return 1.0 / (1.0 + jnp.exp(-z))


def _gelu(y):
    return 0.5 * y * (1.0 + jnp.tanh(GELU_C0 * (y + GELU_C1 * y * y * y)))


def _gelu_grad(y):
    t = jnp.tanh(GELU_C0 * (y + GELU_C1 * y * y * y))
    return 0.5 * (1.0 + t) + 0.5 * y * (1.0 - t * t) * GELU_C0 * (1.0 + 3.0 * GELU_C1 * y * y)


def _all_gather(shard, name):
    r, c_ = shard.shape

    def body(x_ref, out_ref, send_sems, recv_sems, local_sem):
        x, y, c = lax.axis_index("x"), lax.axis_index("y"), lax.axis_index("c")
        me, sibling = (x, y, c), (x, y, 1 - c)
        chips = [(1 - x, y), (x, 1 - y), (1 - x, 1 - y)]

        def slot(px, py, pc):
            return out_ref.at[4 * px + 2 * py + pc]

        def copy(k, block, to, src=None):
            return pltpu.make_async_remote_copy(
                src_ref=slot(*block) if src is None else src, dst_ref=slot(*block),
                send_sem=send_sems.at[k], recv_sem=recv_sems.at[k],
                device_id=to, device_id_type=pl.DeviceIdType.MESH)

        mine = pltpu.make_async_copy(x_ref, slot(*me), local_sem)
        mine.start()
        first = [copy(0, me, sibling, src=x_ref)]
        first += [copy(1 + j, me, (*chip, c), src=x_ref) for j, chip in enumerate(chips)]
        for cp in first:
            cp.start()
        passed = [copy(4 + j, (*chip, c), sibling) for j, chip in enumerate(chips)]
        for j, chip in enumerate(chips):
            copy(1 + j, (*chip, c), me).wait_recv()
            passed[j].start()
        copy(0, sibling, me).wait_recv()
        for j, chip in enumerate(chips):
            copy(4 + j, (*chip, 1 - c), me).wait_recv()
        for cp in first + passed:
            cp.wait_send()
        mine.wait()

    return pl.pallas_call(
        body, name=name,
        out_shape=jax.ShapeDtypeStruct((N_DEV, r, c_), shard.dtype),
        in_specs=[pl.BlockSpec(memory_space=pl.ANY)],
        out_specs=pl.BlockSpec(memory_space=pl.ANY),
        scratch_shapes=[pltpu.SemaphoreType.DMA((7,)), pltpu.SemaphoreType.DMA((7,)),
                        pltpu.SemaphoreType.DMA],
    )(shard)


HBM_SPEC = pl.BlockSpec(memory_space=pltpu.HBM)
SEM_SPEC = pl.BlockSpec(memory_space=pltpu.SEMAPHORE)
DATAFLOW_EFFECT = pltpu.SideEffectType.DATAFLOW_SIDE_EFFECTING
OTHER_CHIPS = (2, 4, 6)


def _flip(pos, mask):
    x, y, c = pos
    return x ^ ((mask >> 2) & 1), y ^ ((mask >> 1) & 1), c ^ (mask & 1)


def _dev(pos):
    return 4 * pos[0] + 2 * pos[1] + pos[2]


def _chip(pos):
    return 2 * pos[0] + pos[1]


def _plan_gather_chips(me):
    return [(_flip(me, k), None, _dev(me), _dev(_flip(me, k))) for k in (1,) + OTHER_CHIPS]


def _plan_gather_forward(me):
    sib = _flip(me, 1)
    return [(sib, _dev(_flip(me, k)), _dev(_flip(me, k)), _dev(_flip(sib, k))) for k in OTHER_CHIPS]


def _plan_reduce_sibling(me):
    sib = _flip(me, 1)
    return [(sib, 2 * q + sib[2], q, q) for q in range(4)]


def _plan_reduce_chips(me):
    return [(_flip(me, k), _chip(_flip(me, k)), _chip(me), _chip(_flip(me, k))) for k in OTHER_CHIPS]


PLAN_COPIES = {_plan_gather_chips: 4, _plan_gather_forward: 3, _plan_reduce_sibling: 4, _plan_reduce_chips: 3}


def _exchange_copies(plan, src_refs, land_refs, send_sems, recv_sems):
    me = (lax.axis_index("x"), lax.axis_index("y"), lax.axis_index("c"))
    pairs = []
    for b, (src_ref, land_ref) in enumerate(zip(src_refs, land_refs)):
        for j, (peer, src_slot, there, here) in enumerate(plan(me)):
            sem = b * PLAN_COPIES[plan] + j
            src = src_ref if src_slot is None else src_ref.at[src_slot]
            out = pltpu.make_async_remote_copy(
                src_ref=src, dst_ref=land_ref.at[there], send_sem=send_sems.at[sem], recv_sem=recv_sems.at[sem],
                device_id=peer, device_id_type=pl.DeviceIdType.MESH)
            inc = pltpu.make_async_remote_copy(
                src_ref=src, dst_ref=land_ref.at[here], send_sem=send_sems.at[sem], recv_sem=recv_sems.at[sem],
                device_id=peer, device_id_type=pl.DeviceIdType.MESH)
            pairs.append((out, inc))
    return pairs


def _exchange_start(plan, srcs, lands, name):
    srcs = [] if srcs is None else list(srcs)
    ns, n = len(srcs), len(lands)

    def body(*refs):
        land_refs = refs[ns:ns + n]
        pairs = _exchange_copies(plan, refs[:ns] if ns else land_refs, land_refs, refs[ns + n], refs[ns + n + 1])
        for out, _ in pairs:
            out.start()
        token = refs[-1]
        token[...] = jnp.zeros_like(token)

    sems = pltpu.SemaphoreType.DMA((PLAN_COPIES[plan] * n,))
    bufs = srcs + list(lands)
    outs = pl.pallas_call(
        body, name=name,
        out_shape=(sems, sems, *[pltpu.HBM(a.shape, a.dtype) for a in bufs],
                   jax.ShapeDtypeStruct((SUBLANES, LANES), F32)),
        in_specs=[HBM_SPEC] * (ns + n),
        out_specs=(SEM_SPEC, SEM_SPEC, *[HBM_SPEC] * (ns + n), pl.BlockSpec(memory_space=pltpu.VMEM)),
        input_output_aliases={i: 2 + i for i in range(ns + n)},
        compiler_params=pltpu.CompilerParams(has_side_effects=DATAFLOW_EFFECT),
    )(*[pltpu.with_memory_space_constraint(a, pltpu.HBM) for a in bufs])
    return (outs[0], outs[1]), (outs[2:2 + ns] if ns else None), outs[2 + ns:2 + ns + n], outs[-1]


def _exchange_wait(plan, sems, srcs, lands, after, name):
    srcs = [] if srcs is None else list(srcs)
    ns, n = len(srcs), len(lands)

    def body(*refs):
        land_refs = refs[ns:ns + n]
        pairs = _exchange_copies(plan, refs[:ns] if ns else land_refs, land_refs, refs[ns + n], refs[ns + n + 1])
        for out, inc in pairs:
            out.wait_send()
            inc.wait_recv()

    bufs = srcs + list(lands)
    outs = pl.pallas_call(
        body, name=name,
        out_shape=[pltpu.HBM(a.shape, a.dtype) for a in bufs],
        in_specs=[HBM_SPEC] * (ns + n) + [SEM_SPEC, SEM_SPEC, pl.BlockSpec(memory_space=pl.ANY)],
        out_specs=[HBM_SPEC] * (ns + n),
        input_output_aliases={i: i for i in range(ns + n)},
        compiler_params=pltpu.CompilerParams(has_side_effects=DATAFLOW_EFFECT),
    )(*bufs, sems[0], sems[1], after)
    return outs[:ns], outs[ns:]


def _landing(own, slots, slot):
    land = lax.empty((slots,) + own.shape, own.dtype)
    return lax.dynamic_update_slice(land, own[None], (slot,) + (0,) * own.ndim)


def _chip_sums(pieces, land, name):
    _, r, c_ = land.shape
    tr = _tile(r, max(2 * SUBLANES, 1 << int(math.log2(ADAMW_BLOCK_ELEMS // c_))))

    def body(core_ref, p_ref, l_ref, o_ref):
        o_ref[...] = (p_ref[...].astype(F32) + l_ref[...].astype(F32)).astype(o_ref.dtype)

    spec = pl.BlockSpec((None, tr, c_), lambda q, i, core: (q, i, 0))
    return pl.pallas_call(
        body, name=name,
        grid_spec=pltpu.PrefetchScalarGridSpec(
            num_scalar_prefetch=1, grid=(4, r // tr),
            in_specs=[pl.BlockSpec((None, tr, c_), lambda q, i, core: (2 * q + core[0], i, 0)), spec],
            out_specs=spec),
        out_shape=jax.ShapeDtypeStruct(land.shape, land.dtype),
        compiler_params=_params("parallel", "parallel"),
    )(lax.axis_index("c").reshape(1), pieces, land)


def _after(a, dep):
    return lax.optimization_barrier((a, dep))[0]


def _mm(a, b, *, name, nt=False, out_dtype=F32, add=None, split_n=None, tm=512, tn=1024):
    m, k = a.shape
    n = b.shape[0] if nt else b.shape[1]
    tm = _tile(m, tm)
    tn = n // split_n if split_n else _tile(n, tn)
    dims = (((1,), (1,)), ((), ())) if nt else (((1,), (0,)), ((), ()))

    def body(*refs):
        a_ref, b_ref = refs[0], refs[1]
        o_ref = refs[-1]
        acc = lax.dot_general(a_ref[...], b_ref[...], dims, preferred_element_type=F32)
        if add is not None:
            acc = acc + refs[2][...]
        o_ref[...] = acc.astype(o_ref.dtype)

    in_specs = [pl.BlockSpec((tm, k), lambda i, j: (i, 0)),
                pl.BlockSpec((tn, k), lambda i, j: (j, 0)) if nt
                else pl.BlockSpec((k, tn), lambda i, j: (0, j))]
    args = [a, b]
    if add is not None:
        in_specs.append(pl.BlockSpec((tm, tn), lambda i, j: (i, j)))
        args.append(add)
    if split_n:
        out_shape = jax.ShapeDtypeStruct((split_n, m, tn), out_dtype)
        out_spec = pl.BlockSpec((None, tm, tn), lambda i, j: (j, i, 0))
    else:
        out_shape = jax.ShapeDtypeStruct((m, n), out_dtype)
        out_spec = pl.BlockSpec((tm, tn), lambda i, j: (i, j))
    return pl.pallas_call(
        body, name=name, grid=(m // tm, n // tn), in_specs=in_specs, out_specs=out_spec,
        out_shape=out_shape, compiler_params=_params("parallel", "parallel"),
    )(*args)


def _mm_win_fwd(h, w_g, name):
    m, k = h.shape
    nj = w_g.shape[2]
    tm = _tile(m, 512)

    def body(a_ref, b_ref, o_ref):
        o_ref[...] = jnp.dot(a_ref[...], b_ref[...], preferred_element_type=F32)

    return pl.pallas_call(
        body, name=name, grid=(N_DEV, m // tm),
        in_specs=[pl.BlockSpec((tm, k), lambda j, i: (i, 0)),
                  pl.BlockSpec((None, k, nj), lambda j, i: (j, 0, 0))],
        out_specs=pl.BlockSpec((tm, nj), lambda j, i: (i, j)),
        out_shape=jax.ShapeDtypeStruct((m, N_DEV * nj), F32),
        compiler_params=_params("parallel", "parallel"),
    )(h, w_g)


def _mm_win_bwd(dproj, w_g, name):
    m = dproj.shape[0]
    d, nj = w_g.shape[1], w_g.shape[2]
    tm = _tile(m, 512)
    tn = _tile(d, 1024)

    def body(a_ref, b_ref, o_ref, acc_ref):
        j = pl.program_id(2)

        @pl.when(j == 0)
        def _():
            acc_ref[...] = jnp.zeros_like(acc_ref)

        acc_ref[...] += lax.dot_general(a_ref[...], b_ref[...], (((1,), (1,)), ((), ())),
                                        preferred_element_type=F32)

        @pl.when(j == N_DEV - 1)
        def _():
            o_ref[...] = acc_ref[...]

    return pl.pallas_call(
        body, name=name, grid=(m // tm, d // tn, N_DEV),
        in_specs=[pl.BlockSpec((tm, nj), lambda i, n, j: (i, j)),
                  pl.BlockSpec((None, tn, nj), lambda i, n, j: (j, n, 0))],
        out_specs=pl.BlockSpec((tm, tn), lambda i, n, j: (i, n)),
        out_shape=jax.ShapeDtypeStruct((m, d), F32),
        scratch_shapes=[pltpu.VMEM((tm, tn), F32)],
        compiler_params=_params("parallel", "parallel", "arbitrary"),
    )(dproj, w_g)


def _row_spec(tr, w, col):
    return pl.BlockSpec((tr, w), lambda i: (i, col))


def _full_spec(shape):
    return pl.BlockSpec(shape, lambda i: (0,) * len(shape))


def _rmsnorm_fwd(x, g, name):
    l, d = x.shape
    tr = _tile(l, 256)

    def body(x_ref, g_ref, o_ref):
        xv = x_ref[...]
        rstd = lax.rsqrt(jnp.mean(xv * xv, axis=-1, keepdims=True) + RMS_EPS)
        o_ref[...] = (xv * rstd * g_ref[...]).astype(o_ref.dtype)

    return pl.pallas_call(
        body, name=name, grid=(l // tr,),
        in_specs=[_row_spec(tr, d, 0), _full_spec((1, d))],
        out_specs=_row_spec(tr, d, 0),
        out_shape=jax.ShapeDtypeStruct((l, d), BF16),
        compiler_params=_params("parallel"),
    )(x, g.reshape(1, d))


def _rmsnorm_bwd(x, g, dh, dxo, name):
    l, d = x.shape
    tr = _tile(l, 256)

    def body(x_ref, g_ref, dh_ref, dxo_ref, dx_ref, dg_ref):
        xv = x_ref[...]
        rstd = lax.rsqrt(jnp.mean(xv * xv, axis=-1, keepdims=True) + RMS_EPS)
        dhv = dh_ref[...]
        gdy = dhv * g_ref[...]
        dot = jnp.mean(gdy * xv, axis=-1, keepdims=True)
        dx_ref[...] = dxo_ref[...] + rstd * gdy - xv * (rstd * rstd * rstd * dot)

        @pl.when(pl.program_id(0) == 0)
        def _():
            dg_ref[...] = jnp.zeros_like(dg_ref)

        dg_ref[...] += jnp.sum(dhv * xv * rstd, axis=0, keepdims=True)

    return pl.pallas_call(
        body, name=name, grid=(l // tr,),
        in_specs=[_row_spec(tr, d, 0), _full_spec((1, d)), _row_spec(tr, d, 0), _row_spec(tr, d, 0)],
        out_specs=[_row_spec(tr, d, 0), _full_spec((1, d))],
        out_shape=[jax.ShapeDtypeStruct((l, d), F32), jax.ShapeDtypeStruct((1, d), F32)],
        compiler_params=_params("arbitrary"),
    )(x, g.reshape(1, d), dh, dxo)


def _final_loss(x, g, tgt, name):
    l, d = x.shape
    tr = _tile(l, 256)

    def body(x_ref, g_ref, t_ref, dx_ref, dg_ref, loss_ref):
        xv = x_ref[...]
        gv = g_ref[...]
        rstd = lax.rsqrt(jnp.mean(xv * xv, axis=-1, keepdims=True) + RMS_EPS)
        xn = xv * rstd
        err = xn * gv - t_ref[...]
        dy = err * (1.0 / d)
        gdy = dy * gv
        dot = jnp.mean(gdy * xv, axis=-1, keepdims=True)
        dx_ref[...] = rstd * gdy - xv * (rstd * rstd * rstd * dot)

        @pl.when(pl.program_id(0) == 0)
        def _():
            dg_ref[...] = jnp.zeros_like(dg_ref)
            loss_ref[...] = jnp.zeros_like(loss_ref)

        dg_ref[...] += jnp.sum(dy * xn, axis=0, keepdims=True)
        loss_ref[...] += (0.5 / d) * jnp.sum(err * err)

    return pl.pallas_call(
        body, name=name, grid=(l // tr,),
        in_specs=[_row_spec(tr, d, 0), _full_spec((1, d)), _row_spec(tr, d, 0)],
        out_specs=[_row_spec(tr, d, 0), _full_spec((1, d)), _full_spec((SUBLANES, LANES))],
        out_shape=[jax.ShapeDtypeStruct((l, d), F32), jax.ShapeDtypeStruct((1, d), F32),
                   jax.ShapeDtypeStruct((SUBLANES, LANES), F32)],
        compiler_params=_params("arbitrary"),
    )(x, g.reshape(1, d), tgt)


def _halo_spec(tr, w, col, nblk8, before):
    step = tr // SUBLANES
    if before:
        return pl.BlockSpec((SUBLANES, w), lambda i: (jnp.maximum(i * step - 1, 0), col))
    return pl.BlockSpec((SUBLANES, w), lambda i: (jnp.minimum((i + 1) * step, nblk8 - 1), col))


def _shift_down(cur, before, k):
    ext = jnp.concatenate([before, cur], axis=0)
    return pltpu.roll(ext, k, axis=0)[SUBLANES:, :]


def _shift_up(cur, after, k):
    tr = cur.shape[0]
    ext = jnp.concatenate([cur, after], axis=0)
    return pltpu.roll(ext, tr + SUBLANES - k, axis=0)[:tr, :]


def _branch_a_fwd(proj, conv_w, d, name):
    l = proj.shape[0]
    tr = _tile(l, 256)
    nblk8 = l // SUBLANES

    def body(v_ref, bg_ref, cg_ref, za_ref, vh_ref, cgh_ref, w_ref, o_ref):
        first = pl.program_id(0) == 0
        cv = cg_ref[...] * v_ref[...]
        cvh = jnp.where(first, 0.0, cgh_ref[...] * vh_ref[...])
        w0, w1, w2 = w_ref[0:1, :], w_ref[1:2, :], w_ref[2:3, :]
        q = w2 * cv + w1 * _shift_down(cv, cvh, 1) + w0 * _shift_down(cv, cvh, 2)
        za = za_ref[...]
        o_ref[...] = (bg_ref[...] * q * (za * _sigmoid(za))).astype(o_ref.dtype)

    return pl.pallas_call(
        body, name=name, grid=(l // tr,),
        in_specs=[_row_spec(tr, d, 0), _row_spec(tr, d, 1), _row_spec(tr, d, 2), _row_spec(tr, d, 3),
                  _halo_spec(tr, d, 0, nblk8, True), _halo_spec(tr, d, 2, nblk8, True),
                  _full_spec((SUBLANES, d))],
        out_specs=_row_spec(tr, d, 0),
        out_shape=jax.ShapeDtypeStruct((l, d), BF16),
        compiler_params=_params("parallel"),
    )(proj, proj, proj, proj, proj, proj, conv_w)


def _branch_a_bwd(proj, dpa, conv_w, d, name):
    l = proj.shape[0]
    tr = _tile(l, 128)
    nblk8 = l // SUBLANES
    ntiles = l // tr

    def body(v_ref, bg_ref, cg_ref, za_ref, dpa_ref, vh_ref, cgh_ref, bgn_ref, zan_ref, dpan_ref,
             w_ref, dv_ref, dbg_ref, dcg_ref, dza_ref, dw0_ref, dw1_ref, dw2_ref):
        i = pl.program_id(0)
        v, bg, cg, za, dpa_v = v_ref[...], bg_ref[...], cg_ref[...], za_ref[...], dpa_ref[...]
        w0, w1, w2 = w_ref[0:1, :], w_ref[1:2, :], w_ref[2:3, :]
        cv = cg * v
        cvh = jnp.where(i == 0, 0.0, cgh_ref[...] * vh_ref[...])
        cv1 = _shift_down(cv, cvh, 1)
        cv2 = _shift_down(cv, cvh, 2)
        q = w2 * cv + w1 * cv1 + w0 * cv2
        sg = _sigmoid(za)
        s = za * sg
        dbg_ref[...] = (dpa_v * q * s).astype(dbg_ref.dtype)
        dza_ref[...] = (dpa_v * bg * q * (sg * (1.0 + za * (1.0 - sg)))).astype(dza_ref.dtype)
        dq = dpa_v * bg * s
        zan = zan_ref[...]
        dqn = jnp.where(i == ntiles - 1, 0.0, dpan_ref[...] * bgn_ref[...] * (zan * _sigmoid(zan)))
        dcv = w2 * dq + w1 * _shift_up(dq, dqn, 1) + w0 * _shift_up(dq, dqn, 2)
        dcg_ref[...] = (dcv * v).astype(dcg_ref.dtype)
        dv_ref[...] = (dcv * cg).astype(dv_ref.dtype)

        @pl.when(i == 0)
        def _():
            dw0_ref[...] = jnp.zeros_like(dw0_ref)
            dw1_ref[...] = jnp.zeros_like(dw1_ref)
            dw2_ref[...] = jnp.zeros_like(dw2_ref)

        dw0_ref[...] += jnp.sum(dq * cv2, axis=0, keepdims=True)
        dw1_ref[...] += jnp.sum(dq * cv1, axis=0, keepdims=True)
        dw2_ref[...] += jnp.sum(dq * cv, axis=0, keepdims=True)

    act = jax.ShapeDtypeStruct((l, d), BF16)
    wsum = jax.ShapeDtypeStruct((1, d), F32)
    return pl.pallas_call(
        body, name=name, grid=(ntiles,),
        in_specs=[_row_spec(tr, d, 0), _row_spec(tr, d, 1), _row_spec(tr, d, 2), _row_spec(tr, d, 3),
                  _row_spec(tr, d, 0),
                  _halo_spec(tr, d, 0, nblk8, True), _halo_spec(tr, d, 2, nblk8, True),
                  _halo_spec(tr, d, 1, nblk8, False), _halo_spec(tr, d, 3, nblk8, False),
                  _halo_spec(tr, d, 0, nblk8, False),
                  _full_spec((SUBLANES, d))],
        out_specs=[_row_spec(tr, d, 0)] * 4 + [_full_spec((1, d))] * 3,
        out_shape=[act] * 4 + [wsum] * 3,
        compiler_params=_params("arbitrary"),
    )(proj, proj, proj, proj, dpa, proj, proj, proj, proj, dpa, conv_w)


def _gelu_cast(y, name):
    l, w = y.shape
    tr = _tile(l, 512)

    def body(y_ref, o_ref):
        o_ref[...] = _gelu(y_ref[...]).astype(o_ref.dtype)

    return pl.pallas_call(
        body, name=name, grid=(l // tr,), in_specs=[_row_spec(tr, w, 0)],
        out_specs=_row_spec(tr, w, 0), out_shape=jax.ShapeDtypeStruct((l, w), BF16),
        compiler_params=_params("parallel"),
    )(y)


def _glu_post(y, gl, proj, b_glu, zb_col, name):
    l, w = y.shape
    tr = _tile(l, 512)

    def body(y_ref, gl_ref, zb_ref, b_ref, o_ref):
        zb = zb_ref[...]
        o_ref[...] = (_gelu(y_ref[...]) * _sigmoid(gl_ref[...] + b_ref[...])
                      * (zb * _sigmoid(zb))).astype(o_ref.dtype)

    return pl.pallas_call(
        body, name=name, grid=(l // tr,),
        in_specs=[_row_spec(tr, w, 0), _row_spec(tr, w, 0), _row_spec(tr, w, zb_col), _full_spec((1, w))],
        out_specs=_row_spec(tr, w, 0), out_shape=jax.ShapeDtypeStruct((l, w), BF16),
        compiler_params=_params("parallel"),
    )(y, gl, proj, b_glu.reshape(1, w))


def _glu_bwd1(y, gl, proj, b_glu, dpb, zb_col, name):
    l, w = y.shape
    tr = _tile(l, 512)

    def body(y_ref, gl_ref, zb_ref, b_ref, dpb_ref, dzb_ref, dgl_ref, t_ref, db_ref):
        zb = zb_ref[...]
        dpb_v = dpb_ref[...]
        yg = _gelu(y_ref[...])
        sgl = _sigmoid(gl_ref[...] + b_ref[...])
        szb = _sigmoid(zb)
        dzb_ref[...] = (dpb_v * yg * sgl * (szb * (1.0 + zb * (1.0 - szb)))).astype(dzb_ref.dtype)
        e = dpb_v * (zb * szb)
        dgl = e * yg * sgl * (1.0 - sgl)
        dgl_ref[...] = dgl.astype(dgl_ref.dtype)
        t_ref[...] = e * sgl

        @pl.when(pl.program_id(0) == 0)
        def _():
            db_ref[...] = jnp.zeros_like(db_ref)

        db_ref[...] += jnp.sum(dgl, axis=0, keepdims=True)

    return pl.pallas_call(
        body, name=name, grid=(l // tr,),
        in_specs=[_row_spec(tr, w, 0), _row_spec(tr, w, 0), _row_spec(tr, w, zb_col), _full_spec((1, w)),
                  _row_spec(tr, w, 0)],
        out_specs=[_row_spec(tr, w, 0)] * 3 + [_full_spec((1, w))],
        out_shape=[jax.ShapeDtypeStruct((l, w), BF16), jax.ShapeDtypeStruct((l, w), BF16),
                   jax.ShapeDtypeStruct((l, w), F32), jax.ShapeDtypeStruct((1, w), F32)],
        compiler_params=_params("arbitrary"),
    )(y, gl, proj, b_glu.reshape(1, w), dpb)


def _glu_bwd2(y, t1, dyg2, name):
    l, w = y.shape
    tr = _tile(l, 512)

    def body(y_ref, t_ref, d_ref, o_ref):
        o_ref[...] = (t_ref[...] + d_ref[...]) * _gelu_grad(y_ref[...])

    return pl.pallas_call(
        body, name=name, grid=(l // tr,), in_specs=[_row_spec(tr, w, 0)] * 3,
        out_specs=_row_spec(tr, w, 0), out_shape=jax.ShapeDtypeStruct((l, w), F32),
        compiler_params=_params("parallel"),
    )(y, t1, dyg2)


def _merge_fwd(proj, ya, yb, d, ga_col, gb_col, name):
    l = proj.shape[0]
    tr = _tile(l, 256)

    def body(ga_ref, gb_ref, ya_ref, yb_ref, o_ref):
        o_ref[...] = (_sigmoid(ga_ref[...]) * ya_ref[...]
                      + _sigmoid(gb_ref[...]) * yb_ref[...]).astype(o_ref.dtype)

    return pl.pallas_call(
        body, name=name, grid=(l // tr,),
        in_specs=[_row_spec(tr, d, ga_col), _row_spec(tr, d, gb_col), _row_spec(tr, d, 0), _row_spec(tr, d, 0)],
        out_specs=_row_spec(tr, d, 0), out_shape=jax.ShapeDtypeStruct((l, d), BF16),
        compiler_params=_params("parallel"),
    )(proj, proj, ya, yb)


def _merge_bwd(proj, ya, yb, dm, d, ga_col, gb_col, name):
    l = proj.shape[0]
    tr = _tile(l, 256)

    def body(ga_ref, gb_ref, ya_ref, yb_ref, dm_ref, dya_ref, dyb_ref, dga_ref, dgb_ref):
        dmv = dm_ref[...]
        sa = _sigmoid(ga_ref[...])
        sb = _sigmoid(gb_ref[...])
        dya_ref[...] = (dmv * sa).astype(dya_ref.dtype)
        dyb_ref[...] = (dmv * sb).astype(dyb_ref.dtype)
        dga_ref[...] = (dmv * ya_ref[...] * sa * (1.0 - sa)).astype(dga_ref.dtype)
        dgb_ref[...] = (dmv * yb_ref[...] * sb * (1.0 - sb)).astype(dgb_ref.dtype)

    act = jax.ShapeDtypeStruct((l, d), BF16)
    return pl.pallas_call(
        body, name=name, grid=(l // tr,),
        in_specs=[_row_spec(tr, d, ga_col), _row_spec(tr, d, gb_col), _row_spec(tr, d, 0), _row_spec(tr, d, 0),
                  _row_spec(tr, d, 0)],
        out_specs=[_row_spec(tr, d, 0)] * 4, out_shape=[act] * 4,
        compiler_params=_params("parallel"),
    )(proj, proj, ya, yb, dm)


def _to_segments(a):
    l, w = a.shape
    return a.reshape(SUBLANES, l // SUBLANES, w).transpose(1, 0, 2).reshape(l, w)


def _from_segments(a):
    l, w = a.shape
    return a.reshape(l // SUBLANES, SUBLANES, w).transpose(1, 0, 2).reshape(l, w)


def _dense(z, shape):
    return jnp.broadcast_to(z, shape).reshape(-1, LANES)


def _s5_disc(are, aim, ldt):
    dt = jnp.exp(ldt)
    er = jnp.exp(are * dt)
    lbr = er * jnp.cos(aim * dt)
    lbi = er * jnp.sin(aim * dt)
    inv = 1.0 / (are * are + aim * aim)
    fr = ((lbr - 1.0) * are + lbi * aim) * inv
    fi = (lbi * are - (lbr - 1.0) * aim) * inv
    return dt, lbr, lbi, inv, fr, fi


def _s5_params(are, aim, ldt, bre, bim, name):
    shape = are.shape

    def body(are_ref, aim_ref, ldt_ref, bre_ref, bim_ref, lbr_ref, lbi_ref, bbr_ref, bbi_ref):
        _, lbr, lbi, _, fr, fi = _s5_disc(are_ref[...], aim_ref[...], ldt_ref[...])
        lbr_ref[...] = lbr
        lbi_ref[...] = lbi
        bbr_ref[...] = fr * bre_ref[...] - fi * bim_ref[...]
        bbi_ref[...] = fr * bim_ref[...] + fi * bre_ref[...]

    out = jax.ShapeDtypeStruct(shape, F32)
    return pl.pallas_call(body, name=name, out_shape=[out] * 4,
                          compiler_params=pltpu.CompilerParams(vmem_limit_bytes=VMEM_LIMIT_BYTES),
                          )(are, aim, ldt, bre, bim)


def _s5_params_bwd(are, aim, ldt, bre, bim, glbr, glbi, gbbr, gbbi, n_groups, name):
    shape = are.shape
    rows_per_group = shape[0] // n_groups

    def body(are_ref, aim_ref, ldt_ref, bre_ref, bim_ref, glbr_ref, glbi_ref, gbbr_ref, gbbi_ref,
             gar_ref, gai_ref, gdt_ref, gbr_ref, gbi_ref):
        are_v, aim_v = are_ref[...], aim_ref[...]
        bre_v, bim_v = bre_ref[...], bim_ref[...]
        gbbr_v, gbbi_v = gbbr_ref[...], gbbi_ref[...]
        dt, lbr, lbi, inv, fr, fi = _s5_disc(are_v, aim_v, ldt_ref[...])
        gbr_ref[...] = fr * gbbr_v + fi * gbbi_v
        gbi_ref[...] = fr * gbbi_v - fi * gbbr_v
        lane_group = lax.broadcasted_iota(jnp.int32, (LANES, LANES), 0) // S5_GROUP
        same_group = (lane_group == lax.broadcasted_iota(jnp.int32, (LANES, LANES), 1) // S5_GROUP)
        ones = same_group.astype(F32)
        gfr = jnp.dot(bre_v * gbbr_v + bim_v * gbbi_v, ones, precision=HIGHEST, preferred_element_type=F32)
        gfi = jnp.dot(bre_v * gbbi_v - bim_v * gbbr_v, ones, precision=HIGHEST, preferred_element_type=F32)
        glr = glbr_ref[...] + (are_v * gfr - aim_v * gfi) * inv
        gli = glbi_ref[...] + (are_v * gfi + aim_v * gfr) * inv
        qr = (fr * are_v + fi * aim_v) * inv
        qi = (fi * are_v - fr * aim_v) * inv
        gzr = lbr * glr + lbi * gli
        gzi = lbr * gli - lbi * glr
        gar_ref[...] = dt * gzr - (qr * gfr + qi * gfi)
        gai_ref[...] = dt * gzi - (qr * gfi - qi * gfr)
        e = dt * (are_v * gzr + aim_v * gzi)
        per_group = jnp.sum(e.reshape(n_groups, rows_per_group, LANES), axis=1)
        total = jnp.sum(per_group, axis=1, keepdims=True) * (1.0 / S5_GROUP)
        gdt_ref[...] = jnp.broadcast_to(total, gdt_ref.shape)

    out = jax.ShapeDtypeStruct(shape, F32)
    return pl.pallas_call(
        body, name=name,
        out_shape=[out, out, jax.ShapeDtypeStruct((n_groups, LANES), F32), out, out],
        compiler_params=pltpu.CompilerParams(vmem_limit_bytes=VMEM_LIMIT_BYTES),
    )(are, aim, ldt, bre, bim, glbr, glbi, gbbr, gbbi)


def _s5_up(xs, m_re, m_im, name):
    l = xs.shape[0]
    nb, kin, kout = m_re.shape
    tl = _tile(l, 512)

    def body(x_ref, mr_ref, mi_ref, or_ref, oi_ref):
        xv = x_ref[...].astype(BF16)
        or_ref[...] = jnp.dot(xv, mr_ref[...].astype(BF16), preferred_element_type=F32)
        oi_ref[...] = jnp.dot(xv, mi_ref[...].astype(BF16), preferred_element_type=F32)

    out = jax.ShapeDtypeStruct((l, nb * kout), F32)
    return pl.pallas_call(
        body, name=name, grid=(l // tl, nb),
        in_specs=[pl.BlockSpec((tl, kin), lambda i, b: (i, b)),
                  pl.BlockSpec((None, kin, kout), lambda i, b: (b, 0, 0)),
                  pl.BlockSpec((None, kin, kout), lambda i, b: (b, 0, 0))],
        out_specs=[pl.BlockSpec((tl, kout), lambda i, b: (i, b))] * 2,
        out_shape=[out, out], compiler_params=_params("parallel", "parallel"),
    )(xs, m_re, m_im)


def _s5_down(a_re, a_im, m1, m2, xs, dvec, name):
    l = xs.shape[0]
    nb, kin, kout = m1.shape
    tl = _tile(l, 512)

    def body(ar_ref, ai_ref, m1_ref, m2_ref, x_ref, d_ref, o_ref):
        o_ref[...] = (jnp.dot(ar_ref[...].astype(BF16), m1_ref[...].astype(BF16), preferred_element_type=F32)
                      - jnp.dot(ai_ref[...].astype(BF16), m2_ref[...].astype(BF16), preferred_element_type=F32)
                      + d_ref[...] * x_ref[...])

    return pl.pallas_call(
        body, name=name, grid=(l // tl, nb),
        in_specs=[pl.BlockSpec((tl, kin), lambda i, b: (i, b)), pl.BlockSpec((tl, kin), lambda i, b: (i, b)),
                  pl.BlockSpec((None, kin, kout), lambda i, b: (b, 0, 0)),
                  pl.BlockSpec((None, kin, kout), lambda i, b: (b, 0, 0)),
                  pl.BlockSpec((tl, kout), lambda i, b: (i, b)),
                  pl.BlockSpec((1, kout), lambda i, b: (0, b))],
        out_specs=pl.BlockSpec((tl, kout), lambda i, b: (i, b)),
        out_shape=jax.ShapeDtypeStruct((l, nb * kout), F32),
        compiler_params=_params("parallel", "parallel"),
    )(a_re, a_im, m1, m2, xs, dvec)


def _s5_outer(xt, h_re, h_im, name):
    l = xt.shape[1]
    nb = xt.shape[0] // LANES
    kout = h_re.shape[1] // nb
    tl = _tile(l, 512)

    def body(x_ref, hr_ref, hi_ref, or_ref, oi_ref):
        @pl.when(pl.program_id(1) == 0)
        def _():
            or_ref[...] = jnp.zeros_like(or_ref)
            oi_ref[...] = jnp.zeros_like(oi_ref)

        xv = x_ref[...].astype(BF16)
        or_ref[...] += jnp.dot(xv, hr_ref[...].astype(BF16), preferred_element_type=F32)
        oi_ref[...] += jnp.dot(xv, hi_ref[...].astype(BF16), preferred_element_type=F32)

    out = jax.ShapeDtypeStruct((nb, LANES, kout), F32)
    return pl.pallas_call(
        body, name=name, grid=(nb, l // tl),
        in_specs=[pl.BlockSpec((LANES, tl), lambda b, t: (b, t)),
                  pl.BlockSpec((tl, kout), lambda b, t: (t, b)),
                  pl.BlockSpec((tl, kout), lambda b, t: (t, b))],
        out_specs=[pl.BlockSpec((None, LANES, kout), lambda b, t: (b, 0, 0))] * 2,
        out_shape=[out, out], compiler_params=_params("parallel", "arbitrary"),
    )(xt, h_re, h_im)


def _cmul(ar, ai, br, bi):
    return ar * br - ai * bi, ar * bi + ai * br


def _s5_scan(x_re, x_im, lam_re, lam_im, reverse, name):
    l, n = x_re.shape
    wb = _tile(n, 256)
    nt = l // SUBLANES
    shift = SUBLANES - 1 if reverse else 1
    unroll = 8 if nt % 8 == 0 else 1

    def rows(k):
        t = (nt - 1 - k) if reverse else k
        return pl.ds(pl.multiple_of(t * SUBLANES, SUBLANES), SUBLANES)

    def body(xr_ref, xi_ref, lr_ref, li_ref, hr_ref, hi_ref):
        lr = jnp.broadcast_to(lr_ref[...], (SUBLANES, wb))
        li = jnp.broadcast_to(li_ref[...], (SUBLANES, wb))
        zero = jnp.zeros((SUBLANES, wb), F32)
        one = jnp.ones((SUBLANES, wb), F32)

        def local_step(k, carry):
            hr, hi, pr, pi = carry
            r = rows(k)
            tr_, ti_ = _cmul(lr, li, hr, hi)
            hr, hi = tr_ + xr_ref[r, :], ti_ + xi_ref[r, :]
            hr_ref[r, :] = hr
            hi_ref[r, :] = hi
            pr, pi = _cmul(lr, li, pr, pi)
            return hr, hi, pr, pi

        er, ei, lnr, lni = lax.fori_loop(0, nt, local_step, (zero, zero, one, zero), unroll=unroll)

        row = lax.broadcasted_iota(jnp.int32, (SUBLANES, wb), 0)
        tr_, ti_ = er, ei
        for j in range(1, SUBLANES):
            pr_, pi_ = _cmul(lnr, lni, pltpu.roll(tr_, shift, axis=0), pltpu.roll(ti_, shift, axis=0))
            at = row == ((SUBLANES - 1 - j) if reverse else j)
            tr_ = jnp.where(at, er + pr_, tr_)
            ti_ = jnp.where(at, ei + pi_, ti_)
        edge = row == ((SUBLANES - 1) if reverse else 0)
        cr = jnp.where(edge, 0.0, pltpu.roll(tr_, shift, axis=0))
        ci = jnp.where(edge, 0.0, pltpu.roll(ti_, shift, axis=0))

        def fix_step(k, carry):
            pr, pi = carry
            pr, pi = _cmul(lr, li, pr, pi)
            r = rows(k)
            ar_, ai_ = _cmul(pr, pi, cr, ci)
            hr_ref[r, :] = hr_ref[r, :] + ar_
            hi_ref[r, :] = hi_ref[r, :] + ai_
            return pr, pi

        lax.fori_loop(0, nt, fix_step, (one, zero), unroll=unroll)

    out = jax.ShapeDtypeStruct((l, n), F32)
    col = lambda b: (0, b)
    return pl.pallas_call(
        body, name=name, grid=(n // wb,),
        in_specs=[pl.BlockSpec((l, wb), col), pl.BlockSpec((l, wb), col),
                  pl.BlockSpec((1, wb), col), pl.BlockSpec((1, wb), col)],
        out_specs=[pl.BlockSpec((l, wb), col)] * 2, out_shape=[out, out],
        compiler_params=_params("parallel"),
    )(x_re, x_im, lam_re, lam_im)


def _s5_dlam(g_re, g_im, h_re, h_im, name):
    l, n = g_re.shape
    wb = _tile(n, 256)
    nt = l // SUBLANES

    def body(gr_ref, gi_ref, hr_ref, hi_ref, or_ref, oi_ref):
        row = lax.broadcasted_iota(jnp.int32, (SUBLANES, wb), 0)
        last = pl.ds((nt - 1) * SUBLANES, SUBLANES)
        first = pl.ds(0, SUBLANES)
        pr = jnp.where(row == 0, 0.0, pltpu.roll(hr_ref[last, :], 1, axis=0))
        pi = jnp.where(row == 0, 0.0, pltpu.roll(hi_ref[last, :], 1, axis=0))
        gr, gi = gr_ref[first, :], gi_ref[first, :]
        acc_r = gr * pr + gi * pi
        acc_i = gi * pr - gr * pi

        def step(t, carry):
            acc_r, acc_i = carry
            cur = pl.ds(pl.multiple_of(t * SUBLANES, SUBLANES), SUBLANES)
            prev = pl.ds(pl.multiple_of((t - 1) * SUBLANES, SUBLANES), SUBLANES)
            gr, gi = gr_ref[cur, :], gi_ref[cur, :]
            pr, pi = hr_ref[prev, :], hi_ref[prev, :]
            return acc_r + gr * pr + gi * pi, acc_i + gi * pr - gr * pi

        acc_r, acc_i = lax.fori_loop(1, nt, step, (acc_r, acc_i))
        or_ref[...] = jnp.sum(acc_r, axis=0, keepdims=True)
        oi_ref[...] = jnp.sum(acc_i, axis=0, keepdims=True)

    col = lambda b: (0, b)
    out = jax.ShapeDtypeStruct((1, n), F32)
    return pl.pallas_call(
        body, name=name, grid=(n // wb,),
        in_specs=[pl.BlockSpec((l, wb), col)] * 4,
        out_specs=[pl.BlockSpec((1, wb), col)] * 2, out_shape=[out, out],
        compiler_params=_params("parallel"),
    )(g_re, g_im, h_re, h_im)


def _colsum_prod(a, b, name):
    l, w = a.shape
    tr = _tile(l, 512)

    def body(a_ref, b_ref, o_ref):
        @pl.when(pl.program_id(0) == 0)
        def _():
            o_ref[...] = jnp.zeros_like(o_ref)

        o_ref[...] += jnp.sum(a_ref[...] * b_ref[...], axis=0, keepdims=True)

    return pl.pallas_call(
        body, name=name, grid=(l // tr,), in_specs=[_row_spec(tr, w, 0)] * 2,
        out_specs=_full_spec((1, w)), out_shape=jax.ShapeDtypeStruct((1, w), F32),
        compiler_params=_params("arbitrary"),
    )(a, b)


def _block_diag(m, nb):
    g, r, s = m.shape
    gb = g // nb
    eye = jnp.eye(gb, dtype=m.dtype)
    out = m.reshape(nb, gb, r, 1, s) * eye[None, :, None, :, None]
    return out.reshape(nb, gb * r, gb * s)


def _block_diag_extract(mat, g, r, s):
    nb = mat.shape[0]
    gb = g // nb
    eye = jnp.eye(gb, dtype=mat.dtype)
    m5 = mat.reshape(nb, gb, r, gb, s) * eye[None, :, None, :, None]
    return jnp.sum(m5, axis=3).reshape(g, r, s)


def _adamw(w, m, v, gslots, name, layer=0, prev=None):
    layers, r, c = w.shape
    s = gslots.shape[0]
    tr = _tile(r, max(SUBLANES, 1 << int(math.log2(ADAMW_BLOCK_ELEMS // c))))
    bc1 = 1.0 / (1.0 - ADAM_B1 ** ADAM_STEP)
    bc2 = 1.0 / (1.0 - ADAM_B2 ** ADAM_STEP)

    def body(w_ref, m_ref, v_ref, g_ref, *rest):
        go_ref, d_ref, mo_ref, vo_ref = rest[-4:]
        g = g_ref[0].astype(F32)
        for k in range(1, s):
            g = g + g_ref[k].astype(F32)
        mn = ADAM_B1 * m_ref[...] + (1.0 - ADAM_B1) * g
        vn = ADAM_B2 * v_ref[...] + (1.0 - ADAM_B2) * (g * g)
        go_ref[...] = g
        mo_ref[...] = mn
        vo_ref[...] = vn
        d_ref[...] = -ADAM_LR * ((mn * bc1) / (jnp.sqrt(vn * bc2) + ADAM_EPS) + ADAM_WD * w_ref[...])

    spec = pl.BlockSpec((None, tr, c), lambda i: (layer, i, 0))
    out = jax.ShapeDtypeStruct((layers, r, c), F32)
    in_specs = [spec, spec, spec, pl.BlockSpec((s, tr, c), lambda i: (0, i, 0))]
    args = [w, m, v, gslots]
    aliases = {}
    if prev is not None:
        in_specs += [pl.BlockSpec(memory_space=pl.ANY)] * 4
        args += list(prev)
        aliases = {4 + q: q for q in range(4)}
    return pl.pallas_call(
        body, name=name, grid=(r // tr,), in_specs=in_specs,
        out_specs=[spec] * 4, out_shape=[out] * 4, input_output_aliases=aliases,
        compiler_params=_params("parallel"),
    )(*args)


def _pack(parts):
    flat = jnp.concatenate([p.reshape(-1) for p in parts])
    pad = (-flat.shape[0]) % (PACK_ROWS * LANES)
    return jnp.pad(flat, (0, pad)).reshape(-1, LANES)


def _unpack(packed, shapes):
    flat = packed.reshape(-1)
    out, off = [], 0
    for shp in shapes:
        size = math.prod(shp)
        out.append(flat[off:off + size].reshape(shp))
        off += size
    return out


def kernel(x, norm_g, w_in, conv_w, w_out_a, a_re, a_im, log_dt, b_re, b_im, c_re, c_im, d_skip, w_glu, b_glu, w_out_b, w_o, final_g, loss_target, m_norm_g, m_w_in, m_conv_w, m_w_out_a, m_a_re, m_a_im, m_log_dt, m_b_re, m_b_im, m_c_re, m_c_im, m_d_skip, m_w_glu, m_b_glu, m_w_out_b, m_w_o, m_final_g, v_norm_g, v_w_in, v_conv_w, v_w_out_a, v_a_re, v_a_im, v_log_dt, v_b_re, v_b_im, v_c_re, v_c_im, v_d_skip, v_w_glu, v_b_glu, v_w_out_b, v_w_o, v_final_g):
    depth = norm_g.shape[0]
    l, d = x.shape[1], x.shape[2]
    ws = w_glu.shape[2]
    n_groups, n_state = a_re.shape[1], a_re.shape[2]
    nb = ws // LANES
    assert S5_GROUP == b_re.shape[3] and n_state * S5_GB == 4 * LANES
    u_col, zb_col = 4 * d // ws, 4 * d // ws + 1
    ga_col, gb_col = (4 * d + 2 * ws) // d, (4 * d + 2 * ws) // d + 1
    me = 4 * lax.axis_index("x") + 2 * lax.axis_index("y") + lax.axis_index("c")

    xs = [x[0]]
    tgt = loss_target[0]

    conv_full = _all_gather(jnp.pad(conv_w.reshape(depth * 3, -1), ((0, SUBLANES - depth * 3), (0, 0))),
                            "ag_conv_w")
    conv_full = conv_full.transpose(1, 0, 2).reshape(SUBLANES, d)[:depth * 3].reshape(depth, 3, d)
    big_names = ("w_in", "w_out_a", "w_glu", "w_out_b", "w_o")
    big = dict(w_in=(w_in, m_w_in, v_w_in), w_out_a=(w_out_a, m_w_out_a, v_w_out_a),
               w_glu=(w_glu, m_w_glu, v_w_glu), w_out_b=(w_out_b, m_w_out_b, v_w_out_b),
               w_o=(w_o, m_w_o, v_w_o))

    def shards_bf16(i):
        return [big[k][0][i].astype(BF16) for k in big_names]

    gathered = {0: [_all_gather(s_, f"ag_{k}_0") for k, s_ in zip(big_names, shards_bf16(0))]}
    ag_pending = {}
    dep = gathered[0][-1]
    for i in range(1, depth):
        shards = _after(shards_bf16(i), dep)
        sems, srcs, lands, dep = _exchange_start(
            _plan_gather_chips, shards, [_landing(s_, N_DEV, me) for s_ in shards], f"ag_start_{i}")
        ag_pending[i] = (sems, srcs, lands)
    norm_g_first = _after(norm_g[0], dep)

    def as_operands(gath):
        return dict(w_in=gath[0], w_a=gath[1].reshape(d, d), w_glu=gath[2].reshape(ws, ws),
                    w_b=gath[3], w_o=gath[4].reshape(d, d))

    saved = []
    wg = [None] * depth
    for i in range(depth):
        xi = xs[-1]
        if i in ag_pending:
            sems, _, lands = ag_pending[i]
            _, gathered[i] = _exchange_wait(_plan_gather_forward, sems, None, lands, xi, f"ag_forward_wait_{i}")
        wg[i] = g = as_operands(gathered[i])
        conv8 = jnp.pad(conv_full[i], ((0, SUBLANES - 3), (0, 0)))
        h = _rmsnorm_fwd(xi, norm_g_first if i == 0 else norm_g[i], f"rmsnorm_fwd_{i}")
        proj = _mm_win_fwd(h, g["w_in"], f"mm_proj_{i}")
        pa = _branch_a_fwd(proj, conv8, d, f"branch_a_fwd_{i}")
        ya = _mm(pa, g["w_a"], name=f"mm_ya_{i}")
        shape3 = (n_groups, n_state, S5_GROUP)
        dense_in = (_dense(a_re[i][:, :, None], shape3), _dense(a_im[i][:, :, None], shape3),
                    _dense(log_dt[i][:, None, None], shape3), b_re[i].reshape(-1, LANES), b_im[i].reshape(-1, LANES))
        lbr, lbi, bbr, bbi = _s5_params(*dense_in, f"s5_params_{i}")
        lam_re = lbr.reshape(shape3)[:, :, 0].reshape(1, -1)
        lam_im = lbi.reshape(shape3)[:, :, 0].reshape(1, -1)
        bbr3, bbi3 = bbr.reshape(shape3), bbi.reshape(shape3)
        u_seg = _to_segments(proj[:, 4 * d:4 * d + ws])
        bu_re, bu_im = _s5_up(u_seg, _block_diag(bbr3.transpose(0, 2, 1), nb),
                              _block_diag(bbi3.transpose(0, 2, 1), nb), f"s5_bu_{i}")
        h_re, h_im = _s5_scan(bu_re, bu_im, lam_re, lam_im, False, f"s5_scan_fwd_{i}")
        y_seg = _s5_down(h_re, h_im, _block_diag(c_re[i].transpose(0, 2, 1), nb),
                         _block_diag(c_im[i].transpose(0, 2, 1), nb), u_seg,
                         d_skip[i].reshape(1, ws), f"s5_y_{i}")
        y = _from_segments(y_seg)
        if i + 1 in ag_pending:
            sems, srcs, lands = ag_pending[i + 1]
            _, lands = _exchange_wait(_plan_gather_chips, sems, srcs, lands, y, f"ag_wait_{i + 1}")
            sems, _, lands, token = _exchange_start(_plan_gather_forward, None, lands, f"ag_forward_start_{i + 1}")
            ag_pending[i + 1] = (sems, None, lands)
            y = _after(y, token)
        yg = _gelu_cast(y, f"gelu_{i}")
        gl = _mm(yg, g["w_glu"], name=f"mm_glu_{i}")
        pb = _glu_post(y, gl, proj, b_glu[i], zb_col, f"glu_post_{i}")
        w_b2d = g["w_b"].transpose(1, 0, 2).reshape(ws, d)
        yb = _mm(pb, w_b2d, name=f"mm_yb_{i}")
        mrg = _merge_fwd(proj, ya, yb, d, ga_col, gb_col, f"merge_fwd_{i}")
        xs.append(_mm(mrg, g["w_o"], name=f"mm_out_{i}", add=xi))
        saved.append(dict(h=h, proj=proj, pa=pa, ya=ya, yb=yb, y=y, yg=yg, gl=gl, pb=pb, mrg=mrg,
                          u_seg=u_seg, h_re=h_re, h_im=h_im, lam_re=lam_re, lam_im=lam_im,
                          bbr3=bbr3, bbi3=bbi3, dense_in=dense_in, conv8=conv8, w_b2d=w_b2d))

    dx, g_final, loss_part = _final_loss(xs[-1], final_g, tgt, "final_loss")
    loss = lax.psum(loss_part[0, 0], MESH_AXES)

    rs_pending = []
    small = {k: [None] * depth for k in ("norm_g", "a_re", "a_im", "log_dt", "b_re", "b_im", "c_re", "c_im",
                                         "d_skip", "b_glu", "conv_w")}

    my_chip = 2 * lax.axis_index("x") + lax.axis_index("y")

    def reduce_on_chip(pieces, tag):
        lands = [lax.empty((4,) + p.shape[1:], p.dtype) for p in pieces]
        sems, srcs, lands, token = _exchange_start(_plan_reduce_sibling, pieces, lands, f"rs_sibling_start_{tag}")
        return (sems, srcs, lands), token

    def reduce_across_chips(names_, state, layer, tag, after):
        sems, srcs, lands = state
        srcs, lands = _exchange_wait(_plan_reduce_sibling, sems, srcs, lands, after, f"rs_sibling_wait_{tag}")
        sums = [_chip_sums(p, l_, f"chip_sum_{k}_{layer}") for k, p, l_ in zip(names_, srcs, lands)]
        lands = [_landing(lax.dynamic_index_in_dim(s_, my_chip, 0, keepdims=False), 4, my_chip) for s_ in sums]
        sems, srcs, lands, token = _exchange_start(_plan_reduce_chips, sums, lands, f"rs_chips_start_{tag}")
        rs_pending.append((names_, layer, sems, srcs, lands, f"rs_chips_wait_{tag}"))
        return token

    for i in reversed(range(depth)):
        s, g = saved[i], wg[i]
        proj = s["proj"]
        dxo_b = dx.astype(BF16)
        dm = _mm(dxo_b, g["w_o"], name=f"mm_dm_{i}", nt=True)
        gw_o = _mm(s["mrg"].T, dxo_b, name=f"mm_gw_o_{i}", out_dtype=BF16)
        dya, dyb, dga, dgb = _merge_bwd(proj, s["ya"], s["yb"], dm, d, ga_col, gb_col, f"merge_bwd_{i}")
        dpa = _mm(dya, g["w_a"], name=f"mm_dpa_{i}", nt=True)
        gw_a = _mm(s["pa"].T, dya, name=f"mm_gw_a_{i}", out_dtype=BF16)
        dpb = _mm(dyb, s["w_b2d"], name=f"mm_dpb_{i}", nt=True)
        gw_b = _mm(s["pb"].T, dyb, name=f"mm_gw_b_{i}", split_n=N_DEV, out_dtype=BF16)
        dv, dbg, dcg, dza, dw0, dw1, dw2 = _branch_a_bwd(proj, dpa, s["conv8"], d, f"branch_a_bwd_{i}")
        small["conv_w"][i] = jnp.concatenate([dw0, dw1, dw2], axis=0)
        dzb, dgl, t1, db_glu = _glu_bwd1(s["y"], s["gl"], proj, b_glu[i], dpb, zb_col, f"glu_bwd1_{i}")
        small["b_glu"][i] = db_glu.reshape(ws)
        dyg2 = _mm(dgl, g["w_glu"], name=f"mm_dyg_{i}", nt=True)
        gw_glu = _mm(s["yg"].T, dgl, name=f"mm_gw_glu_{i}", out_dtype=BF16)
        small_names_ = ("w_out_a", "w_glu", "w_out_b", "w_o")
        state, token = reduce_on_chip(
            [gw_a.reshape(N_DEV, d // N_DEV, d), gw_glu.reshape(N_DEV, ws // N_DEV, ws), gw_b,
             gw_o.reshape(N_DEV, d // N_DEV, d)], f"small_{i}")
        dy = _glu_bwd2(s["y"], _after(t1, token), dyg2, f"glu_bwd2_{i}")
        dy_seg = _to_segments(dy)
        u_seg = s["u_seg"]
        small["d_skip"][i] = _colsum_prod(dy_seg, u_seg, f"s5_dskip_{i}").reshape(n_groups, S5_GROUP)
        gh_re, gh_im = _s5_up(dy_seg, _block_diag(c_re[i], nb), _block_diag(-c_im[i], nb), f"s5_gh_{i}")
        token = reduce_across_chips(small_names_, state, i, f"small_{i}", gh_re)
        q_re, q_im = _s5_scan(gh_re, gh_im, _after(s["lam_re"], token), -s["lam_im"], True, f"s5_scan_bwd_{i}")
        du_seg = _s5_down(q_re, q_im, _block_diag(s["bbr3"], nb), _block_diag(-s["bbi3"], nb),
                          dy_seg, d_skip[i].reshape(1, ws), f"s5_du_{i}")
        gc_re, gc_im = _s5_outer(dy_seg.T, s["h_re"], s["h_im"], f"s5_gc_{i}")
        small["c_re"][i] = _block_diag_extract(gc_re, n_groups, S5_GROUP, n_state)
        small["c_im"][i] = -_block_diag_extract(gc_im, n_groups, S5_GROUP, n_state)
        gbb_re, gbb_im = _s5_outer(u_seg.T, q_re, q_im, f"s5_gbb_{i}")
        gbb_re = _block_diag_extract(gbb_re, n_groups, S5_GROUP, n_state).transpose(0, 2, 1)
        gbb_im = _block_diag_extract(gbb_im, n_groups, S5_GROUP, n_state).transpose(0, 2, 1)
        glam_re, glam_im = _s5_dlam(q_re, q_im, s["h_re"], s["h_im"], f"s5_dlam_{i}")
        shape3 = (n_groups, n_state, S5_GROUP)
        gar, gai, gdt, gbr, gbi = _s5_params_bwd(
            *s["dense_in"], _dense(glam_re.reshape(n_groups, n_state, 1), shape3),
            _dense(glam_im.reshape(n_groups, n_state, 1), shape3),
            gbb_re.reshape(-1, LANES), gbb_im.reshape(-1, LANES), n_groups, f"s5_params_bwd_{i}")
        small["a_re"][i] = gar.reshape(shape3)[:, :, 0]
        small["a_im"][i] = gai.reshape(shape3)[:, :, 0]
        small["log_dt"][i] = gdt[:, 0]
        small["b_re"][i] = gbr.reshape(shape3)
        small["b_im"][i] = gbi.reshape(shape3)
        du = _from_segments(du_seg).astype(BF16)
        dproj = jnp.concatenate([dv, dbg, dcg, dza, du, dzb, dga, dgb], axis=1)
        gw_in = _mm(s["h"].T, dproj, name=f"mm_gw_in_{i}", split_n=N_DEV, tm=1024,
                    out_dtype=BF16)
        state, token = reduce_on_chip([gw_in], f"w_in_{i}")
        dh = _mm_win_bwd(_after(dproj, token), g["w_in"], f"mm_dh_{i}")
        token = reduce_across_chips(("w_in",), state, i, f"w_in_{i}", dh)
        dx, dng = _rmsnorm_bwd(xs[i], norm_g[i], _after(dh, token), dx, f"rmsnorm_bwd_{i}")
        small["norm_g"][i] = dng.reshape(d)

    results = {}

    small_names = ("norm_g", "a_re", "a_im", "log_dt", "b_re", "b_im", "c_re", "c_im", "d_skip", "b_glu")
    small_w = dict(norm_g=(norm_g, m_norm_g, v_norm_g), a_re=(a_re, m_a_re, v_a_re), a_im=(a_im, m_a_im, v_a_im),
                   log_dt=(log_dt, m_log_dt, v_log_dt), b_re=(b_re, m_b_re, v_b_re), b_im=(b_im, m_b_im, v_b_im),
                   c_re=(c_re, m_c_re, v_c_re), c_im=(c_im, m_c_im, v_c_im), d_skip=(d_skip, m_d_skip, v_d_skip),
                   b_glu=(b_glu, m_b_glu, v_b_glu), final_g=(final_g, m_final_g, v_final_g))
    order = small_names + ("final_g", "conv_w")
    part = {k: jnp.stack(small[k]) for k in small_names + ("conv_w",)}
    part["final_g"] = g_final.reshape(d)
    shapes = [part[k].shape for k in order]
    gpack = _all_gather(_pack([part[k] for k in order]), "ag_small_grads")
    zeros_conv = jnp.zeros(part["conv_w"].shape, F32)
    wpack = _pack([small_w[k][0] for k in order[:-1]] + [zeros_conv])
    mpack = _pack([small_w[k][1] for k in order[:-1]] + [zeros_conv])
    vpack = _pack([small_w[k][2] for k in order[:-1]] + [zeros_conv])
    sres = [_unpack(p[0], shapes)
            for p in _adamw(wpack[None], mpack[None], vpack[None], gpack, "adamw_small")]
    for j, k in enumerate(order[:-1]):
        results[k] = [sres[q][j] for q in range(4)]
    dc = d // N_DEV
    gconv = lax.dynamic_slice_in_dim(sres[0][-1], me * dc, dc, axis=2)
    pad8 = lambda a: jnp.pad(a.reshape(depth * 3, dc), ((0, SUBLANES - depth * 3), (0, 0)))[None]
    cres = _adamw(pad8(conv_w), pad8(m_conv_w), pad8(v_conv_w), pad8(gconv), "adamw_conv_w")
    results["conv_w"] = [r_[0, :depth * 3].reshape(depth, 3, dc) for r_ in cres]

    after = cres[0]
    for names_, layer, sems, srcs, lands, wait_name in rs_pending:
        _, slots = _exchange_wait(_plan_reduce_chips, sems, srcs, lands, after, wait_name)
        for k, land in zip(names_, slots):
            w_, m_, v_ = big[k]
            results[k] = _adamw(w_, m_, v_, land, f"adamw_{k}_{layer}", layer=layer, prev=results.get(k))
            after = results[k][0]

    names = ("norm_g", "w_in", "conv_w", "w_out_a", "a_re", "a_im", "log_dt", "b_re", "b_im", "c_re", "c_im",
             "d_skip", "w_glu", "b_glu", "w_out_b", "w_o", "final_g")
    outs = [loss, dx[None]]
    for q in range(4):
        outs += [results[k][q] for k in names]
    return tuple(outs)
```

```python
import functools
import math

import jax
import jax.numpy as jnp
from jax import lax
from jax.experimental import pallas as pl
from jax.experimental.pallas import tpu as pltpu

F32 = jnp.float32
BF16 = jnp.bfloat16
HIGHEST = lax.Precision.HIGHEST

N_DEV = 8
MESH_AXES = ("x", "y", "c")
LANES = 128
SUBLANES = 8
VMEM_LIMIT_BYTES = 56 * 1024 * 1024

RMS_EPS = 1e-6
ADAM_LR = 0.001
ADAM_B1 = 0.9
ADAM_B2 = 0.999
ADAM_EPS = 1e-08
ADAM_WD = 0.01
ADAM_STEP = 10
GELU_C0 = math.sqrt(2.0 / math.pi)
GELU_C1 = 0.044715

ADAMW_BLOCK_ELEMS = 1 << 17
PACK_ROWS = 512

S5_GROUP = 16
S5_GB = LANES // S5_GROUP


def _params(*semantics):
    return pltpu.CompilerParams(dimension_semantics=semantics, vmem_limit_bytes=VMEM_LIMIT_BYTES)


def _tile(n, pref):
    t = min(n, pref)
    while n % t:
        assert t % 2 == 0, (n, pref)
        t //= 2
    return t


def _sigmoid(z):
    return 1.0 / (1.0 + jnp.exp(-z))


def _gelu(y):
    return 0.5 * y * (1.0 + jnp.tanh(GELU_C0 * (y + GELU_C1 * y * y * y)))


def _gelu_grad(y):
    t = jnp.tanh(GELU_C0 * (y + GELU_C1 * y * y * y))
    return 0.5 * (1.0 + t) + 0.5 * y * (1.0 - t * t) * GELU_C0 * (1.0 + 3.0 * GELU_C1 * y * y)


def _all_gather(shard, name):
    r, c_ = shard.shape

    def body(x_ref, out_ref, send_sems, recv_sems, local_sem):
        x, y, c = lax.axis_index("x"), lax.axis_index("y"), lax.axis_index("c")
        me, sibling = (x, y, c), (x, y, 1 - c)
        chips = [(1 - x, y), (x, 1 - y), (1 - x, 1 - y)]

        def slot(px, py, pc):
            return out_ref.at[4 * px + 2 * py + pc]

        def copy(k, block, to, src=None):
            return pltpu.make_async_remote_copy(
                src_ref=slot(*block) if src is None else src, dst_ref=slot(*block),
                send_sem=send_sems.at[k], recv_sem=recv_sems.at[k],
                device_id=to, device_id_type=pl.DeviceIdType.MESH)

        mine = pltpu.make_async_copy(x_ref, slot(*me), local_sem)
        mine.start()
        first = [copy(0, me, sibling, src=x_ref)]
        first += [copy(1 + j, me, (*chip, c), src=x_ref) for j, chip in enumerate(chips)]
        for cp in first:
            cp.start()
        passed = [copy(4 + j, (*chip, c), sibling) for j, chip in enumerate(chips)]
        for j, chip in enumerate(chips):
            copy(1 + j, (*chip, c), me).wait_recv()
            passed[j].start()
        copy(0, sibling, me).wait_recv()
        for j, chip in enumerate(chips):
            copy(4 + j, (*chip, 1 - c), me).wait_recv()
        for cp in first + passed:
            cp.wait_send()
        mine.wait()

    return pl.pallas_call(
        body, name=name,
        out_shape=jax.ShapeDtypeStruct((N_DEV, r, c_), shard.dtype),
        in_specs=[pl.BlockSpec(memory_space=pl.ANY)],
        out_specs=pl.BlockSpec(memory_space=pl.ANY),
        scratch_shapes=[pltpu.SemaphoreType.DMA((7,)), pltpu.SemaphoreType.DMA((7,)),
                        pltpu.SemaphoreType.DMA],
    )(shard)


HBM_SPEC = pl.BlockSpec(memory_space=pltpu.HBM)
SEM_SPEC = pl.BlockSpec(memory_space=pltpu.SEMAPHORE)
DATAFLOW_EFFECT = pltpu.SideEffectType.DATAFLOW_SIDE_EFFECTING
OTHER_CHIPS = (2, 4, 6)


def _flip(pos, mask):
    x, y, c = pos
    return x ^ ((mask >> 2) & 1), y ^ ((mask >> 1) & 1), c ^ (mask & 1)


def _dev(pos):
    return 4 * pos[0] + 2 * pos[1] + pos[2]


def _chip(pos):
    return 2 * pos[0] + pos[1]


def _plan_gather_chips(me):
    return [(_flip(me, k), None, _dev(me), _dev(_flip(me, k))) for k in (1,) + OTHER_CHIPS]


def _plan_gather_forward(me):
    sib = _flip(me, 1)
    return [(sib, _dev(_flip(me, k)), _dev(_flip(me, k)), _dev(_flip(sib, k))) for k in OTHER_CHIPS]


def _plan_reduce_sibling(me):
    sib = _flip(me, 1)
    return [(sib, 2 * q + sib[2], q, q) for q in range(4)]


def _plan_reduce_chips(me):
    return [(_flip(me, k), _chip(_flip(me, k)), _chip(me), _chip(_flip(me, k))) for k in OTHER_CHIPS]


PLAN_COPIES = {_plan_gather_chips: 4, _plan_gather_forward: 3, _plan_reduce_sibling: 4, _plan_reduce_chips: 3}


def _exchange_copies(plan, src_refs, land_refs, send_sems, recv_sems):
    me = (lax.axis_index("x"), lax.axis_index("y"), lax.axis_index("c"))
    pairs = []
    for b, (src_ref, land_ref) in enumerate(zip(src_refs, land_refs)):
        for j, (peer, src_slot, there, here) in enumerate(plan(me)):
            sem = b * PLAN_COPIES[plan] + j
            src = src_ref if src_slot is None else src_ref.at[src_slot]
            out = pltpu.make_async_remote_copy(
                src_ref=src, dst_ref=land_ref.at[there], send_sem=send_sems.at[sem], recv_sem=recv_sems.at[sem],
                device_id=peer, device_id_type=pl.DeviceIdType.MESH)
            inc = pltpu.make_async_remote_copy(
                src_ref=src, dst_ref=land_ref.at[here], send_sem=send_sems.at[sem], recv_sem=recv_sems.at[sem],
                device_id=peer, device_id_type=pl.DeviceIdType.MESH)
            pairs.append((out, inc))
    return pairs


def _exchange_start(plan, srcs, lands, name):
    srcs = [] if srcs is None else list(srcs)
    ns, n = len(srcs), len(lands)

    def body(*refs):
        land_refs = refs[ns:ns + n]
        pairs = _exchange_copies(plan, refs[:ns] if ns else land_refs, land_refs, refs[ns + n], refs[ns + n + 1])
        for out, _ in pairs:
            out.start()
        token = refs[-1]
        token[...] = jnp.zeros_like(token)

    sems = pltpu.SemaphoreType.DMA((PLAN_COPIES[plan] * n,))
    bufs = srcs + list(lands)
    outs = pl.pallas_call(
        body, name=name,
        out_shape=(sems, sems, *[pltpu.HBM(a.shape, a.dtype) for a in bufs],
                   jax.ShapeDtypeStruct((SUBLANES, LANES), F32)),
        in_specs=[HBM_SPEC] * (ns + n),
        out_specs=(SEM_SPEC, SEM_SPEC, *[HBM_SPEC] * (ns + n), pl.BlockSpec(memory_space=pltpu.VMEM)),
        input_output_aliases={i: 2 + i for i in range(ns + n)},
        compiler_params=pltpu.CompilerParams(has_side_effects=DATAFLOW_EFFECT),
    )(*[pltpu.with_memory_space_constraint(a, pltpu.HBM) for a in bufs])
    return (outs[0], outs[1]), (outs[2:2 + ns] if ns else None), outs[2 + ns:2 + ns + n], outs[-1]


def _exchange_wait(plan, sems, srcs, lands, after, name):
    srcs = [] if srcs is None else list(srcs)
    ns, n = len(srcs), len(lands)

    def body(*refs):
        land_refs = refs[ns:ns + n]
        pairs = _exchange_copies(plan, refs[:ns] if ns else land_refs, land_refs, refs[ns + n], refs[ns + n + 1])
        for out, inc in pairs:
            out.wait_send()
            inc.wait_recv()

    bufs = srcs + list(lands)
    outs = pl.pallas_call(
        body, name=name,
        out_shape=[pltpu.HBM(a.shape, a.dtype) for a in bufs],
        in_specs=[HBM_SPEC] * (ns + n) + [SEM_SPEC, SEM_SPEC, pl.BlockSpec(memory_space=pl.ANY)],
        out_specs=[HBM_SPEC] * (ns + n),
        input_output_aliases={i: i for i in range(ns + n)},
        compiler_params=pltpu.CompilerParams(has_side_effects=DATAFLOW_EFFECT),
    )(*bufs, sems[0], sems[1], after)
    return outs[:ns], outs[ns:]


def _landing(own, slots, slot):
    land = lax.empty((slots,) + own.shape, own.dtype)
    return lax.dynamic_update_slice(land, own[None], (slot,) + (0,) * own.ndim)


def _chip_sums(pieces, land, name):
    _, r, c_ = land.shape
    tr = _tile(r, max(2 * SUBLANES, 1 << int(math.log2(4 * ADAMW_BLOCK_ELEMS // c_))))

    def body(core_ref, p_ref, l_ref, o_ref):
        o_ref[...] = (p_ref[...].astype(F32) + l_ref[...].astype(F32)).astype(o_ref.dtype)

    spec = pl.BlockSpec((None, tr, c_), lambda q, i, core: (q, i, 0))
    return pl.pallas_call(
        body, name=name,
        grid_spec=pltpu.PrefetchScalarGridSpec(
            num_scalar_prefetch=1, grid=(4, r // tr),
            in_specs=[pl.BlockSpec((None, tr, c_), lambda q, i, core: (2 * q + core[0], i, 0)), spec],
            out_specs=spec),
        out_shape=jax.ShapeDtypeStruct(land.shape, land.dtype),
        compiler_params=_params("parallel", "parallel"),
    )(lax.axis_index("c").reshape(1), pieces, land)


def _after(a, dep):
    return lax.optimization_barrier((a, dep))[0]


def _mm(a, b, *, name, nt=False, out_dtype=F32, add=None, split_n=None, tm=512, tn=1024):
    m, k = a.shape
    n = b.shape[0] if nt else b.shape[1]
    tm = _tile(m, tm)
    tn = n // split_n if split_n else _tile(n, tn)
    dims = (((1,), (1,)), ((), ())) if nt else (((1,), (0,)), ((), ()))

    def body(*refs):
        a_ref, b_ref = refs[0], refs[1]
        o_ref = refs[-1]
        acc = lax.dot_general(a_ref[...], b_ref[...], dims, preferred_element_type=F32)
        if add is not None:
            acc = acc + refs[2][...]
        o_ref[...] = acc.astype(o_ref.dtype)

    in_specs = [pl.BlockSpec((tm, k), lambda i, j: (i, 0)),
                pl.BlockSpec((tn, k), lambda i, j: (j, 0)) if nt
                else pl.BlockSpec((k, tn), lambda i, j: (0, j))]
    args = [a, b]
    if add is not None:
        in_specs.append(pl.BlockSpec((tm, tn), lambda i, j: (i, j)))
        args.append(add)
    if split_n:
        out_shape = jax.ShapeDtypeStruct((split_n, m, tn), out_dtype)
        out_spec = pl.BlockSpec((None, tm, tn), lambda i, j: (j, i, 0))
    else:
        out_shape = jax.ShapeDtypeStruct((m, n), out_dtype)
        out_spec = pl.BlockSpec((tm, tn), lambda i, j: (i, j))
    return pl.pallas_call(
        body, name=name, grid=(m // tm, n // tn), in_specs=in_specs, out_specs=out_spec,
        out_shape=out_shape, compiler_params=_params("parallel", "parallel"),
    )(*args)


def _mm_win_fwd(h, w_g, name):
    m, k = h.shape
    nj = w_g.shape[2]
    tm = _tile(m, 512)

    def body(a_ref, b_ref, o_ref):
        o_ref[...] = jnp.dot(a_ref[...], b_ref[...], preferred_element_type=F32)

    return pl.pallas_call(
        body, name=name, grid=(N_DEV, m // tm),
        in_specs=[pl.BlockSpec((tm, k), lambda j, i: (i, 0)),
                  pl.BlockSpec((None, k, nj), lambda j, i: (j, 0, 0))],
        out_specs=pl.BlockSpec((tm, nj), lambda j, i: (i, j)),
        out_shape=jax.ShapeDtypeStruct((m, N_DEV * nj), F32),
        compiler_params=_params("parallel", "parallel"),
    )(h, w_g)


def _mm_win_bwd(dproj, w_g, name):
    m = dproj.shape[0]
    d, nj = w_g.shape[1], w_g.shape[2]
    tm = _tile(m, 512)
    tn = _tile(d, 1024)

    def body(a_ref, b_ref, o_ref, acc_ref):
        j = pl.program_id(2)

        @pl.when(j == 0)
        def _():
            acc_ref[...] = jnp.zeros_like(acc_ref)

        acc_ref[...] += lax.dot_general(a_ref[...], b_ref[...], (((1,), (1,)), ((), ())),
                                        preferred_element_type=F32)

        @pl.when(j == N_DEV - 1)
        def _():
            o_ref[...] = acc_ref[...]

    return pl.pallas_call(
        body, name=name, grid=(m // tm, d // tn, N_DEV),
        in_specs=[pl.BlockSpec((tm, nj), lambda i, n, j: (i, j)),
                  pl.BlockSpec((None, tn, nj), lambda i, n, j: (j, n, 0))],
        out_specs=pl.BlockSpec((tm, tn), lambda i, n, j: (i, n)),
        out_shape=jax.ShapeDtypeStruct((m, d), F32),
        scratch_shapes=[pltpu.VMEM((tm, tn), F32)],
        compiler_params=_params("parallel", "parallel", "arbitrary"),
    )(dproj, w_g)


def _row_spec(tr, w, col):
    return pl.BlockSpec((tr, w), lambda i: (i, col))


def _full_spec(shape):
    return pl.BlockSpec(shape, lambda i: (0,) * len(shape))


def _rmsnorm_fwd(x, g, name):
    l, d = x.shape
    tr = _tile(l, 256)

    def body(x_ref, g_ref, o_ref):
        xv = x_ref[...]
        rstd = lax.rsqrt(jnp.mean(xv * xv, axis=-1, keepdims=True) + RMS_EPS)
        o_ref[...] = (xv * rstd * g_ref[...]).astype(o_ref.dtype)

    return pl.pallas_call(
        body, name=name, grid=(l // tr,),
        in_specs=[_row_spec(tr, d, 0), _full_spec((1, d))],
        out_specs=_row_spec(tr, d, 0),
        out_shape=jax.ShapeDtypeStruct((l, d), BF16),
        compiler_params=_params("parallel"),
    )(x, g.reshape(1, d))


def _rmsnorm_bwd(x, g, dh, dxo, name):
    l, d = x.shape
    tr = _tile(l, 256)

    def body(x_ref, g_ref, dh_ref, dxo_ref, dx_ref, dg_ref):
        xv = x_ref[...]
        rstd = lax.rsqrt(jnp.mean(xv * xv, axis=-1, keepdims=True) + RMS_EPS)
        dhv = dh_ref[...]
        gdy = dhv * g_ref[...]
        dot = jnp.mean(gdy * xv, axis=-1, keepdims=True)
        dx_ref[...] = dxo_ref[...] + rstd * gdy - xv * (rstd * rstd * rstd * dot)

        @pl.when(pl.program_id(0) == 0)
        def _():
            dg_ref[...] = jnp.zeros_like(dg_ref)

        dg_ref[...] += jnp.sum(dhv * xv * rstd, axis=0, keepdims=True)

    return pl.pallas_call(
        body, name=name, grid=(l // tr,),
        in_specs=[_row_spec(tr, d, 0), _full_spec((1, d)), _row_spec(tr, d, 0), _row_spec(tr, d, 0)],
        out_specs=[_row_spec(tr, d, 0), _full_spec((1, d))],
        out_shape=[jax.ShapeDtypeStruct((l, d), F32), jax.ShapeDtypeStruct((1, d), F32)],
        compiler_params=_params("arbitrary"),
    )(x, g.reshape(1, d), dh, dxo)


def _final_loss(x, g, tgt, name):
    l, d = x.shape
    tr = _tile(l, 256)

    def body(x_ref, g_ref, t_ref, dx_ref, dg_ref, loss_ref):
        xv = x_ref[...]
        gv = g_ref[...]
        rstd = lax.rsqrt(jnp.mean(xv * xv, axis=-1, keepdims=True) + RMS_EPS)
        xn = xv * rstd
        err = xn * gv - t_ref[...]
        dy = err * (1.0 / d)
        gdy = dy * gv
        dot = jnp.mean(gdy * xv, axis=-1, keepdims=True)
        dx_ref[...] = rstd * gdy - xv * (rstd * rstd * rstd * dot)

        @pl.when(pl.program_id(0) == 0)
        def _():
            dg_ref[...] = jnp.zeros_like(dg_ref)
            loss_ref[...] = jnp.zeros_like(loss_ref)

        dg_ref[...] += jnp.sum(dy * xn, axis=0, keepdims=True)
        loss_ref[...] += (0.5 / d) * jnp.sum(err * err)

    return pl.pallas_call(
        body, name=name, grid=(l // tr,),
        in_specs=[_row_spec(tr, d, 0), _full_spec((1, d)), _row_spec(tr, d, 0)],
        out_specs=[_row_spec(tr, d, 0), _full_spec((1, d)), _full_spec((SUBLANES, LANES))],
        out_shape=[jax.ShapeDtypeStruct((l, d), F32), jax.ShapeDtypeStruct((1, d), F32),
                   jax.ShapeDtypeStruct((SUBLANES, LANES), F32)],
        compiler_params=_params("arbitrary"),
    )(x, g.reshape(1, d), tgt)


def _halo_spec(tr, w, col, nblk8, before):
    step = tr // SUBLANES
    if before:
        return pl.BlockSpec((SUBLANES, w), lambda i: (jnp.maximum(i * step - 1, 0), col))
    return pl.BlockSpec((SUBLANES, w), lambda i: (jnp.minimum((i + 1) * step, nblk8 - 1), col))


def _shift_down(cur, before, k):
    ext = jnp.concatenate([before, cur], axis=0)
    return pltpu.roll(ext, k, axis=0)[SUBLANES:, :]


def _shift_up(cur, after, k):
    tr = cur.shape[0]
    ext = jnp.concatenate([cur, after], axis=0)
    return pltpu.roll(ext, tr + SUBLANES - k, axis=0)[:tr, :]


def _branch_a_fwd(proj, conv_w, d, name):
    l = proj.shape[0]
    tr = _tile(l, 256)
    nblk8 = l // SUBLANES

    def body(v_ref, bg_ref, cg_ref, za_ref, vh_ref, cgh_ref, w_ref, o_ref):
        first = pl.program_id(0) == 0
        cv = cg_ref[...] * v_ref[...]
        cvh = jnp.where(first, 0.0, cgh_ref[...] * vh_ref[...])
        w0, w1, w2 = w_ref[0:1, :], w_ref[1:2, :], w_ref[2:3, :]
        q = w2 * cv + w1 * _shift_down(cv, cvh, 1) + w0 * _shift_down(cv, cvh, 2)
        za = za_ref[...]
        o_ref[...] = (bg_ref[...] * q * (za * _sigmoid(za))).astype(o_ref.dtype)

    return pl.pallas_call(
        body, name=name, grid=(l // tr,),
        in_specs=[_row_spec(tr, d, 0), _row_spec(tr, d, 1), _row_spec(tr, d, 2), _row_spec(tr, d, 3),
                  _halo_spec(tr, d, 0, nblk8, True), _halo_spec(tr, d, 2, nblk8, True),
                  _full_spec((SUBLANES, d))],
        out_specs=_row_spec(tr, d, 0),
        out_shape=jax.ShapeDtypeStruct((l, d), BF16),
        compiler_params=_params("parallel"),
    )(proj, proj, proj, proj, proj, proj, conv_w)


def _branch_a_bwd(proj, dpa, conv_w, d, name):
    l = proj.shape[0]
    tr = _tile(l, 128)
    nblk8 = l // SUBLANES
    ntiles = l // tr

    def body(v_ref, bg_ref, cg_ref, za_ref, dpa_ref, vh_ref, cgh_ref, bgn_ref, zan_ref, dpan_ref,
             w_ref, dv_ref, dbg_ref, dcg_ref, dza_ref, dw0_ref, dw1_ref, dw2_ref):
        i = pl.program_id(0)
        v, bg, cg, za, dpa_v = v_ref[...], bg_ref[...], cg_ref[...], za_ref[...], dpa_ref[...]
        w0, w1, w2 = w_ref[0:1, :], w_ref[1:2, :], w_ref[2:3, :]
        cv = cg * v
        cvh = jnp.where(i == 0, 0.0, cgh_ref[...] * vh_ref[...])
        cv1 = _shift_down(cv, cvh, 1)
        cv2 = _shift_down(cv, cvh, 2)
        q = w2 * cv + w1 * cv1 + w0 * cv2
        sg = _sigmoid(za)
        s = za * sg
        dbg_ref[...] = (dpa_v * q * s).astype(dbg_ref.dtype)
        dza_ref[...] = (dpa_v * bg * q * (sg * (1.0 + za * (1.0 - sg)))).astype(dza_ref.dtype)
        dq = dpa_v * bg * s
        zan = zan_ref[...]
        dqn = jnp.where(i == ntiles - 1, 0.0, dpan_ref[...] * bgn_ref[...] * (zan * _sigmoid(zan)))
        dcv = w2 * dq + w1 * _shift_up(dq, dqn, 1) + w0 * _shift_up(dq, dqn, 2)
        dcg_ref[...] = (dcv * v).astype(dcg_ref.dtype)
        dv_ref[...] = (dcv * cg).astype(dv_ref.dtype)

        @pl.when(i == 0)
        def _():
            dw0_ref[...] = jnp.zeros_like(dw0_ref)
            dw1_ref[...] = jnp.zeros_like(dw1_ref)
            dw2_ref[...] = jnp.zeros_like(dw2_ref)

        dw0_ref[...] += jnp.sum(dq * cv2, axis=0, keepdims=True)
        dw1_ref[...] += jnp.sum(dq * cv1, axis=0, keepdims=True)
        dw2_ref[...] += jnp.sum(dq * cv, axis=0, keepdims=True)

    act = jax.ShapeDtypeStruct((l, d), BF16)
    wsum = jax.ShapeDtypeStruct((1, d), F32)
    return pl.pallas_call(
        body, name=name, grid=(ntiles,),
        in_specs=[_row_spec(tr, d, 0), _row_spec(tr, d, 1), _row_spec(tr, d, 2), _row_spec(tr, d, 3),
                  _row_spec(tr, d, 0),
                  _halo_spec(tr, d, 0, nblk8, True), _halo_spec(tr, d, 2, nblk8, True),
                  _halo_spec(tr, d, 1, nblk8, False), _halo_spec(tr, d, 3, nblk8, False),
                  _halo_spec(tr, d, 0, nblk8, False),
                  _full_spec((SUBLANES, d))],
        out_specs=[_row_spec(tr, d, 0)] * 4 + [_full_spec((1, d))] * 3,
        out_shape=[act] * 4 + [wsum] * 3,
        compiler_params=_params("arbitrary"),
    )(proj, proj, proj, proj, dpa, proj, proj, proj, proj, dpa, conv_w)


def _gelu_cast(y, name):
    l, w = y.shape
    tr = _tile(l, 512)

    def body(y_ref, o_ref):
        o_ref[...] = _gelu(y_ref[...]).astype(o_ref.dtype)

    return pl.pallas_call(
        body, name=name, grid=(l // tr,), in_specs=[_row_spec(tr, w, 0)],
        out_specs=_row_spec(tr, w, 0), out_shape=jax.ShapeDtypeStruct((l, w), BF16),
        compiler_params=_params("parallel"),
    )(y)


def _glu_post(y, gl, proj, b_glu, zb_col, name):
    l, w = y.shape
    tr = _tile(l, 512)

    def body(y_ref, gl_ref, zb_ref, b_ref, o_ref):
        zb = zb_ref[...]
        o_ref[...] = (_gelu(y_ref[...]) * _sigmoid(gl_ref[...] + b_ref[...])
                      * (zb * _sigmoid(zb))).astype(o_ref.dtype)

    return pl.pallas_call(
        body, name=name, grid=(l // tr,),
        in_specs=[_row_spec(tr, w, 0), _row_spec(tr, w, 0), _row_spec(tr, w, zb_col), _full_spec((1, w))],
        out_specs=_row_spec(tr, w, 0), out_shape=jax.ShapeDtypeStruct((l, w), BF16),
        compiler_params=_params("parallel"),
    )(y, gl, proj, b_glu.reshape(1, w))


def _glu_bwd1(y, gl, proj, b_glu, dpb, zb_col, name):
    l, w = y.shape
    tr = _tile(l, 512)

    def body(y_ref, gl_ref, zb_ref, b_ref, dpb_ref, dzb_ref, dgl_ref, t_ref, db_ref):
        zb = zb_ref[...]
        dpb_v = dpb_ref[...]
        yg = _gelu(y_ref[...])
        sgl = _sigmoid(gl_ref[...] + b_ref[...])
        szb = _sigmoid(zb)
        dzb_ref[...] = (dpb_v * yg * sgl * (szb * (1.0 + zb * (1.0 - szb)))).astype(dzb_ref.dtype)
        e = dpb_v * (zb * szb)
        dgl = e * yg * sgl * (1.0 - sgl)
        dgl_ref[...] = dgl.astype(dgl_ref.dtype)
        t_ref[...] = e * sgl

        @pl.when(pl.program_id(0) == 0)
        def _():
            db_ref[...] = jnp.zeros_like(db_ref)

        db_ref[...] += jnp.sum(dgl, axis=0, keepdims=True)

    return pl.pallas_call(
        body, name=name, grid=(l // tr,),
        in_specs=[_row_spec(tr, w, 0), _row_spec(tr, w, 0), _row_spec(tr, w, zb_col), _full_spec((1, w)),
                  _row_spec(tr, w, 0)],
        out_specs=[_row_spec(tr, w, 0)] * 3 + [_full_spec((1, w))],
        out_shape=[jax.ShapeDtypeStruct((l, w), BF16), jax.ShapeDtypeStruct((l, w), BF16),
                   jax.ShapeDtypeStruct((l, w), F32), jax.ShapeDtypeStruct((1, w), F32)],
        compiler_params=_params("arbitrary"),
    )(y, gl, proj, b_glu.reshape(1, w), dpb)


def _glu_bwd2(y, t1, dyg2, name):
    l, w = y.shape
    tr = _tile(l, 512)

    def body(y_ref, t_ref, d_ref, o_ref):
        o_ref[...] = (t_ref[...] + d_ref[...]) * _gelu_grad(y_ref[...])

    return pl.pallas_call(
        body, name=name, grid=(l // tr,), in_specs=[_row_spec(tr, w, 0)] * 3,
        out_specs=_row_spec(tr, w, 0), out_shape=jax.ShapeDtypeStruct((l, w), F32),
        compiler_params=_params("parallel"),
    )(y, t1, dyg2)


def _merge_fwd(proj, ya, yb, d, ga_col, gb_col, name):
    l = proj.shape[0]
    tr = _tile(l, 256)

    def body(ga_ref, gb_ref, ya_ref, yb_ref, o_ref):
        o_ref[...] = (_sigmoid(ga_ref[...]) * ya_ref[...]
                      + _sigmoid(gb_ref[...]) * yb_ref[...]).astype(o_ref.dtype)

    return pl.pallas_call(
        body, name=name, grid=(l // tr,),
        in_specs=[_row_spec(tr, d, ga_col), _row_spec(tr, d, gb_col), _row_spec(tr, d, 0), _row_spec(tr, d, 0)],
        out_specs=_row_spec(tr, d, 0), out_shape=jax.ShapeDtypeStruct((l, d), BF16),
        compiler_params=_params("parallel"),
    )(proj, proj, ya, yb)


def _merge_bwd(proj, ya, yb, dm, d, ga_col, gb_col, name):
    l = proj.shape[0]
    tr = _tile(l, 256)

    def body(ga_ref, gb_ref, ya_ref, yb_ref, dm_ref, dya_ref, dyb_ref, dga_ref, dgb_ref):
        dmv = dm_ref[...]
        sa = _sigmoid(ga_ref[...])
        sb = _sigmoid(gb_ref[...])
        dya_ref[...] = (dmv * sa).astype(dya_ref.dtype)
        dyb_ref[...] = (dmv * sb).astype(dyb_ref.dtype)
        dga_ref[...] = (dmv * ya_ref[...] * sa * (1.0 - sa)).astype(dga_ref.dtype)
        dgb_ref[...] = (dmv * yb_ref[...] * sb * (1.0 - sb)).astype(dgb_ref.dtype)

    act = jax.ShapeDtypeStruct((l, d), BF16)
    return pl.pallas_call(
        body, name=name, grid=(l // tr,),
        in_specs=[_row_spec(tr, d, ga_col), _row_spec(tr, d, gb_col), _row_spec(tr, d, 0), _row_spec(tr, d, 0),
                  _row_spec(tr, d, 0)],
        out_specs=[_row_spec(tr, d, 0)] * 4, out_shape=[act] * 4,
        compiler_params=_params("parallel"),
    )(proj, proj, ya, yb, dm)


def _to_segments(a):
    l, w = a.shape
    return a.reshape(SUBLANES, l // SUBLANES, w).transpose(1, 0, 2).reshape(l, w)


def _from_segments(a):
    l, w = a.shape
    return a.reshape(l // SUBLANES, SUBLANES, w).transpose(1, 0, 2).reshape(l, w)


def _dense(z, shape):
    return jnp.broadcast_to(z, shape).reshape(-1, LANES)


def _s5_disc(are, aim, ldt):
    dt = jnp.exp(ldt)
    er = jnp.exp(are * dt)
    lbr = er * jnp.cos(aim * dt)
    lbi = er * jnp.sin(aim * dt)
    inv = 1.0 / (are * are + aim * aim)
    fr = ((lbr - 1.0) * are + lbi * aim) * inv
    fi = (lbi * are - (lbr - 1.0) * aim) * inv
    return dt, lbr, lbi, inv, fr, fi


def _s5_params(are, aim, ldt, bre, bim, name):
    shape = are.shape

    def body(are_ref, aim_ref, ldt_ref, bre_ref, bim_ref, lbr_ref, lbi_ref, bbr_ref, bbi_ref):
        _, lbr, lbi, _, fr, fi = _s5_disc(are_ref[...], aim_ref[...], ldt_ref[...])
        lbr_ref[...] = lbr
        lbi_ref[...] = lbi
        bbr_ref[...] = fr * bre_ref[...] - fi * bim_ref[...]
        bbi_ref[...] = fr * bim_ref[...] + fi * bre_ref[...]

    out = jax.ShapeDtypeStruct(shape, F32)
    return pl.pallas_call(body, name=name, out_shape=[out] * 4,
                          compiler_params=pltpu.CompilerParams(vmem_limit_bytes=VMEM_LIMIT_BYTES),
                          )(are, aim, ldt, bre, bim)


def _s5_params_bwd(are, aim, ldt, bre, bim, glbr, glbi, gbbr, gbbi, n_groups, name):
    shape = are.shape
    rows_per_group = shape[0] // n_groups

    def body(are_ref, aim_ref, ldt_ref, bre_ref, bim_ref, glbr_ref, glbi_ref, gbbr_ref, gbbi_ref,
             gar_ref, gai_ref, gdt_ref, gbr_ref, gbi_ref):
        are_v, aim_v = are_ref[...], aim_ref[...]
        bre_v, bim_v = bre_ref[...], bim_ref[...]
        gbbr_v, gbbi_v = gbbr_ref[...], gbbi_ref[...]
        dt, lbr, lbi, inv, fr, fi = _s5_disc(are_v, aim_v, ldt_ref[...])
        gbr_ref[...] = fr * gbbr_v + fi * gbbi_v
        gbi_ref[...] = fr * gbbi_v - fi * gbbr_v
        lane_group = lax.broadcasted_iota(jnp.int32, (LANES, LANES), 0) // S5_GROUP
        same_group = (lane_group == lax.broadcasted_iota(jnp.int32, (LANES, LANES), 1) // S5_GROUP)
        ones = same_group.astype(F32)
        gfr = jnp.dot(bre_v * gbbr_v + bim_v * gbbi_v, ones, precision=HIGHEST, preferred_element_type=F32)
        gfi = jnp.dot(bre_v * gbbi_v - bim_v * gbbr_v, ones, precision=HIGHEST, preferred_element_type=F32)
        glr = glbr_ref[...] + (are_v * gfr - aim_v * gfi) * inv
        gli = glbi_ref[...] + (are_v * gfi + aim_v * gfr) * inv
        qr = (fr * are_v + fi * aim_v) * inv
        qi = (fi * are_v - fr * aim_v) * inv
        gzr = lbr * glr + lbi * gli
        gzi = lbr * gli - lbi * glr
        gar_ref[...] = dt * gzr - (qr * gfr + qi * gfi)
        gai_ref[...] = dt * gzi - (qr * gfi - qi * gfr)
        e = dt * (are_v * gzr + aim_v * gzi)
        per_group = jnp.sum(e.reshape(n_groups, rows_per_group, LANES), axis=1)
        total = jnp.sum(per_group, axis=1, keepdims=True) * (1.0 / S5_GROUP)
        gdt_ref[...] = jnp.broadcast_to(total, gdt_ref.shape)

    out = jax.ShapeDtypeStruct(shape, F32)
    return pl.pallas_call(
        body, name=name,
        out_shape=[out, out, jax.ShapeDtypeStruct((n_groups, LANES), F32), out, out],
        compiler_params=pltpu.CompilerParams(vmem_limit_bytes=VMEM_LIMIT_BYTES),
    )(are, aim, ldt, bre, bim, glbr, glbi, gbbr, gbbi)


def _cmul(ar, ai, br, bi):
    return ar * br - ai * bi, ar * bi + ai * br


def _scan_in_place(hr_ref, hi_ref, lr, li, reverse):
    l, wb = hr_ref.shape
    nt = l // SUBLANES
    shift = SUBLANES - 1 if reverse else 1
    unroll = 8 if nt % 8 == 0 else 1

    def rows(k):
        t = (nt - 1 - k) if reverse else k
        return pl.ds(pl.multiple_of(t * SUBLANES, SUBLANES), SUBLANES)

    zero = jnp.zeros((SUBLANES, wb), F32)
    one = jnp.ones((SUBLANES, wb), F32)

    def local_step(k, carry):
        hr, hi, pr, pi = carry
        r = rows(k)
        tr_, ti_ = _cmul(lr, li, hr, hi)
        hr, hi = tr_ + hr_ref[r, :], ti_ + hi_ref[r, :]
        hr_ref[r, :] = hr
        hi_ref[r, :] = hi
        pr, pi = _cmul(lr, li, pr, pi)
        return hr, hi, pr, pi

    er, ei, lnr, lni = lax.fori_loop(0, nt, local_step, (zero, zero, one, zero), unroll=unroll)

    row = lax.broadcasted_iota(jnp.int32, (SUBLANES, wb), 0)
    tr_, ti_ = er, ei
    for j in range(1, SUBLANES):
        pr_, pi_ = _cmul(lnr, lni, pltpu.roll(tr_, shift, axis=0), pltpu.roll(ti_, shift, axis=0))
        at = row == ((SUBLANES - 1 - j) if reverse else j)
        tr_ = jnp.where(at, er + pr_, tr_)
        ti_ = jnp.where(at, ei + pi_, ti_)
    edge = row == ((SUBLANES - 1) if reverse else 0)
    cr = jnp.where(edge, 0.0, pltpu.roll(tr_, shift, axis=0))
    ci = jnp.where(edge, 0.0, pltpu.roll(ti_, shift, axis=0))

    def fix_step(k, carry):
        pr, pi = carry
        pr, pi = _cmul(lr, li, pr, pi)
        r = rows(k)
        ar_, ai_ = _cmul(pr, pi, cr, ci)
        hr_ref[r, :] = hr_ref[r, :] + ar_
        hi_ref[r, :] = hi_ref[r, :] + ai_
        return pr, pi

    lax.fori_loop(0, nt, fix_step, (one, zero), unroll=unroll)


def _dot(a, b):
    return jnp.dot(a.astype(BF16), b.astype(BF16), preferred_element_type=F32)


def _s5_forward(u_seg, mb_re, mb_im, mc_re, mc_im, lam_re, lam_im, dvec, name):
    l = u_seg.shape[0]
    nb, kin, kst = mb_re.shape

    def body(u_ref, mbr_ref, mbi_ref, mcr_ref, mci_ref, lr_ref, li_ref, d_ref, hr_ref, hi_ref, y_ref):
        u = u_ref[...]
        hr_ref[...] = _dot(u, mbr_ref[...])
        hi_ref[...] = _dot(u, mbi_ref[...])
        _scan_in_place(hr_ref, hi_ref, jnp.broadcast_to(lr_ref[...], (SUBLANES, kst)),
                       jnp.broadcast_to(li_ref[...], (SUBLANES, kst)), False)
        y_ref[...] = _dot(hr_ref[...], mcr_ref[...]) - _dot(hi_ref[...], mci_ref[...]) + d_ref[...] * u

    act = pl.BlockSpec((l, kin), lambda b: (0, b))
    state = pl.BlockSpec((l, kst), lambda b: (0, b))
    up = pl.BlockSpec((None, kin, kst), lambda b: (b, 0, 0))
    down = pl.BlockSpec((None, kst, kin), lambda b: (b, 0, 0))
    hshape = jax.ShapeDtypeStruct((l, nb * kst), F32)
    return pl.pallas_call(
        body, name=name, grid=(nb,),
        in_specs=[act, up, up, down, down, pl.BlockSpec((1, kst), lambda b: (0, b)),
                  pl.BlockSpec((1, kst), lambda b: (0, b)), pl.BlockSpec((1, kin), lambda b: (0, b))],
        out_specs=[state, state, act],
        out_shape=[hshape, hshape, jax.ShapeDtypeStruct((l, nb * kin), F32)],
        compiler_params=_params("parallel"),
    )(u_seg, mb_re, mb_im, mc_re, mc_im, lam_re, lam_im, dvec)


def _s5_backward(dy_seg, u_seg, dy_t, u_t, h_re, h_im, mg_re, mg_im, md_re, md_im, lam_re, lam_im_neg, dvec, name):
    l = dy_seg.shape[0]
    nb, kin, kst = mg_re.shape
    nt = l // SUBLANES

    def body(dy_ref, u_ref, dyt_ref, ut_ref, hr_ref, hi_ref, mgr_ref, mgi_ref, mdr_ref, mdi_ref, lr_ref, li_ref,
             d_ref, du_ref, gcr_ref, gci_ref, gbr_ref, gbi_ref, glr_ref, gli_ref, dsk_ref, qr_ref, qi_ref):
        dy = dy_ref[...]
        qr_ref[...] = _dot(dy, mgr_ref[...])
        qi_ref[...] = _dot(dy, mgi_ref[...])
        _scan_in_place(qr_ref, qi_ref, jnp.broadcast_to(lr_ref[...], (SUBLANES, kst)),
                       jnp.broadcast_to(li_ref[...], (SUBLANES, kst)), True)
        du_ref[...] = _dot(qr_ref[...], mdr_ref[...]) - _dot(qi_ref[...], mdi_ref[...]) + d_ref[...] * dy
        dsk_ref[...] = jnp.sum(dy * u_ref[...], axis=0, keepdims=True)
        gcr_ref[...] = _dot(dyt_ref[...], hr_ref[...])
        gci_ref[...] = _dot(dyt_ref[...], hi_ref[...])
        gbr_ref[...] = _dot(ut_ref[...], qr_ref[...])
        gbi_ref[...] = _dot(ut_ref[...], qi_ref[...])

        row = lax.broadcasted_iota(jnp.int32, (SUBLANES, kst), 0)
        last = pl.ds((nt - 1) * SUBLANES, SUBLANES)
        first = pl.ds(0, SUBLANES)
        pr = jnp.where(row == 0, 0.0, pltpu.roll(hr_ref[last, :], 1, axis=0))
        pi = jnp.where(row == 0, 0.0, pltpu.roll(hi_ref[last, :], 1, axis=0))
        gr, gi = qr_ref[first, :], qi_ref[first, :]

        def step(t, carry):
            acc_r, acc_i = carry
            cur = pl.ds(pl.multiple_of(t * SUBLANES, SUBLANES), SUBLANES)
            prev = pl.ds(pl.multiple_of((t - 1) * SUBLANES, SUBLANES), SUBLANES)
            gr, gi = qr_ref[cur, :], qi_ref[cur, :]
            pr, pi = hr_ref[prev, :], hi_ref[prev, :]
            return acc_r + gr * pr + gi * pi, acc_i + gi * pr - gr * pi

        acc_r, acc_i = lax.fori_loop(1, nt, step, (gr * pr + gi * pi, gi * pr - gr * pi))
        glr_ref[...] = jnp.sum(acc_r, axis=0, keepdims=True)
        gli_ref[...] = jnp.sum(acc_i, axis=0, keepdims=True)

    act = pl.BlockSpec((l, kin), lambda b: (0, b))
    act_t = pl.BlockSpec((kin, l), lambda b: (b, 0))
    state = pl.BlockSpec((l, kst), lambda b: (0, b))
    up = pl.BlockSpec((None, kin, kst), lambda b: (b, 0, 0))
    down = pl.BlockSpec((None, kst, kin), lambda b: (b, 0, 0))
    vec_st = pl.BlockSpec((1, kst), lambda b: (0, b))
    vec_in = pl.BlockSpec((1, kin), lambda b: (0, b))
    outer = jax.ShapeDtypeStruct((nb, kin, kst), F32)
    lam_shape = jax.ShapeDtypeStruct((1, nb * kst), F32)
    return pl.pallas_call(
        body, name=name, grid=(nb,),
        in_specs=[act, act, act_t, act_t, state, state, up, up, down, down, vec_st, vec_st, vec_in],
        out_specs=[act, up, up, up, up, vec_st, vec_st, vec_in],
        out_shape=[jax.ShapeDtypeStruct((l, nb * kin), F32), outer, outer, outer, outer, lam_shape, lam_shape,
                   jax.ShapeDtypeStruct((1, nb * kin), F32)],
        scratch_shapes=[pltpu.VMEM((l, kst), F32), pltpu.VMEM((l, kst), F32)],
        compiler_params=_params("parallel"),
    )(dy_seg, u_seg, dy_t, u_t, h_re, h_im, mg_re, mg_im, md_re, md_im, lam_re, lam_im_neg, dvec)


def _block_diag(m, nb):
    g, r, s = m.shape
    gb = g // nb
    eye = jnp.eye(gb, dtype=m.dtype)
    out = m.reshape(nb, gb, r, 1, s) * eye[None, :, None, :, None]
    return out.reshape(nb, gb * r, gb * s)


def _block_diag_extract(mat, g, r, s):
    nb = mat.shape[0]
    gb = g // nb
    eye = jnp.eye(gb, dtype=mat.dtype)
    m5 = mat.reshape(nb, gb, r, gb, s) * eye[None, :, None, :, None]
    return jnp.sum(m5, axis=3).reshape(g, r, s)


def _adamw(w, m, v, gslots, name, layer=0, prev=None):
    layers, r, c = w.shape
    s = gslots.shape[0]
    tr = _tile(r, max(SUBLANES, 1 << int(math.log2(ADAMW_BLOCK_ELEMS // c))))
    bc1 = 1.0 / (1.0 - ADAM_B1 ** ADAM_STEP)
    bc2 = 1.0 / (1.0 - ADAM_B2 ** ADAM_STEP)

    def body(w_ref, m_ref, v_ref, g_ref, *rest):
        go_ref, d_ref, mo_ref, vo_ref = rest[-4:]
        g = g_ref[0].astype(F32)
        for k in range(1, s):
            g = g + g_ref[k].astype(F32)
        mn = ADAM_B1 * m_ref[...] + (1.0 - ADAM_B1) * g
        vn = ADAM_B2 * v_ref[...] + (1.0 - ADAM_B2) * (g * g)
        go_ref[...] = g
        mo_ref[...] = mn
        vo_ref[...] = vn
        d_ref[...] = -ADAM_LR * ((mn * bc1) / (jnp.sqrt(vn * bc2) + ADAM_EPS) + ADAM_WD * w_ref[...])

    spec = pl.BlockSpec((None, tr, c), lambda i: (layer, i, 0))
    out = jax.ShapeDtypeStruct((layers, r, c), F32)
    in_specs = [spec, spec, spec, pl.BlockSpec((s, tr, c), lambda i: (0, i, 0))]
    args = [w, m, v, gslots]
    aliases = {}
    if prev is not None:
        in_specs += [pl.BlockSpec(memory_space=pl.ANY)] * 4
        args += list(prev)
        aliases = {4 + q: q for q in range(4)}
    return pl.pallas_call(
        body, name=name, grid=(r // tr,), in_specs=in_specs,
        out_specs=[spec] * 4, out_shape=[out] * 4, input_output_aliases=aliases,
        compiler_params=_params("parallel"),
    )(*args)


def _pack(parts):
    flat = jnp.concatenate([p.reshape(-1) for p in parts])
    pad = (-flat.shape[0]) % (PACK_ROWS * LANES)
    return jnp.pad(flat, (0, pad)).reshape(-1, LANES)


def _unpack(packed, shapes):
    flat = packed.reshape(-1)
    out, off = [], 0
    for shp in shapes:
        size = math.prod(shp)
        out.append(flat[off:off + size].reshape(shp))
        off += size
    return out


def kernel(x, norm_g, w_in, conv_w, w_out_a, a_re, a_im, log_dt, b_re, b_im, c_re, c_im, d_skip, w_glu, b_glu, w_out_b, w_o, final_g, loss_target, m_norm_g, m_w_in, m_conv_w, m_w_out_a, m_a_re, m_a_im, m_log_dt, m_b_re, m_b_im, m_c_re, m_c_im, m_d_skip, m_w_glu, m_b_glu, m_w_out_b, m_w_o, m_final_g, v_norm_g, v_w_in, v_conv_w, v_w_out_a, v_a_re, v_a_im, v_log_dt, v_b_re, v_b_im, v_c_re, v_c_im, v_d_skip, v_w_glu, v_b_glu, v_w_out_b, v_w_o, v_final_g):
    depth = norm_g.shape[0]
    l, d = x.shape[1], x.shape[2]
    ws = w_glu.shape[2]
    n_groups, n_state = a_re.shape[1], a_re.shape[2]
    nb = ws // LANES
    assert S5_GROUP == b_re.shape[3] and n_state * S5_GB == 4 * LANES
    u_col, zb_col = 4 * d // ws, 4 * d // ws + 1
    ga_col, gb_col = (4 * d + 2 * ws) // d, (4 * d + 2 * ws) // d + 1
    me = 4 * lax.axis_index("x") + 2 * lax.axis_index("y") + lax.axis_index("c")

    xs = [x[0]]
    tgt = loss_target[0]

    conv_full = _all_gather(jnp.pad(conv_w.reshape(depth * 3, -1), ((0, SUBLANES - depth * 3), (0, 0))),
                            "ag_conv_w")
    conv_full = conv_full.transpose(1, 0, 2).reshape(SUBLANES, d)[:depth * 3].reshape(depth, 3, d)
    big_names = ("w_in", "w_out_a", "w_glu", "w_out_b", "w_o")
    big = dict(w_in=(w_in, m_w_in, v_w_in), w_out_a=(w_out_a, m_w_out_a, v_w_out_a),
               w_glu=(w_glu, m_w_glu, v_w_glu), w_out_b=(w_out_b, m_w_out_b, v_w_out_b),
               w_o=(w_o, m_w_o, v_w_o))

    def shards_bf16(i):
        return [big[k][0][i].astype(BF16) for k in big_names]

    def start_gather(shards, name):
        return _exchange_start(_plan_gather_chips, shards, [_landing(s_, N_DEV, me) for s_ in shards], name)

    first = shards_bf16(0)
    w_in_first = _all_gather(first[0], "ag_w_in_0")
    sems, srcs, lands, token = start_gather(_after(first[1:], w_in_first), "ag_rest_start_0")
    rest_pending = (sems, srcs, lands)
    norm_g_first = _after(norm_g[0], token)
    gathered, ag_pending = {}, {}

    def as_operands(gath):
        return dict(w_in=gath[0], w_a=gath[1].reshape(d, d), w_glu=gath[2].reshape(ws, ws),
                    w_b=gath[3], w_o=gath[4].reshape(d, d))

    saved = []
    wg = [None] * depth
    for i in range(depth):
        xi = xs[-1]
        if i in ag_pending:
            sems, _, lands = ag_pending[i]
            _, gathered[i] = _exchange_wait(_plan_gather_forward, sems, None, lands, xi, f"ag_forward_wait_{i}")
        conv8 = jnp.pad(conv_full[i], ((0, SUBLANES - 3), (0, 0)))
        h = _rmsnorm_fwd(xi, norm_g_first if i == 0 else norm_g[i], f"rmsnorm_fwd_{i}")
        proj = _mm_win_fwd(h, w_in_first if i == 0 else gathered[i][0], f"mm_proj_{i}")
        token = None
        if i == 0:
            sems, srcs, lands = rest_pending
            _, lands = _exchange_wait(_plan_gather_chips, sems, srcs, lands, proj, "ag_rest_wait_0")
            sems, _, lands, token = _exchange_start(_plan_gather_forward, None, lands, "ag_rest_forward_start_0")
            rest_pending = (sems, lands)
        if i + 1 < depth:
            sems, srcs, lands, token = start_gather(
                _after(shards_bf16(i + 1), proj if token is None else token), f"ag_start_{i + 1}")
            ag_pending[i + 1] = (sems, srcs, lands)
        if token is not None:
            proj = _after(proj, token)
        pa = _branch_a_fwd(proj, conv8, d, f"branch_a_fwd_{i}")
        if i == 0:
            sems, lands = rest_pending
            _, rest = _exchange_wait(_plan_gather_forward, sems, None, lands, pa, "ag_rest_forward_wait_0")
            gathered[0] = [w_in_first] + list(rest)
        wg[i] = g = as_operands(gathered[i])
        ya = _mm(pa, g["w_a"], name=f"mm_ya_{i}")
        shape3 = (n_groups, n_state, S5_GROUP)
        dense_in = (_dense(a_re[i][:, :, None], shape3), _dense(a_im[i][:, :, None], shape3),
                    _dense(log_dt[i][:, None, None], shape3), b_re[i].reshape(-1, LANES), b_im[i].reshape(-1, LANES))
        lbr, lbi, bbr, bbi = _s5_params(*dense_in, f"s5_params_{i}")
        lam_re = lbr.reshape(shape3)[:, :, 0].reshape(1, -1)
        lam_im = lbi.reshape(shape3)[:, :, 0].reshape(1, -1)
        bbr3, bbi3 = bbr.reshape(shape3), bbi.reshape(shape3)
        u_seg = _to_segments(proj[:, 4 * d:4 * d + ws])
        h_re, h_im, y_seg = _s5_forward(
            u_seg, _block_diag(bbr3.transpose(0, 2, 1), nb).astype(BF16),
            _block_diag(bbi3.transpose(0, 2, 1), nb).astype(BF16),
            _block_diag(c_re[i].transpose(0, 2, 1), nb).astype(BF16),
            _block_diag(c_im[i].transpose(0, 2, 1), nb).astype(BF16),
            lam_re, lam_im, d_skip[i].reshape(1, ws), f"s5_forward_{i}")
        y = _from_segments(y_seg)
        yg = _gelu_cast(y, f"gelu_{i}")
        gl = _mm(yg, g["w_glu"], name=f"mm_glu_{i}")
        pb = _glu_post(y, gl, proj, b_glu[i], zb_col, f"glu_post_{i}")
        w_b2d = g["w_b"].transpose(1, 0, 2).reshape(ws, d)
        yb = _mm(pb, w_b2d, name=f"mm_yb_{i}")
        mrg = _merge_fwd(proj, ya, yb, d, ga_col, gb_col, f"merge_fwd_{i}")
        if i + 1 in ag_pending:
            sems, srcs, lands = ag_pending[i + 1]
            _, lands = _exchange_wait(_plan_gather_chips, sems, srcs, lands, mrg, f"ag_wait_{i + 1}")
            sems, _, lands, token = _exchange_start(_plan_gather_forward, None, lands, f"ag_forward_start_{i + 1}")
            ag_pending[i + 1] = (sems, None, lands)
            mrg = _after(mrg, token)
        xs.append(_mm(mrg, g["w_o"], name=f"mm_out_{i}", add=xi))
        saved.append(dict(h=h, proj=proj, pa=pa, ya=ya, yb=yb, y=y, yg=yg, gl=gl, pb=pb, mrg=mrg,
                          u_seg=u_seg, h_re=h_re, h_im=h_im, lam_re=lam_re, lam_im=lam_im,
                          bbr3=bbr3, bbi3=bbi3, dense_in=dense_in, conv8=conv8, w_b2d=w_b2d))

    dx, g_final, loss_part = _final_loss(xs[-1], final_g, tgt, "final_loss")
    loss = lax.psum(loss_part[0, 0], MESH_AXES)

    rs_pending = []
    small = {k: [None] * depth for k in ("norm_g", "a_re", "a_im", "log_dt", "b_re", "b_im", "c_re", "c_im",
                                         "d_skip", "b_glu", "conv_w")}

    my_chip = 2 * lax.axis_index("x") + lax.axis_index("y")

    def reduce_on_chip(pieces, tag):
        lands = [lax.empty((4,) + p.shape[1:], p.dtype) for p in pieces]
        sems, srcs, lands, token = _exchange_start(_plan_reduce_sibling, pieces, lands, f"rs_sibling_start_{tag}")
        return (sems, srcs, lands), token

    def reduce_across_chips(names_, state, layer, tag, after):
        sems, srcs, lands = state
        srcs, lands = _exchange_wait(_plan_reduce_sibling, sems, srcs, lands, after, f"rs_sibling_wait_{tag}")
        sums = [_chip_sums(p, l_, f"chip_sum_{k}_{layer}") for k, p, l_ in zip(names_, srcs, lands)]
        lands = [_landing(lax.dynamic_index_in_dim(s_, my_chip, 0, keepdims=False), 4, my_chip) for s_ in sums]
        sems, srcs, lands, token = _exchange_start(_plan_reduce_chips, sums, lands, f"rs_chips_start_{tag}")
        rs_pending.append((names_, layer, sems, srcs, lands, f"rs_chips_wait_{tag}"))
        return token

    for i in reversed(range(depth)):
        s, g = saved[i], wg[i]
        proj = s["proj"]
        dxo_b = dx.astype(BF16)
        dm = _mm(dxo_b, g["w_o"], name=f"mm_dm_{i}", nt=True)
        gw_o = _mm(s["mrg"].T, dxo_b, name=f"mm_gw_o_{i}", out_dtype=BF16)
        dya, dyb, dga, dgb = _merge_bwd(proj, s["ya"], s["yb"], dm, d, ga_col, gb_col, f"merge_bwd_{i}")
        dpa = _mm(dya, g["w_a"], name=f"mm_dpa_{i}", nt=True)
        gw_a = _mm(s["pa"].T, dya, name=f"mm_gw_a_{i}", out_dtype=BF16)
        dpb = _mm(dyb, s["w_b2d"], name=f"mm_dpb_{i}", nt=True)
        gw_b = _mm(s["pb"].T, dyb, name=f"mm_gw_b_{i}", split_n=N_DEV, out_dtype=BF16)
        dv, dbg, dcg, dza, dw0, dw1, dw2 = _branch_a_bwd(proj, dpa, s["conv8"], d, f"branch_a_bwd_{i}")
        small["conv_w"][i] = jnp.concatenate([dw0, dw1, dw2], axis=0)
        dzb, dgl, t1, db_glu = _glu_bwd1(s["y"], s["gl"], proj, b_glu[i], dpb, zb_col, f"glu_bwd1_{i}")
        small["b_glu"][i] = db_glu.reshape(ws)
        dyg2 = _mm(dgl, g["w_glu"], name=f"mm_dyg_{i}", nt=True)
        gw_glu = _mm(s["yg"].T, dgl, name=f"mm_gw_glu_{i}", out_dtype=BF16)
        small_names_ = ("w_out_a", "w_glu", "w_out_b", "w_o")
        state, token = reduce_on_chip(
            [gw_a.reshape(N_DEV, d // N_DEV, d), gw_glu.reshape(N_DEV, ws // N_DEV, ws), gw_b,
             gw_o.reshape(N_DEV, d // N_DEV, d)], f"small_{i}")
        dy = _glu_bwd2(s["y"], _after(t1, token), dyg2, f"glu_bwd2_{i}")
        dy_seg = _to_segments(dy)
        u_seg = s["u_seg"]
        du_seg, gc_re, gc_im, gbb_re, gbb_im, glam_re, glam_im, dskip = _s5_backward(
            dy_seg, u_seg, dy_seg.T, u_seg.T, s["h_re"], s["h_im"],
            _block_diag(c_re[i], nb).astype(BF16), _block_diag(-c_im[i], nb).astype(BF16),
            _block_diag(s["bbr3"], nb).astype(BF16), _block_diag(-s["bbi3"], nb).astype(BF16),
            s["lam_re"], -s["lam_im"], d_skip[i].reshape(1, ws), f"s5_backward_{i}")
        token = reduce_across_chips(small_names_, state, i, f"small_{i}", du_seg)
        small["d_skip"][i] = dskip.reshape(n_groups, S5_GROUP)
        small["c_re"][i] = _block_diag_extract(gc_re, n_groups, S5_GROUP, n_state)
        small["c_im"][i] = -_block_diag_extract(gc_im, n_groups, S5_GROUP, n_state)
        gbb_re = _block_diag_extract(gbb_re, n_groups, S5_GROUP, n_state).transpose(0, 2, 1)
        gbb_im = _block_diag_extract(gbb_im, n_groups, S5_GROUP, n_state).transpose(0, 2, 1)
        glam_re = _after(glam_re, token)
        shape3 = (n_groups, n_state, S5_GROUP)
        gar, gai, gdt, gbr, gbi = _s5_params_bwd(
            *s["dense_in"], _dense(glam_re.reshape(n_groups, n_state, 1), shape3),
            _dense(glam_im.reshape(n_groups, n_state, 1), shape3),
            gbb_re.reshape(-1, LANES), gbb_im.reshape(-1, LANES), n_groups, f"s5_params_bwd_{i}")
        small["a_re"][i] = gar.reshape(shape3)[:, :, 0]
        small["a_im"][i] = gai.reshape(shape3)[:, :, 0]
        small["log_dt"][i] = gdt[:, 0]
        small["b_re"][i] = gbr.reshape(shape3)
        small["b_im"][i] = gbi.reshape(shape3)
        du = _from_segments(du_seg).astype(BF16)
        dproj = jnp.concatenate([dv, dbg, dcg, dza, du, dzb, dga, dgb], axis=1)
        gw_in = _mm(s["h"].T, dproj, name=f"mm_gw_in_{i}", split_n=N_DEV, tm=1024,
                    out_dtype=BF16)
        state, token = reduce_on_chip([gw_in], f"w_in_{i}")
        dh = _mm_win_bwd(_after(dproj, token), g["w_in"], f"mm_dh_{i}")
        token = reduce_across_chips(("w_in",), state, i, f"w_in_{i}", dh)
        dx, dng = _rmsnorm_bwd(xs[i], norm_g[i], _after(dh, token), dx, f"rmsnorm_bwd_{i}")
        small["norm_g"][i] = dng.reshape(d)

    results = {}

    small_names = ("norm_g", "a_re", "a_im", "log_dt", "b_re", "b_im", "c_re", "c_im", "d_skip", "b_glu")
    small_w = dict(norm_g=(norm_g, m_norm_g, v_norm_g), a_re=(a_re, m_a_re, v_a_re), a_im=(a_im, m_a_im, v_a_im),
                   log_dt=(log_dt, m_log_dt, v_log_dt), b_re=(b_re, m_b_re, v_b_re), b_im=(b_im, m_b_im, v_b_im),
                   c_re=(c_re, m_c_re, v_c_re), c_im=(c_im, m_c_im, v_c_im), d_skip=(d_skip, m_d_skip, v_d_skip),
                   b_glu=(b_glu, m_b_glu, v_b_glu), final_g=(final_g, m_final_g, v_final_g))
    order = small_names + ("final_g", "conv_w")
    part = {k: jnp.stack(small[k]) for k in small_names + ("conv_w",)}
    part["final_g"] = g_final.reshape(d)
    shapes = [part[k].shape for k in order]
    gpack = _all_gather(_pack([part[k] for k in order]), "ag_small_grads")
    zeros_conv = jnp.zeros(part["conv_w"].shape, F32)
    wpack = _pack([small_w[k][0] for k in order[:-1]] + [zeros_conv])
    mpack = _pack([small_w[k][1] for k in order[:-1]] + [zeros_conv])
    vpack = _pack([small_w[k][2] for k in order[:-1]] + [zeros_conv])
    sres = [_unpack(p[0], shapes)
            for p in _adamw(wpack[None], mpack[None], vpack[None], gpack, "adamw_small")]
    for j, k in enumerate(order[:-1]):
        results[k] = [sres[q][j] for q in range(4)]
    dc = d // N_DEV
    gconv = lax.dynamic_slice_in_dim(sres[0][-1], me * dc, dc, axis=2)
    pad8 = lambda a: jnp.pad(a.reshape(depth * 3, dc), ((0, SUBLANES - depth * 3), (0, 0)))[None]
    cres = _adamw(pad8(conv_w), pad8(m_conv_w), pad8(v_conv_w), pad8(gconv), "adamw_conv_w")
    results["conv_w"] = [r_[0, :depth * 3].reshape(depth, 3, dc) for r_ in cres]

    after = cres[0]
    for names_, layer, sems, srcs, lands, wait_name in rs_pending:
        _, slots = _exchange_wait(_plan_reduce_chips, sems, srcs, lands, after, wait_name)
        for k, land in zip(names_, slots):
            w_, m_, v_ = big[k]
            results[k] = _adamw(w_, m_, v_, land, f"adamw_{k}_{layer}", layer=layer, prev=results.get(k))
            after = results[k][0]

    names = ("norm_g", "w_in", "conv_w", "w_out_a", "a_re", "a_im", "log_dt", "b_re", "b_im", "c_re", "c_im",
             "d_skip", "w_glu", "b_glu", "w_out_b", "w_o", "final_g")
    outs = [loss, dx[None]]
    for q in range(4):
        outs += [results[k][q] for k in names]
    return tuple(outs)
```

```python
import functools
import math

import jax
import jax.numpy as jnp
from jax import lax
from jax.experimental import pallas as pl
from jax.experimental.pallas import tpu as pltpu

F32 = jnp.float32
BF16 = jnp.bfloat16
HIGHEST = lax.Precision.HIGHEST

N_DEV = 8
MESH_AXES = ("x", "y", "c")
LANES = 128
SUBLANES = 8
VMEM_LIMIT_BYTES = 56 * 1024 * 1024

RMS_EPS = 1e-6
ADAM_LR = 0.001
ADAM_B1 = 0.9
ADAM_B2 = 0.999
ADAM_EPS = 1e-08
ADAM_WD = 0.01
ADAM_STEP = 10
GELU_C0 = math.sqrt(2.0 / math.pi)
GELU_C1 = 0.044715

ADAMW_BLOCK_ELEMS = 1 << 17
PACK_ROWS = 512

S5_GROUP = 16
S5_GB = LANES // S5_GROUP


def _params(*semantics):
    return pltpu.CompilerParams(dimension_semantics=semantics, vmem_limit_bytes=VMEM_LIMIT_BYTES)


ANY_SPEC = pl.BlockSpec(memory_space=pl.ANY)


def _pallas(body, args, deps=(), *, in_specs, **kwargs):
    deps = tuple(deps)
    if not deps:
        return pl.pallas_call(body, in_specs=in_specs, **kwargs)(*args)

    def body_after(*refs):
        body(*refs[len(deps):])

    return pl.pallas_call(body_after, in_specs=[ANY_SPEC] * len(deps) + list(in_specs), **kwargs)(*deps, *args)


def _tile(n, pref):
    t = min(n, pref)
    while n % t:
        assert t % 2 == 0, (n, pref)
        t //= 2
    return t


def _sigmoid(z):
    return 1.0 / (1.0 + jnp.exp(-z))


def _gelu(y):
    return 0.5 * y * (1.0 + jnp.tanh(GELU_C0 * (y + GELU_C1 * y * y * y)))


def _gelu_grad(y):
    t = jnp.tanh(GELU_C0 * (y + GELU_C1 * y * y * y))
    return 0.5 * (1.0 + t) + 0.5 * y * (1.0 - t * t) * GELU_C0 * (1.0 + 3.0 * GELU_C1 * y * y)


def _all_gather(shard, name):
    r, c_ = shard.shape

    def body(x_ref, out_ref, send_sems, recv_sems, local_sem):
        x, y, c = lax.axis_index("x"), lax.axis_index("y"), lax.axis_index("c")
        me, sibling = (x, y, c), (x, y, 1 - c)
        chips = [(1 - x, y), (x, 1 - y), (1 - x, 1 - y)]

        def slot(px, py, pc):
            return out_ref.at[4 * px + 2 * py + pc]

        def copy(k, block, to, src=None):
            return pltpu.make_async_remote_copy(
                src_ref=slot(*block) if src is None else src, dst_ref=slot(*block),
                send_sem=send_sems.at[k], recv_sem=recv_sems.at[k],
                device_id=to, device_id_type=pl.DeviceIdType.MESH)

        mine = pltpu.make_async_copy(x_ref, slot(*me), local_sem)
        mine.start()
        first = [copy(0, me, sibling, src=x_ref)]
        first += [copy(1 + j, me, (*chip, c), src=x_ref) for j, chip in enumerate(chips)]
        for cp in first:
            cp.start()
        passed = [copy(4 + j, (*chip, c), sibling) for j, chip in enumerate(chips)]
        for j, chip in enumerate(chips):
            copy(1 + j, (*chip, c), me).wait_recv()
            passed[j].start()
        copy(0, sibling, me).wait_recv()
        for j, chip in enumerate(chips):
            copy(4 + j, (*chip, 1 - c), me).wait_recv()
        for cp in first + passed:
            cp.wait_send()
        mine.wait()

    return pl.pallas_call(
        body, name=name,
        out_shape=jax.ShapeDtypeStruct((N_DEV, r, c_), shard.dtype),
        in_specs=[pl.BlockSpec(memory_space=pl.ANY)],
        out_specs=pl.BlockSpec(memory_space=pl.ANY),
        scratch_shapes=[pltpu.SemaphoreType.DMA((7,)), pltpu.SemaphoreType.DMA((7,)),
                        pltpu.SemaphoreType.DMA],
    )(shard)


HBM_SPEC = pl.BlockSpec(memory_space=pltpu.HBM)
SEM_SPEC = pl.BlockSpec(memory_space=pltpu.SEMAPHORE)
DATAFLOW_EFFECT = pltpu.SideEffectType.DATAFLOW_SIDE_EFFECTING
OTHER_CHIPS = (2, 4, 6)


def _flip(pos, mask):
    x, y, c = pos
    return x ^ ((mask >> 2) & 1), y ^ ((mask >> 1) & 1), c ^ (mask & 1)


def _dev(pos):
    return 4 * pos[0] + 2 * pos[1] + pos[2]


def _chip(pos):
    return 2 * pos[0] + pos[1]


def _plan_gather_chips(me):
    return [(_flip(me, k), None, _dev(me), _dev(_flip(me, k))) for k in (1,) + OTHER_CHIPS]


def _plan_gather_forward(me):
    sib = _flip(me, 1)
    return [(sib, _dev(_flip(me, k)), _dev(_flip(me, k)), _dev(_flip(sib, k))) for k in OTHER_CHIPS]


def _plan_reduce_sibling(me):
    sib = _flip(me, 1)
    return [(sib, 2 * q + sib[2], q, q) for q in range(4)]


def _plan_reduce_chips(me):
    return [(_flip(me, k), _chip(_flip(me, k)), _chip(me), _chip(_flip(me, k))) for k in OTHER_CHIPS]


PLAN_COPIES = {_plan_gather_chips: 4, _plan_gather_forward: 3, _plan_reduce_sibling: 4, _plan_reduce_chips: 3}


def _exchange_copies(plan, src_refs, land_refs, send_sems, recv_sems):
    me = (lax.axis_index("x"), lax.axis_index("y"), lax.axis_index("c"))
    pairs = []
    for b, (src_ref, land_ref) in enumerate(zip(src_refs, land_refs)):
        for j, (peer, src_slot, there, here) in enumerate(plan(me)):
            sem = b * PLAN_COPIES[plan] + j
            src = src_ref if src_slot is None else src_ref.at[src_slot]
            out = pltpu.make_async_remote_copy(
                src_ref=src, dst_ref=land_ref.at[there], send_sem=send_sems.at[sem], recv_sem=recv_sems.at[sem],
                device_id=peer, device_id_type=pl.DeviceIdType.MESH)
            inc = pltpu.make_async_remote_copy(
                src_ref=src, dst_ref=land_ref.at[here], send_sem=send_sems.at[sem], recv_sem=recv_sems.at[sem],
                device_id=peer, device_id_type=pl.DeviceIdType.MESH)
            pairs.append((out, inc))
    return pairs


def _exchange_start(plan, srcs, lands, name, deps=()):
    srcs = [] if srcs is None else list(srcs)
    ns, n, nd = len(srcs), len(lands), len(deps)

    def body(*refs):
        land_refs = refs[ns:ns + n]
        sems_at = ns + n + nd
        pairs = _exchange_copies(plan, refs[:ns] if ns else land_refs, land_refs, refs[sems_at], refs[sems_at + 1])
        for out, _ in pairs:
            out.start()
        token = refs[-1]
        token[...] = jnp.zeros_like(token)

    sems = pltpu.SemaphoreType.DMA((PLAN_COPIES[plan] * n,))
    bufs = srcs + list(lands)
    outs = pl.pallas_call(
        body, name=name,
        out_shape=(sems, sems, *[pltpu.HBM(a.shape, a.dtype) for a in bufs],
                   jax.ShapeDtypeStruct((SUBLANES, LANES), F32)),
        in_specs=[HBM_SPEC] * (ns + n) + [ANY_SPEC] * nd,
        out_specs=(SEM_SPEC, SEM_SPEC, *[HBM_SPEC] * (ns + n), pl.BlockSpec(memory_space=pltpu.VMEM)),
        input_output_aliases={i: 2 + i for i in range(ns + n)},
        compiler_params=pltpu.CompilerParams(has_side_effects=DATAFLOW_EFFECT),
    )(*[pltpu.with_memory_space_constraint(a, pltpu.HBM) for a in bufs], *deps)
    return (outs[0], outs[1]), (outs[2:2 + ns] if ns else None), outs[2 + ns:2 + ns + n], outs[-1]


def _exchange_wait(plan, sems, srcs, lands, after, name):
    srcs = [] if srcs is None else list(srcs)
    ns, n = len(srcs), len(lands)

    def body(*refs):
        land_refs = refs[ns:ns + n]
        pairs = _exchange_copies(plan, refs[:ns] if ns else land_refs, land_refs, refs[ns + n], refs[ns + n + 1])
        for out, inc in pairs:
            out.wait_send()
            inc.wait_recv()

    bufs = srcs + list(lands)
    outs = pl.pallas_call(
        body, name=name,
        out_shape=[pltpu.HBM(a.shape, a.dtype) for a in bufs],
        in_specs=[HBM_SPEC] * (ns + n) + [SEM_SPEC, SEM_SPEC] + [ANY_SPEC] * len(after),
        out_specs=[HBM_SPEC] * (ns + n),
        input_output_aliases={i: i for i in range(ns + n)},
        compiler_params=pltpu.CompilerParams(has_side_effects=DATAFLOW_EFFECT),
    )(*bufs, sems[0], sems[1], *after)
    return outs[:ns], outs[ns:]


def _landing(own, slots, slot):
    land = lax.empty((slots,) + own.shape, own.dtype)
    return lax.dynamic_update_slice(land, own[None], (slot,) + (0,) * own.ndim)


def _chip_sums(pieces, land, name):
    _, r, c_ = land.shape
    tr = _tile(r, max(2 * SUBLANES, 1 << int(math.log2(4 * ADAMW_BLOCK_ELEMS // c_))))

    def body(core_ref, p_ref, l_ref, o_ref):
        o_ref[...] = (p_ref[...].astype(F32) + l_ref[...].astype(F32)).astype(o_ref.dtype)

    spec = pl.BlockSpec((None, tr, c_), lambda q, i, core: (q, i, 0))
    return pl.pallas_call(
        body, name=name,
        grid_spec=pltpu.PrefetchScalarGridSpec(
            num_scalar_prefetch=1, grid=(4, r // tr),
            in_specs=[pl.BlockSpec((None, tr, c_), lambda q, i, core: (2 * q + core[0], i, 0)), spec],
            out_specs=spec),
        out_shape=jax.ShapeDtypeStruct(land.shape, land.dtype),
        compiler_params=_params("parallel", "parallel"),
    )(lax.axis_index("c").reshape(1), pieces, land)


def _mm(a, b, *, name, nt=False, out_dtype=F32, add=None, split_n=None, tm=512, tn=1024, deps=()):
    m, k = a.shape
    n = b.shape[0] if nt else b.shape[1]
    tm = _tile(m, tm)
    tn = n // split_n if split_n else _tile(n, tn)
    dims = (((1,), (1,)), ((), ())) if nt else (((1,), (0,)), ((), ()))

    def body(*refs):
        a_ref, b_ref = refs[0], refs[1]
        o_ref = refs[-1]
        acc = lax.dot_general(a_ref[...], b_ref[...], dims, preferred_element_type=F32)
        if add is not None:
            acc = acc + refs[2][...]
        o_ref[...] = acc.astype(o_ref.dtype)

    in_specs = [pl.BlockSpec((tm, k), lambda i, j: (i, 0)),
                pl.BlockSpec((tn, k), lambda i, j: (j, 0)) if nt
                else pl.BlockSpec((k, tn), lambda i, j: (0, j))]
    args = [a, b]
    if add is not None:
        in_specs.append(pl.BlockSpec((tm, tn), lambda i, j: (i, j)))
        args.append(add)
    if split_n:
        out_shape = jax.ShapeDtypeStruct((split_n, m, tn), out_dtype)
        out_spec = pl.BlockSpec((None, tm, tn), lambda i, j: (j, i, 0))
    else:
        out_shape = jax.ShapeDtypeStruct((m, n), out_dtype)
        out_spec = pl.BlockSpec((tm, tn), lambda i, j: (i, j))
    return _pallas(
        body, args, deps, name=name, grid=(m // tm, n // tn), in_specs=in_specs, out_specs=out_spec,
        out_shape=out_shape, compiler_params=_params("parallel", "parallel"))


def _mm_win_fwd(h, w_g, name, deps=()):
    m, k = h.shape
    nj = w_g.shape[2]
    tm = _tile(m, 512)

    def body(a_ref, b_ref, o_ref):
        o_ref[...] = jnp.dot(a_ref[...], b_ref[...], preferred_element_type=F32)

    return _pallas(
        body, [h, w_g], deps, name=name, grid=(N_DEV, m // tm),
        in_specs=[pl.BlockSpec((tm, k), lambda j, i: (i, 0)),
                  pl.BlockSpec((None, k, nj), lambda j, i: (j, 0, 0))],
        out_specs=pl.BlockSpec((tm, nj), lambda j, i: (i, j)),
        out_shape=jax.ShapeDtypeStruct((m, N_DEV * nj), F32),
        compiler_params=_params("parallel", "parallel"))


def _mm_win_bwd(dproj, w_g, name, deps=()):
    m = dproj.shape[0]
    d, nj = w_g.shape[1], w_g.shape[2]
    tm = _tile(m, 512)
    tn = _tile(d, 1024)

    def body(a_ref, b_ref, o_ref, acc_ref):
        j = pl.program_id(2)

        @pl.when(j == 0)
        def _():
            acc_ref[...] = jnp.zeros_like(acc_ref)

        acc_ref[...] += lax.dot_general(a_ref[...], b_ref[...], (((1,), (1,)), ((), ())),
                                        preferred_element_type=F32)

        @pl.when(j == N_DEV - 1)
        def _():
            o_ref[...] = acc_ref[...]

    return _pallas(
        body, [dproj, w_g], deps, name=name, grid=(m // tm, d // tn, N_DEV),
        in_specs=[pl.BlockSpec((tm, nj), lambda i, n, j: (i, j)),
                  pl.BlockSpec((None, tn, nj), lambda i, n, j: (j, n, 0))],
        out_specs=pl.BlockSpec((tm, tn), lambda i, n, j: (i, n)),
        out_shape=jax.ShapeDtypeStruct((m, d), F32),
        scratch_shapes=[pltpu.VMEM((tm, tn), F32)],
        compiler_params=_params("parallel", "parallel", "arbitrary"))


def _row_spec(tr, w, col):
    return pl.BlockSpec((tr, w), lambda i: (i, col))


def _full_spec(shape):
    return pl.BlockSpec(shape, lambda i: (0,) * len(shape))


def _rmsnorm_fwd(x, g, name, deps=()):
    l, d = x.shape
    tr = _tile(l, 256)

    def body(x_ref, g_ref, o_ref):
        xv = x_ref[...]
        rstd = lax.rsqrt(jnp.mean(xv * xv, axis=-1, keepdims=True) + RMS_EPS)
        o_ref[...] = (xv * rstd * g_ref[...]).astype(o_ref.dtype)

    return _pallas(
        body, [x, g.reshape(1, d)], deps, name=name, grid=(l // tr,),
        in_specs=[_row_spec(tr, d, 0), _full_spec((1, d))],
        out_specs=_row_spec(tr, d, 0),
        out_shape=jax.ShapeDtypeStruct((l, d), BF16),
        compiler_params=_params("parallel"))


def _rmsnorm_bwd(x, g, dh, dxo, name, deps=()):
    l, d = x.shape
    tr = _tile(l, 256)

    def body(x_ref, g_ref, dh_ref, dxo_ref, dx_ref, dg_ref):
        xv = x_ref[...]
        rstd = lax.rsqrt(jnp.mean(xv * xv, axis=-1, keepdims=True) + RMS_EPS)
        dhv = dh_ref[...]
        gdy = dhv * g_ref[...]
        dot = jnp.mean(gdy * xv, axis=-1, keepdims=True)
        dx_ref[...] = dxo_ref[...] + rstd * gdy - xv * (rstd * rstd * rstd * dot)

        @pl.when(pl.program_id(0) == 0)
        def _():
            dg_ref[...] = jnp.zeros_like(dg_ref)

        dg_ref[...] += jnp.sum(dhv * xv * rstd, axis=0, keepdims=True)

    return _pallas(
        body, [x, g.reshape(1, d), dh, dxo], deps, name=name, grid=(l // tr,),
        in_specs=[_row_spec(tr, d, 0), _full_spec((1, d)), _row_spec(tr, d, 0), _row_spec(tr, d, 0)],
        out_specs=[_row_spec(tr, d, 0), _full_spec((1, d))],
        out_shape=[jax.ShapeDtypeStruct((l, d), F32), jax.ShapeDtypeStruct((1, d), F32)],
        compiler_params=_params("arbitrary"))


def _final_loss(x, g, tgt, name):
    l, d = x.shape
    tr = _tile(l, 256)

    def body(x_ref, g_ref, t_ref, dx_ref, dg_ref, loss_ref):
        xv = x_ref[...]
        gv = g_ref[...]
        rstd = lax.rsqrt(jnp.mean(xv * xv, axis=-1, keepdims=True) + RMS_EPS)
        xn = xv * rstd
        err = xn * gv - t_ref[...]
        dy = err * (1.0 / d)
        gdy = dy * gv
        dot = jnp.mean(gdy * xv, axis=-1, keepdims=True)
        dx_ref[...] = rstd * gdy - xv * (rstd * rstd * rstd * dot)

        @pl.when(pl.program_id(0) == 0)
        def _():
            dg_ref[...] = jnp.zeros_like(dg_ref)
            loss_ref[...] = jnp.zeros_like(loss_ref)

        dg_ref[...] += jnp.sum(dy * xn, axis=0, keepdims=True)
        loss_ref[...] += (0.5 / d) * jnp.sum(err * err)

    return pl.pallas_call(
        body, name=name, grid=(l // tr,),
        in_specs=[_row_spec(tr, d, 0), _full_spec((1, d)), _row_spec(tr, d, 0)],
        out_specs=[_row_spec(tr, d, 0), _full_spec((1, d)), _full_spec((SUBLANES, LANES))],
        out_shape=[jax.ShapeDtypeStruct((l, d), F32), jax.ShapeDtypeStruct((1, d), F32),
                   jax.ShapeDtypeStruct((SUBLANES, LANES), F32)],
        compiler_params=_params("arbitrary"),
    )(x, g.reshape(1, d), tgt)


def _halo_spec(tr, w, col, nblk8, before):
    step = tr // SUBLANES
    if before:
        return pl.BlockSpec((SUBLANES, w), lambda i: (jnp.maximum(i * step - 1, 0), col))
    return pl.BlockSpec((SUBLANES, w), lambda i: (jnp.minimum((i + 1) * step, nblk8 - 1), col))


def _shift_down(cur, before, k):
    ext = jnp.concatenate([before, cur], axis=0)
    return pltpu.roll(ext, k, axis=0)[SUBLANES:, :]


def _shift_up(cur, after, k):
    tr = cur.shape[0]
    ext = jnp.concatenate([cur, after], axis=0)
    return pltpu.roll(ext, tr + SUBLANES - k, axis=0)[:tr, :]


def _branch_a_fwd(proj, conv_w, d, name, deps=()):
    l = proj.shape[0]
    tr = _tile(l, 256)
    nblk8 = l // SUBLANES

    def body(v_ref, bg_ref, cg_ref, za_ref, vh_ref, cgh_ref, w_ref, o_ref):
        first = pl.program_id(0) == 0
        cv = cg_ref[...] * v_ref[...]
        cvh = jnp.where(first, 0.0, cgh_ref[...] * vh_ref[...])
        w0, w1, w2 = w_ref[0:1, :], w_ref[1:2, :], w_ref[2:3, :]
        q = w2 * cv + w1 * _shift_down(cv, cvh, 1) + w0 * _shift_down(cv, cvh, 2)
        za = za_ref[...]
        o_ref[...] = (bg_ref[...] * q * (za * _sigmoid(za))).astype(o_ref.dtype)

    return _pallas(
        body, [proj, proj, proj, proj, proj, proj, conv_w], deps, name=name, grid=(l // tr,),
        in_specs=[_row_spec(tr, d, 0), _row_spec(tr, d, 1), _row_spec(tr, d, 2), _row_spec(tr, d, 3),
                  _halo_spec(tr, d, 0, nblk8, True), _halo_spec(tr, d, 2, nblk8, True),
                  _full_spec((SUBLANES, d))],
        out_specs=_row_spec(tr, d, 0),
        out_shape=jax.ShapeDtypeStruct((l, d), BF16),
        compiler_params=_params("parallel"))


def _branch_a_bwd(proj, dpa, conv_w, d, name):
    l = proj.shape[0]
    tr = _tile(l, 128)
    nblk8 = l // SUBLANES
    ntiles = l // tr

    def body(v_ref, bg_ref, cg_ref, za_ref, dpa_ref, vh_ref, cgh_ref, bgn_ref, zan_ref, dpan_ref,
             w_ref, dv_ref, dbg_ref, dcg_ref, dza_ref, dw0_ref, dw1_ref, dw2_ref):
        i = pl.program_id(0)
        v, bg, cg, za, dpa_v = v_ref[...], bg_ref[...], cg_ref[...], za_ref[...], dpa_ref[...]
        w0, w1, w2 = w_ref[0:1, :], w_ref[1:2, :], w_ref[2:3, :]
        cv = cg * v
        cvh = jnp.where(i == 0, 0.0, cgh_ref[...] * vh_ref[...])
        cv1 = _shift_down(cv, cvh, 1)
        cv2 = _shift_down(cv, cvh, 2)
        q = w2 * cv + w1 * cv1 + w0 * cv2
        sg = _sigmoid(za)
        s = za * sg
        dbg_ref[...] = (dpa_v * q * s).astype(dbg_ref.dtype)
        dza_ref[...] = (dpa_v * bg * q * (sg * (1.0 + za * (1.0 - sg)))).astype(dza_ref.dtype)
        dq = dpa_v * bg * s
        zan = zan_ref[...]
        dqn = jnp.where(i == ntiles - 1, 0.0, dpan_ref[...] * bgn_ref[...] * (zan * _sigmoid(zan)))
        dcv = w2 * dq + w1 * _shift_up(dq, dqn, 1) + w0 * _shift_up(dq, dqn, 2)
        dcg_ref[...] = (dcv * v).astype(dcg_ref.dtype)
        dv_ref[...] = (dcv * cg).astype(dv_ref.dtype)

        @pl.when(i == 0)
        def _():
            dw0_ref[...] = jnp.zeros_like(dw0_ref)
            dw1_ref[...] = jnp.zeros_like(dw1_ref)
            dw2_ref[...] = jnp.zeros_like(dw2_ref)

        dw0_ref[...] += jnp.sum(dq * cv2, axis=0, keepdims=True)
        dw1_ref[...] += jnp.sum(dq * cv1, axis=0, keepdims=True)
        dw2_ref[...] += jnp.sum(dq * cv, axis=0, keepdims=True)

    act = jax.ShapeDtypeStruct((l, d), BF16)
    wsum = jax.ShapeDtypeStruct((1, d), F32)
    return pl.pallas_call(
        body, name=name, grid=(ntiles,),
        in_specs=[_row_spec(tr, d, 0), _row_spec(tr, d, 1), _row_spec(tr, d, 2), _row_spec(tr, d, 3),
                  _row_spec(tr, d, 0),
                  _halo_spec(tr, d, 0, nblk8, True), _halo_spec(tr, d, 2, nblk8, True),
                  _halo_spec(tr, d, 1, nblk8, False), _halo_spec(tr, d, 3, nblk8, False),
                  _halo_spec(tr, d, 0, nblk8, False),
                  _full_spec((SUBLANES, d))],
        out_specs=[_row_spec(tr, d, 0)] * 4 + [_full_spec((1, d))] * 3,
        out_shape=[act] * 4 + [wsum] * 3,
        compiler_params=_params("arbitrary"),
    )(proj, proj, proj, proj, dpa, proj, proj, proj, proj, dpa, conv_w)


def _gelu_cast(y, name):
    l, w = y.shape
    tr = _tile(l, 512)

    def body(y_ref, o_ref):
        o_ref[...] = _gelu(y_ref[...]).astype(o_ref.dtype)

    return pl.pallas_call(
        body, name=name, grid=(l // tr,), in_specs=[_row_spec(tr, w, 0)],
        out_specs=_row_spec(tr, w, 0), out_shape=jax.ShapeDtypeStruct((l, w), BF16),
        compiler_params=_params("parallel"),
    )(y)


def _glu_post(y, gl, proj, b_glu, zb_col, name):
    l, w = y.shape
    tr = _tile(l, 512)

    def body(y_ref, gl_ref, zb_ref, b_ref, o_ref):
        zb = zb_ref[...]
        o_ref[...] = (_gelu(y_ref[...]) * _sigmoid(gl_ref[...] + b_ref[...])
                      * (zb * _sigmoid(zb))).astype(o_ref.dtype)

    return pl.pallas_call(
        body, name=name, grid=(l // tr,),
        in_specs=[_row_spec(tr, w, 0), _row_spec(tr, w, 0), _row_spec(tr, w, zb_col), _full_spec((1, w))],
        out_specs=_row_spec(tr, w, 0), out_shape=jax.ShapeDtypeStruct((l, w), BF16),
        compiler_params=_params("parallel"),
    )(y, gl, proj, b_glu.reshape(1, w))


def _glu_bwd1(y, gl, proj, b_glu, dpb, zb_col, name):
    l, w = y.shape
    tr = _tile(l, 512)

    def body(y_ref, gl_ref, zb_ref, b_ref, dpb_ref, dzb_ref, dgl_ref, t_ref, db_ref):
        zb = zb_ref[...]
        dpb_v = dpb_ref[...]
        yg = _gelu(y_ref[...])
        sgl = _sigmoid(gl_ref[...] + b_ref[...])
        szb = _sigmoid(zb)
        dzb_ref[...] = (dpb_v * yg * sgl * (szb * (1.0 + zb * (1.0 - szb)))).astype(dzb_ref.dtype)
        e = dpb_v * (zb * szb)
        dgl = e * yg * sgl * (1.0 - sgl)
        dgl_ref[...] = dgl.astype(dgl_ref.dtype)
        t_ref[...] = e * sgl

        @pl.when(pl.program_id(0) == 0)
        def _():
            db_ref[...] = jnp.zeros_like(db_ref)

        db_ref[...] += jnp.sum(dgl, axis=0, keepdims=True)

    return pl.pallas_call(
        body, name=name, grid=(l // tr,),
        in_specs=[_row_spec(tr, w, 0), _row_spec(tr, w, 0), _row_spec(tr, w, zb_col), _full_spec((1, w)),
                  _row_spec(tr, w, 0)],
        out_specs=[_row_spec(tr, w, 0)] * 3 + [_full_spec((1, w))],
        out_shape=[jax.ShapeDtypeStruct((l, w), BF16), jax.ShapeDtypeStruct((l, w), BF16),
                   jax.ShapeDtypeStruct((l, w), F32), jax.ShapeDtypeStruct((1, w), F32)],
        compiler_params=_params("arbitrary"),
    )(y, gl, proj, b_glu.reshape(1, w), dpb)


def _glu_bwd2(y, t1, dyg2, name, deps=()):
    l, w = y.shape
    tr = _tile(l, 512)

    def body(y_ref, t_ref, d_ref, o_ref):
        o_ref[...] = (t_ref[...] + d_ref[...]) * _gelu_grad(y_ref[...])

    return _pallas(
        body, [y, t1, dyg2], deps, name=name, grid=(l // tr,), in_specs=[_row_spec(tr, w, 0)] * 3,
        out_specs=_row_spec(tr, w, 0), out_shape=jax.ShapeDtypeStruct((l, w), F32),
        compiler_params=_params("parallel"))


def _merge_fwd(proj, ya, yb, d, ga_col, gb_col, name):
    l = proj.shape[0]
    tr = _tile(l, 256)

    def body(ga_ref, gb_ref, ya_ref, yb_ref, o_ref):
        o_ref[...] = (_sigmoid(ga_ref[...]) * ya_ref[...]
                      + _sigmoid(gb_ref[...]) * yb_ref[...]).astype(o_ref.dtype)

    return pl.pallas_call(
        body, name=name, grid=(l // tr,),
        in_specs=[_row_spec(tr, d, ga_col), _row_spec(tr, d, gb_col), _row_spec(tr, d, 0), _row_spec(tr, d, 0)],
        out_specs=_row_spec(tr, d, 0), out_shape=jax.ShapeDtypeStruct((l, d), BF16),
        compiler_params=_params("parallel"),
    )(proj, proj, ya, yb)


def _merge_bwd(proj, ya, yb, dm, d, ga_col, gb_col, name):
    l = proj.shape[0]
    tr = _tile(l, 256)

    def body(ga_ref, gb_ref, ya_ref, yb_ref, dm_ref, dya_ref, dyb_ref, dga_ref, dgb_ref):
        dmv = dm_ref[...]
        sa = _sigmoid(ga_ref[...])
        sb = _sigmoid(gb_ref[...])
        dya_ref[...] = (dmv * sa).astype(dya_ref.dtype)
        dyb_ref[...] = (dmv * sb).astype(dyb_ref.dtype)
        dga_ref[...] = (dmv * ya_ref[...] * sa * (1.0 - sa)).astype(dga_ref.dtype)
        dgb_ref[...] = (dmv * yb_ref[...] * sb * (1.0 - sb)).astype(dgb_ref.dtype)

    act = jax.ShapeDtypeStruct((l, d), BF16)
    return pl.pallas_call(
        body, name=name, grid=(l // tr,),
        in_specs=[_row_spec(tr, d, ga_col), _row_spec(tr, d, gb_col), _row_spec(tr, d, 0), _row_spec(tr, d, 0),
                  _row_spec(tr, d, 0)],
        out_specs=[_row_spec(tr, d, 0)] * 4, out_shape=[act] * 4,
        compiler_params=_params("parallel"),
    )(proj, proj, ya, yb, dm)


def _to_segments(a):
    l, w = a.shape
    return a.reshape(SUBLANES, l // SUBLANES, w).transpose(1, 0, 2).reshape(l, w)


def _from_segments(a):
    l, w = a.shape
    return a.reshape(l // SUBLANES, SUBLANES, w).transpose(1, 0, 2).reshape(l, w)


def _dense(z, shape):
    return jnp.broadcast_to(z, shape).reshape(-1, LANES)


def _s5_disc(are, aim, ldt):
    dt = jnp.exp(ldt)
    er = jnp.exp(are * dt)
    lbr = er * jnp.cos(aim * dt)
    lbi = er * jnp.sin(aim * dt)
    inv = 1.0 / (are * are + aim * aim)
    fr = ((lbr - 1.0) * are + lbi * aim) * inv
    fi = (lbi * are - (lbr - 1.0) * aim) * inv
    return dt, lbr, lbi, inv, fr, fi


def _s5_params(are, aim, ldt, bre, bim, name, deps=()):
    shape = are.shape

    def body(are_ref, aim_ref, ldt_ref, bre_ref, bim_ref, lbr_ref, lbi_ref, bbr_ref, bbi_ref):
        _, lbr, lbi, _, fr, fi = _s5_disc(are_ref[...], aim_ref[...], ldt_ref[...])
        lbr_ref[...] = lbr
        lbi_ref[...] = lbi
        bbr_ref[...] = fr * bre_ref[...] - fi * bim_ref[...]
        bbi_ref[...] = fr * bim_ref[...] + fi * bre_ref[...]

    out = jax.ShapeDtypeStruct(shape, F32)
    return _pallas(body, [are, aim, ldt, bre, bim], deps, name=name,
                   in_specs=[pl.BlockSpec(memory_space=pltpu.VMEM)] * 5, out_shape=[out] * 4,
                   compiler_params=pltpu.CompilerParams(vmem_limit_bytes=VMEM_LIMIT_BYTES))


def _s5_params_bwd(are, aim, ldt, bre, bim, glbr, glbi, gbbr, gbbi, n_groups, name, deps=()):
    shape = are.shape
    rows_per_group = shape[0] // n_groups

    def body(are_ref, aim_ref, ldt_ref, bre_ref, bim_ref, glbr_ref, glbi_ref, gbbr_ref, gbbi_ref,
             gar_ref, gai_ref, gdt_ref, gbr_ref, gbi_ref):
        are_v, aim_v = are_ref[...], aim_ref[...]
        bre_v, bim_v = bre_ref[...], bim_ref[...]
        gbbr_v, gbbi_v = gbbr_ref[...], gbbi_ref[...]
        dt, lbr, lbi, inv, fr, fi = _s5_disc(are_v, aim_v, ldt_ref[...])
        gbr_ref[...] = fr * gbbr_v + fi * gbbi_v
        gbi_ref[...] = fr * gbbi_v - fi * gbbr_v
        lane_group = lax.broadcasted_iota(jnp.int32, (LANES, LANES), 0) // S5_GROUP
        same_group = (lane_group == lax.broadcasted_iota(jnp.int32, (LANES, LANES), 1) // S5_GROUP)
        ones = same_group.astype(F32)
        gfr = jnp.dot(bre_v * gbbr_v + bim_v * gbbi_v, ones, precision=HIGHEST, preferred_element_type=F32)
        gfi = jnp.dot(bre_v * gbbi_v - bim_v * gbbr_v, ones, precision=HIGHEST, preferred_element_type=F32)
        glr = glbr_ref[...] + (are_v * gfr - aim_v * gfi) * inv
        gli = glbi_ref[...] + (are_v * gfi + aim_v * gfr) * inv
        qr = (fr * are_v + fi * aim_v) * inv
        qi = (fi * are_v - fr * aim_v) * inv
        gzr = lbr * glr + lbi * gli
        gzi = lbr * gli - lbi * glr
        gar_ref[...] = dt * gzr - (qr * gfr + qi * gfi)
        gai_ref[...] = dt * gzi - (qr * gfi - qi * gfr)
        e = dt * (are_v * gzr + aim_v * gzi)
        per_group = jnp.sum(e.reshape(n_groups, rows_per_group, LANES), axis=1)
        total = jnp.sum(per_group, axis=1, keepdims=True) * (1.0 / S5_GROUP)
        gdt_ref[...] = jnp.broadcast_to(total, gdt_ref.shape)

    out = jax.ShapeDtypeStruct(shape, F32)
    return _pallas(
        body, [are, aim, ldt, bre, bim, glbr, glbi, gbbr, gbbi], deps, name=name,
        in_specs=[pl.BlockSpec(memory_space=pltpu.VMEM)] * 9,
        out_shape=[out, out, jax.ShapeDtypeStruct((n_groups, LANES), F32), out, out],
        compiler_params=pltpu.CompilerParams(vmem_limit_bytes=VMEM_LIMIT_BYTES))


def _cmul(ar, ai, br, bi):
    return ar * br - ai * bi, ar * bi + ai * br


def _scan_in_place(hr_ref, hi_ref, lr, li, reverse):
    l, wb = hr_ref.shape
    nt = l // SUBLANES
    shift = SUBLANES - 1 if reverse else 1
    unroll = 8 if nt % 8 == 0 else 1

    def rows(k):
        t = (nt - 1 - k) if reverse else k
        return pl.ds(pl.multiple_of(t * SUBLANES, SUBLANES), SUBLANES)

    zero = jnp.zeros((SUBLANES, wb), F32)
    one = jnp.ones((SUBLANES, wb), F32)

    def local_step(k, carry):
        hr, hi, pr, pi = carry
        r = rows(k)
        tr_, ti_ = _cmul(lr, li, hr, hi)
        hr, hi = tr_ + hr_ref[r, :], ti_ + hi_ref[r, :]
        hr_ref[r, :] = hr
        hi_ref[r, :] = hi
        pr, pi = _cmul(lr, li, pr, pi)
        return hr, hi, pr, pi

    er, ei, lnr, lni = lax.fori_loop(0, nt, local_step, (zero, zero, one, zero), unroll=unroll)

    row = lax.broadcasted_iota(jnp.int32, (SUBLANES, wb), 0)
    tr_, ti_ = er, ei
    for j in range(1, SUBLANES):
        pr_, pi_ = _cmul(lnr, lni, pltpu.roll(tr_, shift, axis=0), pltpu.roll(ti_, shift, axis=0))
        at = row == ((SUBLANES - 1 - j) if reverse else j)
        tr_ = jnp.where(at, er + pr_, tr_)
        ti_ = jnp.where(at, ei + pi_, ti_)
    edge = row == ((SUBLANES - 1) if reverse else 0)
    cr = jnp.where(edge, 0.0, pltpu.roll(tr_, shift, axis=0))
    ci = jnp.where(edge, 0.0, pltpu.roll(ti_, shift, axis=0))

    def fix_step(k, carry):
        pr, pi = carry
        pr, pi = _cmul(lr, li, pr, pi)
        r = rows(k)
        ar_, ai_ = _cmul(pr, pi, cr, ci)
        hr_ref[r, :] = hr_ref[r, :] + ar_
        hi_ref[r, :] = hi_ref[r, :] + ai_
        return pr, pi

    lax.fori_loop(0, nt, fix_step, (one, zero), unroll=unroll)


def _dot(a, b):
    return jnp.dot(a.astype(BF16), b.astype(BF16), preferred_element_type=F32)


def _s5_forward(u_seg, mb_re, mb_im, mc_re, mc_im, lam_re, lam_im, dvec, name):
    l = u_seg.shape[0]
    nb, kin, kst = mb_re.shape

    def body(u_ref, mbr_ref, mbi_ref, mcr_ref, mci_ref, lr_ref, li_ref, d_ref, hr_ref, hi_ref, y_ref):
        u = u_ref[...]
        hr_ref[...] = _dot(u, mbr_ref[...])
        hi_ref[...] = _dot(u, mbi_ref[...])
        _scan_in_place(hr_ref, hi_ref, jnp.broadcast_to(lr_ref[...], (SUBLANES, kst)),
                       jnp.broadcast_to(li_ref[...], (SUBLANES, kst)), False)
        y_ref[...] = _dot(hr_ref[...], mcr_ref[...]) - _dot(hi_ref[...], mci_ref[...]) + d_ref[...] * u

    act = pl.BlockSpec((l, kin), lambda b: (0, b))
    state = pl.BlockSpec((l, kst), lambda b: (0, b))
    up = pl.BlockSpec((None, kin, kst), lambda b: (b, 0, 0))
    down = pl.BlockSpec((None, kst, kin), lambda b: (b, 0, 0))
    hshape = jax.ShapeDtypeStruct((l, nb * kst), F32)
    return pl.pallas_call(
        body, name=name, grid=(nb,),
        in_specs=[act, up, up, down, down, pl.BlockSpec((1, kst), lambda b: (0, b)),
                  pl.BlockSpec((1, kst), lambda b: (0, b)), pl.BlockSpec((1, kin), lambda b: (0, b))],
        out_specs=[state, state, act],
        out_shape=[hshape, hshape, jax.ShapeDtypeStruct((l, nb * kin), F32)],
        compiler_params=_params("parallel"),
    )(u_seg, mb_re, mb_im, mc_re, mc_im, lam_re, lam_im, dvec)


def _s5_backward(dy_seg, u_seg, dy_t, u_t, h_re, h_im, mg_re, mg_im, md_re, md_im, lam_re, lam_im_neg, dvec, name):
    l = dy_seg.shape[0]
    nb, kin, kst = mg_re.shape
    nt = l // SUBLANES

    def body(dy_ref, u_ref, dyt_ref, ut_ref, hr_ref, hi_ref, mgr_ref, mgi_ref, mdr_ref, mdi_ref, lr_ref, li_ref,
             d_ref, du_ref, gcr_ref, gci_ref, gbr_ref, gbi_ref, glr_ref, gli_ref, dsk_ref, qr_ref, qi_ref):
        dy = dy_ref[...]
        qr_ref[...] = _dot(dy, mgr_ref[...])
        qi_ref[...] = _dot(dy, mgi_ref[...])
        _scan_in_place(qr_ref, qi_ref, jnp.broadcast_to(lr_ref[...], (SUBLANES, kst)),
                       jnp.broadcast_to(li_ref[...], (SUBLANES, kst)), True)
        du_ref[...] = _dot(qr_ref[...], mdr_ref[...]) - _dot(qi_ref[...], mdi_ref[...]) + d_ref[...] * dy
        dsk_ref[...] = jnp.sum(dy * u_ref[...], axis=0, keepdims=True)
        gcr_ref[...] = _dot(dyt_ref[...], hr_ref[...])
        gci_ref[...] = _dot(dyt_ref[...], hi_ref[...])
        gbr_ref[...] = _dot(ut_ref[...], qr_ref[...])
        gbi_ref[...] = _dot(ut_ref[...], qi_ref[...])

        row = lax.broadcasted_iota(jnp.int32, (SUBLANES, kst), 0)
        last = pl.ds((nt - 1) * SUBLANES, SUBLANES)
        first = pl.ds(0, SUBLANES)
        pr = jnp.where(row == 0, 0.0, pltpu.roll(hr_ref[last, :], 1, axis=0))
        pi = jnp.where(row == 0, 0.0, pltpu.roll(hi_ref[last, :], 1, axis=0))
        gr, gi = qr_ref[first, :], qi_ref[first, :]

        def step(t, carry):
            acc_r, acc_i = carry
            cur = pl.ds(pl.multiple_of(t * SUBLANES, SUBLANES), SUBLANES)
            prev = pl.ds(pl.multiple_of((t - 1) * SUBLANES, SUBLANES), SUBLANES)
            gr, gi = qr_ref[cur, :], qi_ref[cur, :]
            pr, pi = hr_ref[prev, :], hi_ref[prev, :]
            return acc_r + gr * pr + gi * pi, acc_i + gi * pr - gr * pi

        acc_r, acc_i = lax.fori_loop(1, nt, step, (gr * pr + gi * pi, gi * pr - gr * pi))
        glr_ref[...] = jnp.sum(acc_r, axis=0, keepdims=True)
        gli_ref[...] = jnp.sum(acc_i, axis=0, keepdims=True)

    act = pl.BlockSpec((l, kin), lambda b: (0, b))
    act_t = pl.BlockSpec((kin, l), lambda b: (b, 0))
    state = pl.BlockSpec((l, kst), lambda b: (0, b))
    up = pl.BlockSpec((None, kin, kst), lambda b: (b, 0, 0))
    down = pl.BlockSpec((None, kst, kin), lambda b: (b, 0, 0))
    vec_st = pl.BlockSpec((1, kst), lambda b: (0, b))
    vec_in = pl.BlockSpec((1, kin), lambda b: (0, b))
    outer = jax.ShapeDtypeStruct((nb, kin, kst), F32)
    lam_shape = jax.ShapeDtypeStruct((1, nb * kst), F32)
    return pl.pallas_call(
        body, name=name, grid=(nb,),
        in_specs=[act, act, act_t, act_t, state, state, up, up, down, down, vec_st, vec_st, vec_in],
        out_specs=[act, up, up, up, up, vec_st, vec_st, vec_in],
        out_shape=[jax.ShapeDtypeStruct((l, nb * kin), F32), outer, outer, outer, outer, lam_shape, lam_shape,
                   jax.ShapeDtypeStruct((1, nb * kin), F32)],
        scratch_shapes=[pltpu.VMEM((l, kst), F32), pltpu.VMEM((l, kst), F32)],
        compiler_params=_params("parallel"),
    )(dy_seg, u_seg, dy_t, u_t, h_re, h_im, mg_re, mg_im, md_re, md_im, lam_re, lam_im_neg, dvec)


def _block_diag(m, nb):
    g, r, s = m.shape
    gb = g // nb
    eye = jnp.eye(gb, dtype=m.dtype)
    out = m.reshape(nb, gb, r, 1, s) * eye[None, :, None, :, None]
    return out.reshape(nb, gb * r, gb * s)


def _block_diag_extract(mat, g, r, s):
    nb = mat.shape[0]
    gb = g // nb
    eye = jnp.eye(gb, dtype=mat.dtype)
    m5 = mat.reshape(nb, gb, r, gb, s) * eye[None, :, None, :, None]
    return jnp.sum(m5, axis=3).reshape(g, r, s)


def _adamw(w, m, v, gslots, name, layer=0, prev=None):
    layers, r, c = w.shape
    s = gslots.shape[0]
    tr = _tile(r, max(SUBLANES, 1 << int(math.log2(ADAMW_BLOCK_ELEMS // c))))
    bc1 = 1.0 / (1.0 - ADAM_B1 ** ADAM_STEP)
    bc2 = 1.0 / (1.0 - ADAM_B2 ** ADAM_STEP)

    def body(w_ref, m_ref, v_ref, g_ref, *rest):
        go_ref, d_ref, mo_ref, vo_ref = rest[-4:]
        g = g_ref[0].astype(F32)
        for k in range(1, s):
            g = g + g_ref[k].astype(F32)
        mn = ADAM_B1 * m_ref[...] + (1.0 - ADAM_B1) * g
        vn = ADAM_B2 * v_ref[...] + (1.0 - ADAM_B2) * (g * g)
        go_ref[...] = g
        mo_ref[...] = mn
        vo_ref[...] = vn
        d_ref[...] = -ADAM_LR * ((mn * bc1) / (jnp.sqrt(vn * bc2) + ADAM_EPS) + ADAM_WD * w_ref[...])

    spec = pl.BlockSpec((None, tr, c), lambda i: (layer, i, 0))
    out = jax.ShapeDtypeStruct((layers, r, c), F32)
    in_specs = [spec, spec, spec, pl.BlockSpec((s, tr, c), lambda i: (0, i, 0))]
    args = [w, m, v, gslots]
    aliases = {}
    if prev is not None:
        in_specs += [pl.BlockSpec(memory_space=pl.ANY)] * 4
        args += list(prev)
        aliases = {4 + q: q for q in range(4)}
    return pl.pallas_call(
        body, name=name, grid=(r // tr,), in_specs=in_specs,
        out_specs=[spec] * 4, out_shape=[out] * 4, input_output_aliases=aliases,
        compiler_params=_params("parallel"),
    )(*args)


def _pack(parts):
    flat = jnp.concatenate([p.reshape(-1) for p in parts])
    pad = (-flat.shape[0]) % (PACK_ROWS * LANES)
    return jnp.pad(flat, (0, pad)).reshape(-1, LANES)


def _unpack(packed, shapes):
    flat = packed.reshape(-1)
    out, off = [], 0
    for shp in shapes:
        size = math.prod(shp)
        out.append(flat[off:off + size].reshape(shp))
        off += size
    return out


def kernel(x, norm_g, w_in, conv_w, w_out_a, a_re, a_im, log_dt, b_re, b_im, c_re, c_im, d_skip, w_glu, b_glu, w_out_b, w_o, final_g, loss_target, m_norm_g, m_w_in, m_conv_w, m_w_out_a, m_a_re, m_a_im, m_log_dt, m_b_re, m_b_im, m_c_re, m_c_im, m_d_skip, m_w_glu, m_b_glu, m_w_out_b, m_w_o, m_final_g, v_norm_g, v_w_in, v_conv_w, v_w_out_a, v_a_re, v_a_im, v_log_dt, v_b_re, v_b_im, v_c_re, v_c_im, v_d_skip, v_w_glu, v_b_glu, v_w_out_b, v_w_o, v_final_g):
    depth = norm_g.shape[0]
    l, d = x.shape[1], x.shape[2]
    ws = w_glu.shape[2]
    n_groups, n_state = a_re.shape[1], a_re.shape[2]
    nb = ws // LANES
    assert S5_GROUP == b_re.shape[3] and n_state * S5_GB == 4 * LANES
    u_col, zb_col = 4 * d // ws, 4 * d // ws + 1
    ga_col, gb_col = (4 * d + 2 * ws) // d, (4 * d + 2 * ws) // d + 1
    me = 4 * lax.axis_index("x") + 2 * lax.axis_index("y") + lax.axis_index("c")

    xs = [x[0]]
    tgt = loss_target[0]

    big_names = ("w_in", "w_out_a", "w_glu", "w_out_b", "w_o")
    big = dict(w_in=(w_in, m_w_in, v_w_in), w_out_a=(w_out_a, m_w_out_a, v_w_out_a),
               w_glu=(w_glu, m_w_glu, v_w_glu), w_out_b=(w_out_b, m_w_out_b, v_w_out_b),
               w_o=(w_o, m_w_o, v_w_o))

    def shards_bf16(i):
        return [big[k][0][i].astype(BF16) for k in big_names]

    def gather_start(shards, name, deps=()):
        sems, srcs, lands, token = _exchange_start(
            _plan_gather_chips, shards, [_landing(s_, N_DEV, me) for s_ in shards], f"{name}_start", deps)
        return (name, sems, srcs, lands), token

    def gather_forward(state, after, deps=()):
        name, sems, srcs, lands = state
        _, lands = _exchange_wait(_plan_gather_chips, sems, srcs, lands, after, f"{name}_wait")
        sems, _, lands, token = _exchange_start(_plan_gather_forward, None, lands, f"{name}_forward_start", deps)
        return (name, sems, lands), token

    def gather_finish(state, after):
        name, sems, lands = state
        return _exchange_wait(_plan_gather_forward, sems, None, lands, after, f"{name}_forward_wait")[1]

    conv_shard = jnp.pad(conv_w.reshape(depth * 3, -1), ((0, SUBLANES - depth * 3), (0, 0)))
    w_in_state, token = gather_start([shards_bf16(0)[0], conv_shard], "ag_w_in_0")
    s5 = []
    shape3 = (n_groups, n_state, S5_GROUP)
    for i in range(depth):
        dense_in = (_dense(a_re[i][:, :, None], shape3), _dense(a_im[i][:, :, None], shape3),
                    _dense(log_dt[i][:, None, None], shape3), b_re[i].reshape(-1, LANES), b_im[i].reshape(-1, LANES))
        lbr, lbi, bbr, bbi = _s5_params(*dense_in, f"s5_params_{i}", deps=(token,))
        bbr3, bbi3 = bbr.reshape(shape3), bbi.reshape(shape3)
        diag = lambda m: _block_diag(m, nb).astype(BF16)
        s5.append(dict(
            dense_in=dense_in,
            lam_re=lbr.reshape(shape3)[:, :, 0].reshape(1, -1), lam_im=lbi.reshape(shape3)[:, :, 0].reshape(1, -1),
            up=(diag(bbr3.transpose(0, 2, 1)), diag(bbi3.transpose(0, 2, 1))),
            down=(diag(c_re[i].transpose(0, 2, 1)), diag(c_im[i].transpose(0, 2, 1))),
            up_bwd=(diag(c_re[i]), diag(-c_im[i])), down_bwd=(diag(bbr3), diag(-bbi3))))
    prologue = [m for p in s5 for k in ("up", "down", "up_bwd", "down_bwd") for m in p[k]]
    prologue += [p[k] for p in s5 for k in ("lam_re", "lam_im")]
    w_in_state, token = gather_forward(w_in_state, prologue)
    rest_state, token = gather_start(shards_bf16(0)[1:], "ag_rest_0", deps=(token,))

    saved = []
    wg = [None] * depth
    conv_full = None
    for i in range(depth):
        xi = xs[-1]
        h = _rmsnorm_fwd(xi, norm_g[i], f"rmsnorm_fwd_{i}", deps=(token,))
        arrived = gather_finish(w_in_state, [h])
        if i == 0:
            conv_full = arrived[1].transpose(1, 0, 2).reshape(SUBLANES, d)[:depth * 3].reshape(depth, 3, d)
        conv8 = jnp.pad(conv_full[i], ((0, SUBLANES - 3), (0, 0)))
        proj = _mm_win_fwd(h, arrived[0], f"mm_proj_{i}")
        rest_state, token = gather_forward(rest_state, [proj])
        if i + 1 < depth:
            w_in_state, token = gather_start([shards_bf16(i + 1)[0]], f"ag_w_in_{i + 1}", deps=(token,))
        pa = _branch_a_fwd(proj, conv8, d, f"branch_a_fwd_{i}", deps=(token,))
        rest = gather_finish(rest_state, [pa])
        wg[i] = g = dict(w_in=arrived[0], w_a=rest[0].reshape(d, d), w_glu=rest[1].reshape(ws, ws),
                         w_b=rest[2], w_o=rest[3].reshape(d, d))
        ya = _mm(pa, g["w_a"], name=f"mm_ya_{i}")
        u_seg = _to_segments(proj[:, 4 * d:4 * d + ws])
        h_re, h_im, y_seg = _s5_forward(u_seg, *s5[i]["up"], *s5[i]["down"], s5[i]["lam_re"], s5[i]["lam_im"],
                                        d_skip[i].reshape(1, ws), f"s5_forward_{i}")
        y = _from_segments(y_seg)
        yg = _gelu_cast(y, f"gelu_{i}")
        gl = _mm(yg, g["w_glu"], name=f"mm_glu_{i}")
        pb = _glu_post(y, gl, proj, b_glu[i], zb_col, f"glu_post_{i}")
        w_b2d = g["w_b"].transpose(1, 0, 2).reshape(ws, d)
        yb = _mm(pb, w_b2d, name=f"mm_yb_{i}")
        mrg = _merge_fwd(proj, ya, yb, d, ga_col, gb_col, f"merge_fwd_{i}")
        deps = ()
        if i + 1 < depth:
            w_in_state, token = gather_forward(w_in_state, [mrg])
            rest_state, token = gather_start(shards_bf16(i + 1)[1:], f"ag_rest_{i + 1}", deps=(token,))
            deps = (token,)
        xs.append(_mm(mrg, g["w_o"], name=f"mm_out_{i}", add=xi, deps=deps))
        saved.append(dict(h=h, proj=proj, pa=pa, ya=ya, yb=yb, y=y, yg=yg, gl=gl, pb=pb, mrg=mrg,
                          u_seg=u_seg, h_re=h_re, h_im=h_im, conv8=conv8, w_b2d=w_b2d))

    dx, g_final, loss_part = _final_loss(xs[-1], final_g, tgt, "final_loss")
    loss = lax.psum(loss_part[0, 0], MESH_AXES)

    rs_pending = []
    small = {k: [None] * depth for k in ("norm_g", "a_re", "a_im", "log_dt", "b_re", "b_im", "c_re", "c_im",
                                         "d_skip", "b_glu", "conv_w")}

    my_chip = 2 * lax.axis_index("x") + lax.axis_index("y")

    def reduce_on_chip(pieces, tag):
        lands = [lax.empty((4,) + p.shape[1:], p.dtype) for p in pieces]
        sems, srcs, lands, token = _exchange_start(_plan_reduce_sibling, pieces, lands, f"rs_sibling_start_{tag}")
        return (sems, srcs, lands), token

    def reduce_across_chips(names_, state, layer, tag, after):
        sems, srcs, lands = state
        srcs, lands = _exchange_wait(_plan_reduce_sibling, sems, srcs, lands, after, f"rs_sibling_wait_{tag}")
        sums = [_chip_sums(p, l_, f"chip_sum_{k}_{layer}") for k, p, l_ in zip(names_, srcs, lands)]
        lands = [_landing(lax.dynamic_index_in_dim(s_, my_chip, 0, keepdims=False), 4, my_chip) for s_ in sums]
        sems, srcs, lands, token = _exchange_start(_plan_reduce_chips, sums, lands, f"rs_chips_start_{tag}")
        rs_pending.append((names_, layer, sems, srcs, lands, f"rs_chips_wait_{tag}"))
        return token

    for i in reversed(range(depth)):
        s, g = saved[i], wg[i]
        proj = s["proj"]
        dxo_b = dx.astype(BF16)
        dm = _mm(dxo_b, g["w_o"], name=f"mm_dm_{i}", nt=True)
        gw_o = _mm(s["mrg"].T, dxo_b, name=f"mm_gw_o_{i}", out_dtype=BF16)
        dya, dyb, dga, dgb = _merge_bwd(proj, s["ya"], s["yb"], dm, d, ga_col, gb_col, f"merge_bwd_{i}")
        dpa = _mm(dya, g["w_a"], name=f"mm_dpa_{i}", nt=True)
        gw_a = _mm(s["pa"].T, dya, name=f"mm_gw_a_{i}", out_dtype=BF16)
        dpb = _mm(dyb, s["w_b2d"], name=f"mm_dpb_{i}", nt=True)
        gw_b = _mm(s["pb"].T, dyb, name=f"mm_gw_b_{i}", split_n=N_DEV, out_dtype=BF16)
        dv, dbg, dcg, dza, dw0, dw1, dw2 = _branch_a_bwd(proj, dpa, s["conv8"], d, f"branch_a_bwd_{i}")
        small["conv_w"][i] = jnp.concatenate([dw0, dw1, dw2], axis=0)
        dzb, dgl, t1, db_glu = _glu_bwd1(s["y"], s["gl"], proj, b_glu[i], dpb, zb_col, f"glu_bwd1_{i}")
        small["b_glu"][i] = db_glu.reshape(ws)
        dyg2 = _mm(dgl, g["w_glu"], name=f"mm_dyg_{i}", nt=True)
        gw_glu = _mm(s["yg"].T, dgl, name=f"mm_gw_glu_{i}", out_dtype=BF16)
        small_names_ = ("w_out_a", "w_glu", "w_out_b", "w_o")
        state, token = reduce_on_chip(
            [gw_a.reshape(N_DEV, d // N_DEV, d), gw_glu.reshape(N_DEV, ws // N_DEV, ws), gw_b,
             gw_o.reshape(N_DEV, d // N_DEV, d)], f"small_{i}")
        dy = _glu_bwd2(s["y"], t1, dyg2, f"glu_bwd2_{i}", deps=(token,))
        dy_seg = _to_segments(dy)
        u_seg = s["u_seg"]
        du_seg, gc_re, gc_im, gbb_re, gbb_im, glam_re, glam_im, dskip = _s5_backward(
            dy_seg, u_seg, dy_seg.T, u_seg.T, s["h_re"], s["h_im"], *s5[i]["up_bwd"], *s5[i]["down_bwd"],
            s5[i]["lam_re"], -s5[i]["lam_im"], d_skip[i].reshape(1, ws), f"s5_backward_{i}")
        token = reduce_across_chips(small_names_, state, i, f"small_{i}", [du_seg])
        small["d_skip"][i] = dskip.reshape(n_groups, S5_GROUP)
        small["c_re"][i] = _block_diag_extract(gc_re, n_groups, S5_GROUP, n_state)
        small["c_im"][i] = -_block_diag_extract(gc_im, n_groups, S5_GROUP, n_state)
        gbb_re = _block_diag_extract(gbb_re, n_groups, S5_GROUP, n_state).transpose(0, 2, 1)
        gbb_im = _block_diag_extract(gbb_im, n_groups, S5_GROUP, n_state).transpose(0, 2, 1)
        gar, gai, gdt, gbr, gbi = _s5_params_bwd(
            *s5[i]["dense_in"], _dense(glam_re.reshape(n_groups, n_state, 1), shape3),
            _dense(glam_im.reshape(n_groups, n_state, 1), shape3),
            gbb_re.reshape(-1, LANES), gbb_im.reshape(-1, LANES), n_groups, f"s5_params_bwd_{i}", deps=(token,))
        small["a_re"][i] = gar.reshape(shape3)[:, :, 0]
        small["a_im"][i] = gai.reshape(shape3)[:, :, 0]
        small["log_dt"][i] = gdt[:, 0]
        small["b_re"][i] = gbr.reshape(shape3)
        small["b_im"][i] = gbi.reshape(shape3)
        du = _from_segments(du_seg).astype(BF16)
        dproj = jnp.concatenate([dv, dbg, dcg, dza, du, dzb, dga, dgb], axis=1)
        gw_in = _mm(s["h"].T, dproj, name=f"mm_gw_in_{i}", split_n=N_DEV, tm=1024,
                    out_dtype=BF16)
        state, token = reduce_on_chip([gw_in], f"w_in_{i}")
        dh = _mm_win_bwd(dproj, g["w_in"], f"mm_dh_{i}", deps=(token,))
        token = reduce_across_chips(("w_in",), state, i, f"w_in_{i}", [dh])
        dx, dng = _rmsnorm_bwd(xs[i], norm_g[i], dh, dx, f"rmsnorm_bwd_{i}", deps=(token,))
        small["norm_g"][i] = dng.reshape(d)

    results = {}

    small_names = ("norm_g", "a_re", "a_im", "log_dt", "b_re", "b_im", "c_re", "c_im", "d_skip", "b_glu")
    small_w = dict(norm_g=(norm_g, m_norm_g, v_norm_g), a_re=(a_re, m_a_re, v_a_re), a_im=(a_im, m_a_im, v_a_im),
                   log_dt=(log_dt, m_log_dt, v_log_dt), b_re=(b_re, m_b_re, v_b_re), b_im=(b_im, m_b_im, v_b_im),
                   c_re=(c_re, m_c_re, v_c_re), c_im=(c_im, m_c_im, v_c_im), d_skip=(d_skip, m_d_skip, v_d_skip),
                   b_glu=(b_glu, m_b_glu, v_b_glu), final_g=(final_g, m_final_g, v_final_g))
    order = small_names + ("final_g", "conv_w")
    part = {k: jnp.stack(small[k]) for k in small_names + ("conv_w",)}
    part["final_g"] = g_final.reshape(d)
    shapes = [part[k].shape for k in order]
    gpack = _all_gather(_pack([part[k] for k in order]), "ag_small_grads")
    zeros_conv = jnp.zeros(part["conv_w"].shape, F32)
    wpack = _pack([small_w[k][0] for k in order[:-1]] + [zeros_conv])
    mpack = _pack([small_w[k][1] for k in order[:-1]] + [zeros_conv])
    vpack = _pack([small_w[k][2] for k in order[:-1]] + [zeros_conv])
    sres = [_unpack(p[0], shapes)
            for p in _adamw(wpack[None], mpack[None], vpack[None], gpack, "adamw_small")]
    for j, k in enumerate(order[:-1]):
        results[k] = [sres[q][j] for q in range(4)]
    dc = d // N_DEV
    gconv = lax.dynamic_slice_in_dim(sres[0][-1], me * dc, dc, axis=2)
    pad8 = lambda a: jnp.pad(a.reshape(depth * 3, dc), ((0, SUBLANES - depth * 3), (0, 0)))[None]
    cres = _adamw(pad8(conv_w), pad8(m_conv_w), pad8(v_conv_w), pad8(gconv), "adamw_conv_w")
    results["conv_w"] = [r_[0, :depth * 3].reshape(depth, 3, dc) for r_ in cres]

    after = [cres[0]]
    for names_, layer, sems, srcs, lands, wait_name in rs_pending:
        _, slots = _exchange_wait(_plan_reduce_chips, sems, srcs, lands, after, wait_name)
        for k, land in zip(names_, slots):
            w_, m_, v_ = big[k]
            results[k] = _adamw(w_, m_, v_, land, f"adamw_{k}_{layer}", layer=layer, prev=results.get(k))
            after = [results[k][0]]

    names = ("norm_g", "w_in", "conv_w", "w_out_a", "a_re", "a_im", "log_dt", "b_re", "b_im", "c_re", "c_im",
             "d_skip", "w_glu", "b_glu", "w_out_b", "w_o", "final_g")
    outs = [loss, dx[None]]
    for q in range(4):
        outs += [results[k][q] for k in names]
    return tuple(outs)
```

```python
import functools
import math

import jax
import jax.numpy as jnp
from jax import lax
from jax.experimental import pallas as pl
from jax.experimental.pallas import tpu as pltpu

F32 = jnp.float32
BF16 = jnp.bfloat16
HIGHEST = lax.Precision.HIGHEST

N_DEV = 8
LANES = 128
SUBLANES = 8
VMEM_LIMIT_BYTES = 56 * 1024 * 1024

RMS_EPS = 1e-6
ADAM_LR = 0.001
ADAM_B1 = 0.9
ADAM_B2 = 0.999
ADAM_EPS = 1e-08
ADAM_WD = 0.01
ADAM_STEP = 10
GELU_C0 = math.sqrt(2.0 / math.pi)
GELU_C1 = 0.044715

ADAMW_BLOCK_ELEMS = 1 << 17
PACK_ROWS = 512

S5_GROUP = 16
S5_GB = LANES // S5_GROUP


def _params(*semantics):
    return pltpu.CompilerParams(dimension_semantics=semantics, vmem_limit_bytes=VMEM_LIMIT_BYTES)


ANY_SPEC = pl.BlockSpec(memory_space=pl.ANY)


def _pallas(body, args, deps=(), *, in_specs, **kwargs):
    deps = tuple(deps)
    if not deps:
        return pl.pallas_call(body, in_specs=in_specs, **kwargs)(*args)

    def body_after(*refs):
        body(*refs[len(deps):])

    return pl.pallas_call(body_after, in_specs=[ANY_SPEC] * len(deps) + list(in_specs), **kwargs)(*deps, *args)


def _tile(n, pref):
    t = min(n, pref)
    while n % t:
        assert t % 2 == 0, (n, pref)
        t //= 2
    return t


def _sigmoid(z):
    return 1.0 / (1.0 + jnp.exp(-z))


def _gelu(y):
    return 0.5 * y * (1.0 + jnp.tanh(GELU_C0 * (y + GELU_C1 * y * y * y)))


def _gelu_grad(y):
    t = jnp.tanh(GELU_C0 * (y + GELU_C1 * y * y * y))
    return 0.5 * (1.0 + t) + 0.5 * y * (1.0 - t * t) * GELU_C0 * (1.0 + 3.0 * GELU_C1 * y * y)


def _all_gather(shard, name):
    r, c_ = shard.shape

    def body(x_ref, out_ref, send_sems, recv_sems, local_sem):
        x, y, c = lax.axis_index("x"), lax.axis_index("y"), lax.axis_index("c")
        me, sibling = (x, y, c), (x, y, 1 - c)
        chips = [(1 - x, y), (x, 1 - y), (1 - x, 1 - y)]

        def slot(px, py, pc):
            return out_ref.at[4 * px + 2 * py + pc]

        def copy(k, block, to, src=None):
            return pltpu.make_async_remote_copy(
                src_ref=slot(*block) if src is None else src, dst_ref=slot(*block),
                send_sem=send_sems.at[k], recv_sem=recv_sems.at[k],
                device_id=to, device_id_type=pl.DeviceIdType.MESH)

        mine = pltpu.make_async_copy(x_ref, slot(*me), local_sem)
        mine.start()
        first = [copy(0, me, sibling, src=x_ref)]
        first += [copy(1 + j, me, (*chip, c), src=x_ref) for j, chip in enumerate(chips)]
        for cp in first:
            cp.start()
        passed = [copy(4 + j, (*chip, c), sibling) for j, chip in enumerate(chips)]
        for j, chip in enumerate(chips):
            copy(1 + j, (*chip, c), me).wait_recv()
            passed[j].start()
        copy(0, sibling, me).wait_recv()
        for j, chip in enumerate(chips):
            copy(4 + j, (*chip, 1 - c), me).wait_recv()
        for cp in first + passed:
            cp.wait_send()
        mine.wait()

    return pl.pallas_call(
        body, name=name,
        out_shape=jax.ShapeDtypeStruct((N_DEV, r, c_), shard.dtype),
        in_specs=[pl.BlockSpec(memory_space=pl.ANY)],
        out_specs=pl.BlockSpec(memory_space=pl.ANY),
        scratch_shapes=[pltpu.SemaphoreType.DMA((7,)), pltpu.SemaphoreType.DMA((7,)),
                        pltpu.SemaphoreType.DMA],
    )(shard)


HBM_SPEC = pl.BlockSpec(memory_space=pltpu.HBM)
SEM_SPEC = pl.BlockSpec(memory_space=pltpu.SEMAPHORE)
DATAFLOW_EFFECT = pltpu.SideEffectType.DATAFLOW_SIDE_EFFECTING
OTHER_CHIPS = (2, 4, 6)


def _flip(pos, mask):
    x, y, c = pos
    return x ^ ((mask >> 2) & 1), y ^ ((mask >> 1) & 1), c ^ (mask & 1)


def _dev(pos):
    return 4 * pos[0] + 2 * pos[1] + pos[2]


def _chip(pos):
    return 2 * pos[0] + pos[1]


def _plan_gather_chips(me):
    return [(_flip(me, k), None, _dev(me), _dev(_flip(me, k))) for k in (1,) + OTHER_CHIPS]


def _plan_gather_forward(me):
    sib = _flip(me, 1)
    return [(sib, _dev(_flip(me, k)), _dev(_flip(me, k)), _dev(_flip(sib, k))) for k in OTHER_CHIPS]


def _plan_reduce_sibling(me):
    sib = _flip(me, 1)
    return [(sib, 2 * q + sib[2], q, q) for q in range(4)]


def _plan_reduce_chips(me):
    return [(_flip(me, k), _chip(_flip(me, k)), _chip(me), _chip(_flip(me, k))) for k in OTHER_CHIPS]


PLAN_COPIES = {_plan_gather_chips: 4, _plan_gather_forward: 3, _plan_reduce_sibling: 4, _plan_reduce_chips: 3}


def _exchange_copies(plan, src_refs, land_refs, send_sems, recv_sems):
    me = (lax.axis_index("x"), lax.axis_index("y"), lax.axis_index("c"))
    pairs = []
    for b, (src_ref, land_ref) in enumerate(zip(src_refs, land_refs)):
        for j, (peer, src_slot, there, here) in enumerate(plan(me)):
            sem = b * PLAN_COPIES[plan] + j
            src = src_ref if src_slot is None else src_ref.at[src_slot]
            out = pltpu.make_async_remote_copy(
                src_ref=src, dst_ref=land_ref.at[there], send_sem=send_sems.at[sem], recv_sem=recv_sems.at[sem],
                device_id=peer, device_id_type=pl.DeviceIdType.MESH)
            inc = pltpu.make_async_remote_copy(
                src_ref=src, dst_ref=land_ref.at[here], send_sem=send_sems.at[sem], recv_sem=recv_sems.at[sem],
                device_id=peer, device_id_type=pl.DeviceIdType.MESH)
            pairs.append((out, inc))
    return pairs


def _exchange_start(plan, srcs, lands, name, deps=()):
    srcs = [] if srcs is None else list(srcs)
    ns, n, nd = len(srcs), len(lands), len(deps)

    def body(*refs):
        land_refs = refs[ns:ns + n]
        sems_at = ns + n + nd
        pairs = _exchange_copies(plan, refs[:ns] if ns else land_refs, land_refs, refs[sems_at], refs[sems_at + 1])
        for out, _ in pairs:
            out.start()
        token = refs[-1]
        token[...] = jnp.zeros_like(token)

    sems = pltpu.SemaphoreType.DMA((PLAN_COPIES[plan] * n,))
    bufs = srcs + list(lands)
    outs = pl.pallas_call(
        body, name=name,
        out_shape=(sems, sems, *[pltpu.HBM(a.shape, a.dtype) for a in bufs],
                   jax.ShapeDtypeStruct((SUBLANES, LANES), F32)),
        in_specs=[HBM_SPEC] * (ns + n) + [ANY_SPEC] * nd,
        out_specs=(SEM_SPEC, SEM_SPEC, *[HBM_SPEC] * (ns + n), pl.BlockSpec(memory_space=pltpu.VMEM)),
        input_output_aliases={i: 2 + i for i in range(ns + n)},
        compiler_params=pltpu.CompilerParams(has_side_effects=DATAFLOW_EFFECT),
    )(*[pltpu.with_memory_space_constraint(a, pltpu.HBM) for a in bufs], *deps)
    return (outs[0], outs[1]), (outs[2:2 + ns] if ns else None), outs[2 + ns:2 + ns + n], outs[-1]


def _exchange_wait(plan, sems, srcs, lands, after, name):
    srcs = [] if srcs is None else list(srcs)
    ns, n = len(srcs), len(lands)

    def body(*refs):
        land_refs = refs[ns:ns + n]
        pairs = _exchange_copies(plan, refs[:ns] if ns else land_refs, land_refs, refs[ns + n], refs[ns + n + 1])
        for out, inc in pairs:
            out.wait_send()
            inc.wait_recv()

    bufs = srcs + list(lands)
    outs = pl.pallas_call(
        body, name=name,
        out_shape=[pltpu.HBM(a.shape, a.dtype) for a in bufs],
        in_specs=[HBM_SPEC] * (ns + n) + [SEM_SPEC, SEM_SPEC] + [ANY_SPEC] * len(after),
        out_specs=[HBM_SPEC] * (ns + n),
        input_output_aliases={i: i for i in range(ns + n)},
        compiler_params=pltpu.CompilerParams(has_side_effects=DATAFLOW_EFFECT),
    )(*bufs, sems[0], sems[1], *after)
    return outs[:ns], outs[ns:]


def _landing(own, slots, slot):
    land = lax.empty((slots,) + own.shape, own.dtype)
    return lax.dynamic_update_slice(land, own[None], (slot,) + (0,) * own.ndim)


def _chip_sums(pieces, land, name):
    _, r, c_ = land.shape
    tr = _tile(r, max(2 * SUBLANES, 1 << int(math.log2(4 * ADAMW_BLOCK_ELEMS // c_))))

    def body(core_ref, p_ref, l_ref, o_ref):
        o_ref[...] = (p_ref[...].astype(F32) + l_ref[...].astype(F32)).astype(o_ref.dtype)

    spec = pl.BlockSpec((None, tr, c_), lambda q, i, core: (q, i, 0))
    return pl.pallas_call(
        body, name=name,
        grid_spec=pltpu.PrefetchScalarGridSpec(
            num_scalar_prefetch=1, grid=(4, r // tr),
            in_specs=[pl.BlockSpec((None, tr, c_), lambda q, i, core: (2 * q + core[0], i, 0)), spec],
            out_specs=spec),
        out_shape=jax.ShapeDtypeStruct(land.shape, land.dtype),
        compiler_params=_params("parallel", "parallel"),
    )(lax.axis_index("c").reshape(1), pieces, land)


def _mm(a, b, *, name, nt=False, out_dtype=F32, add=None, split_n=None, tm=512, tn=1024, deps=()):
    m, k = a.shape
    n = b.shape[0] if nt else b.shape[1]
    tm = _tile(m, tm)
    tn = n // split_n if split_n else _tile(n, tn)
    dims = (((1,), (1,)), ((), ())) if nt else (((1,), (0,)), ((), ()))

    def body(*refs):
        a_ref, b_ref = refs[0], refs[1]
        o_ref = refs[-1]
        acc = lax.dot_general(a_ref[...], b_ref[...], dims, preferred_element_type=F32)
        if add is not None:
            acc = acc + refs[2][...]
        o_ref[...] = acc.astype(o_ref.dtype)

    in_specs = [pl.BlockSpec((tm, k), lambda i, j: (i, 0)),
                pl.BlockSpec((tn, k), lambda i, j: (j, 0)) if nt
                else pl.BlockSpec((k, tn), lambda i, j: (0, j))]
    args = [a, b]
    if add is not None:
        in_specs.append(pl.BlockSpec((tm, tn), lambda i, j: (i, j)))
        args.append(add)
    if split_n:
        out_shape = jax.ShapeDtypeStruct((split_n, m, tn), out_dtype)
        out_spec = pl.BlockSpec((None, tm, tn), lambda i, j: (j, i, 0))
    else:
        out_shape = jax.ShapeDtypeStruct((m, n), out_dtype)
        out_spec = pl.BlockSpec((tm, tn), lambda i, j: (i, j))
    return _pallas(
        body, args, deps, name=name, grid=(m // tm, n // tn), in_specs=in_specs, out_specs=out_spec,
        out_shape=out_shape, compiler_params=_params("parallel", "parallel"))


def _mm_win_fwd(h, w_g, name, deps=()):
    m, k = h.shape
    nj = w_g.shape[2]
    tm = _tile(m, 512)

    def body(a_ref, b_ref, o_ref):
        o_ref[...] = jnp.dot(a_ref[...], b_ref[...], preferred_element_type=F32)

    return _pallas(
        body, [h, w_g], deps, name=name, grid=(N_DEV, m // tm),
        in_specs=[pl.BlockSpec((tm, k), lambda j, i: (i, 0)),
                  pl.BlockSpec((None, k, nj), lambda j, i: (j, 0, 0))],
        out_specs=pl.BlockSpec((tm, nj), lambda j, i: (i, j)),
        out_shape=jax.ShapeDtypeStruct((m, N_DEV * nj), F32),
        compiler_params=_params("parallel", "parallel"))


def _mm_win_bwd(dproj, w_g, name, deps=()):
    m = dproj.shape[0]
    d, nj = w_g.shape[1], w_g.shape[2]
    tm = _tile(m, 512)
    tn = _tile(d, 1024)

    def body(a_ref, b_ref, o_ref, acc_ref):
        j = pl.program_id(2)

        @pl.when(j == 0)
        def _():
            acc_ref[...] = jnp.zeros_like(acc_ref)

        acc_ref[...] += lax.dot_general(a_ref[...], b_ref[...], (((1,), (1,)), ((), ())),
                                        preferred_element_type=F32)

        @pl.when(j == N_DEV - 1)
        def _():
            o_ref[...] = acc_ref[...]

    return _pallas(
        body, [dproj, w_g], deps, name=name, grid=(m // tm, d // tn, N_DEV),
        in_specs=[pl.BlockSpec((tm, nj), lambda i, n, j: (i, j)),
                  pl.BlockSpec((None, tn, nj), lambda i, n, j: (j, n, 0))],
        out_specs=pl.BlockSpec((tm, tn), lambda i, n, j: (i, n)),
        out_shape=jax.ShapeDtypeStruct((m, d), F32),
        scratch_shapes=[pltpu.VMEM((tm, tn), F32)],
        compiler_params=_params("parallel", "parallel", "arbitrary"))


def _row_spec(tr, w, col):
    return pl.BlockSpec((tr, w), lambda i: (i, col))


def _full_spec(shape):
    return pl.BlockSpec(shape, lambda i: (0,) * len(shape))


def _rmsnorm_fwd(x, g, name, deps=()):
    l, d = x.shape
    tr = _tile(l, 256)

    def body(x_ref, g_ref, o_ref):
        xv = x_ref[...]
        rstd = lax.rsqrt(jnp.mean(xv * xv, axis=-1, keepdims=True) + RMS_EPS)
        o_ref[...] = (xv * rstd * g_ref[...]).astype(o_ref.dtype)

    return _pallas(
        body, [x, g.reshape(1, d)], deps, name=name, grid=(l // tr,),
        in_specs=[_row_spec(tr, d, 0), _full_spec((1, d))],
        out_specs=_row_spec(tr, d, 0),
        out_shape=jax.ShapeDtypeStruct((l, d), BF16),
        compiler_params=_params("parallel"))


def _rmsnorm_bwd(x, g, dh, dxo, name, deps=()):
    l, d = x.shape
    tr = _tile(l, 256)

    def body(x_ref, g_ref, dh_ref, dxo_ref, dx_ref, dg_ref):
        xv = x_ref[...]
        rstd = lax.rsqrt(jnp.mean(xv * xv, axis=-1, keepdims=True) + RMS_EPS)
        dhv = dh_ref[...]
        gdy = dhv * g_ref[...]
        dot = jnp.mean(gdy * xv, axis=-1, keepdims=True)
        dx_ref[...] = dxo_ref[...] + rstd * gdy - xv * (rstd * rstd * rstd * dot)

        @pl.when(pl.program_id(0) == 0)
        def _():
            dg_ref[...] = jnp.zeros_like(dg_ref)

        dg_ref[...] += jnp.sum(dhv * xv * rstd, axis=0, keepdims=True)

    return _pallas(
        body, [x, g.reshape(1, d), dh, dxo], deps, name=name, grid=(l // tr,),
        in_specs=[_row_spec(tr, d, 0), _full_spec((1, d)), _row_spec(tr, d, 0), _row_spec(tr, d, 0)],
        out_specs=[_row_spec(tr, d, 0), _full_spec((1, d))],
        out_shape=[jax.ShapeDtypeStruct((l, d), F32), jax.ShapeDtypeStruct((1, d), F32)],
        compiler_params=_params("arbitrary"))


def _final_loss(x, g, tgt, name):
    l, d = x.shape
    tr = _tile(l, 256)

    def body(x_ref, g_ref, t_ref, dx_ref, dg_ref, loss_ref):
        xv = x_ref[...]
        gv = g_ref[...]
        rstd = lax.rsqrt(jnp.mean(xv * xv, axis=-1, keepdims=True) + RMS_EPS)
        xn = xv * rstd
        err = xn * gv - t_ref[...]
        dy = err * (1.0 / d)
        gdy = dy * gv
        dot = jnp.mean(gdy * xv, axis=-1, keepdims=True)
        dx_ref[...] = rstd * gdy - xv * (rstd * rstd * rstd * dot)

        @pl.when(pl.program_id(0) == 0)
        def _():
            dg_ref[...] = jnp.zeros_like(dg_ref)
            loss_ref[...] = jnp.zeros_like(loss_ref)

        dg_ref[...] += jnp.sum(dy * xn, axis=0, keepdims=True)
        loss_ref[...] += (0.5 / d) * jnp.sum(err * err)

    return pl.pallas_call(
        body, name=name, grid=(l // tr,),
        in_specs=[_row_spec(tr, d, 0), _full_spec((1, d)), _row_spec(tr, d, 0)],
        out_specs=[_row_spec(tr, d, 0), _full_spec((1, d)), _full_spec((SUBLANES, LANES))],
        out_shape=[jax.ShapeDtypeStruct((l, d), F32), jax.ShapeDtypeStruct((1, d), F32),
                   jax.ShapeDtypeStruct((SUBLANES, LANES), F32)],
        compiler_params=_params("arbitrary"),
    )(x, g.reshape(1, d), tgt)


def _halo_spec(tr, w, col, nblk8, before):
    step = tr // SUBLANES
    if before:
        return pl.BlockSpec((SUBLANES, w), lambda i: (jnp.maximum(i * step - 1, 0), col))
    return pl.BlockSpec((SUBLANES, w), lambda i: (jnp.minimum((i + 1) * step, nblk8 - 1), col))


def _shift_down(cur, before, k):
    ext = jnp.concatenate([before, cur], axis=0)
    return pltpu.roll(ext, k, axis=0)[SUBLANES:, :]


def _shift_up(cur, after, k):
    tr = cur.shape[0]
    ext = jnp.concatenate([cur, after], axis=0)
    return pltpu.roll(ext, tr + SUBLANES - k, axis=0)[:tr, :]


def _branch_a_fwd(proj, conv_w, d, name, deps=()):
    l = proj.shape[0]
    tr = _tile(l, 256)
    nblk8 = l // SUBLANES

    def body(v_ref, bg_ref, cg_ref, za_ref, vh_ref, cgh_ref, w_ref, o_ref):
        first = pl.program_id(0) == 0
        cv = cg_ref[...] * v_ref[...]
        cvh = jnp.where(first, 0.0, cgh_ref[...] * vh_ref[...])
        w0, w1, w2 = w_ref[0:1, :], w_ref[1:2, :], w_ref[2:3, :]
        q = w2 * cv + w1 * _shift_down(cv, cvh, 1) + w0 * _shift_down(cv, cvh, 2)
        za = za_ref[...]
        o_ref[...] = (bg_ref[...] * q * (za * _sigmoid(za))).astype(o_ref.dtype)

    return _pallas(
        body, [proj, proj, proj, proj, proj, proj, conv_w], deps, name=name, grid=(l // tr,),
        in_specs=[_row_spec(tr, d, 0), _row_spec(tr, d, 1), _row_spec(tr, d, 2), _row_spec(tr, d, 3),
                  _halo_spec(tr, d, 0, nblk8, True), _halo_spec(tr, d, 2, nblk8, True),
                  _full_spec((SUBLANES, d))],
        out_specs=_row_spec(tr, d, 0),
        out_shape=jax.ShapeDtypeStruct((l, d), BF16),
        compiler_params=_params("parallel"))


def _branch_a_bwd(proj, dpa, conv_w, d, name):
    l = proj.shape[0]
    tr = _tile(l, 128)
    nblk8 = l // SUBLANES
    ntiles = l // tr

    def body(v_ref, bg_ref, cg_ref, za_ref, dpa_ref, vh_ref, cgh_ref, bgn_ref, zan_ref, dpan_ref,
             w_ref, dv_ref, dbg_ref, dcg_ref, dza_ref, dw0_ref, dw1_ref, dw2_ref):
        i = pl.program_id(0)
        v, bg, cg, za, dpa_v = v_ref[...], bg_ref[...], cg_ref[...], za_ref[...], dpa_ref[...]
        w0, w1, w2 = w_ref[0:1, :], w_ref[1:2, :], w_ref[2:3, :]
        cv = cg * v
        cvh = jnp.where(i == 0, 0.0, cgh_ref[...] * vh_ref[...])
        cv1 = _shift_down(cv, cvh, 1)
        cv2 = _shift_down(cv, cvh, 2)
        q = w2 * cv + w1 * cv1 + w0 * cv2
        sg = _sigmoid(za)
        s = za * sg
        dbg_ref[...] = (dpa_v * q * s).astype(dbg_ref.dtype)
        dza_ref[...] = (dpa_v * bg * q * (sg * (1.0 + za * (1.0 - sg)))).astype(dza_ref.dtype)
        dq = dpa_v * bg * s
        zan = zan_ref[...]
        dqn = jnp.where(i == ntiles - 1, 0.0, dpan_ref[...] * bgn_ref[...] * (zan * _sigmoid(zan)))
        dcv = w2 * dq + w1 * _shift_up(dq, dqn, 1) + w0 * _shift_up(dq, dqn, 2)
        dcg_ref[...] = (dcv * v).astype(dcg_ref.dtype)
        dv_ref[...] = (dcv * cg).astype(dv_ref.dtype)

        @pl.when(i == 0)
        def _():
            dw0_ref[...] = jnp.zeros_like(dw0_ref)
            dw1_ref[...] = jnp.zeros_like(dw1_ref)
            dw2_ref[...] = jnp.zeros_like(dw2_ref)

        dw0_ref[...] += jnp.sum(dq * cv2, axis=0, keepdims=True)
        dw1_ref[...] += jnp.sum(dq * cv1, axis=0, keepdims=True)
        dw2_ref[...] += jnp.sum(dq * cv, axis=0, keepdims=True)

    act = jax.ShapeDtypeStruct((l, d), BF16)
    wsum = jax.ShapeDtypeStruct((1, d), F32)
    return pl.pallas_call(
        body, name=name, grid=(ntiles,),
        in_specs=[_row_spec(tr, d, 0), _row_spec(tr, d, 1), _row_spec(tr, d, 2), _row_spec(tr, d, 3),
                  _row_spec(tr, d, 0),
                  _halo_spec(tr, d, 0, nblk8, True), _halo_spec(tr, d, 2, nblk8, True),
                  _halo_spec(tr, d, 1, nblk8, False), _halo_spec(tr, d, 3, nblk8, False),
                  _halo_spec(tr, d, 0, nblk8, False),
                  _full_spec((SUBLANES, d))],
        out_specs=[_row_spec(tr, d, 0)] * 4 + [_full_spec((1, d))] * 3,
        out_shape=[act] * 4 + [wsum] * 3,
        compiler_params=_params("arbitrary"),
    )(proj, proj, proj, proj, dpa, proj, proj, proj, proj, dpa, conv_w)


def _gelu_cast(y, name):
    l, w = y.shape
    tr = _tile(l, 512)

    def body(y_ref, o_ref):
        o_ref[...] = _gelu(y_ref[...]).astype(o_ref.dtype)

    return pl.pallas_call(
        body, name=name, grid=(l // tr,), in_specs=[_row_spec(tr, w, 0)],
        out_specs=_row_spec(tr, w, 0), out_shape=jax.ShapeDtypeStruct((l, w), BF16),
        compiler_params=_params("parallel"),
    )(y)


def _glu_post(y, gl, proj, b_glu, zb_col, name):
    l, w = y.shape
    tr = _tile(l, 512)

    def body(y_ref, gl_ref, zb_ref, b_ref, o_ref):
        zb = zb_ref[...]
        o_ref[...] = (_gelu(y_ref[...]) * _sigmoid(gl_ref[...] + b_ref[...])
                      * (zb * _sigmoid(zb))).astype(o_ref.dtype)

    return pl.pallas_call(
        body, name=name, grid=(l // tr,),
        in_specs=[_row_spec(tr, w, 0), _row_spec(tr, w, 0), _row_spec(tr, w, zb_col), _full_spec((1, w))],
        out_specs=_row_spec(tr, w, 0), out_shape=jax.ShapeDtypeStruct((l, w), BF16),
        compiler_params=_params("parallel"),
    )(y, gl, proj, b_glu.reshape(1, w))


def _glu_bwd1(y, gl, proj, b_glu, dpb, zb_col, name):
    l, w = y.shape
    tr = _tile(l, 512)

    def body(y_ref, gl_ref, zb_ref, b_ref, dpb_ref, dzb_ref, dgl_ref, t_ref, db_ref):
        zb = zb_ref[...]
        dpb_v = dpb_ref[...]
        yg = _gelu(y_ref[...])
        sgl = _sigmoid(gl_ref[...] + b_ref[...])
        szb = _sigmoid(zb)
        dzb_ref[...] = (dpb_v * yg * sgl * (szb * (1.0 + zb * (1.0 - szb)))).astype(dzb_ref.dtype)
        e = dpb_v * (zb * szb)
        dgl = e * yg * sgl * (1.0 - sgl)
        dgl_ref[...] = dgl.astype(dgl_ref.dtype)
        t_ref[...] = e * sgl

        @pl.when(pl.program_id(0) == 0)
        def _():
            db_ref[...] = jnp.zeros_like(db_ref)

        db_ref[...] += jnp.sum(dgl, axis=0, keepdims=True)

    return pl.pallas_call(
        body, name=name, grid=(l // tr,),
        in_specs=[_row_spec(tr, w, 0), _row_spec(tr, w, 0), _row_spec(tr, w, zb_col), _full_spec((1, w)),
                  _row_spec(tr, w, 0)],
        out_specs=[_row_spec(tr, w, 0)] * 3 + [_full_spec((1, w))],
        out_shape=[jax.ShapeDtypeStruct((l, w), BF16), jax.ShapeDtypeStruct((l, w), BF16),
                   jax.ShapeDtypeStruct((l, w), F32), jax.ShapeDtypeStruct((1, w), F32)],
        compiler_params=_params("arbitrary"),
    )(y, gl, proj, b_glu.reshape(1, w), dpb)


def _glu_bwd2(y, t1, dyg2, name, deps=()):
    l, w = y.shape
    tr = _tile(l, 512)

    def body(y_ref, t_ref, d_ref, o_ref):
        o_ref[...] = (t_ref[...] + d_ref[...]) * _gelu_grad(y_ref[...])

    return _pallas(
        body, [y, t1, dyg2], deps, name=name, grid=(l // tr,), in_specs=[_row_spec(tr, w, 0)] * 3,
        out_specs=_row_spec(tr, w, 0), out_shape=jax.ShapeDtypeStruct((l, w), F32),
        compiler_params=_params("parallel"))


def _merge_fwd(proj, ya, yb, d, ga_col, gb_col, name):
    l = proj.shape[0]
    tr = _tile(l, 256)

    def body(ga_ref, gb_ref, ya_ref, yb_ref, o_ref):
        o_ref[...] = (_sigmoid(ga_ref[...]) * ya_ref[...]
                      + _sigmoid(gb_ref[...]) * yb_ref[...]).astype(o_ref.dtype)

    return pl.pallas_call(
        body, name=name, grid=(l // tr,),
        in_specs=[_row_spec(tr, d, ga_col), _row_spec(tr, d, gb_col), _row_spec(tr, d, 0), _row_spec(tr, d, 0)],
        out_specs=_row_spec(tr, d, 0), out_shape=jax.ShapeDtypeStruct((l, d), BF16),
        compiler_params=_params("parallel"),
    )(proj, proj, ya, yb)


def _merge_bwd(proj, ya, yb, dm, d, ga_col, gb_col, name):
    l = proj.shape[0]
    tr = _tile(l, 256)

    def body(ga_ref, gb_ref, ya_ref, yb_ref, dm_ref, dya_ref, dyb_ref, dga_ref, dgb_ref):
        dmv = dm_ref[...]
        sa = _sigmoid(ga_ref[...])
        sb = _sigmoid(gb_ref[...])
        dya_ref[...] = (dmv * sa).astype(dya_ref.dtype)
        dyb_ref[...] = (dmv * sb).astype(dyb_ref.dtype)
        dga_ref[...] = (dmv * ya_ref[...] * sa * (1.0 - sa)).astype(dga_ref.dtype)
        dgb_ref[...] = (dmv * yb_ref[...] * sb * (1.0 - sb)).astype(dgb_ref.dtype)

    act = jax.ShapeDtypeStruct((l, d), BF16)
    return pl.pallas_call(
        body, name=name, grid=(l // tr,),
        in_specs=[_row_spec(tr, d, ga_col), _row_spec(tr, d, gb_col), _row_spec(tr, d, 0), _row_spec(tr, d, 0),
                  _row_spec(tr, d, 0)],
        out_specs=[_row_spec(tr, d, 0)] * 4, out_shape=[act] * 4,
        compiler_params=_params("parallel"),
    )(proj, proj, ya, yb, dm)


def _to_segments(a):
    l, w = a.shape
    return a.reshape(SUBLANES, l // SUBLANES, w).transpose(1, 0, 2).reshape(l, w)


def _from_segments(a):
    l, w = a.shape
    return a.reshape(l // SUBLANES, SUBLANES, w).transpose(1, 0, 2).reshape(l, w)


def _dense(z, shape):
    return jnp.broadcast_to(z, shape).reshape(-1, LANES)


def _s5_disc(are, aim, ldt):
    dt = jnp.exp(ldt)
    er = jnp.exp(are * dt)
    lbr = er * jnp.cos(aim * dt)
    lbi = er * jnp.sin(aim * dt)
    inv = 1.0 / (are * are + aim * aim)
    fr = ((lbr - 1.0) * are + lbi * aim) * inv
    fi = (lbi * are - (lbr - 1.0) * aim) * inv
    return dt, lbr, lbi, inv, fr, fi


def _s5_params(are, aim, ldt, bre, bim, name, deps=()):
    shape = are.shape

    def body(are_ref, aim_ref, ldt_ref, bre_ref, bim_ref, lbr_ref, lbi_ref, bbr_ref, bbi_ref):
        _, lbr, lbi, _, fr, fi = _s5_disc(are_ref[...], aim_ref[...], ldt_ref[...])
        lbr_ref[...] = lbr
        lbi_ref[...] = lbi
        bbr_ref[...] = fr * bre_ref[...] - fi * bim_ref[...]
        bbi_ref[...] = fr * bim_ref[...] + fi * bre_ref[...]

    out = jax.ShapeDtypeStruct(shape, F32)
    return _pallas(body, [are, aim, ldt, bre, bim], deps, name=name,
                   in_specs=[pl.BlockSpec(memory_space=pltpu.VMEM)] * 5, out_shape=[out] * 4,
                   compiler_params=pltpu.CompilerParams(vmem_limit_bytes=VMEM_LIMIT_BYTES))


def _s5_params_bwd(are, aim, ldt, bre, bim, glbr, glbi, gbbr, gbbi, n_groups, name, deps=()):
    shape = are.shape
    rows_per_group = shape[0] // n_groups

    def body(are_ref, aim_ref, ldt_ref, bre_ref, bim_ref, glbr_ref, glbi_ref, gbbr_ref, gbbi_ref,
             gar_ref, gai_ref, gdt_ref, gbr_ref, gbi_ref):
        are_v, aim_v = are_ref[...], aim_ref[...]
        bre_v, bim_v = bre_ref[...], bim_ref[...]
        gbbr_v, gbbi_v = gbbr_ref[...], gbbi_ref[...]
        dt, lbr, lbi, inv, fr, fi = _s5_disc(are_v, aim_v, ldt_ref[...])
        gbr_ref[...] = fr * gbbr_v + fi * gbbi_v
        gbi_ref[...] = fr * gbbi_v - fi * gbbr_v
        lane_group = lax.broadcasted_iota(jnp.int32, (LANES, LANES), 0) // S5_GROUP
        same_group = (lane_group == lax.broadcasted_iota(jnp.int32, (LANES, LANES), 1) // S5_GROUP)
        ones = same_group.astype(F32)
        gfr = jnp.dot(bre_v * gbbr_v + bim_v * gbbi_v, ones, precision=HIGHEST, preferred_element_type=F32)
        gfi = jnp.dot(bre_v * gbbi_v - bim_v * gbbr_v, ones, precision=HIGHEST, preferred_element_type=F32)
        glr = glbr_ref[...] + (are_v * gfr - aim_v * gfi) * inv
        gli = glbi_ref[...] + (are_v * gfi + aim_v * gfr) * inv
        qr = (fr * are_v + fi * aim_v) * inv
        qi = (fi * are_v - fr * aim_v) * inv
        gzr = lbr * glr + lbi * gli
        gzi = lbr * gli - lbi * glr
        gar_ref[...] = dt * gzr - (qr * gfr + qi * gfi)
        gai_ref[...] = dt * gzi - (qr * gfi - qi * gfr)
        e = dt * (are_v * gzr + aim_v * gzi)
        per_group = jnp.sum(e.reshape(n_groups, rows_per_group, LANES), axis=1)
        total = jnp.sum(per_group, axis=1, keepdims=True) * (1.0 / S5_GROUP)
        gdt_ref[...] = jnp.broadcast_to(total, gdt_ref.shape)

    out = jax.ShapeDtypeStruct(shape, F32)
    return _pallas(
        body, [are, aim, ldt, bre, bim, glbr, glbi, gbbr, gbbi], deps, name=name,
        in_specs=[pl.BlockSpec(memory_space=pltpu.VMEM)] * 9,
        out_shape=[out, out, jax.ShapeDtypeStruct((n_groups, LANES), F32), out, out],
        compiler_params=pltpu.CompilerParams(vmem_limit_bytes=VMEM_LIMIT_BYTES))


def _cmul(ar, ai, br, bi):
    return ar * br - ai * bi, ar * bi + ai * br


def _scan_in_place(hr_ref, hi_ref, lr, li, reverse):
    l, wb = hr_ref.shape
    nt = l // SUBLANES
    shift = SUBLANES - 1 if reverse else 1
    unroll = 8 if nt % 8 == 0 else 1

    def rows(k):
        t = (nt - 1 - k) if reverse else k
        return pl.ds(pl.multiple_of(t * SUBLANES, SUBLANES), SUBLANES)

    zero = jnp.zeros((SUBLANES, wb), F32)
    one = jnp.ones((SUBLANES, wb), F32)

    def local_step(k, carry):
        hr, hi, pr, pi = carry
        r = rows(k)
        tr_, ti_ = _cmul(lr, li, hr, hi)
        hr, hi = tr_ + hr_ref[r, :], ti_ + hi_ref[r, :]
        hr_ref[r, :] = hr
        hi_ref[r, :] = hi
        pr, pi = _cmul(lr, li, pr, pi)
        return hr, hi, pr, pi

    er, ei, lnr, lni = lax.fori_loop(0, nt, local_step, (zero, zero, one, zero), unroll=unroll)

    row = lax.broadcasted_iota(jnp.int32, (SUBLANES, wb), 0)
    tr_, ti_ = er, ei
    for j in range(1, SUBLANES):
        pr_, pi_ = _cmul(lnr, lni, pltpu.roll(tr_, shift, axis=0), pltpu.roll(ti_, shift, axis=0))
        at = row == ((SUBLANES - 1 - j) if reverse else j)
        tr_ = jnp.where(at, er + pr_, tr_)
        ti_ = jnp.where(at, ei + pi_, ti_)
    edge = row == ((SUBLANES - 1) if reverse else 0)
    cr = jnp.where(edge, 0.0, pltpu.roll(tr_, shift, axis=0))
    ci = jnp.where(edge, 0.0, pltpu.roll(ti_, shift, axis=0))

    def fix_step(k, carry):
        pr, pi = carry
        pr, pi = _cmul(lr, li, pr, pi)
        r = rows(k)
        ar_, ai_ = _cmul(pr, pi, cr, ci)
        hr_ref[r, :] = hr_ref[r, :] + ar_
        hi_ref[r, :] = hi_ref[r, :] + ai_
        return pr, pi

    lax.fori_loop(0, nt, fix_step, (one, zero), unroll=unroll)


def _dot(a, b):
    return jnp.dot(a.astype(BF16), b.astype(BF16), preferred_element_type=F32)


def _s5_forward(u_seg, mb_re, mb_im, mc_re, mc_im, lam_re, lam_im, dvec, name):
    l = u_seg.shape[0]
    nb, kin, kst = mb_re.shape

    def body(u_ref, mbr_ref, mbi_ref, mcr_ref, mci_ref, lr_ref, li_ref, d_ref, hr_ref, hi_ref, y_ref):
        u = u_ref[...]
        hr_ref[...] = _dot(u, mbr_ref[...])
        hi_ref[...] = _dot(u, mbi_ref[...])
        _scan_in_place(hr_ref, hi_ref, jnp.broadcast_to(lr_ref[...], (SUBLANES, kst)),
                       jnp.broadcast_to(li_ref[...], (SUBLANES, kst)), False)
        y_ref[...] = _dot(hr_ref[...], mcr_ref[...]) - _dot(hi_ref[...], mci_ref[...]) + d_ref[...] * u

    act = pl.BlockSpec((l, kin), lambda b: (0, b))
    state = pl.BlockSpec((l, kst), lambda b: (0, b))
    up = pl.BlockSpec((None, kin, kst), lambda b: (b, 0, 0))
    down = pl.BlockSpec((None, kst, kin), lambda b: (b, 0, 0))
    hshape = jax.ShapeDtypeStruct((l, nb * kst), F32)
    return pl.pallas_call(
        body, name=name, grid=(nb,),
        in_specs=[act, up, up, down, down, pl.BlockSpec((1, kst), lambda b: (0, b)),
                  pl.BlockSpec((1, kst), lambda b: (0, b)), pl.BlockSpec((1, kin), lambda b: (0, b))],
        out_specs=[state, state, act],
        out_shape=[hshape, hshape, jax.ShapeDtypeStruct((l, nb * kin), F32)],
        compiler_params=_params("parallel"),
    )(u_seg, mb_re, mb_im, mc_re, mc_im, lam_re, lam_im, dvec)


def _s5_backward(dy_seg, u_seg, dy_t, u_t, h_re, h_im, mg_re, mg_im, md_re, md_im, lam_re, lam_im_neg, dvec, name):
    l = dy_seg.shape[0]
    nb, kin, kst = mg_re.shape
    nt = l // SUBLANES

    def body(dy_ref, u_ref, dyt_ref, ut_ref, hr_ref, hi_ref, mgr_ref, mgi_ref, mdr_ref, mdi_ref, lr_ref, li_ref,
             d_ref, du_ref, gcr_ref, gci_ref, gbr_ref, gbi_ref, glr_ref, gli_ref, dsk_ref, qr_ref, qi_ref):
        dy = dy_ref[...]
        qr_ref[...] = _dot(dy, mgr_ref[...])
        qi_ref[...] = _dot(dy, mgi_ref[...])
        _scan_in_place(qr_ref, qi_ref, jnp.broadcast_to(lr_ref[...], (SUBLANES, kst)),
                       jnp.broadcast_to(li_ref[...], (SUBLANES, kst)), True)
        du_ref[...] = _dot(qr_ref[...], mdr_ref[...]) - _dot(qi_ref[...], mdi_ref[...]) + d_ref[...] * dy
        dsk_ref[...] = jnp.sum(dy * u_ref[...], axis=0, keepdims=True)
        gcr_ref[...] = _dot(dyt_ref[...], hr_ref[...])
        gci_ref[...] = _dot(dyt_ref[...], hi_ref[...])
        gbr_ref[...] = _dot(ut_ref[...], qr_ref[...])
        gbi_ref[...] = _dot(ut_ref[...], qi_ref[...])

        row = lax.broadcasted_iota(jnp.int32, (SUBLANES, kst), 0)
        last = pl.ds((nt - 1) * SUBLANES, SUBLANES)
        first = pl.ds(0, SUBLANES)
        pr = jnp.where(row == 0, 0.0, pltpu.roll(hr_ref[last, :], 1, axis=0))
        pi = jnp.where(row == 0, 0.0, pltpu.roll(hi_ref[last, :], 1, axis=0))
        gr, gi = qr_ref[first, :], qi_ref[first, :]

        def step(t, carry):
            acc_r, acc_i = carry
            cur = pl.ds(pl.multiple_of(t * SUBLANES, SUBLANES), SUBLANES)
            prev = pl.ds(pl.multiple_of((t - 1) * SUBLANES, SUBLANES), SUBLANES)
            gr, gi = qr_ref[cur, :], qi_ref[cur, :]
            pr, pi = hr_ref[prev, :], hi_ref[prev, :]
            return acc_r + gr * pr + gi * pi, acc_i + gi * pr - gr * pi

        acc_r, acc_i = lax.fori_loop(1, nt, step, (gr * pr + gi * pi, gi * pr - gr * pi))
        glr_ref[...] = jnp.sum(acc_r, axis=0, keepdims=True)
        gli_ref[...] = jnp.sum(acc_i, axis=0, keepdims=True)

    act = pl.BlockSpec((l, kin), lambda b: (0, b))
    act_t = pl.BlockSpec((kin, l), lambda b: (b, 0))
    state = pl.BlockSpec((l, kst), lambda b: (0, b))
    up = pl.BlockSpec((None, kin, kst), lambda b: (b, 0, 0))
    down = pl.BlockSpec((None, kst, kin), lambda b: (b, 0, 0))
    vec_st = pl.BlockSpec((1, kst), lambda b: (0, b))
    vec_in = pl.BlockSpec((1, kin), lambda b: (0, b))
    outer = jax.ShapeDtypeStruct((nb, kin, kst), F32)
    lam_shape = jax.ShapeDtypeStruct((1, nb * kst), F32)
    return pl.pallas_call(
        body, name=name, grid=(nb,),
        in_specs=[act, act, act_t, act_t, state, state, up, up, down, down, vec_st, vec_st, vec_in],
        out_specs=[act, up, up, up, up, vec_st, vec_st, vec_in],
        out_shape=[jax.ShapeDtypeStruct((l, nb * kin), F32), outer, outer, outer, outer, lam_shape, lam_shape,
                   jax.ShapeDtypeStruct((1, nb * kin), F32)],
        scratch_shapes=[pltpu.VMEM((l, kst), F32), pltpu.VMEM((l, kst), F32)],
        compiler_params=_params("parallel"),
    )(dy_seg, u_seg, dy_t, u_t, h_re, h_im, mg_re, mg_im, md_re, md_im, lam_re, lam_im_neg, dvec)


def _block_diag(m, nb):
    g, r, s = m.shape
    gb = g // nb
    eye = jnp.eye(gb, dtype=m.dtype)
    out = m.reshape(nb, gb, r, 1, s) * eye[None, :, None, :, None]
    return out.reshape(nb, gb * r, gb * s)


def _block_diag_extract(mat, g, r, s):
    nb = mat.shape[0]
    gb = g // nb
    eye = jnp.eye(gb, dtype=mat.dtype)
    m5 = mat.reshape(nb, gb, r, gb, s) * eye[None, :, None, :, None]
    return jnp.sum(m5, axis=3).reshape(g, r, s)


def _adamw(w, m, v, gslots, name, layer=0, prev=None):
    layers, r, c = w.shape
    s = gslots.shape[0]
    tr = _tile(r, max(SUBLANES, 1 << int(math.log2(ADAMW_BLOCK_ELEMS // c))))
    bc1 = 1.0 / (1.0 - ADAM_B1 ** ADAM_STEP)
    bc2 = 1.0 / (1.0 - ADAM_B2 ** ADAM_STEP)

    def body(w_ref, m_ref, v_ref, g_ref, *rest):
        go_ref, d_ref, mo_ref, vo_ref = rest[-4:]
        g = g_ref[0].astype(F32)
        for k in range(1, s):
            g = g + g_ref[k].astype(F32)
        mn = ADAM_B1 * m_ref[...] + (1.0 - ADAM_B1) * g
        vn = ADAM_B2 * v_ref[...] + (1.0 - ADAM_B2) * (g * g)
        go_ref[...] = g
        mo_ref[...] = mn
        vo_ref[...] = vn
        d_ref[...] = -ADAM_LR * ((mn * bc1) / (jnp.sqrt(vn * bc2) + ADAM_EPS) + ADAM_WD * w_ref[...])

    spec = pl.BlockSpec((None, tr, c), lambda i: (layer, i, 0))
    out = jax.ShapeDtypeStruct((layers, r, c), F32)
    in_specs = [spec, spec, spec, pl.BlockSpec((s, tr, c), lambda i: (0, i, 0))]
    args = [w, m, v, gslots]
    aliases = {}
    if prev is not None:
        in_specs += [pl.BlockSpec(memory_space=pl.ANY)] * 4
        args += list(prev)
        aliases = {4 + q: q for q in range(4)}
    return pl.pallas_call(
        body, name=name, grid=(r // tr,), in_specs=in_specs,
        out_specs=[spec] * 4, out_shape=[out] * 4, input_output_aliases=aliases,
        compiler_params=_params("parallel"),
    )(*args)


def _pack(parts):
    flat = jnp.concatenate([p.reshape(-1) for p in parts])
    pad = (-flat.shape[0]) % (PACK_ROWS * LANES)
    return jnp.pad(flat, (0, pad)).reshape(-1, LANES)


def _unpack(packed, shapes):
    flat = packed.reshape(-1)
    out, off = [], 0
    for shp in shapes:
        size = math.prod(shp)
        out.append(flat[off:off + size].reshape(shp))
        off += size
    return out


def kernel(x, norm_g, w_in, conv_w, w_out_a, a_re, a_im, log_dt, b_re, b_im, c_re, c_im, d_skip, w_glu, b_glu, w_out_b, w_o, final_g, loss_target, m_norm_g, m_w_in, m_conv_w, m_w_out_a, m_a_re, m_a_im, m_log_dt, m_b_re, m_b_im, m_c_re, m_c_im, m_d_skip, m_w_glu, m_b_glu, m_w_out_b, m_w_o, m_final_g, v_norm_g, v_w_in, v_conv_w, v_w_out_a, v_a_re, v_a_im, v_log_dt, v_b_re, v_b_im, v_c_re, v_c_im, v_d_skip, v_w_glu, v_b_glu, v_w_out_b, v_w_o, v_final_g):
    depth = norm_g.shape[0]
    l, d = x.shape[1], x.shape[2]
    ws = w_glu.shape[2]
    n_groups, n_state = a_re.shape[1], a_re.shape[2]
    nb = ws // LANES
    assert S5_GROUP == b_re.shape[3] and n_state * S5_GB == 4 * LANES
    u_col, zb_col = 4 * d // ws, 4 * d // ws + 1
    ga_col, gb_col = (4 * d + 2 * ws) // d, (4 * d + 2 * ws) // d + 1
    me = 4 * lax.axis_index("x") + 2 * lax.axis_index("y") + lax.axis_index("c")

    xs = [x[0]]
    tgt = loss_target[0]

    big_names = ("w_in", "w_out_a", "w_glu", "w_out_b", "w_o")
    big = dict(w_in=(w_in, m_w_in, v_w_in), w_out_a=(w_out_a, m_w_out_a, v_w_out_a),
               w_glu=(w_glu, m_w_glu, v_w_glu), w_out_b=(w_out_b, m_w_out_b, v_w_out_b),
               w_o=(w_o, m_w_o, v_w_o))

    def shards_bf16(i):
        return [big[k][0][i].astype(BF16) for k in big_names]

    def gather_start(shards, name, deps=()):
        sems, srcs, lands, token = _exchange_start(
            _plan_gather_chips, shards, [_landing(s_, N_DEV, me) for s_ in shards], f"{name}_start", deps)
        return (name, sems, srcs, lands), token

    def gather_forward(state, after, deps=()):
        name, sems, srcs, lands = state
        _, lands = _exchange_wait(_plan_gather_chips, sems, srcs, lands, after, f"{name}_wait")
        sems, _, lands, token = _exchange_start(_plan_gather_forward, None, lands, f"{name}_forward_start", deps)
        return (name, sems, lands), token

    def gather_finish(state, after):
        name, sems, lands = state
        return _exchange_wait(_plan_gather_forward, sems, None, lands, after, f"{name}_forward_wait")[1]

    conv_shard = jnp.pad(conv_w.reshape(depth * 3, -1), ((0, SUBLANES - depth * 3), (0, 0)))
    w_in_state, token = gather_start([shards_bf16(0)[0], conv_shard], "ag_w_in_0")
    s5 = []
    shape3 = (n_groups, n_state, S5_GROUP)
    for i in range(depth):
        dense_in = (_dense(a_re[i][:, :, None], shape3), _dense(a_im[i][:, :, None], shape3),
                    _dense(log_dt[i][:, None, None], shape3), b_re[i].reshape(-1, LANES), b_im[i].reshape(-1, LANES))
        lbr, lbi, bbr, bbi = _s5_params(*dense_in, f"s5_params_{i}", deps=(token,))
        bbr3, bbi3 = bbr.reshape(shape3), bbi.reshape(shape3)
        diag = lambda m: _block_diag(m, nb).astype(BF16)
        s5.append(dict(
            dense_in=dense_in,
            lam_re=lbr.reshape(shape3)[:, :, 0].reshape(1, -1), lam_im=lbi.reshape(shape3)[:, :, 0].reshape(1, -1),
            up=(diag(bbr3.transpose(0, 2, 1)), diag(bbi3.transpose(0, 2, 1))),
            down=(diag(c_re[i].transpose(0, 2, 1)), diag(c_im[i].transpose(0, 2, 1))),
            up_bwd=(diag(c_re[i]), diag(-c_im[i])), down_bwd=(diag(bbr3), diag(-bbi3))))
    prologue = [m for p in s5 for k in ("up", "down", "up_bwd", "down_bwd") for m in p[k]]
    prologue += [p[k] for p in s5 for k in ("lam_re", "lam_im")]
    w_in_state, token = gather_forward(w_in_state, prologue)
    rest_state, token = gather_start(shards_bf16(0)[1:], "ag_rest_0", deps=(token,))
    next_state = None
    if depth > 1:
        next_state, token = gather_start([shards_bf16(1)[0]], "ag_w_in_1", deps=(token,))

    saved = []
    wg = [None] * depth
    conv_full = None
    for i in range(depth):
        xi = xs[-1]
        h = _rmsnorm_fwd(xi, norm_g[i], f"rmsnorm_fwd_{i}", deps=(token,))
        arrived = gather_finish(w_in_state, [h])
        if i == 0:
            conv_full = arrived[1].transpose(1, 0, 2).reshape(SUBLANES, d)[:depth * 3].reshape(depth, 3, d)
        conv8 = jnp.pad(conv_full[i], ((0, SUBLANES - 3), (0, 0)))
        proj = _mm_win_fwd(h, arrived[0], f"mm_proj_{i}")
        u_seg = _to_segments(proj[:, 4 * d:4 * d + ws])
        h_re, h_im, y_seg = _s5_forward(u_seg, *s5[i]["up"], *s5[i]["down"], s5[i]["lam_re"], s5[i]["lam_im"],
                                        d_skip[i].reshape(1, ws), f"s5_forward_{i}")
        rest_state, token = gather_forward(rest_state, [y_seg])
        pa = _branch_a_fwd(proj, conv8, d, f"branch_a_fwd_{i}", deps=(token,))
        rest = gather_finish(rest_state, [pa])
        wg[i] = g = dict(w_in=arrived[0], w_a=rest[0].reshape(d, d), w_glu=rest[1].reshape(ws, ws),
                         w_b=rest[2], w_o=rest[3].reshape(d, d))
        ya = _mm(pa, g["w_a"], name=f"mm_ya_{i}")
        y = _from_segments(y_seg)
        yg = _gelu_cast(y, f"gelu_{i}")
        gl = _mm(yg, g["w_glu"], name=f"mm_glu_{i}")
        pb = _glu_post(y, gl, proj, b_glu[i], zb_col, f"glu_post_{i}")
        w_b2d = g["w_b"].transpose(1, 0, 2).reshape(ws, d)
        yb = _mm(pb, w_b2d, name=f"mm_yb_{i}")
        mrg = _merge_fwd(proj, ya, yb, d, ga_col, gb_col, f"merge_fwd_{i}")
        deps = ()
        if i + 1 < depth:
            w_in_state, token = gather_forward(next_state, [mrg])
            rest_state, token = gather_start(shards_bf16(i + 1)[1:], f"ag_rest_{i + 1}", deps=(token,))
            if i + 2 < depth:
                next_state, token = gather_start([shards_bf16(i + 2)[0]], f"ag_w_in_{i + 2}", deps=(token,))
            deps = (token,)
        xs.append(_mm(mrg, g["w_o"], name=f"mm_out_{i}", add=xi, deps=deps))
        saved.append(dict(h=h, proj=proj, pa=pa, ya=ya, yb=yb, y=y, yg=yg, gl=gl, pb=pb, mrg=mrg,
                          u_seg=u_seg, h_re=h_re, h_im=h_im, conv8=conv8, w_b2d=w_b2d))

    dx, g_final, loss_part = _final_loss(xs[-1], final_g, tgt, "final_loss")

    main_names = ("a_re", "a_im", "log_dt", "b_re", "b_im", "c_re", "c_im", "d_skip", "b_glu", "conv_w")
    rs_pending = []
    small = {k: [None] * depth for k in ("norm_g", "a_re", "a_im", "log_dt", "b_re", "b_im", "c_re", "c_im",
                                         "d_skip", "b_glu", "conv_w")}

    my_chip = 2 * lax.axis_index("x") + lax.axis_index("y")

    def reduce_on_chip(pieces, tag):
        lands = [lax.empty((4,) + p.shape[1:], p.dtype) for p in pieces]
        sems, srcs, lands, token = _exchange_start(_plan_reduce_sibling, pieces, lands, f"rs_sibling_start_{tag}")
        return (sems, srcs, lands), token

    def reduce_across_chips(names_, state, layer, tag, after):
        sems, srcs, lands = state
        srcs, lands = _exchange_wait(_plan_reduce_sibling, sems, srcs, lands, after, f"rs_sibling_wait_{tag}")
        sums = [_chip_sums(p, l_, f"chip_sum_{k}_{layer}") for k, p, l_ in zip(names_, srcs, lands)]
        lands = [_landing(lax.dynamic_index_in_dim(s_, my_chip, 0, keepdims=False), 4, my_chip) for s_ in sums]
        sems, srcs, lands, token = _exchange_start(_plan_reduce_chips, sums, lands, f"rs_chips_start_{tag}")
        rs_pending.append((names_, layer, sems, srcs, lands, f"rs_chips_wait_{tag}"))
        return token

    for i in reversed(range(depth)):
        s, g = saved[i], wg[i]
        proj = s["proj"]
        dxo_b = dx.astype(BF16)
        dm = _mm(dxo_b, g["w_o"], name=f"mm_dm_{i}", nt=True)
        gw_o = _mm(s["mrg"].T, dxo_b, name=f"mm_gw_o_{i}", out_dtype=BF16)
        dya, dyb, dga, dgb = _merge_bwd(proj, s["ya"], s["yb"], dm, d, ga_col, gb_col, f"merge_bwd_{i}")
        dpa = _mm(dya, g["w_a"], name=f"mm_dpa_{i}", nt=True)
        gw_a = _mm(s["pa"].T, dya, name=f"mm_gw_a_{i}", out_dtype=BF16)
        dpb = _mm(dyb, s["w_b2d"], name=f"mm_dpb_{i}", nt=True)
        gw_b = _mm(s["pb"].T, dyb, name=f"mm_gw_b_{i}", split_n=N_DEV, out_dtype=BF16)
        dv, dbg, dcg, dza, dw0, dw1, dw2 = _branch_a_bwd(proj, dpa, s["conv8"], d, f"branch_a_bwd_{i}")
        small["conv_w"][i] = jnp.concatenate([dw0, dw1, dw2], axis=0)
        dzb, dgl, t1, db_glu = _glu_bwd1(s["y"], s["gl"], proj, b_glu[i], dpb, zb_col, f"glu_bwd1_{i}")
        small["b_glu"][i] = db_glu.reshape(ws)
        dyg2 = _mm(dgl, g["w_glu"], name=f"mm_dyg_{i}", nt=True)
        gw_glu = _mm(s["yg"].T, dgl, name=f"mm_gw_glu_{i}", out_dtype=BF16)
        small_names_ = ("w_out_a", "w_glu", "w_out_b", "w_o")
        state, token = reduce_on_chip(
            [gw_a.reshape(N_DEV, d // N_DEV, d), gw_glu.reshape(N_DEV, ws // N_DEV, ws), gw_b,
             gw_o.reshape(N_DEV, d // N_DEV, d)], f"small_{i}")
        dy = _glu_bwd2(s["y"], t1, dyg2, f"glu_bwd2_{i}", deps=(token,))
        dy_seg = _to_segments(dy)
        u_seg = s["u_seg"]
        du_seg, gc_re, gc_im, gbb_re, gbb_im, glam_re, glam_im, dskip = _s5_backward(
            dy_seg, u_seg, dy_seg.T, u_seg.T, s["h_re"], s["h_im"], *s5[i]["up_bwd"], *s5[i]["down_bwd"],
            s5[i]["lam_re"], -s5[i]["lam_im"], d_skip[i].reshape(1, ws), f"s5_backward_{i}")
        token = reduce_across_chips(small_names_, state, i, f"small_{i}", [du_seg])
        small["d_skip"][i] = dskip.reshape(n_groups, S5_GROUP)
        small["c_re"][i] = _block_diag_extract(gc_re, n_groups, S5_GROUP, n_state)
        small["c_im"][i] = -_block_diag_extract(gc_im, n_groups, S5_GROUP, n_state)
        gbb_re = _block_diag_extract(gbb_re, n_groups, S5_GROUP, n_state).transpose(0, 2, 1)
        gbb_im = _block_diag_extract(gbb_im, n_groups, S5_GROUP, n_state).transpose(0, 2, 1)
        gar, gai, gdt, gbr, gbi = _s5_params_bwd(
            *s5[i]["dense_in"], _dense(glam_re.reshape(n_groups, n_state, 1), shape3),
            _dense(glam_im.reshape(n_groups, n_state, 1), shape3),
            gbb_re.reshape(-1, LANES), gbb_im.reshape(-1, LANES), n_groups, f"s5_params_bwd_{i}", deps=(token,))
        small["a_re"][i] = gar.reshape(shape3)[:, :, 0]
        small["a_im"][i] = gai.reshape(shape3)[:, :, 0]
        small["log_dt"][i] = gdt[:, 0]
        small["b_re"][i] = gbr.reshape(shape3)
        small["b_im"][i] = gbi.reshape(shape3)
        du = _from_segments(du_seg).astype(BF16)
        dproj = jnp.concatenate([dv, dbg, dcg, dza, du, dzb, dga, dgb], axis=1)
        deps = ()
        if i == 0:
            part = {k: jnp.stack(small[k]) for k in main_names}
            main_state, token = gather_start([_pack([part[k] for k in main_names]).astype(BF16)], "ag_small")
            deps = (token,)
        gw_in = _mm(s["h"].T, dproj, name=f"mm_gw_in_{i}", split_n=N_DEV, tm=1024,
                    out_dtype=BF16, deps=deps)
        state, token = reduce_on_chip([gw_in], f"w_in_{i}")
        if i == 0:
            token = reduce_across_chips(("w_in",), state, i, f"w_in_{i}", [])
            main_state, token = gather_forward(main_state, [token])
            dh = _mm_win_bwd(dproj, g["w_in"], f"mm_dh_{i}", deps=(token,))
            main_slots = gather_finish(main_state, [dh])[0]
            deps = ()
        else:
            dh = _mm_win_bwd(dproj, g["w_in"], f"mm_dh_{i}", deps=(token,))
            deps = (reduce_across_chips(("w_in",), state, i, f"w_in_{i}", [dh]),)
        dx, dng = _rmsnorm_bwd(xs[i], norm_g[i], dh, dx, f"rmsnorm_bwd_{i}", deps=deps)
        small["norm_g"][i] = dng.reshape(d)

    results = {}

    gain_grads = jnp.concatenate([jnp.stack(small["norm_g"]).reshape(-1), g_final.reshape(d)])
    gain_shapes = [(depth, d), (d,), (1,)]
    gpack = _all_gather(_pack([gain_grads, loss_part[0, :1]]), "ag_gains_loss")
    zero1 = jnp.zeros((1,), F32)
    gres = [_unpack(p[0], gain_shapes) for p in _adamw(
        _pack([norm_g, final_g, zero1])[None], _pack([m_norm_g, m_final_g, zero1])[None],
        _pack([v_norm_g, v_final_g, zero1])[None], gpack, "adamw_gains")]
    results["norm_g"] = [gres[q][0] for q in range(4)]
    results["final_g"] = [gres[q][1] for q in range(4)]
    loss = gres[0][2][0]

    small_w = dict(a_re=(a_re, m_a_re, v_a_re), a_im=(a_im, m_a_im, v_a_im),
                   log_dt=(log_dt, m_log_dt, v_log_dt), b_re=(b_re, m_b_re, v_b_re), b_im=(b_im, m_b_im, v_b_im),
                   c_re=(c_re, m_c_re, v_c_re), c_im=(c_im, m_c_im, v_c_im), d_skip=(d_skip, m_d_skip, v_d_skip),
                   b_glu=(b_glu, m_b_glu, v_b_glu))
    shapes = [part[k].shape for k in main_names]
    zeros_conv = jnp.zeros(part["conv_w"].shape, F32)
    wpack, mpack, vpack = [_pack([small_w[k][q] for k in main_names[:-1]] + [zeros_conv])[None] for q in range(3)]
    sres = [_unpack(p[0], shapes) for p in _adamw(wpack, mpack, vpack, main_slots, "adamw_small")]
    for j, k in enumerate(main_names[:-1]):
        results[k] = [sres[q][j] for q in range(4)]
    dc = d // N_DEV
    gconv = lax.dynamic_slice_in_dim(sres[0][-1], me * dc, dc, axis=2)
    pad8 = lambda a: jnp.pad(a.reshape(depth * 3, dc), ((0, SUBLANES - depth * 3), (0, 0)))[None]
    cres = _adamw(pad8(conv_w), pad8(m_conv_w), pad8(v_conv_w), pad8(gconv), "adamw_conv_w")
    results["conv_w"] = [r_[0, :depth * 3].reshape(depth, 3, dc) for r_ in cres]

    after = [cres[0]]
    for names_, layer, sems, srcs, lands, wait_name in rs_pending:
        _, slots = _exchange_wait(_plan_reduce_chips, sems, srcs, lands, after, wait_name)
        for k, land in zip(names_, slots):
            w_, m_, v_ = big[k]
            results[k] = _adamw(w_, m_, v_, land, f"adamw_{k}_{layer}", layer=layer, prev=results.get(k))
            after = [results[k][0]]

    names = ("norm_g", "w_in", "conv_w", "w_out_a", "a_re", "a_im", "log_dt", "b_re", "b_im", "c_re", "c_im",
             "d_skip", "w_glu", "b_glu", "w_out_b", "w_o", "final_g")
    outs = [loss, dx[None]]
    for q in range(4):
        outs += [results[k][q] for k in names]
    return tuple(outs)
```

```python
import functools
import math

import jax
import jax.numpy as jnp
from jax import lax
from jax.experimental import pallas as pl
from jax.experimental.pallas import tpu as pltpu

F32 = jnp.float32
BF16 = jnp.bfloat16
HIGHEST = lax.Precision.HIGHEST

N_DEV = 8
LANES = 128
SUBLANES = 8
VMEM_LIMIT_BYTES = 56 * 1024 * 1024

RMS_EPS = 1e-6
ADAM_LR = 0.001
ADAM_B1 = 0.9
ADAM_B2 = 0.999
ADAM_EPS = 1e-08
ADAM_WD = 0.01
ADAM_STEP = 10
GELU_C0 = math.sqrt(2.0 / math.pi)
GELU_C1 = 0.044715

ADAMW_BLOCK_ELEMS = 1 << 17
PACK_ROWS = 512

S5_GROUP = 16
S5_GB = LANES // S5_GROUP


def _params(*semantics):
    return pltpu.CompilerParams(dimension_semantics=semantics, vmem_limit_bytes=VMEM_LIMIT_BYTES)


ANY_SPEC = pl.BlockSpec(memory_space=pl.ANY)


def _pallas(body, args, deps=(), *, in_specs, **kwargs):
    deps = tuple(deps)
    if not deps:
        return pl.pallas_call(body, in_specs=in_specs, **kwargs)(*args)

    def body_after(*refs):
        body(*refs[len(deps):])

    return pl.pallas_call(body_after, in_specs=[ANY_SPEC] * len(deps) + list(in_specs), **kwargs)(*deps, *args)


def _tile(n, pref):
    t = min(n, pref)
    while n % t:
        assert t % 2 == 0, (n, pref)
        t //= 2
    return t


def _sigmoid(z):
    return 1.0 / (1.0 + jnp.exp(-z))


def _gelu(y):
    return 0.5 * y * (1.0 + jnp.tanh(GELU_C0 * (y + GELU_C1 * y * y * y)))


def _gelu_grad(y):
    t = jnp.tanh(GELU_C0 * (y + GELU_C1 * y * y * y))
    return 0.5 * (1.0 + t) + 0.5 * y * (1.0 - t * t) * GELU_C0 * (1.0 + 3.0 * GELU_C1 * y * y)


HBM_SPEC = pl.BlockSpec(memory_space=pltpu.HBM)
SEM_SPEC = pl.BlockSpec(memory_space=pltpu.SEMAPHORE)
DATAFLOW_EFFECT = pltpu.SideEffectType.DATAFLOW_SIDE_EFFECTING
OTHER_CHIPS = (2, 4, 6)


def _flip(pos, mask):
    x, y, c = pos
    return x ^ ((mask >> 2) & 1), y ^ ((mask >> 1) & 1), c ^ (mask & 1)


def _dev(pos):
    return 4 * pos[0] + 2 * pos[1] + pos[2]


def _chip(pos):
    return 2 * pos[0] + pos[1]


def _plan_gather_chips(me):
    return [(_flip(me, k), None, _dev(me), _dev(_flip(me, k))) for k in (1,) + OTHER_CHIPS]


def _plan_gather_forward(me):
    sib = _flip(me, 1)
    return [(sib, _dev(_flip(me, k)), _dev(_flip(me, k)), _dev(_flip(sib, k))) for k in OTHER_CHIPS]


def _plan_reduce_sibling(me):
    sib = _flip(me, 1)
    return [(sib, 2 * q + sib[2], q, q) for q in range(4)]


def _plan_reduce_chips(me):
    return [(_flip(me, k), _chip(_flip(me, k)), _chip(me), _chip(_flip(me, k))) for k in OTHER_CHIPS]


PLAN_COPIES = {_plan_gather_chips: 4, _plan_gather_forward: 3, _plan_reduce_sibling: 4, _plan_reduce_chips: 3}


def _exchange_copies(plan, src_refs, land_refs, send_sems, recv_sems):
    me = (lax.axis_index("x"), lax.axis_index("y"), lax.axis_index("c"))
    pairs = []
    for b, (src_ref, land_ref) in enumerate(zip(src_refs, land_refs)):
        for j, (peer, src_slot, there, here) in enumerate(plan(me)):
            sem = b * PLAN_COPIES[plan] + j
            src = src_ref if src_slot is None else src_ref.at[src_slot]
            out = pltpu.make_async_remote_copy(
                src_ref=src, dst_ref=land_ref.at[there], send_sem=send_sems.at[sem], recv_sem=recv_sems.at[sem],
                device_id=peer, device_id_type=pl.DeviceIdType.MESH)
            inc = pltpu.make_async_remote_copy(
                src_ref=src, dst_ref=land_ref.at[here], send_sem=send_sems.at[sem], recv_sem=recv_sems.at[sem],
                device_id=peer, device_id_type=pl.DeviceIdType.MESH)
            pairs.append((out, inc))
    return pairs


def _exchange_start(plan, srcs, lands, name, deps=()):
    srcs = [] if srcs is None else list(srcs)
    ns, n, nd = len(srcs), len(lands), len(deps)

    def body(*refs):
        land_refs = refs[ns:ns + n]
        sems_at = ns + n + nd
        pairs = _exchange_copies(plan, refs[:ns] if ns else land_refs, land_refs, refs[sems_at], refs[sems_at + 1])
        for out, _ in pairs:
            out.start()
        token = refs[-1]
        token[...] = jnp.zeros_like(token)

    sems = pltpu.SemaphoreType.DMA((PLAN_COPIES[plan] * n,))
    bufs = srcs + list(lands)
    outs = pl.pallas_call(
        body, name=name,
        out_shape=(sems, sems, *[pltpu.HBM(a.shape, a.dtype) for a in bufs],
                   jax.ShapeDtypeStruct((SUBLANES, LANES), F32)),
        in_specs=[HBM_SPEC] * (ns + n) + [ANY_SPEC] * nd,
        out_specs=(SEM_SPEC, SEM_SPEC, *[HBM_SPEC] * (ns + n), pl.BlockSpec(memory_space=pltpu.VMEM)),
        input_output_aliases={i: 2 + i for i in range(ns + n)},
        compiler_params=pltpu.CompilerParams(has_side_effects=DATAFLOW_EFFECT),
    )(*[pltpu.with_memory_space_constraint(a, pltpu.HBM) for a in bufs], *deps)
    return (outs[0], outs[1]), (outs[2:2 + ns] if ns else None), outs[2 + ns:2 + ns + n], outs[-1]


def _exchange_wait(plan, sems, srcs, lands, after, name):
    srcs = [] if srcs is None else list(srcs)
    ns, n = len(srcs), len(lands)

    def body(*refs):
        land_refs = refs[ns:ns + n]
        pairs = _exchange_copies(plan, refs[:ns] if ns else land_refs, land_refs, refs[ns + n], refs[ns + n + 1])
        for out, inc in pairs:
            out.wait_send()
            inc.wait_recv()

    bufs = srcs + list(lands)
    outs = pl.pallas_call(
        body, name=name,
        out_shape=[pltpu.HBM(a.shape, a.dtype) for a in bufs],
        in_specs=[HBM_SPEC] * (ns + n) + [SEM_SPEC, SEM_SPEC] + [ANY_SPEC] * len(after),
        out_specs=[HBM_SPEC] * (ns + n),
        input_output_aliases={i: i for i in range(ns + n)},
        compiler_params=pltpu.CompilerParams(has_side_effects=DATAFLOW_EFFECT),
    )(*bufs, sems[0], sems[1], *after)
    return outs[:ns], outs[ns:]


def _landing(own, slots, slot):
    land = lax.empty((slots,) + own.shape, own.dtype)
    return lax.dynamic_update_slice(land, own[None], (slot,) + (0,) * own.ndim)


def _chip_sums(pieces, land, name):
    _, r, c_ = land.shape
    tr = _tile(r, max(2 * SUBLANES, 1 << int(math.log2(4 * ADAMW_BLOCK_ELEMS // c_))))

    def body(core_ref, p_ref, l_ref, o_ref):
        o_ref[...] = (p_ref[...].astype(F32) + l_ref[...].astype(F32)).astype(o_ref.dtype)

    spec = pl.BlockSpec((None, tr, c_), lambda q, i, core: (q, i, 0))
    return pl.pallas_call(
        body, name=name,
        grid_spec=pltpu.PrefetchScalarGridSpec(
            num_scalar_prefetch=1, grid=(4, r // tr),
            in_specs=[pl.BlockSpec((None, tr, c_), lambda q, i, core: (2 * q + core[0], i, 0)), spec],
            out_specs=spec),
        out_shape=jax.ShapeDtypeStruct(land.shape, land.dtype),
        compiler_params=_params("parallel", "parallel"),
    )(lax.axis_index("c").reshape(1), pieces, land)


def _mm(a, b, *, name, nt=False, out_dtype=F32, add=None, split_n=None, tm=512, tn=1024, deps=()):
    m, k = a.shape
    n = b.shape[0] if nt else b.shape[1]
    tm = _tile(m, tm)
    tn = n // split_n if split_n else _tile(n, tn)
    dims = (((1,), (1,)), ((), ())) if nt else (((1,), (0,)), ((), ()))

    def body(*refs):
        a_ref, b_ref = refs[0], refs[1]
        o_ref = refs[-1]
        acc = lax.dot_general(a_ref[...], b_ref[...], dims, preferred_element_type=F32)
        if add is not None:
            acc = acc + refs[2][...]
        o_ref[...] = acc.astype(o_ref.dtype)

    in_specs = [pl.BlockSpec((tm, k), lambda i, j: (i, 0)),
                pl.BlockSpec((tn, k), lambda i, j: (j, 0)) if nt
                else pl.BlockSpec((k, tn), lambda i, j: (0, j))]
    args = [a, b]
    if add is not None:
        in_specs.append(pl.BlockSpec((tm, tn), lambda i, j: (i, j)))
        args.append(add)
    if split_n:
        out_shape = jax.ShapeDtypeStruct((split_n, m, tn), out_dtype)
        out_spec = pl.BlockSpec((None, tm, tn), lambda i, j: (j, i, 0))
    else:
        out_shape = jax.ShapeDtypeStruct((m, n), out_dtype)
        out_spec = pl.BlockSpec((tm, tn), lambda i, j: (i, j))
    return _pallas(
        body, args, deps, name=name, grid=(m // tm, n // tn), in_specs=in_specs, out_specs=out_spec,
        out_shape=out_shape, compiler_params=_params("parallel", "parallel"))


def _mm_win_fwd(h, w_g, name, deps=()):
    m, k = h.shape
    nj = w_g.shape[2]
    tm = _tile(m, 512)

    def body(a_ref, b_ref, o_ref):
        o_ref[...] = jnp.dot(a_ref[...], b_ref[...], preferred_element_type=F32).astype(o_ref.dtype)

    return _pallas(
        body, [h, w_g], deps, name=name, grid=(N_DEV, m // tm),
        in_specs=[pl.BlockSpec((tm, k), lambda j, i: (i, 0)),
                  pl.BlockSpec((None, k, nj), lambda j, i: (j, 0, 0))],
        out_specs=pl.BlockSpec((tm, nj), lambda j, i: (i, j)),
        out_shape=jax.ShapeDtypeStruct((m, N_DEV * nj), BF16),
        compiler_params=_params("parallel", "parallel"))


def _mm_win_bwd(dproj, w_g, name, deps=()):
    m = dproj.shape[0]
    d, nj = w_g.shape[1], w_g.shape[2]
    tm = _tile(m, 512)
    tn = _tile(d, 1024)

    def body(a_ref, b_ref, o_ref, acc_ref):
        j = pl.program_id(2)

        @pl.when(j == 0)
        def _():
            acc_ref[...] = jnp.zeros_like(acc_ref)

        acc_ref[...] += lax.dot_general(a_ref[...], b_ref[...], (((1,), (1,)), ((), ())),
                                        preferred_element_type=F32)

        @pl.when(j == N_DEV - 1)
        def _():
            o_ref[...] = acc_ref[...]

    return _pallas(
        body, [dproj, w_g], deps, name=name, grid=(m // tm, d // tn, N_DEV),
        in_specs=[pl.BlockSpec((tm, nj), lambda i, n, j: (i, j)),
                  pl.BlockSpec((None, tn, nj), lambda i, n, j: (j, n, 0))],
        out_specs=pl.BlockSpec((tm, tn), lambda i, n, j: (i, n)),
        out_shape=jax.ShapeDtypeStruct((m, d), F32),
        scratch_shapes=[pltpu.VMEM((tm, tn), F32)],
        compiler_params=_params("parallel", "parallel", "arbitrary"))


def _row_spec(tr, w, col):
    return pl.BlockSpec((tr, w), lambda i: (i, col))


def _full_spec(shape):
    return pl.BlockSpec(shape, lambda i: (0,) * len(shape))


def _rmsnorm_fwd(x, g, name, deps=()):
    l, d = x.shape
    tr = _tile(l, 256)

    def body(x_ref, g_ref, o_ref):
        xv = x_ref[...]
        rstd = lax.rsqrt(jnp.mean(xv * xv, axis=-1, keepdims=True) + RMS_EPS)
        o_ref[...] = (xv * rstd * g_ref[...]).astype(o_ref.dtype)

    return _pallas(
        body, [x, g.reshape(1, d)], deps, name=name, grid=(l // tr,),
        in_specs=[_row_spec(tr, d, 0), _full_spec((1, d))],
        out_specs=_row_spec(tr, d, 0),
        out_shape=jax.ShapeDtypeStruct((l, d), BF16),
        compiler_params=_params("parallel"))


def _rmsnorm_bwd(x, g, dh, dxo, name, deps=()):
    l, d = x.shape
    tr = _tile(l, 256)

    def body(x_ref, g_ref, dh_ref, dxo_ref, dx_ref, dg_ref):
        xv = x_ref[...]
        rstd = lax.rsqrt(jnp.mean(xv * xv, axis=-1, keepdims=True) + RMS_EPS)
        dhv = dh_ref[...]
        gdy = dhv * g_ref[...]
        dot = jnp.mean(gdy * xv, axis=-1, keepdims=True)
        dx_ref[...] = dxo_ref[...] + rstd * gdy - xv * (rstd * rstd * rstd * dot)

        @pl.when(pl.program_id(0) == 0)
        def _():
            dg_ref[...] = jnp.zeros_like(dg_ref)

        dg_ref[...] += jnp.sum(dhv * xv * rstd, axis=0, keepdims=True)

    return _pallas(
        body, [x, g.reshape(1, d), dh, dxo], deps, name=name, grid=(l // tr,),
        in_specs=[_row_spec(tr, d, 0), _full_spec((1, d)), _row_spec(tr, d, 0), _row_spec(tr, d, 0)],
        out_specs=[_row_spec(tr, d, 0), _full_spec((1, d))],
        out_shape=[jax.ShapeDtypeStruct((l, d), F32), jax.ShapeDtypeStruct((1, d), F32)],
        compiler_params=_params("arbitrary"))


def _final_loss(x, g, tgt, name):
    l, d = x.shape
    tr = _tile(l, 256)

    def body(x_ref, g_ref, t_ref, dx_ref, dg_ref, loss_ref):
        xv = x_ref[...]
        gv = g_ref[...]
        rstd = lax.rsqrt(jnp.mean(xv * xv, axis=-1, keepdims=True) + RMS_EPS)
        xn = xv * rstd
        err = xn * gv - t_ref[...]
        dy = err * (1.0 / d)
        gdy = dy * gv
        dot = jnp.mean(gdy * xv, axis=-1, keepdims=True)
        dx_ref[...] = rstd * gdy - xv * (rstd * rstd * rstd * dot)

        @pl.when(pl.program_id(0) == 0)
        def _():
            dg_ref[...] = jnp.zeros_like(dg_ref)
            loss_ref[...] = jnp.zeros_like(loss_ref)

        dg_ref[...] += jnp.sum(dy * xn, axis=0, keepdims=True)
        loss_ref[...] += (0.5 / d) * jnp.sum(err * err)

    return pl.pallas_call(
        body, name=name, grid=(l // tr,),
        in_specs=[_row_spec(tr, d, 0), _full_spec((1, d)), _row_spec(tr, d, 0)],
        out_specs=[_row_spec(tr, d, 0), _full_spec((1, d)), _full_spec((SUBLANES, LANES))],
        out_shape=[jax.ShapeDtypeStruct((l, d), F32), jax.ShapeDtypeStruct((1, d), F32),
                   jax.ShapeDtypeStruct((SUBLANES, LANES), F32)],
        compiler_params=_params("arbitrary"),
    )(x, g.reshape(1, d), tgt)


HALO = 2 * SUBLANES


def _halo_spec(tr, w, col, nblk, before):
    step = tr // HALO
    if before:
        return pl.BlockSpec((HALO, w), lambda i: (jnp.maximum(i * step - 1, 0), col))
    return pl.BlockSpec((HALO, w), lambda i: (jnp.minimum((i + 1) * step, nblk - 1), col))


def _shift_down(cur, before, k):
    ext = jnp.concatenate([before, cur], axis=0)
    return pltpu.roll(ext, k, axis=0)[HALO:, :]


def _shift_up(cur, after, k):
    tr = cur.shape[0]
    ext = jnp.concatenate([cur, after], axis=0)
    return pltpu.roll(ext, tr + HALO - k, axis=0)[:tr, :]


def _f32(ref):
    return ref[...].astype(F32)


def _branch_a_fwd(proj, conv_w, d, name, deps=()):
    l = proj.shape[0]
    tr = _tile(l, 256)
    nblk8 = l // HALO

    def body(v_ref, bg_ref, cg_ref, za_ref, vh_ref, cgh_ref, w_ref, o_ref):
        first = pl.program_id(0) == 0
        cv = _f32(cg_ref) * _f32(v_ref)
        cvh = jnp.where(first, 0.0, _f32(cgh_ref) * _f32(vh_ref))
        w0, w1, w2 = w_ref[0:1, :], w_ref[1:2, :], w_ref[2:3, :]
        q = w2 * cv + w1 * _shift_down(cv, cvh, 1) + w0 * _shift_down(cv, cvh, 2)
        za = _f32(za_ref)
        o_ref[...] = (_f32(bg_ref) * q * (za * _sigmoid(za))).astype(o_ref.dtype)

    return _pallas(
        body, [proj, proj, proj, proj, proj, proj, conv_w], deps, name=name, grid=(l // tr,),
        in_specs=[_row_spec(tr, d, 0), _row_spec(tr, d, 1), _row_spec(tr, d, 2), _row_spec(tr, d, 3),
                  _halo_spec(tr, d, 0, nblk8, True), _halo_spec(tr, d, 2, nblk8, True),
                  _full_spec((SUBLANES, d))],
        out_specs=_row_spec(tr, d, 0),
        out_shape=jax.ShapeDtypeStruct((l, d), BF16),
        compiler_params=_params("parallel"))


def _branch_a_bwd(proj, dpa, conv_w, d, name):
    l = proj.shape[0]
    tr = _tile(l, 128)
    nblk8 = l // HALO
    ntiles = l // tr

    def body(v_ref, bg_ref, cg_ref, za_ref, dpa_ref, vh_ref, cgh_ref, bgn_ref, zan_ref, dpan_ref,
             w_ref, dv_ref, dbg_ref, dcg_ref, dza_ref, dw0_ref, dw1_ref, dw2_ref):
        i = pl.program_id(0)
        v, bg, cg, za, dpa_v = _f32(v_ref), _f32(bg_ref), _f32(cg_ref), _f32(za_ref), dpa_ref[...]
        w0, w1, w2 = w_ref[0:1, :], w_ref[1:2, :], w_ref[2:3, :]
        cv = cg * v
        cvh = jnp.where(i == 0, 0.0, _f32(cgh_ref) * _f32(vh_ref))
        cv1 = _shift_down(cv, cvh, 1)
        cv2 = _shift_down(cv, cvh, 2)
        q = w2 * cv + w1 * cv1 + w0 * cv2
        sg = _sigmoid(za)
        s = za * sg
        dbg_ref[...] = (dpa_v * q * s).astype(dbg_ref.dtype)
        dza_ref[...] = (dpa_v * bg * q * (sg * (1.0 + za * (1.0 - sg)))).astype(dza_ref.dtype)
        dq = dpa_v * bg * s
        zan = _f32(zan_ref)
        dqn = jnp.where(i == ntiles - 1, 0.0, dpan_ref[...] * _f32(bgn_ref) * (zan * _sigmoid(zan)))
        dcv = w2 * dq + w1 * _shift_up(dq, dqn, 1) + w0 * _shift_up(dq, dqn, 2)
        dcg_ref[...] = (dcv * v).astype(dcg_ref.dtype)
        dv_ref[...] = (dcv * cg).astype(dv_ref.dtype)

        @pl.when(i == 0)
        def _():
            dw0_ref[...] = jnp.zeros_like(dw0_ref)
            dw1_ref[...] = jnp.zeros_like(dw1_ref)
            dw2_ref[...] = jnp.zeros_like(dw2_ref)

        dw0_ref[...] += jnp.sum(dq * cv2, axis=0, keepdims=True)
        dw1_ref[...] += jnp.sum(dq * cv1, axis=0, keepdims=True)
        dw2_ref[...] += jnp.sum(dq * cv, axis=0, keepdims=True)

    act = jax.ShapeDtypeStruct((l, d), BF16)
    wsum = jax.ShapeDtypeStruct((1, d), F32)
    return pl.pallas_call(
        body, name=name, grid=(ntiles,),
        in_specs=[_row_spec(tr, d, 0), _row_spec(tr, d, 1), _row_spec(tr, d, 2), _row_spec(tr, d, 3),
                  _row_spec(tr, d, 0),
                  _halo_spec(tr, d, 0, nblk8, True), _halo_spec(tr, d, 2, nblk8, True),
                  _halo_spec(tr, d, 1, nblk8, False), _halo_spec(tr, d, 3, nblk8, False),
                  _halo_spec(tr, d, 0, nblk8, False),
                  _full_spec((SUBLANES, d))],
        out_specs=[_row_spec(tr, d, 0)] * 4 + [_full_spec((1, d))] * 3,
        out_shape=[act] * 4 + [wsum] * 3,
        compiler_params=_params("arbitrary"),
    )(proj, proj, proj, proj, dpa, proj, proj, proj, proj, dpa, conv_w)


def _gelu_cast(y, name):
    l, w = y.shape
    tr = _tile(l, 512)

    def body(y_ref, o_ref):
        o_ref[...] = _gelu(y_ref[...]).astype(o_ref.dtype)

    return pl.pallas_call(
        body, name=name, grid=(l // tr,), in_specs=[_row_spec(tr, w, 0)],
        out_specs=_row_spec(tr, w, 0), out_shape=jax.ShapeDtypeStruct((l, w), BF16),
        compiler_params=_params("parallel"),
    )(y)


def _glu_post(y, gl, proj, b_glu, zb_col, name):
    l, w = y.shape
    tr = _tile(l, 512)

    def body(y_ref, gl_ref, zb_ref, b_ref, o_ref):
        zb = _f32(zb_ref)
        o_ref[...] = (_gelu(y_ref[...]) * _sigmoid(gl_ref[...] + b_ref[...])
                      * (zb * _sigmoid(zb))).astype(o_ref.dtype)

    return pl.pallas_call(
        body, name=name, grid=(l // tr,),
        in_specs=[_row_spec(tr, w, 0), _row_spec(tr, w, 0), _row_spec(tr, w, zb_col), _full_spec((1, w))],
        out_specs=_row_spec(tr, w, 0), out_shape=jax.ShapeDtypeStruct((l, w), BF16),
        compiler_params=_params("parallel"),
    )(y, gl, proj, b_glu.reshape(1, w))


def _glu_bwd1(y, gl, proj, b_glu, dpb, zb_col, name):
    l, w = y.shape
    tr = _tile(l, 512)

    def body(y_ref, gl_ref, zb_ref, b_ref, dpb_ref, dzb_ref, dgl_ref, t_ref, db_ref):
        zb = _f32(zb_ref)
        dpb_v = dpb_ref[...]
        yg = _gelu(y_ref[...])
        sgl = _sigmoid(gl_ref[...] + b_ref[...])
        szb = _sigmoid(zb)
        dzb_ref[...] = (dpb_v * yg * sgl * (szb * (1.0 + zb * (1.0 - szb)))).astype(dzb_ref.dtype)
        e = dpb_v * (zb * szb)
        dgl = e * yg * sgl * (1.0 - sgl)
        dgl_ref[...] = dgl.astype(dgl_ref.dtype)
        t_ref[...] = e * sgl

        @pl.when(pl.program_id(0) == 0)
        def _():
            db_ref[...] = jnp.zeros_like(db_ref)

        db_ref[...] += jnp.sum(dgl, axis=0, keepdims=True)

    return pl.pallas_call(
        body, name=name, grid=(l // tr,),
        in_specs=[_row_spec(tr, w, 0), _row_spec(tr, w, 0), _row_spec(tr, w, zb_col), _full_spec((1, w)),
                  _row_spec(tr, w, 0)],
        out_specs=[_row_spec(tr, w, 0)] * 3 + [_full_spec((1, w))],
        out_shape=[jax.ShapeDtypeStruct((l, w), BF16), jax.ShapeDtypeStruct((l, w), BF16),
                   jax.ShapeDtypeStruct((l, w), F32), jax.ShapeDtypeStruct((1, w), F32)],
        compiler_params=_params("arbitrary"),
    )(y, gl, proj, b_glu.reshape(1, w), dpb)


def _glu_bwd2(y, t1, dyg2, name, deps=()):
    l, w = y.shape
    tr = _tile(l, 512)

    def body(y_ref, t_ref, d_ref, o_ref):
        o_ref[...] = (t_ref[...] + d_ref[...]) * _gelu_grad(y_ref[...])

    return _pallas(
        body, [y, t1, dyg2], deps, name=name, grid=(l // tr,), in_specs=[_row_spec(tr, w, 0)] * 3,
        out_specs=_row_spec(tr, w, 0), out_shape=jax.ShapeDtypeStruct((l, w), F32),
        compiler_params=_params("parallel"))


def _merge_fwd(proj, ya, yb, d, ga_col, gb_col, name):
    l = proj.shape[0]
    tr = _tile(l, 256)

    def body(ga_ref, gb_ref, ya_ref, yb_ref, o_ref):
        o_ref[...] = (_sigmoid(_f32(ga_ref)) * ya_ref[...]
                      + _sigmoid(_f32(gb_ref)) * yb_ref[...]).astype(o_ref.dtype)

    return pl.pallas_call(
        body, name=name, grid=(l // tr,),
        in_specs=[_row_spec(tr, d, ga_col), _row_spec(tr, d, gb_col), _row_spec(tr, d, 0), _row_spec(tr, d, 0)],
        out_specs=_row_spec(tr, d, 0), out_shape=jax.ShapeDtypeStruct((l, d), BF16),
        compiler_params=_params("parallel"),
    )(proj, proj, ya, yb)


def _merge_bwd(proj, ya, yb, dm, d, ga_col, gb_col, name):
    l = proj.shape[0]
    tr = _tile(l, 256)

    def body(ga_ref, gb_ref, ya_ref, yb_ref, dm_ref, dya_ref, dyb_ref, dga_ref, dgb_ref):
        dmv = dm_ref[...]
        sa = _sigmoid(_f32(ga_ref))
        sb = _sigmoid(_f32(gb_ref))
        dya_ref[...] = (dmv * sa).astype(dya_ref.dtype)
        dyb_ref[...] = (dmv * sb).astype(dyb_ref.dtype)
        dga_ref[...] = (dmv * ya_ref[...] * sa * (1.0 - sa)).astype(dga_ref.dtype)
        dgb_ref[...] = (dmv * yb_ref[...] * sb * (1.0 - sb)).astype(dgb_ref.dtype)

    act = jax.ShapeDtypeStruct((l, d), BF16)
    return pl.pallas_call(
        body, name=name, grid=(l // tr,),
        in_specs=[_row_spec(tr, d, ga_col), _row_spec(tr, d, gb_col), _row_spec(tr, d, 0), _row_spec(tr, d, 0),
                  _row_spec(tr, d, 0)],
        out_specs=[_row_spec(tr, d, 0)] * 4, out_shape=[act] * 4,
        compiler_params=_params("parallel"),
    )(proj, proj, ya, yb, dm)


def _to_segments(a):
    l, w = a.shape
    return a.reshape(SUBLANES, l // SUBLANES, w).transpose(1, 0, 2).reshape(l, w)


def _from_segments(a):
    l, w = a.shape
    return a.reshape(l // SUBLANES, SUBLANES, w).transpose(1, 0, 2).reshape(l, w)


def _dense(z, shape):
    return jnp.broadcast_to(z, shape).reshape(-1, LANES)


def _s5_disc(are, aim, ldt):
    dt = jnp.exp(ldt)
    er = jnp.exp(are * dt)
    lbr = er * jnp.cos(aim * dt)
    lbi = er * jnp.sin(aim * dt)
    inv = 1.0 / (are * are + aim * aim)
    fr = ((lbr - 1.0) * are + lbi * aim) * inv
    fi = (lbi * are - (lbr - 1.0) * aim) * inv
    return dt, lbr, lbi, inv, fr, fi


def _s5_params(are, aim, ldt, bre, bim, name, deps=()):
    shape = are.shape

    def body(are_ref, aim_ref, ldt_ref, bre_ref, bim_ref, lbr_ref, lbi_ref, bbr_ref, bbi_ref):
        _, lbr, lbi, _, fr, fi = _s5_disc(are_ref[...], aim_ref[...], ldt_ref[...])
        lbr_ref[...] = lbr
        lbi_ref[...] = lbi
        bbr_ref[...] = fr * bre_ref[...] - fi * bim_ref[...]
        bbi_ref[...] = fr * bim_ref[...] + fi * bre_ref[...]

    out = jax.ShapeDtypeStruct(shape, F32)
    return _pallas(body, [are, aim, ldt, bre, bim], deps, name=name,
                   in_specs=[pl.BlockSpec(memory_space=pltpu.VMEM)] * 5, out_shape=[out] * 4,
                   compiler_params=pltpu.CompilerParams(vmem_limit_bytes=VMEM_LIMIT_BYTES))


def _s5_params_bwd(are, aim, ldt, bre, bim, glbr, glbi, gbbr, gbbi, n_groups, name, deps=()):
    shape = are.shape
    rows_per_group = shape[0] // n_groups

    def body(are_ref, aim_ref, ldt_ref, bre_ref, bim_ref, glbr_ref, glbi_ref, gbbr_ref, gbbi_ref,
             gar_ref, gai_ref, gdt_ref, gbr_ref, gbi_ref):
        are_v, aim_v = are_ref[...], aim_ref[...]
        bre_v, bim_v = bre_ref[...], bim_ref[...]
        gbbr_v, gbbi_v = gbbr_ref[...], gbbi_ref[...]
        dt, lbr, lbi, inv, fr, fi = _s5_disc(are_v, aim_v, ldt_ref[...])
        gbr_ref[...] = fr * gbbr_v + fi * gbbi_v
        gbi_ref[...] = fr * gbbi_v - fi * gbbr_v
        lane_group = lax.broadcasted_iota(jnp.int32, (LANES, LANES), 0) // S5_GROUP
        same_group = (lane_group == lax.broadcasted_iota(jnp.int32, (LANES, LANES), 1) // S5_GROUP)
        ones = same_group.astype(F32)
        gfr = jnp.dot(bre_v * gbbr_v + bim_v * gbbi_v, ones, precision=HIGHEST, preferred_element_type=F32)
        gfi = jnp.dot(bre_v * gbbi_v - bim_v * gbbr_v, ones, precision=HIGHEST, preferred_element_type=F32)
        glr = glbr_ref[...] + (are_v * gfr - aim_v * gfi) * inv
        gli = glbi_ref[...] + (are_v * gfi + aim_v * gfr) * inv
        qr = (fr * are_v + fi * aim_v) * inv
        qi = (fi * are_v - fr * aim_v) * inv
        gzr = lbr * glr + lbi * gli
        gzi = lbr * gli - lbi * glr
        gar_ref[...] = dt * gzr - (qr * gfr + qi * gfi)
        gai_ref[...] = dt * gzi - (qr * gfi - qi * gfr)
        e = dt * (are_v * gzr + aim_v * gzi)
        per_group = jnp.sum(e.reshape(n_groups, rows_per_group, LANES), axis=1)
        total = jnp.sum(per_group, axis=1, keepdims=True) * (1.0 / S5_GROUP)
        gdt_ref[...] = jnp.broadcast_to(total, gdt_ref.shape)

    out = jax.ShapeDtypeStruct(shape, F32)
    return _pallas(
        body, [are, aim, ldt, bre, bim, glbr, glbi, gbbr, gbbi], deps, name=name,
        in_specs=[pl.BlockSpec(memory_space=pltpu.VMEM)] * 9,
        out_shape=[out, out, jax.ShapeDtypeStruct((n_groups, LANES), F32), out, out],
        compiler_params=pltpu.CompilerParams(vmem_limit_bytes=VMEM_LIMIT_BYTES))


def _cmul(ar, ai, br, bi):
    return ar * br - ai * bi, ar * bi + ai * br


def _scan_in_place(hr_ref, hi_ref, lr, li, reverse):
    l, wb = hr_ref.shape
    nt = l // SUBLANES
    shift = SUBLANES - 1 if reverse else 1
    unroll = 8 if nt % 8 == 0 else 1

    def rows(k):
        t = (nt - 1 - k) if reverse else k
        return pl.ds(pl.multiple_of(t * SUBLANES, SUBLANES), SUBLANES)

    zero = jnp.zeros((SUBLANES, wb), F32)
    one = jnp.ones((SUBLANES, wb), F32)

    def local_step(k, carry):
        hr, hi, pr, pi = carry
        r = rows(k)
        tr_, ti_ = _cmul(lr, li, hr, hi)
        hr, hi = tr_ + hr_ref[r, :], ti_ + hi_ref[r, :]
        hr_ref[r, :] = hr
        hi_ref[r, :] = hi
        pr, pi = _cmul(lr, li, pr, pi)
        return hr, hi, pr, pi

    er, ei, lnr, lni = lax.fori_loop(0, nt, local_step, (zero, zero, one, zero), unroll=unroll)

    row = lax.broadcasted_iota(jnp.int32, (SUBLANES, wb), 0)
    tr_, ti_ = er, ei
    for j in range(1, SUBLANES):
        pr_, pi_ = _cmul(lnr, lni, pltpu.roll(tr_, shift, axis=0), pltpu.roll(ti_, shift, axis=0))
        at = row == ((SUBLANES - 1 - j) if reverse else j)
        tr_ = jnp.where(at, er + pr_, tr_)
        ti_ = jnp.where(at, ei + pi_, ti_)
    edge = row == ((SUBLANES - 1) if reverse else 0)
    cr = jnp.where(edge, 0.0, pltpu.roll(tr_, shift, axis=0))
    ci = jnp.where(edge, 0.0, pltpu.roll(ti_, shift, axis=0))

    def fix_step(k, carry):
        pr, pi = carry
        pr, pi = _cmul(lr, li, pr, pi)
        r = rows(k)
        ar_, ai_ = _cmul(pr, pi, cr, ci)
        hr_ref[r, :] = hr_ref[r, :] + ar_
        hi_ref[r, :] = hi_ref[r, :] + ai_
        return pr, pi

    lax.fori_loop(0, nt, fix_step, (one, zero), unroll=unroll)


def _dot(a, b):
    return jnp.dot(a.astype(BF16), b.astype(BF16), preferred_element_type=F32)


def _s5_forward(u_seg, mb_re, mb_im, mc_re, mc_im, lam_re, lam_im, dvec, name):
    l = u_seg.shape[0]
    nb, kin, kst = mb_re.shape

    def body(u_ref, mbr_ref, mbi_ref, mcr_ref, mci_ref, lr_ref, li_ref, d_ref, hr_ref, hi_ref, y_ref):
        u = u_ref[...]
        hr_ref[...] = _dot(u, mbr_ref[...])
        hi_ref[...] = _dot(u, mbi_ref[...])
        _scan_in_place(hr_ref, hi_ref, jnp.broadcast_to(lr_ref[...], (SUBLANES, kst)),
                       jnp.broadcast_to(li_ref[...], (SUBLANES, kst)), False)
        y_ref[...] = (_dot(hr_ref[...], mcr_ref[...]) - _dot(hi_ref[...], mci_ref[...])
                      + d_ref[...] * u.astype(F32))

    act = pl.BlockSpec((l, kin), lambda b: (0, b))
    state = pl.BlockSpec((l, kst), lambda b: (0, b))
    up = pl.BlockSpec((None, kin, kst), lambda b: (b, 0, 0))
    down = pl.BlockSpec((None, kst, kin), lambda b: (b, 0, 0))
    hshape = jax.ShapeDtypeStruct((l, nb * kst), F32)
    return pl.pallas_call(
        body, name=name, grid=(nb,),
        in_specs=[act, up, up, down, down, pl.BlockSpec((1, kst), lambda b: (0, b)),
                  pl.BlockSpec((1, kst), lambda b: (0, b)), pl.BlockSpec((1, kin), lambda b: (0, b))],
        out_specs=[state, state, act],
        out_shape=[hshape, hshape, jax.ShapeDtypeStruct((l, nb * kin), F32)],
        compiler_params=_params("parallel"),
    )(u_seg, mb_re, mb_im, mc_re, mc_im, lam_re, lam_im, dvec)


def _s5_backward(dy_seg, u_seg, dy_t, u_t, h_re, h_im, mg_re, mg_im, md_re, md_im, lam_re, lam_im_neg, dvec, name):
    l = dy_seg.shape[0]
    nb, kin, kst = mg_re.shape
    nt = l // SUBLANES

    def body(dy_ref, u_ref, dyt_ref, ut_ref, hr_ref, hi_ref, mgr_ref, mgi_ref, mdr_ref, mdi_ref, lr_ref, li_ref,
             d_ref, du_ref, gcr_ref, gci_ref, gbr_ref, gbi_ref, glr_ref, gli_ref, dsk_ref, qr_ref, qi_ref):
        dy = dy_ref[...]
        qr_ref[...] = _dot(dy, mgr_ref[...])
        qi_ref[...] = _dot(dy, mgi_ref[...])
        _scan_in_place(qr_ref, qi_ref, jnp.broadcast_to(lr_ref[...], (SUBLANES, kst)),
                       jnp.broadcast_to(li_ref[...], (SUBLANES, kst)), True)
        du_ref[...] = _dot(qr_ref[...], mdr_ref[...]) - _dot(qi_ref[...], mdi_ref[...]) + d_ref[...] * dy
        dsk_ref[...] = jnp.sum(dy * _f32(u_ref), axis=0, keepdims=True)
        gcr_ref[...] = _dot(dyt_ref[...], hr_ref[...])
        gci_ref[...] = _dot(dyt_ref[...], hi_ref[...])
        gbr_ref[...] = _dot(ut_ref[...], qr_ref[...])
        gbi_ref[...] = _dot(ut_ref[...], qi_ref[...])

        row = lax.broadcasted_iota(jnp.int32, (SUBLANES, kst), 0)
        last = pl.ds((nt - 1) * SUBLANES, SUBLANES)
        first = pl.ds(0, SUBLANES)
        pr = jnp.where(row == 0, 0.0, pltpu.roll(hr_ref[last, :], 1, axis=0))
        pi = jnp.where(row == 0, 0.0, pltpu.roll(hi_ref[last, :], 1, axis=0))
        gr, gi = qr_ref[first, :], qi_ref[first, :]

        def step(t, carry):
            acc_r, acc_i = carry
            cur = pl.ds(pl.multiple_of(t * SUBLANES, SUBLANES), SUBLANES)
            prev = pl.ds(pl.multiple_of((t - 1) * SUBLANES, SUBLANES), SUBLANES)
            gr, gi = qr_ref[cur, :], qi_ref[cur, :]
            pr, pi = hr_ref[prev, :], hi_ref[prev, :]
            return acc_r + gr * pr + gi * pi, acc_i + gi * pr - gr * pi

        acc_r, acc_i = lax.fori_loop(1, nt, step, (gr * pr + gi * pi, gi * pr - gr * pi))
        glr_ref[...] = jnp.sum(acc_r, axis=0, keepdims=True)
        gli_ref[...] = jnp.sum(acc_i, axis=0, keepdims=True)

    act = pl.BlockSpec((l, kin), lambda b: (0, b))
    act_t = pl.BlockSpec((kin, l), lambda b: (b, 0))
    state = pl.BlockSpec((l, kst), lambda b: (0, b))
    up = pl.BlockSpec((None, kin, kst), lambda b: (b, 0, 0))
    down = pl.BlockSpec((None, kst, kin), lambda b: (b, 0, 0))
    vec_st = pl.BlockSpec((1, kst), lambda b: (0, b))
    vec_in = pl.BlockSpec((1, kin), lambda b: (0, b))
    outer = jax.ShapeDtypeStruct((nb, kin, kst), F32)
    lam_shape = jax.ShapeDtypeStruct((1, nb * kst), F32)
    return pl.pallas_call(
        body, name=name, grid=(nb,),
        in_specs=[act, act, act_t, act_t, state, state, up, up, down, down, vec_st, vec_st, vec_in],
        out_specs=[act, up, up, up, up, vec_st, vec_st, vec_in],
        out_shape=[jax.ShapeDtypeStruct((l, nb * kin), F32), outer, outer, outer, outer, lam_shape, lam_shape,
                   jax.ShapeDtypeStruct((1, nb * kin), F32)],
        scratch_shapes=[pltpu.VMEM((l, kst), F32), pltpu.VMEM((l, kst), F32)],
        compiler_params=_params("parallel"),
    )(dy_seg, u_seg, dy_t, u_t, h_re, h_im, mg_re, mg_im, md_re, md_im, lam_re, lam_im_neg, dvec)


def _block_diag(m, nb):
    g, r, s = m.shape
    gb = g // nb
    eye = jnp.eye(gb, dtype=m.dtype)
    out = m.reshape(nb, gb, r, 1, s) * eye[None, :, None, :, None]
    return out.reshape(nb, gb * r, gb * s)


def _block_diag_extract(mat, g, r, s):
    nb = mat.shape[0]
    gb = g // nb
    eye = jnp.eye(gb, dtype=mat.dtype)
    m5 = mat.reshape(nb, gb, r, gb, s) * eye[None, :, None, :, None]
    return jnp.sum(m5, axis=3).reshape(g, r, s)


def _adamw(w, m, v, gslots, name, layer=0, prev=None, deps=()):
    layers, r, c = w.shape
    s = gslots.shape[0]
    tr = _tile(r, max(SUBLANES, 1 << int(math.log2(ADAMW_BLOCK_ELEMS // c))))
    bc1 = 1.0 / (1.0 - ADAM_B1 ** ADAM_STEP)
    bc2 = 1.0 / (1.0 - ADAM_B2 ** ADAM_STEP)

    def body(w_ref, m_ref, v_ref, g_ref, *rest):
        go_ref, d_ref, mo_ref, vo_ref = rest[-4:]
        g = g_ref[0].astype(F32)
        for k in range(1, s):
            g = g + g_ref[k].astype(F32)
        mn = ADAM_B1 * m_ref[...] + (1.0 - ADAM_B1) * g
        vn = ADAM_B2 * v_ref[...] + (1.0 - ADAM_B2) * (g * g)
        go_ref[...] = g
        mo_ref[...] = mn
        vo_ref[...] = vn
        d_ref[...] = -ADAM_LR * ((mn * bc1) / (jnp.sqrt(vn * bc2) + ADAM_EPS) + ADAM_WD * w_ref[...])

    spec = pl.BlockSpec((None, tr, c), lambda i: (layer, i, 0))
    out = jax.ShapeDtypeStruct((layers, r, c), F32)
    in_specs = [spec, spec, spec, pl.BlockSpec((s, tr, c), lambda i: (0, i, 0))]
    args = [w, m, v, gslots]
    aliases = {}
    if prev is not None:
        in_specs += [ANY_SPEC] * 4
        args += list(prev)
        aliases = {4 + q: q for q in range(4)}
    in_specs += [ANY_SPEC] * len(deps)
    args += list(deps)
    return pl.pallas_call(
        body, name=name, grid=(r // tr,), in_specs=in_specs,
        out_specs=[spec] * 4, out_shape=[out] * 4, input_output_aliases=aliases,
        compiler_params=_params("parallel"),
    )(*args)


def _pack(parts):
    flat = jnp.concatenate([p.reshape(-1) for p in parts])
    pad = (-flat.shape[0]) % (PACK_ROWS * LANES)
    return jnp.pad(flat, (0, pad)).reshape(-1, LANES)


def _unpack(packed, shapes):
    flat = packed.reshape(-1)
    out, off = [], 0
    for shp in shapes:
        size = math.prod(shp)
        out.append(flat[off:off + size].reshape(shp))
        off += size
    return out


def kernel(x, norm_g, w_in, conv_w, w_out_a, a_re, a_im, log_dt, b_re, b_im, c_re, c_im, d_skip, w_glu, b_glu, w_out_b, w_o, final_g, loss_target, m_norm_g, m_w_in, m_conv_w, m_w_out_a, m_a_re, m_a_im, m_log_dt, m_b_re, m_b_im, m_c_re, m_c_im, m_d_skip, m_w_glu, m_b_glu, m_w_out_b, m_w_o, m_final_g, v_norm_g, v_w_in, v_conv_w, v_w_out_a, v_a_re, v_a_im, v_log_dt, v_b_re, v_b_im, v_c_re, v_c_im, v_d_skip, v_w_glu, v_b_glu, v_w_out_b, v_w_o, v_final_g):
    depth = norm_g.shape[0]
    l, d = x.shape[1], x.shape[2]
    ws = w_glu.shape[2]
    n_groups, n_state = a_re.shape[1], a_re.shape[2]
    nb = ws // LANES
    assert S5_GROUP == b_re.shape[3] and n_state * S5_GB == 4 * LANES
    u_col, zb_col = 4 * d // ws, 4 * d // ws + 1
    ga_col, gb_col = (4 * d + 2 * ws) // d, (4 * d + 2 * ws) // d + 1
    me = 4 * lax.axis_index("x") + 2 * lax.axis_index("y") + lax.axis_index("c")

    xs = [x[0]]
    tgt = loss_target[0]

    big_names = ("w_in", "w_out_a", "w_glu", "w_out_b", "w_o")
    big = dict(w_in=(w_in, m_w_in, v_w_in), w_out_a=(w_out_a, m_w_out_a, v_w_out_a),
               w_glu=(w_glu, m_w_glu, v_w_glu), w_out_b=(w_out_b, m_w_out_b, v_w_out_b),
               w_o=(w_o, m_w_o, v_w_o))

    def shards_bf16(i):
        return [big[k][0][i].astype(BF16) for k in big_names]

    def gather_start(shards, name, deps=()):
        sems, srcs, lands, token = _exchange_start(
            _plan_gather_chips, shards, [_landing(s_, N_DEV, me) for s_ in shards], f"{name}_start", deps)
        return (name, sems, srcs, lands), token

    def gather_forward(state, after, deps=()):
        name, sems, srcs, lands = state
        _, lands = _exchange_wait(_plan_gather_chips, sems, srcs, lands, after, f"{name}_wait")
        sems, _, lands, token = _exchange_start(_plan_gather_forward, None, lands, f"{name}_forward_start", deps)
        return (name, sems, lands), token

    def gather_finish(state, after):
        name, sems, lands = state
        return _exchange_wait(_plan_gather_forward, sems, None, lands, after, f"{name}_forward_wait")[1]

    conv_shard = jnp.pad(conv_w.reshape(depth * 3, -1), ((0, SUBLANES - depth * 3), (0, 0)))
    w_in_state, token = gather_start([shards_bf16(0)[0], conv_shard], "ag_w_in_0")
    s5 = []
    shape3 = (n_groups, n_state, S5_GROUP)
    for i in range(depth):
        dense_in = (_dense(a_re[i][:, :, None], shape3), _dense(a_im[i][:, :, None], shape3),
                    _dense(log_dt[i][:, None, None], shape3), b_re[i].reshape(-1, LANES), b_im[i].reshape(-1, LANES))
        lbr, lbi, bbr, bbi = _s5_params(*dense_in, f"s5_params_{i}", deps=(token,))
        bbr3, bbi3 = bbr.reshape(shape3), bbi.reshape(shape3)
        diag = lambda m: _block_diag(m, nb).astype(BF16)
        s5.append(dict(
            dense_in=dense_in,
            lam_re=lbr.reshape(shape3)[:, :, 0].reshape(1, -1), lam_im=lbi.reshape(shape3)[:, :, 0].reshape(1, -1),
            up=(diag(bbr3.transpose(0, 2, 1)), diag(bbi3.transpose(0, 2, 1))),
            down=(diag(c_re[i].transpose(0, 2, 1)), diag(c_im[i].transpose(0, 2, 1))),
            up_bwd=(diag(c_re[i]), diag(-c_im[i])), down_bwd=(diag(bbr3), diag(-bbi3))))
    prologue = [m for p in s5 for k in ("up", "down", "up_bwd", "down_bwd") for m in p[k]]
    prologue += [p[k] for p in s5 for k in ("lam_re", "lam_im")]
    w_in_state, token = gather_forward(w_in_state, prologue)
    rest_state, token = gather_start(shards_bf16(0)[1:], "ag_rest_0", deps=(token,))
    next_state = None
    if depth > 1:
        next_state, token = gather_start([shards_bf16(1)[0]], "ag_w_in_1", deps=(token,))

    saved = []
    wg = [None] * depth
    conv_full = None
    for i in range(depth):
        xi = xs[-1]
        h = _rmsnorm_fwd(xi, norm_g[i], f"rmsnorm_fwd_{i}", deps=(token,))
        arrived = gather_finish(w_in_state, [h])
        if i == 0:
            conv_full = arrived[1].transpose(1, 0, 2).reshape(SUBLANES, d)[:depth * 3].reshape(depth, 3, d)
        conv8 = jnp.pad(conv_full[i], ((0, SUBLANES - 3), (0, 0)))
        proj = _mm_win_fwd(h, arrived[0], f"mm_proj_{i}")
        u_seg = _to_segments(proj[:, 4 * d:4 * d + ws])
        h_re, h_im, y_seg = _s5_forward(u_seg, *s5[i]["up"], *s5[i]["down"], s5[i]["lam_re"], s5[i]["lam_im"],
                                        d_skip[i].reshape(1, ws), f"s5_forward_{i}")
        rest_state, token = gather_forward(rest_state, [y_seg])
        pa = _branch_a_fwd(proj, conv8, d, f"branch_a_fwd_{i}", deps=(token,))
        rest = gather_finish(rest_state, [pa])
        wg[i] = g = dict(w_in=arrived[0], w_a=rest[0].reshape(d, d), w_glu=rest[1].reshape(ws, ws),
                         w_b=rest[2], w_o=rest[3].reshape(d, d))
        ya = _mm(pa, g["w_a"], name=f"mm_ya_{i}")
        y = _from_segments(y_seg)
        yg = _gelu_cast(y, f"gelu_{i}")
        gl = _mm(yg, g["w_glu"], name=f"mm_glu_{i}")
        pb = _glu_post(y, gl, proj, b_glu[i], zb_col, f"glu_post_{i}")
        w_b2d = g["w_b"].transpose(1, 0, 2).reshape(ws, d)
        yb = _mm(pb, w_b2d, name=f"mm_yb_{i}")
        mrg = _merge_fwd(proj, ya, yb, d, ga_col, gb_col, f"merge_fwd_{i}")
        deps = ()
        if i + 1 < depth:
            w_in_state, token = gather_forward(next_state, [mrg])
            rest_state, token = gather_start(shards_bf16(i + 1)[1:], f"ag_rest_{i + 1}", deps=(token,))
            if i + 2 < depth:
                next_state, token = gather_start([shards_bf16(i + 2)[0]], f"ag_w_in_{i + 2}", deps=(token,))
            deps = (token,)
        xs.append(_mm(mrg, g["w_o"], name=f"mm_out_{i}", add=xi, deps=deps))
        saved.append(dict(h=h, proj=proj, pa=pa, ya=ya, yb=yb, y=y, yg=yg, gl=gl, pb=pb, mrg=mrg,
                          u_seg=u_seg, h_re=h_re, h_im=h_im, conv8=conv8, w_b2d=w_b2d))

    dx, g_final, loss_part = _final_loss(xs[-1], final_g, tgt, "final_loss")

    main_names = ("a_re", "a_im", "log_dt", "b_re", "b_im", "c_re", "c_im", "d_skip", "b_glu", "conv_w")
    rs_pending = []
    small = {k: [None] * depth for k in ("norm_g", "a_re", "a_im", "log_dt", "b_re", "b_im", "c_re", "c_im",
                                         "d_skip", "b_glu", "conv_w")}

    my_chip = 2 * lax.axis_index("x") + lax.axis_index("y")

    def reduce_on_chip(pieces, tag):
        lands = [lax.empty((4,) + p.shape[1:], p.dtype) for p in pieces]
        sems, srcs, lands, token = _exchange_start(_plan_reduce_sibling, pieces, lands, f"rs_sibling_start_{tag}")
        return (sems, srcs, lands), token

    def reduce_across_chips(names_, state, layer, tag, after):
        sems, srcs, lands = state
        srcs, lands = _exchange_wait(_plan_reduce_sibling, sems, srcs, lands, after, f"rs_sibling_wait_{tag}")
        sums = [_chip_sums(p, l_, f"chip_sum_{k}_{layer}") for k, p, l_ in zip(names_, srcs, lands)]
        lands = [_landing(lax.dynamic_index_in_dim(s_, my_chip, 0, keepdims=False), 4, my_chip) for s_ in sums]
        sems, srcs, lands, token = _exchange_start(_plan_reduce_chips, sums, lands, f"rs_chips_start_{tag}")
        rs_pending.append((names_, layer, sems, srcs, lands, f"rs_chips_wait_{tag}"))
        return token

    for i in reversed(range(depth)):
        s, g = saved[i], wg[i]
        proj = s["proj"]
        dxo_b = dx.astype(BF16)
        dm = _mm(dxo_b, g["w_o"], name=f"mm_dm_{i}", nt=True)
        gw_o = _mm(s["mrg"].T, dxo_b, name=f"mm_gw_o_{i}", out_dtype=BF16)
        dya, dyb, dga, dgb = _merge_bwd(proj, s["ya"], s["yb"], dm, d, ga_col, gb_col, f"merge_bwd_{i}")
        dpa = _mm(dya, g["w_a"], name=f"mm_dpa_{i}", nt=True)
        gw_a = _mm(s["pa"].T, dya, name=f"mm_gw_a_{i}", out_dtype=BF16)
        dpb = _mm(dyb, s["w_b2d"], name=f"mm_dpb_{i}", nt=True)
        gw_b = _mm(s["pb"].T, dyb, name=f"mm_gw_b_{i}", split_n=N_DEV, out_dtype=BF16)
        dv, dbg, dcg, dza, dw0, dw1, dw2 = _branch_a_bwd(proj, dpa, s["conv8"], d, f"branch_a_bwd_{i}")
        small["conv_w"][i] = jnp.concatenate([dw0, dw1, dw2], axis=0)
        dzb, dgl, t1, db_glu = _glu_bwd1(s["y"], s["gl"], proj, b_glu[i], dpb, zb_col, f"glu_bwd1_{i}")
        small["b_glu"][i] = db_glu.reshape(ws)
        dyg2 = _mm(dgl, g["w_glu"], name=f"mm_dyg_{i}", nt=True)
        gw_glu = _mm(s["yg"].T, dgl, name=f"mm_gw_glu_{i}", out_dtype=BF16)
        small_names_ = ("w_out_a", "w_glu", "w_out_b", "w_o")
        state, token = reduce_on_chip(
            [gw_a.reshape(N_DEV, d // N_DEV, d), gw_glu.reshape(N_DEV, ws // N_DEV, ws), gw_b,
             gw_o.reshape(N_DEV, d // N_DEV, d)], f"small_{i}")
        dy = _glu_bwd2(s["y"], t1, dyg2, f"glu_bwd2_{i}", deps=(token,))
        dy_seg = _to_segments(dy)
        u_seg = s["u_seg"]
        du_seg, gc_re, gc_im, gbb_re, gbb_im, glam_re, glam_im, dskip = _s5_backward(
            dy_seg, u_seg, dy_seg.T, u_seg.T, s["h_re"], s["h_im"], *s5[i]["up_bwd"], *s5[i]["down_bwd"],
            s5[i]["lam_re"], -s5[i]["lam_im"], d_skip[i].reshape(1, ws), f"s5_backward_{i}")
        token = reduce_across_chips(small_names_, state, i, f"small_{i}", [du_seg])
        du = _from_segments(du_seg).astype(BF16)
        dproj = jnp.concatenate([dv, dbg, dcg, dza, du, dzb, dga, dgb], axis=1)
        gw_in = _mm(s["h"].T, dproj, name=f"mm_gw_in_{i}", split_n=N_DEV, tm=1024,
                    out_dtype=BF16, deps=(token,))
        state, token = reduce_on_chip([gw_in], f"w_in_{i}")
        small["d_skip"][i] = dskip.reshape(n_groups, S5_GROUP)
        small["c_re"][i] = _block_diag_extract(gc_re, n_groups, S5_GROUP, n_state)
        small["c_im"][i] = -_block_diag_extract(gc_im, n_groups, S5_GROUP, n_state)
        gbb_re = _block_diag_extract(gbb_re, n_groups, S5_GROUP, n_state).transpose(0, 2, 1)
        gbb_im = _block_diag_extract(gbb_im, n_groups, S5_GROUP, n_state).transpose(0, 2, 1)
        gar, gai, gdt, gbr, gbi = _s5_params_bwd(
            *s5[i]["dense_in"], _dense(glam_re.reshape(n_groups, n_state, 1), shape3),
            _dense(glam_im.reshape(n_groups, n_state, 1), shape3),
            gbb_re.reshape(-1, LANES), gbb_im.reshape(-1, LANES), n_groups, f"s5_params_bwd_{i}", deps=(token,))
        small["a_re"][i] = gar.reshape(shape3)[:, :, 0]
        small["a_im"][i] = gai.reshape(shape3)[:, :, 0]
        small["log_dt"][i] = gdt[:, 0]
        small["b_re"][i] = gbr.reshape(shape3)
        small["b_im"][i] = gbi.reshape(shape3)
        if i == 0:
            part = {k: jnp.stack(small[k]) for k in main_names}
            main_state, token = gather_start([_pack([part[k] for k in main_names]).astype(BF16)], "ag_small")
            token = reduce_across_chips(("w_in",), state, i, f"w_in_{i}", [token])
            main_state, token = gather_forward(main_state, [token])
            dh = _mm_win_bwd(dproj, g["w_in"], f"mm_dh_{i}", deps=(token,))
            main_slots = gather_finish(main_state, [dh])[0]
            deps = ()
        else:
            dh = _mm_win_bwd(dproj, g["w_in"], f"mm_dh_{i}", deps=(gar,))
            deps = (reduce_across_chips(("w_in",), state, i, f"w_in_{i}", [dh]),)
        dx, dng = _rmsnorm_bwd(xs[i], norm_g[i], dh, dx, f"rmsnorm_bwd_{i}", deps=deps)
        small["norm_g"][i] = dng.reshape(d)

    results = {}

    gain_grads = jnp.concatenate([jnp.stack(small["norm_g"]).reshape(-1), g_final.reshape(d)])
    gain_shapes = [(depth, d), (d,), (1,)]
    gains_state, token = gather_start([_pack([gain_grads, loss_part[0, :1]])], "ag_gains")

    small_w = dict(a_re=(a_re, m_a_re, v_a_re), a_im=(a_im, m_a_im, v_a_im),
                   log_dt=(log_dt, m_log_dt, v_log_dt), b_re=(b_re, m_b_re, v_b_re), b_im=(b_im, m_b_im, v_b_im),
                   c_re=(c_re, m_c_re, v_c_re), c_im=(c_im, m_c_im, v_c_im), d_skip=(d_skip, m_d_skip, v_d_skip),
                   b_glu=(b_glu, m_b_glu, v_b_glu))
    shapes = [part[k].shape for k in main_names]
    zeros_conv = jnp.zeros(part["conv_w"].shape, F32)
    wpack, mpack, vpack = [_pack([small_w[k][q] for k in main_names[:-1]] + [zeros_conv])[None] for q in range(3)]
    sres = [_unpack(p[0], shapes)
            for p in _adamw(wpack, mpack, vpack, main_slots, "adamw_small", deps=(token,))]
    for j, k in enumerate(main_names[:-1]):
        results[k] = [sres[q][j] for q in range(4)]
    dc = d // N_DEV
    gconv = lax.dynamic_slice_in_dim(sres[0][-1], me * dc, dc, axis=2)
    pad8 = lambda a: jnp.pad(a.reshape(depth * 3, dc), ((0, SUBLANES - depth * 3), (0, 0)))[None]
    cres = _adamw(pad8(conv_w), pad8(m_conv_w), pad8(v_conv_w), pad8(gconv), "adamw_conv_w")
    results["conv_w"] = [r_[0, :depth * 3].reshape(depth, 3, dc) for r_ in cres]

    gains_state, token = gather_forward(gains_state, [cres[0]])
    after, deps = [token], (token,)
    for names_, layer, sems, srcs, lands, wait_name in rs_pending:
        _, slots = _exchange_wait(_plan_reduce_chips, sems, srcs, lands, after, wait_name)
        for k, land in zip(names_, slots):
            w_, m_, v_ = big[k]
            results[k] = _adamw(w_, m_, v_, land, f"adamw_{k}_{layer}", layer=layer, prev=results.get(k),
                                deps=deps)
            after, deps = [results[k][0]], ()

    gpack = gather_finish(gains_state, after)[0]
    zero1 = jnp.zeros((1,), F32)
    gres = [_unpack(p[0], gain_shapes) for p in _adamw(
        _pack([norm_g, final_g, zero1])[None], _pack([m_norm_g, m_final_g, zero1])[None],
        _pack([v_norm_g, v_final_g, zero1])[None], gpack, "adamw_gains")]
    results["norm_g"] = [gres[q][0] for q in range(4)]
    results["final_g"] = [gres[q][1] for q in range(4)]
    loss = gres[0][2][0]

    names = ("norm_g", "w_in", "conv_w", "w_out_a", "a_re", "a_im", "log_dt", "b_re", "b_im", "c_re", "c_im",
             "d_skip", "w_glu", "b_glu", "w_out_b", "w_o", "final_g")
    outs = [loss, dx[None]]
    for q in range(4):
        outs += [results[k][q] for k in names]
    return tuple(outs)
```

```python
import functools
import math

import jax
import jax.numpy as jnp
from jax import lax
from jax.experimental import pallas as pl
from jax.experimental.pallas import tpu as pltpu

F32 = jnp.float32
BF16 = jnp.bfloat16
HIGHEST = lax.Precision.HIGHEST

N_DEV = 8
LANES = 128
SUBLANES = 8
VMEM_LIMIT_BYTES = 56 * 1024 * 1024

RMS_EPS = 1e-6
ADAM_LR = 0.001
ADAM_B1 = 0.9
ADAM_B2 = 0.999
ADAM_EPS = 1e-08
ADAM_WD = 0.01
ADAM_STEP = 10
GELU_C0 = math.sqrt(2.0 / math.pi)
GELU_C1 = 0.044715

ADAMW_BLOCK_ELEMS = 1 << 17
PACK_ROWS = 512

S5_GROUP = 16
S5_GB = LANES // S5_GROUP


def _params(*semantics):
    return pltpu.CompilerParams(dimension_semantics=semantics, vmem_limit_bytes=VMEM_LIMIT_BYTES)


ANY_SPEC = pl.BlockSpec(memory_space=pl.ANY)


def _pallas(body, args, deps=(), *, in_specs, **kwargs):
    deps = tuple(deps)
    if not deps:
        return pl.pallas_call(body, in_specs=in_specs, **kwargs)(*args)

    def body_after(*refs):
        body(*refs[len(deps):])

    return pl.pallas_call(body_after, in_specs=[ANY_SPEC] * len(deps) + list(in_specs), **kwargs)(*deps, *args)


def _tile(n, pref):
    t = min(n, pref)
    while n % t:
        assert t % 2 == 0, (n, pref)
        t //= 2
    return t


def _sigmoid(z):
    return 1.0 / (1.0 + jnp.exp(-z))


def _gelu(y):
    return 0.5 * y * (1.0 + jnp.tanh(GELU_C0 * (y + GELU_C1 * y * y * y)))


def _gelu_grad(y):
    t = jnp.tanh(GELU_C0 * (y + GELU_C1 * y * y * y))
    return 0.5 * (1.0 + t) + 0.5 * y * (1.0 - t * t) * GELU_C0 * (1.0 + 3.0 * GELU_C1 * y * y)


HBM_SPEC = pl.BlockSpec(memory_space=pltpu.HBM)
SEM_SPEC = pl.BlockSpec(memory_space=pltpu.SEMAPHORE)
DATAFLOW_EFFECT = pltpu.SideEffectType.DATAFLOW_SIDE_EFFECTING
OTHER_CHIPS = (2, 4, 6)


def _flip(pos, mask):
    x, y, c = pos
    return x ^ ((mask >> 2) & 1), y ^ ((mask >> 1) & 1), c ^ (mask & 1)


def _dev(pos):
    return 4 * pos[0] + 2 * pos[1] + pos[2]


def _chip(pos):
    return 2 * pos[0] + pos[1]


def _plan_gather_chips(me):
    return [(_flip(me, k), None, _dev(me), _dev(_flip(me, k))) for k in (1,) + OTHER_CHIPS]


def _plan_gather_forward(me):
    sib = _flip(me, 1)
    return [(sib, _dev(_flip(me, k)), _dev(_flip(me, k)), _dev(_flip(sib, k))) for k in OTHER_CHIPS]


def _plan_reduce_sibling(me):
    sib = _flip(me, 1)
    return [(sib, 2 * q + sib[2], q, q) for q in range(4)]


def _plan_reduce_chips(me):
    return [(_flip(me, k), _chip(_flip(me, k)), _chip(me), _chip(_flip(me, k))) for k in OTHER_CHIPS]


PLAN_COPIES = {_plan_gather_chips: 4, _plan_gather_forward: 3, _plan_reduce_sibling: 4, _plan_reduce_chips: 3}


def _exchange_copies(plan, src_refs, land_refs, send_sems, recv_sems, incoming=True):
    me = (lax.axis_index("x"), lax.axis_index("y"), lax.axis_index("c"))
    pairs = []
    for b, (src_ref, land_ref) in enumerate(zip(src_refs, land_refs)):
        for j, (peer, src_slot, there, here) in enumerate(plan(me)):
            sem = b * PLAN_COPIES[plan] + j
            src = src_ref if src_slot is None else src_ref.at[src_slot]
            out = pltpu.make_async_remote_copy(
                src_ref=src, dst_ref=land_ref.at[there], send_sem=send_sems.at[sem], recv_sem=recv_sems.at[sem],
                device_id=peer, device_id_type=pl.DeviceIdType.MESH)
            inc = pltpu.make_async_remote_copy(
                src_ref=src, dst_ref=land_ref.at[here], send_sem=send_sems.at[sem], recv_sem=recv_sems.at[sem],
                device_id=peer, device_id_type=pl.DeviceIdType.MESH) if incoming else None
            pairs.append((out, inc))
    return pairs


def _exchange_start(plan, srcs, lands, name, deps=()):
    srcs = [] if srcs is None else list(srcs)
    ns, n, nd = len(srcs), len(lands), len(deps)

    def body(*refs):
        land_refs = refs[ns:ns + n]
        sems_at = ns + n + nd
        pairs = _exchange_copies(plan, refs[:ns] if ns else land_refs, land_refs, refs[sems_at], refs[sems_at + 1],
                                 incoming=False)
        for out, _ in pairs:
            out.start()
        token = refs[-1]
        token[...] = jnp.zeros_like(token)

    sems = pltpu.SemaphoreType.DMA((PLAN_COPIES[plan] * n,))
    bufs = srcs + list(lands)
    outs = pl.pallas_call(
        body, name=name,
        out_shape=(sems, sems, *[pltpu.HBM(a.shape, a.dtype) for a in bufs],
                   jax.ShapeDtypeStruct((SUBLANES, LANES), F32)),
        in_specs=[HBM_SPEC] * (ns + n) + [ANY_SPEC] * nd,
        out_specs=(SEM_SPEC, SEM_SPEC, *[HBM_SPEC] * (ns + n), pl.BlockSpec(memory_space=pltpu.VMEM)),
        input_output_aliases={i: 2 + i for i in range(ns + n)},
        compiler_params=pltpu.CompilerParams(has_side_effects=DATAFLOW_EFFECT),
    )(*[pltpu.with_memory_space_constraint(a, pltpu.HBM) for a in bufs], *deps)
    return (outs[0], outs[1]), (outs[2:2 + ns] if ns else None), outs[2 + ns:2 + ns + n], outs[-1]


def _exchange_wait(plan, sems, srcs, lands, after, name):
    srcs = [] if srcs is None else list(srcs)
    ns, n = len(srcs), len(lands)

    def body(*refs):
        land_refs = refs[ns:ns + n]
        pairs = _exchange_copies(plan, refs[:ns] if ns else land_refs, land_refs, refs[ns + n], refs[ns + n + 1])
        for out, inc in pairs:
            out.wait_send()
            inc.wait_recv()

    bufs = srcs + list(lands)
    outs = pl.pallas_call(
        body, name=name,
        out_shape=[pltpu.HBM(a.shape, a.dtype) for a in bufs],
        in_specs=[HBM_SPEC] * (ns + n) + [SEM_SPEC, SEM_SPEC] + [ANY_SPEC] * len(after),
        out_specs=[HBM_SPEC] * (ns + n),
        input_output_aliases={i: i for i in range(ns + n)},
        compiler_params=pltpu.CompilerParams(has_side_effects=DATAFLOW_EFFECT),
    )(*bufs, sems[0], sems[1], *after)
    return outs[:ns], outs[ns:]


def _landing(own, slots, slot):
    land = lax.empty((slots,) + own.shape, own.dtype)
    return lax.dynamic_update_slice(land, own[None], (slot,) + (0,) * own.ndim)


def _chip_sums(pieces, land, name):
    _, r, c_ = land.shape
    tr = _tile(r, max(2 * SUBLANES, 1 << int(math.log2(4 * ADAMW_BLOCK_ELEMS // c_))))

    def body(core_ref, p_ref, l_ref, o_ref):
        o_ref[...] = (p_ref[...].astype(F32) + l_ref[...].astype(F32)).astype(o_ref.dtype)

    spec = pl.BlockSpec((None, tr, c_), lambda q, i, core: (q, i, 0))
    return pl.pallas_call(
        body, name=name,
        grid_spec=pltpu.PrefetchScalarGridSpec(
            num_scalar_prefetch=1, grid=(4, r // tr),
            in_specs=[pl.BlockSpec((None, tr, c_), lambda q, i, core: (2 * q + core[0], i, 0)), spec],
            out_specs=spec),
        out_shape=jax.ShapeDtypeStruct(land.shape, land.dtype),
        compiler_params=_params("parallel", "parallel"),
    )(lax.axis_index("c").reshape(1), pieces, land)


def _mm(a, b, *, name, nt=False, ta=False, out_dtype=F32, add=None, split_n=None, tm=512, tn=1024, deps=()):
    k, m = a.shape if ta else a.shape[::-1]
    n = b.shape[0] if nt else b.shape[1]
    tm = _tile(m, tm)
    tn = n // split_n if split_n else _tile(n, tn)
    dims = (((0 if ta else 1,), (1 if nt else 0,)), ((), ()))

    def body(*refs):
        a_ref, b_ref = refs[0], refs[1]
        o_ref = refs[-1]
        acc = lax.dot_general(a_ref[...], b_ref[...], dims, preferred_element_type=F32)
        if add is not None:
            acc = acc + refs[2][...]
        o_ref[...] = acc.astype(o_ref.dtype)

    in_specs = [pl.BlockSpec((k, tm), lambda i, j: (0, i)) if ta else pl.BlockSpec((tm, k), lambda i, j: (i, 0)),
                pl.BlockSpec((tn, k), lambda i, j: (j, 0)) if nt
                else pl.BlockSpec((k, tn), lambda i, j: (0, j))]
    args = [a, b]
    if add is not None:
        in_specs.append(pl.BlockSpec((tm, tn), lambda i, j: (i, j)))
        args.append(add)
    if split_n:
        out_shape = jax.ShapeDtypeStruct((split_n, m, tn), out_dtype)
        out_spec = pl.BlockSpec((None, tm, tn), lambda i, j: (j, i, 0))
    else:
        out_shape = jax.ShapeDtypeStruct((m, n), out_dtype)
        out_spec = pl.BlockSpec((tm, tn), lambda i, j: (i, j))
    return _pallas(
        body, args, deps, name=name, grid=(m // tm, n // tn), in_specs=in_specs, out_specs=out_spec,
        out_shape=out_shape, compiler_params=_params("parallel", "parallel"))


def _mm_win_fwd(h, w_g, name, deps=()):
    m, k = h.shape
    nj = w_g.shape[2]
    tm = _tile(m, 512)

    def body(a_ref, b_ref, o_ref):
        o_ref[...] = jnp.dot(a_ref[...], b_ref[...], preferred_element_type=F32).astype(o_ref.dtype)

    return _pallas(
        body, [h, w_g], deps, name=name, grid=(N_DEV, m // tm),
        in_specs=[pl.BlockSpec((tm, k), lambda j, i: (i, 0)),
                  pl.BlockSpec((None, k, nj), lambda j, i: (j, 0, 0))],
        out_specs=pl.BlockSpec((tm, nj), lambda j, i: (i, j)),
        out_shape=jax.ShapeDtypeStruct((m, N_DEV * nj), BF16),
        compiler_params=_params("parallel", "parallel"))


def _mm_win_bwd(dproj, w_g, name, deps=()):
    m = dproj.shape[0]
    d, nj = w_g.shape[1], w_g.shape[2]
    tm = _tile(m, 512)
    tn = _tile(d, 1024)

    def body(a_ref, b_ref, o_ref, acc_ref):
        j = pl.program_id(2)

        @pl.when(j == 0)
        def _():
            acc_ref[...] = jnp.zeros_like(acc_ref)

        acc_ref[...] += lax.dot_general(a_ref[...], b_ref[...], (((1,), (1,)), ((), ())),
                                        preferred_element_type=F32)

        @pl.when(j == N_DEV - 1)
        def _():
            o_ref[...] = acc_ref[...]

    return _pallas(
        body, [dproj, w_g], deps, name=name, grid=(m // tm, d // tn, N_DEV),
        in_specs=[pl.BlockSpec((tm, nj), lambda i, n, j: (i, j)),
                  pl.BlockSpec((None, tn, nj), lambda i, n, j: (j, n, 0))],
        out_specs=pl.BlockSpec((tm, tn), lambda i, n, j: (i, n)),
        out_shape=jax.ShapeDtypeStruct((m, d), F32),
        scratch_shapes=[pltpu.VMEM((tm, tn), F32)],
        compiler_params=_params("parallel", "parallel", "arbitrary"))


def _row_spec(tr, w, col):
    return pl.BlockSpec((tr, w), lambda i: (i, col))


def _full_spec(shape):
    return pl.BlockSpec(shape, lambda i: (0,) * len(shape))


def _rmsnorm_fwd(x, g, name, deps=()):
    l, d = x.shape
    tr = _tile(l, 256)

    def body(x_ref, g_ref, o_ref):
        xv = x_ref[...]
        rstd = lax.rsqrt(jnp.mean(xv * xv, axis=-1, keepdims=True) + RMS_EPS)
        o_ref[...] = (xv * rstd * g_ref[...]).astype(o_ref.dtype)

    return _pallas(
        body, [x, g.reshape(1, d)], deps, name=name, grid=(l // tr,),
        in_specs=[_row_spec(tr, d, 0), _full_spec((1, d))],
        out_specs=_row_spec(tr, d, 0),
        out_shape=jax.ShapeDtypeStruct((l, d), BF16),
        compiler_params=_params("parallel"))


def _rmsnorm_bwd(x, g, dh, dxo, name, deps=()):
    l, d = x.shape
    tr = _tile(l, 256)

    def body(x_ref, g_ref, dh_ref, dxo_ref, dx_ref, dg_ref):
        xv = x_ref[...]
        rstd = lax.rsqrt(jnp.mean(xv * xv, axis=-1, keepdims=True) + RMS_EPS)
        dhv = dh_ref[...]
        gdy = dhv * g_ref[...]
        dot = jnp.mean(gdy * xv, axis=-1, keepdims=True)
        dx_ref[...] = dxo_ref[...] + rstd * gdy - xv * (rstd * rstd * rstd * dot)

        @pl.when(pl.program_id(0) == 0)
        def _():
            dg_ref[...] = jnp.zeros_like(dg_ref)

        dg_ref[...] += jnp.sum(dhv * xv * rstd, axis=0, keepdims=True)

    return _pallas(
        body, [x, g.reshape(1, d), dh, dxo], deps, name=name, grid=(l // tr,),
        in_specs=[_row_spec(tr, d, 0), _full_spec((1, d)), _row_spec(tr, d, 0), _row_spec(tr, d, 0)],
        out_specs=[_row_spec(tr, d, 0), _full_spec((1, d))],
        out_shape=[jax.ShapeDtypeStruct((l, d), F32), jax.ShapeDtypeStruct((1, d), F32)],
        compiler_params=_params("arbitrary"))


def _final_loss(x, g, tgt, name):
    l, d = x.shape
    tr = _tile(l, 256)

    def body(x_ref, g_ref, t_ref, dx_ref, dg_ref, loss_ref):
        xv = x_ref[...]
        gv = g_ref[...]
        rstd = lax.rsqrt(jnp.mean(xv * xv, axis=-1, keepdims=True) + RMS_EPS)
        xn = xv * rstd
        err = xn * gv - t_ref[...]
        dy = err * (1.0 / d)
        gdy = dy * gv
        dot = jnp.mean(gdy * xv, axis=-1, keepdims=True)
        dx_ref[...] = rstd * gdy - xv * (rstd * rstd * rstd * dot)

        @pl.when(pl.program_id(0) == 0)
        def _():
            dg_ref[...] = jnp.zeros_like(dg_ref)
            loss_ref[...] = jnp.zeros_like(loss_ref)

        dg_ref[...] += jnp.sum(dy * xn, axis=0, keepdims=True)
        loss_ref[...] += (0.5 / d) * jnp.sum(err * err)

    return pl.pallas_call(
        body, name=name, grid=(l // tr,),
        in_specs=[_row_spec(tr, d, 0), _full_spec((1, d)), _row_spec(tr, d, 0)],
        out_specs=[_row_spec(tr, d, 0), _full_spec((1, d)), _full_spec((SUBLANES, LANES))],
        out_shape=[jax.ShapeDtypeStruct((l, d), F32), jax.ShapeDtypeStruct((1, d), F32),
                   jax.ShapeDtypeStruct((SUBLANES, LANES), F32)],
        compiler_params=_params("arbitrary"),
    )(x, g.reshape(1, d), tgt)


HALO = 2 * SUBLANES


def _halo_spec(tr, w, col, nblk, before):
    step = tr // HALO
    if before:
        return pl.BlockSpec((HALO, w), lambda i: (jnp.maximum(i * step - 1, 0), col))
    return pl.BlockSpec((HALO, w), lambda i: (jnp.minimum((i + 1) * step, nblk - 1), col))


def _shift_down(cur, before, k):
    ext = jnp.concatenate([before, cur], axis=0)
    return pltpu.roll(ext, k, axis=0)[HALO:, :]


def _shift_up(cur, after, k):
    tr = cur.shape[0]
    ext = jnp.concatenate([cur, after], axis=0)
    return pltpu.roll(ext, tr + HALO - k, axis=0)[:tr, :]


def _f32(ref):
    return ref[...].astype(F32)


def _branch_a_fwd(proj, conv_w, d, name, deps=()):
    l = proj.shape[0]
    tr = _tile(l, 256)
    nblk8 = l // HALO

    def body(v_ref, bg_ref, cg_ref, za_ref, vh_ref, cgh_ref, w_ref, o_ref):
        first = pl.program_id(0) == 0
        cv = _f32(cg_ref) * _f32(v_ref)
        cvh = jnp.where(first, 0.0, _f32(cgh_ref) * _f32(vh_ref))
        w0, w1, w2 = w_ref[0:1, :], w_ref[1:2, :], w_ref[2:3, :]
        q = w2 * cv + w1 * _shift_down(cv, cvh, 1) + w0 * _shift_down(cv, cvh, 2)
        za = _f32(za_ref)
        o_ref[...] = (_f32(bg_ref) * q * (za * _sigmoid(za))).astype(o_ref.dtype)

    return _pallas(
        body, [proj, proj, proj, proj, proj, proj, conv_w], deps, name=name, grid=(l // tr,),
        in_specs=[_row_spec(tr, d, 0), _row_spec(tr, d, 1), _row_spec(tr, d, 2), _row_spec(tr, d, 3),
                  _halo_spec(tr, d, 0, nblk8, True), _halo_spec(tr, d, 2, nblk8, True),
                  _full_spec((SUBLANES, d))],
        out_specs=_row_spec(tr, d, 0),
        out_shape=jax.ShapeDtypeStruct((l, d), BF16),
        compiler_params=_params("parallel"))


def _branch_a_bwd(proj, dpa, conv_w, d, name):
    l = proj.shape[0]
    tr = _tile(l, 128)
    nblk8 = l // HALO
    ntiles = l // tr

    def body(v_ref, bg_ref, cg_ref, za_ref, dpa_ref, vh_ref, cgh_ref, bgn_ref, zan_ref, dpan_ref,
             w_ref, dv_ref, dbg_ref, dcg_ref, dza_ref, dw0_ref, dw1_ref, dw2_ref):
        i = pl.program_id(0)
        v, bg, cg, za, dpa_v = _f32(v_ref), _f32(bg_ref), _f32(cg_ref), _f32(za_ref), dpa_ref[...]
        w0, w1, w2 = w_ref[0:1, :], w_ref[1:2, :], w_ref[2:3, :]
        cv = cg * v
        cvh = jnp.where(i == 0, 0.0, _f32(cgh_ref) * _f32(vh_ref))
        cv1 = _shift_down(cv, cvh, 1)
        cv2 = _shift_down(cv, cvh, 2)
        q = w2 * cv + w1 * cv1 + w0 * cv2
        sg = _sigmoid(za)
        s = za * sg
        dbg_ref[...] = (dpa_v * q * s).astype(dbg_ref.dtype)
        dza_ref[...] = (dpa_v * bg * q * (sg * (1.0 + za * (1.0 - sg)))).astype(dza_ref.dtype)
        dq = dpa_v * bg * s
        zan = _f32(zan_ref)
        dqn = jnp.where(i == ntiles - 1, 0.0, dpan_ref[...] * _f32(bgn_ref) * (zan * _sigmoid(zan)))
        dcv = w2 * dq + w1 * _shift_up(dq, dqn, 1) + w0 * _shift_up(dq, dqn, 2)
        dcg_ref[...] = (dcv * v).astype(dcg_ref.dtype)
        dv_ref[...] = (dcv * cg).astype(dv_ref.dtype)

        @pl.when(i == 0)
        def _():
            dw0_ref[...] = jnp.zeros_like(dw0_ref)
            dw1_ref[...] = jnp.zeros_like(dw1_ref)
            dw2_ref[...] = jnp.zeros_like(dw2_ref)

        dw0_ref[...] += jnp.sum(dq * cv2, axis=0, keepdims=True)
        dw1_ref[...] += jnp.sum(dq * cv1, axis=0, keepdims=True)
        dw2_ref[...] += jnp.sum(dq * cv, axis=0, keepdims=True)

    act = jax.ShapeDtypeStruct((l, d), BF16)
    wsum = jax.ShapeDtypeStruct((1, d), F32)
    return pl.pallas_call(
        body, name=name, grid=(ntiles,),
        in_specs=[_row_spec(tr, d, 0), _row_spec(tr, d, 1), _row_spec(tr, d, 2), _row_spec(tr, d, 3),
                  _row_spec(tr, d, 0),
                  _halo_spec(tr, d, 0, nblk8, True), _halo_spec(tr, d, 2, nblk8, True),
                  _halo_spec(tr, d, 1, nblk8, False), _halo_spec(tr, d, 3, nblk8, False),
                  _halo_spec(tr, d, 0, nblk8, False),
                  _full_spec((SUBLANES, d))],
        out_specs=[_row_spec(tr, d, 0)] * 4 + [_full_spec((1, d))] * 3,
        out_shape=[act] * 4 + [wsum] * 3,
        compiler_params=_params("arbitrary"),
    )(proj, proj, proj, proj, dpa, proj, proj, proj, proj, dpa, conv_w)


def _gelu_cast(y, name):
    l, w = y.shape
    tr = _tile(l, 512)

    def body(y_ref, o_ref):
        o_ref[...] = _gelu(y_ref[...]).astype(o_ref.dtype)

    return pl.pallas_call(
        body, name=name, grid=(l // tr,), in_specs=[_row_spec(tr, w, 0)],
        out_specs=_row_spec(tr, w, 0), out_shape=jax.ShapeDtypeStruct((l, w), BF16),
        compiler_params=_params("parallel"),
    )(y)


def _glu_post(y, gl, proj, b_glu, zb_col, name):
    l, w = y.shape
    tr = _tile(l, 512)

    def body(y_ref, gl_ref, zb_ref, b_ref, o_ref):
        zb = _f32(zb_ref)
        o_ref[...] = (_gelu(y_ref[...]) * _sigmoid(gl_ref[...] + b_ref[...])
                      * (zb * _sigmoid(zb))).astype(o_ref.dtype)

    return pl.pallas_call(
        body, name=name, grid=(l // tr,),
        in_specs=[_row_spec(tr, w, 0), _row_spec(tr, w, 0), _row_spec(tr, w, zb_col), _full_spec((1, w))],
        out_specs=_row_spec(tr, w, 0), out_shape=jax.ShapeDtypeStruct((l, w), BF16),
        compiler_params=_params("parallel"),
    )(y, gl, proj, b_glu.reshape(1, w))


def _glu_bwd1(y, gl, proj, b_glu, dpb, zb_col, name):
    l, w = y.shape
    tr = _tile(l, 512)

    def body(y_ref, gl_ref, zb_ref, b_ref, dpb_ref, dzb_ref, dgl_ref, t_ref, db_ref):
        zb = _f32(zb_ref)
        dpb_v = dpb_ref[...]
        yg = _gelu(y_ref[...])
        sgl = _sigmoid(gl_ref[...] + b_ref[...])
        szb = _sigmoid(zb)
        dzb_ref[...] = (dpb_v * yg * sgl * (szb * (1.0 + zb * (1.0 - szb)))).astype(dzb_ref.dtype)
        e = dpb_v * (zb * szb)
        dgl = e * yg * sgl * (1.0 - sgl)
        dgl_ref[...] = dgl.astype(dgl_ref.dtype)
        t_ref[...] = e * sgl

        @pl.when(pl.program_id(0) == 0)
        def _():
            db_ref[...] = jnp.zeros_like(db_ref)

        db_ref[...] += jnp.sum(dgl, axis=0, keepdims=True)

    return pl.pallas_call(
        body, name=name, grid=(l // tr,),
        in_specs=[_row_spec(tr, w, 0), _row_spec(tr, w, 0), _row_spec(tr, w, zb_col), _full_spec((1, w)),
                  _row_spec(tr, w, 0)],
        out_specs=[_row_spec(tr, w, 0)] * 3 + [_full_spec((1, w))],
        out_shape=[jax.ShapeDtypeStruct((l, w), BF16), jax.ShapeDtypeStruct((l, w), BF16),
                   jax.ShapeDtypeStruct((l, w), F32), jax.ShapeDtypeStruct((1, w), F32)],
        compiler_params=_params("arbitrary"),
    )(y, gl, proj, b_glu.reshape(1, w), dpb)


def _glu_bwd2(y, t1, dyg2, name, deps=()):
    l, w = y.shape
    tr = _tile(l, 512)

    def body(y_ref, t_ref, d_ref, o_ref):
        o_ref[...] = (t_ref[...] + d_ref[...]) * _gelu_grad(y_ref[...])

    return _pallas(
        body, [y, t1, dyg2], deps, name=name, grid=(l // tr,), in_specs=[_row_spec(tr, w, 0)] * 3,
        out_specs=_row_spec(tr, w, 0), out_shape=jax.ShapeDtypeStruct((l, w), F32),
        compiler_params=_params("parallel"))


def _merge_fwd(proj, ya, yb, d, ga_col, gb_col, name):
    l = proj.shape[0]
    tr = _tile(l, 256)

    def body(ga_ref, gb_ref, ya_ref, yb_ref, o_ref):
        o_ref[...] = (_sigmoid(_f32(ga_ref)) * ya_ref[...]
                      + _sigmoid(_f32(gb_ref)) * yb_ref[...]).astype(o_ref.dtype)

    return pl.pallas_call(
        body, name=name, grid=(l // tr,),
        in_specs=[_row_spec(tr, d, ga_col), _row_spec(tr, d, gb_col), _row_spec(tr, d, 0), _row_spec(tr, d, 0)],
        out_specs=_row_spec(tr, d, 0), out_shape=jax.ShapeDtypeStruct((l, d), BF16),
        compiler_params=_params("parallel"),
    )(proj, proj, ya, yb)


def _merge_bwd(proj, ya, yb, dm, d, ga_col, gb_col, name):
    l = proj.shape[0]
    tr = _tile(l, 256)

    def body(ga_ref, gb_ref, ya_ref, yb_ref, dm_ref, dya_ref, dyb_ref, dga_ref, dgb_ref):
        dmv = dm_ref[...]
        sa = _sigmoid(_f32(ga_ref))
        sb = _sigmoid(_f32(gb_ref))
        dya_ref[...] = (dmv * sa).astype(dya_ref.dtype)
        dyb_ref[...] = (dmv * sb).astype(dyb_ref.dtype)
        dga_ref[...] = (dmv * ya_ref[...] * sa * (1.0 - sa)).astype(dga_ref.dtype)
        dgb_ref[...] = (dmv * yb_ref[...] * sb * (1.0 - sb)).astype(dgb_ref.dtype)

    act = jax.ShapeDtypeStruct((l, d), BF16)
    return pl.pallas_call(
        body, name=name, grid=(l // tr,),
        in_specs=[_row_spec(tr, d, ga_col), _row_spec(tr, d, gb_col), _row_spec(tr, d, 0), _row_spec(tr, d, 0),
                  _row_spec(tr, d, 0)],
        out_specs=[_row_spec(tr, d, 0)] * 4, out_shape=[act] * 4,
        compiler_params=_params("parallel"),
    )(proj, proj, ya, yb, dm)


def _to_segments(a):
    l, w = a.shape
    return a.reshape(SUBLANES, l // SUBLANES, w).transpose(1, 0, 2).reshape(l, w)


def _from_segments(a):
    l, w = a.shape
    return a.reshape(l // SUBLANES, SUBLANES, w).transpose(1, 0, 2).reshape(l, w)


def _dense(z, shape):
    return jnp.broadcast_to(z, shape).reshape(-1, LANES)


def _s5_disc(are, aim, ldt):
    dt = jnp.exp(ldt)
    er = jnp.exp(are * dt)
    lbr = er * jnp.cos(aim * dt)
    lbi = er * jnp.sin(aim * dt)
    inv = 1.0 / (are * are + aim * aim)
    fr = ((lbr - 1.0) * are + lbi * aim) * inv
    fi = (lbi * are - (lbr - 1.0) * aim) * inv
    return dt, lbr, lbi, inv, fr, fi


def _s5_params(are, aim, ldt, bre, bim, name, deps=()):
    shape = are.shape

    def body(are_ref, aim_ref, ldt_ref, bre_ref, bim_ref, lbr_ref, lbi_ref, bbr_ref, bbi_ref):
        _, lbr, lbi, _, fr, fi = _s5_disc(are_ref[...], aim_ref[...], ldt_ref[...])
        lbr_ref[...] = lbr
        lbi_ref[...] = lbi
        bbr_ref[...] = fr * bre_ref[...] - fi * bim_ref[...]
        bbi_ref[...] = fr * bim_ref[...] + fi * bre_ref[...]

    out = jax.ShapeDtypeStruct(shape, F32)
    return _pallas(body, [are, aim, ldt, bre, bim], deps, name=name,
                   in_specs=[pl.BlockSpec(memory_space=pltpu.VMEM)] * 5, out_shape=[out] * 4,
                   compiler_params=pltpu.CompilerParams(vmem_limit_bytes=VMEM_LIMIT_BYTES))


def _s5_params_bwd(are, aim, ldt, bre, bim, glbr, glbi, gbbr, gbbi, n_groups, name, deps=()):
    shape = are.shape
    rows_per_group = shape[0] // n_groups

    def body(are_ref, aim_ref, ldt_ref, bre_ref, bim_ref, glbr_ref, glbi_ref, gbbr_ref, gbbi_ref,
             gar_ref, gai_ref, gdt_ref, gbr_ref, gbi_ref):
        are_v, aim_v = are_ref[...], aim_ref[...]
        bre_v, bim_v = bre_ref[...], bim_ref[...]
        gbbr_v, gbbi_v = gbbr_ref[...], gbbi_ref[...]
        dt, lbr, lbi, inv, fr, fi = _s5_disc(are_v, aim_v, ldt_ref[...])
        gbr_ref[...] = fr * gbbr_v + fi * gbbi_v
        gbi_ref[...] = fr * gbbi_v - fi * gbbr_v
        lane_group = lax.broadcasted_iota(jnp.int32, (LANES, LANES), 0) // S5_GROUP
        same_group = (lane_group == lax.broadcasted_iota(jnp.int32, (LANES, LANES), 1) // S5_GROUP)
        ones = same_group.astype(F32)
        gfr = jnp.dot(bre_v * gbbr_v + bim_v * gbbi_v, ones, precision=HIGHEST, preferred_element_type=F32)
        gfi = jnp.dot(bre_v * gbbi_v - bim_v * gbbr_v, ones, precision=HIGHEST, preferred_element_type=F32)
        glr = glbr_ref[...] + (are_v * gfr - aim_v * gfi) * inv
        gli = glbi_ref[...] + (are_v * gfi + aim_v * gfr) * inv
        qr = (fr * are_v + fi * aim_v) * inv
        qi = (fi * are_v - fr * aim_v) * inv
        gzr = lbr * glr + lbi * gli
        gzi = lbr * gli - lbi * glr
        gar_ref[...] = dt * gzr - (qr * gfr + qi * gfi)
        gai_ref[...] = dt * gzi - (qr * gfi - qi * gfr)
        e = dt * (are_v * gzr + aim_v * gzi)
        per_group = jnp.sum(e.reshape(n_groups, rows_per_group, LANES), axis=1)
        total = jnp.sum(per_group, axis=1, keepdims=True) * (1.0 / S5_GROUP)
        gdt_ref[...] = jnp.broadcast_to(total, gdt_ref.shape)

    out = jax.ShapeDtypeStruct(shape, F32)
    return _pallas(
        body, [are, aim, ldt, bre, bim, glbr, glbi, gbbr, gbbi], deps, name=name,
        in_specs=[pl.BlockSpec(memory_space=pltpu.VMEM)] * 9,
        out_shape=[out, out, jax.ShapeDtypeStruct((n_groups, LANES), F32), out, out],
        compiler_params=pltpu.CompilerParams(vmem_limit_bytes=VMEM_LIMIT_BYTES))


def _cmul(ar, ai, br, bi):
    return ar * br - ai * bi, ar * bi + ai * br


def _scan_in_place(hr_ref, hi_ref, lr, li, reverse):
    l, wb = hr_ref.shape
    nt = l // SUBLANES
    shift = SUBLANES - 1 if reverse else 1
    unroll = 8 if nt % 8 == 0 else 1

    def rows(k):
        t = (nt - 1 - k) if reverse else k
        return pl.ds(pl.multiple_of(t * SUBLANES, SUBLANES), SUBLANES)

    zero = jnp.zeros((SUBLANES, wb), F32)
    one = jnp.ones((SUBLANES, wb), F32)

    def local_step(k, carry):
        hr, hi, pr, pi = carry
        r = rows(k)
        tr_, ti_ = _cmul(lr, li, hr, hi)
        hr, hi = tr_ + hr_ref[r, :], ti_ + hi_ref[r, :]
        hr_ref[r, :] = hr
        hi_ref[r, :] = hi
        pr, pi = _cmul(lr, li, pr, pi)
        return hr, hi, pr, pi

    er, ei, lnr, lni = lax.fori_loop(0, nt, local_step, (zero, zero, one, zero), unroll=unroll)

    row = lax.broadcasted_iota(jnp.int32, (SUBLANES, wb), 0)
    tr_, ti_ = er, ei
    for j in range(1, SUBLANES):
        pr_, pi_ = _cmul(lnr, lni, pltpu.roll(tr_, shift, axis=0), pltpu.roll(ti_, shift, axis=0))
        at = row == ((SUBLANES - 1 - j) if reverse else j)
        tr_ = jnp.where(at, er + pr_, tr_)
        ti_ = jnp.where(at, ei + pi_, ti_)
    edge = row == ((SUBLANES - 1) if reverse else 0)
    cr = jnp.where(edge, 0.0, pltpu.roll(tr_, shift, axis=0))
    ci = jnp.where(edge, 0.0, pltpu.roll(ti_, shift, axis=0))

    def fix_step(k, carry):
        pr, pi = carry
        pr, pi = _cmul(lr, li, pr, pi)
        r = rows(k)
        ar_, ai_ = _cmul(pr, pi, cr, ci)
        hr_ref[r, :] = hr_ref[r, :] + ar_
        hi_ref[r, :] = hi_ref[r, :] + ai_
        return pr, pi

    lax.fori_loop(0, nt, fix_step, (one, zero), unroll=unroll)


def _dot(a, b):
    return jnp.dot(a.astype(BF16), b.astype(BF16), preferred_element_type=F32)


def _s5_forward(u_seg, mb_re, mb_im, mc_re, mc_im, lam_re, lam_im, dvec, name):
    l = u_seg.shape[0]
    nb, kin, kst = mb_re.shape

    def body(u_ref, mbr_ref, mbi_ref, mcr_ref, mci_ref, lr_ref, li_ref, d_ref, hr_ref, hi_ref, y_ref):
        u = u_ref[...]
        hr_ref[...] = _dot(u, mbr_ref[...])
        hi_ref[...] = _dot(u, mbi_ref[...])
        _scan_in_place(hr_ref, hi_ref, jnp.broadcast_to(lr_ref[...], (SUBLANES, kst)),
                       jnp.broadcast_to(li_ref[...], (SUBLANES, kst)), False)
        y_ref[...] = (_dot(hr_ref[...], mcr_ref[...]) - _dot(hi_ref[...], mci_ref[...])
                      + d_ref[...] * u.astype(F32))

    act = pl.BlockSpec((l, kin), lambda b: (0, b))
    state = pl.BlockSpec((l, kst), lambda b: (0, b))
    up = pl.BlockSpec((None, kin, kst), lambda b: (b, 0, 0))
    down = pl.BlockSpec((None, kst, kin), lambda b: (b, 0, 0))
    hshape = jax.ShapeDtypeStruct((l, nb * kst), F32)
    return pl.pallas_call(
        body, name=name, grid=(nb,),
        in_specs=[act, up, up, down, down, pl.BlockSpec((1, kst), lambda b: (0, b)),
                  pl.BlockSpec((1, kst), lambda b: (0, b)), pl.BlockSpec((1, kin), lambda b: (0, b))],
        out_specs=[state, state, act],
        out_shape=[hshape, hshape, jax.ShapeDtypeStruct((l, nb * kin), F32)],
        compiler_params=_params("parallel"),
    )(u_seg, mb_re, mb_im, mc_re, mc_im, lam_re, lam_im, dvec)


def _dot_ta(a, b):
    return lax.dot_general(a.astype(BF16), b.astype(BF16), (((0,), (0,)), ((), ())), preferred_element_type=F32)


def _s5_backward(dy_seg, u_seg, h_re, h_im, mg_re, mg_im, md_re, md_im, lam_re, lam_im_neg, dvec, name):
    l = dy_seg.shape[0]
    nb, kin, kst = mg_re.shape
    nt = l // SUBLANES

    def body(dy_ref, u_ref, hr_ref, hi_ref, mgr_ref, mgi_ref, mdr_ref, mdi_ref, lr_ref, li_ref,
             d_ref, du_ref, gcr_ref, gci_ref, gbr_ref, gbi_ref, glr_ref, gli_ref, dsk_ref, qr_ref, qi_ref):
        dy = dy_ref[...]
        qr_ref[...] = _dot(dy, mgr_ref[...])
        qi_ref[...] = _dot(dy, mgi_ref[...])
        _scan_in_place(qr_ref, qi_ref, jnp.broadcast_to(lr_ref[...], (SUBLANES, kst)),
                       jnp.broadcast_to(li_ref[...], (SUBLANES, kst)), True)
        du_ref[...] = _dot(qr_ref[...], mdr_ref[...]) - _dot(qi_ref[...], mdi_ref[...]) + d_ref[...] * dy
        dsk_ref[...] = jnp.sum(dy * _f32(u_ref), axis=0, keepdims=True)
        gcr_ref[...] = _dot_ta(dy, hr_ref[...])
        gci_ref[...] = _dot_ta(dy, hi_ref[...])
        gbr_ref[...] = _dot_ta(u_ref[...], qr_ref[...])
        gbi_ref[...] = _dot_ta(u_ref[...], qi_ref[...])

        row = lax.broadcasted_iota(jnp.int32, (SUBLANES, kst), 0)
        last = pl.ds((nt - 1) * SUBLANES, SUBLANES)
        first = pl.ds(0, SUBLANES)
        pr = jnp.where(row == 0, 0.0, pltpu.roll(hr_ref[last, :], 1, axis=0))
        pi = jnp.where(row == 0, 0.0, pltpu.roll(hi_ref[last, :], 1, axis=0))
        gr, gi = qr_ref[first, :], qi_ref[first, :]

        def step(t, carry):
            acc_r, acc_i = carry
            cur = pl.ds(pl.multiple_of(t * SUBLANES, SUBLANES), SUBLANES)
            prev = pl.ds(pl.multiple_of((t - 1) * SUBLANES, SUBLANES), SUBLANES)
            gr, gi = qr_ref[cur, :], qi_ref[cur, :]
            pr, pi = hr_ref[prev, :], hi_ref[prev, :]
            return acc_r + gr * pr + gi * pi, acc_i + gi * pr - gr * pi

        acc_r, acc_i = lax.fori_loop(1, nt, step, (gr * pr + gi * pi, gi * pr - gr * pi))
        glr_ref[...] = jnp.sum(acc_r, axis=0, keepdims=True)
        gli_ref[...] = jnp.sum(acc_i, axis=0, keepdims=True)

    act = pl.BlockSpec((l, kin), lambda b: (0, b))
    state = pl.BlockSpec((l, kst), lambda b: (0, b))
    up = pl.BlockSpec((None, kin, kst), lambda b: (b, 0, 0))
    down = pl.BlockSpec((None, kst, kin), lambda b: (b, 0, 0))
    vec_st = pl.BlockSpec((1, kst), lambda b: (0, b))
    vec_in = pl.BlockSpec((1, kin), lambda b: (0, b))
    outer = jax.ShapeDtypeStruct((nb, kin, kst), F32)
    lam_shape = jax.ShapeDtypeStruct((1, nb * kst), F32)
    return pl.pallas_call(
        body, name=name, grid=(nb,),
        in_specs=[act, act, state, state, up, up, down, down, vec_st, vec_st, vec_in],
        out_specs=[act, up, up, up, up, vec_st, vec_st, vec_in],
        out_shape=[jax.ShapeDtypeStruct((l, nb * kin), F32), outer, outer, outer, outer, lam_shape, lam_shape,
                   jax.ShapeDtypeStruct((1, nb * kin), F32)],
        scratch_shapes=[pltpu.VMEM((l, kst), F32), pltpu.VMEM((l, kst), F32)],
        compiler_params=_params("parallel"),
    )(dy_seg, u_seg, h_re, h_im, mg_re, mg_im, md_re, md_im, lam_re, lam_im_neg, dvec)


def _block_diag(m, nb):
    g, r, s = m.shape
    gb = g // nb
    eye = jnp.eye(gb, dtype=m.dtype)
    out = m.reshape(nb, gb, r, 1, s) * eye[None, :, None, :, None]
    return out.reshape(nb, gb * r, gb * s)


def _block_diag_extract(mat, g, r, s):
    nb = mat.shape[0]
    gb = g // nb
    eye = jnp.eye(gb, dtype=mat.dtype)
    m5 = mat.reshape(nb, gb, r, gb, s) * eye[None, :, None, :, None]
    return jnp.sum(m5, axis=3).reshape(g, r, s)


def _adamw(w, m, v, gslots, name, layer=0, prev=None, deps=()):
    layers, r, c = w.shape
    s = gslots.shape[0]
    tr = _tile(r, max(SUBLANES, 1 << int(math.log2(ADAMW_BLOCK_ELEMS // c))))
    bc1 = 1.0 / (1.0 - ADAM_B1 ** ADAM_STEP)
    bc2 = 1.0 / (1.0 - ADAM_B2 ** ADAM_STEP)

    def body(w_ref, m_ref, v_ref, g_ref, *rest):
        go_ref, d_ref, mo_ref, vo_ref = rest[-4:]
        g = g_ref[0].astype(F32)
        for k in range(1, s):
            g = g + g_ref[k].astype(F32)
        mn = ADAM_B1 * m_ref[...] + (1.0 - ADAM_B1) * g
        vn = ADAM_B2 * v_ref[...] + (1.0 - ADAM_B2) * (g * g)
        go_ref[...] = g
        mo_ref[...] = mn
        vo_ref[...] = vn
        d_ref[...] = -ADAM_LR * ((mn * bc1) / (jnp.sqrt(vn * bc2) + ADAM_EPS) + ADAM_WD * w_ref[...])

    spec = pl.BlockSpec((None, tr, c), lambda i: (layer, i, 0))
    out = jax.ShapeDtypeStruct((layers, r, c), F32)
    in_specs = [spec, spec, spec, pl.BlockSpec((s, tr, c), lambda i: (0, i, 0))]
    args = [w, m, v, gslots]
    aliases = {}
    if prev is not None:
        in_specs += [ANY_SPEC] * 4
        args += list(prev)
        aliases = {4 + q: q for q in range(4)}
    in_specs += [ANY_SPEC] * len(deps)
    args += list(deps)
    return pl.pallas_call(
        body, name=name, grid=(r // tr,), in_specs=in_specs,
        out_specs=[spec] * 4, out_shape=[out] * 4, input_output_aliases=aliases,
        compiler_params=_params("parallel"),
    )(*args)


def _pack(parts):
    flat = jnp.concatenate([p.reshape(-1) for p in parts])
    pad = (-flat.shape[0]) % (PACK_ROWS * LANES)
    return jnp.pad(flat, (0, pad)).reshape(-1, LANES)


def _unpack(packed, shapes):
    flat = packed.reshape(-1)
    out, off = [], 0
    for shp in shapes:
        size = math.prod(shp)
        out.append(flat[off:off + size].reshape(shp))
        off += size
    return out


def kernel(x, norm_g, w_in, conv_w, w_out_a, a_re, a_im, log_dt, b_re, b_im, c_re, c_im, d_skip, w_glu, b_glu, w_out_b, w_o, final_g, loss_target, m_norm_g, m_w_in, m_conv_w, m_w_out_a, m_a_re, m_a_im, m_log_dt, m_b_re, m_b_im, m_c_re, m_c_im, m_d_skip, m_w_glu, m_b_glu, m_w_out_b, m_w_o, m_final_g, v_norm_g, v_w_in, v_conv_w, v_w_out_a, v_a_re, v_a_im, v_log_dt, v_b_re, v_b_im, v_c_re, v_c_im, v_d_skip, v_w_glu, v_b_glu, v_w_out_b, v_w_o, v_final_g):
    depth = norm_g.shape[0]
    l, d = x.shape[1], x.shape[2]
    ws = w_glu.shape[2]
    n_groups, n_state = a_re.shape[1], a_re.shape[2]
    nb = ws // LANES
    assert S5_GROUP == b_re.shape[3] and n_state * S5_GB == 4 * LANES
    u_col, zb_col = 4 * d // ws, 4 * d // ws + 1
    ga_col, gb_col = (4 * d + 2 * ws) // d, (4 * d + 2 * ws) // d + 1
    me = 4 * lax.axis_index("x") + 2 * lax.axis_index("y") + lax.axis_index("c")

    xs = [x[0]]
    tgt = loss_target[0]

    big_names = ("w_in", "w_out_a", "w_glu", "w_out_b", "w_o")
    big = dict(w_in=(w_in, m_w_in, v_w_in), w_out_a=(w_out_a, m_w_out_a, v_w_out_a),
               w_glu=(w_glu, m_w_glu, v_w_glu), w_out_b=(w_out_b, m_w_out_b, v_w_out_b),
               w_o=(w_o, m_w_o, v_w_o))

    def shards_bf16(i):
        return [big[k][0][i].astype(BF16) for k in big_names]

    def gather_start(shards, name, deps=()):
        sems, srcs, lands, token = _exchange_start(
            _plan_gather_chips, shards, [_landing(s_, N_DEV, me) for s_ in shards], f"{name}_start", deps)
        return (name, sems, srcs, lands), token

    def gather_forward(state, after, deps=()):
        name, sems, srcs, lands = state
        _, lands = _exchange_wait(_plan_gather_chips, sems, srcs, lands, after, f"{name}_wait")
        sems, _, lands, token = _exchange_start(_plan_gather_forward, None, lands, f"{name}_forward_start", deps)
        return (name, sems, lands), token

    def gather_finish(state, after):
        name, sems, lands = state
        return _exchange_wait(_plan_gather_forward, sems, None, lands, after, f"{name}_forward_wait")[1]

    conv_shard = jnp.pad(conv_w.reshape(depth * 3, -1), ((0, SUBLANES - depth * 3), (0, 0)))
    w_in_state, token = gather_start([shards_bf16(0)[0], conv_shard], "ag_w_in_0")
    s5 = []
    shape3 = (n_groups, n_state, S5_GROUP)
    for i in range(depth):
        dense_in = (_dense(a_re[i][:, :, None], shape3), _dense(a_im[i][:, :, None], shape3),
                    _dense(log_dt[i][:, None, None], shape3), b_re[i].reshape(-1, LANES), b_im[i].reshape(-1, LANES))
        lbr, lbi, bbr, bbi = _s5_params(*dense_in, f"s5_params_{i}", deps=(token,))
        bbr3, bbi3 = bbr.reshape(shape3), bbi.reshape(shape3)
        diag = lambda m: _block_diag(m, nb).astype(BF16)
        s5.append(dict(
            dense_in=dense_in,
            lam_re=lbr.reshape(shape3)[:, :, 0].reshape(1, -1), lam_im=lbi.reshape(shape3)[:, :, 0].reshape(1, -1),
            up=(diag(bbr3.transpose(0, 2, 1)), diag(bbi3.transpose(0, 2, 1))),
            down=(diag(c_re[i].transpose(0, 2, 1)), diag(c_im[i].transpose(0, 2, 1))),
            up_bwd=(diag(c_re[i]), diag(-c_im[i])), down_bwd=(diag(bbr3), diag(-bbi3))))
    prologue = [m for p in s5 for k in ("up", "down", "up_bwd", "down_bwd") for m in p[k]]
    prologue += [p[k] for p in s5 for k in ("lam_re", "lam_im")]
    w_in_state, token = gather_forward(w_in_state, prologue)
    rest_state, token = gather_start(shards_bf16(0)[1:], "ag_rest_0", deps=(token,))
    next_state = None
    if depth > 1:
        next_state, token = gather_start([shards_bf16(1)[0]], "ag_w_in_1", deps=(token,))

    saved = []
    wg = [None] * depth
    conv_full = None
    for i in range(depth):
        xi = xs[-1]
        h = _rmsnorm_fwd(xi, norm_g[i], f"rmsnorm_fwd_{i}", deps=(token,))
        arrived = gather_finish(w_in_state, [h])
        if i == 0:
            conv_full = arrived[1].transpose(1, 0, 2).reshape(SUBLANES, d)[:depth * 3].reshape(depth, 3, d)
        conv8 = jnp.pad(conv_full[i], ((0, SUBLANES - 3), (0, 0)))
        proj = _mm_win_fwd(h, arrived[0], f"mm_proj_{i}")
        u_seg = _to_segments(proj[:, 4 * d:4 * d + ws])
        h_re, h_im, y_seg = _s5_forward(u_seg, *s5[i]["up"], *s5[i]["down"], s5[i]["lam_re"], s5[i]["lam_im"],
                                        d_skip[i].reshape(1, ws), f"s5_forward_{i}")
        rest_state, token = gather_forward(rest_state, [y_seg])
        pa = _branch_a_fwd(proj, conv8, d, f"branch_a_fwd_{i}", deps=(token,))
        rest = gather_finish(rest_state, [pa])
        wg[i] = g = dict(w_in=arrived[0], w_a=rest[0].reshape(d, d), w_glu=rest[1].reshape(ws, ws),
                         w_b=rest[2], w_o=rest[3].reshape(d, d))
        ya = _mm(pa, g["w_a"], name=f"mm_ya_{i}")
        y = _from_segments(y_seg)
        yg = _gelu_cast(y, f"gelu_{i}")
        gl = _mm(yg, g["w_glu"], name=f"mm_glu_{i}")
        pb = _glu_post(y, gl, proj, b_glu[i], zb_col, f"glu_post_{i}")
        w_b2d = g["w_b"].transpose(1, 0, 2).reshape(ws, d)
        yb = _mm(pb, w_b2d, name=f"mm_yb_{i}")
        mrg = _merge_fwd(proj, ya, yb, d, ga_col, gb_col, f"merge_fwd_{i}")
        deps = ()
        if i + 1 < depth:
            w_in_state, token = gather_forward(next_state, [mrg])
            rest_state, token = gather_start(shards_bf16(i + 1)[1:], f"ag_rest_{i + 1}", deps=(token,))
            if i + 2 < depth:
                next_state, token = gather_start([shards_bf16(i + 2)[0]], f"ag_w_in_{i + 2}", deps=(token,))
            deps = (token,)
        xs.append(_mm(mrg, g["w_o"], name=f"mm_out_{i}", add=xi, deps=deps))
        saved.append(dict(h=h, proj=proj, pa=pa, ya=ya, yb=yb, y=y, yg=yg, gl=gl, pb=pb, mrg=mrg,
                          u_seg=u_seg, h_re=h_re, h_im=h_im, conv8=conv8, w_b2d=w_b2d))

    dx, g_final, loss_part = _final_loss(xs[-1], final_g, tgt, "final_loss")

    main_names = ("a_re", "a_im", "log_dt", "b_re", "b_im", "c_re", "c_im", "d_skip", "b_glu", "conv_w")
    rs_pending = []
    small = {k: [None] * depth for k in ("norm_g", "a_re", "a_im", "log_dt", "b_re", "b_im", "c_re", "c_im",
                                         "d_skip", "b_glu", "conv_w")}

    my_chip = 2 * lax.axis_index("x") + lax.axis_index("y")

    def reduce_on_chip(pieces, tag):
        lands = [lax.empty((4,) + p.shape[1:], p.dtype) for p in pieces]
        sems, srcs, lands, token = _exchange_start(_plan_reduce_sibling, pieces, lands, f"rs_sibling_start_{tag}")
        return (sems, srcs, lands), token

    def reduce_across_chips(names_, state, layer, tag, after):
        sems, srcs, lands = state
        srcs, lands = _exchange_wait(_plan_reduce_sibling, sems, srcs, lands, after, f"rs_sibling_wait_{tag}")
        sums = [_chip_sums(p, l_, f"chip_sum_{k}_{layer}") for k, p, l_ in zip(names_, srcs, lands)]
        lands = [_landing(lax.dynamic_index_in_dim(s_, my_chip, 0, keepdims=False), 4, my_chip) for s_ in sums]
        sems, srcs, lands, token = _exchange_start(_plan_reduce_chips, sums, lands, f"rs_chips_start_{tag}")
        rs_pending.append((names_, layer, sems, srcs, lands, f"rs_chips_wait_{tag}"))
        return token

    for i in reversed(range(depth)):
        s, g = saved[i], wg[i]
        proj = s["proj"]
        dxo_b = dx.astype(BF16)
        dm = _mm(dxo_b, g["w_o"], name=f"mm_dm_{i}", nt=True)
        gw_o = _mm(s["mrg"], dxo_b, ta=True, name=f"mm_gw_o_{i}", out_dtype=BF16)
        dya, dyb, dga, dgb = _merge_bwd(proj, s["ya"], s["yb"], dm, d, ga_col, gb_col, f"merge_bwd_{i}")
        dpa = _mm(dya, g["w_a"], name=f"mm_dpa_{i}", nt=True)
        gw_a = _mm(s["pa"], dya, ta=True, name=f"mm_gw_a_{i}", out_dtype=BF16)
        dpb = _mm(dyb, s["w_b2d"], name=f"mm_dpb_{i}", nt=True)
        gw_b = _mm(s["pb"], dyb, ta=True, name=f"mm_gw_b_{i}", split_n=N_DEV, out_dtype=BF16)
        dv, dbg, dcg, dza, dw0, dw1, dw2 = _branch_a_bwd(proj, dpa, s["conv8"], d, f"branch_a_bwd_{i}")
        small["conv_w"][i] = jnp.concatenate([dw0, dw1, dw2], axis=0)
        dzb, dgl, t1, db_glu = _glu_bwd1(s["y"], s["gl"], proj, b_glu[i], dpb, zb_col, f"glu_bwd1_{i}")
        small["b_glu"][i] = db_glu.reshape(ws)
        dyg2 = _mm(dgl, g["w_glu"], name=f"mm_dyg_{i}", nt=True)
        gw_glu = _mm(s["yg"], dgl, ta=True, name=f"mm_gw_glu_{i}", out_dtype=BF16)
        small_names_ = ("w_out_a", "w_glu", "w_out_b", "w_o")
        state, token = reduce_on_chip(
            [gw_a.reshape(N_DEV, d // N_DEV, d), gw_glu.reshape(N_DEV, ws // N_DEV, ws), gw_b,
             gw_o.reshape(N_DEV, d // N_DEV, d)], f"small_{i}")
        dy = _glu_bwd2(s["y"], t1, dyg2, f"glu_bwd2_{i}", deps=(token,))
        dy_seg = _to_segments(dy)
        u_seg = s["u_seg"]
        du_seg, gc_re, gc_im, gbb_re, gbb_im, glam_re, glam_im, dskip = _s5_backward(
            dy_seg, u_seg, s["h_re"], s["h_im"], *s5[i]["up_bwd"], *s5[i]["down_bwd"],
            s5[i]["lam_re"], -s5[i]["lam_im"], d_skip[i].reshape(1, ws), f"s5_backward_{i}")
        token = reduce_across_chips(small_names_, state, i, f"small_{i}", [du_seg])
        du = _from_segments(du_seg).astype(BF16)
        dproj = jnp.concatenate([dv, dbg, dcg, dza, du, dzb, dga, dgb], axis=1)
        gw_in = _mm(s["h"], dproj, ta=True, name=f"mm_gw_in_{i}", split_n=N_DEV, tm=1024,
                    out_dtype=BF16, deps=(token,))
        state, token = reduce_on_chip([gw_in], f"w_in_{i}")
        small["d_skip"][i] = dskip.reshape(n_groups, S5_GROUP)
        small["c_re"][i] = _block_diag_extract(gc_re, n_groups, S5_GROUP, n_state)
        small["c_im"][i] = -_block_diag_extract(gc_im, n_groups, S5_GROUP, n_state)
        gbb_re = _block_diag_extract(gbb_re, n_groups, S5_GROUP, n_state).transpose(0, 2, 1)
        gbb_im = _block_diag_extract(gbb_im, n_groups, S5_GROUP, n_state).transpose(0, 2, 1)
        gar, gai, gdt, gbr, gbi = _s5_params_bwd(
            *s5[i]["dense_in"], _dense(glam_re.reshape(n_groups, n_state, 1), shape3),
            _dense(glam_im.reshape(n_groups, n_state, 1), shape3),
            gbb_re.reshape(-1, LANES), gbb_im.reshape(-1, LANES), n_groups, f"s5_params_bwd_{i}", deps=(token,))
        small["a_re"][i] = gar.reshape(shape3)[:, :, 0]
        small["a_im"][i] = gai.reshape(shape3)[:, :, 0]
        small["log_dt"][i] = gdt[:, 0]
        small["b_re"][i] = gbr.reshape(shape3)
        small["b_im"][i] = gbi.reshape(shape3)
        if i == 0:
            part = {k: jnp.stack(small[k]) for k in main_names}
            main_state, token = gather_start([_pack([part[k] for k in main_names]).astype(BF16)], "ag_small")
            token = reduce_across_chips(("w_in",), state, i, f"w_in_{i}", [token])
            main_state, token = gather_forward(main_state, [token])
            dh = _mm_win_bwd(dproj, g["w_in"], f"mm_dh_{i}", deps=(token,))
            main_slots = gather_finish(main_state, [dh])[0]
            deps = ()
        else:
            dh = _mm_win_bwd(dproj, g["w_in"], f"mm_dh_{i}", deps=(gar,))
            deps = (reduce_across_chips(("w_in",), state, i, f"w_in_{i}", [dh]),)
        dx, dng = _rmsnorm_bwd(xs[i], norm_g[i], dh, dx, f"rmsnorm_bwd_{i}", deps=deps)
        small["norm_g"][i] = dng.reshape(d)

    results = {}

    gain_grads = jnp.concatenate([jnp.stack(small["norm_g"]).reshape(-1), g_final.reshape(d)])
    gain_shapes = [(depth, d), (d,), (1,)]
    gains_state, token = gather_start([_pack([gain_grads, loss_part[0, :1]])], "ag_gains")

    small_w = dict(a_re=(a_re, m_a_re, v_a_re), a_im=(a_im, m_a_im, v_a_im),
                   log_dt=(log_dt, m_log_dt, v_log_dt), b_re=(b_re, m_b_re, v_b_re), b_im=(b_im, m_b_im, v_b_im),
                   c_re=(c_re, m_c_re, v_c_re), c_im=(c_im, m_c_im, v_c_im), d_skip=(d_skip, m_d_skip, v_d_skip),
                   b_glu=(b_glu, m_b_glu, v_b_glu))
    shapes = [part[k].shape for k in main_names]
    zeros_conv = jnp.zeros(part["conv_w"].shape, F32)
    wpack, mpack, vpack = [_pack([small_w[k][q] for k in main_names[:-1]] + [zeros_conv])[None] for q in range(3)]
    sres = [_unpack(p[0], shapes)
            for p in _adamw(wpack, mpack, vpack, main_slots, "adamw_small", deps=(token,))]
    for j, k in enumerate(main_names[:-1]):
        results[k] = [sres[q][j] for q in range(4)]
    dc = d // N_DEV
    gconv = lax.dynamic_slice_in_dim(sres[0][-1], me * dc, dc, axis=2)
    pad8 = lambda a: jnp.pad(a.reshape(depth * 3, dc), ((0, SUBLANES - depth * 3), (0, 0)))[None]
    cres = _adamw(pad8(conv_w), pad8(m_conv_w), pad8(v_conv_w), pad8(gconv), "adamw_conv_w")
    results["conv_w"] = [r_[0, :depth * 3].reshape(depth, 3, dc) for r_ in cres]

    after = [cres[0]]
    for names_, layer, sems, srcs, lands, wait_name in rs_pending:
        _, slots = _exchange_wait(_plan_reduce_chips, sems, srcs, lands, after, wait_name)
        for k, land in zip(names_, slots):
            w_, m_, v_ = big[k]
            results[k] = _adamw(w_, m_, v_, land, f"adamw_{k}_{layer}", layer=layer, prev=results.get(k))
            after = [results[k][0]]

    gains_state, token = gather_forward(gains_state, after)
    gpack = gather_finish(gains_state, [token])[0]
    zero1 = jnp.zeros((1,), F32)
    gres = [_unpack(p[0], gain_shapes) for p in _adamw(
        _pack([norm_g, final_g, zero1])[None], _pack([m_norm_g, m_final_g, zero1])[None],
        _pack([v_norm_g, v_final_g, zero1])[None], gpack, "adamw_gains")]
    results["norm_g"] = [gres[q][0] for q in range(4)]
    results["final_g"] = [gres[q][1] for q in range(4)]
    loss = gres[0][2][0]

    names = ("norm_g", "w_in", "conv_w", "w_out_a", "a_re", "a_im", "log_dt", "b_re", "b_im", "c_re", "c_im",
             "d_skip", "w_glu", "b_glu", "w_out_b", "w_o", "final_g")
    outs = [loss, dx[None]]
    for q in range(4):
        outs += [results[k][q] for k in names]
    return tuple(outs)
```

```python
import functools
import math

import jax
import jax.numpy as jnp
from jax import lax
from jax.experimental import pallas as pl
from jax.experimental.pallas import tpu as pltpu

F32 = jnp.float32
BF16 = jnp.bfloat16
HIGHEST = lax.Precision.HIGHEST

N_DEV = 8
LANES = 128
SUBLANES = 8
VMEM_LIMIT_BYTES = 56 * 1024 * 1024

RMS_EPS = 1e-6
ADAM_LR = 0.001
ADAM_B1 = 0.9
ADAM_B2 = 0.999
ADAM_EPS = 1e-08
ADAM_WD = 0.01
ADAM_STEP = 10
GELU_C0 = math.sqrt(2.0 / math.pi)
GELU_C1 = 0.044715

ADAMW_BLOCK_ELEMS = 1 << 17
PACK_ROWS = 512

S5_GROUP = 16
S5_GB = LANES // S5_GROUP


def _params(*semantics):
    return pltpu.CompilerParams(dimension_semantics=semantics, vmem_limit_bytes=VMEM_LIMIT_BYTES)


ANY_SPEC = pl.BlockSpec(memory_space=pl.ANY)


def _pallas(body, args, deps=(), *, in_specs, **kwargs):
    deps = tuple(deps)
    if not deps:
        return pl.pallas_call(body, in_specs=in_specs, **kwargs)(*args)

    def body_after(*refs):
        body(*refs[len(deps):])

    return pl.pallas_call(body_after, in_specs=[ANY_SPEC] * len(deps) + list(in_specs), **kwargs)(*deps, *args)


def _tile(n, pref):
    t = min(n, pref)
    while n % t:
        assert t % 2 == 0, (n, pref)
        t //= 2
    return t


def _sigmoid(z):
    return 1.0 / (1.0 + jnp.exp(-z))


def _gelu(y):
    return 0.5 * y * (1.0 + jnp.tanh(GELU_C0 * (y + GELU_C1 * y * y * y)))


def _gelu_grad(y):
    t = jnp.tanh(GELU_C0 * (y + GELU_C1 * y * y * y))
    return 0.5 * (1.0 + t) + 0.5 * y * (1.0 - t * t) * GELU_C0 * (1.0 + 3.0 * GELU_C1 * y * y)


HBM_SPEC = pl.BlockSpec(memory_space=pltpu.HBM)
SEM_SPEC = pl.BlockSpec(memory_space=pltpu.SEMAPHORE)
DATAFLOW_EFFECT = pltpu.SideEffectType.DATAFLOW_SIDE_EFFECTING
OTHER_CHIPS = (2, 4, 6)


def _flip(pos, mask):
    x, y, c = pos
    return x ^ ((mask >> 2) & 1), y ^ ((mask >> 1) & 1), c ^ (mask & 1)


def _dev(pos):
    return 4 * pos[0] + 2 * pos[1] + pos[2]


def _chip(pos):
    return 2 * pos[0] + pos[1]


def _plan_gather_chips(me):
    return [(_flip(me, k), None, _dev(me), _dev(_flip(me, k))) for k in (1,) + OTHER_CHIPS]


def _plan_gather_forward(me):
    sib = _flip(me, 1)
    return [(sib, _dev(_flip(me, k)), _dev(_flip(me, k)), _dev(_flip(sib, k))) for k in OTHER_CHIPS]


def _plan_reduce_sibling(me):
    sib = _flip(me, 1)
    return [(sib, 2 * q + sib[2], q, q) for q in range(4)]


def _plan_reduce_chips(me):
    return [(_flip(me, k), _chip(_flip(me, k)), _chip(me), _chip(_flip(me, k))) for k in OTHER_CHIPS]


PLAN_COPIES = {_plan_gather_chips: 4, _plan_gather_forward: 3, _plan_reduce_sibling: 4, _plan_reduce_chips: 3}


def _exchange_copies(plan, src_refs, land_refs, send_sems, recv_sems, incoming=True):
    me = (lax.axis_index("x"), lax.axis_index("y"), lax.axis_index("c"))
    pairs = []
    for b, (src_ref, land_ref) in enumerate(zip(src_refs, land_refs)):
        for j, (peer, src_slot, there, here) in enumerate(plan(me)):
            sem = b * PLAN_COPIES[plan] + j
            src = src_ref if src_slot is None else src_ref.at[src_slot]
            out = pltpu.make_async_remote_copy(
                src_ref=src, dst_ref=land_ref.at[there], send_sem=send_sems.at[sem], recv_sem=recv_sems.at[sem],
                device_id=peer, device_id_type=pl.DeviceIdType.MESH)
            inc = pltpu.make_async_remote_copy(
                src_ref=src, dst_ref=land_ref.at[here], send_sem=send_sems.at[sem], recv_sem=recv_sems.at[sem],
                device_id=peer, device_id_type=pl.DeviceIdType.MESH) if incoming else None
            pairs.append((out, inc))
    return pairs


def _exchange_start(plan, srcs, lands, name, deps=()):
    srcs = [] if srcs is None else list(srcs)
    ns, n, nd = len(srcs), len(lands), len(deps)

    def body(*refs):
        land_refs = refs[ns:ns + n]
        sems_at = ns + n + nd
        pairs = _exchange_copies(plan, refs[:ns] if ns else land_refs, land_refs, refs[sems_at], refs[sems_at + 1],
                                 incoming=False)
        for out, _ in pairs:
            out.start()
        token = refs[-1]
        token[...] = jnp.zeros_like(token)

    sems = pltpu.SemaphoreType.DMA((PLAN_COPIES[plan] * n,))
    bufs = srcs + list(lands)
    outs = pl.pallas_call(
        body, name=name,
        out_shape=(sems, sems, *[pltpu.HBM(a.shape, a.dtype) for a in bufs],
                   jax.ShapeDtypeStruct((SUBLANES, LANES), F32)),
        in_specs=[HBM_SPEC] * (ns + n) + [ANY_SPEC] * nd,
        out_specs=(SEM_SPEC, SEM_SPEC, *[HBM_SPEC] * (ns + n), pl.BlockSpec(memory_space=pltpu.VMEM)),
        input_output_aliases={i: 2 + i for i in range(ns + n)},
        compiler_params=pltpu.CompilerParams(has_side_effects=DATAFLOW_EFFECT),
    )(*[pltpu.with_memory_space_constraint(a, pltpu.HBM) for a in bufs], *deps)
    return (outs[0], outs[1]), (outs[2:2 + ns] if ns else None), outs[2 + ns:2 + ns + n], outs[-1]


def _exchange_wait(plan, sems, srcs, lands, after, name):
    srcs = [] if srcs is None else list(srcs)
    ns, n = len(srcs), len(lands)

    def body(*refs):
        land_refs = refs[ns:ns + n]
        pairs = _exchange_copies(plan, refs[:ns] if ns else land_refs, land_refs, refs[ns + n], refs[ns + n + 1])
        for out, inc in pairs:
            out.wait_send()
            inc.wait_recv()

    bufs = srcs + list(lands)
    outs = pl.pallas_call(
        body, name=name,
        out_shape=[pltpu.HBM(a.shape, a.dtype) for a in bufs],
        in_specs=[HBM_SPEC] * (ns + n) + [SEM_SPEC, SEM_SPEC] + [ANY_SPEC] * len(after),
        out_specs=[HBM_SPEC] * (ns + n),
        input_output_aliases={i: i for i in range(ns + n)},
        compiler_params=pltpu.CompilerParams(has_side_effects=DATAFLOW_EFFECT),
    )(*bufs, sems[0], sems[1], *after)
    return outs[:ns], outs[ns:]


def _landing(own, slots, slot):
    land = lax.empty((slots,) + own.shape, own.dtype)
    return lax.dynamic_update_slice(land, own[None], (slot,) + (0,) * own.ndim)


def _chip_sums(pieces, land, name):
    _, r, c_ = land.shape
    tr = _tile(r, max(2 * SUBLANES, 1 << int(math.log2(4 * ADAMW_BLOCK_ELEMS // c_))))

    def body(core_ref, p_ref, l_ref, o_ref):
        o_ref[...] = (p_ref[...].astype(F32) + l_ref[...].astype(F32)).astype(o_ref.dtype)

    spec = pl.BlockSpec((None, tr, c_), lambda q, i, core: (q, i, 0))
    return pl.pallas_call(
        body, name=name,
        grid_spec=pltpu.PrefetchScalarGridSpec(
            num_scalar_prefetch=1, grid=(4, r // tr),
            in_specs=[pl.BlockSpec((None, tr, c_), lambda q, i, core: (2 * q + core[0], i, 0)), spec],
            out_specs=spec),
        out_shape=jax.ShapeDtypeStruct(land.shape, land.dtype),
        compiler_params=_params("parallel", "parallel"),
    )(lax.axis_index("c").reshape(1), pieces, land)


def _mm(a, b, *, name, nt=False, ta=False, out_dtype=F32, add=None, split_n=None, tm=512, tn=1024, deps=()):
    k, m = a.shape if ta else a.shape[::-1]
    n = b.shape[0] if nt else b.shape[1]
    tm = _tile(m, tm)
    tn = n // split_n if split_n else _tile(n, tn)
    dims = (((0 if ta else 1,), (1 if nt else 0,)), ((), ()))

    def body(*refs):
        a_ref, b_ref = refs[0], refs[1]
        o_ref = refs[-1]
        acc = lax.dot_general(a_ref[...], b_ref[...], dims, preferred_element_type=F32)
        if add is not None:
            acc = acc + refs[2][...]
        o_ref[...] = acc.astype(o_ref.dtype)

    in_specs = [pl.BlockSpec((k, tm), lambda i, j: (0, i)) if ta else pl.BlockSpec((tm, k), lambda i, j: (i, 0)),
                pl.BlockSpec((tn, k), lambda i, j: (j, 0)) if nt
                else pl.BlockSpec((k, tn), lambda i, j: (0, j))]
    args = [a, b]
    if add is not None:
        in_specs.append(pl.BlockSpec((tm, tn), lambda i, j: (i, j)))
        args.append(add)
    if split_n:
        out_shape = jax.ShapeDtypeStruct((split_n, m, tn), out_dtype)
        out_spec = pl.BlockSpec((None, tm, tn), lambda i, j: (j, i, 0))
    else:
        out_shape = jax.ShapeDtypeStruct((m, n), out_dtype)
        out_spec = pl.BlockSpec((tm, tn), lambda i, j: (i, j))
    return _pallas(
        body, args, deps, name=name, grid=(m // tm, n // tn), in_specs=in_specs, out_specs=out_spec,
        out_shape=out_shape, compiler_params=_params("parallel", "parallel"))


def _mm_win_fwd(h, w_g, name, deps=()):
    m, k = h.shape
    nj = w_g.shape[2]
    tm = _tile(m, 512)

    def body(a_ref, b_ref, o_ref):
        o_ref[...] = jnp.dot(a_ref[...], b_ref[...], preferred_element_type=F32).astype(o_ref.dtype)

    return _pallas(
        body, [h, w_g], deps, name=name, grid=(N_DEV, m // tm),
        in_specs=[pl.BlockSpec((tm, k), lambda j, i: (i, 0)),
                  pl.BlockSpec((None, k, nj), lambda j, i: (j, 0, 0))],
        out_specs=pl.BlockSpec((tm, nj), lambda j, i: (i, j)),
        out_shape=jax.ShapeDtypeStruct((m, N_DEV * nj), BF16),
        compiler_params=_params("parallel", "parallel"))


def _mm_win_bwd(dproj, w_g, name, deps=()):
    m = dproj.shape[0]
    d, nj = w_g.shape[1], w_g.shape[2]
    tm = _tile(m, 512)
    tn = _tile(d, 1024)

    def body(a_ref, b_ref, o_ref, acc_ref):
        j = pl.program_id(2)

        @pl.when(j == 0)
        def _():
            acc_ref[...] = jnp.zeros_like(acc_ref)

        acc_ref[...] += lax.dot_general(a_ref[...], b_ref[...], (((1,), (1,)), ((), ())),
                                        preferred_element_type=F32)

        @pl.when(j == N_DEV - 1)
        def _():
            o_ref[...] = acc_ref[...]

    return _pallas(
        body, [dproj, w_g], deps, name=name, grid=(m // tm, d // tn, N_DEV),
        in_specs=[pl.BlockSpec((tm, nj), lambda i, n, j: (i, j)),
                  pl.BlockSpec((None, tn, nj), lambda i, n, j: (j, n, 0))],
        out_specs=pl.BlockSpec((tm, tn), lambda i, n, j: (i, n)),
        out_shape=jax.ShapeDtypeStruct((m, d), F32),
        scratch_shapes=[pltpu.VMEM((tm, tn), F32)],
        compiler_params=_params("parallel", "parallel", "arbitrary"))


def _row_spec(tr, w, col):
    return pl.BlockSpec((tr, w), lambda i: (i, col))


def _full_spec(shape):
    return pl.BlockSpec(shape, lambda i: (0,) * len(shape))


def _rmsnorm_fwd(x, g, name, deps=()):
    l, d = x.shape
    tr = _tile(l, 256)

    def body(x_ref, g_ref, o_ref):
        xv = x_ref[...]
        rstd = lax.rsqrt(jnp.mean(xv * xv, axis=-1, keepdims=True) + RMS_EPS)
        o_ref[...] = (xv * rstd * g_ref[...]).astype(o_ref.dtype)

    return _pallas(
        body, [x, g.reshape(1, d)], deps, name=name, grid=(l // tr,),
        in_specs=[_row_spec(tr, d, 0), _full_spec((1, d))],
        out_specs=_row_spec(tr, d, 0),
        out_shape=jax.ShapeDtypeStruct((l, d), BF16),
        compiler_params=_params("parallel"))


def _rmsnorm_bwd(x, g, dh, dxo, name, deps=()):
    l, d = x.shape
    tr = _tile(l, 256)

    def body(x_ref, g_ref, dh_ref, dxo_ref, dx_ref, dg_ref):
        xv = x_ref[...]
        rstd = lax.rsqrt(jnp.mean(xv * xv, axis=-1, keepdims=True) + RMS_EPS)
        dhv = dh_ref[...]
        gdy = dhv * g_ref[...]
        dot = jnp.mean(gdy * xv, axis=-1, keepdims=True)
        dx_ref[...] = dxo_ref[...] + rstd * gdy - xv * (rstd * rstd * rstd * dot)

        @pl.when(pl.program_id(0) == 0)
        def _():
            dg_ref[...] = jnp.zeros_like(dg_ref)

        dg_ref[...] += jnp.sum(dhv * xv * rstd, axis=0, keepdims=True)

    return _pallas(
        body, [x, g.reshape(1, d), dh, dxo], deps, name=name, grid=(l // tr,),
        in_specs=[_row_spec(tr, d, 0), _full_spec((1, d)), _row_spec(tr, d, 0), _row_spec(tr, d, 0)],
        out_specs=[_row_spec(tr, d, 0), _full_spec((1, d))],
        out_shape=[jax.ShapeDtypeStruct((l, d), F32), jax.ShapeDtypeStruct((1, d), F32)],
        compiler_params=_params("arbitrary"))


def _final_loss(x, g, tgt, name):
    l, d = x.shape
    tr = _tile(l, 256)

    def body(x_ref, g_ref, t_ref, dx_ref, dg_ref, loss_ref):
        xv = x_ref[...]
        gv = g_ref[...]
        rstd = lax.rsqrt(jnp.mean(xv * xv, axis=-1, keepdims=True) + RMS_EPS)
        xn = xv * rstd
        err = xn * gv - t_ref[...]
        dy = err * (1.0 / d)
        gdy = dy * gv
        dot = jnp.mean(gdy * xv, axis=-1, keepdims=True)
        dx_ref[...] = rstd * gdy - xv * (rstd * rstd * rstd * dot)

        @pl.when(pl.program_id(0) == 0)
        def _():
            dg_ref[...] = jnp.zeros_like(dg_ref)
            loss_ref[...] = jnp.zeros_like(loss_ref)

        dg_ref[...] += jnp.sum(dy * xn, axis=0, keepdims=True)
        loss_ref[...] += (0.5 / d) * jnp.sum(err * err)

    return pl.pallas_call(
        body, name=name, grid=(l // tr,),
        in_specs=[_row_spec(tr, d, 0), _full_spec((1, d)), _row_spec(tr, d, 0)],
        out_specs=[_row_spec(tr, d, 0), _full_spec((1, d)), _full_spec((SUBLANES, LANES))],
        out_shape=[jax.ShapeDtypeStruct((l, d), F32), jax.ShapeDtypeStruct((1, d), F32),
                   jax.ShapeDtypeStruct((SUBLANES, LANES), F32)],
        compiler_params=_params("arbitrary"),
    )(x, g.reshape(1, d), tgt)


HALO = 2 * SUBLANES


def _halo_spec(tr, w, col, nblk, before):
    step = tr // HALO
    if before:
        return pl.BlockSpec((HALO, w), lambda i: (jnp.maximum(i * step - 1, 0), col))
    return pl.BlockSpec((HALO, w), lambda i: (jnp.minimum((i + 1) * step, nblk - 1), col))


def _shift_down(cur, before, k):
    ext = jnp.concatenate([before, cur], axis=0)
    return pltpu.roll(ext, k, axis=0)[HALO:, :]


def _shift_up(cur, after, k):
    tr = cur.shape[0]
    ext = jnp.concatenate([cur, after], axis=0)
    return pltpu.roll(ext, tr + HALO - k, axis=0)[:tr, :]


def _f32(ref):
    return ref[...].astype(F32)


def _branch_a_fwd(proj, conv_w, d, name, deps=()):
    l = proj.shape[0]
    tr = _tile(l, 256)
    nblk8 = l // HALO

    def body(v_ref, bg_ref, cg_ref, za_ref, vh_ref, cgh_ref, w_ref, o_ref):
        first = pl.program_id(0) == 0
        cv = _f32(cg_ref) * _f32(v_ref)
        cvh = jnp.where(first, 0.0, _f32(cgh_ref) * _f32(vh_ref))
        w0, w1, w2 = w_ref[0:1, :], w_ref[1:2, :], w_ref[2:3, :]
        q = w2 * cv + w1 * _shift_down(cv, cvh, 1) + w0 * _shift_down(cv, cvh, 2)
        za = _f32(za_ref)
        o_ref[...] = (_f32(bg_ref) * q * (za * _sigmoid(za))).astype(o_ref.dtype)

    return _pallas(
        body, [proj, proj, proj, proj, proj, proj, conv_w], deps, name=name, grid=(l // tr,),
        in_specs=[_row_spec(tr, d, 0), _row_spec(tr, d, 1), _row_spec(tr, d, 2), _row_spec(tr, d, 3),
                  _halo_spec(tr, d, 0, nblk8, True), _halo_spec(tr, d, 2, nblk8, True),
                  _full_spec((SUBLANES, d))],
        out_specs=_row_spec(tr, d, 0),
        out_shape=jax.ShapeDtypeStruct((l, d), BF16),
        compiler_params=_params("parallel"))


def _branch_a_bwd(proj, dpa, conv_w, dproj, d, name):
    l = proj.shape[0]
    tr = _tile(l, 128)
    nblk8 = l // HALO
    ntiles = l // tr

    def body(v_ref, bg_ref, cg_ref, za_ref, dpa_ref, vh_ref, cgh_ref, bgn_ref, zan_ref, dpan_ref,
             w_ref, _, o_ref, dw0_ref, dw1_ref, dw2_ref):
        dv_ref, dbg_ref, dcg_ref, dza_ref = [o_ref.at[:, pl.ds(k * d, d)] for k in range(4)]
        i = pl.program_id(0)
        v, bg, cg, za, dpa_v = _f32(v_ref), _f32(bg_ref), _f32(cg_ref), _f32(za_ref), dpa_ref[...]
        w0, w1, w2 = w_ref[0:1, :], w_ref[1:2, :], w_ref[2:3, :]
        cv = cg * v
        cvh = jnp.where(i == 0, 0.0, _f32(cgh_ref) * _f32(vh_ref))
        cv1 = _shift_down(cv, cvh, 1)
        cv2 = _shift_down(cv, cvh, 2)
        q = w2 * cv + w1 * cv1 + w0 * cv2
        sg = _sigmoid(za)
        s = za * sg
        dbg_ref[...] = (dpa_v * q * s).astype(dbg_ref.dtype)
        dza_ref[...] = (dpa_v * bg * q * (sg * (1.0 + za * (1.0 - sg)))).astype(dza_ref.dtype)
        dq = dpa_v * bg * s
        zan = _f32(zan_ref)
        dqn = jnp.where(i == ntiles - 1, 0.0, dpan_ref[...] * _f32(bgn_ref) * (zan * _sigmoid(zan)))
        dcv = w2 * dq + w1 * _shift_up(dq, dqn, 1) + w0 * _shift_up(dq, dqn, 2)
        dcg_ref[...] = (dcv * v).astype(dcg_ref.dtype)
        dv_ref[...] = (dcv * cg).astype(dv_ref.dtype)

        @pl.when(i == 0)
        def _():
            dw0_ref[...] = jnp.zeros_like(dw0_ref)
            dw1_ref[...] = jnp.zeros_like(dw1_ref)
            dw2_ref[...] = jnp.zeros_like(dw2_ref)

        dw0_ref[...] += jnp.sum(dq * cv2, axis=0, keepdims=True)
        dw1_ref[...] += jnp.sum(dq * cv1, axis=0, keepdims=True)
        dw2_ref[...] += jnp.sum(dq * cv, axis=0, keepdims=True)

    wsum = jax.ShapeDtypeStruct((1, d), F32)
    return pl.pallas_call(
        body, name=name, grid=(ntiles,),
        in_specs=[_row_spec(tr, d, 0), _row_spec(tr, d, 1), _row_spec(tr, d, 2), _row_spec(tr, d, 3),
                  _row_spec(tr, d, 0),
                  _halo_spec(tr, d, 0, nblk8, True), _halo_spec(tr, d, 2, nblk8, True),
                  _halo_spec(tr, d, 1, nblk8, False), _halo_spec(tr, d, 3, nblk8, False),
                  _halo_spec(tr, d, 0, nblk8, False),
                  _full_spec((SUBLANES, d)), ANY_SPEC],
        out_specs=[_row_spec(tr, 4 * d, 0)] + [_full_spec((1, d))] * 3,
        out_shape=[jax.ShapeDtypeStruct(dproj.shape, dproj.dtype)] + [wsum] * 3,
        input_output_aliases={11: 0},
        compiler_params=_params("arbitrary"),
    )(proj, proj, proj, proj, dpa, proj, proj, proj, proj, dpa, conv_w, dproj)


def _gelu_cast(y, name):
    l, w = y.shape
    tr = _tile(l, 512)

    def body(y_ref, o_ref):
        o_ref[...] = _gelu(y_ref[...]).astype(o_ref.dtype)

    return pl.pallas_call(
        body, name=name, grid=(l // tr,), in_specs=[_row_spec(tr, w, 0)],
        out_specs=_row_spec(tr, w, 0), out_shape=jax.ShapeDtypeStruct((l, w), BF16),
        compiler_params=_params("parallel"),
    )(y)


def _glu_post(y, gl, proj, b_glu, zb_col, name):
    l, w = y.shape
    tr = _tile(l, 512)

    def body(y_ref, gl_ref, zb_ref, b_ref, o_ref):
        zb = _f32(zb_ref)
        o_ref[...] = (_gelu(y_ref[...]) * _sigmoid(gl_ref[...] + b_ref[...])
                      * (zb * _sigmoid(zb))).astype(o_ref.dtype)

    return pl.pallas_call(
        body, name=name, grid=(l // tr,),
        in_specs=[_row_spec(tr, w, 0), _row_spec(tr, w, 0), _row_spec(tr, w, zb_col), _full_spec((1, w))],
        out_specs=_row_spec(tr, w, 0), out_shape=jax.ShapeDtypeStruct((l, w), BF16),
        compiler_params=_params("parallel"),
    )(y, gl, proj, b_glu.reshape(1, w))


def _glu_bwd1(y, gl, proj, b_glu, dpb, dproj, zb_col, name):
    l, w = y.shape
    tr = _tile(l, 512)

    def body(y_ref, gl_ref, zb_ref, b_ref, dpb_ref, _, dzb_ref, dgl_ref, t_ref, db_ref):
        zb = _f32(zb_ref)
        dpb_v = dpb_ref[...]
        yg = _gelu(y_ref[...])
        sgl = _sigmoid(gl_ref[...] + b_ref[...])
        szb = _sigmoid(zb)
        dzb_ref[...] = (dpb_v * yg * sgl * (szb * (1.0 + zb * (1.0 - szb)))).astype(dzb_ref.dtype)
        e = dpb_v * (zb * szb)
        dgl = e * yg * sgl * (1.0 - sgl)
        dgl_ref[...] = dgl.astype(dgl_ref.dtype)
        t_ref[...] = e * sgl

        @pl.when(pl.program_id(0) == 0)
        def _():
            db_ref[...] = jnp.zeros_like(db_ref)

        db_ref[...] += jnp.sum(dgl, axis=0, keepdims=True)

    return pl.pallas_call(
        body, name=name, grid=(l // tr,),
        in_specs=[_row_spec(tr, w, 0), _row_spec(tr, w, 0), _row_spec(tr, w, zb_col), _full_spec((1, w)),
                  _row_spec(tr, w, 0), ANY_SPEC],
        out_specs=[_row_spec(tr, w, zb_col)] + [_row_spec(tr, w, 0)] * 2 + [_full_spec((1, w))],
        out_shape=[jax.ShapeDtypeStruct(dproj.shape, dproj.dtype), jax.ShapeDtypeStruct((l, w), BF16),
                   jax.ShapeDtypeStruct((l, w), F32), jax.ShapeDtypeStruct((1, w), F32)],
        input_output_aliases={5: 0},
        compiler_params=_params("arbitrary"),
    )(y, gl, proj, b_glu.reshape(1, w), dpb, dproj)


def _write_cols(dproj, cols, col, name):
    l, w = cols.shape
    tr = _tile(l, 512)

    def body(c_ref, _, o_ref):
        o_ref[...] = c_ref[...].astype(o_ref.dtype)

    return pl.pallas_call(
        body, name=name, grid=(l // tr,), in_specs=[_row_spec(tr, w, 0), ANY_SPEC],
        out_specs=_row_spec(tr, w, col), out_shape=jax.ShapeDtypeStruct(dproj.shape, dproj.dtype),
        input_output_aliases={1: 0}, compiler_params=_params("parallel"),
    )(cols, dproj)


def _glu_bwd2(y, t1, dyg2, name, deps=()):
    l, w = y.shape
    tr = _tile(l, 512)

    def body(y_ref, t_ref, d_ref, o_ref):
        o_ref[...] = (t_ref[...] + d_ref[...]) * _gelu_grad(y_ref[...])

    return _pallas(
        body, [y, t1, dyg2], deps, name=name, grid=(l // tr,), in_specs=[_row_spec(tr, w, 0)] * 3,
        out_specs=_row_spec(tr, w, 0), out_shape=jax.ShapeDtypeStruct((l, w), F32),
        compiler_params=_params("parallel"))


def _merge_fwd(proj, ya, yb, d, ga_col, gb_col, name):
    l = proj.shape[0]
    tr = _tile(l, 256)

    def body(ga_ref, gb_ref, ya_ref, yb_ref, o_ref):
        o_ref[...] = (_sigmoid(_f32(ga_ref)) * ya_ref[...]
                      + _sigmoid(_f32(gb_ref)) * yb_ref[...]).astype(o_ref.dtype)

    return pl.pallas_call(
        body, name=name, grid=(l // tr,),
        in_specs=[_row_spec(tr, d, ga_col), _row_spec(tr, d, gb_col), _row_spec(tr, d, 0), _row_spec(tr, d, 0)],
        out_specs=_row_spec(tr, d, 0), out_shape=jax.ShapeDtypeStruct((l, d), BF16),
        compiler_params=_params("parallel"),
    )(proj, proj, ya, yb)


def _merge_bwd(proj, ya, yb, dm, d, ga_col, name):
    l, n = proj.shape
    tr = _tile(l, 256)

    def body(g_ref, ya_ref, yb_ref, dm_ref, dy_ref, dg_ref):
        dmv = dm_ref[...]
        sg = _sigmoid(_f32(g_ref))
        yv = jnp.where(pl.program_id(1) == 0, ya_ref[...], yb_ref[...])
        dy_ref[...] = (dmv * sg).astype(dy_ref.dtype)
        dg_ref[...] = (dmv * yv * sg * (1.0 - sg)).astype(dg_ref.dtype)

    row = pl.BlockSpec((tr, d), lambda i, j: (i, 0))
    return pl.pallas_call(
        body, name=name, grid=(l // tr, 2),
        in_specs=[pl.BlockSpec((tr, d), lambda i, j: (i, ga_col + j)), row, row, row],
        out_specs=[pl.BlockSpec((None, tr, d), lambda i, j: (j, i, 0)),
                   pl.BlockSpec((tr, d), lambda i, j: (i, ga_col + j))],
        out_shape=[jax.ShapeDtypeStruct((2, l, d), BF16), jax.ShapeDtypeStruct((l, n), BF16)],
        compiler_params=_params("parallel", "arbitrary"),
    )(proj, ya, yb, dm)


def _to_segments(a):
    l, w = a.shape
    return a.reshape(SUBLANES, l // SUBLANES, w).transpose(1, 0, 2).reshape(l, w)


def _from_segments(a):
    l, w = a.shape
    return a.reshape(l // SUBLANES, SUBLANES, w).transpose(1, 0, 2).reshape(l, w)


def _dense(z, shape):
    return jnp.broadcast_to(z, shape).reshape(-1, LANES)


def _s5_disc(are, aim, ldt):
    dt = jnp.exp(ldt)
    er = jnp.exp(are * dt)
    lbr = er * jnp.cos(aim * dt)
    lbi = er * jnp.sin(aim * dt)
    inv = 1.0 / (are * are + aim * aim)
    fr = ((lbr - 1.0) * are + lbi * aim) * inv
    fi = (lbi * are - (lbr - 1.0) * aim) * inv
    return dt, lbr, lbi, inv, fr, fi


def _s5_params(are, aim, ldt, bre, bim, name, deps=()):
    shape = are.shape

    def body(are_ref, aim_ref, ldt_ref, bre_ref, bim_ref, lbr_ref, lbi_ref, bbr_ref, bbi_ref):
        _, lbr, lbi, _, fr, fi = _s5_disc(are_ref[...], aim_ref[...], ldt_ref[...])
        lbr_ref[...] = lbr
        lbi_ref[...] = lbi
        bbr_ref[...] = fr * bre_ref[...] - fi * bim_ref[...]
        bbi_ref[...] = fr * bim_ref[...] + fi * bre_ref[...]

    out = jax.ShapeDtypeStruct(shape, F32)
    return _pallas(body, [are, aim, ldt, bre, bim], deps, name=name,
                   in_specs=[pl.BlockSpec(memory_space=pltpu.VMEM)] * 5, out_shape=[out] * 4,
                   compiler_params=pltpu.CompilerParams(vmem_limit_bytes=VMEM_LIMIT_BYTES))


def _s5_params_bwd(are, aim, ldt, bre, bim, glbr, glbi, gbbr, gbbi, n_groups, name, deps=()):
    shape = are.shape
    rows_per_group = shape[0] // n_groups

    def body(are_ref, aim_ref, ldt_ref, bre_ref, bim_ref, glbr_ref, glbi_ref, gbbr_ref, gbbi_ref,
             gar_ref, gai_ref, gdt_ref, gbr_ref, gbi_ref):
        are_v, aim_v = are_ref[...], aim_ref[...]
        bre_v, bim_v = bre_ref[...], bim_ref[...]
        gbbr_v, gbbi_v = gbbr_ref[...], gbbi_ref[...]
        dt, lbr, lbi, inv, fr, fi = _s5_disc(are_v, aim_v, ldt_ref[...])
        gbr_ref[...] = fr * gbbr_v + fi * gbbi_v
        gbi_ref[...] = fr * gbbi_v - fi * gbbr_v
        lane_group = lax.broadcasted_iota(jnp.int32, (LANES, LANES), 0) // S5_GROUP
        same_group = (lane_group == lax.broadcasted_iota(jnp.int32, (LANES, LANES), 1) // S5_GROUP)
        ones = same_group.astype(F32)
        gfr = jnp.dot(bre_v * gbbr_v + bim_v * gbbi_v, ones, precision=HIGHEST, preferred_element_type=F32)
        gfi = jnp.dot(bre_v * gbbi_v - bim_v * gbbr_v, ones, precision=HIGHEST, preferred_element_type=F32)
        glr = glbr_ref[...] + (are_v * gfr - aim_v * gfi) * inv
        gli = glbi_ref[...] + (are_v * gfi + aim_v * gfr) * inv
        qr = (fr * are_v + fi * aim_v) * inv
        qi = (fi * are_v - fr * aim_v) * inv
        gzr = lbr * glr + lbi * gli
        gzi = lbr * gli - lbi * glr
        gar_ref[...] = dt * gzr - (qr * gfr + qi * gfi)
        gai_ref[...] = dt * gzi - (qr * gfi - qi * gfr)
        e = dt * (are_v * gzr + aim_v * gzi)
        per_group = jnp.sum(e.reshape(n_groups, rows_per_group, LANES), axis=1)
        total = jnp.sum(per_group, axis=1, keepdims=True) * (1.0 / S5_GROUP)
        gdt_ref[...] = jnp.broadcast_to(total, gdt_ref.shape)

    out = jax.ShapeDtypeStruct(shape, F32)
    return _pallas(
        body, [are, aim, ldt, bre, bim, glbr, glbi, gbbr, gbbi], deps, name=name,
        in_specs=[pl.BlockSpec(memory_space=pltpu.VMEM)] * 9,
        out_shape=[out, out, jax.ShapeDtypeStruct((n_groups, LANES), F32), out, out],
        compiler_params=pltpu.CompilerParams(vmem_limit_bytes=VMEM_LIMIT_BYTES))


def _cmul(ar, ai, br, bi):
    return ar * br - ai * bi, ar * bi + ai * br


def _scan_in_place(hr_ref, hi_ref, lr, li, reverse):
    l, wb = hr_ref.shape
    nt = l // SUBLANES
    shift = SUBLANES - 1 if reverse else 1
    unroll = 8 if nt % 8 == 0 else 1

    def rows(k):
        t = (nt - 1 - k) if reverse else k
        return pl.ds(pl.multiple_of(t * SUBLANES, SUBLANES), SUBLANES)

    zero = jnp.zeros((SUBLANES, wb), F32)
    one = jnp.ones((SUBLANES, wb), F32)

    def local_step(k, carry):
        hr, hi, pr, pi = carry
        r = rows(k)
        tr_, ti_ = _cmul(lr, li, hr, hi)
        hr, hi = tr_ + hr_ref[r, :], ti_ + hi_ref[r, :]
        hr_ref[r, :] = hr
        hi_ref[r, :] = hi
        pr, pi = _cmul(lr, li, pr, pi)
        return hr, hi, pr, pi

    er, ei, lnr, lni = lax.fori_loop(0, nt, local_step, (zero, zero, one, zero), unroll=unroll)

    row = lax.broadcasted_iota(jnp.int32, (SUBLANES, wb), 0)
    tr_, ti_ = er, ei
    for j in range(1, SUBLANES):
        pr_, pi_ = _cmul(lnr, lni, pltpu.roll(tr_, shift, axis=0), pltpu.roll(ti_, shift, axis=0))
        at = row == ((SUBLANES - 1 - j) if reverse else j)
        tr_ = jnp.where(at, er + pr_, tr_)
        ti_ = jnp.where(at, ei + pi_, ti_)
    edge = row == ((SUBLANES - 1) if reverse else 0)
    cr = jnp.where(edge, 0.0, pltpu.roll(tr_, shift, axis=0))
    ci = jnp.where(edge, 0.0, pltpu.roll(ti_, shift, axis=0))

    def fix_step(k, carry):
        pr, pi = carry
        pr, pi = _cmul(lr, li, pr, pi)
        r = rows(k)
        ar_, ai_ = _cmul(pr, pi, cr, ci)
        hr_ref[r, :] = hr_ref[r, :] + ar_
        hi_ref[r, :] = hi_ref[r, :] + ai_
        return pr, pi

    lax.fori_loop(0, nt, fix_step, (one, zero), unroll=unroll)


def _dot(a, b):
    return jnp.dot(a.astype(BF16), b.astype(BF16), preferred_element_type=F32)


def _s5_forward(u_seg, mb_re, mb_im, mc_re, mc_im, lam_re, lam_im, dvec, name):
    l = u_seg.shape[0]
    nb, kin, kst = mb_re.shape

    def body(u_ref, mbr_ref, mbi_ref, mcr_ref, mci_ref, lr_ref, li_ref, d_ref, hr_ref, hi_ref, y_ref):
        u = u_ref[...]
        hr_ref[...] = _dot(u, mbr_ref[...])
        hi_ref[...] = _dot(u, mbi_ref[...])
        _scan_in_place(hr_ref, hi_ref, jnp.broadcast_to(lr_ref[...], (SUBLANES, kst)),
                       jnp.broadcast_to(li_ref[...], (SUBLANES, kst)), False)
        y_ref[...] = (_dot(hr_ref[...], mcr_ref[...]) - _dot(hi_ref[...], mci_ref[...])
                      + d_ref[...] * u.astype(F32))

    act = pl.BlockSpec((l, kin), lambda b: (0, b))
    state = pl.BlockSpec((l, kst), lambda b: (0, b))
    up = pl.BlockSpec((None, kin, kst), lambda b: (b, 0, 0))
    down = pl.BlockSpec((None, kst, kin), lambda b: (b, 0, 0))
    hshape = jax.ShapeDtypeStruct((l, nb * kst), F32)
    return pl.pallas_call(
        body, name=name, grid=(nb,),
        in_specs=[act, up, up, down, down, pl.BlockSpec((1, kst), lambda b: (0, b)),
                  pl.BlockSpec((1, kst), lambda b: (0, b)), pl.BlockSpec((1, kin), lambda b: (0, b))],
        out_specs=[state, state, act],
        out_shape=[hshape, hshape, jax.ShapeDtypeStruct((l, nb * kin), F32)],
        compiler_params=_params("parallel"),
    )(u_seg, mb_re, mb_im, mc_re, mc_im, lam_re, lam_im, dvec)


def _dot_ta(a, b):
    return lax.dot_general(a.astype(BF16), b.astype(BF16), (((0,), (0,)), ((), ())), preferred_element_type=F32)


def _s5_backward(dy_seg, u_seg, h_re, h_im, mg_re, mg_im, md_re, md_im, lam_re, lam_im_neg, dvec, name):
    l = dy_seg.shape[0]
    nb, kin, kst = mg_re.shape
    nt = l // SUBLANES

    def body(dy_ref, u_ref, hr_ref, hi_ref, mgr_ref, mgi_ref, mdr_ref, mdi_ref, lr_ref, li_ref,
             d_ref, du_ref, gcr_ref, gci_ref, gbr_ref, gbi_ref, glr_ref, gli_ref, dsk_ref, qr_ref, qi_ref):
        dy = dy_ref[...]
        qr_ref[...] = _dot(dy, mgr_ref[...])
        qi_ref[...] = _dot(dy, mgi_ref[...])
        _scan_in_place(qr_ref, qi_ref, jnp.broadcast_to(lr_ref[...], (SUBLANES, kst)),
                       jnp.broadcast_to(li_ref[...], (SUBLANES, kst)), True)
        du_ref[...] = _dot(qr_ref[...], mdr_ref[...]) - _dot(qi_ref[...], mdi_ref[...]) + d_ref[...] * dy
        dsk_ref[...] = jnp.sum(dy * _f32(u_ref), axis=0, keepdims=True)
        gcr_ref[...] = _dot_ta(dy, hr_ref[...])
        gci_ref[...] = _dot_ta(dy, hi_ref[...])
        gbr_ref[...] = _dot_ta(u_ref[...], qr_ref[...])
        gbi_ref[...] = _dot_ta(u_ref[...], qi_ref[...])

        row = lax.broadcasted_iota(jnp.int32, (SUBLANES, kst), 0)
        last = pl.ds((nt - 1) * SUBLANES, SUBLANES)
        first = pl.ds(0, SUBLANES)
        pr = jnp.where(row == 0, 0.0, pltpu.roll(hr_ref[last, :], 1, axis=0))
        pi = jnp.where(row == 0, 0.0, pltpu.roll(hi_ref[last, :], 1, axis=0))
        gr, gi = qr_ref[first, :], qi_ref[first, :]

        def step(t, carry):
            acc_r, acc_i = carry
            cur = pl.ds(pl.multiple_of(t * SUBLANES, SUBLANES), SUBLANES)
            prev = pl.ds(pl.multiple_of((t - 1) * SUBLANES, SUBLANES), SUBLANES)
            gr, gi = qr_ref[cur, :], qi_ref[cur, :]
            pr, pi = hr_ref[prev, :], hi_ref[prev, :]
            return acc_r + gr * pr + gi * pi, acc_i + gi * pr - gr * pi

        acc_r, acc_i = lax.fori_loop(1, nt, step, (gr * pr + gi * pi, gi * pr - gr * pi))
        glr_ref[...] = jnp.sum(acc_r, axis=0, keepdims=True)
        gli_ref[...] = jnp.sum(acc_i, axis=0, keepdims=True)

    act = pl.BlockSpec((l, kin), lambda b: (0, b))
    state = pl.BlockSpec((l, kst), lambda b: (0, b))
    up = pl.BlockSpec((None, kin, kst), lambda b: (b, 0, 0))
    down = pl.BlockSpec((None, kst, kin), lambda b: (b, 0, 0))
    vec_st = pl.BlockSpec((1, kst), lambda b: (0, b))
    vec_in = pl.BlockSpec((1, kin), lambda b: (0, b))
    outer = jax.ShapeDtypeStruct((nb, kin, kst), F32)
    lam_shape = jax.ShapeDtypeStruct((1, nb * kst), F32)
    return pl.pallas_call(
        body, name=name, grid=(nb,),
        in_specs=[act, act, state, state, up, up, down, down, vec_st, vec_st, vec_in],
        out_specs=[act, up, up, up, up, vec_st, vec_st, vec_in],
        out_shape=[jax.ShapeDtypeStruct((l, nb * kin), F32), outer, outer, outer, outer, lam_shape, lam_shape,
                   jax.ShapeDtypeStruct((1, nb * kin), F32)],
        scratch_shapes=[pltpu.VMEM((l, kst), F32), pltpu.VMEM((l, kst), F32)],
        compiler_params=_params("parallel"),
    )(dy_seg, u_seg, h_re, h_im, mg_re, mg_im, md_re, md_im, lam_re, lam_im_neg, dvec)


def _block_diag(m, nb):
    g, r, s = m.shape
    gb = g // nb
    eye = jnp.eye(gb, dtype=m.dtype)
    out = m.reshape(nb, gb, r, 1, s) * eye[None, :, None, :, None]
    return out.reshape(nb, gb * r, gb * s)


def _block_diag_extract(mat, g, r, s):
    nb = mat.shape[0]
    gb = g // nb
    eye = jnp.eye(gb, dtype=mat.dtype)
    m5 = mat.reshape(nb, gb, r, gb, s) * eye[None, :, None, :, None]
    return jnp.sum(m5, axis=3).reshape(g, r, s)


def _adamw(w, m, v, gslots, name, layer=0, prev=None, deps=()):
    layers, r, c = w.shape
    s = gslots.shape[0]
    tr = _tile(r, max(SUBLANES, 1 << int(math.log2(ADAMW_BLOCK_ELEMS // c))))
    bc1 = 1.0 / (1.0 - ADAM_B1 ** ADAM_STEP)
    bc2 = 1.0 / (1.0 - ADAM_B2 ** ADAM_STEP)

    def body(w_ref, m_ref, v_ref, g_ref, *rest):
        go_ref, d_ref, mo_ref, vo_ref = rest[-4:]
        g = g_ref[0].astype(F32)
        for k in range(1, s):
            g = g + g_ref[k].astype(F32)
        mn = ADAM_B1 * m_ref[...] + (1.0 - ADAM_B1) * g
        vn = ADAM_B2 * v_ref[...] + (1.0 - ADAM_B2) * (g * g)
        go_ref[...] = g
        mo_ref[...] = mn
        vo_ref[...] = vn
        d_ref[...] = -ADAM_LR * ((mn * bc1) / (jnp.sqrt(vn * bc2) + ADAM_EPS) + ADAM_WD * w_ref[...])

    spec = pl.BlockSpec((None, tr, c), lambda i: (layer, i, 0))
    out = jax.ShapeDtypeStruct((layers, r, c), F32)
    in_specs = [spec, spec, spec, pl.BlockSpec((s, tr, c), lambda i: (0, i, 0))]
    args = [w, m, v, gslots]
    aliases = {}
    if prev is not None:
        in_specs += [ANY_SPEC] * 4
        args += list(prev)
        aliases = {4 + q: q for q in range(4)}
    in_specs += [ANY_SPEC] * len(deps)
    args += list(deps)
    return pl.pallas_call(
        body, name=name, grid=(r // tr,), in_specs=in_specs,
        out_specs=[spec] * 4, out_shape=[out] * 4, input_output_aliases=aliases,
        compiler_params=_params("parallel"),
    )(*args)


def _pack(parts):
    flat = jnp.concatenate([p.reshape(-1) for p in parts])
    pad = (-flat.shape[0]) % (PACK_ROWS * LANES)
    return jnp.pad(flat, (0, pad)).reshape(-1, LANES)


def _unpack(packed, shapes):
    flat = packed.reshape(-1)
    out, off = [], 0
    for shp in shapes:
        size = math.prod(shp)
        out.append(flat[off:off + size].reshape(shp))
        off += size
    return out


def kernel(x, norm_g, w_in, conv_w, w_out_a, a_re, a_im, log_dt, b_re, b_im, c_re, c_im, d_skip, w_glu, b_glu, w_out_b, w_o, final_g, loss_target, m_norm_g, m_w_in, m_conv_w, m_w_out_a, m_a_re, m_a_im, m_log_dt, m_b_re, m_b_im, m_c_re, m_c_im, m_d_skip, m_w_glu, m_b_glu, m_w_out_b, m_w_o, m_final_g, v_norm_g, v_w_in, v_conv_w, v_w_out_a, v_a_re, v_a_im, v_log_dt, v_b_re, v_b_im, v_c_re, v_c_im, v_d_skip, v_w_glu, v_b_glu, v_w_out_b, v_w_o, v_final_g):
    depth = norm_g.shape[0]
    l, d = x.shape[1], x.shape[2]
    ws = w_glu.shape[2]
    n_groups, n_state = a_re.shape[1], a_re.shape[2]
    nb = ws // LANES
    assert S5_GROUP == b_re.shape[3] and n_state * S5_GB == 4 * LANES
    u_col, zb_col = 4 * d // ws, 4 * d // ws + 1
    ga_col, gb_col = (4 * d + 2 * ws) // d, (4 * d + 2 * ws) // d + 1
    me = 4 * lax.axis_index("x") + 2 * lax.axis_index("y") + lax.axis_index("c")

    xs = [x[0]]
    tgt = loss_target[0]

    big_names = ("w_in", "w_out_a", "w_glu", "w_out_b", "w_o")
    big = dict(w_in=(w_in, m_w_in, v_w_in), w_out_a=(w_out_a, m_w_out_a, v_w_out_a),
               w_glu=(w_glu, m_w_glu, v_w_glu), w_out_b=(w_out_b, m_w_out_b, v_w_out_b),
               w_o=(w_o, m_w_o, v_w_o))

    all_shards = [[big[k][0][i].astype(BF16) for k in big_names] for i in range(depth)]

    def shards_bf16(i):
        return all_shards[i]

    main_names = ("a_re", "a_im", "log_dt", "b_re", "b_im", "c_re", "c_im", "d_skip", "b_glu", "conv_w")
    small_w = dict(a_re=(a_re, m_a_re, v_a_re), a_im=(a_im, m_a_im, v_a_im),
                   log_dt=(log_dt, m_log_dt, v_log_dt), b_re=(b_re, m_b_re, v_b_re), b_im=(b_im, m_b_im, v_b_im),
                   c_re=(c_re, m_c_re, v_c_re), c_im=(c_im, m_c_im, v_c_im), d_skip=(d_skip, m_d_skip, v_d_skip),
                   b_glu=(b_glu, m_b_glu, v_b_glu))
    main_shapes = [small_w[k][0].shape for k in main_names[:-1]] + [(depth, 3, d)]
    zeros_conv = jnp.zeros((depth, 3, d), F32)
    main_wmv = [_pack([small_w[k][q] for k in main_names[:-1]] + [zeros_conv])[None] for q in range(3)]
    zero1 = jnp.zeros((1,), F32)
    gains_wmv = [_pack([g_, f_, zero1])[None]
                 for g_, f_ in ((norm_g, final_g), (m_norm_g, m_final_g), (v_norm_g, v_final_g))]
    dc = d // N_DEV
    pad8 = lambda a: jnp.pad(a.reshape(depth * 3, dc), ((0, SUBLANES - depth * 3), (0, 0)))[None]
    conv_wmv = [pad8(conv_w), pad8(m_conv_w), pad8(v_conv_w)]

    def gather_start(shards, name, deps=()):
        sems, srcs, lands, token = _exchange_start(
            _plan_gather_chips, shards, [_landing(s_, N_DEV, me) for s_ in shards], f"{name}_start", deps)
        return (name, sems, srcs, lands), token

    def gather_forward(state, after, deps=()):
        name, sems, srcs, lands = state
        _, lands = _exchange_wait(_plan_gather_chips, sems, srcs, lands, after, f"{name}_wait")
        sems, _, lands, token = _exchange_start(_plan_gather_forward, None, lands, f"{name}_forward_start", deps)
        return (name, sems, lands), token

    def gather_finish(state, after):
        name, sems, lands = state
        return _exchange_wait(_plan_gather_forward, sems, None, lands, after, f"{name}_forward_wait")[1]

    conv_shard = jnp.pad(conv_w.reshape(depth * 3, -1), ((0, SUBLANES - depth * 3), (0, 0)))
    w_in_state, token = gather_start([shards_bf16(0)[0], conv_shard], "ag_w_in_0")
    s5 = []
    shape3 = (n_groups, n_state, S5_GROUP)
    for i in range(depth):
        dense_in = (_dense(a_re[i][:, :, None], shape3), _dense(a_im[i][:, :, None], shape3),
                    _dense(log_dt[i][:, None, None], shape3), b_re[i].reshape(-1, LANES), b_im[i].reshape(-1, LANES))
        lbr, lbi, bbr, bbi = _s5_params(*dense_in, f"s5_params_{i}", deps=(token,))
        bbr3, bbi3 = bbr.reshape(shape3), bbi.reshape(shape3)
        diag = lambda m: _block_diag(m, nb).astype(BF16)
        s5.append(dict(
            dense_in=dense_in,
            lam_re=lbr.reshape(shape3)[:, :, 0].reshape(1, -1), lam_im=lbi.reshape(shape3)[:, :, 0].reshape(1, -1),
            up=(diag(bbr3.transpose(0, 2, 1)), diag(bbi3.transpose(0, 2, 1))),
            down=(diag(c_re[i].transpose(0, 2, 1)), diag(c_im[i].transpose(0, 2, 1))),
            up_bwd=(diag(c_re[i]), diag(-c_im[i])), down_bwd=(diag(bbr3), diag(-bbi3))))
    prologue = [m for p in s5 for k in ("up", "down", "up_bwd", "down_bwd") for m in p[k]]
    prologue += [p[k] for p in s5 for k in ("lam_re", "lam_im")]
    prologue += main_wmv + gains_wmv + conv_wmv + all_shards[0][1:] + [s_ for sh in all_shards[1:] for s_ in sh]
    w_in_state, token = gather_forward(w_in_state, prologue)
    rest_state, token = gather_start(shards_bf16(0)[1:], "ag_rest_0", deps=(token,))
    next_state = None
    if depth > 1:
        next_state, token = gather_start([shards_bf16(1)[0]], "ag_w_in_1", deps=(token,))

    saved = []
    wg = [None] * depth
    conv_full = None
    for i in range(depth):
        xi = xs[-1]
        h = _rmsnorm_fwd(xi, norm_g[i], f"rmsnorm_fwd_{i}", deps=(token,))
        arrived = gather_finish(w_in_state, [h])
        if i == 0:
            conv_full = arrived[1].transpose(1, 0, 2).reshape(SUBLANES, d)[:depth * 3].reshape(depth, 3, d)
        conv8 = jnp.pad(conv_full[i], ((0, SUBLANES - 3), (0, 0)))
        proj = _mm_win_fwd(h, arrived[0], f"mm_proj_{i}")
        u_seg = _to_segments(proj[:, 4 * d:4 * d + ws])
        h_re, h_im, y_seg = _s5_forward(u_seg, *s5[i]["up"], *s5[i]["down"], s5[i]["lam_re"], s5[i]["lam_im"],
                                        d_skip[i].reshape(1, ws), f"s5_forward_{i}")
        rest_state, token = gather_forward(rest_state, [y_seg])
        pa = _branch_a_fwd(proj, conv8, d, f"branch_a_fwd_{i}", deps=(token,))
        rest = gather_finish(rest_state, [pa])
        wg[i] = g = dict(w_in=arrived[0], w_a=rest[0].reshape(d, d), w_glu=rest[1].reshape(ws, ws),
                         w_b=rest[2], w_o=rest[3].reshape(d, d))
        ya = _mm(pa, g["w_a"], name=f"mm_ya_{i}")
        y = _from_segments(y_seg)
        yg = _gelu_cast(y, f"gelu_{i}")
        gl = _mm(yg, g["w_glu"], name=f"mm_glu_{i}")
        pb = _glu_post(y, gl, proj, b_glu[i], zb_col, f"glu_post_{i}")
        w_b2d = g["w_b"].transpose(1, 0, 2).reshape(ws, d)
        yb = _mm(pb, w_b2d, name=f"mm_yb_{i}")
        mrg = _merge_fwd(proj, ya, yb, d, ga_col, gb_col, f"merge_fwd_{i}")
        deps = ()
        if i + 1 < depth:
            w_in_state, token = gather_forward(next_state, [mrg])
            rest_state, token = gather_start(shards_bf16(i + 1)[1:], f"ag_rest_{i + 1}", deps=(token,))
            if i + 2 < depth:
                next_state, token = gather_start([shards_bf16(i + 2)[0]], f"ag_w_in_{i + 2}", deps=(token,))
            deps = (token,)
        xs.append(_mm(mrg, g["w_o"], name=f"mm_out_{i}", add=xi, deps=deps))
        saved.append(dict(h=h, proj=proj, pa=pa, ya=ya, yb=yb, y=y, yg=yg, gl=gl, pb=pb, mrg=mrg,
                          u_seg=u_seg, h_re=h_re, h_im=h_im, conv8=conv8, w_b2d=w_b2d))

    dx, g_final, loss_part = _final_loss(xs[-1], final_g, tgt, "final_loss")

    rs_pending = []
    small = {k: [None] * depth for k in ("norm_g", "a_re", "a_im", "log_dt", "b_re", "b_im", "c_re", "c_im",
                                         "d_skip", "b_glu", "conv_w")}

    my_chip = 2 * lax.axis_index("x") + lax.axis_index("y")

    def reduce_on_chip(pieces, tag):
        lands = [lax.empty((4,) + p.shape[1:], p.dtype) for p in pieces]
        sems, srcs, lands, token = _exchange_start(_plan_reduce_sibling, pieces, lands, f"rs_sibling_start_{tag}")
        return (sems, srcs, lands), token

    def reduce_across_chips(names_, state, layer, tag, after):
        sems, srcs, lands = state
        srcs, lands = _exchange_wait(_plan_reduce_sibling, sems, srcs, lands, after, f"rs_sibling_wait_{tag}")
        sums = [_chip_sums(p, l_, f"chip_sum_{k}_{layer}") for k, p, l_ in zip(names_, srcs, lands)]
        lands = [_landing(lax.dynamic_index_in_dim(s_, my_chip, 0, keepdims=False), 4, my_chip) for s_ in sums]
        sems, srcs, lands, token = _exchange_start(_plan_reduce_chips, sums, lands, f"rs_chips_start_{tag}")
        rs_pending.append((names_, layer, sems, srcs, lands, f"rs_chips_wait_{tag}"))
        return token

    for i in reversed(range(depth)):
        s, g = saved[i], wg[i]
        proj = s["proj"]
        dxo_b = dx.astype(BF16)
        dm = _mm(dxo_b, g["w_o"], name=f"mm_dm_{i}", nt=True)
        gw_o = _mm(s["mrg"], dxo_b, ta=True, name=f"mm_gw_o_{i}", out_dtype=BF16)
        dy2, dproj = _merge_bwd(proj, s["ya"], s["yb"], dm, d, ga_col, f"merge_bwd_{i}")
        dya, dyb = dy2[0], dy2[1]
        dpa = _mm(dya, g["w_a"], name=f"mm_dpa_{i}", nt=True)
        gw_a = _mm(s["pa"], dya, ta=True, name=f"mm_gw_a_{i}", out_dtype=BF16)
        dpb = _mm(dyb, s["w_b2d"], name=f"mm_dpb_{i}", nt=True)
        gw_b = _mm(s["pb"], dyb, ta=True, name=f"mm_gw_b_{i}", split_n=N_DEV, out_dtype=BF16)
        dproj, dw0, dw1, dw2 = _branch_a_bwd(proj, dpa, s["conv8"], dproj, d, f"branch_a_bwd_{i}")
        small["conv_w"][i] = jnp.concatenate([dw0, dw1, dw2], axis=0)
        dproj, dgl, t1, db_glu = _glu_bwd1(s["y"], s["gl"], proj, b_glu[i], dpb, dproj, zb_col, f"glu_bwd1_{i}")
        small["b_glu"][i] = db_glu.reshape(ws)
        dyg2 = _mm(dgl, g["w_glu"], name=f"mm_dyg_{i}", nt=True)
        gw_glu = _mm(s["yg"], dgl, ta=True, name=f"mm_gw_glu_{i}", out_dtype=BF16)
        small_names_ = ("w_out_a", "w_glu", "w_out_b", "w_o")
        state, token = reduce_on_chip(
            [gw_a.reshape(N_DEV, d // N_DEV, d), gw_glu.reshape(N_DEV, ws // N_DEV, ws), gw_b,
             gw_o.reshape(N_DEV, d // N_DEV, d)], f"small_{i}")
        dy = _glu_bwd2(s["y"], t1, dyg2, f"glu_bwd2_{i}", deps=(token,))
        dy_seg = _to_segments(dy)
        u_seg = s["u_seg"]
        du_seg, gc_re, gc_im, gbb_re, gbb_im, glam_re, glam_im, dskip = _s5_backward(
            dy_seg, u_seg, s["h_re"], s["h_im"], *s5[i]["up_bwd"], *s5[i]["down_bwd"],
            s5[i]["lam_re"], -s5[i]["lam_im"], d_skip[i].reshape(1, ws), f"s5_backward_{i}")
        token = reduce_across_chips(small_names_, state, i, f"small_{i}", [du_seg])
        dproj = _write_cols(dproj, _from_segments(du_seg), u_col, f"write_du_{i}")
        gw_in = _mm(s["h"], dproj, ta=True, name=f"mm_gw_in_{i}", split_n=N_DEV, tm=1024,
                    out_dtype=BF16, deps=(token,))
        state, token = reduce_on_chip([gw_in], f"w_in_{i}")
        small["d_skip"][i] = dskip.reshape(n_groups, S5_GROUP)
        small["c_re"][i] = _block_diag_extract(gc_re, n_groups, S5_GROUP, n_state)
        small["c_im"][i] = -_block_diag_extract(gc_im, n_groups, S5_GROUP, n_state)
        gbb_re = _block_diag_extract(gbb_re, n_groups, S5_GROUP, n_state).transpose(0, 2, 1)
        gbb_im = _block_diag_extract(gbb_im, n_groups, S5_GROUP, n_state).transpose(0, 2, 1)
        gar, gai, gdt, gbr, gbi = _s5_params_bwd(
            *s5[i]["dense_in"], _dense(glam_re.reshape(n_groups, n_state, 1), shape3),
            _dense(glam_im.reshape(n_groups, n_state, 1), shape3),
            gbb_re.reshape(-1, LANES), gbb_im.reshape(-1, LANES), n_groups, f"s5_params_bwd_{i}", deps=(token,))
        small["a_re"][i] = gar.reshape(shape3)[:, :, 0]
        small["a_im"][i] = gai.reshape(shape3)[:, :, 0]
        small["log_dt"][i] = gdt[:, 0]
        small["b_re"][i] = gbr.reshape(shape3)
        small["b_im"][i] = gbi.reshape(shape3)
        if i == 0:
            part = {k: jnp.stack(small[k]) for k in main_names}
            main_state, token = gather_start([_pack([part[k] for k in main_names]).astype(BF16)], "ag_small")
            token = reduce_across_chips(("w_in",), state, i, f"w_in_{i}", [token])
            main_state, token = gather_forward(main_state, [token])
            dh = _mm_win_bwd(dproj, g["w_in"], f"mm_dh_{i}", deps=(token,))
            main_slots = gather_finish(main_state, [dh])[0]
            deps = ()
        else:
            dh = _mm_win_bwd(dproj, g["w_in"], f"mm_dh_{i}", deps=(gar,))
            deps = (reduce_across_chips(("w_in",), state, i, f"w_in_{i}", [dh]),)
        dx, dng = _rmsnorm_bwd(xs[i], norm_g[i], dh, dx, f"rmsnorm_bwd_{i}", deps=deps)
        small["norm_g"][i] = dng.reshape(d)

    results = {}

    gain_grads = jnp.concatenate([jnp.stack(small["norm_g"]).reshape(-1), g_final.reshape(d)])
    gain_shapes = [(depth, d), (d,), (1,)]
    gains_state, token = gather_start([_pack([gain_grads, loss_part[0, :1]])], "ag_gains")

    sres = [_unpack(p[0], main_shapes) for p in _adamw(*main_wmv, main_slots, "adamw_small", deps=(token,))]
    for j, k in enumerate(main_names[:-1]):
        results[k] = [sres[q][j] for q in range(4)]
    gconv = lax.dynamic_slice_in_dim(sres[0][-1], me * dc, dc, axis=2)
    cres = _adamw(*conv_wmv, pad8(gconv), "adamw_conv_w")
    results["conv_w"] = [r_[0, :depth * 3].reshape(depth, 3, dc) for r_ in cres]

    after = [cres[0]]
    for names_, layer, sems, srcs, lands, wait_name in rs_pending:
        _, slots = _exchange_wait(_plan_reduce_chips, sems, srcs, lands, after, wait_name)
        for k, land in zip(names_, slots):
            w_, m_, v_ = big[k]
            results[k] = _adamw(w_, m_, v_, land, f"adamw_{k}_{layer}", layer=layer, prev=results.get(k))
            after = [results[k][0]]

    gains_state, token = gather_forward(gains_state, after)
    gpack = gather_finish(gains_state, [token])[0]
    gres = [_unpack(p[0], gain_shapes) for p in _adamw(*gains_wmv, gpack, "adamw_gains")]
    results["norm_g"] = [gres[q][0] for q in range(4)]
    results["final_g"] = [gres[q][1] for q in range(4)]
    loss = gres[0][2][0]

    names = ("norm_g", "w_in", "conv_w", "w_out_a", "a_re", "a_im", "log_dt", "b_re", "b_im", "c_re", "c_im",
             "d_skip", "w_glu", "b_glu", "w_out_b", "w_o", "final_g")
    outs = [loss, dx[None]]
    for q in range(4):
        outs += [results[k][q] for k in names]
    return tuple(outs)
```

```python
import functools
import math

import jax
import jax.numpy as jnp
from jax import lax
from jax.experimental import pallas as pl
from jax.experimental.pallas import tpu as pltpu

F32 = jnp.float32
BF16 = jnp.bfloat16
HIGHEST = lax.Precision.HIGHEST

N_DEV = 8
LANES = 128
SUBLANES = 8
VMEM_LIMIT_BYTES = 56 * 1024 * 1024

RMS_EPS = 1e-6
ADAM_LR = 0.001
ADAM_B1 = 0.9
ADAM_B2 = 0.999
ADAM_EPS = 1e-08
ADAM_WD = 0.01
ADAM_STEP = 10
GELU_C0 = math.sqrt(2.0 / math.pi)
GELU_C1 = 0.044715

ADAMW_BLOCK_ELEMS = 1 << 17
PACK_ROWS = 512

S5_GROUP = 16
S5_GB = LANES // S5_GROUP


def _params(*semantics):
    return pltpu.CompilerParams(dimension_semantics=semantics, vmem_limit_bytes=VMEM_LIMIT_BYTES)


ANY_SPEC = pl.BlockSpec(memory_space=pl.ANY)


def _pallas(body, args, deps=(), *, in_specs, **kwargs):
    deps = tuple(deps)
    if not deps:
        return pl.pallas_call(body, in_specs=in_specs, **kwargs)(*args)

    def body_after(*refs):
        body(*refs[len(deps):])

    return pl.pallas_call(body_after, in_specs=[ANY_SPEC] * len(deps) + list(in_specs), **kwargs)(*deps, *args)


def _tile(n, pref):
    t = min(n, pref)
    while n % t:
        assert t % 2 == 0, (n, pref)
        t //= 2
    return t


def _sigmoid(z):
    return 1.0 / (1.0 + jnp.exp(-z))


def _gelu(y):
    return 0.5 * y * (1.0 + jnp.tanh(GELU_C0 * (y + GELU_C1 * y * y * y)))


def _gelu_grad(y):
    t = jnp.tanh(GELU_C0 * (y + GELU_C1 * y * y * y))
    return 0.5 * (1.0 + t) + 0.5 * y * (1.0 - t * t) * GELU_C0 * (1.0 + 3.0 * GELU_C1 * y * y)


HBM_SPEC = pl.BlockSpec(memory_space=pltpu.HBM)
SEM_SPEC = pl.BlockSpec(memory_space=pltpu.SEMAPHORE)
DATAFLOW_EFFECT = pltpu.SideEffectType.DATAFLOW_SIDE_EFFECTING
OTHER_CHIPS = (2, 4, 6)


def _flip(pos, mask):
    x, y, c = pos
    return x ^ ((mask >> 2) & 1), y ^ ((mask >> 1) & 1), c ^ (mask & 1)


def _dev(pos):
    return 4 * pos[0] + 2 * pos[1] + pos[2]


def _chip(pos):
    return 2 * pos[0] + pos[1]


def _plan_gather_chips(me):
    return [(_flip(me, k), None, _dev(me), _dev(_flip(me, k))) for k in (1,) + OTHER_CHIPS]


def _plan_gather_forward(me):
    sib = _flip(me, 1)
    return [(sib, _dev(_flip(me, k)), _dev(_flip(me, k)), _dev(_flip(sib, k))) for k in OTHER_CHIPS]


def _plan_reduce_sibling(me):
    sib = _flip(me, 1)
    return [(sib, 2 * q + sib[2], q, q) for q in range(4)]


def _plan_reduce_chips(me):
    return [(_flip(me, k), _chip(_flip(me, k)), _chip(me), _chip(_flip(me, k))) for k in OTHER_CHIPS]


PLAN_COPIES = {_plan_gather_chips: 4, _plan_gather_forward: 3, _plan_reduce_sibling: 4, _plan_reduce_chips: 3}


def _exchange_copies(plan, src_refs, land_refs, send_sems, recv_sems, incoming=True):
    me = (lax.axis_index("x"), lax.axis_index("y"), lax.axis_index("c"))
    pairs = []
    for b, (src_ref, land_ref) in enumerate(zip(src_refs, land_refs)):
        for j, (peer, src_slot, there, here) in enumerate(plan(me)):
            sem = b * PLAN_COPIES[plan] + j
            src = src_ref if src_slot is None else src_ref.at[src_slot]
            out = pltpu.make_async_remote_copy(
                src_ref=src, dst_ref=land_ref.at[there], send_sem=send_sems.at[sem], recv_sem=recv_sems.at[sem],
                device_id=peer, device_id_type=pl.DeviceIdType.MESH)
            inc = pltpu.make_async_remote_copy(
                src_ref=src, dst_ref=land_ref.at[here], send_sem=send_sems.at[sem], recv_sem=recv_sems.at[sem],
                device_id=peer, device_id_type=pl.DeviceIdType.MESH) if incoming else None
            pairs.append((out, inc))
    return pairs


def _exchange_start(plan, srcs, lands, name, deps=()):
    srcs = [] if srcs is None else list(srcs)
    ns, n, nd = len(srcs), len(lands), len(deps)

    def body(*refs):
        land_refs = refs[ns:ns + n]
        sems_at = ns + n + nd
        pairs = _exchange_copies(plan, refs[:ns] if ns else land_refs, land_refs, refs[sems_at], refs[sems_at + 1],
                                 incoming=False)
        for out, _ in pairs:
            out.start()
        token = refs[-1]
        token[...] = jnp.zeros_like(token)

    sems = pltpu.SemaphoreType.DMA((PLAN_COPIES[plan] * n,))
    bufs = srcs + list(lands)
    outs = pl.pallas_call(
        body, name=name,
        out_shape=(sems, sems, *[pltpu.HBM(a.shape, a.dtype) for a in bufs],
                   jax.ShapeDtypeStruct((SUBLANES, LANES), F32)),
        in_specs=[HBM_SPEC] * (ns + n) + [ANY_SPEC] * nd,
        out_specs=(SEM_SPEC, SEM_SPEC, *[HBM_SPEC] * (ns + n), pl.BlockSpec(memory_space=pltpu.VMEM)),
        input_output_aliases={i: 2 + i for i in range(ns + n)},
        compiler_params=pltpu.CompilerParams(has_side_effects=DATAFLOW_EFFECT),
    )(*[pltpu.with_memory_space_constraint(a, pltpu.HBM) for a in bufs], *deps)
    return (outs[0], outs[1]), (outs[2:2 + ns] if ns else None), outs[2 + ns:2 + ns + n], outs[-1]


def _exchange_wait(plan, sems, srcs, lands, after, name):
    srcs = [] if srcs is None else list(srcs)
    ns, n = len(srcs), len(lands)

    def body(*refs):
        land_refs = refs[ns:ns + n]
        pairs = _exchange_copies(plan, refs[:ns] if ns else land_refs, land_refs, refs[ns + n], refs[ns + n + 1])
        for out, inc in pairs:
            out.wait_send()
            inc.wait_recv()

    bufs = srcs + list(lands)
    outs = pl.pallas_call(
        body, name=name,
        out_shape=[pltpu.HBM(a.shape, a.dtype) for a in bufs],
        in_specs=[HBM_SPEC] * (ns + n) + [SEM_SPEC, SEM_SPEC] + [ANY_SPEC] * len(after),
        out_specs=[HBM_SPEC] * (ns + n),
        input_output_aliases={i: i for i in range(ns + n)},
        compiler_params=pltpu.CompilerParams(has_side_effects=DATAFLOW_EFFECT),
    )(*bufs, sems[0], sems[1], *after)
    return outs[:ns], outs[ns:]


def _landing(own, slots, slot):
    land = lax.empty((slots,) + own.shape, own.dtype)
    return lax.dynamic_update_slice(land, own[None], (slot,) + (0,) * own.ndim)


def _chip_sums(pieces, land, name):
    _, r, c_ = land.shape
    tr = _tile(r, max(2 * SUBLANES, 1 << int(math.log2(4 * ADAMW_BLOCK_ELEMS // c_))))

    def body(core_ref, p_ref, l_ref, o_ref):
        o_ref[...] = (p_ref[...].astype(F32) + l_ref[...].astype(F32)).astype(o_ref.dtype)

    spec = pl.BlockSpec((None, tr, c_), lambda q, i, core: (q, i, 0))
    return pl.pallas_call(
        body, name=name,
        grid_spec=pltpu.PrefetchScalarGridSpec(
            num_scalar_prefetch=1, grid=(4, r // tr),
            in_specs=[pl.BlockSpec((None, tr, c_), lambda q, i, core: (2 * q + core[0], i, 0)), spec],
            out_specs=spec),
        out_shape=jax.ShapeDtypeStruct(land.shape, land.dtype),
        compiler_params=_params("parallel", "parallel"),
    )(lax.axis_index("c").reshape(1), pieces, land)


def _mm(a, b, *, name, nt=False, ta=False, out_dtype=F32, add=None, split_n=None, tm=512, tn=1024, deps=()):
    k, m = a.shape if ta else a.shape[::-1]
    n = b.shape[0] if nt else b.shape[1]
    tm = _tile(m, tm)
    tn = n // split_n if split_n else _tile(n, tn)
    dims = (((0 if ta else 1,), (1 if nt else 0,)), ((), ()))

    def body(*refs):
        a_ref, b_ref = refs[0], refs[1]
        o_ref = refs[-1]
        acc = lax.dot_general(a_ref[...], b_ref[...], dims, preferred_element_type=F32)
        if add is not None:
            acc = acc + refs[2][...]
        o_ref[...] = acc.astype(o_ref.dtype)

    in_specs = [pl.BlockSpec((k, tm), lambda i, j: (0, i)) if ta else pl.BlockSpec((tm, k), lambda i, j: (i, 0)),
                pl.BlockSpec((tn, k), lambda i, j: (j, 0)) if nt
                else pl.BlockSpec((k, tn), lambda i, j: (0, j))]
    args = [a, b]
    if add is not None:
        in_specs.append(pl.BlockSpec((tm, tn), lambda i, j: (i, j)))
        args.append(add)
    if split_n:
        out_shape = jax.ShapeDtypeStruct((split_n, m, tn), out_dtype)
        out_spec = pl.BlockSpec((None, tm, tn), lambda i, j: (j, i, 0))
    else:
        out_shape = jax.ShapeDtypeStruct((m, n), out_dtype)
        out_spec = pl.BlockSpec((tm, tn), lambda i, j: (i, j))
    return _pallas(
        body, args, deps, name=name, grid=(m // tm, n // tn), in_specs=in_specs, out_specs=out_spec,
        out_shape=out_shape, compiler_params=_params("parallel", "parallel"))


def _mm_win_fwd(h, w_g, name, deps=()):
    m, k = h.shape
    nj = w_g.shape[2]
    tm = _tile(m, 512)

    def body(a_ref, b_ref, o_ref):
        o_ref[...] = jnp.dot(a_ref[...], b_ref[...], preferred_element_type=F32).astype(o_ref.dtype)

    return _pallas(
        body, [h, w_g], deps, name=name, grid=(N_DEV, m // tm),
        in_specs=[pl.BlockSpec((tm, k), lambda j, i: (i, 0)),
                  pl.BlockSpec((None, k, nj), lambda j, i: (j, 0, 0))],
        out_specs=pl.BlockSpec((tm, nj), lambda j, i: (i, j)),
        out_shape=jax.ShapeDtypeStruct((m, N_DEV * nj), BF16),
        compiler_params=_params("parallel", "parallel"))


def _mm_win_bwd(dproj, w_g, name, deps=()):
    m = dproj.shape[0]
    d, nj = w_g.shape[1], w_g.shape[2]
    tm = _tile(m, 512)
    tn = _tile(d, 1024)

    per_step = 2

    def body(a_ref, b_ref, o_ref, acc_ref):
        j = pl.program_id(2)

        @pl.when(j == 0)
        def _():
            acc_ref[...] = jnp.zeros_like(acc_ref)

        part = None
        for k in range(per_step):
            term = lax.dot_general(a_ref[:, k * nj:(k + 1) * nj], b_ref[k], (((1,), (1,)), ((), ())),
                                   preferred_element_type=F32)
            part = term if part is None else part + term
        acc_ref[...] += part

        @pl.when(j == N_DEV // per_step - 1)
        def _():
            o_ref[...] = acc_ref[...]

    return _pallas(
        body, [dproj, w_g], deps, name=name, grid=(m // tm, d // tn, N_DEV // per_step),
        in_specs=[pl.BlockSpec((tm, per_step * nj), lambda i, n, j: (i, j)),
                  pl.BlockSpec((per_step, tn, nj), lambda i, n, j: (j, n, 0))],
        out_specs=pl.BlockSpec((tm, tn), lambda i, n, j: (i, n)),
        out_shape=jax.ShapeDtypeStruct((m, d), F32),
        scratch_shapes=[pltpu.VMEM((tm, tn), F32)],
        compiler_params=_params("parallel", "parallel", "arbitrary"))


def _row_spec(tr, w, col):
    return pl.BlockSpec((tr, w), lambda i: (i, col))


def _full_spec(shape):
    return pl.BlockSpec(shape, lambda i: (0,) * len(shape))


def _rmsnorm_fwd(x, g, name, deps=()):
    l, d = x.shape
    tr = _tile(l, 256)

    def body(x_ref, g_ref, o_ref):
        xv = x_ref[...]
        rstd = lax.rsqrt(jnp.mean(xv * xv, axis=-1, keepdims=True) + RMS_EPS)
        o_ref[...] = (xv * rstd * g_ref[...]).astype(o_ref.dtype)

    return _pallas(
        body, [x, g.reshape(1, d)], deps, name=name, grid=(l // tr,),
        in_specs=[_row_spec(tr, d, 0), _full_spec((1, d))],
        out_specs=_row_spec(tr, d, 0),
        out_shape=jax.ShapeDtypeStruct((l, d), BF16),
        compiler_params=_params("parallel"))


def _rmsnorm_bwd(x, g, dh, dxo, name, deps=()):
    l, d = x.shape
    tr = _tile(l, 256)

    def body(x_ref, g_ref, dh_ref, dxo_ref, dx_ref, dg_ref):
        xv = x_ref[...]
        rstd = lax.rsqrt(jnp.mean(xv * xv, axis=-1, keepdims=True) + RMS_EPS)
        dhv = dh_ref[...]
        gdy = dhv * g_ref[...]
        dot = jnp.mean(gdy * xv, axis=-1, keepdims=True)
        dx_ref[...] = dxo_ref[...] + rstd * gdy - xv * (rstd * rstd * rstd * dot)

        @pl.when(pl.program_id(0) == 0)
        def _():
            dg_ref[...] = jnp.zeros_like(dg_ref)

        dg_ref[...] += jnp.sum(dhv * xv * rstd, axis=0, keepdims=True)

    return _pallas(
        body, [x, g.reshape(1, d), dh, dxo], deps, name=name, grid=(l // tr,),
        in_specs=[_row_spec(tr, d, 0), _full_spec((1, d)), _row_spec(tr, d, 0), _row_spec(tr, d, 0)],
        out_specs=[_row_spec(tr, d, 0), _full_spec((1, d))],
        out_shape=[jax.ShapeDtypeStruct((l, d), F32), jax.ShapeDtypeStruct((1, d), F32)],
        compiler_params=_params("arbitrary"))


def _final_loss(x, g, tgt, name):
    l, d = x.shape
    tr = _tile(l, 256)

    def body(x_ref, g_ref, t_ref, dx_ref, dg_ref, loss_ref):
        xv = x_ref[...]
        gv = g_ref[...]
        rstd = lax.rsqrt(jnp.mean(xv * xv, axis=-1, keepdims=True) + RMS_EPS)
        xn = xv * rstd
        err = xn * gv - t_ref[...]
        dy = err * (1.0 / d)
        gdy = dy * gv
        dot = jnp.mean(gdy * xv, axis=-1, keepdims=True)
        dx_ref[...] = rstd * gdy - xv * (rstd * rstd * rstd * dot)

        @pl.when(pl.program_id(0) == 0)
        def _():
            dg_ref[...] = jnp.zeros_like(dg_ref)
            loss_ref[...] = jnp.zeros_like(loss_ref)

        dg_ref[...] += jnp.sum(dy * xn, axis=0, keepdims=True)
        loss_ref[...] += (0.5 / d) * jnp.sum(err * err)

    return pl.pallas_call(
        body, name=name, grid=(l // tr,),
        in_specs=[_row_spec(tr, d, 0), _full_spec((1, d)), _row_spec(tr, d, 0)],
        out_specs=[_row_spec(tr, d, 0), _full_spec((1, d)), _full_spec((SUBLANES, LANES))],
        out_shape=[jax.ShapeDtypeStruct((l, d), F32), jax.ShapeDtypeStruct((1, d), F32),
                   jax.ShapeDtypeStruct((SUBLANES, LANES), F32)],
        compiler_params=_params("arbitrary"),
    )(x, g.reshape(1, d), tgt)


HALO = 2 * SUBLANES


def _halo_spec(tr, w, col, nblk, before):
    step = tr // HALO
    if before:
        return pl.BlockSpec((HALO, w), lambda i: (jnp.maximum(i * step - 1, 0), col))
    return pl.BlockSpec((HALO, w), lambda i: (jnp.minimum((i + 1) * step, nblk - 1), col))


def _shift_down(cur, before, k):
    ext = jnp.concatenate([before, cur], axis=0)
    return pltpu.roll(ext, k, axis=0)[HALO:, :]


def _shift_up(cur, after, k):
    tr = cur.shape[0]
    ext = jnp.concatenate([cur, after], axis=0)
    return pltpu.roll(ext, tr + HALO - k, axis=0)[:tr, :]


def _f32(ref):
    return ref[...].astype(F32)


def _branch_a_fwd(proj, conv_w, d, name, deps=()):
    l = proj.shape[0]
    tr = _tile(l, 256)
    nblk8 = l // HALO

    def body(v_ref, bg_ref, cg_ref, za_ref, vh_ref, cgh_ref, w_ref, o_ref):
        first = pl.program_id(0) == 0
        cv = _f32(cg_ref) * _f32(v_ref)
        cvh = jnp.where(first, 0.0, _f32(cgh_ref) * _f32(vh_ref))
        w0, w1, w2 = w_ref[0:1, :], w_ref[1:2, :], w_ref[2:3, :]
        q = w2 * cv + w1 * _shift_down(cv, cvh, 1) + w0 * _shift_down(cv, cvh, 2)
        za = _f32(za_ref)
        o_ref[...] = (_f32(bg_ref) * q * (za * _sigmoid(za))).astype(o_ref.dtype)

    return _pallas(
        body, [proj, proj, proj, proj, proj, proj, conv_w], deps, name=name, grid=(l // tr,),
        in_specs=[_row_spec(tr, d, 0), _row_spec(tr, d, 1), _row_spec(tr, d, 2), _row_spec(tr, d, 3),
                  _halo_spec(tr, d, 0, nblk8, True), _halo_spec(tr, d, 2, nblk8, True),
                  _full_spec((SUBLANES, d))],
        out_specs=_row_spec(tr, d, 0),
        out_shape=jax.ShapeDtypeStruct((l, d), BF16),
        compiler_params=_params("parallel"))


def _branch_a_bwd(proj, dpa, conv_w, dproj, d, name):
    l = proj.shape[0]
    tr = _tile(l, 128)
    nblk8 = l // HALO
    ntiles = l // tr

    def body(v_ref, bg_ref, cg_ref, za_ref, dpa_ref, vh_ref, cgh_ref, bgn_ref, zan_ref, dpan_ref,
             w_ref, _, o_ref, dw0_ref, dw1_ref, dw2_ref):
        dv_ref, dbg_ref, dcg_ref, dza_ref = [o_ref.at[:, pl.ds(k * d, d)] for k in range(4)]
        i = pl.program_id(0)
        v, bg, cg, za, dpa_v = _f32(v_ref), _f32(bg_ref), _f32(cg_ref), _f32(za_ref), dpa_ref[...]
        w0, w1, w2 = w_ref[0:1, :], w_ref[1:2, :], w_ref[2:3, :]
        cv = cg * v
        cvh = jnp.where(i == 0, 0.0, _f32(cgh_ref) * _f32(vh_ref))
        cv1 = _shift_down(cv, cvh, 1)
        cv2 = _shift_down(cv, cvh, 2)
        q = w2 * cv + w1 * cv1 + w0 * cv2
        sg = _sigmoid(za)
        s = za * sg
        dbg_ref[...] = (dpa_v * q * s).astype(dbg_ref.dtype)
        dza_ref[...] = (dpa_v * bg * q * (sg * (1.0 + za * (1.0 - sg)))).astype(dza_ref.dtype)
        dq = dpa_v * bg * s
        zan = _f32(zan_ref)
        dqn = jnp.where(i == ntiles - 1, 0.0, dpan_ref[...] * _f32(bgn_ref) * (zan * _sigmoid(zan)))
        dcv = w2 * dq + w1 * _shift_up(dq, dqn, 1) + w0 * _shift_up(dq, dqn, 2)
        dcg_ref[...] = (dcv * v).astype(dcg_ref.dtype)
        dv_ref[...] = (dcv * cg).astype(dv_ref.dtype)

        @pl.when(i == 0)
        def _():
            dw0_ref[...] = jnp.zeros_like(dw0_ref)
            dw1_ref[...] = jnp.zeros_like(dw1_ref)
            dw2_ref[...] = jnp.zeros_like(dw2_ref)

        dw0_ref[...] += jnp.sum(dq * cv2, axis=0, keepdims=True)
        dw1_ref[...] += jnp.sum(dq * cv1, axis=0, keepdims=True)
        dw2_ref[...] += jnp.sum(dq * cv, axis=0, keepdims=True)

    wsum = jax.ShapeDtypeStruct((1, d), F32)
    return pl.pallas_call(
        body, name=name, grid=(ntiles,),
        in_specs=[_row_spec(tr, d, 0), _row_spec(tr, d, 1), _row_spec(tr, d, 2), _row_spec(tr, d, 3),
                  _row_spec(tr, d, 0),
                  _halo_spec(tr, d, 0, nblk8, True), _halo_spec(tr, d, 2, nblk8, True),
                  _halo_spec(tr, d, 1, nblk8, False), _halo_spec(tr, d, 3, nblk8, False),
                  _halo_spec(tr, d, 0, nblk8, False),
                  _full_spec((SUBLANES, d)), ANY_SPEC],
        out_specs=[_row_spec(tr, 4 * d, 0)] + [_full_spec((1, d))] * 3,
        out_shape=[jax.ShapeDtypeStruct(dproj.shape, dproj.dtype)] + [wsum] * 3,
        input_output_aliases={11: 0},
        compiler_params=_params("arbitrary"),
    )(proj, proj, proj, proj, dpa, proj, proj, proj, proj, dpa, conv_w, dproj)


def _gelu_cast(y, name):
    l, w = y.shape
    tr = _tile(l, 512)

    def body(y_ref, o_ref):
        o_ref[...] = _gelu(y_ref[...]).astype(o_ref.dtype)

    return pl.pallas_call(
        body, name=name, grid=(l // tr,), in_specs=[_row_spec(tr, w, 0)],
        out_specs=_row_spec(tr, w, 0), out_shape=jax.ShapeDtypeStruct((l, w), BF16),
        compiler_params=_params("parallel"),
    )(y)


def _glu_post(y, gl, proj, b_glu, zb_col, name):
    l, w = y.shape
    tr = _tile(l, 512)

    def body(y_ref, gl_ref, zb_ref, b_ref, o_ref):
        zb = _f32(zb_ref)
        o_ref[...] = (_gelu(y_ref[...]) * _sigmoid(gl_ref[...] + b_ref[...])
                      * (zb * _sigmoid(zb))).astype(o_ref.dtype)

    return pl.pallas_call(
        body, name=name, grid=(l // tr,),
        in_specs=[_row_spec(tr, w, 0), _row_spec(tr, w, 0), _row_spec(tr, w, zb_col), _full_spec((1, w))],
        out_specs=_row_spec(tr, w, 0), out_shape=jax.ShapeDtypeStruct((l, w), BF16),
        compiler_params=_params("parallel"),
    )(y, gl, proj, b_glu.reshape(1, w))


def _glu_bwd1(y, gl, proj, b_glu, dpb, dproj, zb_col, name):
    l, w = y.shape
    tr = _tile(l, 512)

    def body(y_ref, gl_ref, zb_ref, b_ref, dpb_ref, _, dzb_ref, dgl_ref, t_ref, db_ref):
        zb = _f32(zb_ref)
        dpb_v = dpb_ref[...]
        yg = _gelu(y_ref[...])
        sgl = _sigmoid(gl_ref[...] + b_ref[...])
        szb = _sigmoid(zb)
        dzb_ref[...] = (dpb_v * yg * sgl * (szb * (1.0 + zb * (1.0 - szb)))).astype(dzb_ref.dtype)
        e = dpb_v * (zb * szb)
        dgl = e * yg * sgl * (1.0 - sgl)
        dgl_ref[...] = dgl.astype(dgl_ref.dtype)
        t_ref[...] = e * sgl

        @pl.when(pl.program_id(0) == 0)
        def _():
            db_ref[...] = jnp.zeros_like(db_ref)

        db_ref[...] += jnp.sum(dgl, axis=0, keepdims=True)

    return pl.pallas_call(
        body, name=name, grid=(l // tr,),
        in_specs=[_row_spec(tr, w, 0), _row_spec(tr, w, 0), _row_spec(tr, w, zb_col), _full_spec((1, w)),
                  _row_spec(tr, w, 0), ANY_SPEC],
        out_specs=[_row_spec(tr, w, zb_col)] + [_row_spec(tr, w, 0)] * 2 + [_full_spec((1, w))],
        out_shape=[jax.ShapeDtypeStruct(dproj.shape, dproj.dtype), jax.ShapeDtypeStruct((l, w), BF16),
                   jax.ShapeDtypeStruct((l, w), F32), jax.ShapeDtypeStruct((1, w), F32)],
        input_output_aliases={5: 0},
        compiler_params=_params("arbitrary"),
    )(y, gl, proj, b_glu.reshape(1, w), dpb, dproj)


def _write_cols(dproj, cols, col, name):
    l, w = cols.shape
    tr = _tile(l, 512)

    def body(c_ref, _, o_ref):
        o_ref[...] = c_ref[...].astype(o_ref.dtype)

    return pl.pallas_call(
        body, name=name, grid=(l // tr,), in_specs=[_row_spec(tr, w, 0), ANY_SPEC],
        out_specs=_row_spec(tr, w, col), out_shape=jax.ShapeDtypeStruct(dproj.shape, dproj.dtype),
        input_output_aliases={1: 0}, compiler_params=_params("parallel"),
    )(cols, dproj)


def _glu_bwd2(y, t1, dyg2, name, deps=()):
    l, w = y.shape
    tr = _tile(l, 512)

    def body(y_ref, t_ref, d_ref, o_ref):
        o_ref[...] = (t_ref[...] + d_ref[...]) * _gelu_grad(y_ref[...])

    return _pallas(
        body, [y, t1, dyg2], deps, name=name, grid=(l // tr,), in_specs=[_row_spec(tr, w, 0)] * 3,
        out_specs=_row_spec(tr, w, 0), out_shape=jax.ShapeDtypeStruct((l, w), F32),
        compiler_params=_params("parallel"))


def _merge_fwd(proj, ya, yb, d, ga_col, gb_col, name):
    l = proj.shape[0]
    tr = _tile(l, 256)

    def body(ga_ref, gb_ref, ya_ref, yb_ref, o_ref):
        o_ref[...] = (_sigmoid(_f32(ga_ref)) * ya_ref[...]
                      + _sigmoid(_f32(gb_ref)) * yb_ref[...]).astype(o_ref.dtype)

    return pl.pallas_call(
        body, name=name, grid=(l // tr,),
        in_specs=[_row_spec(tr, d, ga_col), _row_spec(tr, d, gb_col), _row_spec(tr, d, 0), _row_spec(tr, d, 0)],
        out_specs=_row_spec(tr, d, 0), out_shape=jax.ShapeDtypeStruct((l, d), BF16),
        compiler_params=_params("parallel"),
    )(proj, proj, ya, yb)


def _merge_bwd(proj, ya, yb, dm, d, ga_col, name):
    l, n = proj.shape
    tr = _tile(l, 256)

    def body(g_ref, ya_ref, yb_ref, dm_ref, dy_ref, dg_ref):
        dmv = dm_ref[...]
        sg = _sigmoid(_f32(g_ref))
        yv = jnp.where(pl.program_id(1) == 0, ya_ref[...], yb_ref[...])
        dy_ref[...] = (dmv * sg).astype(dy_ref.dtype)
        dg_ref[...] = (dmv * yv * sg * (1.0 - sg)).astype(dg_ref.dtype)

    row = pl.BlockSpec((tr, d), lambda i, j: (i, 0))
    return pl.pallas_call(
        body, name=name, grid=(l // tr, 2),
        in_specs=[pl.BlockSpec((tr, d), lambda i, j: (i, ga_col + j)), row, row, row],
        out_specs=[pl.BlockSpec((None, tr, d), lambda i, j: (j, i, 0)),
                   pl.BlockSpec((tr, d), lambda i, j: (i, ga_col + j))],
        out_shape=[jax.ShapeDtypeStruct((2, l, d), BF16), jax.ShapeDtypeStruct((l, n), BF16)],
        compiler_params=_params("parallel", "arbitrary"),
    )(proj, ya, yb, dm)


def _to_segments(a):
    l, w = a.shape
    return a.reshape(SUBLANES, l // SUBLANES, w).transpose(1, 0, 2).reshape(l, w)


def _from_segments(a):
    l, w = a.shape
    return a.reshape(l // SUBLANES, SUBLANES, w).transpose(1, 0, 2).reshape(l, w)


def _dense(z, shape):
    return jnp.broadcast_to(z, shape).reshape(-1, LANES)


def _s5_disc(are, aim, ldt):
    dt = jnp.exp(ldt)
    er = jnp.exp(are * dt)
    lbr = er * jnp.cos(aim * dt)
    lbi = er * jnp.sin(aim * dt)
    inv = 1.0 / (are * are + aim * aim)
    fr = ((lbr - 1.0) * are + lbi * aim) * inv
    fi = (lbi * are - (lbr - 1.0) * aim) * inv
    return dt, lbr, lbi, inv, fr, fi


def _s5_params(are, aim, ldt, bre, bim, name, deps=()):
    shape = are.shape

    def body(are_ref, aim_ref, ldt_ref, bre_ref, bim_ref, lbr_ref, lbi_ref, bbr_ref, bbi_ref):
        _, lbr, lbi, _, fr, fi = _s5_disc(are_ref[...], aim_ref[...], ldt_ref[...])
        lbr_ref[...] = lbr
        lbi_ref[...] = lbi
        bbr_ref[...] = fr * bre_ref[...] - fi * bim_ref[...]
        bbi_ref[...] = fr * bim_ref[...] + fi * bre_ref[...]

    out = jax.ShapeDtypeStruct(shape, F32)
    return _pallas(body, [are, aim, ldt, bre, bim], deps, name=name,
                   in_specs=[pl.BlockSpec(memory_space=pltpu.VMEM)] * 5, out_shape=[out] * 4,
                   compiler_params=pltpu.CompilerParams(vmem_limit_bytes=VMEM_LIMIT_BYTES))


def _s5_params_bwd(are, aim, ldt, bre, bim, glbr, glbi, gbbr, gbbi, n_groups, name, deps=()):
    shape = are.shape
    rows_per_group = shape[0] // n_groups

    def body(are_ref, aim_ref, ldt_ref, bre_ref, bim_ref, glbr_ref, glbi_ref, gbbr_ref, gbbi_ref,
             gar_ref, gai_ref, gdt_ref, gbr_ref, gbi_ref):
        are_v, aim_v = are_ref[...], aim_ref[...]
        bre_v, bim_v = bre_ref[...], bim_ref[...]
        gbbr_v, gbbi_v = gbbr_ref[...], gbbi_ref[...]
        dt, lbr, lbi, inv, fr, fi = _s5_disc(are_v, aim_v, ldt_ref[...])
        gbr_ref[...] = fr * gbbr_v + fi * gbbi_v
        gbi_ref[...] = fr * gbbi_v - fi * gbbr_v
        lane_group = lax.broadcasted_iota(jnp.int32, (LANES, LANES), 0) // S5_GROUP
        same_group = (lane_group == lax.broadcasted_iota(jnp.int32, (LANES, LANES), 1) // S5_GROUP)
        ones = same_group.astype(F32)
        gfr = jnp.dot(bre_v * gbbr_v + bim_v * gbbi_v, ones, precision=HIGHEST, preferred_element_type=F32)
        gfi = jnp.dot(bre_v * gbbi_v - bim_v * gbbr_v, ones, precision=HIGHEST, preferred_element_type=F32)
        glr = glbr_ref[...] + (are_v * gfr - aim_v * gfi) * inv
        gli = glbi_ref[...] + (are_v * gfi + aim_v * gfr) * inv
        qr = (fr * are_v + fi * aim_v) * inv
        qi = (fi * are_v - fr * aim_v) * inv
        gzr = lbr * glr + lbi * gli
        gzi = lbr * gli - lbi * glr
        gar_ref[...] = dt * gzr - (qr * gfr + qi * gfi)
        gai_ref[...] = dt * gzi - (qr * gfi - qi * gfr)
        e = dt * (are_v * gzr + aim_v * gzi)
        per_group = jnp.sum(e.reshape(n_groups, rows_per_group, LANES), axis=1)
        total = jnp.sum(per_group, axis=1, keepdims=True) * (1.0 / S5_GROUP)
        gdt_ref[...] = jnp.broadcast_to(total, gdt_ref.shape)

    out = jax.ShapeDtypeStruct(shape, F32)
    return _pallas(
        body, [are, aim, ldt, bre, bim, glbr, glbi, gbbr, gbbi], deps, name=name,
        in_specs=[pl.BlockSpec(memory_space=pltpu.VMEM)] * 9,
        out_shape=[out, out, jax.ShapeDtypeStruct((n_groups, LANES), F32), out, out],
        compiler_params=pltpu.CompilerParams(vmem_limit_bytes=VMEM_LIMIT_BYTES))


def _cmul(ar, ai, br, bi):
    return ar * br - ai * bi, ar * bi + ai * br


def _scan_in_place(hr_ref, hi_ref, lr, li, reverse):
    l, wb = hr_ref.shape
    nt = l // SUBLANES
    shift = SUBLANES - 1 if reverse else 1
    unroll = 8 if nt % 8 == 0 else 1

    def rows(k):
        t = (nt - 1 - k) if reverse else k
        return pl.ds(pl.multiple_of(t * SUBLANES, SUBLANES), SUBLANES)

    zero = jnp.zeros((SUBLANES, wb), F32)
    one = jnp.ones((SUBLANES, wb), F32)

    def local_step(k, carry):
        hr, hi, pr, pi = carry
        r = rows(k)
        tr_, ti_ = _cmul(lr, li, hr, hi)
        hr, hi = tr_ + hr_ref[r, :], ti_ + hi_ref[r, :]
        hr_ref[r, :] = hr
        hi_ref[r, :] = hi
        pr, pi = _cmul(lr, li, pr, pi)
        return hr, hi, pr, pi

    er, ei, lnr, lni = lax.fori_loop(0, nt, local_step, (zero, zero, one, zero), unroll=unroll)

    row = lax.broadcasted_iota(jnp.int32, (SUBLANES, wb), 0)
    tr_, ti_ = er, ei
    for j in range(1, SUBLANES):
        pr_, pi_ = _cmul(lnr, lni, pltpu.roll(tr_, shift, axis=0), pltpu.roll(ti_, shift, axis=0))
        at = row == ((SUBLANES - 1 - j) if reverse else j)
        tr_ = jnp.where(at, er + pr_, tr_)
        ti_ = jnp.where(at, ei + pi_, ti_)
    edge = row == ((SUBLANES - 1) if reverse else 0)
    cr = jnp.where(edge, 0.0, pltpu.roll(tr_, shift, axis=0))
    ci = jnp.where(edge, 0.0, pltpu.roll(ti_, shift, axis=0))

    def fix_step(k, carry):
        pr, pi = carry
        pr, pi = _cmul(lr, li, pr, pi)
        r = rows(k)
        ar_, ai_ = _cmul(pr, pi, cr, ci)
        hr_ref[r, :] = hr_ref[r, :] + ar_
        hi_ref[r, :] = hi_ref[r, :] + ai_
        return pr, pi

    lax.fori_loop(0, nt, fix_step, (one, zero), unroll=unroll)


def _dot(a, b):
    return jnp.dot(a.astype(BF16), b.astype(BF16), preferred_element_type=F32)


def _s5_forward(u_seg, mb_re, mb_im, mc_re, mc_im, lam_re, lam_im, dvec, name):
    l = u_seg.shape[0]
    nb, kin, kst = mb_re.shape

    def body(u_ref, mbr_ref, mbi_ref, mcr_ref, mci_ref, lr_ref, li_ref, d_ref, hr_ref, hi_ref, y_ref):
        u = u_ref[...]
        hr_ref[...] = _dot(u, mbr_ref[...])
        hi_ref[...] = _dot(u, mbi_ref[...])
        _scan_in_place(hr_ref, hi_ref, jnp.broadcast_to(lr_ref[...], (SUBLANES, kst)),
                       jnp.broadcast_to(li_ref[...], (SUBLANES, kst)), False)
        y_ref[...] = (_dot(hr_ref[...], mcr_ref[...]) - _dot(hi_ref[...], mci_ref[...])
                      + d_ref[...] * u.astype(F32))

    act = pl.BlockSpec((l, kin), lambda b: (0, b))
    state = pl.BlockSpec((l, kst), lambda b: (0, b))
    up = pl.BlockSpec((None, kin, kst), lambda b: (b, 0, 0))
    down = pl.BlockSpec((None, kst, kin), lambda b: (b, 0, 0))
    hshape = jax.ShapeDtypeStruct((l, nb * kst), F32)
    return pl.pallas_call(
        body, name=name, grid=(nb,),
        in_specs=[act, up, up, down, down, pl.BlockSpec((1, kst), lambda b: (0, b)),
                  pl.BlockSpec((1, kst), lambda b: (0, b)), pl.BlockSpec((1, kin), lambda b: (0, b))],
        out_specs=[state, state, act],
        out_shape=[hshape, hshape, jax.ShapeDtypeStruct((l, nb * kin), F32)],
        compiler_params=_params("parallel"),
    )(u_seg, mb_re, mb_im, mc_re, mc_im, lam_re, lam_im, dvec)


def _dot_ta(a, b):
    return lax.dot_general(a.astype(BF16), b.astype(BF16), (((0,), (0,)), ((), ())), preferred_element_type=F32)


def _dot_nt(a, b):
    return lax.dot_general(a.astype(BF16), b.astype(BF16), (((1,), (1,)), ((), ())), preferred_element_type=F32)


def _s5_backward(dy_seg, u_seg, h_re, h_im, mb_re, mb_im, mc_re, mc_im, lam_re, lam_im_neg, dvec, name):
    l = dy_seg.shape[0]
    nb, kin, kst = mb_re.shape
    nt = l // SUBLANES

    def body(dy_ref, u_ref, hr_ref, hi_ref, mbr_ref, mbi_ref, mcr_ref, mci_ref, lr_ref, li_ref,
             d_ref, du_ref, gcr_ref, gci_ref, gbr_ref, gbi_ref, glr_ref, gli_ref, dsk_ref, qr_ref, qi_ref):
        dy = dy_ref[...]
        qr_ref[...] = _dot_nt(dy, mcr_ref[...])
        qi_ref[...] = -_dot_nt(dy, mci_ref[...])
        _scan_in_place(qr_ref, qi_ref, jnp.broadcast_to(lr_ref[...], (SUBLANES, kst)),
                       jnp.broadcast_to(li_ref[...], (SUBLANES, kst)), True)
        du_ref[...] = _dot_nt(qr_ref[...], mbr_ref[...]) + _dot_nt(qi_ref[...], mbi_ref[...]) + d_ref[...] * dy
        dsk_ref[...] = jnp.sum(dy * _f32(u_ref), axis=0, keepdims=True)
        gcr_ref[...] = _dot_ta(dy, hr_ref[...])
        gci_ref[...] = _dot_ta(dy, hi_ref[...])
        gbr_ref[...] = _dot_ta(u_ref[...], qr_ref[...])
        gbi_ref[...] = _dot_ta(u_ref[...], qi_ref[...])

        row = lax.broadcasted_iota(jnp.int32, (SUBLANES, kst), 0)
        last = pl.ds((nt - 1) * SUBLANES, SUBLANES)
        first = pl.ds(0, SUBLANES)
        pr = jnp.where(row == 0, 0.0, pltpu.roll(hr_ref[last, :], 1, axis=0))
        pi = jnp.where(row == 0, 0.0, pltpu.roll(hi_ref[last, :], 1, axis=0))
        gr, gi = qr_ref[first, :], qi_ref[first, :]

        def step(t, carry):
            acc_r, acc_i = carry
            cur = pl.ds(pl.multiple_of(t * SUBLANES, SUBLANES), SUBLANES)
            prev = pl.ds(pl.multiple_of((t - 1) * SUBLANES, SUBLANES), SUBLANES)
            gr, gi = qr_ref[cur, :], qi_ref[cur, :]
            pr, pi = hr_ref[prev, :], hi_ref[prev, :]
            return acc_r + gr * pr + gi * pi, acc_i + gi * pr - gr * pi

        acc_r, acc_i = lax.fori_loop(1, nt, step, (gr * pr + gi * pi, gi * pr - gr * pi))
        glr_ref[...] = jnp.sum(acc_r, axis=0, keepdims=True)
        gli_ref[...] = jnp.sum(acc_i, axis=0, keepdims=True)

    act = pl.BlockSpec((l, kin), lambda b: (0, b))
    state = pl.BlockSpec((l, kst), lambda b: (0, b))
    up = pl.BlockSpec((None, kin, kst), lambda b: (b, 0, 0))
    down = pl.BlockSpec((None, kst, kin), lambda b: (b, 0, 0))
    vec_st = pl.BlockSpec((1, kst), lambda b: (0, b))
    vec_in = pl.BlockSpec((1, kin), lambda b: (0, b))
    outer = jax.ShapeDtypeStruct((nb, kin, kst), F32)
    lam_shape = jax.ShapeDtypeStruct((1, nb * kst), F32)
    return pl.pallas_call(
        body, name=name, grid=(nb,),
        in_specs=[act, act, state, state, up, up, down, down, vec_st, vec_st, vec_in],
        out_specs=[act, up, up, up, up, vec_st, vec_st, vec_in],
        out_shape=[jax.ShapeDtypeStruct((l, nb * kin), F32), outer, outer, outer, outer, lam_shape, lam_shape,
                   jax.ShapeDtypeStruct((1, nb * kin), F32)],
        scratch_shapes=[pltpu.VMEM((l, kst), F32), pltpu.VMEM((l, kst), F32)],
        compiler_params=_params("parallel"),
    )(dy_seg, u_seg, h_re, h_im, mb_re, mb_im, mc_re, mc_im, lam_re, lam_im_neg, dvec)


def _block_diag(m, nb):
    g, r, s = m.shape
    gb = g // nb
    eye = jnp.eye(gb, dtype=m.dtype)
    out = m.reshape(nb, gb, r, 1, s) * eye[None, :, None, :, None]
    return out.reshape(nb, gb * r, gb * s)


def _block_diag_extract(mat, g, r, s):
    nb = mat.shape[0]
    gb = g // nb
    eye = jnp.eye(gb, dtype=mat.dtype)
    m5 = mat.reshape(nb, gb, r, gb, s) * eye[None, :, None, :, None]
    return jnp.sum(m5, axis=3).reshape(g, r, s)


def _adamw(w, m, v, gslots, name, layer=0, prev=None, deps=()):
    layers, r, c = w.shape
    s = gslots.shape[0]
    tr = _tile(r, max(SUBLANES, 1 << int(math.log2(ADAMW_BLOCK_ELEMS // c))))
    bc1 = 1.0 / (1.0 - ADAM_B1 ** ADAM_STEP)
    bc2 = 1.0 / (1.0 - ADAM_B2 ** ADAM_STEP)

    def body(w_ref, m_ref, v_ref, g_ref, *rest):
        go_ref, d_ref, mo_ref, vo_ref = rest[-4:]
        g = g_ref[0].astype(F32)
        for k in range(1, s):
            g = g + g_ref[k].astype(F32)
        mn = ADAM_B1 * m_ref[...] + (1.0 - ADAM_B1) * g
        vn = ADAM_B2 * v_ref[...] + (1.0 - ADAM_B2) * (g * g)
        go_ref[...] = g
        mo_ref[...] = mn
        vo_ref[...] = vn
        d_ref[...] = -ADAM_LR * ((mn * bc1) / (jnp.sqrt(vn * bc2) + ADAM_EPS) + ADAM_WD * w_ref[...])

    spec = pl.BlockSpec((None, tr, c), lambda i: (layer, i, 0))
    out = jax.ShapeDtypeStruct((layers, r, c), F32)
    in_specs = [spec, spec, spec, pl.BlockSpec((s, tr, c), lambda i: (0, i, 0))]
    args = [w, m, v, gslots]
    aliases = {}
    if prev is not None:
        in_specs += [ANY_SPEC] * 4
        args += list(prev)
        aliases = {4 + q: q for q in range(4)}
    in_specs += [ANY_SPEC] * len(deps)
    args += list(deps)
    return pl.pallas_call(
        body, name=name, grid=(r // tr,), in_specs=in_specs,
        out_specs=[spec] * 4, out_shape=[out] * 4, input_output_aliases=aliases,
        compiler_params=_params("parallel"),
    )(*args)


def _pack(parts):
    flat = jnp.concatenate([p.reshape(-1) for p in parts])
    pad = (-flat.shape[0]) % (PACK_ROWS * LANES)
    return jnp.pad(flat, (0, pad)).reshape(-1, LANES)


def _unpack(packed, shapes):
    flat = packed.reshape(-1)
    out, off = [], 0
    for shp in shapes:
        size = math.prod(shp)
        out.append(flat[off:off + size].reshape(shp))
        off += size
    return out


def kernel(x, norm_g, w_in, conv_w, w_out_a, a_re, a_im, log_dt, b_re, b_im, c_re, c_im, d_skip, w_glu, b_glu, w_out_b, w_o, final_g, loss_target, m_norm_g, m_w_in, m_conv_w, m_w_out_a, m_a_re, m_a_im, m_log_dt, m_b_re, m_b_im, m_c_re, m_c_im, m_d_skip, m_w_glu, m_b_glu, m_w_out_b, m_w_o, m_final_g, v_norm_g, v_w_in, v_conv_w, v_w_out_a, v_a_re, v_a_im, v_log_dt, v_b_re, v_b_im, v_c_re, v_c_im, v_d_skip, v_w_glu, v_b_glu, v_w_out_b, v_w_o, v_final_g):
    depth = norm_g.shape[0]
    l, d = x.shape[1], x.shape[2]
    ws = w_glu.shape[2]
    n_groups, n_state = a_re.shape[1], a_re.shape[2]
    nb = ws // LANES
    assert S5_GROUP == b_re.shape[3] and n_state * S5_GB == 4 * LANES
    u_col, zb_col = 4 * d // ws, 4 * d // ws + 1
    ga_col, gb_col = (4 * d + 2 * ws) // d, (4 * d + 2 * ws) // d + 1
    me = 4 * lax.axis_index("x") + 2 * lax.axis_index("y") + lax.axis_index("c")

    xs = [x[0]]
    tgt = loss_target[0]

    big_names = ("w_in", "w_out_a", "w_glu", "w_out_b", "w_o")
    big = dict(w_in=(w_in, m_w_in, v_w_in), w_out_a=(w_out_a, m_w_out_a, v_w_out_a),
               w_glu=(w_glu, m_w_glu, v_w_glu), w_out_b=(w_out_b, m_w_out_b, v_w_out_b),
               w_o=(w_o, m_w_o, v_w_o))

    all_shards = [[big[k][0][i].astype(BF16) for k in big_names] for i in range(depth)]

    def shards_bf16(i):
        return all_shards[i]

    main_names = ("a_re", "a_im", "log_dt", "b_re", "b_im", "c_re", "c_im", "d_skip", "b_glu", "conv_w")
    small_w = dict(a_re=(a_re, m_a_re, v_a_re), a_im=(a_im, m_a_im, v_a_im),
                   log_dt=(log_dt, m_log_dt, v_log_dt), b_re=(b_re, m_b_re, v_b_re), b_im=(b_im, m_b_im, v_b_im),
                   c_re=(c_re, m_c_re, v_c_re), c_im=(c_im, m_c_im, v_c_im), d_skip=(d_skip, m_d_skip, v_d_skip),
                   b_glu=(b_glu, m_b_glu, v_b_glu))
    main_shapes = [small_w[k][0].shape for k in main_names[:-1]] + [(depth, 3, d)]
    zeros_conv = jnp.zeros((depth, 3, d), F32)
    main_wmv = [_pack([small_w[k][q] for k in main_names[:-1]] + [zeros_conv])[None] for q in range(3)]
    zero1 = jnp.zeros((1,), F32)
    gains_wmv = [_pack([g_, f_, zero1])[None]
                 for g_, f_ in ((norm_g, final_g), (m_norm_g, m_final_g), (v_norm_g, v_final_g))]
    dc = d // N_DEV
    pad8 = lambda a: jnp.pad(a.reshape(depth * 3, dc), ((0, SUBLANES - depth * 3), (0, 0)))[None]
    conv_wmv = [pad8(conv_w), pad8(m_conv_w), pad8(v_conv_w)]

    def gather_start(shards, name, deps=()):
        sems, srcs, lands, token = _exchange_start(
            _plan_gather_chips, shards, [_landing(s_, N_DEV, me) for s_ in shards], f"{name}_start", deps)
        return (name, sems, srcs, lands), token

    def gather_forward(state, after, deps=()):
        name, sems, srcs, lands = state
        _, lands = _exchange_wait(_plan_gather_chips, sems, srcs, lands, after, f"{name}_wait")
        sems, _, lands, token = _exchange_start(_plan_gather_forward, None, lands, f"{name}_forward_start", deps)
        return (name, sems, lands), token

    def gather_finish(state, after):
        name, sems, lands = state
        return _exchange_wait(_plan_gather_forward, sems, None, lands, after, f"{name}_forward_wait")[1]

    conv_shard = jnp.pad(conv_w.reshape(depth * 3, -1), ((0, SUBLANES - depth * 3), (0, 0)))
    w_in_state, token = gather_start([shards_bf16(0)[0], conv_shard], "ag_w_in_0")
    s5 = []
    shape3 = (n_groups, n_state, S5_GROUP)
    for i in range(depth):
        dense_in = (_dense(a_re[i][:, :, None], shape3), _dense(a_im[i][:, :, None], shape3),
                    _dense(log_dt[i][:, None, None], shape3), b_re[i].reshape(-1, LANES), b_im[i].reshape(-1, LANES))
        lbr, lbi, bbr, bbi = _s5_params(*dense_in, f"s5_params_{i}", deps=(token,))
        bbr3, bbi3 = bbr.reshape(shape3), bbi.reshape(shape3)
        diag = lambda m: _block_diag(m, nb).astype(BF16)
        s5.append(dict(
            dense_in=dense_in,
            lam_re=lbr.reshape(shape3)[:, :, 0].reshape(1, -1), lam_im=lbi.reshape(shape3)[:, :, 0].reshape(1, -1),
            up=(diag(bbr3.transpose(0, 2, 1)), diag(bbi3.transpose(0, 2, 1))),
            down=(diag(c_re[i].transpose(0, 2, 1)), diag(c_im[i].transpose(0, 2, 1)))))
    prologue = [m for p in s5 for k in ("up", "down") for m in p[k]]
    prologue += [p[k] for p in s5 for k in ("lam_re", "lam_im")]
    prologue += main_wmv + gains_wmv + conv_wmv + all_shards[0][1:] + [s_ for sh in all_shards[1:] for s_ in sh]
    w_in_state, token = gather_forward(w_in_state, prologue)
    rest_state, token = gather_start(shards_bf16(0)[1:], "ag_rest_0", deps=(token,))
    next_state = None
    if depth > 1:
        next_state, token = gather_start([shards_bf16(1)[0]], "ag_w_in_1", deps=(token,))

    saved = []
    wg = [None] * depth
    conv_full = None
    for i in range(depth):
        xi = xs[-1]
        h = _rmsnorm_fwd(xi, norm_g[i], f"rmsnorm_fwd_{i}", deps=(token,))
        arrived = gather_finish(w_in_state, [h])
        if i == 0:
            conv_full = arrived[1].transpose(1, 0, 2).reshape(SUBLANES, d)[:depth * 3].reshape(depth, 3, d)
        conv8 = jnp.pad(conv_full[i], ((0, SUBLANES - 3), (0, 0)))
        proj = _mm_win_fwd(h, arrived[0], f"mm_proj_{i}")
        u_seg = _to_segments(proj[:, 4 * d:4 * d + ws])
        h_re, h_im, y_seg = _s5_forward(u_seg, *s5[i]["up"], *s5[i]["down"], s5[i]["lam_re"], s5[i]["lam_im"],
                                        d_skip[i].reshape(1, ws), f"s5_forward_{i}")
        rest_state, token = gather_forward(rest_state, [y_seg])
        pa = _branch_a_fwd(proj, conv8, d, f"branch_a_fwd_{i}", deps=(token,))
        rest = gather_finish(rest_state, [pa])
        wg[i] = g = dict(w_in=arrived[0], w_a=rest[0].reshape(d, d), w_glu=rest[1].reshape(ws, ws),
                         w_b=rest[2], w_o=rest[3].reshape(d, d))
        ya = _mm(pa, g["w_a"], name=f"mm_ya_{i}")
        y = _from_segments(y_seg)
        yg = _gelu_cast(y, f"gelu_{i}")
        gl = _mm(yg, g["w_glu"], name=f"mm_glu_{i}")
        pb = _glu_post(y, gl, proj, b_glu[i], zb_col, f"glu_post_{i}")
        w_b2d = g["w_b"].transpose(1, 0, 2).reshape(ws, d)
        yb = _mm(pb, w_b2d, name=f"mm_yb_{i}")
        mrg = _merge_fwd(proj, ya, yb, d, ga_col, gb_col, f"merge_fwd_{i}")
        deps = ()
        if i + 1 < depth:
            w_in_state, token = gather_forward(next_state, [mrg])
            rest_state, token = gather_start(shards_bf16(i + 1)[1:], f"ag_rest_{i + 1}", deps=(token,))
            if i + 2 < depth:
                next_state, token = gather_start([shards_bf16(i + 2)[0]], f"ag_w_in_{i + 2}", deps=(token,))
            deps = (token,)
        xs.append(_mm(mrg, g["w_o"], name=f"mm_out_{i}", add=xi, deps=deps))
        saved.append(dict(h=h, proj=proj, pa=pa, ya=ya, yb=yb, y=y, yg=yg, gl=gl, pb=pb, mrg=mrg,
                          u_seg=u_seg, h_re=h_re, h_im=h_im, conv8=conv8, w_b2d=w_b2d))

    dx, g_final, loss_part = _final_loss(xs[-1], final_g, tgt, "final_loss")

    rs_pending = []
    small = {k: [None] * depth for k in ("norm_g", "a_re", "a_im", "log_dt", "b_re", "b_im", "c_re", "c_im",
                                         "d_skip", "b_glu", "conv_w")}

    my_chip = 2 * lax.axis_index("x") + lax.axis_index("y")

    def reduce_on_chip(pieces, tag):
        lands = [lax.empty((4,) + p.shape[1:], p.dtype) for p in pieces]
        sems, srcs, lands, token = _exchange_start(_plan_reduce_sibling, pieces, lands, f"rs_sibling_start_{tag}")
        return (sems, srcs, lands), token

    def reduce_across_chips(names_, state, layer, tag, after):
        sems, srcs, lands = state
        srcs, lands = _exchange_wait(_plan_reduce_sibling, sems, srcs, lands, after, f"rs_sibling_wait_{tag}")
        sums = [_chip_sums(p, l_, f"chip_sum_{k}_{layer}") for k, p, l_ in zip(names_, srcs, lands)]
        lands = [_landing(lax.dynamic_index_in_dim(s_, my_chip, 0, keepdims=False), 4, my_chip) for s_ in sums]
        sems, srcs, lands, token = _exchange_start(_plan_reduce_chips, sums, lands, f"rs_chips_start_{tag}")
        rs_pending.append((names_, layer, sems, srcs, lands, f"rs_chips_wait_{tag}"))
        return token

    for i in reversed(range(depth)):
        s, g = saved[i], wg[i]
        proj = s["proj"]
        dxo_b = dx.astype(BF16)
        dm = _mm(dxo_b, g["w_o"], name=f"mm_dm_{i}", nt=True)
        gw_o = _mm(s["mrg"], dxo_b, ta=True, name=f"mm_gw_o_{i}", out_dtype=BF16)
        dy2, dproj = _merge_bwd(proj, s["ya"], s["yb"], dm, d, ga_col, f"merge_bwd_{i}")
        dya, dyb = dy2[0], dy2[1]
        dpa = _mm(dya, g["w_a"], name=f"mm_dpa_{i}", nt=True)
        gw_a = _mm(s["pa"], dya, ta=True, name=f"mm_gw_a_{i}", out_dtype=BF16)
        dpb = _mm(dyb, s["w_b2d"], name=f"mm_dpb_{i}", nt=True)
        gw_b = _mm(s["pb"], dyb, ta=True, name=f"mm_gw_b_{i}", split_n=N_DEV, out_dtype=BF16)
        dproj, dw0, dw1, dw2 = _branch_a_bwd(proj, dpa, s["conv8"], dproj, d, f"branch_a_bwd_{i}")
        small["conv_w"][i] = jnp.concatenate([dw0, dw1, dw2], axis=0)
        dproj, dgl, t1, db_glu = _glu_bwd1(s["y"], s["gl"], proj, b_glu[i], dpb, dproj, zb_col, f"glu_bwd1_{i}")
        small["b_glu"][i] = db_glu.reshape(ws)
        dyg2 = _mm(dgl, g["w_glu"], name=f"mm_dyg_{i}", nt=True)
        gw_glu = _mm(s["yg"], dgl, ta=True, name=f"mm_gw_glu_{i}", out_dtype=BF16)
        small_names_ = ("w_out_a", "w_glu", "w_out_b", "w_o")
        state, token = reduce_on_chip(
            [gw_a.reshape(N_DEV, d // N_DEV, d), gw_glu.reshape(N_DEV, ws // N_DEV, ws), gw_b,
             gw_o.reshape(N_DEV, d // N_DEV, d)], f"small_{i}")
        dy = _glu_bwd2(s["y"], t1, dyg2, f"glu_bwd2_{i}", deps=(token,))
        dy_seg = _to_segments(dy)
        u_seg = s["u_seg"]
        du_seg, gc_re, gc_im, gbb_re, gbb_im, glam_re, glam_im, dskip = _s5_backward(
            dy_seg, u_seg, s["h_re"], s["h_im"], *s5[i]["up"], *s5[i]["down"],
            s5[i]["lam_re"], -s5[i]["lam_im"], d_skip[i].reshape(1, ws), f"s5_backward_{i}")
        token = reduce_across_chips(small_names_, state, i, f"small_{i}", [du_seg])
        dproj = _write_cols(dproj, _from_segments(du_seg), u_col, f"write_du_{i}")
        gw_in = _mm(s["h"], dproj, ta=True, name=f"mm_gw_in_{i}", split_n=N_DEV, tm=1024,
                    out_dtype=BF16, deps=(token,))
        state, token = reduce_on_chip([gw_in], f"w_in_{i}")
        small["d_skip"][i] = dskip.reshape(n_groups, S5_GROUP)
        small["c_re"][i] = _block_diag_extract(gc_re, n_groups, S5_GROUP, n_state)
        small["c_im"][i] = -_block_diag_extract(gc_im, n_groups, S5_GROUP, n_state)
        gbb_re = _block_diag_extract(gbb_re, n_groups, S5_GROUP, n_state).transpose(0, 2, 1)
        gbb_im = _block_diag_extract(gbb_im, n_groups, S5_GROUP, n_state).transpose(0, 2, 1)
        gar, gai, gdt, gbr, gbi = _s5_params_bwd(
            *s5[i]["dense_in"], _dense(glam_re.reshape(n_groups, n_state, 1), shape3),
            _dense(glam_im.reshape(n_groups, n_state, 1), shape3),
            gbb_re.reshape(-1, LANES), gbb_im.reshape(-1, LANES), n_groups, f"s5_params_bwd_{i}", deps=(token,))
        small["a_re"][i] = gar.reshape(shape3)[:, :, 0]
        small["a_im"][i] = gai.reshape(shape3)[:, :, 0]
        small["log_dt"][i] = gdt[:, 0]
        small["b_re"][i] = gbr.reshape(shape3)
        small["b_im"][i] = gbi.reshape(shape3)
        if i == 0:
            part = {k: jnp.stack(small[k]) for k in main_names}
            main_state, token = gather_start([_pack([part[k] for k in main_names]).astype(BF16)], "ag_small")
            token = reduce_across_chips(("w_in",), state, i, f"w_in_{i}", [token])
            main_state, token = gather_forward(main_state, [token])
            dh = _mm_win_bwd(dproj, g["w_in"], f"mm_dh_{i}", deps=(token,))
            main_slots = gather_finish(main_state, [dh])[0]
            deps = ()
        else:
            dh = _mm_win_bwd(dproj, g["w_in"], f"mm_dh_{i}", deps=(gar,))
            deps = (reduce_across_chips(("w_in",), state, i, f"w_in_{i}", [dh]),)
        dx, dng = _rmsnorm_bwd(xs[i], norm_g[i], dh, dx, f"rmsnorm_bwd_{i}", deps=deps)
        small["norm_g"][i] = dng.reshape(d)

    results = {}

    gain_grads = jnp.concatenate([jnp.stack(small["norm_g"]).reshape(-1), g_final.reshape(d)])
    gain_shapes = [(depth, d), (d,), (1,)]
    gains_state, token = gather_start([_pack([gain_grads, loss_part[0, :1]])], "ag_gains")

    sres = [_unpack(p[0], main_shapes) for p in _adamw(*main_wmv, main_slots, "adamw_small", deps=(token,))]
    for j, k in enumerate(main_names[:-1]):
        results[k] = [sres[q][j] for q in range(4)]
    gconv = lax.dynamic_slice_in_dim(sres[0][-1], me * dc, dc, axis=2)
    cres = _adamw(*conv_wmv, pad8(gconv), "adamw_conv_w")
    results["conv_w"] = [r_[0, :depth * 3].reshape(depth, 3, dc) for r_ in cres]

    after = [cres[0]]
    for names_, layer, sems, srcs, lands, wait_name in rs_pending:
        _, slots = _exchange_wait(_plan_reduce_chips, sems, srcs, lands, after, wait_name)
        for k, land in zip(names_, slots):
            w_, m_, v_ = big[k]
            results[k] = _adamw(w_, m_, v_, land, f"adamw_{k}_{layer}", layer=layer, prev=results.get(k))
            after = [results[k][0]]

    gains_state, token = gather_forward(gains_state, after)
    gpack = gather_finish(gains_state, [token])[0]
    gres = [_unpack(p[0], gain_shapes) for p in _adamw(*gains_wmv, gpack, "adamw_gains")]
    results["norm_g"] = [gres[q][0] for q in range(4)]
    results["final_g"] = [gres[q][1] for q in range(4)]
    loss = gres[0][2][0]

    names = ("norm_g", "w_in", "conv_w", "w_out_a", "a_re", "a_im", "log_dt", "b_re", "b_im", "c_re", "c_im",
             "d_skip", "w_glu", "b_glu", "w_out_b", "w_o", "final_g")
    outs = [loss, dx[None]]
    for q in range(4):
        outs += [results[k][q] for k in names]
    return tuple(outs)
```

```python
import functools
import math

import jax
import jax.numpy as jnp
from jax import lax
from jax.experimental import pallas as pl
from jax.experimental.pallas import tpu as pltpu

F32 = jnp.float32
BF16 = jnp.bfloat16
HIGHEST = lax.Precision.HIGHEST

N_DEV = 8
LANES = 128
SUBLANES = 8
VMEM_LIMIT_BYTES = 56 * 1024 * 1024

RMS_EPS = 1e-6
ADAM_LR = 0.001
ADAM_B1 = 0.9
ADAM_B2 = 0.999
ADAM_EPS = 1e-08
ADAM_WD = 0.01
ADAM_STEP = 10
GELU_C0 = math.sqrt(2.0 / math.pi)
GELU_C1 = 0.044715

ADAMW_BLOCK_ELEMS = 1 << 17
PACK_ROWS = 512

S5_GROUP = 16
S5_GB = LANES // S5_GROUP


def _params(*semantics):
    return pltpu.CompilerParams(dimension_semantics=semantics, vmem_limit_bytes=VMEM_LIMIT_BYTES)


ANY_SPEC = pl.BlockSpec(memory_space=pl.ANY)


def _pallas(body, args, deps=(), *, in_specs, **kwargs):
    deps = tuple(deps)
    if not deps:
        return pl.pallas_call(body, in_specs=in_specs, **kwargs)(*args)

    def body_after(*refs):
        body(*refs[len(deps):])

    return pl.pallas_call(body_after, in_specs=[ANY_SPEC] * len(deps) + list(in_specs), **kwargs)(*deps, *args)


def _tile(n, pref):
    t = min(n, pref)
    while n % t:
        assert t % 2 == 0, (n, pref)
        t //= 2
    return t


def _sigmoid(z):
    return 1.0 / (1.0 + jnp.exp(-z))


def _gelu(y):
    return 0.5 * y * (1.0 + jnp.tanh(GELU_C0 * (y + GELU_C1 * y * y * y)))


def _gelu_grad(y):
    t = jnp.tanh(GELU_C0 * (y + GELU_C1 * y * y * y))
    return 0.5 * (1.0 + t) + 0.5 * y * (1.0 - t * t) * GELU_C0 * (1.0 + 3.0 * GELU_C1 * y * y)


HBM_SPEC = pl.BlockSpec(memory_space=pltpu.HBM)
SEM_SPEC = pl.BlockSpec(memory_space=pltpu.SEMAPHORE)
DATAFLOW_EFFECT = pltpu.SideEffectType.DATAFLOW_SIDE_EFFECTING
OTHER_CHIPS = (2, 4, 6)


def _flip(pos, mask):
    x, y, c = pos
    return x ^ ((mask >> 2) & 1), y ^ ((mask >> 1) & 1), c ^ (mask & 1)


def _dev(pos):
    return 4 * pos[0] + 2 * pos[1] + pos[2]


def _chip(pos):
    return 2 * pos[0] + pos[1]


def _plan_gather_chips(me):
    return [(_flip(me, k), None, _dev(me), _dev(_flip(me, k))) for k in (1,) + OTHER_CHIPS]


def _plan_gather_forward(me):
    sib = _flip(me, 1)
    return [(sib, _dev(_flip(me, k)), _dev(_flip(me, k)), _dev(_flip(sib, k))) for k in OTHER_CHIPS]


def _plan_reduce_sibling(me):
    sib = _flip(me, 1)
    return [(sib, 2 * q + sib[2], q, q) for q in range(4)]


def _plan_reduce_chips(me):
    return [(_flip(me, k), _chip(_flip(me, k)), _chip(me), _chip(_flip(me, k))) for k in OTHER_CHIPS]


PLAN_COPIES = {_plan_gather_chips: 4, _plan_gather_forward: 3, _plan_reduce_sibling: 4, _plan_reduce_chips: 3}


def _exchange_copies(plan, src_refs, land_refs, send_sems, recv_sems, incoming=True):
    me = (lax.axis_index("x"), lax.axis_index("y"), lax.axis_index("c"))
    pairs = []
    for b, (src_ref, land_ref) in enumerate(zip(src_refs, land_refs)):
        for j, (peer, src_slot, there, here) in enumerate(plan(me)):
            sem = b * PLAN_COPIES[plan] + j
            src = src_ref if src_slot is None else src_ref.at[src_slot]
            out = pltpu.make_async_remote_copy(
                src_ref=src, dst_ref=land_ref.at[there], send_sem=send_sems.at[sem], recv_sem=recv_sems.at[sem],
                device_id=peer, device_id_type=pl.DeviceIdType.MESH)
            inc = pltpu.make_async_remote_copy(
                src_ref=src, dst_ref=land_ref.at[here], send_sem=send_sems.at[sem], recv_sem=recv_sems.at[sem],
                device_id=peer, device_id_type=pl.DeviceIdType.MESH) if incoming else None
            pairs.append((out, inc))
    return pairs


def _exchange_start(plan, srcs, lands, name, deps=()):
    srcs = [] if srcs is None else list(srcs)
    ns, n, nd = len(srcs), len(lands), len(deps)

    def body(*refs):
        land_refs = refs[ns:ns + n]
        sems_at = ns + n + nd
        pairs = _exchange_copies(plan, refs[:ns] if ns else land_refs, land_refs, refs[sems_at], refs[sems_at + 1],
                                 incoming=False)
        for out, _ in pairs:
            out.start()
        token = refs[-1]
        token[...] = jnp.zeros_like(token)

    sems = pltpu.SemaphoreType.DMA((PLAN_COPIES[plan] * n,))
    bufs = srcs + list(lands)
    outs = pl.pallas_call(
        body, name=name,
        out_shape=(sems, sems, *[pltpu.HBM(a.shape, a.dtype) for a in bufs],
                   jax.ShapeDtypeStruct((SUBLANES, LANES), F32)),
        in_specs=[HBM_SPEC] * (ns + n) + [ANY_SPEC] * nd,
        out_specs=(SEM_SPEC, SEM_SPEC, *[HBM_SPEC] * (ns + n), pl.BlockSpec(memory_space=pltpu.VMEM)),
        input_output_aliases={i: 2 + i for i in range(ns + n)},
        compiler_params=pltpu.CompilerParams(has_side_effects=DATAFLOW_EFFECT),
    )(*[pltpu.with_memory_space_constraint(a, pltpu.HBM) for a in bufs], *deps)
    return (outs[0], outs[1]), (outs[2:2 + ns] if ns else None), outs[2 + ns:2 + ns + n], outs[-1]


def _exchange_wait(plan, sems, srcs, lands, after, name):
    srcs = [] if srcs is None else list(srcs)
    ns, n = len(srcs), len(lands)

    def body(*refs):
        land_refs = refs[ns:ns + n]
        pairs = _exchange_copies(plan, refs[:ns] if ns else land_refs, land_refs, refs[ns + n], refs[ns + n + 1])
        for out, inc in pairs:
            out.wait_send()
            inc.wait_recv()

    bufs = srcs + list(lands)
    outs = pl.pallas_call(
        body, name=name,
        out_shape=[pltpu.HBM(a.shape, a.dtype) for a in bufs],
        in_specs=[HBM_SPEC] * (ns + n) + [SEM_SPEC, SEM_SPEC] + [ANY_SPEC] * len(after),
        out_specs=[HBM_SPEC] * (ns + n),
        input_output_aliases={i: i for i in range(ns + n)},
        compiler_params=pltpu.CompilerParams(has_side_effects=DATAFLOW_EFFECT),
    )(*bufs, sems[0], sems[1], *after)
    return outs[:ns], outs[ns:]


def _landing(own, slots, slot):
    land = lax.empty((slots,) + own.shape, own.dtype)
    return lax.dynamic_update_slice(land, own[None], (slot,) + (0,) * own.ndim)


def _chip_sums(pieces, land, name):
    _, r, c_ = land.shape
    tr = _tile(r, max(2 * SUBLANES, 1 << int(math.log2(4 * ADAMW_BLOCK_ELEMS // c_))))

    def body(core_ref, p_ref, l_ref, o_ref):
        o_ref[...] = (p_ref[...].astype(F32) + l_ref[...].astype(F32)).astype(o_ref.dtype)

    spec = pl.BlockSpec((None, tr, c_), lambda q, i, core: (q, i, 0))
    return pl.pallas_call(
        body, name=name,
        grid_spec=pltpu.PrefetchScalarGridSpec(
            num_scalar_prefetch=1, grid=(4, r // tr),
            in_specs=[pl.BlockSpec((None, tr, c_), lambda q, i, core: (2 * q + core[0], i, 0)), spec],
            out_specs=spec),
        out_shape=jax.ShapeDtypeStruct(land.shape, land.dtype),
        compiler_params=_params("parallel", "parallel"),
    )(lax.axis_index("c").reshape(1), pieces, land)


def _mm(a, b, *, name, nt=False, ta=False, out_dtype=F32, add=None, split_n=None, tm=512, tn=1024, deps=()):
    k, m = a.shape if ta else a.shape[::-1]
    n = b.shape[0] if nt else b.shape[1]
    tm = _tile(m, tm)
    tn = n // split_n if split_n else _tile(n, tn)
    dims = (((0 if ta else 1,), (1 if nt else 0,)), ((), ()))

    def body(*refs):
        a_ref, b_ref = refs[0], refs[1]
        o_ref = refs[-1]
        acc = lax.dot_general(a_ref[...], b_ref[...], dims, preferred_element_type=F32)
        if add is not None:
            acc = acc + refs[2][...]
        o_ref[...] = acc.astype(o_ref.dtype)

    in_specs = [pl.BlockSpec((k, tm), lambda i, j: (0, i)) if ta else pl.BlockSpec((tm, k), lambda i, j: (i, 0)),
                pl.BlockSpec((tn, k), lambda i, j: (j, 0)) if nt
                else pl.BlockSpec((k, tn), lambda i, j: (0, j))]
    args = [a, b]
    if add is not None:
        in_specs.append(pl.BlockSpec((tm, tn), lambda i, j: (i, j)))
        args.append(add)
    if split_n:
        out_shape = jax.ShapeDtypeStruct((split_n, m, tn), out_dtype)
        out_spec = pl.BlockSpec((None, tm, tn), lambda i, j: (j, i, 0))
    else:
        out_shape = jax.ShapeDtypeStruct((m, n), out_dtype)
        out_spec = pl.BlockSpec((tm, tn), lambda i, j: (i, j))
    return _pallas(
        body, args, deps, name=name, grid=(m // tm, n // tn), in_specs=in_specs, out_specs=out_spec,
        out_shape=out_shape, compiler_params=_params("parallel", "parallel"))


def _mm_win_fwd(h, w_g, name, deps=()):
    m, k = h.shape
    nj = w_g.shape[2]
    tm = _tile(m, 512)

    def body(a_ref, b_ref, o_ref):
        o_ref[...] = jnp.dot(a_ref[...], b_ref[...], preferred_element_type=F32).astype(o_ref.dtype)

    return _pallas(
        body, [h, w_g], deps, name=name, grid=(N_DEV, m // tm),
        in_specs=[pl.BlockSpec((tm, k), lambda j, i: (i, 0)),
                  pl.BlockSpec((None, k, nj), lambda j, i: (j, 0, 0))],
        out_specs=pl.BlockSpec((tm, nj), lambda j, i: (i, j)),
        out_shape=jax.ShapeDtypeStruct((m, N_DEV * nj), BF16),
        compiler_params=_params("parallel", "parallel"))


def _mm_win_bwd(dproj, w_g, name, deps=()):
    m = dproj.shape[0]
    d, nj = w_g.shape[1], w_g.shape[2]
    tm = _tile(m, 512)
    tn = _tile(d, 1024)

    per_step = 2

    def body(a_ref, b_ref, o_ref, acc_ref):
        j = pl.program_id(2)

        @pl.when(j == 0)
        def _():
            acc_ref[...] = jnp.zeros_like(acc_ref)

        part = None
        for k in range(per_step):
            term = lax.dot_general(a_ref[:, k * nj:(k + 1) * nj], b_ref[k], (((1,), (1,)), ((), ())),
                                   preferred_element_type=F32)
            part = term if part is None else part + term
        acc_ref[...] += part

        @pl.when(j == N_DEV // per_step - 1)
        def _():
            o_ref[...] = acc_ref[...]

    return _pallas(
        body, [dproj, w_g], deps, name=name, grid=(m // tm, d // tn, N_DEV // per_step),
        in_specs=[pl.BlockSpec((tm, per_step * nj), lambda i, n, j: (i, j)),
                  pl.BlockSpec((per_step, tn, nj), lambda i, n, j: (j, n, 0))],
        out_specs=pl.BlockSpec((tm, tn), lambda i, n, j: (i, n)),
        out_shape=jax.ShapeDtypeStruct((m, d), F32),
        scratch_shapes=[pltpu.VMEM((tm, tn), F32)],
        compiler_params=_params("parallel", "parallel", "arbitrary"))


def _row_spec(tr, w, col):
    return pl.BlockSpec((tr, w), lambda i: (i, col))


def _full_spec(shape):
    return pl.BlockSpec(shape, lambda i: (0,) * len(shape))


def _rmsnorm_fwd(x, g, name, deps=()):
    l, d = x.shape
    tr = _tile(l, 256)

    def body(x_ref, g_ref, o_ref):
        xv = x_ref[...]
        rstd = lax.rsqrt(jnp.mean(xv * xv, axis=-1, keepdims=True) + RMS_EPS)
        o_ref[...] = (xv * rstd * g_ref[...]).astype(o_ref.dtype)

    return _pallas(
        body, [x, g.reshape(1, d)], deps, name=name, grid=(l // tr,),
        in_specs=[_row_spec(tr, d, 0), _full_spec((1, d))],
        out_specs=_row_spec(tr, d, 0),
        out_shape=jax.ShapeDtypeStruct((l, d), BF16),
        compiler_params=_params("parallel"))


def _rmsnorm_bwd(x, g, dh, dxo, name, deps=()):
    l, d = x.shape
    tr = _tile(l, 256)

    def body(x_ref, g_ref, dh_ref, dxo_ref, dx_ref, dg_ref):
        xv = x_ref[...]
        rstd = lax.rsqrt(jnp.mean(xv * xv, axis=-1, keepdims=True) + RMS_EPS)
        dhv = dh_ref[...]
        gdy = dhv * g_ref[...]
        dot = jnp.mean(gdy * xv, axis=-1, keepdims=True)
        dx_ref[...] = dxo_ref[...] + rstd * gdy - xv * (rstd * rstd * rstd * dot)

        @pl.when(pl.program_id(0) == 0)
        def _():
            dg_ref[...] = jnp.zeros_like(dg_ref)

        dg_ref[...] += jnp.sum(dhv * xv * rstd, axis=0, keepdims=True)

    return _pallas(
        body, [x, g.reshape(1, d), dh, dxo], deps, name=name, grid=(l // tr,),
        in_specs=[_row_spec(tr, d, 0), _full_spec((1, d)), _row_spec(tr, d, 0), _row_spec(tr, d, 0)],
        out_specs=[_row_spec(tr, d, 0), _full_spec((1, d))],
        out_shape=[jax.ShapeDtypeStruct((l, d), F32), jax.ShapeDtypeStruct((1, d), F32)],
        compiler_params=_params("arbitrary"))


def _final_loss(x, g, tgt, name):
    l, d = x.shape
    tr = _tile(l, 256)

    def body(x_ref, g_ref, t_ref, dx_ref, dg_ref, loss_ref):
        xv = x_ref[...]
        gv = g_ref[...]
        rstd = lax.rsqrt(jnp.mean(xv * xv, axis=-1, keepdims=True) + RMS_EPS)
        xn = xv * rstd
        err = xn * gv - t_ref[...]
        dy = err * (1.0 / d)
        gdy = dy * gv
        dot = jnp.mean(gdy * xv, axis=-1, keepdims=True)
        dx_ref[...] = rstd * gdy - xv * (rstd * rstd * rstd * dot)

        @pl.when(pl.program_id(0) == 0)
        def _():
            dg_ref[...] = jnp.zeros_like(dg_ref)
            loss_ref[...] = jnp.zeros_like(loss_ref)

        dg_ref[...] += jnp.sum(dy * xn, axis=0, keepdims=True)
        loss_ref[...] += (0.5 / d) * jnp.sum(err * err)

    return pl.pallas_call(
        body, name=name, grid=(l // tr,),
        in_specs=[_row_spec(tr, d, 0), _full_spec((1, d)), _row_spec(tr, d, 0)],
        out_specs=[_row_spec(tr, d, 0), _full_spec((1, d)), _full_spec((SUBLANES, LANES))],
        out_shape=[jax.ShapeDtypeStruct((l, d), F32), jax.ShapeDtypeStruct((1, d), F32),
                   jax.ShapeDtypeStruct((SUBLANES, LANES), F32)],
        compiler_params=_params("arbitrary"),
    )(x, g.reshape(1, d), tgt)


HALO = 2 * SUBLANES


def _halo_spec(tr, w, col, nblk, before):
    step = tr // HALO
    if before:
        return pl.BlockSpec((HALO, w), lambda i: (jnp.maximum(i * step - 1, 0), col))
    return pl.BlockSpec((HALO, w), lambda i: (jnp.minimum((i + 1) * step, nblk - 1), col))


def _shift_down(cur, before, k):
    ext = jnp.concatenate([before, cur], axis=0)
    return pltpu.roll(ext, k, axis=0)[HALO:, :]


def _shift_up(cur, after, k):
    tr = cur.shape[0]
    ext = jnp.concatenate([cur, after], axis=0)
    return pltpu.roll(ext, tr + HALO - k, axis=0)[:tr, :]


def _f32(ref):
    return ref[...].astype(F32)


def _branch_a_fwd(proj, conv_w, d, name, deps=()):
    l = proj.shape[0]
    tr = _tile(l, 256)
    nblk8 = l // HALO

    def body(v_ref, bg_ref, cg_ref, za_ref, vh_ref, cgh_ref, w_ref, o_ref):
        first = pl.program_id(0) == 0
        cv = _f32(cg_ref) * _f32(v_ref)
        cvh = jnp.where(first, 0.0, _f32(cgh_ref) * _f32(vh_ref))
        w0, w1, w2 = w_ref[0:1, :], w_ref[1:2, :], w_ref[2:3, :]
        q = w2 * cv + w1 * _shift_down(cv, cvh, 1) + w0 * _shift_down(cv, cvh, 2)
        za = _f32(za_ref)
        o_ref[...] = (_f32(bg_ref) * q * (za * _sigmoid(za))).astype(o_ref.dtype)

    return _pallas(
        body, [proj, proj, proj, proj, proj, proj, conv_w], deps, name=name, grid=(l // tr,),
        in_specs=[_row_spec(tr, d, 0), _row_spec(tr, d, 1), _row_spec(tr, d, 2), _row_spec(tr, d, 3),
                  _halo_spec(tr, d, 0, nblk8, True), _halo_spec(tr, d, 2, nblk8, True),
                  _full_spec((SUBLANES, d))],
        out_specs=_row_spec(tr, d, 0),
        out_shape=jax.ShapeDtypeStruct((l, d), BF16),
        compiler_params=_params("parallel"))


def _branch_a_bwd(proj, dpa, conv_w, dproj, d, name):
    l = proj.shape[0]
    tr = _tile(l, 128)
    nblk8 = l // HALO
    ntiles = l // tr

    def body(v_ref, bg_ref, cg_ref, za_ref, dpa_ref, vh_ref, cgh_ref, bgn_ref, zan_ref, dpan_ref,
             w_ref, _, o_ref, dw0_ref, dw1_ref, dw2_ref):
        dv_ref, dbg_ref, dcg_ref, dza_ref = [o_ref.at[:, pl.ds(k * d, d)] for k in range(4)]
        i = pl.program_id(0)
        v, bg, cg, za, dpa_v = _f32(v_ref), _f32(bg_ref), _f32(cg_ref), _f32(za_ref), _f32(dpa_ref)
        w0, w1, w2 = w_ref[0:1, :], w_ref[1:2, :], w_ref[2:3, :]
        cv = cg * v
        cvh = jnp.where(i == 0, 0.0, _f32(cgh_ref) * _f32(vh_ref))
        cv1 = _shift_down(cv, cvh, 1)
        cv2 = _shift_down(cv, cvh, 2)
        q = w2 * cv + w1 * cv1 + w0 * cv2
        sg = _sigmoid(za)
        s = za * sg
        dbg_ref[...] = (dpa_v * q * s).astype(dbg_ref.dtype)
        dza_ref[...] = (dpa_v * bg * q * (sg * (1.0 + za * (1.0 - sg)))).astype(dza_ref.dtype)
        dq = dpa_v * bg * s
        zan = _f32(zan_ref)
        dqn = jnp.where(i == ntiles - 1, 0.0, _f32(dpan_ref) * _f32(bgn_ref) * (zan * _sigmoid(zan)))
        dcv = w2 * dq + w1 * _shift_up(dq, dqn, 1) + w0 * _shift_up(dq, dqn, 2)
        dcg_ref[...] = (dcv * v).astype(dcg_ref.dtype)
        dv_ref[...] = (dcv * cg).astype(dv_ref.dtype)

        @pl.when(i == 0)
        def _():
            dw0_ref[...] = jnp.zeros_like(dw0_ref)
            dw1_ref[...] = jnp.zeros_like(dw1_ref)
            dw2_ref[...] = jnp.zeros_like(dw2_ref)

        dw0_ref[...] += jnp.sum(dq * cv2, axis=0, keepdims=True)
        dw1_ref[...] += jnp.sum(dq * cv1, axis=0, keepdims=True)
        dw2_ref[...] += jnp.sum(dq * cv, axis=0, keepdims=True)

    wsum = jax.ShapeDtypeStruct((1, d), F32)
    return pl.pallas_call(
        body, name=name, grid=(ntiles,),
        in_specs=[_row_spec(tr, d, 0), _row_spec(tr, d, 1), _row_spec(tr, d, 2), _row_spec(tr, d, 3),
                  _row_spec(tr, d, 0),
                  _halo_spec(tr, d, 0, nblk8, True), _halo_spec(tr, d, 2, nblk8, True),
                  _halo_spec(tr, d, 1, nblk8, False), _halo_spec(tr, d, 3, nblk8, False),
                  _halo_spec(tr, d, 0, nblk8, False),
                  _full_spec((SUBLANES, d)), ANY_SPEC],
        out_specs=[_row_spec(tr, 4 * d, 0)] + [_full_spec((1, d))] * 3,
        out_shape=[jax.ShapeDtypeStruct(dproj.shape, dproj.dtype)] + [wsum] * 3,
        input_output_aliases={11: 0},
        compiler_params=_params("arbitrary"),
    )(proj, proj, proj, proj, dpa, proj, proj, proj, proj, dpa, conv_w, dproj)


def _gelu_cast(y, name):
    l, w = y.shape
    tr = _tile(l, 512)

    def body(y_ref, o_ref):
        o_ref[...] = _gelu(y_ref[...]).astype(o_ref.dtype)

    return pl.pallas_call(
        body, name=name, grid=(l // tr,), in_specs=[_row_spec(tr, w, 0)],
        out_specs=_row_spec(tr, w, 0), out_shape=jax.ShapeDtypeStruct((l, w), BF16),
        compiler_params=_params("parallel"),
    )(y)


def _glu_post(y, gl, proj, b_glu, zb_col, name):
    l, w = y.shape
    tr = _tile(l, 512)

    def body(y_ref, gl_ref, zb_ref, b_ref, o_ref):
        zb = _f32(zb_ref)
        o_ref[...] = (_gelu(y_ref[...]) * _sigmoid(_f32(gl_ref) + b_ref[...])
                      * (zb * _sigmoid(zb))).astype(o_ref.dtype)

    return pl.pallas_call(
        body, name=name, grid=(l // tr,),
        in_specs=[_row_spec(tr, w, 0), _row_spec(tr, w, 0), _row_spec(tr, w, zb_col), _full_spec((1, w))],
        out_specs=_row_spec(tr, w, 0), out_shape=jax.ShapeDtypeStruct((l, w), BF16),
        compiler_params=_params("parallel"),
    )(y, gl, proj, b_glu.reshape(1, w))


def _glu_bwd1(y, gl, proj, b_glu, dpb, dproj, zb_col, name):
    l, w = y.shape
    tr = _tile(l, 512)

    def body(y_ref, gl_ref, zb_ref, b_ref, dpb_ref, _, dzb_ref, dgl_ref, t_ref, db_ref):
        zb = _f32(zb_ref)
        dpb_v = _f32(dpb_ref)
        yg = _gelu(y_ref[...])
        sgl = _sigmoid(_f32(gl_ref) + b_ref[...])
        szb = _sigmoid(zb)
        dzb_ref[...] = (dpb_v * yg * sgl * (szb * (1.0 + zb * (1.0 - szb)))).astype(dzb_ref.dtype)
        e = dpb_v * (zb * szb)
        dgl = e * yg * sgl * (1.0 - sgl)
        dgl_ref[...] = dgl.astype(dgl_ref.dtype)
        t_ref[...] = e * sgl

        @pl.when(pl.program_id(0) == 0)
        def _():
            db_ref[...] = jnp.zeros_like(db_ref)

        db_ref[...] += jnp.sum(dgl, axis=0, keepdims=True)

    return pl.pallas_call(
        body, name=name, grid=(l // tr,),
        in_specs=[_row_spec(tr, w, 0), _row_spec(tr, w, 0), _row_spec(tr, w, zb_col), _full_spec((1, w)),
                  _row_spec(tr, w, 0), ANY_SPEC],
        out_specs=[_row_spec(tr, w, zb_col)] + [_row_spec(tr, w, 0)] * 2 + [_full_spec((1, w))],
        out_shape=[jax.ShapeDtypeStruct(dproj.shape, dproj.dtype), jax.ShapeDtypeStruct((l, w), BF16),
                   jax.ShapeDtypeStruct((l, w), F32), jax.ShapeDtypeStruct((1, w), F32)],
        input_output_aliases={5: 0},
        compiler_params=_params("arbitrary"),
    )(y, gl, proj, b_glu.reshape(1, w), dpb, dproj)


def _write_cols(dproj, cols, col, name):
    l, w = cols.shape
    tr = _tile(l, 512)

    def body(c_ref, _, o_ref):
        o_ref[...] = c_ref[...].astype(o_ref.dtype)

    return pl.pallas_call(
        body, name=name, grid=(l // tr,), in_specs=[_row_spec(tr, w, 0), ANY_SPEC],
        out_specs=_row_spec(tr, w, col), out_shape=jax.ShapeDtypeStruct(dproj.shape, dproj.dtype),
        input_output_aliases={1: 0}, compiler_params=_params("parallel"),
    )(cols, dproj)


def _glu_bwd2(y, t1, dyg2, name, deps=()):
    l, w = y.shape
    tr = _tile(l, 512)

    def body(y_ref, t_ref, d_ref, o_ref):
        o_ref[...] = (t_ref[...] + _f32(d_ref)) * _gelu_grad(y_ref[...])

    return _pallas(
        body, [y, t1, dyg2], deps, name=name, grid=(l // tr,), in_specs=[_row_spec(tr, w, 0)] * 3,
        out_specs=_row_spec(tr, w, 0), out_shape=jax.ShapeDtypeStruct((l, w), F32),
        compiler_params=_params("parallel"))


def _merge_fwd(proj, ya, yb, d, ga_col, gb_col, name):
    l = proj.shape[0]
    tr = _tile(l, 256)

    def body(ga_ref, gb_ref, ya_ref, yb_ref, o_ref):
        o_ref[...] = (_sigmoid(_f32(ga_ref)) * _f32(ya_ref)
                      + _sigmoid(_f32(gb_ref)) * _f32(yb_ref)).astype(o_ref.dtype)

    return pl.pallas_call(
        body, name=name, grid=(l // tr,),
        in_specs=[_row_spec(tr, d, ga_col), _row_spec(tr, d, gb_col), _row_spec(tr, d, 0), _row_spec(tr, d, 0)],
        out_specs=_row_spec(tr, d, 0), out_shape=jax.ShapeDtypeStruct((l, d), BF16),
        compiler_params=_params("parallel"),
    )(proj, proj, ya, yb)


def _merge_bwd(proj, ya, yb, dm, d, ga_col, name):
    l, n = proj.shape
    tr = _tile(l, 256)

    def body(g_ref, ya_ref, yb_ref, dm_ref, dy_ref, dg_ref):
        dmv = _f32(dm_ref)
        sg = _sigmoid(_f32(g_ref))
        yv = jnp.where(pl.program_id(1) == 0, _f32(ya_ref), _f32(yb_ref))
        dy_ref[...] = (dmv * sg).astype(dy_ref.dtype)
        dg_ref[...] = (dmv * yv * sg * (1.0 - sg)).astype(dg_ref.dtype)

    row = pl.BlockSpec((tr, d), lambda i, j: (i, 0))
    return pl.pallas_call(
        body, name=name, grid=(l // tr, 2),
        in_specs=[pl.BlockSpec((tr, d), lambda i, j: (i, ga_col + j)), row, row, row],
        out_specs=[pl.BlockSpec((None, tr, d), lambda i, j: (j, i, 0)),
                   pl.BlockSpec((tr, d), lambda i, j: (i, ga_col + j))],
        out_shape=[jax.ShapeDtypeStruct((2, l, d), BF16), jax.ShapeDtypeStruct((l, n), BF16)],
        compiler_params=_params("parallel", "arbitrary"),
    )(proj, ya, yb, dm)


def _to_segments(a):
    l, w = a.shape
    return a.reshape(SUBLANES, l // SUBLANES, w).transpose(1, 0, 2).reshape(l, w)


def _from_segments(a):
    l, w = a.shape
    return a.reshape(l // SUBLANES, SUBLANES, w).transpose(1, 0, 2).reshape(l, w)


def _dense(z, shape):
    return jnp.broadcast_to(z, shape).reshape(-1, LANES)


def _s5_disc(are, aim, ldt):
    dt = jnp.exp(ldt)
    er = jnp.exp(are * dt)
    lbr = er * jnp.cos(aim * dt)
    lbi = er * jnp.sin(aim * dt)
    inv = 1.0 / (are * are + aim * aim)
    fr = ((lbr - 1.0) * are + lbi * aim) * inv
    fi = (lbi * are - (lbr - 1.0) * aim) * inv
    return dt, lbr, lbi, inv, fr, fi


def _s5_params(are, aim, ldt, bre, bim, name, deps=()):
    shape = are.shape

    def body(are_ref, aim_ref, ldt_ref, bre_ref, bim_ref, lbr_ref, lbi_ref, bbr_ref, bbi_ref):
        _, lbr, lbi, _, fr, fi = _s5_disc(are_ref[...], aim_ref[...], ldt_ref[...])
        lbr_ref[...] = lbr
        lbi_ref[...] = lbi
        bbr_ref[...] = fr * bre_ref[...] - fi * bim_ref[...]
        bbi_ref[...] = fr * bim_ref[...] + fi * bre_ref[...]

    out = jax.ShapeDtypeStruct(shape, F32)
    return _pallas(body, [are, aim, ldt, bre, bim], deps, name=name,
                   in_specs=[pl.BlockSpec(memory_space=pltpu.VMEM)] * 5, out_shape=[out] * 4,
                   compiler_params=pltpu.CompilerParams(vmem_limit_bytes=VMEM_LIMIT_BYTES))


def _s5_params_bwd(are, aim, ldt, bre, bim, glbr, glbi, gbbr, gbbi, n_groups, name, deps=()):
    shape = are.shape
    rows_per_group = shape[0] // n_groups

    def body(are_ref, aim_ref, ldt_ref, bre_ref, bim_ref, glbr_ref, glbi_ref, gbbr_ref, gbbi_ref,
             gar_ref, gai_ref, gdt_ref, gbr_ref, gbi_ref):
        are_v, aim_v = are_ref[...], aim_ref[...]
        bre_v, bim_v = bre_ref[...], bim_ref[...]
        gbbr_v, gbbi_v = gbbr_ref[...], gbbi_ref[...]
        dt, lbr, lbi, inv, fr, fi = _s5_disc(are_v, aim_v, ldt_ref[...])
        gbr_ref[...] = fr * gbbr_v + fi * gbbi_v
        gbi_ref[...] = fr * gbbi_v - fi * gbbr_v
        lane_group = lax.broadcasted_iota(jnp.int32, (LANES, LANES), 0) // S5_GROUP
        same_group = (lane_group == lax.broadcasted_iota(jnp.int32, (LANES, LANES), 1) // S5_GROUP)
        ones = same_group.astype(F32)
        gfr = jnp.dot(bre_v * gbbr_v + bim_v * gbbi_v, ones, precision=HIGHEST, preferred_element_type=F32)
        gfi = jnp.dot(bre_v * gbbi_v - bim_v * gbbr_v, ones, precision=HIGHEST, preferred_element_type=F32)
        glr = glbr_ref[...] + (are_v * gfr - aim_v * gfi) * inv
        gli = glbi_ref[...] + (are_v * gfi + aim_v * gfr) * inv
        qr = (fr * are_v + fi * aim_v) * inv
        qi = (fi * are_v - fr * aim_v) * inv
        gzr = lbr * glr + lbi * gli
        gzi = lbr * gli - lbi * glr
        gar_ref[...] = dt * gzr - (qr * gfr + qi * gfi)
        gai_ref[...] = dt * gzi - (qr * gfi - qi * gfr)
        e = dt * (are_v * gzr + aim_v * gzi)
        per_group = jnp.sum(e.reshape(n_groups, rows_per_group, LANES), axis=1)
        total = jnp.sum(per_group, axis=1, keepdims=True) * (1.0 / S5_GROUP)
        gdt_ref[...] = jnp.broadcast_to(total, gdt_ref.shape)

    out = jax.ShapeDtypeStruct(shape, F32)
    return _pallas(
        body, [are, aim, ldt, bre, bim, glbr, glbi, gbbr, gbbi], deps, name=name,
        in_specs=[pl.BlockSpec(memory_space=pltpu.VMEM)] * 9,
        out_shape=[out, out, jax.ShapeDtypeStruct((n_groups, LANES), F32), out, out],
        compiler_params=pltpu.CompilerParams(vmem_limit_bytes=VMEM_LIMIT_BYTES))


def _cmul(ar, ai, br, bi):
    return ar * br - ai * bi, ar * bi + ai * br


def _scan_in_place(hr_ref, hi_ref, lr, li, reverse):
    l, wb = hr_ref.shape
    nt = l // SUBLANES
    shift = SUBLANES - 1 if reverse else 1
    unroll = 8 if nt % 8 == 0 else 1

    def rows(k):
        t = (nt - 1 - k) if reverse else k
        return pl.ds(pl.multiple_of(t * SUBLANES, SUBLANES), SUBLANES)

    zero = jnp.zeros((SUBLANES, wb), F32)

    def local_step(k, carry):
        hr, hi = carry
        r = rows(k)
        tr_, ti_ = _cmul(lr, li, hr, hi)
        hr, hi = tr_ + hr_ref[r, :], ti_ + hi_ref[r, :]
        hr_ref[r, :] = hr
        hi_ref[r, :] = hi
        return hr, hi

    er, ei = lax.fori_loop(0, nt, local_step, (zero, zero), unroll=unroll)

    lnr = lni = None
    br, bi, n = lr, li, nt
    while n:
        if n & 1:
            lnr, lni = (br, bi) if lnr is None else _cmul(lnr, lni, br, bi)
        n >>= 1
        if n:
            br, bi = _cmul(br, bi, br, bi)

    row = lax.broadcasted_iota(jnp.int32, (SUBLANES, wb), 0)
    tr_, ti_ = er, ei
    for j in range(1, SUBLANES):
        pr_, pi_ = _cmul(lnr, lni, pltpu.roll(tr_, shift, axis=0), pltpu.roll(ti_, shift, axis=0))
        at = row == ((SUBLANES - 1 - j) if reverse else j)
        tr_ = jnp.where(at, er + pr_, tr_)
        ti_ = jnp.where(at, ei + pi_, ti_)
    edge = row == ((SUBLANES - 1) if reverse else 0)
    cr = jnp.where(edge, 0.0, pltpu.roll(tr_, shift, axis=0))
    ci = jnp.where(edge, 0.0, pltpu.roll(ti_, shift, axis=0))

    def fix_step(k, carry):
        zr, zi = _cmul(lr, li, *carry)
        r = rows(k)
        hr_ref[r, :] = hr_ref[r, :] + zr
        hi_ref[r, :] = hi_ref[r, :] + zi
        return zr, zi

    lax.fori_loop(0, nt, fix_step, (cr, ci), unroll=unroll)


def _dot(a, b):
    return jnp.dot(a.astype(BF16), b.astype(BF16), preferred_element_type=F32)


def _s5_forward(u_seg, mb_re, mb_im, mc_re, mc_im, lam_re, lam_im, dvec, name):
    l = u_seg.shape[0]
    nb, kin, kst = mb_re.shape

    def body(u_ref, mbr_ref, mbi_ref, mcr_ref, mci_ref, lr_ref, li_ref, d_ref, hr_ref, hi_ref, y_ref):
        u = u_ref[...]
        hr_ref[...] = _dot(u, mbr_ref[...])
        hi_ref[...] = _dot(u, mbi_ref[...])
        _scan_in_place(hr_ref, hi_ref, jnp.broadcast_to(lr_ref[...], (SUBLANES, kst)),
                       jnp.broadcast_to(li_ref[...], (SUBLANES, kst)), False)
        y_ref[...] = (_dot(hr_ref[...], mcr_ref[...]) - _dot(hi_ref[...], mci_ref[...])
                      + d_ref[...] * u.astype(F32))

    act = pl.BlockSpec((l, kin), lambda b: (0, b))
    state = pl.BlockSpec((l, kst), lambda b: (0, b))
    up = pl.BlockSpec((None, kin, kst), lambda b: (b, 0, 0))
    down = pl.BlockSpec((None, kst, kin), lambda b: (b, 0, 0))
    hshape = jax.ShapeDtypeStruct((l, nb * kst), F32)
    return pl.pallas_call(
        body, name=name, grid=(nb,),
        in_specs=[act, up, up, down, down, pl.BlockSpec((1, kst), lambda b: (0, b)),
                  pl.BlockSpec((1, kst), lambda b: (0, b)), pl.BlockSpec((1, kin), lambda b: (0, b))],
        out_specs=[state, state, act],
        out_shape=[hshape, hshape, jax.ShapeDtypeStruct((l, nb * kin), F32)],
        compiler_params=_params("parallel"),
    )(u_seg, mb_re, mb_im, mc_re, mc_im, lam_re, lam_im, dvec)


def _dot_ta(a, b):
    return lax.dot_general(a.astype(BF16), b.astype(BF16), (((0,), (0,)), ((), ())), preferred_element_type=F32)


def _dot_nt(a, b):
    return lax.dot_general(a.astype(BF16), b.astype(BF16), (((1,), (1,)), ((), ())), preferred_element_type=F32)


def _s5_backward(dy_seg, u_seg, h_re, h_im, mb_re, mb_im, mc_re, mc_im, lam_re, lam_im_neg, dvec, name):
    l = dy_seg.shape[0]
    nb, kin, kst = mb_re.shape
    nt = l // SUBLANES

    def body(dy_ref, u_ref, hr_ref, hi_ref, mbr_ref, mbi_ref, mcr_ref, mci_ref, lr_ref, li_ref,
             d_ref, du_ref, gcr_ref, gci_ref, gbr_ref, gbi_ref, glr_ref, gli_ref, dsk_ref, qr_ref, qi_ref):
        dy = dy_ref[...]
        qr_ref[...] = _dot_nt(dy, mcr_ref[...])
        qi_ref[...] = -_dot_nt(dy, mci_ref[...])
        _scan_in_place(qr_ref, qi_ref, jnp.broadcast_to(lr_ref[...], (SUBLANES, kst)),
                       jnp.broadcast_to(li_ref[...], (SUBLANES, kst)), True)
        du_ref[...] = _dot_nt(qr_ref[...], mbr_ref[...]) + _dot_nt(qi_ref[...], mbi_ref[...]) + d_ref[...] * dy
        dsk_ref[...] = jnp.sum(dy * _f32(u_ref), axis=0, keepdims=True)
        gcr_ref[...] = _dot_ta(dy, hr_ref[...])
        gci_ref[...] = _dot_ta(dy, hi_ref[...])
        gbr_ref[...] = _dot_ta(u_ref[...], qr_ref[...])
        gbi_ref[...] = _dot_ta(u_ref[...], qi_ref[...])

        row = lax.broadcasted_iota(jnp.int32, (SUBLANES, kst), 0)
        last = pl.ds((nt - 1) * SUBLANES, SUBLANES)
        first = pl.ds(0, SUBLANES)
        pr = jnp.where(row == 0, 0.0, pltpu.roll(hr_ref[last, :], 1, axis=0))
        pi = jnp.where(row == 0, 0.0, pltpu.roll(hi_ref[last, :], 1, axis=0))
        gr, gi = qr_ref[first, :], qi_ref[first, :]

        def step(t, carry):
            acc_r, acc_i = carry
            cur = pl.ds(pl.multiple_of(t * SUBLANES, SUBLANES), SUBLANES)
            prev = pl.ds(pl.multiple_of((t - 1) * SUBLANES, SUBLANES), SUBLANES)
            gr, gi = qr_ref[cur, :], qi_ref[cur, :]
            pr, pi = hr_ref[prev, :], hi_ref[prev, :]
            return acc_r + gr * pr + gi * pi, acc_i + gi * pr - gr * pi

        acc_r, acc_i = lax.fori_loop(1, nt, step, (gr * pr + gi * pi, gi * pr - gr * pi))
        glr_ref[...] = jnp.sum(acc_r, axis=0, keepdims=True)
        gli_ref[...] = jnp.sum(acc_i, axis=0, keepdims=True)

    act = pl.BlockSpec((l, kin), lambda b: (0, b))
    state = pl.BlockSpec((l, kst), lambda b: (0, b))
    up = pl.BlockSpec((None, kin, kst), lambda b: (b, 0, 0))
    down = pl.BlockSpec((None, kst, kin), lambda b: (b, 0, 0))
    vec_st = pl.BlockSpec((1, kst), lambda b: (0, b))
    vec_in = pl.BlockSpec((1, kin), lambda b: (0, b))
    outer = jax.ShapeDtypeStruct((nb, kin, kst), F32)
    lam_shape = jax.ShapeDtypeStruct((1, nb * kst), F32)
    return pl.pallas_call(
        body, name=name, grid=(nb,),
        in_specs=[act, act, state, state, up, up, down, down, vec_st, vec_st, vec_in],
        out_specs=[act, up, up, up, up, vec_st, vec_st, vec_in],
        out_shape=[jax.ShapeDtypeStruct((l, nb * kin), F32), outer, outer, outer, outer, lam_shape, lam_shape,
                   jax.ShapeDtypeStruct((1, nb * kin), F32)],
        scratch_shapes=[pltpu.VMEM((l, kst), F32), pltpu.VMEM((l, kst), F32)],
        compiler_params=_params("parallel"),
    )(dy_seg, u_seg, h_re, h_im, mb_re, mb_im, mc_re, mc_im, lam_re, lam_im_neg, dvec)


def _block_diag(m, nb):
    g, r, s = m.shape
    gb = g // nb
    eye = jnp.eye(gb, dtype=m.dtype)
    out = m.reshape(nb, gb, r, 1, s) * eye[None, :, None, :, None]
    return out.reshape(nb, gb * r, gb * s)


def _block_diag_extract(mat, g, r, s):
    nb = mat.shape[0]
    gb = g // nb
    eye = jnp.eye(gb, dtype=mat.dtype)
    m5 = mat.reshape(nb, gb, r, gb, s) * eye[None, :, None, :, None]
    return jnp.sum(m5, axis=3).reshape(g, r, s)


def _adamw(w, m, v, gslots, name, layer=0, prev=None, deps=()):
    layers, r, c = w.shape
    s = gslots.shape[0]
    tr = _tile(r, max(SUBLANES, 1 << int(math.log2(ADAMW_BLOCK_ELEMS // c))))
    bc1 = 1.0 / (1.0 - ADAM_B1 ** ADAM_STEP)
    bc2 = 1.0 / (1.0 - ADAM_B2 ** ADAM_STEP)

    def body(w_ref, m_ref, v_ref, g_ref, *rest):
        go_ref, d_ref, mo_ref, vo_ref = rest[-4:]
        g = g_ref[0].astype(F32)
        for k in range(1, s):
            g = g + g_ref[k].astype(F32)
        mn = ADAM_B1 * m_ref[...] + (1.0 - ADAM_B1) * g
        vn = ADAM_B2 * v_ref[...] + (1.0 - ADAM_B2) * (g * g)
        go_ref[...] = g
        mo_ref[...] = mn
        vo_ref[...] = vn
        d_ref[...] = -ADAM_LR * ((mn * bc1) / (jnp.sqrt(vn * bc2) + ADAM_EPS) + ADAM_WD * w_ref[...])

    spec = pl.BlockSpec((None, tr, c), lambda i: (layer, i, 0))
    out = jax.ShapeDtypeStruct((layers, r, c), F32)
    in_specs = [spec, spec, spec, pl.BlockSpec((s, tr, c), lambda i: (0, i, 0))]
    args = [w, m, v, gslots]
    aliases = {}
    if prev is not None:
        in_specs += [ANY_SPEC] * 4
        args += list(prev)
        aliases = {4 + q: q for q in range(4)}
    in_specs += [ANY_SPEC] * len(deps)
    args += list(deps)
    return pl.pallas_call(
        body, name=name, grid=(r // tr,), in_specs=in_specs,
        out_specs=[spec] * 4, out_shape=[out] * 4, input_output_aliases=aliases,
        compiler_params=_params("parallel"),
    )(*args)


def _pack(parts):
    flat = jnp.concatenate([p.reshape(-1) for p in parts])
    pad = (-flat.shape[0]) % (PACK_ROWS * LANES)
    return jnp.pad(flat, (0, pad)).reshape(-1, LANES)


def _unpack(packed, shapes):
    flat = packed.reshape(-1)
    out, off = [], 0
    for shp in shapes:
        size = math.prod(shp)
        out.append(flat[off:off + size].reshape(shp))
        off += size
    return out


def kernel(x, norm_g, w_in, conv_w, w_out_a, a_re, a_im, log_dt, b_re, b_im, c_re, c_im, d_skip, w_glu, b_glu, w_out_b, w_o, final_g, loss_target, m_norm_g, m_w_in, m_conv_w, m_w_out_a, m_a_re, m_a_im, m_log_dt, m_b_re, m_b_im, m_c_re, m_c_im, m_d_skip, m_w_glu, m_b_glu, m_w_out_b, m_w_o, m_final_g, v_norm_g, v_w_in, v_conv_w, v_w_out_a, v_a_re, v_a_im, v_log_dt, v_b_re, v_b_im, v_c_re, v_c_im, v_d_skip, v_w_glu, v_b_glu, v_w_out_b, v_w_o, v_final_g):
    depth = norm_g.shape[0]
    l, d = x.shape[1], x.shape[2]
    ws = w_glu.shape[2]
    n_groups, n_state = a_re.shape[1], a_re.shape[2]
    nb = ws // LANES
    assert S5_GROUP == b_re.shape[3] and n_state * S5_GB == 4 * LANES
    u_col, zb_col = 4 * d // ws, 4 * d // ws + 1
    ga_col, gb_col = (4 * d + 2 * ws) // d, (4 * d + 2 * ws) // d + 1
    me = 4 * lax.axis_index("x") + 2 * lax.axis_index("y") + lax.axis_index("c")

    xs = [x[0]]
    tgt = loss_target[0]

    big_names = ("w_in", "w_out_a", "w_glu", "w_out_b", "w_o")
    big = dict(w_in=(w_in, m_w_in, v_w_in), w_out_a=(w_out_a, m_w_out_a, v_w_out_a),
               w_glu=(w_glu, m_w_glu, v_w_glu), w_out_b=(w_out_b, m_w_out_b, v_w_out_b),
               w_o=(w_o, m_w_o, v_w_o))

    all_shards = [[big[k][0][i].astype(BF16) for k in big_names] for i in range(depth)]

    def shards_bf16(i):
        return all_shards[i]

    main_names = ("a_re", "a_im", "log_dt", "b_re", "b_im", "c_re", "c_im", "d_skip", "b_glu", "conv_w")
    small_w = dict(a_re=(a_re, m_a_re, v_a_re), a_im=(a_im, m_a_im, v_a_im),
                   log_dt=(log_dt, m_log_dt, v_log_dt), b_re=(b_re, m_b_re, v_b_re), b_im=(b_im, m_b_im, v_b_im),
                   c_re=(c_re, m_c_re, v_c_re), c_im=(c_im, m_c_im, v_c_im), d_skip=(d_skip, m_d_skip, v_d_skip),
                   b_glu=(b_glu, m_b_glu, v_b_glu))
    main_shapes = [small_w[k][0].shape for k in main_names[:-1]] + [(depth, 3, d)]
    zeros_conv = jnp.zeros((depth, 3, d), F32)
    main_wmv = [_pack([small_w[k][q] for k in main_names[:-1]] + [zeros_conv])[None] for q in range(3)]
    zero1 = jnp.zeros((1,), F32)
    gains_wmv = [_pack([g_, f_, zero1])[None]
                 for g_, f_ in ((norm_g, final_g), (m_norm_g, m_final_g), (v_norm_g, v_final_g))]
    dc = d // N_DEV
    pad8 = lambda a: jnp.pad(a.reshape(depth * 3, dc), ((0, SUBLANES - depth * 3), (0, 0)))[None]
    conv_wmv = [pad8(conv_w), pad8(m_conv_w), pad8(v_conv_w)]

    def gather_start(shards, name, deps=()):
        sems, srcs, lands, token = _exchange_start(
            _plan_gather_chips, shards, [_landing(s_, N_DEV, me) for s_ in shards], f"{name}_start", deps)
        return (name, sems, srcs, lands), token

    def gather_forward(state, after, deps=()):
        name, sems, srcs, lands = state
        _, lands = _exchange_wait(_plan_gather_chips, sems, srcs, lands, after, f"{name}_wait")
        sems, _, lands, token = _exchange_start(_plan_gather_forward, None, lands, f"{name}_forward_start", deps)
        return (name, sems, lands), token

    def gather_finish(state, after):
        name, sems, lands = state
        return _exchange_wait(_plan_gather_forward, sems, None, lands, after, f"{name}_forward_wait")[1]

    conv_shard = jnp.pad(conv_w.reshape(depth * 3, -1), ((0, SUBLANES - depth * 3), (0, 0)))
    w_in_state, token = gather_start([shards_bf16(0)[0], conv_shard], "ag_w_in_0")
    s5 = []
    shape3 = (n_groups, n_state, S5_GROUP)
    for i in range(depth):
        dense_in = (_dense(a_re[i][:, :, None], shape3), _dense(a_im[i][:, :, None], shape3),
                    _dense(log_dt[i][:, None, None], shape3), b_re[i].reshape(-1, LANES), b_im[i].reshape(-1, LANES))
        lbr, lbi, bbr, bbi = _s5_params(*dense_in, f"s5_params_{i}", deps=(token,))
        bbr3, bbi3 = bbr.reshape(shape3), bbi.reshape(shape3)
        diag = lambda m: _block_diag(m, nb).astype(BF16)
        s5.append(dict(
            dense_in=dense_in,
            lam_re=lbr.reshape(shape3)[:, :, 0].reshape(1, -1), lam_im=lbi.reshape(shape3)[:, :, 0].reshape(1, -1),
            up=(diag(bbr3.transpose(0, 2, 1)), diag(bbi3.transpose(0, 2, 1))),
            down=(diag(c_re[i].transpose(0, 2, 1)), diag(c_im[i].transpose(0, 2, 1)))))
    prologue = [m for p in s5 for k in ("up", "down") for m in p[k]]
    prologue += [p[k] for p in s5 for k in ("lam_re", "lam_im")]
    prologue += main_wmv + gains_wmv + conv_wmv + all_shards[0][1:] + [s_ for sh in all_shards[1:] for s_ in sh]
    w_in_state, token = gather_forward(w_in_state, prologue)
    rest_state, token = gather_start(shards_bf16(0)[1:], "ag_rest_0", deps=(token,))
    next_state = None
    if depth > 1:
        next_state, token = gather_start([shards_bf16(1)[0]], "ag_w_in_1", deps=(token,))

    saved = []
    wg = [None] * depth
    conv_full = None
    for i in range(depth):
        xi = xs[-1]
        h = _rmsnorm_fwd(xi, norm_g[i], f"rmsnorm_fwd_{i}", deps=(token,))
        arrived = gather_finish(w_in_state, [h])
        if i == 0:
            conv_full = arrived[1].transpose(1, 0, 2).reshape(SUBLANES, d)[:depth * 3].reshape(depth, 3, d)
        conv8 = jnp.pad(conv_full[i], ((0, SUBLANES - 3), (0, 0)))
        proj = _mm_win_fwd(h, arrived[0], f"mm_proj_{i}")
        u_seg = _to_segments(proj[:, 4 * d:4 * d + ws])
        h_re, h_im, y_seg = _s5_forward(u_seg, *s5[i]["up"], *s5[i]["down"], s5[i]["lam_re"], s5[i]["lam_im"],
                                        d_skip[i].reshape(1, ws), f"s5_forward_{i}")
        rest_state, token = gather_forward(rest_state, [y_seg])
        pa = _branch_a_fwd(proj, conv8, d, f"branch_a_fwd_{i}", deps=(token,))
        rest = gather_finish(rest_state, [pa])
        wg[i] = g = dict(w_in=arrived[0], w_a=rest[0].reshape(d, d), w_glu=rest[1].reshape(ws, ws),
                         w_b=rest[2], w_o=rest[3].reshape(d, d))
        ya = _mm(pa, g["w_a"], name=f"mm_ya_{i}", out_dtype=BF16)
        y = _from_segments(y_seg)
        yg = _gelu_cast(y, f"gelu_{i}")
        gl = _mm(yg, g["w_glu"], name=f"mm_glu_{i}", out_dtype=BF16)
        pb = _glu_post(y, gl, proj, b_glu[i], zb_col, f"glu_post_{i}")
        w_b2d = g["w_b"].transpose(1, 0, 2).reshape(ws, d)
        yb = _mm(pb, w_b2d, name=f"mm_yb_{i}", out_dtype=BF16)
        mrg = _merge_fwd(proj, ya, yb, d, ga_col, gb_col, f"merge_fwd_{i}")
        deps = ()
        if i + 1 < depth:
            w_in_state, token = gather_forward(next_state, [mrg])
            rest_state, token = gather_start(shards_bf16(i + 1)[1:], f"ag_rest_{i + 1}", deps=(token,))
            if i + 2 < depth:
                next_state, token = gather_start([shards_bf16(i + 2)[0]], f"ag_w_in_{i + 2}", deps=(token,))
            deps = (token,)
        xs.append(_mm(mrg, g["w_o"], name=f"mm_out_{i}", add=xi, deps=deps))
        saved.append(dict(h=h, proj=proj, pa=pa, ya=ya, yb=yb, y=y, yg=yg, gl=gl, pb=pb, mrg=mrg,
                          u_seg=u_seg, h_re=h_re, h_im=h_im, conv8=conv8, w_b2d=w_b2d))

    dx, g_final, loss_part = _final_loss(xs[-1], final_g, tgt, "final_loss")

    rs_pending = []
    small = {k: [None] * depth for k in ("norm_g", "a_re", "a_im", "log_dt", "b_re", "b_im", "c_re", "c_im",
                                         "d_skip", "b_glu", "conv_w")}

    my_chip = 2 * lax.axis_index("x") + lax.axis_index("y")

    def reduce_on_chip(pieces, tag):
        lands = [lax.empty((4,) + p.shape[1:], p.dtype) for p in pieces]
        sems, srcs, lands, token = _exchange_start(_plan_reduce_sibling, pieces, lands, f"rs_sibling_start_{tag}")
        return (sems, srcs, lands), token

    def reduce_across_chips(names_, state, layer, tag, after):
        sems, srcs, lands = state
        srcs, lands = _exchange_wait(_plan_reduce_sibling, sems, srcs, lands, after, f"rs_sibling_wait_{tag}")
        sums = [_chip_sums(p, l_, f"chip_sum_{k}_{layer}") for k, p, l_ in zip(names_, srcs, lands)]
        lands = [_landing(lax.dynamic_index_in_dim(s_, my_chip, 0, keepdims=False), 4, my_chip) for s_ in sums]
        sems, srcs, lands, token = _exchange_start(_plan_reduce_chips, sums, lands, f"rs_chips_start_{tag}")
        rs_pending.append((names_, layer, sems, srcs, lands, f"rs_chips_wait_{tag}"))
        return token

    for i in reversed(range(depth)):
        s, g = saved[i], wg[i]
        proj = s["proj"]
        dxo_b = dx.astype(BF16)
        dm = _mm(dxo_b, g["w_o"], name=f"mm_dm_{i}", nt=True, out_dtype=BF16)
        gw_o = _mm(s["mrg"], dxo_b, ta=True, name=f"mm_gw_o_{i}", out_dtype=BF16)
        dy2, dproj = _merge_bwd(proj, s["ya"], s["yb"], dm, d, ga_col, f"merge_bwd_{i}")
        dya, dyb = dy2[0], dy2[1]
        dpa = _mm(dya, g["w_a"], name=f"mm_dpa_{i}", nt=True, out_dtype=BF16)
        gw_a = _mm(s["pa"], dya, ta=True, name=f"mm_gw_a_{i}", out_dtype=BF16)
        dpb = _mm(dyb, s["w_b2d"], name=f"mm_dpb_{i}", nt=True, out_dtype=BF16)
        gw_b = _mm(s["pb"], dyb, ta=True, name=f"mm_gw_b_{i}", split_n=N_DEV, out_dtype=BF16)
        dproj, dw0, dw1, dw2 = _branch_a_bwd(proj, dpa, s["conv8"], dproj, d, f"branch_a_bwd_{i}")
        small["conv_w"][i] = jnp.concatenate([dw0, dw1, dw2], axis=0)
        dproj, dgl, t1, db_glu = _glu_bwd1(s["y"], s["gl"], proj, b_glu[i], dpb, dproj, zb_col, f"glu_bwd1_{i}")
        small["b_glu"][i] = db_glu.reshape(ws)
        dyg2 = _mm(dgl, g["w_glu"], name=f"mm_dyg_{i}", nt=True, out_dtype=BF16)
        gw_glu = _mm(s["yg"], dgl, ta=True, name=f"mm_gw_glu_{i}", out_dtype=BF16)
        small_names_ = ("w_out_a", "w_glu", "w_out_b", "w_o")
        state, token = reduce_on_chip(
            [gw_a.reshape(N_DEV, d // N_DEV, d), gw_glu.reshape(N_DEV, ws // N_DEV, ws), gw_b,
             gw_o.reshape(N_DEV, d // N_DEV, d)], f"small_{i}")
        dy = _glu_bwd2(s["y"], t1, dyg2, f"glu_bwd2_{i}", deps=(token,))
        dy_seg = _to_segments(dy)
        u_seg = s["u_seg"]
        du_seg, gc_re, gc_im, gbb_re, gbb_im, glam_re, glam_im, dskip = _s5_backward(
            dy_seg, u_seg, s["h_re"], s["h_im"], *s5[i]["up"], *s5[i]["down"],
            s5[i]["lam_re"], -s5[i]["lam_im"], d_skip[i].reshape(1, ws), f"s5_backward_{i}")
        token = reduce_across_chips(small_names_, state, i, f"small_{i}", [du_seg])
        dproj = _write_cols(dproj, _from_segments(du_seg), u_col, f"write_du_{i}")
        gw_in = _mm(s["h"], dproj, ta=True, name=f"mm_gw_in_{i}", split_n=N_DEV, tm=1024,
                    out_dtype=BF16, deps=(token,))
        state, token = reduce_on_chip([gw_in], f"w_in_{i}")
        small["d_skip"][i] = dskip.reshape(n_groups, S5_GROUP)
        small["c_re"][i] = _block_diag_extract(gc_re, n_groups, S5_GROUP, n_state)
        small["c_im"][i] = -_block_diag_extract(gc_im, n_groups, S5_GROUP, n_state)
        gbb_re = _block_diag_extract(gbb_re, n_groups, S5_GROUP, n_state).transpose(0, 2, 1)
        gbb_im = _block_diag_extract(gbb_im, n_groups, S5_GROUP, n_state).transpose(0, 2, 1)
        gar, gai, gdt, gbr, gbi = _s5_params_bwd(
            *s5[i]["dense_in"], _dense(glam_re.reshape(n_groups, n_state, 1), shape3),
            _dense(glam_im.reshape(n_groups, n_state, 1), shape3),
            gbb_re.reshape(-1, LANES), gbb_im.reshape(-1, LANES), n_groups, f"s5_params_bwd_{i}", deps=(token,))
        small["a_re"][i] = gar.reshape(shape3)[:, :, 0]
        small["a_im"][i] = gai.reshape(shape3)[:, :, 0]
        small["log_dt"][i] = gdt[:, 0]
        small["b_re"][i] = gbr.reshape(shape3)
        small["b_im"][i] = gbi.reshape(shape3)
        if i == 0:
            part = {k: jnp.stack(small[k]) for k in main_names}
            main_state, token = gather_start([_pack([part[k] for k in main_names]).astype(BF16)], "ag_small")
            token = reduce_across_chips(("w_in",), state, i, f"w_in_{i}", [token])
            main_state, token = gather_forward(main_state, [token])
            dh = _mm_win_bwd(dproj, g["w_in"], f"mm_dh_{i}", deps=(token,))
            main_slots = gather_finish(main_state, [dh])[0]
            deps = ()
        else:
            dh = _mm_win_bwd(dproj, g["w_in"], f"mm_dh_{i}", deps=(gar,))
            deps = (reduce_across_chips(("w_in",), state, i, f"w_in_{i}", [dh]),)
        dx, dng = _rmsnorm_bwd(xs[i], norm_g[i], dh, dx, f"rmsnorm_bwd_{i}", deps=deps)
        small["norm_g"][i] = dng.reshape(d)

    results = {}

    gain_grads = jnp.concatenate([jnp.stack(small["norm_g"]).reshape(-1), g_final.reshape(d)])
    gain_shapes = [(depth, d), (d,), (1,)]
    gains_state, token = gather_start([_pack([gain_grads, loss_part[0, :1]])], "ag_gains")

    sres = [_unpack(p[0], main_shapes) for p in _adamw(*main_wmv, main_slots, "adamw_small", deps=(token,))]
    for j, k in enumerate(main_names[:-1]):
        results[k] = [sres[q][j] for q in range(4)]
    gconv = lax.dynamic_slice_in_dim(sres[0][-1], me * dc, dc, axis=2)
    cres = _adamw(*conv_wmv, pad8(gconv), "adamw_conv_w")
    results["conv_w"] = [r_[0, :depth * 3].reshape(depth, 3, dc) for r_ in cres]

    after = [cres[0]]
    for names_, layer, sems, srcs, lands, wait_name in rs_pending:
        _, slots = _exchange_wait(_plan_reduce_chips, sems, srcs, lands, after, wait_name)
        for k, land in zip(names_, slots):
            w_, m_, v_ = big[k]
            results[k] = _adamw(w_, m_, v_, land, f"adamw_{k}_{layer}", layer=layer, prev=results.get(k))
            after = [results[k][0]]

    gains_state, token = gather_forward(gains_state, after)
    gpack = gather_finish(gains_state, [token])[0]
    gres = [_unpack(p[0], gain_shapes) for p in _adamw(*gains_wmv, gpack, "adamw_gains")]
    results["norm_g"] = [gres[q][0] for q in range(4)]
    results["final_g"] = [gres[q][1] for q in range(4)]
    loss = gres[0][2][0]

    names = ("norm_g", "w_in", "conv_w", "w_out_a", "a_re", "a_im", "log_dt", "b_re", "b_im", "c_re", "c_im",
             "d_skip", "w_glu", "b_glu", "w_out_b", "w_o", "final_g")
    outs = [loss, dx[None]]
    for q in range(4):
        outs += [results[k][q] for k in names]
    return tuple(outs)
```

```python
import functools
import math

import jax
import jax.numpy as jnp
from jax import lax
from jax.experimental import pallas as pl
from jax.experimental.pallas import tpu as pltpu

F32 = jnp.float32
BF16 = jnp.bfloat16
HIGHEST = lax.Precision.HIGHEST

N_DEV = 8
LANES = 128
SUBLANES = 8
VMEM_LIMIT_BYTES = 56 * 1024 * 1024

RMS_EPS = 1e-6
ADAM_LR = 0.001
ADAM_B1 = 0.9
ADAM_B2 = 0.999
ADAM_EPS = 1e-08
ADAM_WD = 0.01
ADAM_STEP = 10
GELU_C0 = math.sqrt(2.0 / math.pi)
GELU_C1 = 0.044715

ADAMW_BLOCK_ELEMS = 1 << 17
PACK_ROWS = 512

S5_GROUP = 16
S5_GB = LANES // S5_GROUP


def _params(*semantics):
    return pltpu.CompilerParams(dimension_semantics=semantics, vmem_limit_bytes=VMEM_LIMIT_BYTES)


ANY_SPEC = pl.BlockSpec(memory_space=pl.ANY)


def _pallas(body, args, deps=(), *, in_specs, **kwargs):
    deps = tuple(deps)
    if not deps:
        return pl.pallas_call(body, in_specs=in_specs, **kwargs)(*args)

    def body_after(*refs):
        body(*refs[len(deps):])

    return pl.pallas_call(body_after, in_specs=[ANY_SPEC] * len(deps) + list(in_specs), **kwargs)(*deps, *args)


def _tile(n, pref):
    t = min(n, pref)
    while n % t:
        assert t % 2 == 0, (n, pref)
        t //= 2
    return t


def _sigmoid(z):
    return 1.0 / (1.0 + jnp.exp(-z))


def _gelu(y):
    return 0.5 * y * (1.0 + jnp.tanh(GELU_C0 * (y + GELU_C1 * y * y * y)))


def _gelu_grad(y):
    t = jnp.tanh(GELU_C0 * (y + GELU_C1 * y * y * y))
    return 0.5 * (1.0 + t) + 0.5 * y * (1.0 - t * t) * GELU_C0 * (1.0 + 3.0 * GELU_C1 * y * y)


HBM_SPEC = pl.BlockSpec(memory_space=pltpu.HBM)
SEM_SPEC = pl.BlockSpec(memory_space=pltpu.SEMAPHORE)
DATAFLOW_EFFECT = pltpu.SideEffectType.DATAFLOW_SIDE_EFFECTING
OTHER_CHIPS = (2, 4, 6)


def _flip(pos, mask):
    x, y, c = pos
    return x ^ ((mask >> 2) & 1), y ^ ((mask >> 1) & 1), c ^ (mask & 1)


def _dev(pos):
    return 4 * pos[0] + 2 * pos[1] + pos[2]


def _chip(pos):
    return 2 * pos[0] + pos[1]


def _plan_gather_chips(me):
    return [(_flip(me, k), None, _dev(me), _dev(_flip(me, k))) for k in (1,) + OTHER_CHIPS]


def _plan_gather_forward(me):
    sib = _flip(me, 1)
    return [(sib, _dev(_flip(me, k)), _dev(_flip(me, k)), _dev(_flip(sib, k))) for k in OTHER_CHIPS]


def _plan_reduce_sibling(me):
    sib = _flip(me, 1)
    return [(sib, 2 * q + sib[2], q, q) for q in range(4)]


def _plan_reduce_chips(me):
    return [(_flip(me, k), _chip(_flip(me, k)), _chip(me), _chip(_flip(me, k))) for k in OTHER_CHIPS]


X_CHIP, Y_CHIP, XY_CHIP = 4, 2, 6


def _plan_halves_first(me):
    sib, xn, yn = _flip(me, 1), _flip(me, X_CHIP), _flip(me, Y_CHIP)
    return [(sib, 0, 2 * _dev(me), 2 * _dev(sib)), (sib, 1, 2 * _dev(me) + 1, 2 * _dev(sib) + 1),
            (xn, 0, 2 * _dev(me), 2 * _dev(xn)), (yn, 1, 2 * _dev(me) + 1, 2 * _dev(yn) + 1)]


def _plan_halves_second(me):
    xn, yn, dg = _flip(me, X_CHIP), _flip(me, Y_CHIP), _flip(me, XY_CHIP)
    return [(yn, 2 * _dev(me), 2 * _dev(me), 2 * _dev(yn)), (yn, 2 * _dev(xn), 2 * _dev(xn), 2 * _dev(dg)),
            (xn, 2 * _dev(me) + 1, 2 * _dev(me) + 1, 2 * _dev(xn) + 1),
            (xn, 2 * _dev(yn) + 1, 2 * _dev(yn) + 1, 2 * _dev(dg) + 1)]


def _plan_halves_forward(me):
    sib = _flip(me, 1)
    return [(sib, 2 * _dev(_flip(me, k)) + h, 2 * _dev(_flip(me, k)) + h, 2 * _dev(_flip(sib, k)) + h)
            for k in OTHER_CHIPS for h in (0, 1)]


PLAN_COPIES = {_plan_gather_chips: 4, _plan_gather_forward: 3, _plan_reduce_sibling: 4, _plan_reduce_chips: 3,
               _plan_halves_first: 4, _plan_halves_second: 4, _plan_halves_forward: 6}


def _exchange_copies(plan, src_refs, land_refs, send_sems, recv_sems, incoming=True):
    me = (lax.axis_index("x"), lax.axis_index("y"), lax.axis_index("c"))
    pairs = []
    for b, (src_ref, land_ref) in enumerate(zip(src_refs, land_refs)):
        for j, (peer, src_slot, there, here) in enumerate(plan(me)):
            sem = b * PLAN_COPIES[plan] + j
            src = src_ref if src_slot is None else src_ref.at[src_slot]
            out = pltpu.make_async_remote_copy(
                src_ref=src, dst_ref=land_ref.at[there], send_sem=send_sems.at[sem], recv_sem=recv_sems.at[sem],
                device_id=peer, device_id_type=pl.DeviceIdType.MESH)
            inc = pltpu.make_async_remote_copy(
                src_ref=src, dst_ref=land_ref.at[here], send_sem=send_sems.at[sem], recv_sem=recv_sems.at[sem],
                device_id=peer, device_id_type=pl.DeviceIdType.MESH) if incoming else None
            pairs.append((out, inc))
    return pairs


def _exchange_start(plan, srcs, lands, name, deps=()):
    srcs = [] if srcs is None else list(srcs)
    ns, n, nd = len(srcs), len(lands), len(deps)

    def body(*refs):
        land_refs = refs[ns:ns + n]
        sems_at = ns + n + nd
        pairs = _exchange_copies(plan, refs[:ns] if ns else land_refs, land_refs, refs[sems_at], refs[sems_at + 1],
                                 incoming=False)
        for out, _ in pairs:
            out.start()
        token = refs[-1]
        token[...] = jnp.zeros_like(token)

    sems = pltpu.SemaphoreType.DMA((PLAN_COPIES[plan] * n,))
    bufs = srcs + list(lands)
    outs = pl.pallas_call(
        body, name=name,
        out_shape=(sems, sems, *[pltpu.HBM(a.shape, a.dtype) for a in bufs],
                   jax.ShapeDtypeStruct((SUBLANES, LANES), F32)),
        in_specs=[HBM_SPEC] * (ns + n) + [ANY_SPEC] * nd,
        out_specs=(SEM_SPEC, SEM_SPEC, *[HBM_SPEC] * (ns + n), pl.BlockSpec(memory_space=pltpu.VMEM)),
        input_output_aliases={i: 2 + i for i in range(ns + n)},
        compiler_params=pltpu.CompilerParams(has_side_effects=DATAFLOW_EFFECT),
    )(*[pltpu.with_memory_space_constraint(a, pltpu.HBM) for a in bufs], *deps)
    return (outs[0], outs[1]), (outs[2:2 + ns] if ns else None), outs[2 + ns:2 + ns + n], outs[-1]


def _exchange_wait(plan, sems, srcs, lands, after, name):
    srcs = [] if srcs is None else list(srcs)
    ns, n = len(srcs), len(lands)

    def body(*refs):
        land_refs = refs[ns:ns + n]
        pairs = _exchange_copies(plan, refs[:ns] if ns else land_refs, land_refs, refs[ns + n], refs[ns + n + 1])
        for out, inc in pairs:
            out.wait_send()
            inc.wait_recv()

    bufs = srcs + list(lands)
    outs = pl.pallas_call(
        body, name=name,
        out_shape=[pltpu.HBM(a.shape, a.dtype) for a in bufs],
        in_specs=[HBM_SPEC] * (ns + n) + [SEM_SPEC, SEM_SPEC] + [ANY_SPEC] * len(after),
        out_specs=[HBM_SPEC] * (ns + n),
        input_output_aliases={i: i for i in range(ns + n)},
        compiler_params=pltpu.CompilerParams(has_side_effects=DATAFLOW_EFFECT),
    )(*bufs, sems[0], sems[1], *after)
    return outs[:ns], outs[ns:]


def _landing(own, slots, slot):
    land = lax.empty((slots,) + own.shape, own.dtype)
    return lax.dynamic_update_slice(land, own[None], (slot,) + (0,) * own.ndim)


def _chip_sums(pieces, land, name):
    _, r, c_ = land.shape
    tr = _tile(r, max(2 * SUBLANES, 1 << int(math.log2(4 * ADAMW_BLOCK_ELEMS // c_))))

    def body(core_ref, p_ref, l_ref, o_ref):
        o_ref[...] = (p_ref[...].astype(F32) + l_ref[...].astype(F32)).astype(o_ref.dtype)

    spec = pl.BlockSpec((None, tr, c_), lambda q, i, core: (q, i, 0))
    return pl.pallas_call(
        body, name=name,
        grid_spec=pltpu.PrefetchScalarGridSpec(
            num_scalar_prefetch=1, grid=(4, r // tr),
            in_specs=[pl.BlockSpec((None, tr, c_), lambda q, i, core: (2 * q + core[0], i, 0)), spec],
            out_specs=spec),
        out_shape=jax.ShapeDtypeStruct(land.shape, land.dtype),
        compiler_params=_params("parallel", "parallel"),
    )(lax.axis_index("c").reshape(1), pieces, land)


def _mm(a, b, *, name, nt=False, ta=False, out_dtype=F32, add=None, split_n=None, tm=512, tn=1024, deps=()):
    k, m = a.shape if ta else a.shape[::-1]
    n = b.shape[0] if nt else b.shape[1]
    tm = _tile(m, tm)
    tn = n // split_n if split_n else _tile(n, tn)
    dims = (((0 if ta else 1,), (1 if nt else 0,)), ((), ()))

    def body(*refs):
        a_ref, b_ref = refs[0], refs[1]
        o_ref = refs[-1]
        acc = lax.dot_general(a_ref[...], b_ref[...], dims, preferred_element_type=F32)
        if add is not None:
            acc = acc + refs[2][...]
        o_ref[...] = acc.astype(o_ref.dtype)

    in_specs = [pl.BlockSpec((k, tm), lambda i, j: (0, i)) if ta else pl.BlockSpec((tm, k), lambda i, j: (i, 0)),
                pl.BlockSpec((tn, k), lambda i, j: (j, 0)) if nt
                else pl.BlockSpec((k, tn), lambda i, j: (0, j))]
    args = [a, b]
    if add is not None:
        in_specs.append(pl.BlockSpec((tm, tn), lambda i, j: (i, j)))
        args.append(add)
    if split_n:
        out_shape = jax.ShapeDtypeStruct((split_n, m, tn), out_dtype)
        out_spec = pl.BlockSpec((None, tm, tn), lambda i, j: (j, i, 0))
    else:
        out_shape = jax.ShapeDtypeStruct((m, n), out_dtype)
        out_spec = pl.BlockSpec((tm, tn), lambda i, j: (i, j))
    return _pallas(
        body, args, deps, name=name, grid=(m // tm, n // tn), in_specs=in_specs, out_specs=out_spec,
        out_shape=out_shape, compiler_params=_params("parallel", "parallel"))


def _mm_win_fwd(h, w_g, name, deps=()):
    m, k = h.shape
    nj = w_g.shape[2]
    tm = _tile(m, 512)

    def body(a_ref, b_ref, o_ref):
        o_ref[...] = jnp.dot(a_ref[...], b_ref[...], preferred_element_type=F32).astype(o_ref.dtype)

    return _pallas(
        body, [h, w_g], deps, name=name, grid=(N_DEV, m // tm),
        in_specs=[pl.BlockSpec((tm, k), lambda j, i: (i, 0)),
                  pl.BlockSpec((None, k, nj), lambda j, i: (j, 0, 0))],
        out_specs=pl.BlockSpec((tm, nj), lambda j, i: (i, j)),
        out_shape=jax.ShapeDtypeStruct((m, N_DEV * nj), BF16),
        compiler_params=_params("parallel", "parallel"))


def _mm_win_bwd(dproj, w_g, name, deps=()):
    m = dproj.shape[0]
    d, nj = w_g.shape[1], w_g.shape[2]
    tm = _tile(m, 512)
    tn = _tile(d, 1024)

    per_step = 2

    def body(a_ref, b_ref, o_ref, acc_ref):
        j = pl.program_id(2)

        @pl.when(j == 0)
        def _():
            acc_ref[...] = jnp.zeros_like(acc_ref)

        part = None
        for k in range(per_step):
            term = lax.dot_general(a_ref[:, k * nj:(k + 1) * nj], b_ref[k], (((1,), (1,)), ((), ())),
                                   preferred_element_type=F32)
            part = term if part is None else part + term
        acc_ref[...] += part

        @pl.when(j == N_DEV // per_step - 1)
        def _():
            o_ref[...] = acc_ref[...]

    return _pallas(
        body, [dproj, w_g], deps, name=name, grid=(m // tm, d // tn, N_DEV // per_step),
        in_specs=[pl.BlockSpec((tm, per_step * nj), lambda i, n, j: (i, j)),
                  pl.BlockSpec((per_step, tn, nj), lambda i, n, j: (j, n, 0))],
        out_specs=pl.BlockSpec((tm, tn), lambda i, n, j: (i, n)),
        out_shape=jax.ShapeDtypeStruct((m, d), F32),
        scratch_shapes=[pltpu.VMEM((tm, tn), F32)],
        compiler_params=_params("parallel", "parallel", "arbitrary"))


def _row_spec(tr, w, col):
    return pl.BlockSpec((tr, w), lambda i: (i, col))


def _full_spec(shape):
    return pl.BlockSpec(shape, lambda i: (0,) * len(shape))


def _rmsnorm_fwd(x, g, name, deps=()):
    l, d = x.shape
    tr = _tile(l, 256)

    def body(x_ref, g_ref, o_ref):
        xv = x_ref[...]
        rstd = lax.rsqrt(jnp.mean(xv * xv, axis=-1, keepdims=True) + RMS_EPS)
        o_ref[...] = (xv * rstd * g_ref[...]).astype(o_ref.dtype)

    return _pallas(
        body, [x, g.reshape(1, d)], deps, name=name, grid=(l // tr,),
        in_specs=[_row_spec(tr, d, 0), _full_spec((1, d))],
        out_specs=_row_spec(tr, d, 0),
        out_shape=jax.ShapeDtypeStruct((l, d), BF16),
        compiler_params=_params("parallel"))


def _rmsnorm_bwd(x, g, dh, dxo, name, deps=()):
    l, d = x.shape
    tr = _tile(l, 256)

    def body(x_ref, g_ref, dh_ref, dxo_ref, dx_ref, dg_ref):
        xv = x_ref[...]
        rstd = lax.rsqrt(jnp.mean(xv * xv, axis=-1, keepdims=True) + RMS_EPS)
        dhv = dh_ref[...]
        gdy = dhv * g_ref[...]
        dot = jnp.mean(gdy * xv, axis=-1, keepdims=True)
        dx_ref[...] = dxo_ref[...] + rstd * gdy - xv * (rstd * rstd * rstd * dot)

        @pl.when(pl.program_id(0) == 0)
        def _():
            dg_ref[...] = jnp.zeros_like(dg_ref)

        dg_ref[...] += jnp.sum(dhv * xv * rstd, axis=0, keepdims=True)

    return _pallas(
        body, [x, g.reshape(1, d), dh, dxo], deps, name=name, grid=(l // tr,),
        in_specs=[_row_spec(tr, d, 0), _full_spec((1, d)), _row_spec(tr, d, 0), _row_spec(tr, d, 0)],
        out_specs=[_row_spec(tr, d, 0), _full_spec((1, d))],
        out_shape=[jax.ShapeDtypeStruct((l, d), F32), jax.ShapeDtypeStruct((1, d), F32)],
        compiler_params=_params("arbitrary"))


def _final_loss(x, g, tgt, name):
    l, d = x.shape
    tr = _tile(l, 256)

    def body(x_ref, g_ref, t_ref, dx_ref, dg_ref, loss_ref):
        xv = x_ref[...]
        gv = g_ref[...]
        rstd = lax.rsqrt(jnp.mean(xv * xv, axis=-1, keepdims=True) + RMS_EPS)
        xn = xv * rstd
        err = xn * gv - t_ref[...]
        dy = err * (1.0 / d)
        gdy = dy * gv
        dot = jnp.mean(gdy * xv, axis=-1, keepdims=True)
        dx_ref[...] = rstd * gdy - xv * (rstd * rstd * rstd * dot)

        @pl.when(pl.program_id(0) == 0)
        def _():
            dg_ref[...] = jnp.zeros_like(dg_ref)
            loss_ref[...] = jnp.zeros_like(loss_ref)

        dg_ref[...] += jnp.sum(dy * xn, axis=0, keepdims=True)
        loss_ref[...] += (0.5 / d) * jnp.sum(err * err)

    return pl.pallas_call(
        body, name=name, grid=(l // tr,),
        in_specs=[_row_spec(tr, d, 0), _full_spec((1, d)), _row_spec(tr, d, 0)],
        out_specs=[_row_spec(tr, d, 0), _full_spec((1, d)), _full_spec((SUBLANES, LANES))],
        out_shape=[jax.ShapeDtypeStruct((l, d), F32), jax.ShapeDtypeStruct((1, d), F32),
                   jax.ShapeDtypeStruct((SUBLANES, LANES), F32)],
        compiler_params=_params("arbitrary"),
    )(x, g.reshape(1, d), tgt)


HALO = 2 * SUBLANES


def _halo_spec(tr, w, col, nblk, before):
    step = tr // HALO
    if before:
        return pl.BlockSpec((HALO, w), lambda i: (jnp.maximum(i * step - 1, 0), col))
    return pl.BlockSpec((HALO, w), lambda i: (jnp.minimum((i + 1) * step, nblk - 1), col))


def _shift_down(cur, before, k):
    ext = jnp.concatenate([before, cur], axis=0)
    return pltpu.roll(ext, k, axis=0)[HALO:, :]


def _shift_up(cur, after, k):
    tr = cur.shape[0]
    ext = jnp.concatenate([cur, after], axis=0)
    return pltpu.roll(ext, tr + HALO - k, axis=0)[:tr, :]


def _f32(ref):
    return ref[...].astype(F32)


def _branch_a_fwd(proj, conv_w, d, name, deps=()):
    l = proj.shape[0]
    tr = _tile(l, 256)
    nblk8 = l // HALO

    def body(v_ref, bg_ref, cg_ref, za_ref, vh_ref, cgh_ref, w_ref, o_ref):
        first = pl.program_id(0) == 0
        cv = _f32(cg_ref) * _f32(v_ref)
        cvh = jnp.where(first, 0.0, _f32(cgh_ref) * _f32(vh_ref))
        w0, w1, w2 = w_ref[0:1, :], w_ref[1:2, :], w_ref[2:3, :]
        q = w2 * cv + w1 * _shift_down(cv, cvh, 1) + w0 * _shift_down(cv, cvh, 2)
        za = _f32(za_ref)
        o_ref[...] = (_f32(bg_ref) * q * (za * _sigmoid(za))).astype(o_ref.dtype)

    return _pallas(
        body, [proj, proj, proj, proj, proj, proj, conv_w], deps, name=name, grid=(l // tr,),
        in_specs=[_row_spec(tr, d, 0), _row_spec(tr, d, 1), _row_spec(tr, d, 2), _row_spec(tr, d, 3),
                  _halo_spec(tr, d, 0, nblk8, True), _halo_spec(tr, d, 2, nblk8, True),
                  _full_spec((SUBLANES, d))],
        out_specs=_row_spec(tr, d, 0),
        out_shape=jax.ShapeDtypeStruct((l, d), BF16),
        compiler_params=_params("parallel"))


def _branch_a_bwd(proj, dpa, conv_w, dproj, d, name):
    l = proj.shape[0]
    tr = _tile(l, 128)
    nblk8 = l // HALO
    ntiles = l // tr

    def body(v_ref, bg_ref, cg_ref, za_ref, dpa_ref, vh_ref, cgh_ref, bgn_ref, zan_ref, dpan_ref,
             w_ref, _, o_ref, dw0_ref, dw1_ref, dw2_ref):
        dv_ref, dbg_ref, dcg_ref, dza_ref = [o_ref.at[:, pl.ds(k * d, d)] for k in range(4)]
        i = pl.program_id(0)
        v, bg, cg, za, dpa_v = _f32(v_ref), _f32(bg_ref), _f32(cg_ref), _f32(za_ref), _f32(dpa_ref)
        w0, w1, w2 = w_ref[0:1, :], w_ref[1:2, :], w_ref[2:3, :]
        cv = cg * v
        cvh = jnp.where(i == 0, 0.0, _f32(cgh_ref) * _f32(vh_ref))
        cv1 = _shift_down(cv, cvh, 1)
        cv2 = _shift_down(cv, cvh, 2)
        q = w2 * cv + w1 * cv1 + w0 * cv2
        sg = _sigmoid(za)
        s = za * sg
        dbg_ref[...] = (dpa_v * q * s).astype(dbg_ref.dtype)
        dza_ref[...] = (dpa_v * bg * q * (sg * (1.0 + za * (1.0 - sg)))).astype(dza_ref.dtype)
        dq = dpa_v * bg * s
        zan = _f32(zan_ref)
        dqn = jnp.where(i == ntiles - 1, 0.0, _f32(dpan_ref) * _f32(bgn_ref) * (zan * _sigmoid(zan)))
        dcv = w2 * dq + w1 * _shift_up(dq, dqn, 1) + w0 * _shift_up(dq, dqn, 2)
        dcg_ref[...] = (dcv * v).astype(dcg_ref.dtype)
        dv_ref[...] = (dcv * cg).astype(dv_ref.dtype)

        @pl.when(i == 0)
        def _():
            dw0_ref[...] = jnp.zeros_like(dw0_ref)
            dw1_ref[...] = jnp.zeros_like(dw1_ref)
            dw2_ref[...] = jnp.zeros_like(dw2_ref)

        dw0_ref[...] += jnp.sum(dq * cv2, axis=0, keepdims=True)
        dw1_ref[...] += jnp.sum(dq * cv1, axis=0, keepdims=True)
        dw2_ref[...] += jnp.sum(dq * cv, axis=0, keepdims=True)

    wsum = jax.ShapeDtypeStruct((1, d), F32)
    return pl.pallas_call(
        body, name=name, grid=(ntiles,),
        in_specs=[_row_spec(tr, d, 0), _row_spec(tr, d, 1), _row_spec(tr, d, 2), _row_spec(tr, d, 3),
                  _row_spec(tr, d, 0),
                  _halo_spec(tr, d, 0, nblk8, True), _halo_spec(tr, d, 2, nblk8, True),
                  _halo_spec(tr, d, 1, nblk8, False), _halo_spec(tr, d, 3, nblk8, False),
                  _halo_spec(tr, d, 0, nblk8, False),
                  _full_spec((SUBLANES, d)), ANY_SPEC],
        out_specs=[_row_spec(tr, 4 * d, 0)] + [_full_spec((1, d))] * 3,
        out_shape=[jax.ShapeDtypeStruct(dproj.shape, dproj.dtype)] + [wsum] * 3,
        input_output_aliases={11: 0},
        compiler_params=_params("arbitrary"),
    )(proj, proj, proj, proj, dpa, proj, proj, proj, proj, dpa, conv_w, dproj)


def _gelu_cast(y, name):
    l, w = y.shape
    tr = _tile(l, 512)

    def body(y_ref, o_ref):
        o_ref[...] = _gelu(y_ref[...]).astype(o_ref.dtype)

    return pl.pallas_call(
        body, name=name, grid=(l // tr,), in_specs=[_row_spec(tr, w, 0)],
        out_specs=_row_spec(tr, w, 0), out_shape=jax.ShapeDtypeStruct((l, w), BF16),
        compiler_params=_params("parallel"),
    )(y)


def _glu_post(y, gl, proj, b_glu, zb_col, name):
    l, w = y.shape
    tr = _tile(l, 512)

    def body(y_ref, gl_ref, zb_ref, b_ref, o_ref):
        zb = _f32(zb_ref)
        o_ref[...] = (_gelu(y_ref[...]) * _sigmoid(_f32(gl_ref) + b_ref[...])
                      * (zb * _sigmoid(zb))).astype(o_ref.dtype)

    return pl.pallas_call(
        body, name=name, grid=(l // tr,),
        in_specs=[_row_spec(tr, w, 0), _row_spec(tr, w, 0), _row_spec(tr, w, zb_col), _full_spec((1, w))],
        out_specs=_row_spec(tr, w, 0), out_shape=jax.ShapeDtypeStruct((l, w), BF16),
        compiler_params=_params("parallel"),
    )(y, gl, proj, b_glu.reshape(1, w))


def _glu_bwd1(y, gl, proj, b_glu, dpb, dproj, zb_col, name):
    l, w = y.shape
    tr = _tile(l, 512)

    def body(y_ref, gl_ref, zb_ref, b_ref, dpb_ref, _, dzb_ref, dgl_ref, t_ref, db_ref):
        zb = _f32(zb_ref)
        dpb_v = _f32(dpb_ref)
        yg = _gelu(y_ref[...])
        sgl = _sigmoid(_f32(gl_ref) + b_ref[...])
        szb = _sigmoid(zb)
        dzb_ref[...] = (dpb_v * yg * sgl * (szb * (1.0 + zb * (1.0 - szb)))).astype(dzb_ref.dtype)
        e = dpb_v * (zb * szb)
        dgl = e * yg * sgl * (1.0 - sgl)
        dgl_ref[...] = dgl.astype(dgl_ref.dtype)
        t_ref[...] = e * sgl

        @pl.when(pl.program_id(0) == 0)
        def _():
            db_ref[...] = jnp.zeros_like(db_ref)

        db_ref[...] += jnp.sum(dgl, axis=0, keepdims=True)

    return pl.pallas_call(
        body, name=name, grid=(l // tr,),
        in_specs=[_row_spec(tr, w, 0), _row_spec(tr, w, 0), _row_spec(tr, w, zb_col), _full_spec((1, w)),
                  _row_spec(tr, w, 0), ANY_SPEC],
        out_specs=[_row_spec(tr, w, zb_col)] + [_row_spec(tr, w, 0)] * 2 + [_full_spec((1, w))],
        out_shape=[jax.ShapeDtypeStruct(dproj.shape, dproj.dtype), jax.ShapeDtypeStruct((l, w), BF16),
                   jax.ShapeDtypeStruct((l, w), F32), jax.ShapeDtypeStruct((1, w), F32)],
        input_output_aliases={5: 0},
        compiler_params=_params("arbitrary"),
    )(y, gl, proj, b_glu.reshape(1, w), dpb, dproj)


def _write_cols(dproj, cols, col, name):
    l, w = cols.shape
    tr = _tile(l, 512)

    def body(c_ref, _, o_ref):
        o_ref[...] = c_ref[...].astype(o_ref.dtype)

    return pl.pallas_call(
        body, name=name, grid=(l // tr,), in_specs=[_row_spec(tr, w, 0), ANY_SPEC],
        out_specs=_row_spec(tr, w, col), out_shape=jax.ShapeDtypeStruct(dproj.shape, dproj.dtype),
        input_output_aliases={1: 0}, compiler_params=_params("parallel"),
    )(cols, dproj)


def _glu_bwd2(y, t1, dyg2, name, deps=()):
    l, w = y.shape
    tr = _tile(l, 512)

    def body(y_ref, t_ref, d_ref, o_ref):
        o_ref[...] = (t_ref[...] + _f32(d_ref)) * _gelu_grad(y_ref[...])

    return _pallas(
        body, [y, t1, dyg2], deps, name=name, grid=(l // tr,), in_specs=[_row_spec(tr, w, 0)] * 3,
        out_specs=_row_spec(tr, w, 0), out_shape=jax.ShapeDtypeStruct((l, w), F32),
        compiler_params=_params("parallel"))


def _merge_fwd(proj, ya, yb, d, ga_col, gb_col, name):
    l = proj.shape[0]
    tr = _tile(l, 256)

    def body(ga_ref, gb_ref, ya_ref, yb_ref, o_ref):
        o_ref[...] = (_sigmoid(_f32(ga_ref)) * _f32(ya_ref)
                      + _sigmoid(_f32(gb_ref)) * _f32(yb_ref)).astype(o_ref.dtype)

    return pl.pallas_call(
        body, name=name, grid=(l // tr,),
        in_specs=[_row_spec(tr, d, ga_col), _row_spec(tr, d, gb_col), _row_spec(tr, d, 0), _row_spec(tr, d, 0)],
        out_specs=_row_spec(tr, d, 0), out_shape=jax.ShapeDtypeStruct((l, d), BF16),
        compiler_params=_params("parallel"),
    )(proj, proj, ya, yb)


def _merge_bwd(proj, ya, yb, dm, d, ga_col, name):
    l, n = proj.shape
    tr = _tile(l, 256)

    def body(g_ref, ya_ref, yb_ref, dm_ref, dy_ref, dg_ref):
        dmv = _f32(dm_ref)
        sg = _sigmoid(_f32(g_ref))
        yv = jnp.where(pl.program_id(1) == 0, _f32(ya_ref), _f32(yb_ref))
        dy_ref[...] = (dmv * sg).astype(dy_ref.dtype)
        dg_ref[...] = (dmv * yv * sg * (1.0 - sg)).astype(dg_ref.dtype)

    row = pl.BlockSpec((tr, d), lambda i, j: (i, 0))
    return pl.pallas_call(
        body, name=name, grid=(l // tr, 2),
        in_specs=[pl.BlockSpec((tr, d), lambda i, j: (i, ga_col + j)), row, row, row],
        out_specs=[pl.BlockSpec((None, tr, d), lambda i, j: (j, i, 0)),
                   pl.BlockSpec((tr, d), lambda i, j: (i, ga_col + j))],
        out_shape=[jax.ShapeDtypeStruct((2, l, d), BF16), jax.ShapeDtypeStruct((l, n), BF16)],
        compiler_params=_params("parallel", "arbitrary"),
    )(proj, ya, yb, dm)


def _to_segments(a):
    l, w = a.shape
    return a.reshape(SUBLANES, l // SUBLANES, w).transpose(1, 0, 2).reshape(l, w)


def _from_segments(a):
    l, w = a.shape
    return a.reshape(l // SUBLANES, SUBLANES, w).transpose(1, 0, 2).reshape(l, w)


def _dense(z, shape):
    return jnp.broadcast_to(z, shape).reshape(-1, LANES)


def _s5_disc(are, aim, ldt):
    dt = jnp.exp(ldt)
    er = jnp.exp(are * dt)
    lbr = er * jnp.cos(aim * dt)
    lbi = er * jnp.sin(aim * dt)
    inv = 1.0 / (are * are + aim * aim)
    fr = ((lbr - 1.0) * are + lbi * aim) * inv
    fi = (lbi * are - (lbr - 1.0) * aim) * inv
    return dt, lbr, lbi, inv, fr, fi


def _s5_params(are, aim, ldt, bre, bim, name, deps=()):
    shape = are.shape

    def body(are_ref, aim_ref, ldt_ref, bre_ref, bim_ref, lbr_ref, lbi_ref, bbr_ref, bbi_ref):
        _, lbr, lbi, _, fr, fi = _s5_disc(are_ref[...], aim_ref[...], ldt_ref[...])
        lbr_ref[...] = lbr
        lbi_ref[...] = lbi
        bbr_ref[...] = fr * bre_ref[...] - fi * bim_ref[...]
        bbi_ref[...] = fr * bim_ref[...] + fi * bre_ref[...]

    out = jax.ShapeDtypeStruct(shape, F32)
    return _pallas(body, [are, aim, ldt, bre, bim], deps, name=name,
                   in_specs=[pl.BlockSpec(memory_space=pltpu.VMEM)] * 5, out_shape=[out] * 4,
                   compiler_params=pltpu.CompilerParams(vmem_limit_bytes=VMEM_LIMIT_BYTES))


def _s5_params_bwd(are, aim, ldt, bre, bim, glbr, glbi, gbbr, gbbi, n_groups, name, deps=()):
    shape = are.shape
    rows_per_group = shape[0] // n_groups

    def body(are_ref, aim_ref, ldt_ref, bre_ref, bim_ref, glbr_ref, glbi_ref, gbbr_ref, gbbi_ref,
             gar_ref, gai_ref, gdt_ref, gbr_ref, gbi_ref):
        are_v, aim_v = are_ref[...], aim_ref[...]
        bre_v, bim_v = bre_ref[...], bim_ref[...]
        gbbr_v, gbbi_v = gbbr_ref[...], gbbi_ref[...]
        dt, lbr, lbi, inv, fr, fi = _s5_disc(are_v, aim_v, ldt_ref[...])
        gbr_ref[...] = fr * gbbr_v + fi * gbbi_v
        gbi_ref[...] = fr * gbbi_v - fi * gbbr_v
        lane_group = lax.broadcasted_iota(jnp.int32, (LANES, LANES), 0) // S5_GROUP
        same_group = (lane_group == lax.broadcasted_iota(jnp.int32, (LANES, LANES), 1) // S5_GROUP)
        ones = same_group.astype(F32)
        gfr = jnp.dot(bre_v * gbbr_v + bim_v * gbbi_v, ones, precision=HIGHEST, preferred_element_type=F32)
        gfi = jnp.dot(bre_v * gbbi_v - bim_v * gbbr_v, ones, precision=HIGHEST, preferred_element_type=F32)
        glr = glbr_ref[...] + (are_v * gfr - aim_v * gfi) * inv
        gli = glbi_ref[...] + (are_v * gfi + aim_v * gfr) * inv
        qr = (fr * are_v + fi * aim_v) * inv
        qi = (fi * are_v - fr * aim_v) * inv
        gzr = lbr * glr + lbi * gli
        gzi = lbr * gli - lbi * glr
        gar_ref[...] = dt * gzr - (qr * gfr + qi * gfi)
        gai_ref[...] = dt * gzi - (qr * gfi - qi * gfr)
        e = dt * (are_v * gzr + aim_v * gzi)
        per_group = jnp.sum(e.reshape(n_groups, rows_per_group, LANES), axis=1)
        total = jnp.sum(per_group, axis=1, keepdims=True) * (1.0 / S5_GROUP)
        gdt_ref[...] = jnp.broadcast_to(total, gdt_ref.shape)

    out = jax.ShapeDtypeStruct(shape, F32)
    return _pallas(
        body, [are, aim, ldt, bre, bim, glbr, glbi, gbbr, gbbi], deps, name=name,
        in_specs=[pl.BlockSpec(memory_space=pltpu.VMEM)] * 9,
        out_shape=[out, out, jax.ShapeDtypeStruct((n_groups, LANES), F32), out, out],
        compiler_params=pltpu.CompilerParams(vmem_limit_bytes=VMEM_LIMIT_BYTES))


def _cmul(ar, ai, br, bi):
    return ar * br - ai * bi, ar * bi + ai * br


def _scan_in_place(hr_ref, hi_ref, lr, li, reverse):
    l, wb = hr_ref.shape
    nt = l // SUBLANES
    shift = SUBLANES - 1 if reverse else 1
    unroll = 8 if nt % 8 == 0 else 1

    def rows(k):
        t = (nt - 1 - k) if reverse else k
        return pl.ds(pl.multiple_of(t * SUBLANES, SUBLANES), SUBLANES)

    zero = jnp.zeros((SUBLANES, wb), F32)

    def local_step(k, carry):
        hr, hi = carry
        r = rows(k)
        tr_, ti_ = _cmul(lr, li, hr, hi)
        hr, hi = tr_ + hr_ref[r, :], ti_ + hi_ref[r, :]
        hr_ref[r, :] = hr
        hi_ref[r, :] = hi
        return hr, hi

    er, ei = lax.fori_loop(0, nt, local_step, (zero, zero), unroll=unroll)

    lnr = lni = None
    br, bi, n = lr, li, nt
    while n:
        if n & 1:
            lnr, lni = (br, bi) if lnr is None else _cmul(lnr, lni, br, bi)
        n >>= 1
        if n:
            br, bi = _cmul(br, bi, br, bi)

    row = lax.broadcasted_iota(jnp.int32, (SUBLANES, wb), 0)
    tr_, ti_ = er, ei
    for j in range(1, SUBLANES):
        pr_, pi_ = _cmul(lnr, lni, pltpu.roll(tr_, shift, axis=0), pltpu.roll(ti_, shift, axis=0))
        at = row == ((SUBLANES - 1 - j) if reverse else j)
        tr_ = jnp.where(at, er + pr_, tr_)
        ti_ = jnp.where(at, ei + pi_, ti_)
    edge = row == ((SUBLANES - 1) if reverse else 0)
    cr = jnp.where(edge, 0.0, pltpu.roll(tr_, shift, axis=0))
    ci = jnp.where(edge, 0.0, pltpu.roll(ti_, shift, axis=0))

    def fix_step(k, carry):
        zr, zi = _cmul(lr, li, *carry)
        r = rows(k)
        hr_ref[r, :] = hr_ref[r, :] + zr
        hi_ref[r, :] = hi_ref[r, :] + zi
        return zr, zi

    lax.fori_loop(0, nt, fix_step, (cr, ci), unroll=unroll)


def _dot(a, b):
    return jnp.dot(a.astype(BF16), b.astype(BF16), preferred_element_type=F32)


def _s5_forward(u_seg, mb_re, mb_im, mc_re, mc_im, lam_re, lam_im, dvec, name, deps=()):
    l = u_seg.shape[0]
    nb, kin, kst = mb_re.shape

    def body(u_ref, mbr_ref, mbi_ref, mcr_ref, mci_ref, lr_ref, li_ref, d_ref, hr_ref, hi_ref, y_ref):
        u = u_ref[...]
        hr_ref[...] = _dot(u, mbr_ref[...])
        hi_ref[...] = _dot(u, mbi_ref[...])
        _scan_in_place(hr_ref, hi_ref, jnp.broadcast_to(lr_ref[...], (SUBLANES, kst)),
                       jnp.broadcast_to(li_ref[...], (SUBLANES, kst)), False)
        y_ref[...] = (_dot(hr_ref[...], mcr_ref[...]) - _dot(hi_ref[...], mci_ref[...])
                      + d_ref[...] * u.astype(F32))

    act = pl.BlockSpec((l, kin), lambda b: (0, b))
    state = pl.BlockSpec((l, kst), lambda b: (0, b))
    up = pl.BlockSpec((None, kin, kst), lambda b: (b, 0, 0))
    down = pl.BlockSpec((None, kst, kin), lambda b: (b, 0, 0))
    hshape = jax.ShapeDtypeStruct((l, nb * kst), F32)
    return _pallas(
        body, [u_seg, mb_re, mb_im, mc_re, mc_im, lam_re, lam_im, dvec], deps, name=name, grid=(nb,),
        in_specs=[act, up, up, down, down, pl.BlockSpec((1, kst), lambda b: (0, b)),
                  pl.BlockSpec((1, kst), lambda b: (0, b)), pl.BlockSpec((1, kin), lambda b: (0, b))],
        out_specs=[state, state, act],
        out_shape=[hshape, hshape, jax.ShapeDtypeStruct((l, nb * kin), F32)],
        compiler_params=_params("parallel"))


def _dot_ta(a, b):
    return lax.dot_general(a.astype(BF16), b.astype(BF16), (((0,), (0,)), ((), ())), preferred_element_type=F32)


def _dot_nt(a, b):
    return lax.dot_general(a.astype(BF16), b.astype(BF16), (((1,), (1,)), ((), ())), preferred_element_type=F32)


def _s5_backward(dy_seg, u_seg, h_re, h_im, mb_re, mb_im, mc_re, mc_im, lam_re, lam_im_neg, dvec, name):
    l = dy_seg.shape[0]
    nb, kin, kst = mb_re.shape
    nt = l // SUBLANES

    def body(dy_ref, u_ref, hr_ref, hi_ref, mbr_ref, mbi_ref, mcr_ref, mci_ref, lr_ref, li_ref,
             d_ref, du_ref, gcr_ref, gci_ref, gbr_ref, gbi_ref, glr_ref, gli_ref, dsk_ref, qr_ref, qi_ref):
        dy = dy_ref[...]
        qr_ref[...] = _dot_nt(dy, mcr_ref[...])
        qi_ref[...] = -_dot_nt(dy, mci_ref[...])
        _scan_in_place(qr_ref, qi_ref, jnp.broadcast_to(lr_ref[...], (SUBLANES, kst)),
                       jnp.broadcast_to(li_ref[...], (SUBLANES, kst)), True)
        du_ref[...] = _dot_nt(qr_ref[...], mbr_ref[...]) + _dot_nt(qi_ref[...], mbi_ref[...]) + d_ref[...] * dy
        dsk_ref[...] = jnp.sum(dy * _f32(u_ref), axis=0, keepdims=True)
        gcr_ref[...] = _dot_ta(dy, hr_ref[...])
        gci_ref[...] = _dot_ta(dy, hi_ref[...])
        gbr_ref[...] = _dot_ta(u_ref[...], qr_ref[...])
        gbi_ref[...] = _dot_ta(u_ref[...], qi_ref[...])

        row = lax.broadcasted_iota(jnp.int32, (SUBLANES, kst), 0)
        last = pl.ds((nt - 1) * SUBLANES, SUBLANES)
        first = pl.ds(0, SUBLANES)
        pr = jnp.where(row == 0, 0.0, pltpu.roll(hr_ref[last, :], 1, axis=0))
        pi = jnp.where(row == 0, 0.0, pltpu.roll(hi_ref[last, :], 1, axis=0))
        gr, gi = qr_ref[first, :], qi_ref[first, :]

        def step(t, carry):
            acc_r, acc_i = carry
            cur = pl.ds(pl.multiple_of(t * SUBLANES, SUBLANES), SUBLANES)
            prev = pl.ds(pl.multiple_of((t - 1) * SUBLANES, SUBLANES), SUBLANES)
            gr, gi = qr_ref[cur, :], qi_ref[cur, :]
            pr, pi = hr_ref[prev, :], hi_ref[prev, :]
            return acc_r + gr * pr + gi * pi, acc_i + gi * pr - gr * pi

        acc_r, acc_i = lax.fori_loop(1, nt, step, (gr * pr + gi * pi, gi * pr - gr * pi))
        glr_ref[...] = jnp.sum(acc_r, axis=0, keepdims=True)
        gli_ref[...] = jnp.sum(acc_i, axis=0, keepdims=True)

    act = pl.BlockSpec((l, kin), lambda b: (0, b))
    state = pl.BlockSpec((l, kst), lambda b: (0, b))
    up = pl.BlockSpec((None, kin, kst), lambda b: (b, 0, 0))
    down = pl.BlockSpec((None, kst, kin), lambda b: (b, 0, 0))
    vec_st = pl.BlockSpec((1, kst), lambda b: (0, b))
    vec_in = pl.BlockSpec((1, kin), lambda b: (0, b))
    outer = jax.ShapeDtypeStruct((nb, kin, kst), F32)
    lam_shape = jax.ShapeDtypeStruct((1, nb * kst), F32)
    return pl.pallas_call(
        body, name=name, grid=(nb,),
        in_specs=[act, act, state, state, up, up, down, down, vec_st, vec_st, vec_in],
        out_specs=[act, up, up, up, up, vec_st, vec_st, vec_in],
        out_shape=[jax.ShapeDtypeStruct((l, nb * kin), F32), outer, outer, outer, outer, lam_shape, lam_shape,
                   jax.ShapeDtypeStruct((1, nb * kin), F32)],
        scratch_shapes=[pltpu.VMEM((l, kst), F32), pltpu.VMEM((l, kst), F32)],
        compiler_params=_params("parallel"),
    )(dy_seg, u_seg, h_re, h_im, mb_re, mb_im, mc_re, mc_im, lam_re, lam_im_neg, dvec)


def _block_diag(m, nb):
    g, r, s = m.shape
    gb = g // nb
    eye = jnp.eye(gb, dtype=m.dtype)
    out = m.reshape(nb, gb, r, 1, s) * eye[None, :, None, :, None]
    return out.reshape(nb, gb * r, gb * s)


def _block_diag_extract(mat, g, r, s):
    nb = mat.shape[0]
    gb = g // nb
    eye = jnp.eye(gb, dtype=mat.dtype)
    m5 = mat.reshape(nb, gb, r, gb, s) * eye[None, :, None, :, None]
    return jnp.sum(m5, axis=3).reshape(g, r, s)


def _adamw(w, m, v, gslots, name, layer=0, prev=None, deps=()):
    layers, r, c = w.shape
    s = gslots.shape[0]
    tr = _tile(r, max(SUBLANES, 1 << int(math.log2(ADAMW_BLOCK_ELEMS // c))))
    bc1 = 1.0 / (1.0 - ADAM_B1 ** ADAM_STEP)
    bc2 = 1.0 / (1.0 - ADAM_B2 ** ADAM_STEP)

    def body(w_ref, m_ref, v_ref, g_ref, *rest):
        go_ref, d_ref, mo_ref, vo_ref = rest[-4:]
        g = g_ref[0].astype(F32)
        for k in range(1, s):
            g = g + g_ref[k].astype(F32)
        mn = ADAM_B1 * m_ref[...] + (1.0 - ADAM_B1) * g
        vn = ADAM_B2 * v_ref[...] + (1.0 - ADAM_B2) * (g * g)
        go_ref[...] = g
        mo_ref[...] = mn
        vo_ref[...] = vn
        d_ref[...] = -ADAM_LR * ((mn * bc1) / (jnp.sqrt(vn * bc2) + ADAM_EPS) + ADAM_WD * w_ref[...])

    spec = pl.BlockSpec((None, tr, c), lambda i: (layer, i, 0))
    out = jax.ShapeDtypeStruct((layers, r, c), F32)
    in_specs = [spec, spec, spec, pl.BlockSpec((s, tr, c), lambda i: (0, i, 0))]
    args = [w, m, v, gslots]
    aliases = {}
    if prev is not None:
        in_specs += [ANY_SPEC] * 4
        args += list(prev)
        aliases = {4 + q: q for q in range(4)}
    in_specs += [ANY_SPEC] * len(deps)
    args += list(deps)
    return pl.pallas_call(
        body, name=name, grid=(r // tr,), in_specs=in_specs,
        out_specs=[spec] * 4, out_shape=[out] * 4, input_output_aliases=aliases,
        compiler_params=_params("parallel"),
    )(*args)


def _pack(parts):
    flat = jnp.concatenate([p.reshape(-1) for p in parts])
    pad = (-flat.shape[0]) % (PACK_ROWS * LANES)
    return jnp.pad(flat, (0, pad)).reshape(-1, LANES)


def _unpack(packed, shapes):
    flat = packed.reshape(-1)
    out, off = [], 0
    for shp in shapes:
        size = math.prod(shp)
        out.append(flat[off:off + size].reshape(shp))
        off += size
    return out


def kernel(x, norm_g, w_in, conv_w, w_out_a, a_re, a_im, log_dt, b_re, b_im, c_re, c_im, d_skip, w_glu, b_glu, w_out_b, w_o, final_g, loss_target, m_norm_g, m_w_in, m_conv_w, m_w_out_a, m_a_re, m_a_im, m_log_dt, m_b_re, m_b_im, m_c_re, m_c_im, m_d_skip, m_w_glu, m_b_glu, m_w_out_b, m_w_o, m_final_g, v_norm_g, v_w_in, v_conv_w, v_w_out_a, v_a_re, v_a_im, v_log_dt, v_b_re, v_b_im, v_c_re, v_c_im, v_d_skip, v_w_glu, v_b_glu, v_w_out_b, v_w_o, v_final_g):
    depth = norm_g.shape[0]
    l, d = x.shape[1], x.shape[2]
    ws = w_glu.shape[2]
    n_groups, n_state = a_re.shape[1], a_re.shape[2]
    nb = ws // LANES
    assert S5_GROUP == b_re.shape[3] and n_state * S5_GB == 4 * LANES
    u_col, zb_col = 4 * d // ws, 4 * d // ws + 1
    ga_col, gb_col = (4 * d + 2 * ws) // d, (4 * d + 2 * ws) // d + 1
    me = 4 * lax.axis_index("x") + 2 * lax.axis_index("y") + lax.axis_index("c")

    xs = [x[0]]
    tgt = loss_target[0]

    big_names = ("w_in", "w_out_a", "w_glu", "w_out_b", "w_o")
    big = dict(w_in=(w_in, m_w_in, v_w_in), w_out_a=(w_out_a, m_w_out_a, v_w_out_a),
               w_glu=(w_glu, m_w_glu, v_w_glu), w_out_b=(w_out_b, m_w_out_b, v_w_out_b),
               w_o=(w_o, m_w_o, v_w_o))

    all_shards = [[big[k][0][i].astype(BF16) for k in big_names] for i in range(depth)]

    def shards_bf16(i):
        return all_shards[i]

    main_names = ("a_re", "a_im", "log_dt", "b_re", "b_im", "c_re", "c_im", "d_skip", "b_glu", "conv_w")
    small_w = dict(a_re=(a_re, m_a_re, v_a_re), a_im=(a_im, m_a_im, v_a_im),
                   log_dt=(log_dt, m_log_dt, v_log_dt), b_re=(b_re, m_b_re, v_b_re), b_im=(b_im, m_b_im, v_b_im),
                   c_re=(c_re, m_c_re, v_c_re), c_im=(c_im, m_c_im, v_c_im), d_skip=(d_skip, m_d_skip, v_d_skip),
                   b_glu=(b_glu, m_b_glu, v_b_glu))
    main_shapes = [small_w[k][0].shape for k in main_names[:-1]] + [(depth, 3, d)]
    zeros_conv = jnp.zeros((depth, 3, d), F32)
    main_wmv = [_pack([small_w[k][q] for k in main_names[:-1]] + [zeros_conv])[None] for q in range(3)]
    zero1 = jnp.zeros((1,), F32)
    gains_wmv = [_pack([g_, f_, zero1])[None]
                 for g_, f_ in ((norm_g, final_g), (m_norm_g, m_final_g), (v_norm_g, v_final_g))]
    dc = d // N_DEV
    pad8 = lambda a: jnp.pad(a.reshape(depth * 3, dc), ((0, SUBLANES - depth * 3), (0, 0)))[None]
    conv_wmv = [pad8(conv_w), pad8(m_conv_w), pad8(v_conv_w)]

    def gather_start(shards, name, deps=()):
        sems, srcs, lands, token = _exchange_start(
            _plan_gather_chips, shards, [_landing(s_, N_DEV, me) for s_ in shards], f"{name}_start", deps)
        return (name, sems, srcs, lands), token

    def gather_forward(state, after, deps=()):
        name, sems, srcs, lands = state
        _, lands = _exchange_wait(_plan_gather_chips, sems, srcs, lands, after, f"{name}_wait")
        sems, _, lands, token = _exchange_start(_plan_gather_forward, None, lands, f"{name}_forward_start", deps)
        return (name, sems, lands), token

    def gather_finish(state, after):
        name, sems, lands = state
        return _exchange_wait(_plan_gather_forward, sems, None, lands, after, f"{name}_forward_wait")[1]

    def halves_start(shards, name, deps=()):
        srcs = [s_.reshape(2, s_.shape[0] // 2, *s_.shape[1:]) for s_ in shards]
        lands = [lax.dynamic_update_slice(lax.empty((2 * N_DEV,) + h_.shape[1:], h_.dtype), h_, (2 * me, 0, 0))
                 for h_ in srcs]
        sems, srcs, lands, token = _exchange_start(_plan_halves_first, srcs, lands, f"{name}_start", deps)
        return (name, sems, srcs, lands), token

    def halves_second(state, after, deps=()):
        name, sems, srcs, lands = state
        _, lands = _exchange_wait(_plan_halves_first, sems, srcs, lands, after, f"{name}_wait")
        sems, _, lands, token = _exchange_start(_plan_halves_second, None, lands, f"{name}_second_start", deps)
        return (name, sems, lands), token

    def halves_forward(state, after, deps=()):
        name, sems, lands = state
        _, lands = _exchange_wait(_plan_halves_second, sems, None, lands, after, f"{name}_second_wait")
        sems, _, lands, token = _exchange_start(_plan_halves_forward, None, lands, f"{name}_forward_start", deps)
        return (name, sems, lands), token

    def halves_finish(state, after):
        name, sems, lands = state
        lands = _exchange_wait(_plan_halves_forward, sems, None, lands, after, f"{name}_forward_wait")[1]
        return [l_.reshape(N_DEV, 2 * l_.shape[1], *l_.shape[2:]) for l_ in lands]

    conv_shard = jnp.pad(conv_w.reshape(depth * 3, -1), ((0, HALO - depth * 3), (0, 0)))
    w_in_state, token = halves_start([shards_bf16(0)[0], conv_shard], "ag_w_in_0")
    s5 = []
    shape3 = (n_groups, n_state, S5_GROUP)
    for i in range(depth):
        dense_in = (_dense(a_re[i][:, :, None], shape3), _dense(a_im[i][:, :, None], shape3),
                    _dense(log_dt[i][:, None, None], shape3), b_re[i].reshape(-1, LANES), b_im[i].reshape(-1, LANES))
        lbr, lbi, bbr, bbi = _s5_params(*dense_in, f"s5_params_{i}", deps=(token,))
        bbr3, bbi3 = bbr.reshape(shape3), bbi.reshape(shape3)
        diag = lambda m: _block_diag(m, nb).astype(BF16)
        s5.append(dict(
            dense_in=dense_in,
            lam_re=lbr.reshape(shape3)[:, :, 0].reshape(1, -1), lam_im=lbi.reshape(shape3)[:, :, 0].reshape(1, -1),
            up=(diag(bbr3.transpose(0, 2, 1)), diag(bbi3.transpose(0, 2, 1))),
            down=(diag(c_re[i].transpose(0, 2, 1)), diag(c_im[i].transpose(0, 2, 1)))))
    prologue = [m for p in s5 for k in ("up", "down") for m in p[k]]
    prologue += [p[k] for p in s5 for k in ("lam_re", "lam_im")]
    w_in_state, token = halves_second(w_in_state, prologue)
    w_in_state, token = halves_forward(
        w_in_state, main_wmv + gains_wmv + conv_wmv + all_shards[0][1:] + [s_ for sh in all_shards[1:] for s_ in sh],
        deps=(token,))
    rest_state, token = halves_start(shards_bf16(0)[1:], "ag_rest_0", deps=(token,))
    next_state = None
    if depth > 1:
        next_state, token = halves_start([shards_bf16(1)[0]], "ag_w_in_1", deps=(token,))

    saved = []
    wg = [None] * depth
    conv_full = None
    for i in range(depth):
        xi = xs[-1]
        h = _rmsnorm_fwd(xi, norm_g[i], f"rmsnorm_fwd_{i}", deps=(token,))
        arrived = halves_finish(w_in_state, [h])
        if i == 0:
            conv_full = arrived[1].transpose(1, 0, 2).reshape(HALO, d)[:depth * 3].reshape(depth, 3, d)
        conv8 = jnp.pad(conv_full[i], ((0, SUBLANES - 3), (0, 0)))
        proj = _mm_win_fwd(h, arrived[0], f"mm_proj_{i}")
        rest_state, token = halves_second(rest_state, [proj])
        if next_state is not None:
            next_state, token = halves_second(next_state, [token], deps=(token,))
        u_seg = _to_segments(proj[:, 4 * d:4 * d + ws])
        h_re, h_im, y_seg = _s5_forward(u_seg, *s5[i]["up"], *s5[i]["down"], s5[i]["lam_re"], s5[i]["lam_im"],
                                        d_skip[i].reshape(1, ws), f"s5_forward_{i}", deps=(token,))
        rest_state, token = halves_forward(rest_state, [y_seg])
        pa = _branch_a_fwd(proj, conv8, d, f"branch_a_fwd_{i}", deps=(token,))
        rest = halves_finish(rest_state, [pa])
        wg[i] = g = dict(w_in=arrived[0], w_a=rest[0].reshape(d, d), w_glu=rest[1].reshape(ws, ws),
                         w_b=rest[2], w_o=rest[3].reshape(d, d))
        ya = _mm(pa, g["w_a"], name=f"mm_ya_{i}", out_dtype=BF16)
        y = _from_segments(y_seg)
        yg = _gelu_cast(y, f"gelu_{i}")
        gl = _mm(yg, g["w_glu"], name=f"mm_glu_{i}", out_dtype=BF16)
        pb = _glu_post(y, gl, proj, b_glu[i], zb_col, f"glu_post_{i}")
        w_b2d = g["w_b"].transpose(1, 0, 2).reshape(ws, d)
        yb = _mm(pb, w_b2d, name=f"mm_yb_{i}", out_dtype=BF16)
        mrg = _merge_fwd(proj, ya, yb, d, ga_col, gb_col, f"merge_fwd_{i}")
        deps = ()
        if i + 1 < depth:
            w_in_state, token = halves_forward(next_state, [mrg])
            rest_state, token = halves_start(shards_bf16(i + 1)[1:], f"ag_rest_{i + 1}", deps=(token,))
            next_state = None
            if i + 2 < depth:
                next_state, token = halves_start([shards_bf16(i + 2)[0]], f"ag_w_in_{i + 2}", deps=(token,))
            deps = (token,)
        xs.append(_mm(mrg, g["w_o"], name=f"mm_out_{i}", add=xi, deps=deps))
        saved.append(dict(h=h, proj=proj, pa=pa, ya=ya, yb=yb, y=y, yg=yg, gl=gl, pb=pb, mrg=mrg,
                          u_seg=u_seg, h_re=h_re, h_im=h_im, conv8=conv8, w_b2d=w_b2d))

    dx, g_final, loss_part = _final_loss(xs[-1], final_g, tgt, "final_loss")

    rs_pending = []
    small = {k: [None] * depth for k in ("norm_g", "a_re", "a_im", "log_dt", "b_re", "b_im", "c_re", "c_im",
                                         "d_skip", "b_glu", "conv_w")}

    my_chip = 2 * lax.axis_index("x") + lax.axis_index("y")

    def reduce_on_chip(pieces, tag):
        lands = [lax.empty((4,) + p.shape[1:], p.dtype) for p in pieces]
        sems, srcs, lands, token = _exchange_start(_plan_reduce_sibling, pieces, lands, f"rs_sibling_start_{tag}")
        return (sems, srcs, lands), token

    def reduce_across_chips(names_, state, layer, tag, after):
        sems, srcs, lands = state
        srcs, lands = _exchange_wait(_plan_reduce_sibling, sems, srcs, lands, after, f"rs_sibling_wait_{tag}")
        sums = [_chip_sums(p, l_, f"chip_sum_{k}_{layer}") for k, p, l_ in zip(names_, srcs, lands)]
        lands = [_landing(lax.dynamic_index_in_dim(s_, my_chip, 0, keepdims=False), 4, my_chip) for s_ in sums]
        sems, srcs, lands, token = _exchange_start(_plan_reduce_chips, sums, lands, f"rs_chips_start_{tag}")
        rs_pending.append((names_, layer, sems, srcs, lands, f"rs_chips_wait_{tag}"))
        return token

    for i in reversed(range(depth)):
        s, g = saved[i], wg[i]
        proj = s["proj"]
        dxo_b = dx.astype(BF16)
        dm = _mm(dxo_b, g["w_o"], name=f"mm_dm_{i}", nt=True, out_dtype=BF16)
        gw_o = _mm(s["mrg"], dxo_b, ta=True, name=f"mm_gw_o_{i}", out_dtype=BF16)
        dy2, dproj = _merge_bwd(proj, s["ya"], s["yb"], dm, d, ga_col, f"merge_bwd_{i}")
        dya, dyb = dy2[0], dy2[1]
        dpa = _mm(dya, g["w_a"], name=f"mm_dpa_{i}", nt=True, out_dtype=BF16)
        gw_a = _mm(s["pa"], dya, ta=True, name=f"mm_gw_a_{i}", out_dtype=BF16)
        dpb = _mm(dyb, s["w_b2d"], name=f"mm_dpb_{i}", nt=True, out_dtype=BF16)
        gw_b = _mm(s["pb"], dyb, ta=True, name=f"mm_gw_b_{i}", split_n=N_DEV, out_dtype=BF16)
        dproj, dw0, dw1, dw2 = _branch_a_bwd(proj, dpa, s["conv8"], dproj, d, f"branch_a_bwd_{i}")
        small["conv_w"][i] = jnp.concatenate([dw0, dw1, dw2], axis=0)
        dproj, dgl, t1, db_glu = _glu_bwd1(s["y"], s["gl"], proj, b_glu[i], dpb, dproj, zb_col, f"glu_bwd1_{i}")
        small["b_glu"][i] = db_glu.reshape(ws)
        dyg2 = _mm(dgl, g["w_glu"], name=f"mm_dyg_{i}", nt=True, out_dtype=BF16)
        gw_glu = _mm(s["yg"], dgl, ta=True, name=f"mm_gw_glu_{i}", out_dtype=BF16)
        small_names_ = ("w_out_a", "w_glu", "w_out_b", "w_o")
        state, token = reduce_on_chip(
            [gw_a.reshape(N_DEV, d // N_DEV, d), gw_glu.reshape(N_DEV, ws // N_DEV, ws), gw_b,
             gw_o.reshape(N_DEV, d // N_DEV, d)], f"small_{i}")
        dy = _glu_bwd2(s["y"], t1, dyg2, f"glu_bwd2_{i}", deps=(token,))
        dy_seg = _to_segments(dy)
        u_seg = s["u_seg"]
        du_seg, gc_re, gc_im, gbb_re, gbb_im, glam_re, glam_im, dskip = _s5_backward(
            dy_seg, u_seg, s["h_re"], s["h_im"], *s5[i]["up"], *s5[i]["down"],
            s5[i]["lam_re"], -s5[i]["lam_im"], d_skip[i].reshape(1, ws), f"s5_backward_{i}")
        token = reduce_across_chips(small_names_, state, i, f"small_{i}", [du_seg])
        dproj = _write_cols(dproj, _from_segments(du_seg), u_col, f"write_du_{i}")
        gw_in = _mm(s["h"], dproj, ta=True, name=f"mm_gw_in_{i}", split_n=N_DEV, tm=1024,
                    out_dtype=BF16, deps=(token,))
        state, token = reduce_on_chip([gw_in], f"w_in_{i}")
        small["d_skip"][i] = dskip.reshape(n_groups, S5_GROUP)
        small["c_re"][i] = _block_diag_extract(gc_re, n_groups, S5_GROUP, n_state)
        small["c_im"][i] = -_block_diag_extract(gc_im, n_groups, S5_GROUP, n_state)
        gbb_re = _block_diag_extract(gbb_re, n_groups, S5_GROUP, n_state).transpose(0, 2, 1)
        gbb_im = _block_diag_extract(gbb_im, n_groups, S5_GROUP, n_state).transpose(0, 2, 1)
        gar, gai, gdt, gbr, gbi = _s5_params_bwd(
            *s5[i]["dense_in"], _dense(glam_re.reshape(n_groups, n_state, 1), shape3),
            _dense(glam_im.reshape(n_groups, n_state, 1), shape3),
            gbb_re.reshape(-1, LANES), gbb_im.reshape(-1, LANES), n_groups, f"s5_params_bwd_{i}", deps=(token,))
        small["a_re"][i] = gar.reshape(shape3)[:, :, 0]
        small["a_im"][i] = gai.reshape(shape3)[:, :, 0]
        small["log_dt"][i] = gdt[:, 0]
        small["b_re"][i] = gbr.reshape(shape3)
        small["b_im"][i] = gbi.reshape(shape3)
        if i == 0:
            part = {k: jnp.stack(small[k]) for k in main_names}
            main_state, token = gather_start([_pack([part[k] for k in main_names]).astype(BF16)], "ag_small")
            token = reduce_across_chips(("w_in",), state, i, f"w_in_{i}", [token])
            main_state, token = gather_forward(main_state, [token])
            dh = _mm_win_bwd(dproj, g["w_in"], f"mm_dh_{i}", deps=(token,))
            main_slots = gather_finish(main_state, [dh])[0]
            deps = ()
        else:
            dh = _mm_win_bwd(dproj, g["w_in"], f"mm_dh_{i}", deps=(gar,))
            deps = (reduce_across_chips(("w_in",), state, i, f"w_in_{i}", [dh]),)
        dx, dng = _rmsnorm_bwd(xs[i], norm_g[i], dh, dx, f"rmsnorm_bwd_{i}", deps=deps)
        small["norm_g"][i] = dng.reshape(d)

    results = {}

    gain_grads = jnp.concatenate([jnp.stack(small["norm_g"]).reshape(-1), g_final.reshape(d)])
    gain_shapes = [(depth, d), (d,), (1,)]
    gains_state, token = gather_start([_pack([gain_grads, loss_part[0, :1]])], "ag_gains")

    sres = [_unpack(p[0], main_shapes) for p in _adamw(*main_wmv, main_slots, "adamw_small", deps=(token,))]
    for j, k in enumerate(main_names[:-1]):
        results[k] = [sres[q][j] for q in range(4)]
    gconv = lax.dynamic_slice_in_dim(sres[0][-1], me * dc, dc, axis=2)
    cres = _adamw(*conv_wmv, pad8(gconv), "adamw_conv_w")
    results["conv_w"] = [r_[0, :depth * 3].reshape(depth, 3, dc) for r_ in cres]

    after = [cres[0]]
    for names_, layer, sems, srcs, lands, wait_name in rs_pending:
        _, slots = _exchange_wait(_plan_reduce_chips, sems, srcs, lands, after, wait_name)
        for k, land in zip(names_, slots):
            w_, m_, v_ = big[k]
            results[k] = _adamw(w_, m_, v_, land, f"adamw_{k}_{layer}", layer=layer, prev=results.get(k))
            after = [results[k][0]]

    gains_state, token = gather_forward(gains_state, after)
    gpack = gather_finish(gains_state, [token])[0]
    gres = [_unpack(p[0], gain_shapes) for p in _adamw(*gains_wmv, gpack, "adamw_gains")]
    results["norm_g"] = [gres[q][0] for q in range(4)]
    results["final_g"] = [gres[q][1] for q in range(4)]
    loss = gres[0][2][0]

    names = ("norm_g", "w_in", "conv_w", "w_out_a", "a_re", "a_im", "log_dt", "b_re", "b_im", "c_re", "c_im",
             "d_skip", "w_glu", "b_glu", "w_out_b", "w_o", "final_g")
    outs = [loss, dx[None]]
    for q in range(4):
        outs += [results[k][q] for k in names]
    return tuple(outs)
```

```python
import functools
import math

import jax
import jax.numpy as jnp
from jax import lax
from jax.experimental import pallas as pl
from jax.experimental.pallas import tpu as pltpu

F32 = jnp.float32
BF16 = jnp.bfloat16
HIGHEST = lax.Precision.HIGHEST

N_DEV = 8
LANES = 128
SUBLANES = 8
VMEM_LIMIT_BYTES = 56 * 1024 * 1024

RMS_EPS = 1e-6
ADAM_LR = 0.001
ADAM_B1 = 0.9
ADAM_B2 = 0.999
ADAM_EPS = 1e-08
ADAM_WD = 0.01
ADAM_STEP = 10
GELU_C0 = math.sqrt(2.0 / math.pi)
GELU_C1 = 0.044715

ADAMW_BLOCK_ELEMS = 1 << 17
PACK_ROWS = 512

S5_GROUP = 16
S5_GB = LANES // S5_GROUP


def _params(*semantics):
    return pltpu.CompilerParams(dimension_semantics=semantics, vmem_limit_bytes=VMEM_LIMIT_BYTES)


ANY_SPEC = pl.BlockSpec(memory_space=pl.ANY)


def _pallas(body, args, deps=(), *, in_specs, **kwargs):
    deps = tuple(deps)
    if not deps:
        return pl.pallas_call(body, in_specs=in_specs, **kwargs)(*args)

    def body_after(*refs):
        body(*refs[len(deps):])

    return pl.pallas_call(body_after, in_specs=[ANY_SPEC] * len(deps) + list(in_specs), **kwargs)(*deps, *args)


def _tile(n, pref):
    t = min(n, pref)
    while n % t:
        assert t % 2 == 0, (n, pref)
        t //= 2
    return t


def _sigmoid(z):
    return 1.0 / (1.0 + jnp.exp(-z))


def _gelu(y):
    return 0.5 * y * (1.0 + jnp.tanh(GELU_C0 * (y + GELU_C1 * y * y * y)))


def _gelu_grad(y):
    t = jnp.tanh(GELU_C0 * (y + GELU_C1 * y * y * y))
    return 0.5 * (1.0 + t) + 0.5 * y * (1.0 - t * t) * GELU_C0 * (1.0 + 3.0 * GELU_C1 * y * y)


HBM_SPEC = pl.BlockSpec(memory_space=pltpu.HBM)
SEM_SPEC = pl.BlockSpec(memory_space=pltpu.SEMAPHORE)
DATAFLOW_EFFECT = pltpu.SideEffectType.DATAFLOW_SIDE_EFFECTING
OTHER_CHIPS = (2, 4, 6)


def _flip(pos, mask):
    x, y, c = pos
    return x ^ ((mask >> 2) & 1), y ^ ((mask >> 1) & 1), c ^ (mask & 1)


def _dev(pos):
    return 4 * pos[0] + 2 * pos[1] + pos[2]


def _chip(pos):
    return 2 * pos[0] + pos[1]


def _plan_gather_chips(me):
    return [(_flip(me, k), None, _dev(me), _dev(_flip(me, k))) for k in (1,) + OTHER_CHIPS]


def _plan_gather_forward(me):
    sib = _flip(me, 1)
    return [(sib, _dev(_flip(me, k)), _dev(_flip(me, k)), _dev(_flip(sib, k))) for k in OTHER_CHIPS]


def _plan_reduce_sibling(me):
    sib = _flip(me, 1)
    return [(sib, 2 * q + sib[2], q, q) for q in range(4)]


def _plan_reduce_chips(me):
    return [(_flip(me, k), _chip(_flip(me, k)), _chip(me), _chip(_flip(me, k))) for k in OTHER_CHIPS]


X_CHIP, Y_CHIP, XY_CHIP = 4, 2, 6


def _plan_halves_first(me):
    sib, xn, yn = _flip(me, 1), _flip(me, X_CHIP), _flip(me, Y_CHIP)
    return [(sib, 0, 2 * _dev(me), 2 * _dev(sib)), (sib, 1, 2 * _dev(me) + 1, 2 * _dev(sib) + 1),
            (xn, 0, 2 * _dev(me), 2 * _dev(xn)), (yn, 1, 2 * _dev(me) + 1, 2 * _dev(yn) + 1)]


def _plan_halves_second(me):
    xn, yn, dg = _flip(me, X_CHIP), _flip(me, Y_CHIP), _flip(me, XY_CHIP)
    return [(yn, 2 * _dev(me), 2 * _dev(me), 2 * _dev(yn)), (yn, 2 * _dev(xn), 2 * _dev(xn), 2 * _dev(dg)),
            (xn, 2 * _dev(me) + 1, 2 * _dev(me) + 1, 2 * _dev(xn) + 1),
            (xn, 2 * _dev(yn) + 1, 2 * _dev(yn) + 1, 2 * _dev(dg) + 1)]


def _plan_halves_forward(me):
    sib = _flip(me, 1)
    return [(sib, 2 * _dev(_flip(me, k)) + h, 2 * _dev(_flip(me, k)) + h, 2 * _dev(_flip(sib, k)) + h)
            for k in OTHER_CHIPS for h in (0, 1)]


PLAN_COPIES = {_plan_gather_chips: 4, _plan_gather_forward: 3, _plan_reduce_sibling: 4, _plan_reduce_chips: 3,
               _plan_halves_first: 4, _plan_halves_second: 4, _plan_halves_forward: 6}


def _exchange_copies(plan, src_refs, land_refs, send_sems, recv_sems, incoming=True):
    me = (lax.axis_index("x"), lax.axis_index("y"), lax.axis_index("c"))
    pairs = []
    for b, (src_ref, land_ref) in enumerate(zip(src_refs, land_refs)):
        for j, (peer, src_slot, there, here) in enumerate(plan(me)):
            sem = b * PLAN_COPIES[plan] + j
            src = src_ref if src_slot is None else src_ref.at[src_slot]
            out = pltpu.make_async_remote_copy(
                src_ref=src, dst_ref=land_ref.at[there], send_sem=send_sems.at[sem], recv_sem=recv_sems.at[sem],
                device_id=peer, device_id_type=pl.DeviceIdType.MESH)
            inc = pltpu.make_async_remote_copy(
                src_ref=src, dst_ref=land_ref.at[here], send_sem=send_sems.at[sem], recv_sem=recv_sems.at[sem],
                device_id=peer, device_id_type=pl.DeviceIdType.MESH) if incoming else None
            pairs.append((out, inc))
    return pairs


def _exchange_start(plan, srcs, lands, name, deps=()):
    srcs = [] if srcs is None else list(srcs)
    ns, n, nd = len(srcs), len(lands), len(deps)

    def body(*refs):
        land_refs = refs[ns:ns + n]
        sems_at = ns + n + nd
        pairs = _exchange_copies(plan, refs[:ns] if ns else land_refs, land_refs, refs[sems_at], refs[sems_at + 1],
                                 incoming=False)
        for out, _ in pairs:
            out.start()
        token = refs[-1]
        token[...] = jnp.zeros_like(token)

    sems = pltpu.SemaphoreType.DMA((PLAN_COPIES[plan] * n,))
    bufs = srcs + list(lands)
    outs = pl.pallas_call(
        body, name=name,
        out_shape=(sems, sems, *[pltpu.HBM(a.shape, a.dtype) for a in bufs],
                   jax.ShapeDtypeStruct((SUBLANES, LANES), F32)),
        in_specs=[HBM_SPEC] * (ns + n) + [ANY_SPEC] * nd,
        out_specs=(SEM_SPEC, SEM_SPEC, *[HBM_SPEC] * (ns + n), pl.BlockSpec(memory_space=pltpu.VMEM)),
        input_output_aliases={i: 2 + i for i in range(ns + n)},
        compiler_params=pltpu.CompilerParams(has_side_effects=DATAFLOW_EFFECT),
    )(*[pltpu.with_memory_space_constraint(a, pltpu.HBM) for a in bufs], *deps)
    return (outs[0], outs[1]), (outs[2:2 + ns] if ns else None), outs[2 + ns:2 + ns + n], outs[-1]


def _exchange_wait(plan, sems, srcs, lands, after, name):
    srcs = [] if srcs is None else list(srcs)
    ns, n = len(srcs), len(lands)

    def body(*refs):
        land_refs = refs[ns:ns + n]
        pairs = _exchange_copies(plan, refs[:ns] if ns else land_refs, land_refs, refs[ns + n], refs[ns + n + 1])
        for out, inc in pairs:
            out.wait_send()
            inc.wait_recv()

    bufs = srcs + list(lands)
    outs = pl.pallas_call(
        body, name=name,
        out_shape=[pltpu.HBM(a.shape, a.dtype) for a in bufs],
        in_specs=[HBM_SPEC] * (ns + n) + [SEM_SPEC, SEM_SPEC] + [ANY_SPEC] * len(after),
        out_specs=[HBM_SPEC] * (ns + n),
        input_output_aliases={i: i for i in range(ns + n)},
        compiler_params=pltpu.CompilerParams(has_side_effects=DATAFLOW_EFFECT),
    )(*bufs, sems[0], sems[1], *after)
    return outs[:ns], outs[ns:]


def _landing(own, slots, slot):
    land = lax.empty((slots,) + own.shape, own.dtype)
    return lax.dynamic_update_slice(land, own[None], (slot,) + (0,) * own.ndim)


def _chip_sums(pieces, land, name):
    _, r, c_ = land.shape
    tr = _tile(r, max(2 * SUBLANES, 1 << int(math.log2(4 * ADAMW_BLOCK_ELEMS // c_))))

    def body(core_ref, p_ref, l_ref, o_ref):
        o_ref[...] = (p_ref[...].astype(F32) + l_ref[...].astype(F32)).astype(o_ref.dtype)

    spec = pl.BlockSpec((None, tr, c_), lambda q, i, core: (q, i, 0))
    return pl.pallas_call(
        body, name=name,
        grid_spec=pltpu.PrefetchScalarGridSpec(
            num_scalar_prefetch=1, grid=(4, r // tr),
            in_specs=[pl.BlockSpec((None, tr, c_), lambda q, i, core: (2 * q + core[0], i, 0)), spec],
            out_specs=spec),
        out_shape=jax.ShapeDtypeStruct(land.shape, land.dtype),
        compiler_params=_params("parallel", "parallel"),
    )(lax.axis_index("c").reshape(1), pieces, land)


def _mm(a, b, *, name, nt=False, ta=False, out_dtype=F32, add=None, split_n=None, tm=512, tn=1024, deps=()):
    k, m = a.shape if ta else a.shape[::-1]
    n = b.shape[0] if nt else b.shape[1]
    tm = _tile(m, tm)
    tn = n // split_n if split_n else _tile(n, tn)
    dims = (((0 if ta else 1,), (1 if nt else 0,)), ((), ()))

    def body(*refs):
        a_ref, b_ref = refs[0], refs[1]
        o_ref = refs[-1]
        acc = lax.dot_general(a_ref[...], b_ref[...], dims, preferred_element_type=F32)
        if add is not None:
            acc = acc + refs[2][...]
        o_ref[...] = acc.astype(o_ref.dtype)

    in_specs = [pl.BlockSpec((k, tm), lambda i, j: (0, i)) if ta else pl.BlockSpec((tm, k), lambda i, j: (i, 0)),
                pl.BlockSpec((tn, k), lambda i, j: (j, 0)) if nt
                else pl.BlockSpec((k, tn), lambda i, j: (0, j))]
    args = [a, b]
    if add is not None:
        in_specs.append(pl.BlockSpec((tm, tn), lambda i, j: (i, j)))
        args.append(add)
    if split_n:
        out_shape = jax.ShapeDtypeStruct((split_n, m, tn), out_dtype)
        out_spec = pl.BlockSpec((None, tm, tn), lambda i, j: (j, i, 0))
    else:
        out_shape = jax.ShapeDtypeStruct((m, n), out_dtype)
        out_spec = pl.BlockSpec((tm, tn), lambda i, j: (i, j))
    return _pallas(
        body, args, deps, name=name, grid=(m // tm, n // tn), in_specs=in_specs, out_specs=out_spec,
        out_shape=out_shape, compiler_params=_params("parallel", "parallel"))


def _mm_win_fwd(h, w_g, name, deps=()):
    m, k = h.shape
    nj = w_g.shape[2]
    tm = _tile(m, 512)

    def body(a_ref, b_ref, o_ref):
        o_ref[...] = jnp.dot(a_ref[...], b_ref[...], preferred_element_type=F32).astype(o_ref.dtype)

    return _pallas(
        body, [h, w_g], deps, name=name, grid=(N_DEV, m // tm),
        in_specs=[pl.BlockSpec((tm, k), lambda j, i: (i, 0)),
                  pl.BlockSpec((None, k, nj), lambda j, i: (j, 0, 0))],
        out_specs=pl.BlockSpec((tm, nj), lambda j, i: (i, j)),
        out_shape=jax.ShapeDtypeStruct((m, N_DEV * nj), BF16),
        compiler_params=_params("parallel", "parallel"))


def _mm_win_bwd(dproj, w_g, name, deps=()):
    m = dproj.shape[0]
    d, nj = w_g.shape[1], w_g.shape[2]
    tm = _tile(m, 512)
    tn = _tile(d, 1024)

    per_step = 2

    def body(a_ref, b_ref, o_ref, acc_ref):
        j = pl.program_id(2)

        @pl.when(j == 0)
        def _():
            acc_ref[...] = jnp.zeros_like(acc_ref)

        part = None
        for k in range(per_step):
            term = lax.dot_general(a_ref[:, k * nj:(k + 1) * nj], b_ref[k], (((1,), (1,)), ((), ())),
                                   preferred_element_type=F32)
            part = term if part is None else part + term
        acc_ref[...] += part

        @pl.when(j == N_DEV // per_step - 1)
        def _():
            o_ref[...] = acc_ref[...]

    return _pallas(
        body, [dproj, w_g], deps, name=name, grid=(m // tm, d // tn, N_DEV // per_step),
        in_specs=[pl.BlockSpec((tm, per_step * nj), lambda i, n, j: (i, j)),
                  pl.BlockSpec((per_step, tn, nj), lambda i, n, j: (j, n, 0))],
        out_specs=pl.BlockSpec((tm, tn), lambda i, n, j: (i, n)),
        out_shape=jax.ShapeDtypeStruct((m, d), F32),
        scratch_shapes=[pltpu.VMEM((tm, tn), F32)],
        compiler_params=_params("parallel", "parallel", "arbitrary"))


def _row_spec(tr, w, col):
    return pl.BlockSpec((tr, w), lambda i: (i, col))


def _full_spec(shape):
    return pl.BlockSpec(shape, lambda i: (0,) * len(shape))


def _rmsnorm_fwd(x, g, name, deps=()):
    l, d = x.shape
    tr = _tile(l, 256)

    def body(x_ref, g_ref, o_ref):
        xv = x_ref[...]
        rstd = lax.rsqrt(jnp.mean(xv * xv, axis=-1, keepdims=True) + RMS_EPS)
        o_ref[...] = (xv * rstd * g_ref[...]).astype(o_ref.dtype)

    return _pallas(
        body, [x, g.reshape(1, d)], deps, name=name, grid=(l // tr,),
        in_specs=[_row_spec(tr, d, 0), _full_spec((1, d))],
        out_specs=_row_spec(tr, d, 0),
        out_shape=jax.ShapeDtypeStruct((l, d), BF16),
        compiler_params=_params("parallel"))


def _rmsnorm_bwd(x, g, dh, dxo, name, deps=()):
    l, d = x.shape
    tr = _tile(l, 256)

    def body(x_ref, g_ref, dh_ref, dxo_ref, dx_ref, dg_ref):
        xv = x_ref[...]
        rstd = lax.rsqrt(jnp.mean(xv * xv, axis=-1, keepdims=True) + RMS_EPS)
        dhv = dh_ref[...]
        gdy = dhv * g_ref[...]
        dot = jnp.mean(gdy * xv, axis=-1, keepdims=True)
        dx_ref[...] = dxo_ref[...] + rstd * gdy - xv * (rstd * rstd * rstd * dot)

        @pl.when(pl.program_id(0) == 0)
        def _():
            dg_ref[...] = jnp.zeros_like(dg_ref)

        dg_ref[...] += jnp.sum(dhv * xv * rstd, axis=0, keepdims=True)

    return _pallas(
        body, [x, g.reshape(1, d), dh, dxo], deps, name=name, grid=(l // tr,),
        in_specs=[_row_spec(tr, d, 0), _full_spec((1, d)), _row_spec(tr, d, 0), _row_spec(tr, d, 0)],
        out_specs=[_row_spec(tr, d, 0), _full_spec((1, d))],
        out_shape=[jax.ShapeDtypeStruct((l, d), F32), jax.ShapeDtypeStruct((1, d), F32)],
        compiler_params=_params("arbitrary"))


def _final_loss(x, g, tgt, name):
    l, d = x.shape
    tr = _tile(l, 256)

    def body(x_ref, g_ref, t_ref, dx_ref, dg_ref, loss_ref):
        xv = x_ref[...]
        gv = g_ref[...]
        rstd = lax.rsqrt(jnp.mean(xv * xv, axis=-1, keepdims=True) + RMS_EPS)
        xn = xv * rstd
        err = xn * gv - t_ref[...]
        dy = err * (1.0 / d)
        gdy = dy * gv
        dot = jnp.mean(gdy * xv, axis=-1, keepdims=True)
        dx_ref[...] = rstd * gdy - xv * (rstd * rstd * rstd * dot)

        @pl.when(pl.program_id(0) == 0)
        def _():
            dg_ref[...] = jnp.zeros_like(dg_ref)
            loss_ref[...] = jnp.zeros_like(loss_ref)

        dg_ref[...] += jnp.sum(dy * xn, axis=0, keepdims=True)
        loss_ref[...] += (0.5 / d) * jnp.sum(err * err)

    return pl.pallas_call(
        body, name=name, grid=(l // tr,),
        in_specs=[_row_spec(tr, d, 0), _full_spec((1, d)), _row_spec(tr, d, 0)],
        out_specs=[_row_spec(tr, d, 0), _full_spec((1, d)), _full_spec((SUBLANES, LANES))],
        out_shape=[jax.ShapeDtypeStruct((l, d), F32), jax.ShapeDtypeStruct((1, d), F32),
                   jax.ShapeDtypeStruct((SUBLANES, LANES), F32)],
        compiler_params=_params("arbitrary"),
    )(x, g.reshape(1, d), tgt)


HALO = 2 * SUBLANES


def _halo_spec(tr, w, col, nblk, before):
    step = tr // HALO
    if before:
        return pl.BlockSpec((HALO, w), lambda i: (jnp.maximum(i * step - 1, 0), col))
    return pl.BlockSpec((HALO, w), lambda i: (jnp.minimum((i + 1) * step, nblk - 1), col))


def _shift_down(cur, before, k):
    ext = jnp.concatenate([before, cur], axis=0)
    return pltpu.roll(ext, k, axis=0)[HALO:, :]


def _shift_up(cur, after, k):
    tr = cur.shape[0]
    ext = jnp.concatenate([cur, after], axis=0)
    return pltpu.roll(ext, tr + HALO - k, axis=0)[:tr, :]


def _f32(ref):
    return ref[...].astype(F32)


def _branch_a_fwd(proj, conv_w, d, name, deps=()):
    l = proj.shape[0]
    tr = _tile(l, 256)
    nblk8 = l // HALO

    def body(v_ref, bg_ref, cg_ref, za_ref, vh_ref, cgh_ref, w_ref, o_ref):
        first = pl.program_id(0) == 0
        cv = _f32(cg_ref) * _f32(v_ref)
        cvh = jnp.where(first, 0.0, _f32(cgh_ref) * _f32(vh_ref))
        w0, w1, w2 = w_ref[0:1, :], w_ref[1:2, :], w_ref[2:3, :]
        q = w2 * cv + w1 * _shift_down(cv, cvh, 1) + w0 * _shift_down(cv, cvh, 2)
        za = _f32(za_ref)
        o_ref[...] = (_f32(bg_ref) * q * (za * _sigmoid(za))).astype(o_ref.dtype)

    return _pallas(
        body, [proj, proj, proj, proj, proj, proj, conv_w], deps, name=name, grid=(l // tr,),
        in_specs=[_row_spec(tr, d, 0), _row_spec(tr, d, 1), _row_spec(tr, d, 2), _row_spec(tr, d, 3),
                  _halo_spec(tr, d, 0, nblk8, True), _halo_spec(tr, d, 2, nblk8, True),
                  _full_spec((SUBLANES, d))],
        out_specs=_row_spec(tr, d, 0),
        out_shape=jax.ShapeDtypeStruct((l, d), BF16),
        compiler_params=_params("parallel"))


def _branch_a_bwd(proj, dpa, conv_w, dproj, d, name):
    l = proj.shape[0]
    tr = _tile(l, 128)
    nblk8 = l // HALO
    ntiles = l // tr

    def body(v_ref, bg_ref, cg_ref, za_ref, dpa_ref, vh_ref, cgh_ref, bgn_ref, zan_ref, dpan_ref,
             w_ref, _, o_ref, dw0_ref, dw1_ref, dw2_ref):
        dv_ref, dbg_ref, dcg_ref, dza_ref = [o_ref.at[:, pl.ds(k * d, d)] for k in range(4)]
        i = pl.program_id(0)
        v, bg, cg, za, dpa_v = _f32(v_ref), _f32(bg_ref), _f32(cg_ref), _f32(za_ref), _f32(dpa_ref)
        w0, w1, w2 = w_ref[0:1, :], w_ref[1:2, :], w_ref[2:3, :]
        cv = cg * v
        cvh = jnp.where(i == 0, 0.0, _f32(cgh_ref) * _f32(vh_ref))
        cv1 = _shift_down(cv, cvh, 1)
        cv2 = _shift_down(cv, cvh, 2)
        q = w2 * cv + w1 * cv1 + w0 * cv2
        sg = _sigmoid(za)
        s = za * sg
        dbg_ref[...] = (dpa_v * q * s).astype(dbg_ref.dtype)
        dza_ref[...] = (dpa_v * bg * q * (sg * (1.0 + za * (1.0 - sg)))).astype(dza_ref.dtype)
        dq = dpa_v * bg * s
        zan = _f32(zan_ref)
        dqn = jnp.where(i == ntiles - 1, 0.0, _f32(dpan_ref) * _f32(bgn_ref) * (zan * _sigmoid(zan)))
        dcv = w2 * dq + w1 * _shift_up(dq, dqn, 1) + w0 * _shift_up(dq, dqn, 2)
        dcg_ref[...] = (dcv * v).astype(dcg_ref.dtype)
        dv_ref[...] = (dcv * cg).astype(dv_ref.dtype)

        @pl.when(i == 0)
        def _():
            dw0_ref[...] = jnp.zeros_like(dw0_ref)
            dw1_ref[...] = jnp.zeros_like(dw1_ref)
            dw2_ref[...] = jnp.zeros_like(dw2_ref)

        dw0_ref[...] += jnp.sum(dq * cv2, axis=0, keepdims=True)
        dw1_ref[...] += jnp.sum(dq * cv1, axis=0, keepdims=True)
        dw2_ref[...] += jnp.sum(dq * cv, axis=0, keepdims=True)

    wsum = jax.ShapeDtypeStruct((1, d), F32)
    return pl.pallas_call(
        body, name=name, grid=(ntiles,),
        in_specs=[_row_spec(tr, d, 0), _row_spec(tr, d, 1), _row_spec(tr, d, 2), _row_spec(tr, d, 3),
                  _row_spec(tr, d, 0),
                  _halo_spec(tr, d, 0, nblk8, True), _halo_spec(tr, d, 2, nblk8, True),
                  _halo_spec(tr, d, 1, nblk8, False), _halo_spec(tr, d, 3, nblk8, False),
                  _halo_spec(tr, d, 0, nblk8, False),
                  _full_spec((SUBLANES, d)), ANY_SPEC],
        out_specs=[_row_spec(tr, 4 * d, 0)] + [_full_spec((1, d))] * 3,
        out_shape=[jax.ShapeDtypeStruct(dproj.shape, dproj.dtype)] + [wsum] * 3,
        input_output_aliases={11: 0},
        compiler_params=_params("arbitrary"),
    )(proj, proj, proj, proj, dpa, proj, proj, proj, proj, dpa, conv_w, dproj)


def _gelu_cast(y, name):
    l, w = y.shape
    tr = _tile(l, 512)

    def body(y_ref, o_ref):
        o_ref[...] = _gelu(y_ref[...]).astype(o_ref.dtype)

    return pl.pallas_call(
        body, name=name, grid=(l // tr,), in_specs=[_row_spec(tr, w, 0)],
        out_specs=_row_spec(tr, w, 0), out_shape=jax.ShapeDtypeStruct((l, w), BF16),
        compiler_params=_params("parallel"),
    )(y)


def _glu_post(y, gl, proj, b_glu, zb_col, name):
    l, w = y.shape
    tr = _tile(l, 512)

    def body(y_ref, gl_ref, zb_ref, b_ref, o_ref):
        zb = _f32(zb_ref)
        o_ref[...] = (_gelu(y_ref[...]) * _sigmoid(_f32(gl_ref) + b_ref[...])
                      * (zb * _sigmoid(zb))).astype(o_ref.dtype)

    return pl.pallas_call(
        body, name=name, grid=(l // tr,),
        in_specs=[_row_spec(tr, w, 0), _row_spec(tr, w, 0), _row_spec(tr, w, zb_col), _full_spec((1, w))],
        out_specs=_row_spec(tr, w, 0), out_shape=jax.ShapeDtypeStruct((l, w), BF16),
        compiler_params=_params("parallel"),
    )(y, gl, proj, b_glu.reshape(1, w))


def _glu_bwd1(y, gl, proj, b_glu, dpb, dproj, zb_col, name):
    l, w = y.shape
    tr = _tile(l, 512)

    def body(y_ref, gl_ref, zb_ref, b_ref, dpb_ref, _, dzb_ref, dgl_ref, t_ref, db_ref):
        zb = _f32(zb_ref)
        dpb_v = _f32(dpb_ref)
        yg = _gelu(y_ref[...])
        sgl = _sigmoid(_f32(gl_ref) + b_ref[...])
        szb = _sigmoid(zb)
        dzb_ref[...] = (dpb_v * yg * sgl * (szb * (1.0 + zb * (1.0 - szb)))).astype(dzb_ref.dtype)
        e = dpb_v * (zb * szb)
        dgl = e * yg * sgl * (1.0 - sgl)
        dgl_ref[...] = dgl.astype(dgl_ref.dtype)
        t_ref[...] = e * sgl

        @pl.when(pl.program_id(0) == 0)
        def _():
            db_ref[...] = jnp.zeros_like(db_ref)

        db_ref[...] += jnp.sum(dgl, axis=0, keepdims=True)

    return pl.pallas_call(
        body, name=name, grid=(l // tr,),
        in_specs=[_row_spec(tr, w, 0), _row_spec(tr, w, 0), _row_spec(tr, w, zb_col), _full_spec((1, w)),
                  _row_spec(tr, w, 0), ANY_SPEC],
        out_specs=[_row_spec(tr, w, zb_col)] + [_row_spec(tr, w, 0)] * 2 + [_full_spec((1, w))],
        out_shape=[jax.ShapeDtypeStruct(dproj.shape, dproj.dtype), jax.ShapeDtypeStruct((l, w), BF16),
                   jax.ShapeDtypeStruct((l, w), F32), jax.ShapeDtypeStruct((1, w), F32)],
        input_output_aliases={5: 0},
        compiler_params=_params("arbitrary"),
    )(y, gl, proj, b_glu.reshape(1, w), dpb, dproj)


def _write_cols(dproj, cols, col, name):
    l, w = cols.shape
    tr = _tile(l, 512)

    def body(c_ref, _, o_ref):
        o_ref[...] = c_ref[...].astype(o_ref.dtype)

    return pl.pallas_call(
        body, name=name, grid=(l // tr,), in_specs=[_row_spec(tr, w, 0), ANY_SPEC],
        out_specs=_row_spec(tr, w, col), out_shape=jax.ShapeDtypeStruct(dproj.shape, dproj.dtype),
        input_output_aliases={1: 0}, compiler_params=_params("parallel"),
    )(cols, dproj)


def _glu_bwd2(y, t1, dyg2, name, deps=()):
    l, w = y.shape
    tr = _tile(l, 512)

    def body(y_ref, t_ref, d_ref, o_ref):
        o_ref[...] = (t_ref[...] + _f32(d_ref)) * _gelu_grad(y_ref[...])

    return _pallas(
        body, [y, t1, dyg2], deps, name=name, grid=(l // tr,), in_specs=[_row_spec(tr, w, 0)] * 3,
        out_specs=_row_spec(tr, w, 0), out_shape=jax.ShapeDtypeStruct((l, w), F32),
        compiler_params=_params("parallel"))


def _merge_fwd(proj, ya, yb, d, ga_col, gb_col, name):
    l = proj.shape[0]
    tr = _tile(l, 256)

    def body(ga_ref, gb_ref, ya_ref, yb_ref, o_ref):
        o_ref[...] = (_sigmoid(_f32(ga_ref)) * _f32(ya_ref)
                      + _sigmoid(_f32(gb_ref)) * _f32(yb_ref)).astype(o_ref.dtype)

    return pl.pallas_call(
        body, name=name, grid=(l // tr,),
        in_specs=[_row_spec(tr, d, ga_col), _row_spec(tr, d, gb_col), _row_spec(tr, d, 0), _row_spec(tr, d, 0)],
        out_specs=_row_spec(tr, d, 0), out_shape=jax.ShapeDtypeStruct((l, d), BF16),
        compiler_params=_params("parallel"),
    )(proj, proj, ya, yb)


def _merge_bwd(proj, ya, yb, dm, d, ga_col, name):
    l, n = proj.shape
    tr = _tile(l, 256)

    def body(g_ref, ya_ref, yb_ref, dm_ref, dy_ref, dg_ref):
        dmv = _f32(dm_ref)
        sg = _sigmoid(_f32(g_ref))
        yv = jnp.where(pl.program_id(1) == 0, _f32(ya_ref), _f32(yb_ref))
        dy_ref[...] = (dmv * sg).astype(dy_ref.dtype)
        dg_ref[...] = (dmv * yv * sg * (1.0 - sg)).astype(dg_ref.dtype)

    row = pl.BlockSpec((tr, d), lambda i, j: (i, 0))
    return pl.pallas_call(
        body, name=name, grid=(l // tr, 2),
        in_specs=[pl.BlockSpec((tr, d), lambda i, j: (i, ga_col + j)), row, row, row],
        out_specs=[pl.BlockSpec((None, tr, d), lambda i, j: (j, i, 0)),
                   pl.BlockSpec((tr, d), lambda i, j: (i, ga_col + j))],
        out_shape=[jax.ShapeDtypeStruct((2, l, d), BF16), jax.ShapeDtypeStruct((l, n), BF16)],
        compiler_params=_params("parallel", "arbitrary"),
    )(proj, ya, yb, dm)


def _to_segments(a):
    l, w = a.shape
    return a.reshape(SUBLANES, l // SUBLANES, w).transpose(1, 0, 2).reshape(l, w)


def _from_segments(a):
    l, w = a.shape
    return a.reshape(l // SUBLANES, SUBLANES, w).transpose(1, 0, 2).reshape(l, w)


def _dense(z, shape):
    return jnp.broadcast_to(z, shape).reshape(-1, LANES)


def _s5_disc(are, aim, ldt):
    dt = jnp.exp(ldt)
    er = jnp.exp(are * dt)
    lbr = er * jnp.cos(aim * dt)
    lbi = er * jnp.sin(aim * dt)
    inv = 1.0 / (are * are + aim * aim)
    fr = ((lbr - 1.0) * are + lbi * aim) * inv
    fi = (lbi * are - (lbr - 1.0) * aim) * inv
    return dt, lbr, lbi, inv, fr, fi


def _s5_params(are, aim, ldt, bre, bim, input_matrix, name, deps=()):
    shape = are.shape

    def body(are_ref, aim_ref, ldt_ref, bre_ref, bim_ref, re_ref, im_ref):
        _, lbr, lbi, _, fr, fi = _s5_disc(are_ref[...], aim_ref[...], ldt_ref[...])
        if input_matrix:
            re_ref[...] = fr * bre_ref[...] - fi * bim_ref[...]
            im_ref[...] = fr * bim_ref[...] + fi * bre_ref[...]
        else:
            re_ref[...] = lbr
            im_ref[...] = lbi

    out = jax.ShapeDtypeStruct(shape, F32)
    return _pallas(body, [are, aim, ldt, bre, bim], deps, name=name,
                   in_specs=[pl.BlockSpec(memory_space=pltpu.VMEM)] * 5, out_shape=[out] * 2,
                   compiler_params=pltpu.CompilerParams(vmem_limit_bytes=VMEM_LIMIT_BYTES))


def _s5_params_bwd(are, aim, ldt, bre, bim, glbr, glbi, gbbr, gbbi, n_groups, name, deps=()):
    shape = are.shape
    rows_per_group = shape[0] // n_groups

    def body(are_ref, aim_ref, ldt_ref, bre_ref, bim_ref, glbr_ref, glbi_ref, gbbr_ref, gbbi_ref,
             gar_ref, gai_ref, gdt_ref, gbr_ref, gbi_ref):
        are_v, aim_v = are_ref[...], aim_ref[...]
        bre_v, bim_v = bre_ref[...], bim_ref[...]
        gbbr_v, gbbi_v = gbbr_ref[...], gbbi_ref[...]
        dt, lbr, lbi, inv, fr, fi = _s5_disc(are_v, aim_v, ldt_ref[...])
        gbr_ref[...] = fr * gbbr_v + fi * gbbi_v
        gbi_ref[...] = fr * gbbi_v - fi * gbbr_v
        lane_group = lax.broadcasted_iota(jnp.int32, (LANES, LANES), 0) // S5_GROUP
        same_group = (lane_group == lax.broadcasted_iota(jnp.int32, (LANES, LANES), 1) // S5_GROUP)
        ones = same_group.astype(F32)
        gfr = jnp.dot(bre_v * gbbr_v + bim_v * gbbi_v, ones, precision=HIGHEST, preferred_element_type=F32)
        gfi = jnp.dot(bre_v * gbbi_v - bim_v * gbbr_v, ones, precision=HIGHEST, preferred_element_type=F32)
        glr = glbr_ref[...] + (are_v * gfr - aim_v * gfi) * inv
        gli = glbi_ref[...] + (are_v * gfi + aim_v * gfr) * inv
        qr = (fr * are_v + fi * aim_v) * inv
        qi = (fi * are_v - fr * aim_v) * inv
        gzr = lbr * glr + lbi * gli
        gzi = lbr * gli - lbi * glr
        gar_ref[...] = dt * gzr - (qr * gfr + qi * gfi)
        gai_ref[...] = dt * gzi - (qr * gfi - qi * gfr)
        e = dt * (are_v * gzr + aim_v * gzi)
        per_group = jnp.sum(e.reshape(n_groups, rows_per_group, LANES), axis=1)
        total = jnp.sum(per_group, axis=1, keepdims=True) * (1.0 / S5_GROUP)
        gdt_ref[...] = jnp.broadcast_to(total, gdt_ref.shape)

    out = jax.ShapeDtypeStruct(shape, F32)
    return _pallas(
        body, [are, aim, ldt, bre, bim, glbr, glbi, gbbr, gbbi], deps, name=name,
        in_specs=[pl.BlockSpec(memory_space=pltpu.VMEM)] * 9,
        out_shape=[out, out, jax.ShapeDtypeStruct((n_groups, LANES), F32), out, out],
        compiler_params=pltpu.CompilerParams(vmem_limit_bytes=VMEM_LIMIT_BYTES))


def _cmul(ar, ai, br, bi):
    return ar * br - ai * bi, ar * bi + ai * br


def _scan_in_place(hr_ref, hi_ref, lr, li, reverse):
    l, wb = hr_ref.shape
    nt = l // SUBLANES
    shift = SUBLANES - 1 if reverse else 1
    unroll = 8 if nt % 8 == 0 else 1

    def rows(k):
        t = (nt - 1 - k) if reverse else k
        return pl.ds(pl.multiple_of(t * SUBLANES, SUBLANES), SUBLANES)

    zero = jnp.zeros((SUBLANES, wb), F32)

    def local_step(k, carry):
        hr, hi = carry
        r = rows(k)
        tr_, ti_ = _cmul(lr, li, hr, hi)
        hr, hi = tr_ + hr_ref[r, :], ti_ + hi_ref[r, :]
        hr_ref[r, :] = hr
        hi_ref[r, :] = hi
        return hr, hi

    er, ei = lax.fori_loop(0, nt, local_step, (zero, zero), unroll=unroll)

    lnr = lni = None
    br, bi, n = lr, li, nt
    while n:
        if n & 1:
            lnr, lni = (br, bi) if lnr is None else _cmul(lnr, lni, br, bi)
        n >>= 1
        if n:
            br, bi = _cmul(br, bi, br, bi)

    row = lax.broadcasted_iota(jnp.int32, (SUBLANES, wb), 0)
    tr_, ti_ = er, ei
    for j in range(1, SUBLANES):
        pr_, pi_ = _cmul(lnr, lni, pltpu.roll(tr_, shift, axis=0), pltpu.roll(ti_, shift, axis=0))
        at = row == ((SUBLANES - 1 - j) if reverse else j)
        tr_ = jnp.where(at, er + pr_, tr_)
        ti_ = jnp.where(at, ei + pi_, ti_)
    edge = row == ((SUBLANES - 1) if reverse else 0)
    cr = jnp.where(edge, 0.0, pltpu.roll(tr_, shift, axis=0))
    ci = jnp.where(edge, 0.0, pltpu.roll(ti_, shift, axis=0))

    def fix_step(k, carry):
        zr, zi = _cmul(lr, li, *carry)
        r = rows(k)
        hr_ref[r, :] = hr_ref[r, :] + zr
        hi_ref[r, :] = hi_ref[r, :] + zi
        return zr, zi

    lax.fori_loop(0, nt, fix_step, (cr, ci), unroll=unroll)


def _dot(a, b):
    return jnp.dot(a.astype(BF16), b.astype(BF16), preferred_element_type=F32)


def _s5_forward(u_seg, mb_re, mb_im, mc_re, mc_im, lam_re, lam_im, dvec, name, deps=()):
    l = u_seg.shape[0]
    nb, kin, kst = mb_re.shape

    def body(u_ref, mbr_ref, mbi_ref, mcr_ref, mci_ref, lr_ref, li_ref, d_ref, hr_ref, hi_ref, y_ref):
        u = u_ref[...]
        hr_ref[...] = _dot(u, mbr_ref[...])
        hi_ref[...] = _dot(u, mbi_ref[...])
        _scan_in_place(hr_ref, hi_ref, jnp.broadcast_to(lr_ref[...], (SUBLANES, kst)),
                       jnp.broadcast_to(li_ref[...], (SUBLANES, kst)), False)
        y_ref[...] = (_dot(hr_ref[...], mcr_ref[...]) - _dot(hi_ref[...], mci_ref[...])
                      + d_ref[...] * u.astype(F32))

    act = pl.BlockSpec((l, kin), lambda b: (0, b))
    state = pl.BlockSpec((l, kst), lambda b: (0, b))
    up = pl.BlockSpec((None, kin, kst), lambda b: (b, 0, 0))
    down = pl.BlockSpec((None, kst, kin), lambda b: (b, 0, 0))
    hshape = jax.ShapeDtypeStruct((l, nb * kst), F32)
    return _pallas(
        body, [u_seg, mb_re, mb_im, mc_re, mc_im, lam_re, lam_im, dvec], deps, name=name, grid=(nb,),
        in_specs=[act, up, up, down, down, pl.BlockSpec((1, kst), lambda b: (0, b)),
                  pl.BlockSpec((1, kst), lambda b: (0, b)), pl.BlockSpec((1, kin), lambda b: (0, b))],
        out_specs=[state, state, act],
        out_shape=[hshape, hshape, jax.ShapeDtypeStruct((l, nb * kin), F32)],
        compiler_params=_params("parallel"))


def _dot_ta(a, b):
    return lax.dot_general(a.astype(BF16), b.astype(BF16), (((0,), (0,)), ((), ())), preferred_element_type=F32)


def _dot_nt(a, b):
    return lax.dot_general(a.astype(BF16), b.astype(BF16), (((1,), (1,)), ((), ())), preferred_element_type=F32)


def _s5_backward(dy_seg, u_seg, h_re, h_im, mb_re, mb_im, mc_re, mc_im, lam_re, lam_im_neg, dvec, name):
    l = dy_seg.shape[0]
    nb, kin, kst = mb_re.shape
    nt = l // SUBLANES

    def body(dy_ref, u_ref, hr_ref, hi_ref, mbr_ref, mbi_ref, mcr_ref, mci_ref, lr_ref, li_ref,
             d_ref, du_ref, gcr_ref, gci_ref, gbr_ref, gbi_ref, glr_ref, gli_ref, dsk_ref, qr_ref, qi_ref):
        dy = dy_ref[...]
        qr_ref[...] = _dot_nt(dy, mcr_ref[...])
        qi_ref[...] = -_dot_nt(dy, mci_ref[...])
        _scan_in_place(qr_ref, qi_ref, jnp.broadcast_to(lr_ref[...], (SUBLANES, kst)),
                       jnp.broadcast_to(li_ref[...], (SUBLANES, kst)), True)
        du_ref[...] = _dot_nt(qr_ref[...], mbr_ref[...]) + _dot_nt(qi_ref[...], mbi_ref[...]) + d_ref[...] * dy
        dsk_ref[...] = jnp.sum(dy * _f32(u_ref), axis=0, keepdims=True)
        gcr_ref[...] = _dot_ta(dy, hr_ref[...])
        gci_ref[...] = _dot_ta(dy, hi_ref[...])
        gbr_ref[...] = _dot_ta(u_ref[...], qr_ref[...])
        gbi_ref[...] = _dot_ta(u_ref[...], qi_ref[...])

        row = lax.broadcasted_iota(jnp.int32, (SUBLANES, kst), 0)
        last = pl.ds((nt - 1) * SUBLANES, SUBLANES)
        first = pl.ds(0, SUBLANES)
        pr = jnp.where(row == 0, 0.0, pltpu.roll(hr_ref[last, :], 1, axis=0))
        pi = jnp.where(row == 0, 0.0, pltpu.roll(hi_ref[last, :], 1, axis=0))
        gr, gi = qr_ref[first, :], qi_ref[first, :]

        def step(t, carry):
            acc_r, acc_i = carry
            cur = pl.ds(pl.multiple_of(t * SUBLANES, SUBLANES), SUBLANES)
            prev = pl.ds(pl.multiple_of((t - 1) * SUBLANES, SUBLANES), SUBLANES)
            gr, gi = qr_ref[cur, :], qi_ref[cur, :]
            pr, pi = hr_ref[prev, :], hi_ref[prev, :]
            return acc_r + gr * pr + gi * pi, acc_i + gi * pr - gr * pi

        acc_r, acc_i = lax.fori_loop(1, nt, step, (gr * pr + gi * pi, gi * pr - gr * pi))
        glr_ref[...] = jnp.sum(acc_r, axis=0, keepdims=True)
        gli_ref[...] = jnp.sum(acc_i, axis=0, keepdims=True)

    act = pl.BlockSpec((l, kin), lambda b: (0, b))
    state = pl.BlockSpec((l, kst), lambda b: (0, b))
    up = pl.BlockSpec((None, kin, kst), lambda b: (b, 0, 0))
    down = pl.BlockSpec((None, kst, kin), lambda b: (b, 0, 0))
    vec_st = pl.BlockSpec((1, kst), lambda b: (0, b))
    vec_in = pl.BlockSpec((1, kin), lambda b: (0, b))
    outer = jax.ShapeDtypeStruct((nb, kin, kst), F32)
    lam_shape = jax.ShapeDtypeStruct((1, nb * kst), F32)
    return pl.pallas_call(
        body, name=name, grid=(nb,),
        in_specs=[act, act, state, state, up, up, down, down, vec_st, vec_st, vec_in],
        out_specs=[act, up, up, up, up, vec_st, vec_st, vec_in],
        out_shape=[jax.ShapeDtypeStruct((l, nb * kin), F32), outer, outer, outer, outer, lam_shape, lam_shape,
                   jax.ShapeDtypeStruct((1, nb * kin), F32)],
        scratch_shapes=[pltpu.VMEM((l, kst), F32), pltpu.VMEM((l, kst), F32)],
        compiler_params=_params("parallel"),
    )(dy_seg, u_seg, h_re, h_im, mb_re, mb_im, mc_re, mc_im, lam_re, lam_im_neg, dvec)


def _block_diag(m, nb):
    g, r, s = m.shape
    gb = g // nb
    eye = jnp.eye(gb, dtype=m.dtype)
    out = m.reshape(nb, gb, r, 1, s) * eye[None, :, None, :, None]
    return out.reshape(nb, gb * r, gb * s)


def _block_diag_extract(mat, g, r, s):
    nb = mat.shape[0]
    gb = g // nb
    eye = jnp.eye(gb, dtype=mat.dtype)
    m5 = mat.reshape(nb, gb, r, gb, s) * eye[None, :, None, :, None]
    return jnp.sum(m5, axis=3).reshape(g, r, s)


def _adamw(w, m, v, gslots, name, layer=0, prev=None, deps=()):
    layers, r, c = w.shape
    s = gslots.shape[0]
    tr = _tile(r, max(SUBLANES, 1 << int(math.log2(ADAMW_BLOCK_ELEMS // c))))
    bc1 = 1.0 / (1.0 - ADAM_B1 ** ADAM_STEP)
    bc2 = 1.0 / (1.0 - ADAM_B2 ** ADAM_STEP)

    def body(w_ref, m_ref, v_ref, g_ref, *rest):
        go_ref, d_ref, mo_ref, vo_ref = rest[-4:]
        g = g_ref[0].astype(F32)
        for k in range(1, s):
            g = g + g_ref[k].astype(F32)
        mn = ADAM_B1 * m_ref[...] + (1.0 - ADAM_B1) * g
        vn = ADAM_B2 * v_ref[...] + (1.0 - ADAM_B2) * (g * g)
        go_ref[...] = g
        mo_ref[...] = mn
        vo_ref[...] = vn
        d_ref[...] = -ADAM_LR * ((mn * bc1) / (jnp.sqrt(vn * bc2) + ADAM_EPS) + ADAM_WD * w_ref[...])

    spec = pl.BlockSpec((None, tr, c), lambda i: (layer, i, 0))
    out = jax.ShapeDtypeStruct((layers, r, c), F32)
    in_specs = [spec, spec, spec, pl.BlockSpec((s, tr, c), lambda i: (0, i, 0))]
    args = [w, m, v, gslots]
    aliases = {}
    if prev is not None:
        in_specs += [ANY_SPEC] * 4
        args += list(prev)
        aliases = {4 + q: q for q in range(4)}
    in_specs += [ANY_SPEC] * len(deps)
    args += list(deps)
    return pl.pallas_call(
        body, name=name, grid=(r // tr,), in_specs=in_specs,
        out_specs=[spec] * 4, out_shape=[out] * 4, input_output_aliases=aliases,
        compiler_params=_params("parallel"),
    )(*args)


def _pack(parts):
    flat = jnp.concatenate([p.reshape(-1) for p in parts])
    pad = (-flat.shape[0]) % (PACK_ROWS * LANES)
    return jnp.pad(flat, (0, pad)).reshape(-1, LANES)


def _unpack(packed, shapes):
    flat = packed.reshape(-1)
    out, off = [], 0
    for shp in shapes:
        size = math.prod(shp)
        out.append(flat[off:off + size].reshape(shp))
        off += size
    return out


def kernel(x, norm_g, w_in, conv_w, w_out_a, a_re, a_im, log_dt, b_re, b_im, c_re, c_im, d_skip, w_glu, b_glu, w_out_b, w_o, final_g, loss_target, m_norm_g, m_w_in, m_conv_w, m_w_out_a, m_a_re, m_a_im, m_log_dt, m_b_re, m_b_im, m_c_re, m_c_im, m_d_skip, m_w_glu, m_b_glu, m_w_out_b, m_w_o, m_final_g, v_norm_g, v_w_in, v_conv_w, v_w_out_a, v_a_re, v_a_im, v_log_dt, v_b_re, v_b_im, v_c_re, v_c_im, v_d_skip, v_w_glu, v_b_glu, v_w_out_b, v_w_o, v_final_g):
    depth = norm_g.shape[0]
    l, d = x.shape[1], x.shape[2]
    ws = w_glu.shape[2]
    n_groups, n_state = a_re.shape[1], a_re.shape[2]
    nb = ws // LANES
    assert S5_GROUP == b_re.shape[3] and n_state * S5_GB == 4 * LANES
    u_col, zb_col = 4 * d // ws, 4 * d // ws + 1
    ga_col, gb_col = (4 * d + 2 * ws) // d, (4 * d + 2 * ws) // d + 1
    me = 4 * lax.axis_index("x") + 2 * lax.axis_index("y") + lax.axis_index("c")

    xs = [x[0]]
    tgt = loss_target[0]

    big_names = ("w_in", "w_out_a", "w_glu", "w_out_b", "w_o")
    big = dict(w_in=(w_in, m_w_in, v_w_in), w_out_a=(w_out_a, m_w_out_a, v_w_out_a),
               w_glu=(w_glu, m_w_glu, v_w_glu), w_out_b=(w_out_b, m_w_out_b, v_w_out_b),
               w_o=(w_o, m_w_o, v_w_o))

    all_shards = [[None] * len(big_names) for _ in range(depth)]
    all_shards[0][0] = w_in[0].astype(BF16)

    def shards_bf16(i):
        return all_shards[i]

    main_names = ("a_re", "a_im", "log_dt", "b_re", "b_im", "c_re", "c_im", "d_skip", "b_glu", "conv_w")
    small_w = dict(a_re=(a_re, m_a_re, v_a_re), a_im=(a_im, m_a_im, v_a_im),
                   log_dt=(log_dt, m_log_dt, v_log_dt), b_re=(b_re, m_b_re, v_b_re), b_im=(b_im, m_b_im, v_b_im),
                   c_re=(c_re, m_c_re, v_c_re), c_im=(c_im, m_c_im, v_c_im), d_skip=(d_skip, m_d_skip, v_d_skip),
                   b_glu=(b_glu, m_b_glu, v_b_glu))
    main_shapes = [small_w[k][0].shape for k in main_names[:-1]] + [(depth, 3, d)]
    dc = d // N_DEV
    pad8 = lambda a: jnp.pad(a.reshape(depth * 3, dc), ((0, SUBLANES - depth * 3), (0, 0)))[None]

    def input_only_work(zero):
        for i in range(depth):
            for j, k in enumerate(big_names):
                if all_shards[i][j] is None:
                    all_shards[i][j] = (big[k][0][i] + zero).astype(BF16)
        zeros_conv = jnp.zeros((depth, 3, d), F32) + zero
        zero1 = jnp.zeros((1,), F32) + zero
        main = [_pack([small_w[k][q] for k in main_names[:-1]] + [zeros_conv])[None] for q in range(3)]
        gains = [_pack([g_, f_, zero1])[None]
                 for g_, f_ in ((norm_g, final_g), (m_norm_g, m_final_g), (v_norm_g, v_final_g))]
        return main, gains, [pad8(conv_w), pad8(m_conv_w), pad8(v_conv_w)]

    def gather_start(shards, name, deps=()):
        sems, srcs, lands, token = _exchange_start(
            _plan_gather_chips, shards, [_landing(s_, N_DEV, me) for s_ in shards], f"{name}_start", deps)
        return (name, sems, srcs, lands), token

    def gather_forward(state, after, deps=()):
        name, sems, srcs, lands = state
        _, lands = _exchange_wait(_plan_gather_chips, sems, srcs, lands, after, f"{name}_wait")
        sems, _, lands, token = _exchange_start(_plan_gather_forward, None, lands, f"{name}_forward_start", deps)
        return (name, sems, lands), token

    def gather_finish(state, after):
        name, sems, lands = state
        return _exchange_wait(_plan_gather_forward, sems, None, lands, after, f"{name}_forward_wait")[1]

    def halves_start(shards, name, deps=()):
        srcs = [s_.reshape(2, s_.shape[0] // 2, *s_.shape[1:]) for s_ in shards]
        lands = [lax.dynamic_update_slice(lax.empty((2 * N_DEV,) + h_.shape[1:], h_.dtype), h_, (2 * me, 0, 0))
                 for h_ in srcs]
        sems, srcs, lands, token = _exchange_start(_plan_halves_first, srcs, lands, f"{name}_start", deps)
        return (name, sems, srcs, lands), token

    def halves_second(state, after, deps=()):
        name, sems, srcs, lands = state
        _, lands = _exchange_wait(_plan_halves_first, sems, srcs, lands, after, f"{name}_wait")
        sems, _, lands, token = _exchange_start(_plan_halves_second, None, lands, f"{name}_second_start", deps)
        return (name, sems, lands), token

    def halves_forward(state, after, deps=()):
        name, sems, lands = state
        _, lands = _exchange_wait(_plan_halves_second, sems, None, lands, after, f"{name}_second_wait")
        sems, _, lands, token = _exchange_start(_plan_halves_forward, None, lands, f"{name}_forward_start", deps)
        return (name, sems, lands), token

    def halves_finish(state, after):
        name, sems, lands = state
        lands = _exchange_wait(_plan_halves_forward, sems, None, lands, after, f"{name}_forward_wait")[1]
        return [l_.reshape(N_DEV, 2 * l_.shape[1], *l_.shape[2:]) for l_ in lands]

    conv_shard = jnp.pad(conv_w.reshape(depth * 3, -1), ((0, HALO - depth * 3), (0, 0)))
    w_in_state, token = halves_start([shards_bf16(0)[0], conv_shard], "ag_w_in_0")
    s5 = []
    shape3 = (n_groups, n_state, S5_GROUP)
    for i in range(depth):
        dense_in = (_dense(a_re[i][:, :, None], shape3), _dense(a_im[i][:, :, None], shape3),
                    _dense(log_dt[i][:, None, None], shape3), b_re[i].reshape(-1, LANES), b_im[i].reshape(-1, LANES))
        lbr, lbi = _s5_params(*dense_in, False, f"s5_lam_{i}", deps=(token,))
        s5.append(dict(dense_in=dense_in, lam_re=lbr.reshape(shape3)[:, :, 0].reshape(1, -1),
                       lam_im=lbi.reshape(shape3)[:, :, 0].reshape(1, -1)))
    w_in_state, token = halves_second(w_in_state, [p[k] for p in s5 for k in ("lam_re", "lam_im")])
    main_wmv, gains_wmv, conv_wmv = input_only_work(token[0, 0])
    for i in range(depth):
        bbr, bbi = _s5_params(*s5[i]["dense_in"], True, f"s5_input_matrix_{i}", deps=(token,))
        bbr3, bbi3 = bbr.reshape(shape3), bbi.reshape(shape3)
        diag = lambda m: _block_diag(m, nb).astype(BF16)
        s5[i].update(up=(diag(bbr3.transpose(0, 2, 1)), diag(bbi3.transpose(0, 2, 1))),
                     down=(diag((c_re[i] + token[0, 0]).transpose(0, 2, 1)),
                           diag((c_im[i] + token[0, 0]).transpose(0, 2, 1))))
    prologue = [m for p in s5 for k in ("up", "down") for m in p[k]]
    prologue += main_wmv + gains_wmv + conv_wmv + all_shards[0][1:] + [s_ for sh in all_shards[1:] for s_ in sh]
    w_in_state, token = halves_forward(w_in_state, prologue, deps=(token,))
    rest_state, token = halves_start(shards_bf16(0)[1:], "ag_rest_0", deps=(token,))
    next_state = None
    if depth > 1:
        next_state, token = halves_start([shards_bf16(1)[0]], "ag_w_in_1", deps=(token,))

    saved = []
    wg = [None] * depth
    conv_full = None
    for i in range(depth):
        xi = xs[-1]
        h = _rmsnorm_fwd(xi, norm_g[i], f"rmsnorm_fwd_{i}", deps=(token,))
        arrived = halves_finish(w_in_state, [h])
        if i == 0:
            conv_full = arrived[1].transpose(1, 0, 2).reshape(HALO, d)[:depth * 3].reshape(depth, 3, d)
        conv8 = jnp.pad(conv_full[i], ((0, SUBLANES - 3), (0, 0)))
        proj = _mm_win_fwd(h, arrived[0], f"mm_proj_{i}")
        rest_state, token = halves_second(rest_state, [proj])
        if next_state is not None:
            next_state, token = halves_second(next_state, [token], deps=(token,))
        u_seg = _to_segments(proj[:, 4 * d:4 * d + ws])
        h_re, h_im, y_seg = _s5_forward(u_seg, *s5[i]["up"], *s5[i]["down"], s5[i]["lam_re"], s5[i]["lam_im"],
                                        d_skip[i].reshape(1, ws), f"s5_forward_{i}", deps=(token,))
        rest_state, token = halves_forward(rest_state, [y_seg])
        pa = _branch_a_fwd(proj, conv8, d, f"branch_a_fwd_{i}", deps=(token,))
        rest = halves_finish(rest_state, [pa])
        wg[i] = g = dict(w_in=arrived[0], w_a=rest[0].reshape(d, d), w_glu=rest[1].reshape(ws, ws),
                         w_b=rest[2], w_o=rest[3].reshape(d, d))
        ya = _mm(pa, g["w_a"], name=f"mm_ya_{i}", out_dtype=BF16)
        y = _from_segments(y_seg)
        yg = _gelu_cast(y, f"gelu_{i}")
        gl = _mm(yg, g["w_glu"], name=f"mm_glu_{i}", out_dtype=BF16)
        pb = _glu_post(y, gl, proj, b_glu[i], zb_col, f"glu_post_{i}")
        w_b2d = g["w_b"].transpose(1, 0, 2).reshape(ws, d)
        yb = _mm(pb, w_b2d, name=f"mm_yb_{i}", out_dtype=BF16)
        mrg = _merge_fwd(proj, ya, yb, d, ga_col, gb_col, f"merge_fwd_{i}")
        deps = ()
        if i + 1 < depth:
            w_in_state, token = halves_forward(next_state, [mrg])
            rest_state, token = halves_start(shards_bf16(i + 1)[1:], f"ag_rest_{i + 1}", deps=(token,))
            next_state = None
            if i + 2 < depth:
                next_state, token = halves_start([shards_bf16(i + 2)[0]], f"ag_w_in_{i + 2}", deps=(token,))
            deps = (token,)
        xs.append(_mm(mrg, g["w_o"], name=f"mm_out_{i}", add=xi, deps=deps))
        saved.append(dict(h=h, proj=proj, pa=pa, ya=ya, yb=yb, y=y, yg=yg, gl=gl, pb=pb, mrg=mrg,
                          u_seg=u_seg, h_re=h_re, h_im=h_im, conv8=conv8, w_b2d=w_b2d))

    dx, g_final, loss_part = _final_loss(xs[-1], final_g, tgt, "final_loss")

    rs_pending = []
    small = {k: [None] * depth for k in ("norm_g", "a_re", "a_im", "log_dt", "b_re", "b_im", "c_re", "c_im",
                                         "d_skip", "b_glu", "conv_w")}

    my_chip = 2 * lax.axis_index("x") + lax.axis_index("y")

    def reduce_on_chip(pieces, tag):
        lands = [lax.empty((4,) + p.shape[1:], p.dtype) for p in pieces]
        sems, srcs, lands, token = _exchange_start(_plan_reduce_sibling, pieces, lands, f"rs_sibling_start_{tag}")
        return (sems, srcs, lands), token

    def reduce_across_chips(names_, state, layer, tag, after):
        sems, srcs, lands = state
        srcs, lands = _exchange_wait(_plan_reduce_sibling, sems, srcs, lands, after, f"rs_sibling_wait_{tag}")
        sums = [_chip_sums(p, l_, f"chip_sum_{k}_{layer}") for k, p, l_ in zip(names_, srcs, lands)]
        lands = [_landing(lax.dynamic_index_in_dim(s_, my_chip, 0, keepdims=False), 4, my_chip) for s_ in sums]
        sems, srcs, lands, token = _exchange_start(_plan_reduce_chips, sums, lands, f"rs_chips_start_{tag}")
        rs_pending.append((names_, layer, sems, srcs, lands, f"rs_chips_wait_{tag}"))
        return token

    for i in reversed(range(depth)):
        s, g = saved[i], wg[i]
        proj = s["proj"]
        dxo_b = dx.astype(BF16)
        dm = _mm(dxo_b, g["w_o"], name=f"mm_dm_{i}", nt=True, out_dtype=BF16)
        gw_o = _mm(s["mrg"], dxo_b, ta=True, name=f"mm_gw_o_{i}", out_dtype=BF16)
        dy2, dproj = _merge_bwd(proj, s["ya"], s["yb"], dm, d, ga_col, f"merge_bwd_{i}")
        dya, dyb = dy2[0], dy2[1]
        dpa = _mm(dya, g["w_a"], name=f"mm_dpa_{i}", nt=True, out_dtype=BF16)
        gw_a = _mm(s["pa"], dya, ta=True, name=f"mm_gw_a_{i}", out_dtype=BF16)
        dpb = _mm(dyb, s["w_b2d"], name=f"mm_dpb_{i}", nt=True, out_dtype=BF16)
        gw_b = _mm(s["pb"], dyb, ta=True, name=f"mm_gw_b_{i}", split_n=N_DEV, out_dtype=BF16)
        dproj, dw0, dw1, dw2 = _branch_a_bwd(proj, dpa, s["conv8"], dproj, d, f"branch_a_bwd_{i}")
        small["conv_w"][i] = jnp.concatenate([dw0, dw1, dw2], axis=0)
        dproj, dgl, t1, db_glu = _glu_bwd1(s["y"], s["gl"], proj, b_glu[i], dpb, dproj, zb_col, f"glu_bwd1_{i}")
        small["b_glu"][i] = db_glu.reshape(ws)
        dyg2 = _mm(dgl, g["w_glu"], name=f"mm_dyg_{i}", nt=True, out_dtype=BF16)
        gw_glu = _mm(s["yg"], dgl, ta=True, name=f"mm_gw_glu_{i}", out_dtype=BF16)
        small_names_ = ("w_out_a", "w_glu", "w_out_b", "w_o")
        state, token = reduce_on_chip(
            [gw_a.reshape(N_DEV, d // N_DEV, d), gw_glu.reshape(N_DEV, ws // N_DEV, ws), gw_b,
             gw_o.reshape(N_DEV, d // N_DEV, d)], f"small_{i}")
        dy = _glu_bwd2(s["y"], t1, dyg2, f"glu_bwd2_{i}", deps=(token,))
        dy_seg = _to_segments(dy)
        u_seg = s["u_seg"]
        du_seg, gc_re, gc_im, gbb_re, gbb_im, glam_re, glam_im, dskip = _s5_backward(
            dy_seg, u_seg, s["h_re"], s["h_im"], *s5[i]["up"], *s5[i]["down"],
            s5[i]["lam_re"], -s5[i]["lam_im"], d_skip[i].reshape(1, ws), f"s5_backward_{i}")
        token = reduce_across_chips(small_names_, state, i, f"small_{i}", [du_seg])
        dproj = _write_cols(dproj, _from_segments(du_seg), u_col, f"write_du_{i}")
        gw_in = _mm(s["h"], dproj, ta=True, name=f"mm_gw_in_{i}", split_n=N_DEV, tm=1024,
                    out_dtype=BF16, deps=(token,))
        state, token = reduce_on_chip([gw_in], f"w_in_{i}")
        small["d_skip"][i] = dskip.reshape(n_groups, S5_GROUP)
        small["c_re"][i] = _block_diag_extract(gc_re, n_groups, S5_GROUP, n_state)
        small["c_im"][i] = -_block_diag_extract(gc_im, n_groups, S5_GROUP, n_state)
        gbb_re = _block_diag_extract(gbb_re, n_groups, S5_GROUP, n_state).transpose(0, 2, 1)
        gbb_im = _block_diag_extract(gbb_im, n_groups, S5_GROUP, n_state).transpose(0, 2, 1)
        gar, gai, gdt, gbr, gbi = _s5_params_bwd(
            *s5[i]["dense_in"], _dense(glam_re.reshape(n_groups, n_state, 1), shape3),
            _dense(glam_im.reshape(n_groups, n_state, 1), shape3),
            gbb_re.reshape(-1, LANES), gbb_im.reshape(-1, LANES), n_groups, f"s5_params_bwd_{i}", deps=(token,))
        small["a_re"][i] = gar.reshape(shape3)[:, :, 0]
        small["a_im"][i] = gai.reshape(shape3)[:, :, 0]
        small["log_dt"][i] = gdt[:, 0]
        small["b_re"][i] = gbr.reshape(shape3)
        small["b_im"][i] = gbi.reshape(shape3)
        if i == 0:
            part = {k: jnp.stack(small[k]) for k in main_names}
            main_state, token = gather_start([_pack([part[k] for k in main_names]).astype(BF16)], "ag_small")
            token = reduce_across_chips(("w_in",), state, i, f"w_in_{i}", [token])
            main_state, token = gather_forward(main_state, [token])
            dh = _mm_win_bwd(dproj, g["w_in"], f"mm_dh_{i}", deps=(token,))
            main_slots = gather_finish(main_state, [dh])[0]
            deps = ()
        else:
            dh = _mm_win_bwd(dproj, g["w_in"], f"mm_dh_{i}", deps=(gar,))
            deps = (reduce_across_chips(("w_in",), state, i, f"w_in_{i}", [dh]),)
        dx, dng = _rmsnorm_bwd(xs[i], norm_g[i], dh, dx, f"rmsnorm_bwd_{i}", deps=deps)
        small["norm_g"][i] = dng.reshape(d)

    results = {}

    gain_grads = jnp.concatenate([jnp.stack(small["norm_g"]).reshape(-1), g_final.reshape(d)])
    gain_shapes = [(depth, d), (d,), (1,)]
    gains_state, token = gather_start([_pack([gain_grads, loss_part[0, :1]])], "ag_gains")

    sres = [_unpack(p[0], main_shapes) for p in _adamw(*main_wmv, main_slots, "adamw_small", deps=(token,))]
    for j, k in enumerate(main_names[:-1]):
        results[k] = [sres[q][j] for q in range(4)]
    gconv = lax.dynamic_slice_in_dim(sres[0][-1], me * dc, dc, axis=2)
    cres = _adamw(*conv_wmv, pad8(gconv), "adamw_conv_w")
    results["conv_w"] = [r_[0, :depth * 3].reshape(depth, 3, dc) for r_ in cres]

    after = [cres[0]]
    for names_, layer, sems, srcs, lands, wait_name in rs_pending:
        _, slots = _exchange_wait(_plan_reduce_chips, sems, srcs, lands, after, wait_name)
        for k, land in zip(names_, slots):
            w_, m_, v_ = big[k]
            results[k] = _adamw(w_, m_, v_, land, f"adamw_{k}_{layer}", layer=layer, prev=results.get(k))
            after = [results[k][0]]

    gains_state, token = gather_forward(gains_state, after)
    gpack = gather_finish(gains_state, [token])[0]
    gres = [_unpack(p[0], gain_shapes) for p in _adamw(*gains_wmv, gpack, "adamw_gains")]
    results["norm_g"] = [gres[q][0] for q in range(4)]
    results["final_g"] = [gres[q][1] for q in range(4)]
    loss = gres[0][2][0]

    names = ("norm_g", "w_in", "conv_w", "w_out_a", "a_re", "a_im", "log_dt", "b_re", "b_im", "c_re", "c_im",
             "d_skip", "w_glu", "b_glu", "w_out_b", "w_o", "final_g")
    outs = [loss, dx[None]]
    for q in range(4):
        outs += [results[k][q] for k in names]
    return tuple(outs)
```

```python
import functools
import math

import jax
import jax.numpy as jnp
from jax import lax
from jax.experimental import pallas as pl
from jax.experimental.pallas import tpu as pltpu

F32 = jnp.float32
BF16 = jnp.bfloat16
HIGHEST = lax.Precision.HIGHEST

N_DEV = 8
LANES = 128
SUBLANES = 8
VMEM_LIMIT_BYTES = 56 * 1024 * 1024

RMS_EPS = 1e-6
ADAM_LR = 0.001
ADAM_B1 = 0.9
ADAM_B2 = 0.999
ADAM_EPS = 1e-08
ADAM_WD = 0.01
ADAM_STEP = 10
GELU_C0 = math.sqrt(2.0 / math.pi)
GELU_C1 = 0.044715

ADAMW_BLOCK_ELEMS = 1 << 17
PACK_ROWS = 512

S5_GROUP = 16
S5_GB = LANES // S5_GROUP


def _params(*semantics):
    return pltpu.CompilerParams(dimension_semantics=semantics, vmem_limit_bytes=VMEM_LIMIT_BYTES)


ANY_SPEC = pl.BlockSpec(memory_space=pl.ANY)


def _pallas(body, args, deps=(), *, in_specs, **kwargs):
    deps = tuple(deps)
    if not deps:
        return pl.pallas_call(body, in_specs=in_specs, **kwargs)(*args)

    def body_after(*refs):
        body(*refs[len(deps):])

    return pl.pallas_call(body_after, in_specs=[ANY_SPEC] * len(deps) + list(in_specs), **kwargs)(*deps, *args)


def _tile(n, pref):
    t = min(n, pref)
    while n % t:
        assert t % 2 == 0, (n, pref)
        t //= 2
    return t


def _sigmoid(z):
    return 1.0 / (1.0 + jnp.exp(-z))


def _gelu(y):
    return 0.5 * y * (1.0 + jnp.tanh(GELU_C0 * (y + GELU_C1 * y * y * y)))


def _gelu_grad(y):
    t = jnp.tanh(GELU_C0 * (y + GELU_C1 * y * y * y))
    return 0.5 * (1.0 + t) + 0.5 * y * (1.0 - t * t) * GELU_C0 * (1.0 + 3.0 * GELU_C1 * y * y)


HBM_SPEC = pl.BlockSpec(memory_space=pltpu.HBM)
SEM_SPEC = pl.BlockSpec(memory_space=pltpu.SEMAPHORE)
DATAFLOW_EFFECT = pltpu.SideEffectType.DATAFLOW_SIDE_EFFECTING
OTHER_CHIPS = (2, 4, 6)


def _flip(pos, mask):
    x, y, c = pos
    return x ^ ((mask >> 2) & 1), y ^ ((mask >> 1) & 1), c ^ (mask & 1)


def _dev(pos):
    return 4 * pos[0] + 2 * pos[1] + pos[2]


def _chip(pos):
    return 2 * pos[0] + pos[1]


def _plan_gather_chips(me):
    return [(_flip(me, k), None, _dev(me), _dev(_flip(me, k))) for k in (1,) + OTHER_CHIPS]


def _plan_gather_forward(me):
    sib = _flip(me, 1)
    return [(sib, _dev(_flip(me, k)), _dev(_flip(me, k)), _dev(_flip(sib, k))) for k in OTHER_CHIPS]


def _plan_reduce_sibling(me):
    sib = _flip(me, 1)
    return [(sib, 2 * q + sib[2], q, q) for q in range(4)]


def _plan_reduce_chips(me):
    return [(_flip(me, k), _chip(_flip(me, k)), _chip(me), _chip(_flip(me, k))) for k in OTHER_CHIPS]


X_CHIP, Y_CHIP, XY_CHIP = 4, 2, 6


def _plan_halves_first(me):
    sib, xn, yn = _flip(me, 1), _flip(me, X_CHIP), _flip(me, Y_CHIP)
    return [(sib, 0, 2 * _dev(me), 2 * _dev(sib)), (sib, 1, 2 * _dev(me) + 1, 2 * _dev(sib) + 1),
            (xn, 0, 2 * _dev(me), 2 * _dev(xn)), (yn, 1, 2 * _dev(me) + 1, 2 * _dev(yn) + 1)]


def _plan_halves_second(me):
    xn, yn, dg = _flip(me, X_CHIP), _flip(me, Y_CHIP), _flip(me, XY_CHIP)
    return [(yn, 2 * _dev(me), 2 * _dev(me), 2 * _dev(yn)), (yn, 2 * _dev(xn), 2 * _dev(xn), 2 * _dev(dg)),
            (xn, 2 * _dev(me) + 1, 2 * _dev(me) + 1, 2 * _dev(xn) + 1),
            (xn, 2 * _dev(yn) + 1, 2 * _dev(yn) + 1, 2 * _dev(dg) + 1)]


def _plan_halves_forward(me):
    sib = _flip(me, 1)
    return [(sib, 2 * _dev(_flip(me, k)) + h, 2 * _dev(_flip(me, k)) + h, 2 * _dev(_flip(sib, k)) + h)
            for k in OTHER_CHIPS for h in (0, 1)]


PLAN_COPIES = {_plan_gather_chips: 4, _plan_gather_forward: 3, _plan_reduce_sibling: 4, _plan_reduce_chips: 3,
               _plan_halves_first: 4, _plan_halves_second: 4, _plan_halves_forward: 6}


def _exchange_copies(plan, src_refs, land_refs, send_sems, recv_sems, incoming=True):
    me = (lax.axis_index("x"), lax.axis_index("y"), lax.axis_index("c"))
    pairs = []
    for b, (src_ref, land_ref) in enumerate(zip(src_refs, land_refs)):
        for j, (peer, src_slot, there, here) in enumerate(plan(me)):
            sem = b * PLAN_COPIES[plan] + j
            src = src_ref if src_slot is None else src_ref.at[src_slot]
            out = pltpu.make_async_remote_copy(
                src_ref=src, dst_ref=land_ref.at[there], send_sem=send_sems.at[sem], recv_sem=recv_sems.at[sem],
                device_id=peer, device_id_type=pl.DeviceIdType.MESH)
            inc = pltpu.make_async_remote_copy(
                src_ref=src, dst_ref=land_ref.at[here], send_sem=send_sems.at[sem], recv_sem=recv_sems.at[sem],
                device_id=peer, device_id_type=pl.DeviceIdType.MESH) if incoming else None
            pairs.append((out, inc))
    return pairs


def _exchange_start(plan, srcs, lands, name, deps=()):
    srcs = [] if srcs is None else list(srcs)
    ns, n, nd = len(srcs), len(lands), len(deps)

    def body(*refs):
        land_refs = refs[ns:ns + n]
        sems_at = ns + n + nd
        pairs = _exchange_copies(plan, refs[:ns] if ns else land_refs, land_refs, refs[sems_at], refs[sems_at + 1],
                                 incoming=False)
        for out, _ in pairs:
            out.start()
        token = refs[-1]
        token[...] = jnp.zeros_like(token)

    sems = pltpu.SemaphoreType.DMA((PLAN_COPIES[plan] * n,))
    bufs = srcs + list(lands)
    outs = pl.pallas_call(
        body, name=name,
        out_shape=(sems, sems, *[pltpu.HBM(a.shape, a.dtype) for a in bufs],
                   jax.ShapeDtypeStruct((SUBLANES, LANES), F32)),
        in_specs=[HBM_SPEC] * (ns + n) + [ANY_SPEC] * nd,
        out_specs=(SEM_SPEC, SEM_SPEC, *[HBM_SPEC] * (ns + n), pl.BlockSpec(memory_space=pltpu.VMEM)),
        input_output_aliases={i: 2 + i for i in range(ns + n)},
        compiler_params=pltpu.CompilerParams(has_side_effects=DATAFLOW_EFFECT),
    )(*[pltpu.with_memory_space_constraint(a, pltpu.HBM) for a in bufs], *deps)
    return (outs[0], outs[1]), (outs[2:2 + ns] if ns else None), outs[2 + ns:2 + ns + n], outs[-1]


def _exchange_wait(plan, sems, srcs, lands, after, name):
    srcs = [] if srcs is None else list(srcs)
    ns, n = len(srcs), len(lands)

    def body(*refs):
        land_refs = refs[ns:ns + n]
        pairs = _exchange_copies(plan, refs[:ns] if ns else land_refs, land_refs, refs[ns + n], refs[ns + n + 1])
        for out, inc in pairs:
            out.wait_send()
            inc.wait_recv()

    bufs = srcs + list(lands)
    outs = pl.pallas_call(
        body, name=name,
        out_shape=[pltpu.HBM(a.shape, a.dtype) for a in bufs],
        in_specs=[HBM_SPEC] * (ns + n) + [SEM_SPEC, SEM_SPEC] + [ANY_SPEC] * len(after),
        out_specs=[HBM_SPEC] * (ns + n),
        input_output_aliases={i: i for i in range(ns + n)},
        compiler_params=pltpu.CompilerParams(has_side_effects=DATAFLOW_EFFECT),
    )(*bufs, sems[0], sems[1], *after)
    return outs[:ns], outs[ns:]


def _landing(own, slots, slot):
    land = lax.empty((slots,) + own.shape, own.dtype)
    return lax.dynamic_update_slice(land, own[None], (slot,) + (0,) * own.ndim)


def _chip_sums(pieces, land, name):
    _, r, c_ = land.shape
    tr = _tile(r, max(2 * SUBLANES, 1 << int(math.log2(4 * ADAMW_BLOCK_ELEMS // c_))))

    def body(core_ref, p_ref, l_ref, o_ref):
        o_ref[...] = (p_ref[...].astype(F32) + l_ref[...].astype(F32)).astype(o_ref.dtype)

    spec = pl.BlockSpec((None, tr, c_), lambda q, i, core: (q, i, 0))
    return pl.pallas_call(
        body, name=name,
        grid_spec=pltpu.PrefetchScalarGridSpec(
            num_scalar_prefetch=1, grid=(4, r // tr),
            in_specs=[pl.BlockSpec((None, tr, c_), lambda q, i, core: (2 * q + core[0], i, 0)), spec],
            out_specs=spec),
        out_shape=jax.ShapeDtypeStruct(land.shape, land.dtype),
        compiler_params=_params("parallel", "parallel"),
    )(lax.axis_index("c").reshape(1), pieces, land)


def _mm(a, b, *, name, nt=False, ta=False, out_dtype=F32, add=None, split_n=None, tm=512, tn=1024, deps=()):
    k, m = a.shape if ta else a.shape[::-1]
    n = b.shape[0] if nt else b.shape[1]
    tm = _tile(m, tm)
    tn = n // split_n if split_n else _tile(n, tn)
    dims = (((0 if ta else 1,), (1 if nt else 0,)), ((), ()))

    def body(*refs):
        a_ref, b_ref = refs[0], refs[1]
        o_ref = refs[-1]
        acc = lax.dot_general(a_ref[...], b_ref[...], dims, preferred_element_type=F32)
        if add is not None:
            acc = acc + refs[2][...]
        o_ref[...] = acc.astype(o_ref.dtype)

    in_specs = [pl.BlockSpec((k, tm), lambda i, j: (0, i)) if ta else pl.BlockSpec((tm, k), lambda i, j: (i, 0)),
                pl.BlockSpec((tn, k), lambda i, j: (j, 0)) if nt
                else pl.BlockSpec((k, tn), lambda i, j: (0, j))]
    args = [a, b]
    if add is not None:
        in_specs.append(pl.BlockSpec((tm, tn), lambda i, j: (i, j)))
        args.append(add)
    if split_n:
        out_shape = jax.ShapeDtypeStruct((split_n, m, tn), out_dtype)
        out_spec = pl.BlockSpec((None, tm, tn), lambda i, j: (j, i, 0))
    else:
        out_shape = jax.ShapeDtypeStruct((m, n), out_dtype)
        out_spec = pl.BlockSpec((tm, tn), lambda i, j: (i, j))
    return _pallas(
        body, args, deps, name=name, grid=(m // tm, n // tn), in_specs=in_specs, out_specs=out_spec,
        out_shape=out_shape, compiler_params=_params("parallel", "parallel"))


def _mm_win_fwd(h, w_g, name, deps=()):
    m, k = h.shape
    nj = w_g.shape[2]
    tm = _tile(m, 512)

    def body(a_ref, b_ref, o_ref):
        o_ref[...] = jnp.dot(a_ref[...], b_ref[...], preferred_element_type=F32).astype(o_ref.dtype)

    return _pallas(
        body, [h, w_g], deps, name=name, grid=(N_DEV, m // tm),
        in_specs=[pl.BlockSpec((tm, k), lambda j, i: (i, 0)),
                  pl.BlockSpec((None, k, nj), lambda j, i: (j, 0, 0))],
        out_specs=pl.BlockSpec((tm, nj), lambda j, i: (i, j)),
        out_shape=jax.ShapeDtypeStruct((m, N_DEV * nj), BF16),
        compiler_params=_params("parallel", "parallel"))


def _mm_win_bwd(dproj, w_g, name, deps=()):
    m = dproj.shape[0]
    d, nj = w_g.shape[1], w_g.shape[2]
    tm = _tile(m, 512)
    tn = _tile(d, 1024)

    per_step = 2

    def body(a_ref, b_ref, o_ref, acc_ref):
        j = pl.program_id(2)

        @pl.when(j == 0)
        def _():
            acc_ref[...] = jnp.zeros_like(acc_ref)

        part = None
        for k in range(per_step):
            term = lax.dot_general(a_ref[:, k * nj:(k + 1) * nj], b_ref[k], (((1,), (1,)), ((), ())),
                                   preferred_element_type=F32)
            part = term if part is None else part + term
        acc_ref[...] += part

        @pl.when(j == N_DEV // per_step - 1)
        def _():
            o_ref[...] = acc_ref[...]

    return _pallas(
        body, [dproj, w_g], deps, name=name, grid=(m // tm, d // tn, N_DEV // per_step),
        in_specs=[pl.BlockSpec((tm, per_step * nj), lambda i, n, j: (i, j)),
                  pl.BlockSpec((per_step, tn, nj), lambda i, n, j: (j, n, 0))],
        out_specs=pl.BlockSpec((tm, tn), lambda i, n, j: (i, n)),
        out_shape=jax.ShapeDtypeStruct((m, d), F32),
        scratch_shapes=[pltpu.VMEM((tm, tn), F32)],
        compiler_params=_params("parallel", "parallel", "arbitrary"))


def _row_spec(tr, w, col):
    return pl.BlockSpec((tr, w), lambda i: (i, col))


def _full_spec(shape):
    return pl.BlockSpec(shape, lambda i: (0,) * len(shape))


def _rmsnorm_fwd(x, g, name, deps=()):
    l, d = x.shape
    tr = _tile(l, 256)

    def body(x_ref, g_ref, o_ref):
        xv = x_ref[...]
        rstd = lax.rsqrt(jnp.mean(xv * xv, axis=-1, keepdims=True) + RMS_EPS)
        o_ref[...] = (xv * rstd * g_ref[...]).astype(o_ref.dtype)

    return _pallas(
        body, [x, g.reshape(1, d)], deps, name=name, grid=(l // tr,),
        in_specs=[_row_spec(tr, d, 0), _full_spec((1, d))],
        out_specs=_row_spec(tr, d, 0),
        out_shape=jax.ShapeDtypeStruct((l, d), BF16),
        compiler_params=_params("parallel"))


def _rmsnorm_bwd(x, g, dh, dxo, name, deps=()):
    l, d = x.shape
    tr = _tile(l, 256)

    def body(x_ref, g_ref, dh_ref, dxo_ref, dx_ref, dg_ref):
        xv = x_ref[...]
        rstd = lax.rsqrt(jnp.mean(xv * xv, axis=-1, keepdims=True) + RMS_EPS)
        dhv = dh_ref[...]
        gdy = dhv * g_ref[...]
        dot = jnp.mean(gdy * xv, axis=-1, keepdims=True)
        dx_ref[...] = dxo_ref[...] + rstd * gdy - xv * (rstd * rstd * rstd * dot)

        @pl.when(pl.program_id(0) == 0)
        def _():
            dg_ref[...] = jnp.zeros_like(dg_ref)

        dg_ref[...] += jnp.sum(dhv * xv * rstd, axis=0, keepdims=True)

    return _pallas(
        body, [x, g.reshape(1, d), dh, dxo], deps, name=name, grid=(l // tr,),
        in_specs=[_row_spec(tr, d, 0), _full_spec((1, d)), _row_spec(tr, d, 0), _row_spec(tr, d, 0)],
        out_specs=[_row_spec(tr, d, 0), _full_spec((1, d))],
        out_shape=[jax.ShapeDtypeStruct((l, d), F32), jax.ShapeDtypeStruct((1, d), F32)],
        compiler_params=_params("arbitrary"))


def _final_loss(x, g, tgt, name):
    l, d = x.shape
    tr = _tile(l, 256)

    def body(x_ref, g_ref, t_ref, dx_ref, dg_ref, loss_ref):
        xv = x_ref[...]
        gv = g_ref[...]
        rstd = lax.rsqrt(jnp.mean(xv * xv, axis=-1, keepdims=True) + RMS_EPS)
        xn = xv * rstd
        err = xn * gv - t_ref[...]
        dy = err * (1.0 / d)
        gdy = dy * gv
        dot = jnp.mean(gdy * xv, axis=-1, keepdims=True)
        dx_ref[...] = rstd * gdy - xv * (rstd * rstd * rstd * dot)

        @pl.when(pl.program_id(0) == 0)
        def _():
            dg_ref[...] = jnp.zeros_like(dg_ref)
            loss_ref[...] = jnp.zeros_like(loss_ref)

        dg_ref[...] += jnp.sum(dy * xn, axis=0, keepdims=True)
        loss_ref[...] += (0.5 / d) * jnp.sum(err * err)

    return pl.pallas_call(
        body, name=name, grid=(l // tr,),
        in_specs=[_row_spec(tr, d, 0), _full_spec((1, d)), _row_spec(tr, d, 0)],
        out_specs=[_row_spec(tr, d, 0), _full_spec((1, d)), _full_spec((SUBLANES, LANES))],
        out_shape=[jax.ShapeDtypeStruct((l, d), F32), jax.ShapeDtypeStruct((1, d), F32),
                   jax.ShapeDtypeStruct((SUBLANES, LANES), F32)],
        compiler_params=_params("arbitrary"),
    )(x, g.reshape(1, d), tgt)


HALO = 2 * SUBLANES


def _halo_spec(tr, w, col, nblk, before):
    step = tr // HALO
    if before:
        return pl.BlockSpec((HALO, w), lambda i: (jnp.maximum(i * step - 1, 0), col))
    return pl.BlockSpec((HALO, w), lambda i: (jnp.minimum((i + 1) * step, nblk - 1), col))


def _shift_down(cur, before, k):
    ext = jnp.concatenate([before, cur], axis=0)
    return pltpu.roll(ext, k, axis=0)[HALO:, :]


def _shift_up(cur, after, k):
    tr = cur.shape[0]
    ext = jnp.concatenate([cur, after], axis=0)
    return pltpu.roll(ext, tr + HALO - k, axis=0)[:tr, :]


def _f32(ref):
    return ref[...].astype(F32)


def _branch_a_fwd(proj, conv_w, d, name, deps=()):
    l = proj.shape[0]
    tr = _tile(l, 256)
    nblk8 = l // HALO

    def body(v_ref, bg_ref, cg_ref, za_ref, vh_ref, cgh_ref, w_ref, o_ref):
        first = pl.program_id(0) == 0
        cv = _f32(cg_ref) * _f32(v_ref)
        cvh = jnp.where(first, 0.0, _f32(cgh_ref) * _f32(vh_ref))
        w0, w1, w2 = w_ref[0:1, :], w_ref[1:2, :], w_ref[2:3, :]
        q = w2 * cv + w1 * _shift_down(cv, cvh, 1) + w0 * _shift_down(cv, cvh, 2)
        za = _f32(za_ref)
        o_ref[...] = (_f32(bg_ref) * q * (za * _sigmoid(za))).astype(o_ref.dtype)

    return _pallas(
        body, [proj, proj, proj, proj, proj, proj, conv_w], deps, name=name, grid=(l // tr,),
        in_specs=[_row_spec(tr, d, 0), _row_spec(tr, d, 1), _row_spec(tr, d, 2), _row_spec(tr, d, 3),
                  _halo_spec(tr, d, 0, nblk8, True), _halo_spec(tr, d, 2, nblk8, True),
                  _full_spec((SUBLANES, d))],
        out_specs=_row_spec(tr, d, 0),
        out_shape=jax.ShapeDtypeStruct((l, d), BF16),
        compiler_params=_params("parallel"))


def _branch_a_bwd(proj, dpa, conv_w, dproj, d, name):
    l = proj.shape[0]
    tr = _tile(l, 128)
    nblk8 = l // HALO
    ntiles = l // tr

    def body(v_ref, bg_ref, cg_ref, za_ref, dpa_ref, vh_ref, cgh_ref, bgn_ref, zan_ref, dpan_ref,
             w_ref, _, o_ref, dw0_ref, dw1_ref, dw2_ref):
        dv_ref, dbg_ref, dcg_ref, dza_ref = [o_ref.at[:, pl.ds(k * d, d)] for k in range(4)]
        i = pl.program_id(0)
        v, bg, cg, za, dpa_v = _f32(v_ref), _f32(bg_ref), _f32(cg_ref), _f32(za_ref), _f32(dpa_ref)
        w0, w1, w2 = w_ref[0:1, :], w_ref[1:2, :], w_ref[2:3, :]
        cv = cg * v
        cvh = jnp.where(i == 0, 0.0, _f32(cgh_ref) * _f32(vh_ref))
        cv1 = _shift_down(cv, cvh, 1)
        cv2 = _shift_down(cv, cvh, 2)
        q = w2 * cv + w1 * cv1 + w0 * cv2
        sg = _sigmoid(za)
        s = za * sg
        dbg_ref[...] = (dpa_v * q * s).astype(dbg_ref.dtype)
        dza_ref[...] = (dpa_v * bg * q * (sg * (1.0 + za * (1.0 - sg)))).astype(dza_ref.dtype)
        dq = dpa_v * bg * s
        zan = _f32(zan_ref)
        dqn = jnp.where(i == ntiles - 1, 0.0, _f32(dpan_ref) * _f32(bgn_ref) * (zan * _sigmoid(zan)))
        dcv = w2 * dq + w1 * _shift_up(dq, dqn, 1) + w0 * _shift_up(dq, dqn, 2)
        dcg_ref[...] = (dcv * v).astype(dcg_ref.dtype)
        dv_ref[...] = (dcv * cg).astype(dv_ref.dtype)

        @pl.when(i == 0)
        def _():
            dw0_ref[...] = jnp.zeros_like(dw0_ref)
            dw1_ref[...] = jnp.zeros_like(dw1_ref)
            dw2_ref[...] = jnp.zeros_like(dw2_ref)

        dw0_ref[...] += jnp.sum(dq * cv2, axis=0, keepdims=True)
        dw1_ref[...] += jnp.sum(dq * cv1, axis=0, keepdims=True)
        dw2_ref[...] += jnp.sum(dq * cv, axis=0, keepdims=True)

    wsum = jax.ShapeDtypeStruct((1, d), F32)
    return pl.pallas_call(
        body, name=name, grid=(ntiles,),
        in_specs=[_row_spec(tr, d, 0), _row_spec(tr, d, 1), _row_spec(tr, d, 2), _row_spec(tr, d, 3),
                  _row_spec(tr, d, 0),
                  _halo_spec(tr, d, 0, nblk8, True), _halo_spec(tr, d, 2, nblk8, True),
                  _halo_spec(tr, d, 1, nblk8, False), _halo_spec(tr, d, 3, nblk8, False),
                  _halo_spec(tr, d, 0, nblk8, False),
                  _full_spec((SUBLANES, d)), ANY_SPEC],
        out_specs=[_row_spec(tr, 4 * d, 0)] + [_full_spec((1, d))] * 3,
        out_shape=[jax.ShapeDtypeStruct(dproj.shape, dproj.dtype)] + [wsum] * 3,
        input_output_aliases={11: 0},
        compiler_params=_params("arbitrary"),
    )(proj, proj, proj, proj, dpa, proj, proj, proj, proj, dpa, conv_w, dproj)


def _gelu_cast(y, name):
    l, w = y.shape
    tr = _tile(l, 512)

    def body(y_ref, o_ref):
        o_ref[...] = _gelu(y_ref[...]).astype(o_ref.dtype)

    return pl.pallas_call(
        body, name=name, grid=(l // tr,), in_specs=[_row_spec(tr, w, 0)],
        out_specs=_row_spec(tr, w, 0), out_shape=jax.ShapeDtypeStruct((l, w), BF16),
        compiler_params=_params("parallel"),
    )(y)


def _glu_post(y, gl, proj, b_glu, zb_col, name):
    l, w = y.shape
    tr = _tile(l, 512)

    def body(y_ref, gl_ref, zb_ref, b_ref, o_ref):
        zb = _f32(zb_ref)
        o_ref[...] = (_gelu(y_ref[...]) * _sigmoid(_f32(gl_ref) + b_ref[...])
                      * (zb * _sigmoid(zb))).astype(o_ref.dtype)

    return pl.pallas_call(
        body, name=name, grid=(l // tr,),
        in_specs=[_row_spec(tr, w, 0), _row_spec(tr, w, 0), _row_spec(tr, w, zb_col), _full_spec((1, w))],
        out_specs=_row_spec(tr, w, 0), out_shape=jax.ShapeDtypeStruct((l, w), BF16),
        compiler_params=_params("parallel"),
    )(y, gl, proj, b_glu.reshape(1, w))


def _glu_bwd1(y, gl, proj, b_glu, dpb, dproj, zb_col, name):
    l, w = y.shape
    tr = _tile(l, 512)

    def body(y_ref, gl_ref, zb_ref, b_ref, dpb_ref, _, dzb_ref, dgl_ref, t_ref, db_ref):
        zb = _f32(zb_ref)
        dpb_v = _f32(dpb_ref)
        yg = _gelu(y_ref[...])
        sgl = _sigmoid(_f32(gl_ref) + b_ref[...])
        szb = _sigmoid(zb)
        dzb_ref[...] = (dpb_v * yg * sgl * (szb * (1.0 + zb * (1.0 - szb)))).astype(dzb_ref.dtype)
        e = dpb_v * (zb * szb)
        dgl = e * yg * sgl * (1.0 - sgl)
        dgl_ref[...] = dgl.astype(dgl_ref.dtype)
        t_ref[...] = e * sgl

        @pl.when(pl.program_id(0) == 0)
        def _():
            db_ref[...] = jnp.zeros_like(db_ref)

        db_ref[...] += jnp.sum(dgl, axis=0, keepdims=True)

    return pl.pallas_call(
        body, name=name, grid=(l // tr,),
        in_specs=[_row_spec(tr, w, 0), _row_spec(tr, w, 0), _row_spec(tr, w, zb_col), _full_spec((1, w)),
                  _row_spec(tr, w, 0), ANY_SPEC],
        out_specs=[_row_spec(tr, w, zb_col)] + [_row_spec(tr, w, 0)] * 2 + [_full_spec((1, w))],
        out_shape=[jax.ShapeDtypeStruct(dproj.shape, dproj.dtype), jax.ShapeDtypeStruct((l, w), BF16),
                   jax.ShapeDtypeStruct((l, w), F32), jax.ShapeDtypeStruct((1, w), F32)],
        input_output_aliases={5: 0},
        compiler_params=_params("arbitrary"),
    )(y, gl, proj, b_glu.reshape(1, w), dpb, dproj)


def _write_cols(dproj, cols, col, name):
    l, w = cols.shape
    tr = _tile(l, 512)

    def body(c_ref, _, o_ref):
        o_ref[...] = c_ref[...].astype(o_ref.dtype)

    return pl.pallas_call(
        body, name=name, grid=(l // tr,), in_specs=[_row_spec(tr, w, 0), ANY_SPEC],
        out_specs=_row_spec(tr, w, col), out_shape=jax.ShapeDtypeStruct(dproj.shape, dproj.dtype),
        input_output_aliases={1: 0}, compiler_params=_params("parallel"),
    )(cols, dproj)


def _glu_bwd2(y, t1, dyg2, name, deps=()):
    l, w = y.shape
    tr = _tile(l, 512)

    def body(y_ref, t_ref, d_ref, o_ref):
        o_ref[...] = (t_ref[...] + _f32(d_ref)) * _gelu_grad(y_ref[...])

    return _pallas(
        body, [y, t1, dyg2], deps, name=name, grid=(l // tr,), in_specs=[_row_spec(tr, w, 0)] * 3,
        out_specs=_row_spec(tr, w, 0), out_shape=jax.ShapeDtypeStruct((l, w), F32),
        compiler_params=_params("parallel"))


def _merge_fwd(proj, ya, yb, d, ga_col, gb_col, name):
    l = proj.shape[0]
    tr = _tile(l, 256)

    def body(ga_ref, gb_ref, ya_ref, yb_ref, o_ref):
        o_ref[...] = (_sigmoid(_f32(ga_ref)) * _f32(ya_ref)
                      + _sigmoid(_f32(gb_ref)) * _f32(yb_ref)).astype(o_ref.dtype)

    return pl.pallas_call(
        body, name=name, grid=(l // tr,),
        in_specs=[_row_spec(tr, d, ga_col), _row_spec(tr, d, gb_col), _row_spec(tr, d, 0), _row_spec(tr, d, 0)],
        out_specs=_row_spec(tr, d, 0), out_shape=jax.ShapeDtypeStruct((l, d), BF16),
        compiler_params=_params("parallel"),
    )(proj, proj, ya, yb)


def _merge_bwd(proj, ya, yb, dm, d, ga_col, name):
    l, n = proj.shape
    tr = _tile(l, 256)

    def body(g_ref, ya_ref, yb_ref, dm_ref, dy_ref, dg_ref):
        dmv = _f32(dm_ref)
        sg = _sigmoid(_f32(g_ref))
        yv = jnp.where(pl.program_id(1) == 0, _f32(ya_ref), _f32(yb_ref))
        dy_ref[...] = (dmv * sg).astype(dy_ref.dtype)
        dg_ref[...] = (dmv * yv * sg * (1.0 - sg)).astype(dg_ref.dtype)

    row = pl.BlockSpec((tr, d), lambda i, j: (i, 0))
    return pl.pallas_call(
        body, name=name, grid=(l // tr, 2),
        in_specs=[pl.BlockSpec((tr, d), lambda i, j: (i, ga_col + j)), row, row, row],
        out_specs=[pl.BlockSpec((None, tr, d), lambda i, j: (j, i, 0)),
                   pl.BlockSpec((tr, d), lambda i, j: (i, ga_col + j))],
        out_shape=[jax.ShapeDtypeStruct((2, l, d), BF16), jax.ShapeDtypeStruct((l, n), BF16)],
        compiler_params=_params("parallel", "arbitrary"),
    )(proj, ya, yb, dm)


def _to_segments(a):
    l, w = a.shape
    return a.reshape(SUBLANES, l // SUBLANES, w).transpose(1, 0, 2).reshape(l, w)


def _from_segments(a):
    l, w = a.shape
    return a.reshape(l // SUBLANES, SUBLANES, w).transpose(1, 0, 2).reshape(l, w)


def _dense(z, shape):
    return jnp.broadcast_to(z, shape).reshape(-1, LANES)


def _s5_disc(are, aim, ldt):
    dt = jnp.exp(ldt)
    er = jnp.exp(are * dt)
    lbr = er * jnp.cos(aim * dt)
    lbi = er * jnp.sin(aim * dt)
    inv = 1.0 / (are * are + aim * aim)
    fr = ((lbr - 1.0) * are + lbi * aim) * inv
    fi = (lbi * are - (lbr - 1.0) * aim) * inv
    return dt, lbr, lbi, inv, fr, fi


def _s5_params(are, aim, ldt, bre, bim, input_matrix, name, deps=()):
    shape = are.shape

    def body(are_ref, aim_ref, ldt_ref, bre_ref, bim_ref, re_ref, im_ref):
        _, lbr, lbi, _, fr, fi = _s5_disc(are_ref[...], aim_ref[...], ldt_ref[...])
        if input_matrix:
            re_ref[...] = fr * bre_ref[...] - fi * bim_ref[...]
            im_ref[...] = fr * bim_ref[...] + fi * bre_ref[...]
        else:
            re_ref[...] = lbr
            im_ref[...] = lbi

    out = jax.ShapeDtypeStruct(shape, F32)
    return _pallas(body, [are, aim, ldt, bre, bim], deps, name=name,
                   in_specs=[pl.BlockSpec(memory_space=pltpu.VMEM)] * 5, out_shape=[out] * 2,
                   compiler_params=pltpu.CompilerParams(vmem_limit_bytes=VMEM_LIMIT_BYTES))


def _s5_params_bwd(are, aim, ldt, bre, bim, glbr, glbi, gbbr, gbbi, n_groups, name, deps=()):
    shape = are.shape
    rows_per_group = shape[0] // n_groups

    def body(are_ref, aim_ref, ldt_ref, bre_ref, bim_ref, glbr_ref, glbi_ref, gbbr_ref, gbbi_ref,
             gar_ref, gai_ref, gdt_ref, gbr_ref, gbi_ref):
        are_v, aim_v = are_ref[...], aim_ref[...]
        bre_v, bim_v = bre_ref[...], bim_ref[...]
        gbbr_v, gbbi_v = gbbr_ref[...], gbbi_ref[...]
        dt, lbr, lbi, inv, fr, fi = _s5_disc(are_v, aim_v, ldt_ref[...])
        gbr_ref[...] = fr * gbbr_v + fi * gbbi_v
        gbi_ref[...] = fr * gbbi_v - fi * gbbr_v
        lane_group = lax.broadcasted_iota(jnp.int32, (LANES, LANES), 0) // S5_GROUP
        same_group = (lane_group == lax.broadcasted_iota(jnp.int32, (LANES, LANES), 1) // S5_GROUP)
        ones = same_group.astype(F32)
        gfr = jnp.dot(bre_v * gbbr_v + bim_v * gbbi_v, ones, precision=HIGHEST, preferred_element_type=F32)
        gfi = jnp.dot(bre_v * gbbi_v - bim_v * gbbr_v, ones, precision=HIGHEST, preferred_element_type=F32)
        glr = glbr_ref[...] + (are_v * gfr - aim_v * gfi) * inv
        gli = glbi_ref[...] + (are_v * gfi + aim_v * gfr) * inv
        qr = (fr * are_v + fi * aim_v) * inv
        qi = (fi * are_v - fr * aim_v) * inv
        gzr = lbr * glr + lbi * gli
        gzi = lbr * gli - lbi * glr
        gar_ref[...] = dt * gzr - (qr * gfr + qi * gfi)
        gai_ref[...] = dt * gzi - (qr * gfi - qi * gfr)
        e = dt * (are_v * gzr + aim_v * gzi)
        per_group = jnp.sum(e.reshape(n_groups, rows_per_group, LANES), axis=1)
        total = jnp.sum(per_group, axis=1, keepdims=True) * (1.0 / S5_GROUP)
        gdt_ref[...] = jnp.broadcast_to(total, gdt_ref.shape)

    out = jax.ShapeDtypeStruct(shape, F32)
    return _pallas(
        body, [are, aim, ldt, bre, bim, glbr, glbi, gbbr, gbbi], deps, name=name,
        in_specs=[pl.BlockSpec(memory_space=pltpu.VMEM)] * 9,
        out_shape=[out, out, jax.ShapeDtypeStruct((n_groups, LANES), F32), out, out],
        compiler_params=pltpu.CompilerParams(vmem_limit_bytes=VMEM_LIMIT_BYTES))


def _cmul(ar, ai, br, bi):
    return ar * br - ai * bi, ar * bi + ai * br


def _scan_in_place(hr_ref, hi_ref, lr, li, reverse):
    l, wb = hr_ref.shape
    nt = l // SUBLANES
    shift = SUBLANES - 1 if reverse else 1
    unroll = 8 if nt % 8 == 0 else 1

    def rows(k):
        t = (nt - 1 - k) if reverse else k
        return pl.ds(pl.multiple_of(t * SUBLANES, SUBLANES), SUBLANES)

    zero = jnp.zeros((SUBLANES, wb), F32)

    def local_step(k, carry):
        hr, hi = carry
        r = rows(k)
        tr_, ti_ = _cmul(lr, li, hr, hi)
        hr, hi = tr_ + hr_ref[r, :], ti_ + hi_ref[r, :]
        hr_ref[r, :] = hr
        hi_ref[r, :] = hi
        return hr, hi

    er, ei = lax.fori_loop(0, nt, local_step, (zero, zero), unroll=unroll)

    lnr = lni = None
    br, bi, n = lr, li, nt
    while n:
        if n & 1:
            lnr, lni = (br, bi) if lnr is None else _cmul(lnr, lni, br, bi)
        n >>= 1
        if n:
            br, bi = _cmul(br, bi, br, bi)

    row = lax.broadcasted_iota(jnp.int32, (SUBLANES, wb), 0)
    tr_, ti_ = er, ei
    for j in range(1, SUBLANES):
        pr_, pi_ = _cmul(lnr, lni, pltpu.roll(tr_, shift, axis=0), pltpu.roll(ti_, shift, axis=0))
        at = row == ((SUBLANES - 1 - j) if reverse else j)
        tr_ = jnp.where(at, er + pr_, tr_)
        ti_ = jnp.where(at, ei + pi_, ti_)
    edge = row == ((SUBLANES - 1) if reverse else 0)
    cr = jnp.where(edge, 0.0, pltpu.roll(tr_, shift, axis=0))
    ci = jnp.where(edge, 0.0, pltpu.roll(ti_, shift, axis=0))

    def fix_step(k, carry):
        zr, zi = _cmul(lr, li, *carry)
        r = rows(k)
        hr_ref[r, :] = hr_ref[r, :] + zr
        hi_ref[r, :] = hi_ref[r, :] + zi
        return zr, zi

    lax.fori_loop(0, nt, fix_step, (cr, ci), unroll=unroll)


def _dot(a, b):
    return jnp.dot(a.astype(BF16), b.astype(BF16), preferred_element_type=F32)


def _s5_forward(u_seg, mb_re, mb_im, mc_re, mc_im, lam_re, lam_im, dvec, name, deps=()):
    l = u_seg.shape[0]
    nb, kin, kst = mb_re.shape

    def body(u_ref, mbr_ref, mbi_ref, mcr_ref, mci_ref, lr_ref, li_ref, d_ref, hr_ref, hi_ref, y_ref):
        u = u_ref[...]
        hr_ref[...] = _dot(u, mbr_ref[...])
        hi_ref[...] = _dot(u, mbi_ref[...])
        _scan_in_place(hr_ref, hi_ref, jnp.broadcast_to(lr_ref[...], (SUBLANES, kst)),
                       jnp.broadcast_to(li_ref[...], (SUBLANES, kst)), False)
        y_ref[...] = (_dot(hr_ref[...], mcr_ref[...]) - _dot(hi_ref[...], mci_ref[...])
                      + d_ref[...] * u.astype(F32))

    act = pl.BlockSpec((l, kin), lambda b: (0, b))
    state = pl.BlockSpec((l, kst), lambda b: (0, b))
    up = pl.BlockSpec((None, kin, kst), lambda b: (b, 0, 0))
    down = pl.BlockSpec((None, kst, kin), lambda b: (b, 0, 0))
    hshape = jax.ShapeDtypeStruct((l, nb * kst), F32)
    return _pallas(
        body, [u_seg, mb_re, mb_im, mc_re, mc_im, lam_re, lam_im, dvec], deps, name=name, grid=(nb,),
        in_specs=[act, up, up, down, down, pl.BlockSpec((1, kst), lambda b: (0, b)),
                  pl.BlockSpec((1, kst), lambda b: (0, b)), pl.BlockSpec((1, kin), lambda b: (0, b))],
        out_specs=[state, state, act],
        out_shape=[hshape, hshape, jax.ShapeDtypeStruct((l, nb * kin), F32)],
        compiler_params=_params("parallel"))


def _dot_ta(a, b):
    return lax.dot_general(a.astype(BF16), b.astype(BF16), (((0,), (0,)), ((), ())), preferred_element_type=F32)


def _dot_nt(a, b):
    return lax.dot_general(a.astype(BF16), b.astype(BF16), (((1,), (1,)), ((), ())), preferred_element_type=F32)


def _s5_backward(dy_seg, u_seg, h_re, h_im, mb_re, mb_im, mc_re, mc_im, lam_re, lam_im_neg, dvec, name):
    l = dy_seg.shape[0]
    nb, kin, kst = mb_re.shape
    nt = l // SUBLANES

    def body(dy_ref, u_ref, hr_ref, hi_ref, mbr_ref, mbi_ref, mcr_ref, mci_ref, lr_ref, li_ref,
             d_ref, du_ref, gcr_ref, gci_ref, gbr_ref, gbi_ref, glr_ref, gli_ref, dsk_ref, qr_ref, qi_ref):
        dy = dy_ref[...]
        qr_ref[...] = _dot_nt(dy, mcr_ref[...])
        qi_ref[...] = -_dot_nt(dy, mci_ref[...])
        _scan_in_place(qr_ref, qi_ref, jnp.broadcast_to(lr_ref[...], (SUBLANES, kst)),
                       jnp.broadcast_to(li_ref[...], (SUBLANES, kst)), True)
        du_ref[...] = _dot_nt(qr_ref[...], mbr_ref[...]) + _dot_nt(qi_ref[...], mbi_ref[...]) + d_ref[...] * dy
        dsk_ref[...] = jnp.sum(dy * _f32(u_ref), axis=0, keepdims=True)
        gcr_ref[...] = _dot_ta(dy, hr_ref[...])
        gci_ref[...] = _dot_ta(dy, hi_ref[...])
        gbr_ref[...] = _dot_ta(u_ref[...], qr_ref[...])
        gbi_ref[...] = _dot_ta(u_ref[...], qi_ref[...])

        row = lax.broadcasted_iota(jnp.int32, (SUBLANES, kst), 0)
        last = pl.ds((nt - 1) * SUBLANES, SUBLANES)
        first = pl.ds(0, SUBLANES)
        pr = jnp.where(row == 0, 0.0, pltpu.roll(hr_ref[last, :], 1, axis=0))
        pi = jnp.where(row == 0, 0.0, pltpu.roll(hi_ref[last, :], 1, axis=0))
        gr, gi = qr_ref[first, :], qi_ref[first, :]

        def step(t, carry):
            acc_r, acc_i = carry
            cur = pl.ds(pl.multiple_of(t * SUBLANES, SUBLANES), SUBLANES)
            prev = pl.ds(pl.multiple_of((t - 1) * SUBLANES, SUBLANES), SUBLANES)
            gr, gi = qr_ref[cur, :], qi_ref[cur, :]
            pr, pi = hr_ref[prev, :], hi_ref[prev, :]
            return acc_r + gr * pr + gi * pi, acc_i + gi * pr - gr * pi

        acc_r, acc_i = lax.fori_loop(1, nt, step, (gr * pr + gi * pi, gi * pr - gr * pi))
        glr_ref[...] = jnp.sum(acc_r, axis=0, keepdims=True)
        gli_ref[...] = jnp.sum(acc_i, axis=0, keepdims=True)

    act = pl.BlockSpec((l, kin), lambda b: (0, b))
    state = pl.BlockSpec((l, kst), lambda b: (0, b))
    up = pl.BlockSpec((None, kin, kst), lambda b: (b, 0, 0))
    down = pl.BlockSpec((None, kst, kin), lambda b: (b, 0, 0))
    vec_st = pl.BlockSpec((1, kst), lambda b: (0, b))
    vec_in = pl.BlockSpec((1, kin), lambda b: (0, b))
    outer = jax.ShapeDtypeStruct((nb, kin, kst), F32)
    lam_shape = jax.ShapeDtypeStruct((1, nb * kst), F32)
    return pl.pallas_call(
        body, name=name, grid=(nb,),
        in_specs=[act, act, state, state, up, up, down, down, vec_st, vec_st, vec_in],
        out_specs=[act, up, up, up, up, vec_st, vec_st, vec_in],
        out_shape=[jax.ShapeDtypeStruct((l, nb * kin), F32), outer, outer, outer, outer, lam_shape, lam_shape,
                   jax.ShapeDtypeStruct((1, nb * kin), F32)],
        scratch_shapes=[pltpu.VMEM((l, kst), F32), pltpu.VMEM((l, kst), F32)],
        compiler_params=_params("parallel"),
    )(dy_seg, u_seg, h_re, h_im, mb_re, mb_im, mc_re, mc_im, lam_re, lam_im_neg, dvec)


def _block_diag(m, nb):
    g, r, s = m.shape
    gb = g // nb
    eye = jnp.eye(gb, dtype=m.dtype)
    out = m.reshape(nb, gb, r, 1, s) * eye[None, :, None, :, None]
    return out.reshape(nb, gb * r, gb * s)


def _block_diag_extract(mat, g, r, s):
    nb = mat.shape[0]
    gb = g // nb
    eye = jnp.eye(gb, dtype=mat.dtype)
    m5 = mat.reshape(nb, gb, r, gb, s) * eye[None, :, None, :, None]
    return jnp.sum(m5, axis=3).reshape(g, r, s)


def _adamw(w, m, v, gslots, name, layer=0, prev=None, deps=()):
    layers, r, c = w.shape
    s = gslots.shape[0]
    tr = _tile(r, max(SUBLANES, 1 << int(math.log2(ADAMW_BLOCK_ELEMS // c))))
    bc1 = 1.0 / (1.0 - ADAM_B1 ** ADAM_STEP)
    bc2 = 1.0 / (1.0 - ADAM_B2 ** ADAM_STEP)

    def body(w_ref, m_ref, v_ref, g_ref, *rest):
        go_ref, d_ref, mo_ref, vo_ref = rest[-4:]
        g = g_ref[0].astype(F32)
        for k in range(1, s):
            g = g + g_ref[k].astype(F32)
        mn = ADAM_B1 * m_ref[...] + (1.0 - ADAM_B1) * g
        vn = ADAM_B2 * v_ref[...] + (1.0 - ADAM_B2) * (g * g)
        go_ref[...] = g
        mo_ref[...] = mn
        vo_ref[...] = vn
        d_ref[...] = -ADAM_LR * ((mn * bc1) / (jnp.sqrt(vn * bc2) + ADAM_EPS) + ADAM_WD * w_ref[...])

    spec = pl.BlockSpec((None, tr, c), lambda i: (layer, i, 0))
    out = jax.ShapeDtypeStruct((layers, r, c), F32)
    in_specs = [spec, spec, spec, pl.BlockSpec((s, tr, c), lambda i: (0, i, 0))]
    args = [w, m, v, gslots]
    aliases = {}
    if prev is not None:
        in_specs += [ANY_SPEC] * 4
        args += list(prev)
        aliases = {4 + q: q for q in range(4)}
    in_specs += [ANY_SPEC] * len(deps)
    args += list(deps)
    return pl.pallas_call(
        body, name=name, grid=(r // tr,), in_specs=in_specs,
        out_specs=[spec] * 4, out_shape=[out] * 4, input_output_aliases=aliases,
        compiler_params=_params("parallel"),
    )(*args)


def _pack(parts):
    flat = jnp.concatenate([p.reshape(-1) for p in parts])
    pad = (-flat.shape[0]) % (PACK_ROWS * LANES)
    return jnp.pad(flat, (0, pad)).reshape(-1, LANES)


def _unpack(packed, shapes):
    flat = packed.reshape(-1)
    out, off = [], 0
    for shp in shapes:
        size = math.prod(shp)
        out.append(flat[off:off + size].reshape(shp))
        off += size
    return out


def kernel(x, norm_g, w_in, conv_w, w_out_a, a_re, a_im, log_dt, b_re, b_im, c_re, c_im, d_skip, w_glu, b_glu, w_out_b, w_o, final_g, loss_target, m_norm_g, m_w_in, m_conv_w, m_w_out_a, m_a_re, m_a_im, m_log_dt, m_b_re, m_b_im, m_c_re, m_c_im, m_d_skip, m_w_glu, m_b_glu, m_w_out_b, m_w_o, m_final_g, v_norm_g, v_w_in, v_conv_w, v_w_out_a, v_a_re, v_a_im, v_log_dt, v_b_re, v_b_im, v_c_re, v_c_im, v_d_skip, v_w_glu, v_b_glu, v_w_out_b, v_w_o, v_final_g):
    depth = norm_g.shape[0]
    l, d = x.shape[1], x.shape[2]
    ws = w_glu.shape[2]
    n_groups, n_state = a_re.shape[1], a_re.shape[2]
    nb = ws // LANES
    assert S5_GROUP == b_re.shape[3] and n_state * S5_GB == 4 * LANES
    u_col, zb_col = 4 * d // ws, 4 * d // ws + 1
    ga_col, gb_col = (4 * d + 2 * ws) // d, (4 * d + 2 * ws) // d + 1
    me = 4 * lax.axis_index("x") + 2 * lax.axis_index("y") + lax.axis_index("c")

    xs = [x[0]]
    tgt = loss_target[0]

    big_names = ("w_in", "w_out_a", "w_glu", "w_out_b", "w_o")
    big = dict(w_in=(w_in, m_w_in, v_w_in), w_out_a=(w_out_a, m_w_out_a, v_w_out_a),
               w_glu=(w_glu, m_w_glu, v_w_glu), w_out_b=(w_out_b, m_w_out_b, v_w_out_b),
               w_o=(w_o, m_w_o, v_w_o))

    all_shards = [[None] * len(big_names) for _ in range(depth)]
    all_shards[0][0] = w_in[0].astype(BF16)

    def shards_bf16(i):
        return all_shards[i]

    main_names = ("a_re", "a_im", "log_dt", "b_re", "b_im", "c_re", "c_im", "d_skip", "b_glu", "conv_w")
    small_w = dict(a_re=(a_re, m_a_re, v_a_re), a_im=(a_im, m_a_im, v_a_im),
                   log_dt=(log_dt, m_log_dt, v_log_dt), b_re=(b_re, m_b_re, v_b_re), b_im=(b_im, m_b_im, v_b_im),
                   c_re=(c_re, m_c_re, v_c_re), c_im=(c_im, m_c_im, v_c_im), d_skip=(d_skip, m_d_skip, v_d_skip),
                   b_glu=(b_glu, m_b_glu, v_b_glu))
    main_shapes = [small_w[k][0].shape for k in main_names[:-1]] + [(depth, 3, d)]
    dc = d // N_DEV
    pad8 = lambda a: jnp.pad(a.reshape(depth * 3, dc), ((0, SUBLANES - depth * 3), (0, 0)))[None]

    def input_only_work(zero):
        for i in range(depth):
            for j, k in enumerate(big_names):
                if all_shards[i][j] is None:
                    all_shards[i][j] = (big[k][0][i] + zero).astype(BF16)
        zeros_conv = jnp.zeros((depth, 3, d), F32) + zero
        zero1 = jnp.zeros((1,), F32) + zero
        main = [_pack([small_w[k][q] for k in main_names[:-1]] + [zeros_conv])[None] for q in range(3)]
        gains = [_pack([g_, f_, zero1])[None]
                 for g_, f_ in ((norm_g, final_g), (m_norm_g, m_final_g), (v_norm_g, v_final_g))]
        return main, gains, [pad8(conv_w), pad8(m_conv_w), pad8(v_conv_w)]

    def gather_start(shards, name, deps=()):
        sems, srcs, lands, token = _exchange_start(
            _plan_gather_chips, shards, [_landing(s_, N_DEV, me) for s_ in shards], f"{name}_start", deps)
        return (name, sems, srcs, lands), token

    def gather_forward(state, after, deps=()):
        name, sems, srcs, lands = state
        _, lands = _exchange_wait(_plan_gather_chips, sems, srcs, lands, after, f"{name}_wait")
        sems, _, lands, token = _exchange_start(_plan_gather_forward, None, lands, f"{name}_forward_start", deps)
        return (name, sems, lands), token

    def gather_finish(state, after):
        name, sems, lands = state
        return _exchange_wait(_plan_gather_forward, sems, None, lands, after, f"{name}_forward_wait")[1]

    def halves_start(shards, name, deps=()):
        srcs = [s_.reshape(2, s_.shape[0] // 2, *s_.shape[1:]) for s_ in shards]
        lands = [lax.dynamic_update_slice(lax.empty((2 * N_DEV,) + h_.shape[1:], h_.dtype), h_, (2 * me, 0, 0))
                 for h_ in srcs]
        sems, srcs, lands, token = _exchange_start(_plan_halves_first, srcs, lands, f"{name}_start", deps)
        return (name, sems, srcs, lands), token

    def halves_second(state, after, deps=()):
        name, sems, srcs, lands = state
        _, lands = _exchange_wait(_plan_halves_first, sems, srcs, lands, after, f"{name}_wait")
        sems, _, lands, token = _exchange_start(_plan_halves_second, None, lands, f"{name}_second_start", deps)
        return (name, sems, lands), token

    def halves_forward(state, after, deps=()):
        name, sems, lands = state
        _, lands = _exchange_wait(_plan_halves_second, sems, None, lands, after, f"{name}_second_wait")
        sems, _, lands, token = _exchange_start(_plan_halves_forward, None, lands, f"{name}_forward_start", deps)
        return (name, sems, lands), token

    def halves_finish(state, after):
        name, sems, lands = state
        lands = _exchange_wait(_plan_halves_forward, sems, None, lands, after, f"{name}_forward_wait")[1]
        return [l_.reshape(N_DEV, 2 * l_.shape[1], *l_.shape[2:]) for l_ in lands]

    conv_shard = jnp.pad(conv_w.reshape(depth * 3, -1), ((0, HALO - depth * 3), (0, 0)))
    w_in_state, token = halves_start([shards_bf16(0)[0], conv_shard], "ag_w_in_0")
    s5 = []
    shape3 = (n_groups, n_state, S5_GROUP)
    for i in range(depth):
        dense_in = (_dense(a_re[i][:, :, None], shape3), _dense(a_im[i][:, :, None], shape3),
                    _dense(log_dt[i][:, None, None], shape3), b_re[i].reshape(-1, LANES), b_im[i].reshape(-1, LANES))
        lbr, lbi = _s5_params(*dense_in, False, f"s5_lam_{i}", deps=(token,))
        s5.append(dict(dense_in=dense_in, lam_re=lbr.reshape(shape3)[:, :, 0].reshape(1, -1),
                       lam_im=lbi.reshape(shape3)[:, :, 0].reshape(1, -1)))
    main_wmv, gains_wmv, conv_wmv = input_only_work(token[0, 0])
    w_in_state, token = halves_second(
        w_in_state, [p[k] for p in s5 for k in ("lam_re", "lam_im")] + main_wmv + gains_wmv + conv_wmv)
    for i in range(depth):
        bbr, bbi = _s5_params(*s5[i]["dense_in"], True, f"s5_input_matrix_{i}", deps=(token,))
        bbr3, bbi3 = bbr.reshape(shape3), bbi.reshape(shape3)
        diag = lambda m: _block_diag(m, nb).astype(BF16)
        s5[i].update(up=(diag(bbr3.transpose(0, 2, 1)), diag(bbi3.transpose(0, 2, 1))),
                     down=(diag((c_re[i] + token[0, 0]).transpose(0, 2, 1)),
                           diag((c_im[i] + token[0, 0]).transpose(0, 2, 1))))
    prologue = [m for p in s5 for k in ("up", "down") for m in p[k]]
    prologue += main_wmv + gains_wmv + conv_wmv + all_shards[0][1:] + [s_ for sh in all_shards[1:] for s_ in sh]
    w_in_state, token = halves_forward(w_in_state, prologue, deps=(token,))
    rest_state, token = halves_start(shards_bf16(0)[1:], "ag_rest_0", deps=(token,))
    next_state = None
    if depth > 1:
        next_state, token = halves_start([shards_bf16(1)[0]], "ag_w_in_1", deps=(token,))

    saved = []
    wg = [None] * depth
    conv_full = None
    for i in range(depth):
        xi = xs[-1]
        h = _rmsnorm_fwd(xi, norm_g[i], f"rmsnorm_fwd_{i}", deps=(token,))
        arrived = halves_finish(w_in_state, [h])
        if i == 0:
            conv_full = arrived[1].transpose(1, 0, 2).reshape(HALO, d)[:depth * 3].reshape(depth, 3, d)
        conv8 = jnp.pad(conv_full[i], ((0, SUBLANES - 3), (0, 0)))
        proj = _mm_win_fwd(h, arrived[0], f"mm_proj_{i}")
        rest_state, token = halves_second(rest_state, [proj])
        if next_state is not None:
            next_state, token = halves_second(next_state, [token], deps=(token,))
        u_seg = _to_segments(proj[:, 4 * d:4 * d + ws])
        h_re, h_im, y_seg = _s5_forward(u_seg, *s5[i]["up"], *s5[i]["down"], s5[i]["lam_re"], s5[i]["lam_im"],
                                        d_skip[i].reshape(1, ws), f"s5_forward_{i}", deps=(token,))
        rest_state, token = halves_forward(rest_state, [y_seg])
        pa = _branch_a_fwd(proj, conv8, d, f"branch_a_fwd_{i}", deps=(token,))
        rest = halves_finish(rest_state, [pa])
        wg[i] = g = dict(w_in=arrived[0], w_a=rest[0].reshape(d, d), w_glu=rest[1].reshape(ws, ws),
                         w_b=rest[2], w_o=rest[3].reshape(d, d))
        ya = _mm(pa, g["w_a"], name=f"mm_ya_{i}", out_dtype=BF16)
        y = _from_segments(y_seg)
        yg = _gelu_cast(y, f"gelu_{i}")
        gl = _mm(yg, g["w_glu"], name=f"mm_glu_{i}", out_dtype=BF16)
        pb = _glu_post(y, gl, proj, b_glu[i], zb_col, f"glu_post_{i}")
        w_b2d = g["w_b"].transpose(1, 0, 2).reshape(ws, d)
        yb = _mm(pb, w_b2d, name=f"mm_yb_{i}", out_dtype=BF16)
        mrg = _merge_fwd(proj, ya, yb, d, ga_col, gb_col, f"merge_fwd_{i}")
        deps = ()
        if i + 1 < depth:
            w_in_state, token = halves_forward(next_state, [mrg])
            rest_state, token = halves_start(shards_bf16(i + 1)[1:], f"ag_rest_{i + 1}", deps=(token,))
            next_state = None
            if i + 2 < depth:
                next_state, token = halves_start([shards_bf16(i + 2)[0]], f"ag_w_in_{i + 2}", deps=(token,))
            deps = (token,)
        xs.append(_mm(mrg, g["w_o"], name=f"mm_out_{i}", add=xi, deps=deps))
        saved.append(dict(h=h, proj=proj, pa=pa, ya=ya, yb=yb, y=y, yg=yg, gl=gl, pb=pb, mrg=mrg,
                          u_seg=u_seg, h_re=h_re, h_im=h_im, conv8=conv8, w_b2d=w_b2d))

    dx, g_final, loss_part = _final_loss(xs[-1], final_g, tgt, "final_loss")

    rs_pending = []
    small = {k: [None] * depth for k in ("norm_g", "a_re", "a_im", "log_dt", "b_re", "b_im", "c_re", "c_im",
                                         "d_skip", "b_glu", "conv_w")}

    my_chip = 2 * lax.axis_index("x") + lax.axis_index("y")

    def reduce_on_chip(pieces, tag):
        lands = [lax.empty((4,) + p.shape[1:], p.dtype) for p in pieces]
        sems, srcs, lands, token = _exchange_start(_plan_reduce_sibling, pieces, lands, f"rs_sibling_start_{tag}")
        return (sems, srcs, lands), token

    def reduce_across_chips(names_, state, layer, tag, after):
        sems, srcs, lands = state
        srcs, lands = _exchange_wait(_plan_reduce_sibling, sems, srcs, lands, after, f"rs_sibling_wait_{tag}")
        sums = [_chip_sums(p, l_, f"chip_sum_{k}_{layer}") for k, p, l_ in zip(names_, srcs, lands)]
        lands = [_landing(lax.dynamic_index_in_dim(s_, my_chip, 0, keepdims=False), 4, my_chip) for s_ in sums]
        sems, srcs, lands, token = _exchange_start(_plan_reduce_chips, sums, lands, f"rs_chips_start_{tag}")
        rs_pending.append((names_, layer, sems, srcs, lands, f"rs_chips_wait_{tag}"))
        return token

    for i in reversed(range(depth)):
        s, g = saved[i], wg[i]
        proj = s["proj"]
        dxo_b = dx.astype(BF16)
        dm = _mm(dxo_b, g["w_o"], name=f"mm_dm_{i}", nt=True, out_dtype=BF16)
        gw_o = _mm(s["mrg"], dxo_b, ta=True, name=f"mm_gw_o_{i}", out_dtype=BF16)
        dy2, dproj = _merge_bwd(proj, s["ya"], s["yb"], dm, d, ga_col, f"merge_bwd_{i}")
        dya, dyb = dy2[0], dy2[1]
        dpa = _mm(dya, g["w_a"], name=f"mm_dpa_{i}", nt=True, out_dtype=BF16)
        gw_a = _mm(s["pa"], dya, ta=True, name=f"mm_gw_a_{i}", out_dtype=BF16)
        dpb = _mm(dyb, s["w_b2d"], name=f"mm_dpb_{i}", nt=True, out_dtype=BF16)
        gw_b = _mm(s["pb"], dyb, ta=True, name=f"mm_gw_b_{i}", split_n=N_DEV, out_dtype=BF16)
        dproj, dw0, dw1, dw2 = _branch_a_bwd(proj, dpa, s["conv8"], dproj, d, f"branch_a_bwd_{i}")
        small["conv_w"][i] = jnp.concatenate([dw0, dw1, dw2], axis=0)
        dproj, dgl, t1, db_glu = _glu_bwd1(s["y"], s["gl"], proj, b_glu[i], dpb, dproj, zb_col, f"glu_bwd1_{i}")
        small["b_glu"][i] = db_glu.reshape(ws)
        dyg2 = _mm(dgl, g["w_glu"], name=f"mm_dyg_{i}", nt=True, out_dtype=BF16)
        gw_glu = _mm(s["yg"], dgl, ta=True, name=f"mm_gw_glu_{i}", out_dtype=BF16)
        small_names_ = ("w_out_a", "w_glu", "w_out_b", "w_o")
        state, token = reduce_on_chip(
            [gw_a.reshape(N_DEV, d // N_DEV, d), gw_glu.reshape(N_DEV, ws // N_DEV, ws), gw_b,
             gw_o.reshape(N_DEV, d // N_DEV, d)], f"small_{i}")
        dy = _glu_bwd2(s["y"], t1, dyg2, f"glu_bwd2_{i}", deps=(token,))
        dy_seg = _to_segments(dy)
        u_seg = s["u_seg"]
        du_seg, gc_re, gc_im, gbb_re, gbb_im, glam_re, glam_im, dskip = _s5_backward(
            dy_seg, u_seg, s["h_re"], s["h_im"], *s5[i]["up"], *s5[i]["down"],
            s5[i]["lam_re"], -s5[i]["lam_im"], d_skip[i].reshape(1, ws), f"s5_backward_{i}")
        token = reduce_across_chips(small_names_, state, i, f"small_{i}", [du_seg])
        dproj = _write_cols(dproj, _from_segments(du_seg), u_col, f"write_du_{i}")
        gw_in = _mm(s["h"], dproj, ta=True, name=f"mm_gw_in_{i}", split_n=N_DEV, tm=1024,
                    out_dtype=BF16, deps=(token,))
        state, token = reduce_on_chip([gw_in], f"w_in_{i}")
        small["d_skip"][i] = dskip.reshape(n_groups, S5_GROUP)
        small["c_re"][i] = _block_diag_extract(gc_re, n_groups, S5_GROUP, n_state)
        small["c_im"][i] = -_block_diag_extract(gc_im, n_groups, S5_GROUP, n_state)
        gbb_re = _block_diag_extract(gbb_re, n_groups, S5_GROUP, n_state).transpose(0, 2, 1)
        gbb_im = _block_diag_extract(gbb_im, n_groups, S5_GROUP, n_state).transpose(0, 2, 1)
        gar, gai, gdt, gbr, gbi = _s5_params_bwd(
            *s5[i]["dense_in"], _dense(glam_re.reshape(n_groups, n_state, 1), shape3),
            _dense(glam_im.reshape(n_groups, n_state, 1), shape3),
            gbb_re.reshape(-1, LANES), gbb_im.reshape(-1, LANES), n_groups, f"s5_params_bwd_{i}", deps=(token,))
        small["a_re"][i] = gar.reshape(shape3)[:, :, 0]
        small["a_im"][i] = gai.reshape(shape3)[:, :, 0]
        small["log_dt"][i] = gdt[:, 0]
        small["b_re"][i] = gbr.reshape(shape3)
        small["b_im"][i] = gbi.reshape(shape3)
        if i == 0:
            part = {k: jnp.stack(small[k]) for k in main_names}
            main_state, token = gather_start([_pack([part[k] for k in main_names]).astype(BF16)], "ag_small")
            token = reduce_across_chips(("w_in",), state, i, f"w_in_{i}", [token])
            main_state, token = gather_forward(main_state, [token])
            dh = _mm_win_bwd(dproj, g["w_in"], f"mm_dh_{i}", deps=(token,))
            main_slots = gather_finish(main_state, [dh])[0]
            deps = ()
        else:
            dh = _mm_win_bwd(dproj, g["w_in"], f"mm_dh_{i}", deps=(gar,))
            deps = (reduce_across_chips(("w_in",), state, i, f"w_in_{i}", [dh]),)
        dx, dng = _rmsnorm_bwd(xs[i], norm_g[i], dh, dx, f"rmsnorm_bwd_{i}", deps=deps)
        small["norm_g"][i] = dng.reshape(d)

    results = {}

    gain_grads = jnp.concatenate([jnp.stack(small["norm_g"]).reshape(-1), g_final.reshape(d)])
    gain_shapes = [(depth, d), (d,), (1,)]
    gains_state, token = gather_start([_pack([gain_grads, loss_part[0, :1]])], "ag_gains")

    sres = [_unpack(p[0], main_shapes) for p in _adamw(*main_wmv, main_slots, "adamw_small", deps=(token,))]
    for j, k in enumerate(main_names[:-1]):
        results[k] = [sres[q][j] for q in range(4)]
    gconv = lax.dynamic_slice_in_dim(sres[0][-1], me * dc, dc, axis=2)
    cres = _adamw(*conv_wmv, pad8(gconv), "adamw_conv_w")
    results["conv_w"] = [r_[0, :depth * 3].reshape(depth, 3, dc) for r_ in cres]

    after = [cres[0]]
    for names_, layer, sems, srcs, lands, wait_name in rs_pending:
        _, slots = _exchange_wait(_plan_reduce_chips, sems, srcs, lands, after, wait_name)
        for k, land in zip(names_, slots):
            w_, m_, v_ = big[k]
            results[k] = _adamw(w_, m_, v_, land, f"adamw_{k}_{layer}", layer=layer, prev=results.get(k))
            after = [results[k][0]]

    gains_state, token = gather_forward(gains_state, after)
    gpack = gather_finish(gains_state, [token])[0]
    gres = [_unpack(p[0], gain_shapes) for p in _adamw(*gains_wmv, gpack, "adamw_gains")]
    results["norm_g"] = [gres[q][0] for q in range(4)]
    results["final_g"] = [gres[q][1] for q in range(4)]
    loss = gres[0][2][0]

    names = ("norm_g", "w_in", "conv_w", "w_out_a", "a_re", "a_im", "log_dt", "b_re", "b_im", "c_re", "c_im",
             "d_skip", "w_glu", "b_glu", "w_out_b", "w_o", "final_g")
    outs = [loss, dx[None]]
    for q in range(4):
        outs += [results[k][q] for k in names]
    return tuple(outs)
```

```python
import functools
import math

import jax
import jax.numpy as jnp
from jax import lax
from jax.experimental import pallas as pl
from jax.experimental.pallas import tpu as pltpu

F32 = jnp.float32
BF16 = jnp.bfloat16
HIGHEST = lax.Precision.HIGHEST

N_DEV = 8
LANES = 128
SUBLANES = 8
VMEM_LIMIT_BYTES = 56 * 1024 * 1024

RMS_EPS = 1e-6
ADAM_LR = 0.001
ADAM_B1 = 0.9
ADAM_B2 = 0.999
ADAM_EPS = 1e-08
ADAM_WD = 0.01
ADAM_STEP = 10
GELU_C0 = math.sqrt(2.0 / math.pi)
GELU_C1 = 0.044715

ADAMW_BLOCK_ELEMS = 1 << 17
PACK_ROWS = 512

S5_GROUP = 16
S5_GB = LANES // S5_GROUP


def _params(*semantics):
    return pltpu.CompilerParams(dimension_semantics=semantics, vmem_limit_bytes=VMEM_LIMIT_BYTES)


ANY_SPEC = pl.BlockSpec(memory_space=pl.ANY)


def _pallas(body, args, deps=(), *, in_specs, **kwargs):
    deps = tuple(deps)
    if not deps:
        return pl.pallas_call(body, in_specs=in_specs, **kwargs)(*args)

    def body_after(*refs):
        body(*refs[len(deps):])

    return pl.pallas_call(body_after, in_specs=[ANY_SPEC] * len(deps) + list(in_specs), **kwargs)(*deps, *args)


def _tile(n, pref):
    t = min(n, pref)
    while n % t:
        assert t % 2 == 0, (n, pref)
        t //= 2
    return t


def _sigmoid(z):
    return 1.0 / (1.0 + jnp.exp(-z))


def _gelu(y):
    return 0.5 * y * (1.0 + jnp.tanh(GELU_C0 * (y + GELU_C1 * y * y * y)))


def _gelu_grad(y):
    t = jnp.tanh(GELU_C0 * (y + GELU_C1 * y * y * y))
    return 0.5 * (1.0 + t) + 0.5 * y * (1.0 - t * t) * GELU_C0 * (1.0 + 3.0 * GELU_C1 * y * y)


HBM_SPEC = pl.BlockSpec(memory_space=pltpu.HBM)
SEM_SPEC = pl.BlockSpec(memory_space=pltpu.SEMAPHORE)
DATAFLOW_EFFECT = pltpu.SideEffectType.DATAFLOW_SIDE_EFFECTING
OTHER_CHIPS = (2, 4, 6)


def _flip(pos, mask):
    x, y, c = pos
    return x ^ ((mask >> 2) & 1), y ^ ((mask >> 1) & 1), c ^ (mask & 1)


def _dev(pos):
    return 4 * pos[0] + 2 * pos[1] + pos[2]


def _chip(pos):
    return 2 * pos[0] + pos[1]


def _plan_gather_chips(me):
    return [(_flip(me, k), None, _dev(me), _dev(_flip(me, k))) for k in (1,) + OTHER_CHIPS]


def _plan_gather_forward(me):
    sib = _flip(me, 1)
    return [(sib, _dev(_flip(me, k)), _dev(_flip(me, k)), _dev(_flip(sib, k))) for k in OTHER_CHIPS]


def _plan_reduce_sibling(me):
    sib = _flip(me, 1)
    return [(sib, 2 * q + sib[2], q, q) for q in range(4)]


def _plan_reduce_chips(me):
    return [(_flip(me, k), _chip(_flip(me, k)), _chip(me), _chip(_flip(me, k))) for k in OTHER_CHIPS]


X_CHIP, Y_CHIP, XY_CHIP = 4, 2, 6


def _plan_halves_first(me):
    sib, xn, yn = _flip(me, 1), _flip(me, X_CHIP), _flip(me, Y_CHIP)
    return [(sib, 0, 2 * _dev(me), 2 * _dev(sib)), (sib, 1, 2 * _dev(me) + 1, 2 * _dev(sib) + 1),
            (xn, 0, 2 * _dev(me), 2 * _dev(xn)), (yn, 1, 2 * _dev(me) + 1, 2 * _dev(yn) + 1)]


def _plan_halves_second(me):
    xn, yn, dg = _flip(me, X_CHIP), _flip(me, Y_CHIP), _flip(me, XY_CHIP)
    return [(yn, 2 * _dev(me), 2 * _dev(me), 2 * _dev(yn)), (yn, 2 * _dev(xn), 2 * _dev(xn), 2 * _dev(dg)),
            (xn, 2 * _dev(me) + 1, 2 * _dev(me) + 1, 2 * _dev(xn) + 1),
            (xn, 2 * _dev(yn) + 1, 2 * _dev(yn) + 1, 2 * _dev(dg) + 1)]


def _plan_halves_forward(me):
    sib = _flip(me, 1)
    return [(sib, 2 * _dev(_flip(me, k)) + h, 2 * _dev(_flip(me, k)) + h, 2 * _dev(_flip(sib, k)) + h)
            for k in OTHER_CHIPS for h in (0, 1)]


PLAN_COPIES = {_plan_gather_chips: 4, _plan_gather_forward: 3, _plan_reduce_sibling: 4, _plan_reduce_chips: 3,
               _plan_halves_first: 4, _plan_halves_second: 4, _plan_halves_forward: 6}


def _exchange_copies(plan, src_refs, land_refs, send_sems, recv_sems, incoming=True):
    me = (lax.axis_index("x"), lax.axis_index("y"), lax.axis_index("c"))
    pairs = []
    for b, (src_ref, land_ref) in enumerate(zip(src_refs, land_refs)):
        for j, (peer, src_slot, there, here) in enumerate(plan(me)):
            sem = b * PLAN_COPIES[plan] + j
            src = src_ref if src_slot is None else src_ref.at[src_slot]
            out = pltpu.make_async_remote_copy(
                src_ref=src, dst_ref=land_ref.at[there], send_sem=send_sems.at[sem], recv_sem=recv_sems.at[sem],
                device_id=peer, device_id_type=pl.DeviceIdType.MESH)
            inc = pltpu.make_async_remote_copy(
                src_ref=src, dst_ref=land_ref.at[here], send_sem=send_sems.at[sem], recv_sem=recv_sems.at[sem],
                device_id=peer, device_id_type=pl.DeviceIdType.MESH) if incoming else None
            pairs.append((out, inc))
    return pairs


def _exchange_start(plan, srcs, lands, name, deps=()):
    srcs = [] if srcs is None else list(srcs)
    ns, n, nd = len(srcs), len(lands), len(deps)

    def body(*refs):
        land_refs = refs[ns:ns + n]
        sems_at = ns + n + nd
        pairs = _exchange_copies(plan, refs[:ns] if ns else land_refs, land_refs, refs[sems_at], refs[sems_at + 1],
                                 incoming=False)
        for out, _ in pairs:
            out.start()
        token = refs[-1]
        token[...] = jnp.zeros_like(token)

    sems = pltpu.SemaphoreType.DMA((PLAN_COPIES[plan] * n,))
    bufs = srcs + list(lands)
    outs = pl.pallas_call(
        body, name=name,
        out_shape=(sems, sems, *[pltpu.HBM(a.shape, a.dtype) for a in bufs],
                   jax.ShapeDtypeStruct((SUBLANES, LANES), F32)),
        in_specs=[HBM_SPEC] * (ns + n) + [ANY_SPEC] * nd,
        out_specs=(SEM_SPEC, SEM_SPEC, *[HBM_SPEC] * (ns + n), pl.BlockSpec(memory_space=pltpu.VMEM)),
        input_output_aliases={i: 2 + i for i in range(ns + n)},
        compiler_params=pltpu.CompilerParams(has_side_effects=DATAFLOW_EFFECT),
    )(*[pltpu.with_memory_space_constraint(a, pltpu.HBM) for a in bufs], *deps)
    return (outs[0], outs[1]), (outs[2:2 + ns] if ns else None), outs[2 + ns:2 + ns + n], outs[-1]


def _exchange_wait(plan, sems, srcs, lands, after, name):
    srcs = [] if srcs is None else list(srcs)
    ns, n = len(srcs), len(lands)

    def body(*refs):
        land_refs = refs[ns:ns + n]
        pairs = _exchange_copies(plan, refs[:ns] if ns else land_refs, land_refs, refs[ns + n], refs[ns + n + 1])
        for out, inc in pairs:
            out.wait_send()
            inc.wait_recv()

    bufs = srcs + list(lands)
    outs = pl.pallas_call(
        body, name=name,
        out_shape=[pltpu.HBM(a.shape, a.dtype) for a in bufs],
        in_specs=[HBM_SPEC] * (ns + n) + [SEM_SPEC, SEM_SPEC] + [ANY_SPEC] * len(after),
        out_specs=[HBM_SPEC] * (ns + n),
        input_output_aliases={i: i for i in range(ns + n)},
        compiler_params=pltpu.CompilerParams(has_side_effects=DATAFLOW_EFFECT),
    )(*bufs, sems[0], sems[1], *after)
    return outs[:ns], outs[ns:]


def _landing(own, slots, slot):
    land = lax.empty((slots,) + own.shape, own.dtype)
    return lax.dynamic_update_slice(land, own[None], (slot,) + (0,) * own.ndim)


def _chip_sums(pieces, land, name):
    _, r, c_ = land.shape
    tr = _tile(r, max(2 * SUBLANES, 1 << int(math.log2(4 * ADAMW_BLOCK_ELEMS // c_))))

    def body(core_ref, p_ref, l_ref, o_ref):
        o_ref[...] = (p_ref[...].astype(F32) + l_ref[...].astype(F32)).astype(o_ref.dtype)

    spec = pl.BlockSpec((None, tr, c_), lambda q, i, core: (q, i, 0))
    return pl.pallas_call(
        body, name=name,
        grid_spec=pltpu.PrefetchScalarGridSpec(
            num_scalar_prefetch=1, grid=(4, r // tr),
            in_specs=[pl.BlockSpec((None, tr, c_), lambda q, i, core: (2 * q + core[0], i, 0)), spec],
            out_specs=spec),
        out_shape=jax.ShapeDtypeStruct(land.shape, land.dtype),
        compiler_params=_params("parallel", "parallel"),
    )(lax.axis_index("c").reshape(1), pieces, land)


def _mm(a, b, *, name, nt=False, ta=False, out_dtype=F32, add=None, split_n=None, tm=512, tn=1024, deps=()):
    k, m = a.shape if ta else a.shape[::-1]
    n = b.shape[0] if nt else b.shape[1]
    tm = _tile(m, tm)
    tn = n // split_n if split_n else _tile(n, tn)
    dims = (((0 if ta else 1,), (1 if nt else 0,)), ((), ()))

    def body(*refs):
        a_ref, b_ref = refs[0], refs[1]
        o_ref = refs[-1]
        acc = lax.dot_general(a_ref[...], b_ref[...], dims, preferred_element_type=F32)
        if add is not None:
            acc = acc + refs[2][...]
        o_ref[...] = acc.astype(o_ref.dtype)

    in_specs = [pl.BlockSpec((k, tm), lambda i, j: (0, i)) if ta else pl.BlockSpec((tm, k), lambda i, j: (i, 0)),
                pl.BlockSpec((tn, k), lambda i, j: (j, 0)) if nt
                else pl.BlockSpec((k, tn), lambda i, j: (0, j))]
    args = [a, b]
    if add is not None:
        in_specs.append(pl.BlockSpec((tm, tn), lambda i, j: (i, j)))
        args.append(add)
    if split_n:
        out_shape = jax.ShapeDtypeStruct((split_n, m, tn), out_dtype)
        out_spec = pl.BlockSpec((None, tm, tn), lambda i, j: (j, i, 0))
    else:
        out_shape = jax.ShapeDtypeStruct((m, n), out_dtype)
        out_spec = pl.BlockSpec((tm, tn), lambda i, j: (i, j))
    return _pallas(
        body, args, deps, name=name, grid=(m // tm, n // tn), in_specs=in_specs, out_specs=out_spec,
        out_shape=out_shape, compiler_params=_params("parallel", "parallel"))


def _mm_win_fwd(h, w_g, name, deps=()):
    m, k = h.shape
    nj = w_g.shape[2]
    tm = _tile(m, 512)

    def body(a_ref, b_ref, o_ref):
        o_ref[...] = jnp.dot(a_ref[...], b_ref[...], preferred_element_type=F32).astype(o_ref.dtype)

    return _pallas(
        body, [h, w_g], deps, name=name, grid=(N_DEV, m // tm),
        in_specs=[pl.BlockSpec((tm, k), lambda j, i: (i, 0)),
                  pl.BlockSpec((None, k, nj), lambda j, i: (j, 0, 0))],
        out_specs=pl.BlockSpec((tm, nj), lambda j, i: (i, j)),
        out_shape=jax.ShapeDtypeStruct((m, N_DEV * nj), BF16),
        compiler_params=_params("parallel", "parallel"))


def _mm_win_bwd(dproj, w_g, name, deps=()):
    m = dproj.shape[0]
    d, nj = w_g.shape[1], w_g.shape[2]
    tm = _tile(m, 512)
    tn = _tile(d, 1024)

    per_step = 2

    def body(a_ref, b_ref, o_ref, acc_ref):
        j = pl.program_id(2)

        @pl.when(j == 0)
        def _():
            acc_ref[...] = jnp.zeros_like(acc_ref)

        part = None
        for k in range(per_step):
            term = lax.dot_general(a_ref[:, k * nj:(k + 1) * nj], b_ref[k], (((1,), (1,)), ((), ())),
                                   preferred_element_type=F32)
            part = term if part is None else part + term
        acc_ref[...] += part

        @pl.when(j == N_DEV // per_step - 1)
        def _():
            o_ref[...] = acc_ref[...]

    return _pallas(
        body, [dproj, w_g], deps, name=name, grid=(m // tm, d // tn, N_DEV // per_step),
        in_specs=[pl.BlockSpec((tm, per_step * nj), lambda i, n, j: (i, j)),
                  pl.BlockSpec((per_step, tn, nj), lambda i, n, j: (j, n, 0))],
        out_specs=pl.BlockSpec((tm, tn), lambda i, n, j: (i, n)),
        out_shape=jax.ShapeDtypeStruct((m, d), F32),
        scratch_shapes=[pltpu.VMEM((tm, tn), F32)],
        compiler_params=_params("parallel", "parallel", "arbitrary"))


def _row_spec(tr, w, col):
    return pl.BlockSpec((tr, w), lambda i: (i, col))


def _full_spec(shape):
    return pl.BlockSpec(shape, lambda i: (0,) * len(shape))


def _rmsnorm_fwd(x, g, name, deps=()):
    l, d = x.shape
    tr = _tile(l, 256)

    def body(x_ref, g_ref, o_ref):
        xv = x_ref[...]
        rstd = lax.rsqrt(jnp.mean(xv * xv, axis=-1, keepdims=True) + RMS_EPS)
        o_ref[...] = (xv * rstd * g_ref[...]).astype(o_ref.dtype)

    return _pallas(
        body, [x, g.reshape(1, d)], deps, name=name, grid=(l // tr,),
        in_specs=[_row_spec(tr, d, 0), _full_spec((1, d))],
        out_specs=_row_spec(tr, d, 0),
        out_shape=jax.ShapeDtypeStruct((l, d), BF16),
        compiler_params=_params("parallel"))


def _rmsnorm_bwd(x, g, dh, dxo, name, deps=()):
    l, d = x.shape
    tr = _tile(l, 256)

    def body(x_ref, g_ref, dh_ref, dxo_ref, dx_ref, dg_ref):
        xv = x_ref[...]
        rstd = lax.rsqrt(jnp.mean(xv * xv, axis=-1, keepdims=True) + RMS_EPS)
        dhv = dh_ref[...]
        gdy = dhv * g_ref[...]
        dot = jnp.mean(gdy * xv, axis=-1, keepdims=True)
        dx_ref[...] = dxo_ref[...] + rstd * gdy - xv * (rstd * rstd * rstd * dot)

        @pl.when(pl.program_id(0) == 0)
        def _():
            dg_ref[...] = jnp.zeros_like(dg_ref)

        dg_ref[...] += jnp.sum(dhv * xv * rstd, axis=0, keepdims=True)

    return _pallas(
        body, [x, g.reshape(1, d), dh, dxo], deps, name=name, grid=(l // tr,),
        in_specs=[_row_spec(tr, d, 0), _full_spec((1, d)), _row_spec(tr, d, 0), _row_spec(tr, d, 0)],
        out_specs=[_row_spec(tr, d, 0), _full_spec((1, d))],
        out_shape=[jax.ShapeDtypeStruct((l, d), F32), jax.ShapeDtypeStruct((1, d), F32)],
        compiler_params=_params("arbitrary"))


def _final_loss(x, g, tgt, name):
    l, d = x.shape
    tr = _tile(l, 256)

    def body(x_ref, g_ref, t_ref, dx_ref, dg_ref, loss_ref):
        xv = x_ref[...]
        gv = g_ref[...]
        rstd = lax.rsqrt(jnp.mean(xv * xv, axis=-1, keepdims=True) + RMS_EPS)
        xn = xv * rstd
        err = xn * gv - t_ref[...]
        dy = err * (1.0 / d)
        gdy = dy * gv
        dot = jnp.mean(gdy * xv, axis=-1, keepdims=True)
        dx_ref[...] = rstd * gdy - xv * (rstd * rstd * rstd * dot)

        @pl.when(pl.program_id(0) == 0)
        def _():
            dg_ref[...] = jnp.zeros_like(dg_ref)
            loss_ref[...] = jnp.zeros_like(loss_ref)

        dg_ref[...] += jnp.sum(dy * xn, axis=0, keepdims=True)
        loss_ref[...] += (0.5 / d) * jnp.sum(err * err)

    return pl.pallas_call(
        body, name=name, grid=(l // tr,),
        in_specs=[_row_spec(tr, d, 0), _full_spec((1, d)), _row_spec(tr, d, 0)],
        out_specs=[_row_spec(tr, d, 0), _full_spec((1, d)), _full_spec((SUBLANES, LANES))],
        out_shape=[jax.ShapeDtypeStruct((l, d), F32), jax.ShapeDtypeStruct((1, d), F32),
                   jax.ShapeDtypeStruct((SUBLANES, LANES), F32)],
        compiler_params=_params("arbitrary"),
    )(x, g.reshape(1, d), tgt)


HALO = 2 * SUBLANES


def _halo_spec(tr, w, col, nblk, before):
    step = tr // HALO
    if before:
        return pl.BlockSpec((HALO, w), lambda i: (jnp.maximum(i * step - 1, 0), col))
    return pl.BlockSpec((HALO, w), lambda i: (jnp.minimum((i + 1) * step, nblk - 1), col))


def _shift_down(cur, before, k):
    ext = jnp.concatenate([before, cur], axis=0)
    return pltpu.roll(ext, k, axis=0)[HALO:, :]


def _shift_up(cur, after, k):
    tr = cur.shape[0]
    ext = jnp.concatenate([cur, after], axis=0)
    return pltpu.roll(ext, tr + HALO - k, axis=0)[:tr, :]


def _f32(ref):
    return ref[...].astype(F32)


def _branch_a_fwd(proj, conv_w, d, name, deps=()):
    l = proj.shape[0]
    tr = _tile(l, 256)
    nblk8 = l // HALO

    def body(v_ref, bg_ref, cg_ref, za_ref, vh_ref, cgh_ref, w_ref, o_ref):
        first = pl.program_id(0) == 0
        cv = _f32(cg_ref) * _f32(v_ref)
        cvh = jnp.where(first, 0.0, _f32(cgh_ref) * _f32(vh_ref))
        w0, w1, w2 = w_ref[0:1, :], w_ref[1:2, :], w_ref[2:3, :]
        q = w2 * cv + w1 * _shift_down(cv, cvh, 1) + w0 * _shift_down(cv, cvh, 2)
        za = _f32(za_ref)
        o_ref[...] = (_f32(bg_ref) * q * (za * _sigmoid(za))).astype(o_ref.dtype)

    return _pallas(
        body, [proj, proj, proj, proj, proj, proj, conv_w], deps, name=name, grid=(l // tr,),
        in_specs=[_row_spec(tr, d, 0), _row_spec(tr, d, 1), _row_spec(tr, d, 2), _row_spec(tr, d, 3),
                  _halo_spec(tr, d, 0, nblk8, True), _halo_spec(tr, d, 2, nblk8, True),
                  _full_spec((SUBLANES, d))],
        out_specs=_row_spec(tr, d, 0),
        out_shape=jax.ShapeDtypeStruct((l, d), BF16),
        compiler_params=_params("parallel"))


def _branch_a_bwd(proj, dpa, conv_w, dproj, d, name):
    l = proj.shape[0]
    tr = _tile(l, 128)
    nblk8 = l // HALO
    ntiles = l // tr

    def body(v_ref, bg_ref, cg_ref, za_ref, dpa_ref, vh_ref, cgh_ref, bgn_ref, zan_ref, dpan_ref,
             w_ref, _, o_ref, dw0_ref, dw1_ref, dw2_ref):
        dv_ref, dbg_ref, dcg_ref, dza_ref = [o_ref.at[:, pl.ds(k * d, d)] for k in range(4)]
        i = pl.program_id(0)
        v, bg, cg, za, dpa_v = _f32(v_ref), _f32(bg_ref), _f32(cg_ref), _f32(za_ref), _f32(dpa_ref)
        w0, w1, w2 = w_ref[0:1, :], w_ref[1:2, :], w_ref[2:3, :]
        cv = cg * v
        cvh = jnp.where(i == 0, 0.0, _f32(cgh_ref) * _f32(vh_ref))
        cv1 = _shift_down(cv, cvh, 1)
        cv2 = _shift_down(cv, cvh, 2)
        q = w2 * cv + w1 * cv1 + w0 * cv2
        sg = _sigmoid(za)
        s = za * sg
        dbg_ref[...] = (dpa_v * q * s).astype(dbg_ref.dtype)
        dza_ref[...] = (dpa_v * bg * q * (sg * (1.0 + za * (1.0 - sg)))).astype(dza_ref.dtype)
        dq = dpa_v * bg * s
        zan = _f32(zan_ref)
        dqn = jnp.where(i == ntiles - 1, 0.0, _f32(dpan_ref) * _f32(bgn_ref) * (zan * _sigmoid(zan)))
        dcv = w2 * dq + w1 * _shift_up(dq, dqn, 1) + w0 * _shift_up(dq, dqn, 2)
        dcg_ref[...] = (dcv * v).astype(dcg_ref.dtype)
        dv_ref[...] = (dcv * cg).astype(dv_ref.dtype)

        @pl.when(i == 0)
        def _():
            dw0_ref[...] = jnp.zeros_like(dw0_ref)
            dw1_ref[...] = jnp.zeros_like(dw1_ref)
            dw2_ref[...] = jnp.zeros_like(dw2_ref)

        dw0_ref[...] += jnp.sum(dq * cv2, axis=0, keepdims=True)
        dw1_ref[...] += jnp.sum(dq * cv1, axis=0, keepdims=True)
        dw2_ref[...] += jnp.sum(dq * cv, axis=0, keepdims=True)

    wsum = jax.ShapeDtypeStruct((1, d), F32)
    return pl.pallas_call(
        body, name=name, grid=(ntiles,),
        in_specs=[_row_spec(tr, d, 0), _row_spec(tr, d, 1), _row_spec(tr, d, 2), _row_spec(tr, d, 3),
                  _row_spec(tr, d, 0),
                  _halo_spec(tr, d, 0, nblk8, True), _halo_spec(tr, d, 2, nblk8, True),
                  _halo_spec(tr, d, 1, nblk8, False), _halo_spec(tr, d, 3, nblk8, False),
                  _halo_spec(tr, d, 0, nblk8, False),
                  _full_spec((SUBLANES, d)), ANY_SPEC],
        out_specs=[_row_spec(tr, 4 * d, 0)] + [_full_spec((1, d))] * 3,
        out_shape=[jax.ShapeDtypeStruct(dproj.shape, dproj.dtype)] + [wsum] * 3,
        input_output_aliases={11: 0},
        compiler_params=_params("arbitrary"),
    )(proj, proj, proj, proj, dpa, proj, proj, proj, proj, dpa, conv_w, dproj)


def _gelu_cast(y, name):
    l, w = y.shape
    tr = _tile(l, 512)

    def body(y_ref, o_ref):
        o_ref[...] = _gelu(y_ref[...]).astype(o_ref.dtype)

    return pl.pallas_call(
        body, name=name, grid=(l // tr,), in_specs=[_row_spec(tr, w, 0)],
        out_specs=_row_spec(tr, w, 0), out_shape=jax.ShapeDtypeStruct((l, w), BF16),
        compiler_params=_params("parallel"),
    )(y)


def _glu_post(y, gl, proj, b_glu, zb_col, name):
    l, w = y.shape
    tr = _tile(l, 512)

    def body(y_ref, gl_ref, zb_ref, b_ref, o_ref):
        zb = _f32(zb_ref)
        o_ref[...] = (_gelu(y_ref[...]) * _sigmoid(_f32(gl_ref) + b_ref[...])
                      * (zb * _sigmoid(zb))).astype(o_ref.dtype)

    return pl.pallas_call(
        body, name=name, grid=(l // tr,),
        in_specs=[_row_spec(tr, w, 0), _row_spec(tr, w, 0), _row_spec(tr, w, zb_col), _full_spec((1, w))],
        out_specs=_row_spec(tr, w, 0), out_shape=jax.ShapeDtypeStruct((l, w), BF16),
        compiler_params=_params("parallel"),
    )(y, gl, proj, b_glu.reshape(1, w))


def _glu_bwd1(y, gl, proj, b_glu, dpb, dproj, zb_col, name):
    l, w = y.shape
    tr = _tile(l, 512)

    def body(y_ref, gl_ref, zb_ref, b_ref, dpb_ref, _, dzb_ref, dgl_ref, t_ref, db_ref):
        zb = _f32(zb_ref)
        dpb_v = _f32(dpb_ref)
        yg = _gelu(y_ref[...])
        sgl = _sigmoid(_f32(gl_ref) + b_ref[...])
        szb = _sigmoid(zb)
        dzb_ref[...] = (dpb_v * yg * sgl * (szb * (1.0 + zb * (1.0 - szb)))).astype(dzb_ref.dtype)
        e = dpb_v * (zb * szb)
        dgl = e * yg * sgl * (1.0 - sgl)
        dgl_ref[...] = dgl.astype(dgl_ref.dtype)
        t_ref[...] = e * sgl

        @pl.when(pl.program_id(0) == 0)
        def _():
            db_ref[...] = jnp.zeros_like(db_ref)

        db_ref[...] += jnp.sum(dgl, axis=0, keepdims=True)

    return pl.pallas_call(
        body, name=name, grid=(l // tr,),
        in_specs=[_row_spec(tr, w, 0), _row_spec(tr, w, 0), _row_spec(tr, w, zb_col), _full_spec((1, w)),
                  _row_spec(tr, w, 0), ANY_SPEC],
        out_specs=[_row_spec(tr, w, zb_col)] + [_row_spec(tr, w, 0)] * 2 + [_full_spec((1, w))],
        out_shape=[jax.ShapeDtypeStruct(dproj.shape, dproj.dtype), jax.ShapeDtypeStruct((l, w), BF16),
                   jax.ShapeDtypeStruct((l, w), F32), jax.ShapeDtypeStruct((1, w), F32)],
        input_output_aliases={5: 0},
        compiler_params=_params("arbitrary"),
    )(y, gl, proj, b_glu.reshape(1, w), dpb, dproj)


def _write_cols(dproj, cols, col, name):
    l, w = cols.shape
    tr = _tile(l, 512)

    def body(c_ref, _, o_ref):
        o_ref[...] = c_ref[...].astype(o_ref.dtype)

    return pl.pallas_call(
        body, name=name, grid=(l // tr,), in_specs=[_row_spec(tr, w, 0), ANY_SPEC],
        out_specs=_row_spec(tr, w, col), out_shape=jax.ShapeDtypeStruct(dproj.shape, dproj.dtype),
        input_output_aliases={1: 0}, compiler_params=_params("parallel"),
    )(cols, dproj)


def _glu_bwd2(y, t1, dyg2, name, deps=()):
    l, w = y.shape
    tr = _tile(l, 512)

    def body(y_ref, t_ref, d_ref, o_ref):
        o_ref[...] = (t_ref[...] + _f32(d_ref)) * _gelu_grad(y_ref[...])

    return _pallas(
        body, [y, t1, dyg2], deps, name=name, grid=(l // tr,), in_specs=[_row_spec(tr, w, 0)] * 3,
        out_specs=_row_spec(tr, w, 0), out_shape=jax.ShapeDtypeStruct((l, w), F32),
        compiler_params=_params("parallel"))


def _merge_fwd(proj, ya, yb, d, ga_col, gb_col, name):
    l = proj.shape[0]
    tr = _tile(l, 256)

    def body(ga_ref, gb_ref, ya_ref, yb_ref, o_ref):
        o_ref[...] = (_sigmoid(_f32(ga_ref)) * _f32(ya_ref)
                      + _sigmoid(_f32(gb_ref)) * _f32(yb_ref)).astype(o_ref.dtype)

    return pl.pallas_call(
        body, name=name, grid=(l // tr,),
        in_specs=[_row_spec(tr, d, ga_col), _row_spec(tr, d, gb_col), _row_spec(tr, d, 0), _row_spec(tr, d, 0)],
        out_specs=_row_spec(tr, d, 0), out_shape=jax.ShapeDtypeStruct((l, d), BF16),
        compiler_params=_params("parallel"),
    )(proj, proj, ya, yb)


def _merge_bwd(proj, ya, yb, dm, d, ga_col, name):
    l, n = proj.shape
    tr = _tile(l, 256)

    def body(g_ref, ya_ref, yb_ref, dm_ref, dy_ref, dg_ref):
        dmv = _f32(dm_ref)
        sg = _sigmoid(_f32(g_ref))
        yv = jnp.where(pl.program_id(1) == 0, _f32(ya_ref), _f32(yb_ref))
        dy_ref[...] = (dmv * sg).astype(dy_ref.dtype)
        dg_ref[...] = (dmv * yv * sg * (1.0 - sg)).astype(dg_ref.dtype)

    row = pl.BlockSpec((tr, d), lambda i, j: (i, 0))
    return pl.pallas_call(
        body, name=name, grid=(l // tr, 2),
        in_specs=[pl.BlockSpec((tr, d), lambda i, j: (i, ga_col + j)), row, row, row],
        out_specs=[pl.BlockSpec((None, tr, d), lambda i, j: (j, i, 0)),
                   pl.BlockSpec((tr, d), lambda i, j: (i, ga_col + j))],
        out_shape=[jax.ShapeDtypeStruct((2, l, d), BF16), jax.ShapeDtypeStruct((l, n), BF16)],
        compiler_params=_params("parallel", "arbitrary"),
    )(proj, ya, yb, dm)


def _to_segments(a):
    l, w = a.shape
    return a.reshape(SUBLANES, l // SUBLANES, w).transpose(1, 0, 2).reshape(l, w)


def _from_segments(a):
    l, w = a.shape
    return a.reshape(l // SUBLANES, SUBLANES, w).transpose(1, 0, 2).reshape(l, w)


def _dense(z, shape):
    return jnp.broadcast_to(z, shape).reshape(-1, LANES)


def _s5_disc(are, aim, ldt):
    dt = jnp.exp(ldt)
    er = jnp.exp(are * dt)
    lbr = er * jnp.cos(aim * dt)
    lbi = er * jnp.sin(aim * dt)
    inv = 1.0 / (are * are + aim * aim)
    fr = ((lbr - 1.0) * are + lbi * aim) * inv
    fi = (lbi * are - (lbr - 1.0) * aim) * inv
    return dt, lbr, lbi, inv, fr, fi


def _s5_params(are, aim, ldt, bre, bim, input_matrix, name, deps=()):
    shape = are.shape

    def body(are_ref, aim_ref, ldt_ref, bre_ref, bim_ref, re_ref, im_ref):
        _, lbr, lbi, _, fr, fi = _s5_disc(are_ref[...], aim_ref[...], ldt_ref[...])
        if input_matrix:
            re_ref[...] = fr * bre_ref[...] - fi * bim_ref[...]
            im_ref[...] = fr * bim_ref[...] + fi * bre_ref[...]
        else:
            re_ref[...] = lbr
            im_ref[...] = lbi

    out = jax.ShapeDtypeStruct(shape, F32)
    return _pallas(body, [are, aim, ldt, bre, bim], deps, name=name,
                   in_specs=[pl.BlockSpec(memory_space=pltpu.VMEM)] * 5, out_shape=[out] * 2,
                   compiler_params=pltpu.CompilerParams(vmem_limit_bytes=VMEM_LIMIT_BYTES))


def _s5_params_bwd(are, aim, ldt, bre, bim, glbr, glbi, gbbr, gbbi, n_groups, name, deps=()):
    shape = are.shape
    rows_per_group = shape[0] // n_groups

    def body(are_ref, aim_ref, ldt_ref, bre_ref, bim_ref, glbr_ref, glbi_ref, gbbr_ref, gbbi_ref,
             gar_ref, gai_ref, gdt_ref, gbr_ref, gbi_ref):
        are_v, aim_v = are_ref[...], aim_ref[...]
        bre_v, bim_v = bre_ref[...], bim_ref[...]
        gbbr_v, gbbi_v = gbbr_ref[...], gbbi_ref[...]
        dt, lbr, lbi, inv, fr, fi = _s5_disc(are_v, aim_v, ldt_ref[...])
        gbr_ref[...] = fr * gbbr_v + fi * gbbi_v
        gbi_ref[...] = fr * gbbi_v - fi * gbbr_v
        lane_group = lax.broadcasted_iota(jnp.int32, (LANES, LANES), 0) // S5_GROUP
        same_group = (lane_group == lax.broadcasted_iota(jnp.int32, (LANES, LANES), 1) // S5_GROUP)
        ones = same_group.astype(F32)
        gfr = jnp.dot(bre_v * gbbr_v + bim_v * gbbi_v, ones, precision=HIGHEST, preferred_element_type=F32)
        gfi = jnp.dot(bre_v * gbbi_v - bim_v * gbbr_v, ones, precision=HIGHEST, preferred_element_type=F32)
        glr = glbr_ref[...] + (are_v * gfr - aim_v * gfi) * inv
        gli = glbi_ref[...] + (are_v * gfi + aim_v * gfr) * inv
        qr = (fr * are_v + fi * aim_v) * inv
        qi = (fi * are_v - fr * aim_v) * inv
        gzr = lbr * glr + lbi * gli
        gzi = lbr * gli - lbi * glr
        gar_ref[...] = dt * gzr - (qr * gfr + qi * gfi)
        gai_ref[...] = dt * gzi - (qr * gfi - qi * gfr)
        e = dt * (are_v * gzr + aim_v * gzi)
        per_group = jnp.sum(e.reshape(n_groups, rows_per_group, LANES), axis=1)
        total = jnp.sum(per_group, axis=1, keepdims=True) * (1.0 / S5_GROUP)
        gdt_ref[...] = jnp.broadcast_to(total, gdt_ref.shape)

    out = jax.ShapeDtypeStruct(shape, F32)
    return _pallas(
        body, [are, aim, ldt, bre, bim, glbr, glbi, gbbr, gbbi], deps, name=name,
        in_specs=[pl.BlockSpec(memory_space=pltpu.VMEM)] * 9,
        out_shape=[out, out, jax.ShapeDtypeStruct((n_groups, LANES), F32), out, out],
        compiler_params=pltpu.CompilerParams(vmem_limit_bytes=VMEM_LIMIT_BYTES))


def _cmul(ar, ai, br, bi):
    return ar * br - ai * bi, ar * bi + ai * br


def _scan_in_place(hr_ref, hi_ref, lr, li, reverse):
    l, wb = hr_ref.shape
    nt = l // SUBLANES
    shift = SUBLANES - 1 if reverse else 1
    unroll = 8 if nt % 8 == 0 else 1

    def rows(k):
        t = (nt - 1 - k) if reverse else k
        return pl.ds(pl.multiple_of(t * SUBLANES, SUBLANES), SUBLANES)

    zero = jnp.zeros((SUBLANES, wb), F32)

    def local_step(k, carry):
        hr, hi = carry
        r = rows(k)
        tr_, ti_ = _cmul(lr, li, hr, hi)
        hr, hi = tr_ + hr_ref[r, :], ti_ + hi_ref[r, :]
        hr_ref[r, :] = hr
        hi_ref[r, :] = hi
        return hr, hi

    er, ei = lax.fori_loop(0, nt, local_step, (zero, zero), unroll=unroll)

    lnr = lni = None
    br, bi, n = lr, li, nt
    while n:
        if n & 1:
            lnr, lni = (br, bi) if lnr is None else _cmul(lnr, lni, br, bi)
        n >>= 1
        if n:
            br, bi = _cmul(br, bi, br, bi)

    row = lax.broadcasted_iota(jnp.int32, (SUBLANES, wb), 0)
    tr_, ti_ = er, ei
    for j in range(1, SUBLANES):
        pr_, pi_ = _cmul(lnr, lni, pltpu.roll(tr_, shift, axis=0), pltpu.roll(ti_, shift, axis=0))
        at = row == ((SUBLANES - 1 - j) if reverse else j)
        tr_ = jnp.where(at, er + pr_, tr_)
        ti_ = jnp.where(at, ei + pi_, ti_)
    edge = row == ((SUBLANES - 1) if reverse else 0)
    cr = jnp.where(edge, 0.0, pltpu.roll(tr_, shift, axis=0))
    ci = jnp.where(edge, 0.0, pltpu.roll(ti_, shift, axis=0))

    def fix_step(k, carry):
        zr, zi = _cmul(lr, li, *carry)
        r = rows(k)
        hr_ref[r, :] = hr_ref[r, :] + zr
        hi_ref[r, :] = hi_ref[r, :] + zi
        return zr, zi

    lax.fori_loop(0, nt, fix_step, (cr, ci), unroll=unroll)


def _dot(a, b):
    return jnp.dot(a.astype(BF16), b.astype(BF16), preferred_element_type=F32)


def _s5_forward(u_seg, mb_re, mb_im, mc_re, mc_im, lam_re, lam_im, dvec, name, deps=()):
    l = u_seg.shape[0]
    nb, kin, kst = mb_re.shape

    def body(u_ref, mbr_ref, mbi_ref, mcr_ref, mci_ref, lr_ref, li_ref, d_ref, hr_ref, hi_ref, y_ref):
        u = u_ref[...]
        hr_ref[...] = _dot(u, mbr_ref[...])
        hi_ref[...] = _dot(u, mbi_ref[...])
        _scan_in_place(hr_ref, hi_ref, jnp.broadcast_to(lr_ref[...], (SUBLANES, kst)),
                       jnp.broadcast_to(li_ref[...], (SUBLANES, kst)), False)
        y_ref[...] = (_dot(hr_ref[...], mcr_ref[...]) - _dot(hi_ref[...], mci_ref[...])
                      + d_ref[...] * u.astype(F32))

    act = pl.BlockSpec((l, kin), lambda b: (0, b))
    state = pl.BlockSpec((l, kst), lambda b: (0, b))
    up = pl.BlockSpec((None, kin, kst), lambda b: (b, 0, 0))
    down = pl.BlockSpec((None, kst, kin), lambda b: (b, 0, 0))
    hshape = jax.ShapeDtypeStruct((l, nb * kst), F32)
    return _pallas(
        body, [u_seg, mb_re, mb_im, mc_re, mc_im, lam_re, lam_im, dvec], deps, name=name, grid=(nb,),
        in_specs=[act, up, up, down, down, pl.BlockSpec((1, kst), lambda b: (0, b)),
                  pl.BlockSpec((1, kst), lambda b: (0, b)), pl.BlockSpec((1, kin), lambda b: (0, b))],
        out_specs=[state, state, act],
        out_shape=[hshape, hshape, jax.ShapeDtypeStruct((l, nb * kin), F32)],
        compiler_params=_params("parallel"))


def _dot_ta(a, b):
    return lax.dot_general(a.astype(BF16), b.astype(BF16), (((0,), (0,)), ((), ())), preferred_element_type=F32)


def _dot_nt(a, b):
    return lax.dot_general(a.astype(BF16), b.astype(BF16), (((1,), (1,)), ((), ())), preferred_element_type=F32)


def _s5_backward(dy_seg, u_seg, h_re, h_im, mb_re, mb_im, mc_re, mc_im, lam_re, lam_im_neg, dvec, name):
    l = dy_seg.shape[0]
    nb, kin, kst = mb_re.shape
    nt = l // SUBLANES

    def body(dy_ref, u_ref, hr_ref, hi_ref, mbr_ref, mbi_ref, mcr_ref, mci_ref, lr_ref, li_ref,
             d_ref, du_ref, gcr_ref, gci_ref, gbr_ref, gbi_ref, glr_ref, gli_ref, dsk_ref, qr_ref, qi_ref):
        dy = dy_ref[...]
        qr_ref[...] = _dot_nt(dy, mcr_ref[...])
        qi_ref[...] = -_dot_nt(dy, mci_ref[...])
        _scan_in_place(qr_ref, qi_ref, jnp.broadcast_to(lr_ref[...], (SUBLANES, kst)),
                       jnp.broadcast_to(li_ref[...], (SUBLANES, kst)), True)
        du_ref[...] = _dot_nt(qr_ref[...], mbr_ref[...]) + _dot_nt(qi_ref[...], mbi_ref[...]) + d_ref[...] * dy
        dsk_ref[...] = jnp.sum(dy * _f32(u_ref), axis=0, keepdims=True)
        gcr_ref[...] = _dot_ta(dy, hr_ref[...])
        gci_ref[...] = _dot_ta(dy, hi_ref[...])
        gbr_ref[...] = _dot_ta(u_ref[...], qr_ref[...])
        gbi_ref[...] = _dot_ta(u_ref[...], qi_ref[...])

        row = lax.broadcasted_iota(jnp.int32, (SUBLANES, kst), 0)
        last = pl.ds((nt - 1) * SUBLANES, SUBLANES)
        first = pl.ds(0, SUBLANES)
        pr = jnp.where(row == 0, 0.0, pltpu.roll(hr_ref[last, :], 1, axis=0))
        pi = jnp.where(row == 0, 0.0, pltpu.roll(hi_ref[last, :], 1, axis=0))
        gr, gi = qr_ref[first, :], qi_ref[first, :]

        def step(t, carry):
            acc_r, acc_i = carry
            cur = pl.ds(pl.multiple_of(t * SUBLANES, SUBLANES), SUBLANES)
            prev = pl.ds(pl.multiple_of((t - 1) * SUBLANES, SUBLANES), SUBLANES)
            gr, gi = qr_ref[cur, :], qi_ref[cur, :]
            pr, pi = hr_ref[prev, :], hi_ref[prev, :]
            return acc_r + gr * pr + gi * pi, acc_i + gi * pr - gr * pi

        acc_r, acc_i = lax.fori_loop(1, nt, step, (gr * pr + gi * pi, gi * pr - gr * pi))
        glr_ref[...] = jnp.sum(acc_r, axis=0, keepdims=True)
        gli_ref[...] = jnp.sum(acc_i, axis=0, keepdims=True)

    act = pl.BlockSpec((l, kin), lambda b: (0, b))
    state = pl.BlockSpec((l, kst), lambda b: (0, b))
    up = pl.BlockSpec((None, kin, kst), lambda b: (b, 0, 0))
    down = pl.BlockSpec((None, kst, kin), lambda b: (b, 0, 0))
    vec_st = pl.BlockSpec((1, kst), lambda b: (0, b))
    vec_in = pl.BlockSpec((1, kin), lambda b: (0, b))
    outer = jax.ShapeDtypeStruct((nb, kin, kst), F32)
    lam_shape = jax.ShapeDtypeStruct((1, nb * kst), F32)
    return pl.pallas_call(
        body, name=name, grid=(nb,),
        in_specs=[act, act, state, state, up, up, down, down, vec_st, vec_st, vec_in],
        out_specs=[act, up, up, up, up, vec_st, vec_st, vec_in],
        out_shape=[jax.ShapeDtypeStruct((l, nb * kin), F32), outer, outer, outer, outer, lam_shape, lam_shape,
                   jax.ShapeDtypeStruct((1, nb * kin), F32)],
        scratch_shapes=[pltpu.VMEM((l, kst), F32), pltpu.VMEM((l, kst), F32)],
        compiler_params=_params("parallel"),
    )(dy_seg, u_seg, h_re, h_im, mb_re, mb_im, mc_re, mc_im, lam_re, lam_im_neg, dvec)


def _block_diag(m, nb):
    g, r, s = m.shape
    gb = g // nb
    tiled = jnp.tile(m.reshape(nb, gb * r, s), (1, 1, gb))
    row_group = lax.broadcasted_iota(jnp.int32, tiled.shape, 1) // r
    col_group = lax.broadcasted_iota(jnp.int32, tiled.shape, 2) // s
    return jnp.where(row_group == col_group, tiled, 0.0)


def _block_diag_extract(mat, g, r, s):
    nb = mat.shape[0]
    gb = g // nb
    eye = jnp.eye(gb, dtype=mat.dtype)
    m5 = mat.reshape(nb, gb, r, gb, s) * eye[None, :, None, :, None]
    return jnp.sum(m5, axis=3).reshape(g, r, s)


def _adamw(w, m, v, gslots, name, layer=0, prev=None, deps=()):
    layers, r, c = w.shape
    s = gslots.shape[0]
    tr = _tile(r, max(SUBLANES, 1 << int(math.log2(ADAMW_BLOCK_ELEMS // c))))
    bc1 = 1.0 / (1.0 - ADAM_B1 ** ADAM_STEP)
    bc2 = 1.0 / (1.0 - ADAM_B2 ** ADAM_STEP)

    def body(w_ref, m_ref, v_ref, g_ref, *rest):
        go_ref, d_ref, mo_ref, vo_ref = rest[-4:]
        g = g_ref[0].astype(F32)
        for k in range(1, s):
            g = g + g_ref[k].astype(F32)
        mn = ADAM_B1 * m_ref[...] + (1.0 - ADAM_B1) * g
        vn = ADAM_B2 * v_ref[...] + (1.0 - ADAM_B2) * (g * g)
        go_ref[...] = g
        mo_ref[...] = mn
        vo_ref[...] = vn
        d_ref[...] = -ADAM_LR * ((mn * bc1) / (jnp.sqrt(vn * bc2) + ADAM_EPS) + ADAM_WD * w_ref[...])

    spec = pl.BlockSpec((None, tr, c), lambda i: (layer, i, 0))
    out = jax.ShapeDtypeStruct((layers, r, c), F32)
    in_specs = [spec, spec, spec, pl.BlockSpec((s, tr, c), lambda i: (0, i, 0))]
    args = [w, m, v, gslots]
    aliases = {}
    if prev is not None:
        in_specs += [ANY_SPEC] * 4
        args += list(prev)
        aliases = {4 + q: q for q in range(4)}
    in_specs += [ANY_SPEC] * len(deps)
    args += list(deps)
    return pl.pallas_call(
        body, name=name, grid=(r // tr,), in_specs=in_specs,
        out_specs=[spec] * 4, out_shape=[out] * 4, input_output_aliases=aliases,
        compiler_params=_params("parallel"),
    )(*args)


def _pack(parts):
    flat = jnp.concatenate([p.reshape(-1) for p in parts])
    pad = (-flat.shape[0]) % (PACK_ROWS * LANES)
    return jnp.pad(flat, (0, pad)).reshape(-1, LANES)


def _unpack(packed, shapes):
    flat = packed.reshape(-1)
    out, off = [], 0
    for shp in shapes:
        size = math.prod(shp)
        out.append(flat[off:off + size].reshape(shp))
        off += size
    return out


def kernel(x, norm_g, w_in, conv_w, w_out_a, a_re, a_im, log_dt, b_re, b_im, c_re, c_im, d_skip, w_glu, b_glu, w_out_b, w_o, final_g, loss_target, m_norm_g, m_w_in, m_conv_w, m_w_out_a, m_a_re, m_a_im, m_log_dt, m_b_re, m_b_im, m_c_re, m_c_im, m_d_skip, m_w_glu, m_b_glu, m_w_out_b, m_w_o, m_final_g, v_norm_g, v_w_in, v_conv_w, v_w_out_a, v_a_re, v_a_im, v_log_dt, v_b_re, v_b_im, v_c_re, v_c_im, v_d_skip, v_w_glu, v_b_glu, v_w_out_b, v_w_o, v_final_g):
    depth = norm_g.shape[0]
    l, d = x.shape[1], x.shape[2]
    ws = w_glu.shape[2]
    n_groups, n_state = a_re.shape[1], a_re.shape[2]
    nb = ws // LANES
    assert S5_GROUP == b_re.shape[3] and n_state * S5_GB == 4 * LANES
    u_col, zb_col = 4 * d // ws, 4 * d // ws + 1
    ga_col, gb_col = (4 * d + 2 * ws) // d, (4 * d + 2 * ws) // d + 1
    me = 4 * lax.axis_index("x") + 2 * lax.axis_index("y") + lax.axis_index("c")

    xs = [x[0]]
    tgt = loss_target[0]

    big_names = ("w_in", "w_out_a", "w_glu", "w_out_b", "w_o")
    big = dict(w_in=(w_in, m_w_in, v_w_in), w_out_a=(w_out_a, m_w_out_a, v_w_out_a),
               w_glu=(w_glu, m_w_glu, v_w_glu), w_out_b=(w_out_b, m_w_out_b, v_w_out_b),
               w_o=(w_o, m_w_o, v_w_o))

    all_shards = [[None] * len(big_names) for _ in range(depth)]
    all_shards[0][0] = w_in[0].astype(BF16)

    def shards_bf16(i):
        return all_shards[i]

    main_names = ("a_re", "a_im", "log_dt", "b_re", "b_im", "c_re", "c_im", "d_skip", "b_glu", "conv_w")
    small_w = dict(a_re=(a_re, m_a_re, v_a_re), a_im=(a_im, m_a_im, v_a_im),
                   log_dt=(log_dt, m_log_dt, v_log_dt), b_re=(b_re, m_b_re, v_b_re), b_im=(b_im, m_b_im, v_b_im),
                   c_re=(c_re, m_c_re, v_c_re), c_im=(c_im, m_c_im, v_c_im), d_skip=(d_skip, m_d_skip, v_d_skip),
                   b_glu=(b_glu, m_b_glu, v_b_glu))
    main_shapes = [small_w[k][0].shape for k in main_names[:-1]] + [(depth, 3, d)]
    dc = d // N_DEV
    pad8 = lambda a: jnp.pad(a.reshape(depth * 3, dc), ((0, SUBLANES - depth * 3), (0, 0)))[None]

    def input_only_work(zero):
        for i in range(depth):
            for j, k in enumerate(big_names):
                if all_shards[i][j] is None:
                    all_shards[i][j] = (big[k][0][i] + zero).astype(BF16)
        zeros_conv = jnp.zeros((depth, 3, d), F32) + zero
        zero1 = jnp.zeros((1,), F32) + zero
        main = [_pack([small_w[k][q] for k in main_names[:-1]] + [zeros_conv])[None] for q in range(3)]
        gains = [_pack([g_, f_, zero1])[None]
                 for g_, f_ in ((norm_g, final_g), (m_norm_g, m_final_g), (v_norm_g, v_final_g))]
        return main, gains, [pad8(conv_w), pad8(m_conv_w), pad8(v_conv_w)]

    def gather_start(shards, name, deps=()):
        sems, srcs, lands, token = _exchange_start(
            _plan_gather_chips, shards, [_landing(s_, N_DEV, me) for s_ in shards], f"{name}_start", deps)
        return (name, sems, srcs, lands), token

    def gather_forward(state, after, deps=()):
        name, sems, srcs, lands = state
        _, lands = _exchange_wait(_plan_gather_chips, sems, srcs, lands, after, f"{name}_wait")
        sems, _, lands, token = _exchange_start(_plan_gather_forward, None, lands, f"{name}_forward_start", deps)
        return (name, sems, lands), token

    def gather_finish(state, after):
        name, sems, lands = state
        return _exchange_wait(_plan_gather_forward, sems, None, lands, after, f"{name}_forward_wait")[1]

    def halves_start(shards, name, deps=()):
        srcs = [s_.reshape(2, s_.shape[0] // 2, *s_.shape[1:]) for s_ in shards]
        lands = [lax.dynamic_update_slice(lax.empty((2 * N_DEV,) + h_.shape[1:], h_.dtype), h_, (2 * me, 0, 0))
                 for h_ in srcs]
        sems, srcs, lands, token = _exchange_start(_plan_halves_first, srcs, lands, f"{name}_start", deps)
        return (name, sems, srcs, lands), token

    def halves_second(state, after, deps=()):
        name, sems, srcs, lands = state
        _, lands = _exchange_wait(_plan_halves_first, sems, srcs, lands, after, f"{name}_wait")
        sems, _, lands, token = _exchange_start(_plan_halves_second, None, lands, f"{name}_second_start", deps)
        return (name, sems, lands), token

    def halves_forward(state, after, deps=()):
        name, sems, lands = state
        _, lands = _exchange_wait(_plan_halves_second, sems, None, lands, after, f"{name}_second_wait")
        sems, _, lands, token = _exchange_start(_plan_halves_forward, None, lands, f"{name}_forward_start", deps)
        return (name, sems, lands), token

    def halves_finish(state, after):
        name, sems, lands = state
        lands = _exchange_wait(_plan_halves_forward, sems, None, lands, after, f"{name}_forward_wait")[1]
        return [l_.reshape(N_DEV, 2 * l_.shape[1], *l_.shape[2:]) for l_ in lands]

    conv_shard = jnp.pad(conv_w.reshape(depth * 3, -1), ((0, HALO - depth * 3), (0, 0)))
    w_in_state, token = halves_start([shards_bf16(0)[0], conv_shard], "ag_w_in_0")
    s5 = []
    shape3 = (n_groups, n_state, S5_GROUP)
    for i in range(depth):
        dense_in = (_dense(a_re[i][:, :, None], shape3), _dense(a_im[i][:, :, None], shape3),
                    _dense(log_dt[i][:, None, None], shape3), b_re[i].reshape(-1, LANES), b_im[i].reshape(-1, LANES))
        lbr, lbi = _s5_params(*dense_in, False, f"s5_lam_{i}", deps=(token,))
        s5.append(dict(dense_in=dense_in, lam_re=lbr.reshape(shape3)[:, :, 0].reshape(1, -1),
                       lam_im=lbi.reshape(shape3)[:, :, 0].reshape(1, -1)))
    main_wmv, gains_wmv, conv_wmv = input_only_work(token[0, 0])
    w_in_state, token = halves_second(
        w_in_state, [p[k] for p in s5 for k in ("lam_re", "lam_im")] + main_wmv + gains_wmv + conv_wmv)
    for i in range(depth):
        bbr, bbi = _s5_params(*s5[i]["dense_in"], True, f"s5_input_matrix_{i}", deps=(token,))
        bbr3, bbi3 = bbr.reshape(shape3), bbi.reshape(shape3)
        diag = lambda m: _block_diag(m, nb).astype(BF16)
        s5[i].update(up=(diag(bbr3.transpose(0, 2, 1)), diag(bbi3.transpose(0, 2, 1))),
                     down=(diag((c_re[i] + token[0, 0]).transpose(0, 2, 1)),
                           diag((c_im[i] + token[0, 0]).transpose(0, 2, 1))))
    prologue = [m for p in s5 for k in ("up", "down") for m in p[k]]
    prologue += main_wmv + gains_wmv + conv_wmv + all_shards[0][1:] + [s_ for sh in all_shards[1:] for s_ in sh]
    w_in_state, token = halves_forward(w_in_state, prologue, deps=(token,))
    rest_state, token = halves_start(shards_bf16(0)[1:], "ag_rest_0", deps=(token,))
    next_state = None
    if depth > 1:
        next_state, token = halves_start([shards_bf16(1)[0]], "ag_w_in_1", deps=(token,))

    saved = []
    wg = [None] * depth
    conv_full = None
    for i in range(depth):
        xi = xs[-1]
        h = _rmsnorm_fwd(xi, norm_g[i], f"rmsnorm_fwd_{i}", deps=(token,))
        arrived = halves_finish(w_in_state, [h])
        if i == 0:
            conv_full = arrived[1].transpose(1, 0, 2).reshape(HALO, d)[:depth * 3].reshape(depth, 3, d)
        conv8 = jnp.pad(conv_full[i], ((0, SUBLANES - 3), (0, 0)))
        proj = _mm_win_fwd(h, arrived[0], f"mm_proj_{i}")
        rest_state, token = halves_second(rest_state, [proj])
        if next_state is not None:
            next_state, token = halves_second(next_state, [token], deps=(token,))
        u_seg = _to_segments(proj[:, 4 * d:4 * d + ws])
        h_re, h_im, y_seg = _s5_forward(u_seg, *s5[i]["up"], *s5[i]["down"], s5[i]["lam_re"], s5[i]["lam_im"],
                                        d_skip[i].reshape(1, ws), f"s5_forward_{i}", deps=(token,))
        rest_state, token = halves_forward(rest_state, [y_seg])
        pa = _branch_a_fwd(proj, conv8, d, f"branch_a_fwd_{i}", deps=(token,))
        rest = halves_finish(rest_state, [pa])
        wg[i] = g = dict(w_in=arrived[0], w_a=rest[0].reshape(d, d), w_glu=rest[1].reshape(ws, ws),
                         w_b=rest[2], w_o=rest[3].reshape(d, d))
        ya = _mm(pa, g["w_a"], name=f"mm_ya_{i}", out_dtype=BF16)
        y = _from_segments(y_seg)
        yg = _gelu_cast(y, f"gelu_{i}")
        gl = _mm(yg, g["w_glu"], name=f"mm_glu_{i}", out_dtype=BF16)
        pb = _glu_post(y, gl, proj, b_glu[i], zb_col, f"glu_post_{i}")
        w_b2d = g["w_b"].transpose(1, 0, 2).reshape(ws, d)
        yb = _mm(pb, w_b2d, name=f"mm_yb_{i}", out_dtype=BF16)
        mrg = _merge_fwd(proj, ya, yb, d, ga_col, gb_col, f"merge_fwd_{i}")
        deps = ()
        if i + 1 < depth:
            w_in_state, token = halves_forward(next_state, [mrg])
            rest_state, token = halves_start(shards_bf16(i + 1)[1:], f"ag_rest_{i + 1}", deps=(token,))
            next_state = None
            if i + 2 < depth:
                next_state, token = halves_start([shards_bf16(i + 2)[0]], f"ag_w_in_{i + 2}", deps=(token,))
            deps = (token,)
        xs.append(_mm(mrg, g["w_o"], name=f"mm_out_{i}", add=xi, deps=deps))
        saved.append(dict(h=h, proj=proj, pa=pa, ya=ya, yb=yb, y=y, yg=yg, gl=gl, pb=pb, mrg=mrg,
                          u_seg=u_seg, h_re=h_re, h_im=h_im, conv8=conv8, w_b2d=w_b2d))

    dx, g_final, loss_part = _final_loss(xs[-1], final_g, tgt, "final_loss")

    rs_pending = []
    small = {k: [None] * depth for k in ("norm_g", "a_re", "a_im", "log_dt", "b_re", "b_im", "c_re", "c_im",
                                         "d_skip", "b_glu", "conv_w")}

    my_chip = 2 * lax.axis_index("x") + lax.axis_index("y")

    def reduce_on_chip(pieces, tag):
        lands = [lax.empty((4,) + p.shape[1:], p.dtype) for p in pieces]
        sems, srcs, lands, token = _exchange_start(_plan_reduce_sibling, pieces, lands, f"rs_sibling_start_{tag}")
        return (sems, srcs, lands), token

    def reduce_across_chips(names_, state, layer, tag, after):
        sems, srcs, lands = state
        srcs, lands = _exchange_wait(_plan_reduce_sibling, sems, srcs, lands, after, f"rs_sibling_wait_{tag}")
        sums = [_chip_sums(p, l_, f"chip_sum_{k}_{layer}") for k, p, l_ in zip(names_, srcs, lands)]
        lands = [_landing(lax.dynamic_index_in_dim(s_, my_chip, 0, keepdims=False), 4, my_chip) for s_ in sums]
        sems, srcs, lands, token = _exchange_start(_plan_reduce_chips, sums, lands, f"rs_chips_start_{tag}")
        rs_pending.append((names_, layer, sems, srcs, lands, f"rs_chips_wait_{tag}"))
        return token

    for i in reversed(range(depth)):
        s, g = saved[i], wg[i]
        proj = s["proj"]
        dxo_b = dx.astype(BF16)
        dm = _mm(dxo_b, g["w_o"], name=f"mm_dm_{i}", nt=True, out_dtype=BF16)
        gw_o = _mm(s["mrg"], dxo_b, ta=True, name=f"mm_gw_o_{i}", out_dtype=BF16)
        dy2, dproj = _merge_bwd(proj, s["ya"], s["yb"], dm, d, ga_col, f"merge_bwd_{i}")
        dya, dyb = dy2[0], dy2[1]
        dpa = _mm(dya, g["w_a"], name=f"mm_dpa_{i}", nt=True, out_dtype=BF16)
        gw_a = _mm(s["pa"], dya, ta=True, name=f"mm_gw_a_{i}", out_dtype=BF16)
        dpb = _mm(dyb, s["w_b2d"], name=f"mm_dpb_{i}", nt=True, out_dtype=BF16)
        gw_b = _mm(s["pb"], dyb, ta=True, name=f"mm_gw_b_{i}", split_n=N_DEV, out_dtype=BF16)
        dproj, dw0, dw1, dw2 = _branch_a_bwd(proj, dpa, s["conv8"], dproj, d, f"branch_a_bwd_{i}")
        small["conv_w"][i] = jnp.concatenate([dw0, dw1, dw2], axis=0)
        dproj, dgl, t1, db_glu = _glu_bwd1(s["y"], s["gl"], proj, b_glu[i], dpb, dproj, zb_col, f"glu_bwd1_{i}")
        small["b_glu"][i] = db_glu.reshape(ws)
        dyg2 = _mm(dgl, g["w_glu"], name=f"mm_dyg_{i}", nt=True, out_dtype=BF16)
        gw_glu = _mm(s["yg"], dgl, ta=True, name=f"mm_gw_glu_{i}", out_dtype=BF16)
        small_names_ = ("w_out_a", "w_glu", "w_out_b", "w_o")
        state, token = reduce_on_chip(
            [gw_a.reshape(N_DEV, d // N_DEV, d), gw_glu.reshape(N_DEV, ws // N_DEV, ws), gw_b,
             gw_o.reshape(N_DEV, d // N_DEV, d)], f"small_{i}")
        dy = _glu_bwd2(s["y"], t1, dyg2, f"glu_bwd2_{i}", deps=(token,))
        dy_seg = _to_segments(dy)
        u_seg = s["u_seg"]
        du_seg, gc_re, gc_im, gbb_re, gbb_im, glam_re, glam_im, dskip = _s5_backward(
            dy_seg, u_seg, s["h_re"], s["h_im"], *s5[i]["up"], *s5[i]["down"],
            s5[i]["lam_re"], -s5[i]["lam_im"], d_skip[i].reshape(1, ws), f"s5_backward_{i}")
        token = reduce_across_chips(small_names_, state, i, f"small_{i}", [du_seg])
        dproj = _write_cols(dproj, _from_segments(du_seg), u_col, f"write_du_{i}")
        gw_in = _mm(s["h"], dproj, ta=True, name=f"mm_gw_in_{i}", split_n=N_DEV, tm=1024,
                    out_dtype=BF16, deps=(token,))
        state, token = reduce_on_chip([gw_in], f"w_in_{i}")
        small["d_skip"][i] = dskip.reshape(n_groups, S5_GROUP)
        small["c_re"][i] = _block_diag_extract(gc_re, n_groups, S5_GROUP, n_state)
        small["c_im"][i] = -_block_diag_extract(gc_im, n_groups, S5_GROUP, n_state)
        gbb_re = _block_diag_extract(gbb_re, n_groups, S5_GROUP, n_state).transpose(0, 2, 1)
        gbb_im = _block_diag_extract(gbb_im, n_groups, S5_GROUP, n_state).transpose(0, 2, 1)
        gar, gai, gdt, gbr, gbi = _s5_params_bwd(
            *s5[i]["dense_in"], _dense(glam_re.reshape(n_groups, n_state, 1), shape3),
            _dense(glam_im.reshape(n_groups, n_state, 1), shape3),
            gbb_re.reshape(-1, LANES), gbb_im.reshape(-1, LANES), n_groups, f"s5_params_bwd_{i}", deps=(token,))
        small["a_re"][i] = gar.reshape(shape3)[:, :, 0]
        small["a_im"][i] = gai.reshape(shape3)[:, :, 0]
        small["log_dt"][i] = gdt[:, 0]
        small["b_re"][i] = gbr.reshape(shape3)
        small["b_im"][i] = gbi.reshape(shape3)
        if i == 0:
            part = {k: jnp.stack(small[k]) for k in main_names}
            main_state, token = gather_start([_pack([part[k] for k in main_names]).astype(BF16)], "ag_small")
            token = reduce_across_chips(("w_in",), state, i, f"w_in_{i}", [token])
            main_state, token = gather_forward(main_state, [token])
            dh = _mm_win_bwd(dproj, g["w_in"], f"mm_dh_{i}", deps=(token,))
            main_slots = gather_finish(main_state, [dh])[0]
            deps = ()
        else:
            dh = _mm_win_bwd(dproj, g["w_in"], f"mm_dh_{i}", deps=(gar,))
            deps = (reduce_across_chips(("w_in",), state, i, f"w_in_{i}", [dh]),)
        dx, dng = _rmsnorm_bwd(xs[i], norm_g[i], dh, dx, f"rmsnorm_bwd_{i}", deps=deps)
        small["norm_g"][i] = dng.reshape(d)

    results = {}

    gain_grads = jnp.concatenate([jnp.stack(small["norm_g"]).reshape(-1), g_final.reshape(d)])
    gain_shapes = [(depth, d), (d,), (1,)]
    gains_state, token = gather_start([_pack([gain_grads, loss_part[0, :1]])], "ag_gains")

    sres = [_unpack(p[0], main_shapes) for p in _adamw(*main_wmv, main_slots, "adamw_small", deps=(token,))]
    for j, k in enumerate(main_names[:-1]):
        results[k] = [sres[q][j] for q in range(4)]
    gconv = lax.dynamic_slice_in_dim(sres[0][-1], me * dc, dc, axis=2)
    cres = _adamw(*conv_wmv, pad8(gconv), "adamw_conv_w")
    results["conv_w"] = [r_[0, :depth * 3].reshape(depth, 3, dc) for r_ in cres]

    after = [cres[0]]
    for names_, layer, sems, srcs, lands, wait_name in rs_pending:
        _, slots = _exchange_wait(_plan_reduce_chips, sems, srcs, lands, after, wait_name)
        for k, land in zip(names_, slots):
            w_, m_, v_ = big[k]
            results[k] = _adamw(w_, m_, v_, land, f"adamw_{k}_{layer}", layer=layer, prev=results.get(k))
            after = [results[k][0]]

    gains_state, token = gather_forward(gains_state, after)
    gpack = gather_finish(gains_state, [token])[0]
    gres = [_unpack(p[0], gain_shapes) for p in _adamw(*gains_wmv, gpack, "adamw_gains")]
    results["norm_g"] = [gres[q][0] for q in range(4)]
    results["final_g"] = [gres[q][1] for q in range(4)]
    loss = gres[0][2][0]

    names = ("norm_g", "w_in", "conv_w", "w_out_a", "a_re", "a_im", "log_dt", "b_re", "b_im", "c_re", "c_im",
             "d_skip", "w_glu", "b_glu", "w_out_b", "w_o", "final_g")
    outs = [loss, dx[None]]
    for q in range(4):
        outs += [results[k][q] for k in names]
    return tuple(outs)
```

```python
import functools
import math

import jax
import jax.numpy as jnp
from jax import lax
from jax.experimental import pallas as pl
from jax.experimental.pallas import tpu as pltpu

F32 = jnp.float32
BF16 = jnp.bfloat16
HIGHEST = lax.Precision.HIGHEST

N_DEV = 8
LANES = 128
SUBLANES = 8
VMEM_LIMIT_BYTES = 56 * 1024 * 1024

RMS_EPS = 1e-6
ADAM_LR = 0.001
ADAM_B1 = 0.9
ADAM_B2 = 0.999
ADAM_EPS = 1e-08
ADAM_WD = 0.01
ADAM_STEP = 10
GELU_C0 = math.sqrt(2.0 / math.pi)
GELU_C1 = 0.044715

ADAMW_BLOCK_ELEMS = 1 << 17
PACK_ROWS = 512

S5_GROUP = 16
S5_GB = LANES // S5_GROUP


def _params(*semantics):
    return pltpu.CompilerParams(dimension_semantics=semantics, vmem_limit_bytes=VMEM_LIMIT_BYTES)


ANY_SPEC = pl.BlockSpec(memory_space=pl.ANY)


def _pallas(body, args, deps=(), *, in_specs, **kwargs):
    deps = tuple(deps)
    if not deps:
        return pl.pallas_call(body, in_specs=in_specs, **kwargs)(*args)

    def body_after(*refs):
        body(*refs[len(deps):])

    return pl.pallas_call(body_after, in_specs=[ANY_SPEC] * len(deps) + list(in_specs), **kwargs)(*deps, *args)


def _tile(n, pref):
    t = min(n, pref)
    while n % t:
        assert t % 2 == 0, (n, pref)
        t //= 2
    return t


def _sigmoid(z):
    return 1.0 / (1.0 + jnp.exp(-z))


def _gelu(y):
    return 0.5 * y * (1.0 + jnp.tanh(GELU_C0 * (y + GELU_C1 * y * y * y)))


def _gelu_grad(y):
    t = jnp.tanh(GELU_C0 * (y + GELU_C1 * y * y * y))
    return 0.5 * (1.0 + t) + 0.5 * y * (1.0 - t * t) * GELU_C0 * (1.0 + 3.0 * GELU_C1 * y * y)


HBM_SPEC = pl.BlockSpec(memory_space=pltpu.HBM)
SEM_SPEC = pl.BlockSpec(memory_space=pltpu.SEMAPHORE)
DATAFLOW_EFFECT = pltpu.SideEffectType.DATAFLOW_SIDE_EFFECTING
OTHER_CHIPS = (2, 4, 6)


def _flip(pos, mask):
    x, y, c = pos
    return x ^ ((mask >> 2) & 1), y ^ ((mask >> 1) & 1), c ^ (mask & 1)


def _dev(pos):
    return 4 * pos[0] + 2 * pos[1] + pos[2]


def _chip(pos):
    return 2 * pos[0] + pos[1]


def _plan_gather_chips(me):
    return [(_flip(me, k), None, _dev(me), _dev(_flip(me, k))) for k in (1,) + OTHER_CHIPS]


def _plan_gather_forward(me):
    sib = _flip(me, 1)
    return [(sib, _dev(_flip(me, k)), _dev(_flip(me, k)), _dev(_flip(sib, k))) for k in OTHER_CHIPS]


def _plan_reduce_sibling(me):
    sib = _flip(me, 1)
    return [(sib, 2 * q + sib[2], q, q) for q in range(4)]


def _plan_reduce_chips(me):
    return [(_flip(me, k), _chip(_flip(me, k)), _chip(me), _chip(_flip(me, k))) for k in OTHER_CHIPS]


X_CHIP, Y_CHIP, XY_CHIP = 4, 2, 6


def _plan_halves_first(me):
    sib, xn, yn = _flip(me, 1), _flip(me, X_CHIP), _flip(me, Y_CHIP)
    return [(sib, 0, 2 * _dev(me), 2 * _dev(sib)), (sib, 1, 2 * _dev(me) + 1, 2 * _dev(sib) + 1),
            (xn, 0, 2 * _dev(me), 2 * _dev(xn)), (yn, 1, 2 * _dev(me) + 1, 2 * _dev(yn) + 1)]


def _plan_halves_second(me):
    xn, yn, dg = _flip(me, X_CHIP), _flip(me, Y_CHIP), _flip(me, XY_CHIP)
    return [(yn, 2 * _dev(me), 2 * _dev(me), 2 * _dev(yn)), (yn, 2 * _dev(xn), 2 * _dev(xn), 2 * _dev(dg)),
            (xn, 2 * _dev(me) + 1, 2 * _dev(me) + 1, 2 * _dev(xn) + 1),
            (xn, 2 * _dev(yn) + 1, 2 * _dev(yn) + 1, 2 * _dev(dg) + 1)]


def _plan_halves_forward(me):
    sib = _flip(me, 1)
    return [(sib, 2 * _dev(_flip(me, k)) + h, 2 * _dev(_flip(me, k)) + h, 2 * _dev(_flip(sib, k)) + h)
            for k in OTHER_CHIPS for h in (0, 1)]


PLAN_COPIES = {_plan_gather_chips: 4, _plan_gather_forward: 3, _plan_reduce_sibling: 4, _plan_reduce_chips: 3,
               _plan_halves_first: 4, _plan_halves_second: 4, _plan_halves_forward: 6}


def _exchange_copies(plan, src_refs, land_refs, send_sems, recv_sems, incoming=True):
    me = (lax.axis_index("x"), lax.axis_index("y"), lax.axis_index("c"))
    pairs = []
    for b, (src_ref, land_ref) in enumerate(zip(src_refs, land_refs)):
        for j, (peer, src_slot, there, here) in enumerate(plan(me)):
            sem = b * PLAN_COPIES[plan] + j
            src = src_ref if src_slot is None else src_ref.at[src_slot]
            out = pltpu.make_async_remote_copy(
                src_ref=src, dst_ref=land_ref.at[there], send_sem=send_sems.at[sem], recv_sem=recv_sems.at[sem],
                device_id=peer, device_id_type=pl.DeviceIdType.MESH)
            inc = pltpu.make_async_remote_copy(
                src_ref=src, dst_ref=land_ref.at[here], send_sem=send_sems.at[sem], recv_sem=recv_sems.at[sem],
                device_id=peer, device_id_type=pl.DeviceIdType.MESH) if incoming else None
            pairs.append((out, inc))
    return pairs


def _exchange_start(plan, srcs, lands, name, deps=()):
    srcs = [] if srcs is None else list(srcs)
    ns, n, nd = len(srcs), len(lands), len(deps)

    def body(*refs):
        land_refs = refs[ns:ns + n]
        sems_at = ns + n + nd
        pairs = _exchange_copies(plan, refs[:ns] if ns else land_refs, land_refs, refs[sems_at], refs[sems_at + 1],
                                 incoming=False)
        for out, _ in pairs:
            out.start()
        token = refs[-1]
        token[...] = jnp.zeros_like(token)

    sems = pltpu.SemaphoreType.DMA((PLAN_COPIES[plan] * n,))
    bufs = srcs + list(lands)
    outs = pl.pallas_call(
        body, name=name,
        out_shape=(sems, sems, *[pltpu.HBM(a.shape, a.dtype) for a in bufs],
                   jax.ShapeDtypeStruct((SUBLANES, LANES), F32)),
        in_specs=[HBM_SPEC] * (ns + n) + [ANY_SPEC] * nd,
        out_specs=(SEM_SPEC, SEM_SPEC, *[HBM_SPEC] * (ns + n), pl.BlockSpec(memory_space=pltpu.VMEM)),
        input_output_aliases={i: 2 + i for i in range(ns + n)},
        compiler_params=pltpu.CompilerParams(has_side_effects=DATAFLOW_EFFECT),
    )(*[pltpu.with_memory_space_constraint(a, pltpu.HBM) for a in bufs], *deps)
    return (outs[0], outs[1]), (outs[2:2 + ns] if ns else None), outs[2 + ns:2 + ns + n], outs[-1]


def _exchange_wait(plan, sems, srcs, lands, after, name):
    srcs = [] if srcs is None else list(srcs)
    ns, n = len(srcs), len(lands)

    def body(*refs):
        land_refs = refs[ns:ns + n]
        pairs = _exchange_copies(plan, refs[:ns] if ns else land_refs, land_refs, refs[ns + n], refs[ns + n + 1])
        for out, inc in pairs:
            out.wait_send()
            inc.wait_recv()

    bufs = srcs + list(lands)
    outs = pl.pallas_call(
        body, name=name,
        out_shape=[pltpu.HBM(a.shape, a.dtype) for a in bufs],
        in_specs=[HBM_SPEC] * (ns + n) + [SEM_SPEC, SEM_SPEC] + [ANY_SPEC] * len(after),
        out_specs=[HBM_SPEC] * (ns + n),
        input_output_aliases={i: i for i in range(ns + n)},
        compiler_params=pltpu.CompilerParams(has_side_effects=DATAFLOW_EFFECT),
    )(*bufs, sems[0], sems[1], *after)
    return outs[:ns], outs[ns:]


def _landing(own, slots, slot):
    land = lax.empty((slots,) + own.shape, own.dtype)
    return lax.dynamic_update_slice(land, own[None], (slot,) + (0,) * own.ndim)


def _chip_sums(pieces, land, name):
    _, r, c_ = land.shape
    tr = _tile(r, max(2 * SUBLANES, 1 << int(math.log2(4 * ADAMW_BLOCK_ELEMS // c_))))

    def body(core_ref, p_ref, l_ref, o_ref):
        o_ref[...] = (p_ref[...].astype(F32) + l_ref[...].astype(F32)).astype(o_ref.dtype)

    spec = pl.BlockSpec((None, tr, c_), lambda q, i, core: (q, i, 0))
    return pl.pallas_call(
        body, name=name,
        grid_spec=pltpu.PrefetchScalarGridSpec(
            num_scalar_prefetch=1, grid=(4, r // tr),
            in_specs=[pl.BlockSpec((None, tr, c_), lambda q, i, core: (2 * q + core[0], i, 0)), spec],
            out_specs=spec),
        out_shape=jax.ShapeDtypeStruct(land.shape, land.dtype),
        compiler_params=_params("parallel", "parallel"),
    )(lax.axis_index("c").reshape(1), pieces, land)


def _mm(a, b, *, name, nt=False, ta=False, out_dtype=F32, add=None, split_n=None, tm=512, tn=1024, deps=()):
    k, m = a.shape if ta else a.shape[::-1]
    n = b.shape[0] if nt else b.shape[1]
    tm = _tile(m, tm)
    tn = n // split_n if split_n else _tile(n, tn)
    dims = (((0 if ta else 1,), (1 if nt else 0,)), ((), ()))

    def body(*refs):
        a_ref, b_ref = refs[0], refs[1]
        o_ref = refs[-1]
        acc = lax.dot_general(a_ref[...], b_ref[...], dims, preferred_element_type=F32)
        if add is not None:
            acc = acc + refs[2][...]
        o_ref[...] = acc.astype(o_ref.dtype)

    in_specs = [pl.BlockSpec((k, tm), lambda i, j: (0, i)) if ta else pl.BlockSpec((tm, k), lambda i, j: (i, 0)),
                pl.BlockSpec((tn, k), lambda i, j: (j, 0)) if nt
                else pl.BlockSpec((k, tn), lambda i, j: (0, j))]
    args = [a, b]
    if add is not None:
        in_specs.append(pl.BlockSpec((tm, tn), lambda i, j: (i, j)))
        args.append(add)
    if split_n:
        out_shape = jax.ShapeDtypeStruct((split_n, m, tn), out_dtype)
        out_spec = pl.BlockSpec((None, tm, tn), lambda i, j: (j, i, 0))
    else:
        out_shape = jax.ShapeDtypeStruct((m, n), out_dtype)
        out_spec = pl.BlockSpec((tm, tn), lambda i, j: (i, j))
    return _pallas(
        body, args, deps, name=name, grid=(m // tm, n // tn), in_specs=in_specs, out_specs=out_spec,
        out_shape=out_shape, compiler_params=_params("parallel", "parallel"))


def _mm_win_fwd(h, w_g, name, deps=()):
    m, k = h.shape
    nj = w_g.shape[2]
    tm = _tile(m, 512)

    def body(a_ref, b_ref, o_ref):
        o_ref[...] = jnp.dot(a_ref[...], b_ref[...], preferred_element_type=F32).astype(o_ref.dtype)

    return _pallas(
        body, [h, w_g], deps, name=name, grid=(N_DEV, m // tm),
        in_specs=[pl.BlockSpec((tm, k), lambda j, i: (i, 0)),
                  pl.BlockSpec((None, k, nj), lambda j, i: (j, 0, 0))],
        out_specs=pl.BlockSpec((tm, nj), lambda j, i: (i, j)),
        out_shape=jax.ShapeDtypeStruct((m, N_DEV * nj), BF16),
        compiler_params=_params("parallel", "parallel"))


def _mm_win_bwd(dproj, w_g, name, deps=()):
    m = dproj.shape[0]
    d, nj = w_g.shape[1], w_g.shape[2]
    tm = _tile(m, 512)
    tn = _tile(d, 1024)

    per_step = 2

    def body(a_ref, b_ref, o_ref, acc_ref):
        j = pl.program_id(2)

        @pl.when(j == 0)
        def _():
            acc_ref[...] = jnp.zeros_like(acc_ref)

        part = None
        for k in range(per_step):
            term = lax.dot_general(a_ref[:, k * nj:(k + 1) * nj], b_ref[k], (((1,), (1,)), ((), ())),
                                   preferred_element_type=F32)
            part = term if part is None else part + term
        acc_ref[...] += part

        @pl.when(j == N_DEV // per_step - 1)
        def _():
            o_ref[...] = acc_ref[...]

    return _pallas(
        body, [dproj, w_g], deps, name=name, grid=(m // tm, d // tn, N_DEV // per_step),
        in_specs=[pl.BlockSpec((tm, per_step * nj), lambda i, n, j: (i, j)),
                  pl.BlockSpec((per_step, tn, nj), lambda i, n, j: (j, n, 0))],
        out_specs=pl.BlockSpec((tm, tn), lambda i, n, j: (i, n)),
        out_shape=jax.ShapeDtypeStruct((m, d), F32),
        scratch_shapes=[pltpu.VMEM((tm, tn), F32)],
        compiler_params=_params("parallel", "parallel", "arbitrary"))


def _row_spec(tr, w, col):
    return pl.BlockSpec((tr, w), lambda i: (i, col))


def _full_spec(shape):
    return pl.BlockSpec(shape, lambda i: (0,) * len(shape))


def _rmsnorm_fwd(x, g, name, deps=()):
    l, d = x.shape
    tr = _tile(l, 256)

    def body(x_ref, g_ref, o_ref):
        xv = x_ref[...]
        rstd = lax.rsqrt(jnp.mean(xv * xv, axis=-1, keepdims=True) + RMS_EPS)
        o_ref[...] = (xv * rstd * g_ref[...]).astype(o_ref.dtype)

    return _pallas(
        body, [x, g.reshape(1, d)], deps, name=name, grid=(l // tr,),
        in_specs=[_row_spec(tr, d, 0), _full_spec((1, d))],
        out_specs=_row_spec(tr, d, 0),
        out_shape=jax.ShapeDtypeStruct((l, d), BF16),
        compiler_params=_params("parallel"))


def _rmsnorm_bwd(x, g, dh, dxo, name, deps=()):
    l, d = x.shape
    tr = _tile(l, 256)

    def body(x_ref, g_ref, dh_ref, dxo_ref, dx_ref, dg_ref):
        xv = x_ref[...]
        rstd = lax.rsqrt(jnp.mean(xv * xv, axis=-1, keepdims=True) + RMS_EPS)
        dhv = dh_ref[...]
        gdy = dhv * g_ref[...]
        dot = jnp.mean(gdy * xv, axis=-1, keepdims=True)
        dx_ref[...] = dxo_ref[...] + rstd * gdy - xv * (rstd * rstd * rstd * dot)

        @pl.when(pl.program_id(0) == 0)
        def _():
            dg_ref[...] = jnp.zeros_like(dg_ref)

        dg_ref[...] += jnp.sum(dhv * xv * rstd, axis=0, keepdims=True)

    return _pallas(
        body, [x, g.reshape(1, d), dh, dxo], deps, name=name, grid=(l // tr,),
        in_specs=[_row_spec(tr, d, 0), _full_spec((1, d)), _row_spec(tr, d, 0), _row_spec(tr, d, 0)],
        out_specs=[_row_spec(tr, d, 0), _full_spec((1, d))],
        out_shape=[jax.ShapeDtypeStruct((l, d), F32), jax.ShapeDtypeStruct((1, d), F32)],
        compiler_params=_params("arbitrary"))


def _final_loss(x, g, tgt, name):
    l, d = x.shape
    tr = _tile(l, 256)

    def body(x_ref, g_ref, t_ref, dx_ref, dg_ref, loss_ref):
        xv = x_ref[...]
        gv = g_ref[...]
        rstd = lax.rsqrt(jnp.mean(xv * xv, axis=-1, keepdims=True) + RMS_EPS)
        xn = xv * rstd
        err = xn * gv - t_ref[...]
        dy = err * (1.0 / d)
        gdy = dy * gv
        dot = jnp.mean(gdy * xv, axis=-1, keepdims=True)
        dx_ref[...] = rstd * gdy - xv * (rstd * rstd * rstd * dot)

        @pl.when(pl.program_id(0) == 0)
        def _():
            dg_ref[...] = jnp.zeros_like(dg_ref)
            loss_ref[...] = jnp.zeros_like(loss_ref)

        dg_ref[...] += jnp.sum(dy * xn, axis=0, keepdims=True)
        loss_ref[...] += (0.5 / d) * jnp.sum(err * err)

    return pl.pallas_call(
        body, name=name, grid=(l // tr,),
        in_specs=[_row_spec(tr, d, 0), _full_spec((1, d)), _row_spec(tr, d, 0)],
        out_specs=[_row_spec(tr, d, 0), _full_spec((1, d)), _full_spec((SUBLANES, LANES))],
        out_shape=[jax.ShapeDtypeStruct((l, d), F32), jax.ShapeDtypeStruct((1, d), F32),
                   jax.ShapeDtypeStruct((SUBLANES, LANES), F32)],
        compiler_params=_params("arbitrary"),
    )(x, g.reshape(1, d), tgt)


HALO = 2 * SUBLANES


def _halo_spec(tr, w, col, nblk, before):
    step = tr // HALO
    if before:
        return pl.BlockSpec((HALO, w), lambda i: (jnp.maximum(i * step - 1, 0), col))
    return pl.BlockSpec((HALO, w), lambda i: (jnp.minimum((i + 1) * step, nblk - 1), col))


def _shift_down(cur, before, k):
    ext = jnp.concatenate([before, cur], axis=0)
    return pltpu.roll(ext, k, axis=0)[HALO:, :]


def _shift_up(cur, after, k):
    tr = cur.shape[0]
    ext = jnp.concatenate([cur, after], axis=0)
    return pltpu.roll(ext, tr + HALO - k, axis=0)[:tr, :]


def _f32(ref):
    return ref[...].astype(F32)


def _branch_a_fwd(proj, conv_w, d, name, deps=()):
    l = proj.shape[0]
    tr = _tile(l, 256)
    nblk8 = l // HALO

    def body(v_ref, bg_ref, cg_ref, za_ref, vh_ref, cgh_ref, w_ref, o_ref):
        first = pl.program_id(0) == 0
        cv = _f32(cg_ref) * _f32(v_ref)
        cvh = jnp.where(first, 0.0, _f32(cgh_ref) * _f32(vh_ref))
        w0, w1, w2 = w_ref[0:1, :], w_ref[1:2, :], w_ref[2:3, :]
        q = w2 * cv + w1 * _shift_down(cv, cvh, 1) + w0 * _shift_down(cv, cvh, 2)
        za = _f32(za_ref)
        o_ref[...] = (_f32(bg_ref) * q * (za * _sigmoid(za))).astype(o_ref.dtype)

    return _pallas(
        body, [proj, proj, proj, proj, proj, proj, conv_w], deps, name=name, grid=(l // tr,),
        in_specs=[_row_spec(tr, d, 0), _row_spec(tr, d, 1), _row_spec(tr, d, 2), _row_spec(tr, d, 3),
                  _halo_spec(tr, d, 0, nblk8, True), _halo_spec(tr, d, 2, nblk8, True),
                  _full_spec((SUBLANES, d))],
        out_specs=_row_spec(tr, d, 0),
        out_shape=jax.ShapeDtypeStruct((l, d), BF16),
        compiler_params=_params("parallel"))


def _branch_a_bwd(proj, dpa, conv_w, dproj, d, name):
    l = proj.shape[0]
    tr = _tile(l, 128)
    nblk8 = l // HALO
    ntiles = l // tr

    def body(v_ref, bg_ref, cg_ref, za_ref, dpa_ref, vh_ref, cgh_ref, bgn_ref, zan_ref, dpan_ref,
             w_ref, _, o_ref, dw0_ref, dw1_ref, dw2_ref):
        dv_ref, dbg_ref, dcg_ref, dza_ref = [o_ref.at[:, pl.ds(k * d, d)] for k in range(4)]
        i = pl.program_id(0)
        v, bg, cg, za, dpa_v = _f32(v_ref), _f32(bg_ref), _f32(cg_ref), _f32(za_ref), _f32(dpa_ref)
        w0, w1, w2 = w_ref[0:1, :], w_ref[1:2, :], w_ref[2:3, :]
        cv = cg * v
        cvh = jnp.where(i == 0, 0.0, _f32(cgh_ref) * _f32(vh_ref))
        cv1 = _shift_down(cv, cvh, 1)
        cv2 = _shift_down(cv, cvh, 2)
        q = w2 * cv + w1 * cv1 + w0 * cv2
        sg = _sigmoid(za)
        s = za * sg
        dbg_ref[...] = (dpa_v * q * s).astype(dbg_ref.dtype)
        dza_ref[...] = (dpa_v * bg * q * (sg * (1.0 + za * (1.0 - sg)))).astype(dza_ref.dtype)
        dq = dpa_v * bg * s
        zan = _f32(zan_ref)
        dqn = jnp.where(i == ntiles - 1, 0.0, _f32(dpan_ref) * _f32(bgn_ref) * (zan * _sigmoid(zan)))
        dcv = w2 * dq + w1 * _shift_up(dq, dqn, 1) + w0 * _shift_up(dq, dqn, 2)
        dcg_ref[...] = (dcv * v).astype(dcg_ref.dtype)
        dv_ref[...] = (dcv * cg).astype(dv_ref.dtype)

        @pl.when(i == 0)
        def _():
            dw0_ref[...] = jnp.zeros_like(dw0_ref)
            dw1_ref[...] = jnp.zeros_like(dw1_ref)
            dw2_ref[...] = jnp.zeros_like(dw2_ref)

        dw0_ref[...] += jnp.sum(dq * cv2, axis=0, keepdims=True)
        dw1_ref[...] += jnp.sum(dq * cv1, axis=0, keepdims=True)
        dw2_ref[...] += jnp.sum(dq * cv, axis=0, keepdims=True)

    wsum = jax.ShapeDtypeStruct((1, d), F32)
    return pl.pallas_call(
        body, name=name, grid=(ntiles,),
        in_specs=[_row_spec(tr, d, 0), _row_spec(tr, d, 1), _row_spec(tr, d, 2), _row_spec(tr, d, 3),
                  _row_spec(tr, d, 0),
                  _halo_spec(tr, d, 0, nblk8, True), _halo_spec(tr, d, 2, nblk8, True),
                  _halo_spec(tr, d, 1, nblk8, False), _halo_spec(tr, d, 3, nblk8, False),
                  _halo_spec(tr, d, 0, nblk8, False),
                  _full_spec((SUBLANES, d)), ANY_SPEC],
        out_specs=[_row_spec(tr, 4 * d, 0)] + [_full_spec((1, d))] * 3,
        out_shape=[jax.ShapeDtypeStruct(dproj.shape, dproj.dtype)] + [wsum] * 3,
        input_output_aliases={11: 0},
        compiler_params=_params("arbitrary"),
    )(proj, proj, proj, proj, dpa, proj, proj, proj, proj, dpa, conv_w, dproj)


def _gelu_cast(y, name):
    l, w = y.shape
    tr = _tile(l, 512)

    def body(y_ref, o_ref):
        o_ref[...] = _gelu(y_ref[...]).astype(o_ref.dtype)

    return pl.pallas_call(
        body, name=name, grid=(l // tr,), in_specs=[_row_spec(tr, w, 0)],
        out_specs=_row_spec(tr, w, 0), out_shape=jax.ShapeDtypeStruct((l, w), BF16),
        compiler_params=_params("parallel"),
    )(y)


def _glu_post(y, gl, proj, b_glu, zb_col, name):
    l, w = y.shape
    tr = _tile(l, 512)

    def body(y_ref, gl_ref, zb_ref, b_ref, o_ref):
        zb = _f32(zb_ref)
        o_ref[...] = (_gelu(y_ref[...]) * _sigmoid(_f32(gl_ref) + b_ref[...])
                      * (zb * _sigmoid(zb))).astype(o_ref.dtype)

    return pl.pallas_call(
        body, name=name, grid=(l // tr,),
        in_specs=[_row_spec(tr, w, 0), _row_spec(tr, w, 0), _row_spec(tr, w, zb_col), _full_spec((1, w))],
        out_specs=_row_spec(tr, w, 0), out_shape=jax.ShapeDtypeStruct((l, w), BF16),
        compiler_params=_params("parallel"),
    )(y, gl, proj, b_glu.reshape(1, w))


def _glu_bwd1(y, gl, proj, b_glu, dpb, dproj, zb_col, name):
    l, w = y.shape
    tr = _tile(l, 512)

    def body(y_ref, gl_ref, zb_ref, b_ref, dpb_ref, _, dzb_ref, dgl_ref, t_ref, db_ref):
        zb = _f32(zb_ref)
        dpb_v = _f32(dpb_ref)
        yg = _gelu(y_ref[...])
        sgl = _sigmoid(_f32(gl_ref) + b_ref[...])
        szb = _sigmoid(zb)
        dzb_ref[...] = (dpb_v * yg * sgl * (szb * (1.0 + zb * (1.0 - szb)))).astype(dzb_ref.dtype)
        e = dpb_v * (zb * szb)
        dgl = e * yg * sgl * (1.0 - sgl)
        dgl_ref[...] = dgl.astype(dgl_ref.dtype)
        t_ref[...] = e * sgl

        @pl.when(pl.program_id(0) == 0)
        def _():
            db_ref[...] = jnp.zeros_like(db_ref)

        db_ref[...] += jnp.sum(dgl, axis=0, keepdims=True)

    return pl.pallas_call(
        body, name=name, grid=(l // tr,),
        in_specs=[_row_spec(tr, w, 0), _row_spec(tr, w, 0), _row_spec(tr, w, zb_col), _full_spec((1, w)),
                  _row_spec(tr, w, 0), ANY_SPEC],
        out_specs=[_row_spec(tr, w, zb_col)] + [_row_spec(tr, w, 0)] * 2 + [_full_spec((1, w))],
        out_shape=[jax.ShapeDtypeStruct(dproj.shape, dproj.dtype), jax.ShapeDtypeStruct((l, w), BF16),
                   jax.ShapeDtypeStruct((l, w), F32), jax.ShapeDtypeStruct((1, w), F32)],
        input_output_aliases={5: 0},
        compiler_params=_params("arbitrary"),
    )(y, gl, proj, b_glu.reshape(1, w), dpb, dproj)


def _write_cols(dproj, cols, col, name):
    l, w = cols.shape
    tr = _tile(l, 512)

    def body(c_ref, _, o_ref):
        o_ref[...] = c_ref[...].astype(o_ref.dtype)

    return pl.pallas_call(
        body, name=name, grid=(l // tr,), in_specs=[_row_spec(tr, w, 0), ANY_SPEC],
        out_specs=_row_spec(tr, w, col), out_shape=jax.ShapeDtypeStruct(dproj.shape, dproj.dtype),
        input_output_aliases={1: 0}, compiler_params=_params("parallel"),
    )(cols, dproj)


def _glu_bwd2(y, t1, dyg2, name, deps=()):
    l, w = y.shape
    tr = _tile(l, 512)

    def body(y_ref, t_ref, d_ref, o_ref):
        o_ref[...] = (t_ref[...] + _f32(d_ref)) * _gelu_grad(y_ref[...])

    return _pallas(
        body, [y, t1, dyg2], deps, name=name, grid=(l // tr,), in_specs=[_row_spec(tr, w, 0)] * 3,
        out_specs=_row_spec(tr, w, 0), out_shape=jax.ShapeDtypeStruct((l, w), F32),
        compiler_params=_params("parallel"))


def _merge_fwd(proj, ya, yb, d, ga_col, gb_col, name):
    l = proj.shape[0]
    tr = _tile(l, 256)

    def body(ga_ref, gb_ref, ya_ref, yb_ref, o_ref):
        o_ref[...] = (_sigmoid(_f32(ga_ref)) * _f32(ya_ref)
                      + _sigmoid(_f32(gb_ref)) * _f32(yb_ref)).astype(o_ref.dtype)

    return pl.pallas_call(
        body, name=name, grid=(l // tr,),
        in_specs=[_row_spec(tr, d, ga_col), _row_spec(tr, d, gb_col), _row_spec(tr, d, 0), _row_spec(tr, d, 0)],
        out_specs=_row_spec(tr, d, 0), out_shape=jax.ShapeDtypeStruct((l, d), BF16),
        compiler_params=_params("parallel"),
    )(proj, proj, ya, yb)


def _merge_bwd(proj, ya, yb, dm, d, ga_col, name):
    l, n = proj.shape
    tr = _tile(l, 256)

    def body(g_ref, ya_ref, yb_ref, dm_ref, dy_ref, dg_ref):
        dmv = _f32(dm_ref)
        sg = _sigmoid(_f32(g_ref))
        yv = jnp.where(pl.program_id(1) == 0, _f32(ya_ref), _f32(yb_ref))
        dy_ref[...] = (dmv * sg).astype(dy_ref.dtype)
        dg_ref[...] = (dmv * yv * sg * (1.0 - sg)).astype(dg_ref.dtype)

    row = pl.BlockSpec((tr, d), lambda i, j: (i, 0))
    return pl.pallas_call(
        body, name=name, grid=(l // tr, 2),
        in_specs=[pl.BlockSpec((tr, d), lambda i, j: (i, ga_col + j)), row, row, row],
        out_specs=[pl.BlockSpec((None, tr, d), lambda i, j: (j, i, 0)),
                   pl.BlockSpec((tr, d), lambda i, j: (i, ga_col + j))],
        out_shape=[jax.ShapeDtypeStruct((2, l, d), BF16), jax.ShapeDtypeStruct((l, n), BF16)],
        compiler_params=_params("parallel", "arbitrary"),
    )(proj, ya, yb, dm)


def _to_segments(a):
    l, w = a.shape
    return a.reshape(SUBLANES, l // SUBLANES, w).transpose(1, 0, 2).reshape(l, w)


def _from_segments(a):
    l, w = a.shape
    return a.reshape(l // SUBLANES, SUBLANES, w).transpose(1, 0, 2).reshape(l, w)


def _dense(z, shape):
    return jnp.broadcast_to(z, shape).reshape(-1, LANES)


def _s5_disc(are, aim, ldt):
    dt = jnp.exp(ldt)
    er = jnp.exp(are * dt)
    lbr = er * jnp.cos(aim * dt)
    lbi = er * jnp.sin(aim * dt)
    inv = 1.0 / (are * are + aim * aim)
    fr = ((lbr - 1.0) * are + lbi * aim) * inv
    fi = (lbi * are - (lbr - 1.0) * aim) * inv
    return dt, lbr, lbi, inv, fr, fi


def _s5_params(are, aim, ldt, bre, bim, input_matrix, name, deps=()):
    shape = are.shape

    def body(are_ref, aim_ref, ldt_ref, bre_ref, bim_ref, re_ref, im_ref):
        _, lbr, lbi, _, fr, fi = _s5_disc(are_ref[...], aim_ref[...], ldt_ref[...])
        if input_matrix:
            re_ref[...] = fr * bre_ref[...] - fi * bim_ref[...]
            im_ref[...] = fr * bim_ref[...] + fi * bre_ref[...]
        else:
            re_ref[...] = lbr
            im_ref[...] = lbi

    out = jax.ShapeDtypeStruct(shape, F32)
    return _pallas(body, [are, aim, ldt, bre, bim], deps, name=name,
                   in_specs=[pl.BlockSpec(memory_space=pltpu.VMEM)] * 5, out_shape=[out] * 2,
                   compiler_params=pltpu.CompilerParams(vmem_limit_bytes=VMEM_LIMIT_BYTES))


def _s5_params_bwd(are, aim, ldt, bre, bim, glbr, glbi, gbbr, gbbi, n_groups, name, deps=()):
    shape = are.shape
    rows_per_group = shape[0] // n_groups

    def body(are_ref, aim_ref, ldt_ref, bre_ref, bim_ref, glbr_ref, glbi_ref, gbbr_ref, gbbi_ref,
             gar_ref, gai_ref, gdt_ref, gbr_ref, gbi_ref):
        are_v, aim_v = are_ref[...], aim_ref[...]
        bre_v, bim_v = bre_ref[...], bim_ref[...]
        gbbr_v, gbbi_v = gbbr_ref[...], gbbi_ref[...]
        dt, lbr, lbi, inv, fr, fi = _s5_disc(are_v, aim_v, ldt_ref[...])
        gbr_ref[...] = fr * gbbr_v + fi * gbbi_v
        gbi_ref[...] = fr * gbbi_v - fi * gbbr_v
        lane_group = lax.broadcasted_iota(jnp.int32, (LANES, LANES), 0) // S5_GROUP
        same_group = (lane_group == lax.broadcasted_iota(jnp.int32, (LANES, LANES), 1) // S5_GROUP)
        ones = same_group.astype(F32)
        gfr = jnp.dot(bre_v * gbbr_v + bim_v * gbbi_v, ones, precision=HIGHEST, preferred_element_type=F32)
        gfi = jnp.dot(bre_v * gbbi_v - bim_v * gbbr_v, ones, precision=HIGHEST, preferred_element_type=F32)
        glr = glbr_ref[...] + (are_v * gfr - aim_v * gfi) * inv
        gli = glbi_ref[...] + (are_v * gfi + aim_v * gfr) * inv
        qr = (fr * are_v + fi * aim_v) * inv
        qi = (fi * are_v - fr * aim_v) * inv
        gzr = lbr * glr + lbi * gli
        gzi = lbr * gli - lbi * glr
        gar_ref[...] = dt * gzr - (qr * gfr + qi * gfi)
        gai_ref[...] = dt * gzi - (qr * gfi - qi * gfr)
        e = dt * (are_v * gzr + aim_v * gzi)
        per_group = jnp.sum(e.reshape(n_groups, rows_per_group, LANES), axis=1)
        total = jnp.sum(per_group, axis=1, keepdims=True) * (1.0 / S5_GROUP)
        gdt_ref[...] = jnp.broadcast_to(total, gdt_ref.shape)

    out = jax.ShapeDtypeStruct(shape, F32)
    return _pallas(
        body, [are, aim, ldt, bre, bim, glbr, glbi, gbbr, gbbi], deps, name=name,
        in_specs=[pl.BlockSpec(memory_space=pltpu.VMEM)] * 9,
        out_shape=[out, out, jax.ShapeDtypeStruct((n_groups, LANES), F32), out, out],
        compiler_params=pltpu.CompilerParams(vmem_limit_bytes=VMEM_LIMIT_BYTES))


def _cmul(ar, ai, br, bi):
    return ar * br - ai * bi, ar * bi + ai * br


def _scan_in_place(hr_ref, hi_ref, lr, li, reverse):
    l, wb = hr_ref.shape
    nt = l // SUBLANES
    shift = SUBLANES - 1 if reverse else 1
    unroll = 8 if nt % 8 == 0 else 1

    def rows(k):
        t = (nt - 1 - k) if reverse else k
        return pl.ds(pl.multiple_of(t * SUBLANES, SUBLANES), SUBLANES)

    zero = jnp.zeros((SUBLANES, wb), F32)

    def local_step(k, carry):
        hr, hi = carry
        r = rows(k)
        tr_, ti_ = _cmul(lr, li, hr, hi)
        hr, hi = tr_ + hr_ref[r, :], ti_ + hi_ref[r, :]
        hr_ref[r, :] = hr
        hi_ref[r, :] = hi
        return hr, hi

    er, ei = lax.fori_loop(0, nt, local_step, (zero, zero), unroll=unroll)

    lnr = lni = None
    br, bi, n = lr, li, nt
    while n:
        if n & 1:
            lnr, lni = (br, bi) if lnr is None else _cmul(lnr, lni, br, bi)
        n >>= 1
        if n:
            br, bi = _cmul(br, bi, br, bi)

    row = lax.broadcasted_iota(jnp.int32, (SUBLANES, wb), 0)
    tr_, ti_ = er, ei
    for j in range(1, SUBLANES):
        pr_, pi_ = _cmul(lnr, lni, pltpu.roll(tr_, shift, axis=0), pltpu.roll(ti_, shift, axis=0))
        at = row == ((SUBLANES - 1 - j) if reverse else j)
        tr_ = jnp.where(at, er + pr_, tr_)
        ti_ = jnp.where(at, ei + pi_, ti_)
    edge = row == ((SUBLANES - 1) if reverse else 0)
    cr = jnp.where(edge, 0.0, pltpu.roll(tr_, shift, axis=0))
    ci = jnp.where(edge, 0.0, pltpu.roll(ti_, shift, axis=0))

    def fix_step(k, carry):
        zr, zi = _cmul(lr, li, *carry)
        r = rows(k)
        hr_ref[r, :] = hr_ref[r, :] + zr
        hi_ref[r, :] = hi_ref[r, :] + zi
        return zr, zi

    lax.fori_loop(0, nt, fix_step, (cr, ci), unroll=unroll)


def _dot(a, b):
    return jnp.dot(a.astype(BF16), b.astype(BF16), preferred_element_type=F32)


def _s5_forward(u_seg, mb_re, mb_im, mc_re, mc_im, lam_re, lam_im, dvec, name, deps=()):
    l = u_seg.shape[0]
    nb, kin, kst = mb_re.shape

    def body(u_ref, mbr_ref, mbi_ref, mcr_ref, mci_ref, lr_ref, li_ref, d_ref, hr_ref, hi_ref, y_ref):
        u = u_ref[...]
        hr_ref[...] = _dot(u, mbr_ref[...])
        hi_ref[...] = _dot(u, mbi_ref[...])
        _scan_in_place(hr_ref, hi_ref, jnp.broadcast_to(lr_ref[...], (SUBLANES, kst)),
                       jnp.broadcast_to(li_ref[...], (SUBLANES, kst)), False)
        y_ref[...] = (_dot(hr_ref[...], mcr_ref[...]) - _dot(hi_ref[...], mci_ref[...])
                      + d_ref[...] * u.astype(F32))

    act = pl.BlockSpec((l, kin), lambda b: (0, b))
    state = pl.BlockSpec((l, kst), lambda b: (0, b))
    up = pl.BlockSpec((None, kin, kst), lambda b: (b, 0, 0))
    down = pl.BlockSpec((None, kst, kin), lambda b: (b, 0, 0))
    hshape = jax.ShapeDtypeStruct((l, nb * kst), F32)
    return _pallas(
        body, [u_seg, mb_re, mb_im, mc_re, mc_im, lam_re, lam_im, dvec], deps, name=name, grid=(nb,),
        in_specs=[act, up, up, down, down, pl.BlockSpec((1, kst), lambda b: (0, b)),
                  pl.BlockSpec((1, kst), lambda b: (0, b)), pl.BlockSpec((1, kin), lambda b: (0, b))],
        out_specs=[state, state, act],
        out_shape=[hshape, hshape, jax.ShapeDtypeStruct((l, nb * kin), F32)],
        compiler_params=_params("parallel"))


def _dot_ta(a, b):
    return lax.dot_general(a.astype(BF16), b.astype(BF16), (((0,), (0,)), ((), ())), preferred_element_type=F32)


def _dot_nt(a, b):
    return lax.dot_general(a.astype(BF16), b.astype(BF16), (((1,), (1,)), ((), ())), preferred_element_type=F32)


def _s5_backward(dy_seg, u_seg, h_re, h_im, mb_re, mb_im, mc_re, mc_im, lam_re, lam_im_neg, dvec, name):
    l = dy_seg.shape[0]
    nb, kin, kst = mb_re.shape
    nt = l // SUBLANES

    def body(dy_ref, u_ref, hr_ref, hi_ref, mbr_ref, mbi_ref, mcr_ref, mci_ref, lr_ref, li_ref,
             d_ref, du_ref, gcr_ref, gci_ref, gbr_ref, gbi_ref, glr_ref, gli_ref, dsk_ref, qr_ref, qi_ref):
        dy = dy_ref[...]
        qr_ref[...] = _dot_nt(dy, mcr_ref[...])
        qi_ref[...] = -_dot_nt(dy, mci_ref[...])
        _scan_in_place(qr_ref, qi_ref, jnp.broadcast_to(lr_ref[...], (SUBLANES, kst)),
                       jnp.broadcast_to(li_ref[...], (SUBLANES, kst)), True)
        du_ref[...] = _dot_nt(qr_ref[...], mbr_ref[...]) + _dot_nt(qi_ref[...], mbi_ref[...]) + d_ref[...] * dy
        dsk_ref[...] = jnp.sum(dy * _f32(u_ref), axis=0, keepdims=True)
        gcr_ref[...] = _dot_ta(dy, hr_ref[...])
        gci_ref[...] = _dot_ta(dy, hi_ref[...])
        gbr_ref[...] = _dot_ta(u_ref[...], qr_ref[...])
        gbi_ref[...] = _dot_ta(u_ref[...], qi_ref[...])

        row = lax.broadcasted_iota(jnp.int32, (SUBLANES, kst), 0)
        last = pl.ds((nt - 1) * SUBLANES, SUBLANES)
        first = pl.ds(0, SUBLANES)
        pr = jnp.where(row == 0, 0.0, pltpu.roll(hr_ref[last, :], 1, axis=0))
        pi = jnp.where(row == 0, 0.0, pltpu.roll(hi_ref[last, :], 1, axis=0))
        gr, gi = qr_ref[first, :], qi_ref[first, :]

        def step(t, carry):
            acc_r, acc_i = carry
            cur = pl.ds(pl.multiple_of(t * SUBLANES, SUBLANES), SUBLANES)
            prev = pl.ds(pl.multiple_of((t - 1) * SUBLANES, SUBLANES), SUBLANES)
            gr, gi = qr_ref[cur, :], qi_ref[cur, :]
            pr, pi = hr_ref[prev, :], hi_ref[prev, :]
            return acc_r + gr * pr + gi * pi, acc_i + gi * pr - gr * pi

        acc_r, acc_i = lax.fori_loop(1, nt, step, (gr * pr + gi * pi, gi * pr - gr * pi))
        glr_ref[...] = jnp.sum(acc_r, axis=0, keepdims=True)
        gli_ref[...] = jnp.sum(acc_i, axis=0, keepdims=True)

    act = pl.BlockSpec((l, kin), lambda b: (0, b))
    state = pl.BlockSpec((l, kst), lambda b: (0, b))
    up = pl.BlockSpec((None, kin, kst), lambda b: (b, 0, 0))
    down = pl.BlockSpec((None, kst, kin), lambda b: (b, 0, 0))
    vec_st = pl.BlockSpec((1, kst), lambda b: (0, b))
    vec_in = pl.BlockSpec((1, kin), lambda b: (0, b))
    outer = jax.ShapeDtypeStruct((nb, kin, kst), F32)
    lam_shape = jax.ShapeDtypeStruct((1, nb * kst), F32)
    return pl.pallas_call(
        body, name=name, grid=(nb,),
        in_specs=[act, act, state, state, up, up, down, down, vec_st, vec_st, vec_in],
        out_specs=[act, up, up, up, up, vec_st, vec_st, vec_in],
        out_shape=[jax.ShapeDtypeStruct((l, nb * kin), F32), outer, outer, outer, outer, lam_shape, lam_shape,
                   jax.ShapeDtypeStruct((1, nb * kin), F32)],
        scratch_shapes=[pltpu.VMEM((l, kst), F32), pltpu.VMEM((l, kst), F32)],
        compiler_params=_params("parallel"),
    )(dy_seg, u_seg, h_re, h_im, mb_re, mb_im, mc_re, mc_im, lam_re, lam_im_neg, dvec)


def _block_diag(m, nb):
    g, r, s = m.shape
    gb = g // nb
    tiled = jnp.tile(m.reshape(nb, gb * r, s), (1, 1, gb))
    row_group = lax.broadcasted_iota(jnp.int32, tiled.shape, 1) // r
    col_group = lax.broadcasted_iota(jnp.int32, tiled.shape, 2) // s
    return jnp.where(row_group == col_group, tiled, 0.0)


def _block_diag_extract(mat, g, r, s):
    nb = mat.shape[0]
    gb = g // nb
    eye = jnp.eye(gb, dtype=mat.dtype)
    m5 = mat.reshape(nb, gb, r, gb, s) * eye[None, :, None, :, None]
    return jnp.sum(m5, axis=3).reshape(g, r, s)


def _adamw(w, m, v, gslots, name, layer=0, prev=None, deps=()):
    layers, r, c = w.shape
    s = gslots.shape[0]
    tr = _tile(r, max(SUBLANES, 1 << int(math.log2(ADAMW_BLOCK_ELEMS // c))))
    bc1 = 1.0 / (1.0 - ADAM_B1 ** ADAM_STEP)
    bc2 = 1.0 / (1.0 - ADAM_B2 ** ADAM_STEP)

    def body(w_ref, m_ref, v_ref, g_ref, *rest):
        go_ref, d_ref, mo_ref, vo_ref = rest[-4:]
        g = g_ref[0].astype(F32)
        for k in range(1, s):
            g = g + g_ref[k].astype(F32)
        mn = ADAM_B1 * m_ref[...] + (1.0 - ADAM_B1) * g
        vn = ADAM_B2 * v_ref[...] + (1.0 - ADAM_B2) * (g * g)
        go_ref[...] = g
        mo_ref[...] = mn
        vo_ref[...] = vn
        d_ref[...] = -ADAM_LR * ((mn * bc1) / (jnp.sqrt(vn * bc2) + ADAM_EPS) + ADAM_WD * w_ref[...])

    spec = pl.BlockSpec((None, tr, c), lambda i: (layer, i, 0))
    out = jax.ShapeDtypeStruct((layers, r, c), F32)
    in_specs = [spec, spec, spec, pl.BlockSpec((s, tr, c), lambda i: (0, i, 0))]
    args = [w, m, v, gslots]
    aliases = {}
    if prev is not None:
        in_specs += [ANY_SPEC] * 4
        args += list(prev)
        aliases = {4 + q: q for q in range(4)}
    in_specs += [ANY_SPEC] * len(deps)
    args += list(deps)
    return pl.pallas_call(
        body, name=name, grid=(r // tr,), in_specs=in_specs,
        out_specs=[spec] * 4, out_shape=[out] * 4, input_output_aliases=aliases,
        compiler_params=_params("parallel"),
    )(*args)


def _pack(parts):
    flat = jnp.concatenate([p.reshape(-1) for p in parts])
    pad = (-flat.shape[0]) % (PACK_ROWS * LANES)
    return jnp.pad(flat, (0, pad)).reshape(-1, LANES)


def _unpack(packed, shapes):
    flat = packed.reshape(-1)
    out, off = [], 0
    for shp in shapes:
        size = math.prod(shp)
        out.append(flat[off:off + size].reshape(shp))
        off += size
    return out


def kernel(x, norm_g, w_in, conv_w, w_out_a, a_re, a_im, log_dt, b_re, b_im, c_re, c_im, d_skip, w_glu, b_glu, w_out_b, w_o, final_g, loss_target, m_norm_g, m_w_in, m_conv_w, m_w_out_a, m_a_re, m_a_im, m_log_dt, m_b_re, m_b_im, m_c_re, m_c_im, m_d_skip, m_w_glu, m_b_glu, m_w_out_b, m_w_o, m_final_g, v_norm_g, v_w_in, v_conv_w, v_w_out_a, v_a_re, v_a_im, v_log_dt, v_b_re, v_b_im, v_c_re, v_c_im, v_d_skip, v_w_glu, v_b_glu, v_w_out_b, v_w_o, v_final_g):
    depth = norm_g.shape[0]
    l, d = x.shape[1], x.shape[2]
    ws = w_glu.shape[2]
    n_groups, n_state = a_re.shape[1], a_re.shape[2]
    nb = ws // LANES
    assert S5_GROUP == b_re.shape[3] and n_state * S5_GB == 4 * LANES
    u_col, zb_col = 4 * d // ws, 4 * d // ws + 1
    ga_col, gb_col = (4 * d + 2 * ws) // d, (4 * d + 2 * ws) // d + 1
    me = 4 * lax.axis_index("x") + 2 * lax.axis_index("y") + lax.axis_index("c")

    xs = [x[0]]
    tgt = loss_target[0]

    big_names = ("w_in", "w_out_a", "w_glu", "w_out_b", "w_o")
    big = dict(w_in=(w_in, m_w_in, v_w_in), w_out_a=(w_out_a, m_w_out_a, v_w_out_a),
               w_glu=(w_glu, m_w_glu, v_w_glu), w_out_b=(w_out_b, m_w_out_b, v_w_out_b),
               w_o=(w_o, m_w_o, v_w_o))

    all_shards = [[None] * len(big_names) for _ in range(depth)]
    all_shards[0][0] = w_in[0].astype(BF16)

    def shards_bf16(i):
        return all_shards[i]

    main_names = ("a_re", "a_im", "log_dt", "b_re", "b_im", "c_re", "c_im", "d_skip", "b_glu", "conv_w")
    small_w = dict(a_re=(a_re, m_a_re, v_a_re), a_im=(a_im, m_a_im, v_a_im),
                   log_dt=(log_dt, m_log_dt, v_log_dt), b_re=(b_re, m_b_re, v_b_re), b_im=(b_im, m_b_im, v_b_im),
                   c_re=(c_re, m_c_re, v_c_re), c_im=(c_im, m_c_im, v_c_im), d_skip=(d_skip, m_d_skip, v_d_skip),
                   b_glu=(b_glu, m_b_glu, v_b_glu))
    main_shapes = [small_w[k][0].shape for k in main_names[:-1]] + [(depth, 3, d)]
    dc = d // N_DEV
    pad8 = lambda a: jnp.pad(a.reshape(depth * 3, dc), ((0, SUBLANES - depth * 3), (0, 0)))[None]

    def input_only_work(zero):
        for i in range(depth):
            for j, k in enumerate(big_names):
                if all_shards[i][j] is None:
                    all_shards[i][j] = (big[k][0][i] + zero).astype(BF16)
        zeros_conv = jnp.zeros((depth, 3, d), F32) + zero
        zero1 = jnp.zeros((1,), F32) + zero
        main = [_pack([small_w[k][q] for k in main_names[:-1]] + [zeros_conv])[None] for q in range(3)]
        gains = [_pack([g_, f_, zero1])[None]
                 for g_, f_ in ((norm_g, final_g), (m_norm_g, m_final_g), (v_norm_g, v_final_g))]
        return main, gains, [pad8(conv_w), pad8(m_conv_w), pad8(v_conv_w)]

    def gather_start(shards, name, deps=()):
        sems, srcs, lands, token = _exchange_start(
            _plan_gather_chips, shards, [_landing(s_, N_DEV, me) for s_ in shards], f"{name}_start", deps)
        return (name, sems, srcs, lands), token

    def gather_forward(state, after, deps=()):
        name, sems, srcs, lands = state
        _, lands = _exchange_wait(_plan_gather_chips, sems, srcs, lands, after, f"{name}_wait")
        sems, _, lands, token = _exchange_start(_plan_gather_forward, None, lands, f"{name}_forward_start", deps)
        return (name, sems, lands), token

    def gather_finish(state, after):
        name, sems, lands = state
        return _exchange_wait(_plan_gather_forward, sems, None, lands, after, f"{name}_forward_wait")[1]

    def halves_start(shards, name, deps=()):
        srcs = [s_.reshape(2, s_.shape[0] // 2, *s_.shape[1:]) for s_ in shards]
        lands = [lax.dynamic_update_slice(lax.empty((2 * N_DEV,) + h_.shape[1:], h_.dtype), h_, (2 * me, 0, 0))
                 for h_ in srcs]
        sems, srcs, lands, token = _exchange_start(_plan_halves_first, srcs, lands, f"{name}_start", deps)
        return (name, sems, srcs, lands), token

    def halves_second(state, after, deps=()):
        name, sems, srcs, lands = state
        _, lands = _exchange_wait(_plan_halves_first, sems, srcs, lands, after, f"{name}_wait")
        sems, _, lands, token = _exchange_start(_plan_halves_second, None, lands, f"{name}_second_start", deps)
        return (name, sems, lands), token

    def halves_forward(state, after, deps=()):
        name, sems, lands = state
        _, lands = _exchange_wait(_plan_halves_second, sems, None, lands, after, f"{name}_second_wait")
        sems, _, lands, token = _exchange_start(_plan_halves_forward, None, lands, f"{name}_forward_start", deps)
        return (name, sems, lands), token

    def halves_finish(state, after):
        name, sems, lands = state
        lands = _exchange_wait(_plan_halves_forward, sems, None, lands, after, f"{name}_forward_wait")[1]
        return [l_.reshape(N_DEV, 2 * l_.shape[1], *l_.shape[2:]) for l_ in lands]

    conv_shard = jnp.pad(conv_w.reshape(depth * 3, -1), ((0, HALO - depth * 3), (0, 0)))
    w_in_state, token = halves_start([shards_bf16(0)[0], conv_shard], "ag_w_in_0")
    s5 = []
    shape3 = (n_groups, n_state, S5_GROUP)
    for i in range(depth):
        dense_in = (_dense(a_re[i][:, :, None], shape3), _dense(a_im[i][:, :, None], shape3),
                    _dense(log_dt[i][:, None, None], shape3), b_re[i].reshape(-1, LANES), b_im[i].reshape(-1, LANES))
        lbr, lbi = _s5_params(*dense_in, False, f"s5_lam_{i}", deps=(token,))
        s5.append(dict(dense_in=dense_in, lam_re=lbr.reshape(shape3)[:, :, 0].reshape(1, -1),
                       lam_im=lbi.reshape(shape3)[:, :, 0].reshape(1, -1)))
    main_wmv, gains_wmv, conv_wmv = input_only_work(token[0, 0])
    w_in_state, token = halves_second(w_in_state, [p[k] for p in s5 for k in ("lam_re", "lam_im")] + main_wmv)
    for i in range(depth):
        bbr, bbi = _s5_params(*s5[i]["dense_in"], True, f"s5_input_matrix_{i}", deps=(token,))
        bbr3, bbi3 = bbr.reshape(shape3), bbi.reshape(shape3)
        diag = lambda m: _block_diag(m, nb).astype(BF16)
        s5[i].update(up=(diag(bbr3.transpose(0, 2, 1)), diag(bbi3.transpose(0, 2, 1))),
                     down=(diag((c_re[i] + token[0, 0]).transpose(0, 2, 1)),
                           diag((c_im[i] + token[0, 0]).transpose(0, 2, 1))))
    prologue = [m for p in s5 for k in ("up", "down") for m in p[k]]
    prologue += main_wmv + gains_wmv + conv_wmv + all_shards[0][1:] + [s_ for sh in all_shards[1:] for s_ in sh]
    w_in_state, token = halves_forward(w_in_state, prologue, deps=(token,))
    rest_state, token = halves_start(shards_bf16(0)[1:], "ag_rest_0", deps=(token,))
    next_state = None
    if depth > 1:
        next_state, token = halves_start([shards_bf16(1)[0]], "ag_w_in_1", deps=(token,))

    saved = []
    wg = [None] * depth
    conv_full = None
    for i in range(depth):
        xi = xs[-1]
        h = _rmsnorm_fwd(xi, norm_g[i], f"rmsnorm_fwd_{i}", deps=(token,))
        arrived = halves_finish(w_in_state, [h])
        if i == 0:
            conv_full = arrived[1].transpose(1, 0, 2).reshape(HALO, d)[:depth * 3].reshape(depth, 3, d)
        conv8 = jnp.pad(conv_full[i], ((0, SUBLANES - 3), (0, 0)))
        proj = _mm_win_fwd(h, arrived[0], f"mm_proj_{i}")
        rest_state, token = halves_second(rest_state, [proj])
        if next_state is not None:
            next_state, token = halves_second(next_state, [token], deps=(token,))
        u_seg = _to_segments(proj[:, 4 * d:4 * d + ws])
        h_re, h_im, y_seg = _s5_forward(u_seg, *s5[i]["up"], *s5[i]["down"], s5[i]["lam_re"], s5[i]["lam_im"],
                                        d_skip[i].reshape(1, ws), f"s5_forward_{i}", deps=(token,))
        rest_state, token = halves_forward(rest_state, [y_seg])
        pa = _branch_a_fwd(proj, conv8, d, f"branch_a_fwd_{i}", deps=(token,))
        rest = halves_finish(rest_state, [pa])
        wg[i] = g = dict(w_in=arrived[0], w_a=rest[0].reshape(d, d), w_glu=rest[1].reshape(ws, ws),
                         w_b=rest[2], w_o=rest[3].reshape(d, d))
        ya = _mm(pa, g["w_a"], name=f"mm_ya_{i}", out_dtype=BF16)
        y = _from_segments(y_seg)
        yg = _gelu_cast(y, f"gelu_{i}")
        gl = _mm(yg, g["w_glu"], name=f"mm_glu_{i}", out_dtype=BF16)
        pb = _glu_post(y, gl, proj, b_glu[i], zb_col, f"glu_post_{i}")
        w_b2d = g["w_b"].transpose(1, 0, 2).reshape(ws, d)
        yb = _mm(pb, w_b2d, name=f"mm_yb_{i}", out_dtype=BF16)
        mrg = _merge_fwd(proj, ya, yb, d, ga_col, gb_col, f"merge_fwd_{i}")
        deps = ()
        if i + 1 < depth:
            w_in_state, token = halves_forward(next_state, [mrg])
            rest_state, token = halves_start(shards_bf16(i + 1)[1:], f"ag_rest_{i + 1}", deps=(token,))
            next_state = None
            if i + 2 < depth:
                next_state, token = halves_start([shards_bf16(i + 2)[0]], f"ag_w_in_{i + 2}", deps=(token,))
            deps = (token,)
        xs.append(_mm(mrg, g["w_o"], name=f"mm_out_{i}", add=xi, deps=deps))
        saved.append(dict(h=h, proj=proj, pa=pa, ya=ya, yb=yb, y=y, yg=yg, gl=gl, pb=pb, mrg=mrg,
                          u_seg=u_seg, h_re=h_re, h_im=h_im, conv8=conv8, w_b2d=w_b2d))

    dx, g_final, loss_part = _final_loss(xs[-1], final_g, tgt, "final_loss")

    rs_pending = []
    small = {k: [None] * depth for k in ("norm_g", "a_re", "a_im", "log_dt", "b_re", "b_im", "c_re", "c_im",
                                         "d_skip", "b_glu", "conv_w")}

    my_chip = 2 * lax.axis_index("x") + lax.axis_index("y")

    def reduce_on_chip(pieces, tag):
        lands = [lax.empty((4,) + p.shape[1:], p.dtype) for p in pieces]
        sems, srcs, lands, token = _exchange_start(_plan_reduce_sibling, pieces, lands, f"rs_sibling_start_{tag}")
        return (sems, srcs, lands), token

    def reduce_across_chips(names_, state, layer, tag, after):
        sems, srcs, lands = state
        srcs, lands = _exchange_wait(_plan_reduce_sibling, sems, srcs, lands, after, f"rs_sibling_wait_{tag}")
        sums = [_chip_sums(p, l_, f"chip_sum_{k}_{layer}") for k, p, l_ in zip(names_, srcs, lands)]
        lands = [_landing(lax.dynamic_index_in_dim(s_, my_chip, 0, keepdims=False), 4, my_chip) for s_ in sums]
        sems, srcs, lands, token = _exchange_start(_plan_reduce_chips, sums, lands, f"rs_chips_start_{tag}")
        rs_pending.append((names_, layer, sems, srcs, lands, f"rs_chips_wait_{tag}"))
        return token

    for i in reversed(range(depth)):
        s, g = saved[i], wg[i]
        proj = s["proj"]
        dxo_b = dx.astype(BF16)
        dm = _mm(dxo_b, g["w_o"], name=f"mm_dm_{i}", nt=True, out_dtype=BF16)
        gw_o = _mm(s["mrg"], dxo_b, ta=True, name=f"mm_gw_o_{i}", out_dtype=BF16)
        dy2, dproj = _merge_bwd(proj, s["ya"], s["yb"], dm, d, ga_col, f"merge_bwd_{i}")
        dya, dyb = dy2[0], dy2[1]
        dpa = _mm(dya, g["w_a"], name=f"mm_dpa_{i}", nt=True, out_dtype=BF16)
        gw_a = _mm(s["pa"], dya, ta=True, name=f"mm_gw_a_{i}", out_dtype=BF16)
        dpb = _mm(dyb, s["w_b2d"], name=f"mm_dpb_{i}", nt=True, out_dtype=BF16)
        gw_b = _mm(s["pb"], dyb, ta=True, name=f"mm_gw_b_{i}", split_n=N_DEV, out_dtype=BF16)
        dproj, dw0, dw1, dw2 = _branch_a_bwd(proj, dpa, s["conv8"], dproj, d, f"branch_a_bwd_{i}")
        small["conv_w"][i] = jnp.concatenate([dw0, dw1, dw2], axis=0)
        dproj, dgl, t1, db_glu = _glu_bwd1(s["y"], s["gl"], proj, b_glu[i], dpb, dproj, zb_col, f"glu_bwd1_{i}")
        small["b_glu"][i] = db_glu.reshape(ws)
        dyg2 = _mm(dgl, g["w_glu"], name=f"mm_dyg_{i}", nt=True, out_dtype=BF16)
        gw_glu = _mm(s["yg"], dgl, ta=True, name=f"mm_gw_glu_{i}", out_dtype=BF16)
        small_names_ = ("w_out_a", "w_glu", "w_out_b", "w_o")
        state, token = reduce_on_chip(
            [gw_a.reshape(N_DEV, d // N_DEV, d), gw_glu.reshape(N_DEV, ws // N_DEV, ws), gw_b,
             gw_o.reshape(N_DEV, d // N_DEV, d)], f"small_{i}")
        dy = _glu_bwd2(s["y"], t1, dyg2, f"glu_bwd2_{i}", deps=(token,))
        dy_seg = _to_segments(dy)
        u_seg = s["u_seg"]
        du_seg, gc_re, gc_im, gbb_re, gbb_im, glam_re, glam_im, dskip = _s5_backward(
            dy_seg, u_seg, s["h_re"], s["h_im"], *s5[i]["up"], *s5[i]["down"],
            s5[i]["lam_re"], -s5[i]["lam_im"], d_skip[i].reshape(1, ws), f"s5_backward_{i}")
        token = reduce_across_chips(small_names_, state, i, f"small_{i}", [du_seg])
        dproj = _write_cols(dproj, _from_segments(du_seg), u_col, f"write_du_{i}")
        gw_in = _mm(s["h"], dproj, ta=True, name=f"mm_gw_in_{i}", split_n=N_DEV, tm=1024,
                    out_dtype=BF16, deps=(token,))
        state, token = reduce_on_chip([gw_in], f"w_in_{i}")
        small["d_skip"][i] = dskip.reshape(n_groups, S5_GROUP)
        small["c_re"][i] = _block_diag_extract(gc_re, n_groups, S5_GROUP, n_state)
        small["c_im"][i] = -_block_diag_extract(gc_im, n_groups, S5_GROUP, n_state)
        gbb_re = _block_diag_extract(gbb_re, n_groups, S5_GROUP, n_state).transpose(0, 2, 1)
        gbb_im = _block_diag_extract(gbb_im, n_groups, S5_GROUP, n_state).transpose(0, 2, 1)
        gar, gai, gdt, gbr, gbi = _s5_params_bwd(
            *s5[i]["dense_in"], _dense(glam_re.reshape(n_groups, n_state, 1), shape3),
            _dense(glam_im.reshape(n_groups, n_state, 1), shape3),
            gbb_re.reshape(-1, LANES), gbb_im.reshape(-1, LANES), n_groups, f"s5_params_bwd_{i}", deps=(token,))
        small["a_re"][i] = gar.reshape(shape3)[:, :, 0]
        small["a_im"][i] = gai.reshape(shape3)[:, :, 0]
        small["log_dt"][i] = gdt[:, 0]
        small["b_re"][i] = gbr.reshape(shape3)
        small["b_im"][i] = gbi.reshape(shape3)
        if i == 0:
            part = {k: jnp.stack(small[k]) for k in main_names}
            main_state, token = gather_start([_pack([part[k] for k in main_names]).astype(BF16)], "ag_small")
            token = reduce_across_chips(("w_in",), state, i, f"w_in_{i}", [token])
            main_state, token = gather_forward(main_state, [token])
            dh = _mm_win_bwd(dproj, g["w_in"], f"mm_dh_{i}", deps=(token,))
            main_slots = gather_finish(main_state, [dh])[0]
            deps = ()
        else:
            dh = _mm_win_bwd(dproj, g["w_in"], f"mm_dh_{i}", deps=(gar,))
            deps = (reduce_across_chips(("w_in",), state, i, f"w_in_{i}", [dh]),)
        dx, dng = _rmsnorm_bwd(xs[i], norm_g[i], dh, dx, f"rmsnorm_bwd_{i}", deps=deps)
        small["norm_g"][i] = dng.reshape(d)

    results = {}

    gain_grads = jnp.concatenate([jnp.stack(small["norm_g"]).reshape(-1), g_final.reshape(d)])
    gain_shapes = [(depth, d), (d,), (1,)]
    gains_state, token = gather_start([_pack([gain_grads, loss_part[0, :1]])], "ag_gains")

    sres = [_unpack(p[0], main_shapes) for p in _adamw(*main_wmv, main_slots, "adamw_small", deps=(token,))]
    for j, k in enumerate(main_names[:-1]):
        results[k] = [sres[q][j] for q in range(4)]
    gconv = lax.dynamic_slice_in_dim(sres[0][-1], me * dc, dc, axis=2)
    cres = _adamw(*conv_wmv, pad8(gconv), "adamw_conv_w")
    results["conv_w"] = [r_[0, :depth * 3].reshape(depth, 3, dc) for r_ in cres]

    after = [cres[0]]
    for names_, layer, sems, srcs, lands, wait_name in rs_pending:
        _, slots = _exchange_wait(_plan_reduce_chips, sems, srcs, lands, after, wait_name)
        for k, land in zip(names_, slots):
            w_, m_, v_ = big[k]
            results[k] = _adamw(w_, m_, v_, land, f"adamw_{k}_{layer}", layer=layer, prev=results.get(k))
            after = [results[k][0]]

    gains_state, token = gather_forward(gains_state, after)
    gpack = gather_finish(gains_state, [token])[0]
    gres = [_unpack(p[0], gain_shapes) for p in _adamw(*gains_wmv, gpack, "adamw_gains")]
    results["norm_g"] = [gres[q][0] for q in range(4)]
    results["final_g"] = [gres[q][1] for q in range(4)]
    loss = gres[0][2][0]

    names = ("norm_g", "w_in", "conv_w", "w_out_a", "a_re", "a_im", "log_dt", "b_re", "b_im", "c_re", "c_im",
             "d_skip", "w_glu", "b_glu", "w_out_b", "w_o", "final_g")
    outs = [loss, dx[None]]
    for q in range(4):
        outs += [results[k][q] for k in names]
    return tuple(outs)
```

```python
import functools
import math

import jax
import jax.numpy as jnp
from jax import lax
from jax.experimental import pallas as pl
from jax.experimental.pallas import tpu as pltpu

F32 = jnp.float32
BF16 = jnp.bfloat16
HIGHEST = lax.Precision.HIGHEST

N_DEV = 8
LANES = 128
SUBLANES = 8
VMEM_LIMIT_BYTES = 56 * 1024 * 1024

RMS_EPS = 1e-6
ADAM_LR = 0.001
ADAM_B1 = 0.9
ADAM_B2 = 0.999
ADAM_EPS = 1e-08
ADAM_WD = 0.01
ADAM_STEP = 10
GELU_C0 = math.sqrt(2.0 / math.pi)
GELU_C1 = 0.044715

ADAMW_BLOCK_ELEMS = 1 << 17
PACK_ROWS = 512

S5_GROUP = 16
S5_GB = LANES // S5_GROUP


def _params(*semantics):
    return pltpu.CompilerParams(dimension_semantics=semantics, vmem_limit_bytes=VMEM_LIMIT_BYTES)


ANY_SPEC = pl.BlockSpec(memory_space=pl.ANY)


def _pallas(body, args, deps=(), *, in_specs, **kwargs):
    deps = tuple(deps)
    if not deps:
        return pl.pallas_call(body, in_specs=in_specs, **kwargs)(*args)

    def body_after(*refs):
        body(*refs[len(deps):])

    return pl.pallas_call(body_after, in_specs=[ANY_SPEC] * len(deps) + list(in_specs), **kwargs)(*deps, *args)


def _tile(n, pref):
    t = min(n, pref)
    while n % t:
        assert t % 2 == 0, (n, pref)
        t //= 2
    return t


def _sigmoid(z):
    return 1.0 / (1.0 + jnp.exp(-z))


def _gelu(y):
    return 0.5 * y * (1.0 + jnp.tanh(GELU_C0 * (y + GELU_C1 * y * y * y)))


def _gelu_grad(y):
    t = jnp.tanh(GELU_C0 * (y + GELU_C1 * y * y * y))
    return 0.5 * (1.0 + t) + 0.5 * y * (1.0 - t * t) * GELU_C0 * (1.0 + 3.0 * GELU_C1 * y * y)


HBM_SPEC = pl.BlockSpec(memory_space=pltpu.HBM)
SEM_SPEC = pl.BlockSpec(memory_space=pltpu.SEMAPHORE)
DATAFLOW_EFFECT = pltpu.SideEffectType.DATAFLOW_SIDE_EFFECTING
OTHER_CHIPS = (2, 4, 6)


def _flip(pos, mask):
    x, y, c = pos
    return x ^ ((mask >> 2) & 1), y ^ ((mask >> 1) & 1), c ^ (mask & 1)


def _dev(pos):
    return 4 * pos[0] + 2 * pos[1] + pos[2]


def _chip(pos):
    return 2 * pos[0] + pos[1]


def _plan_gather_chips(me):
    return [(_flip(me, k), None, _dev(me), _dev(_flip(me, k))) for k in (1,) + OTHER_CHIPS]


def _plan_gather_forward(me):
    sib = _flip(me, 1)
    return [(sib, _dev(_flip(me, k)), _dev(_flip(me, k)), _dev(_flip(sib, k))) for k in OTHER_CHIPS]


def _plan_reduce_sibling(me):
    sib = _flip(me, 1)
    return [(sib, 2 * q + sib[2], q, q) for q in range(4)]


def _plan_reduce_chips(me):
    return [(_flip(me, k), _chip(_flip(me, k)), _chip(me), _chip(_flip(me, k))) for k in OTHER_CHIPS]


X_CHIP, Y_CHIP, XY_CHIP = 4, 2, 6


def _plan_halves_first(me):
    sib, xn, yn = _flip(me, 1), _flip(me, X_CHIP), _flip(me, Y_CHIP)
    return [(sib, 0, 2 * _dev(me), 2 * _dev(sib)), (sib, 1, 2 * _dev(me) + 1, 2 * _dev(sib) + 1),
            (xn, 0, 2 * _dev(me), 2 * _dev(xn)), (yn, 1, 2 * _dev(me) + 1, 2 * _dev(yn) + 1)]


def _plan_halves_second(me):
    xn, yn, dg = _flip(me, X_CHIP), _flip(me, Y_CHIP), _flip(me, XY_CHIP)
    return [(yn, 2 * _dev(me), 2 * _dev(me), 2 * _dev(yn)), (yn, 2 * _dev(xn), 2 * _dev(xn), 2 * _dev(dg)),
            (xn, 2 * _dev(me) + 1, 2 * _dev(me) + 1, 2 * _dev(xn) + 1),
            (xn, 2 * _dev(yn) + 1, 2 * _dev(yn) + 1, 2 * _dev(dg) + 1)]


def _plan_halves_forward(me):
    sib = _flip(me, 1)
    return [(sib, 2 * _dev(_flip(me, k)) + h, 2 * _dev(_flip(me, k)) + h, 2 * _dev(_flip(sib, k)) + h)
            for k in OTHER_CHIPS for h in (0, 1)]


PLAN_COPIES = {_plan_gather_chips: 4, _plan_gather_forward: 3, _plan_reduce_sibling: 4, _plan_reduce_chips: 3,
               _plan_halves_first: 4, _plan_halves_second: 4, _plan_halves_forward: 6}


def _exchange_copies(plan, src_refs, land_refs, send_sems, recv_sems, incoming=True):
    me = (lax.axis_index("x"), lax.axis_index("y"), lax.axis_index("c"))
    pairs = []
    for b, (src_ref, land_ref) in enumerate(zip(src_refs, land_refs)):
        for j, (peer, src_slot, there, here) in enumerate(plan(me)):
            sem = b * PLAN_COPIES[plan] + j
            src = src_ref if src_slot is None else src_ref.at[src_slot]
            out = pltpu.make_async_remote_copy(
                src_ref=src, dst_ref=land_ref.at[there], send_sem=send_sems.at[sem], recv_sem=recv_sems.at[sem],
                device_id=peer, device_id_type=pl.DeviceIdType.MESH)
            inc = pltpu.make_async_remote_copy(
                src_ref=src, dst_ref=land_ref.at[here], send_sem=send_sems.at[sem], recv_sem=recv_sems.at[sem],
                device_id=peer, device_id_type=pl.DeviceIdType.MESH) if incoming else None
            pairs.append((out, inc))
    return pairs


def _exchange_start(plan, srcs, lands, name, deps=()):
    srcs = [] if srcs is None else list(srcs)
    ns, n, nd = len(srcs), len(lands), len(deps)

    def body(*refs):
        land_refs = refs[ns:ns + n]
        sems_at = ns + n + nd
        pairs = _exchange_copies(plan, refs[:ns] if ns else land_refs, land_refs, refs[sems_at], refs[sems_at + 1],
                                 incoming=False)
        for out, _ in pairs:
            out.start()
        token = refs[-1]
        token[...] = jnp.zeros_like(token)

    sems = pltpu.SemaphoreType.DMA((PLAN_COPIES[plan] * n,))
    bufs = srcs + list(lands)
    outs = pl.pallas_call(
        body, name=name,
        out_shape=(sems, sems, *[pltpu.HBM(a.shape, a.dtype) for a in bufs],
                   jax.ShapeDtypeStruct((SUBLANES, LANES), F32)),
        in_specs=[HBM_SPEC] * (ns + n) + [ANY_SPEC] * nd,
        out_specs=(SEM_SPEC, SEM_SPEC, *[HBM_SPEC] * (ns + n), pl.BlockSpec(memory_space=pltpu.VMEM)),
        input_output_aliases={i: 2 + i for i in range(ns + n)},
        compiler_params=pltpu.CompilerParams(has_side_effects=DATAFLOW_EFFECT),
    )(*[pltpu.with_memory_space_constraint(a, pltpu.HBM) for a in bufs], *deps)
    return (outs[0], outs[1]), (outs[2:2 + ns] if ns else None), outs[2 + ns:2 + ns + n], outs[-1]


def _exchange_wait(plan, sems, srcs, lands, after, name):
    srcs = [] if srcs is None else list(srcs)
    ns, n = len(srcs), len(lands)

    def body(*refs):
        land_refs = refs[ns:ns + n]
        pairs = _exchange_copies(plan, refs[:ns] if ns else land_refs, land_refs, refs[ns + n], refs[ns + n + 1])
        for out, inc in pairs:
            out.wait_send()
            inc.wait_recv()

    bufs = srcs + list(lands)
    outs = pl.pallas_call(
        body, name=name,
        out_shape=[pltpu.HBM(a.shape, a.dtype) for a in bufs],
        in_specs=[HBM_SPEC] * (ns + n) + [SEM_SPEC, SEM_SPEC] + [ANY_SPEC] * len(after),
        out_specs=[HBM_SPEC] * (ns + n),
        input_output_aliases={i: i for i in range(ns + n)},
        compiler_params=pltpu.CompilerParams(has_side_effects=DATAFLOW_EFFECT),
    )(*bufs, sems[0], sems[1], *after)
    return outs[:ns], outs[ns:]


def _landing(own, slots, slot):
    land = lax.empty((slots,) + own.shape, own.dtype)
    return lax.dynamic_update_slice(land, own[None], (slot,) + (0,) * own.ndim)


def _chip_sums(pieces, land, name):
    _, r, c_ = land.shape
    tr = _tile(r, max(2 * SUBLANES, 1 << int(math.log2(4 * ADAMW_BLOCK_ELEMS // c_))))

    def body(core_ref, p_ref, l_ref, o_ref):
        o_ref[...] = (p_ref[...].astype(F32) + l_ref[...].astype(F32)).astype(o_ref.dtype)

    spec = pl.BlockSpec((None, tr, c_), lambda q, i, core: (q, i, 0))
    return pl.pallas_call(
        body, name=name,
        grid_spec=pltpu.PrefetchScalarGridSpec(
            num_scalar_prefetch=1, grid=(4, r // tr),
            in_specs=[pl.BlockSpec((None, tr, c_), lambda q, i, core: (2 * q + core[0], i, 0)), spec],
            out_specs=spec),
        out_shape=jax.ShapeDtypeStruct(land.shape, land.dtype),
        compiler_params=_params("parallel", "parallel"),
    )(lax.axis_index("c").reshape(1), pieces, land)


def _mm(a, b, *, name, nt=False, ta=False, out_dtype=F32, add=None, split_n=None, tm=512, tn=1024, deps=()):
    k, m = a.shape if ta else a.shape[::-1]
    n = b.shape[0] if nt else b.shape[1]
    tm = _tile(m, tm)
    tn = n // split_n if split_n else _tile(n, tn)
    dims = (((0 if ta else 1,), (1 if nt else 0,)), ((), ()))

    def body(*refs):
        a_ref, b_ref = refs[0], refs[1]
        o_ref = refs[-1]
        acc = lax.dot_general(a_ref[...], b_ref[...], dims, preferred_element_type=F32)
        if add is not None:
            acc = acc + refs[2][...]
        o_ref[...] = acc.astype(o_ref.dtype)

    in_specs = [pl.BlockSpec((k, tm), lambda i, j: (0, i)) if ta else pl.BlockSpec((tm, k), lambda i, j: (i, 0)),
                pl.BlockSpec((tn, k), lambda i, j: (j, 0)) if nt
                else pl.BlockSpec((k, tn), lambda i, j: (0, j))]
    args = [a, b]
    if add is not None:
        in_specs.append(pl.BlockSpec((tm, tn), lambda i, j: (i, j)))
        args.append(add)
    if split_n:
        out_shape = jax.ShapeDtypeStruct((split_n, m, tn), out_dtype)
        out_spec = pl.BlockSpec((None, tm, tn), lambda i, j: (j, i, 0))
    else:
        out_shape = jax.ShapeDtypeStruct((m, n), out_dtype)
        out_spec = pl.BlockSpec((tm, tn), lambda i, j: (i, j))
    return _pallas(
        body, args, deps, name=name, grid=(m // tm, n // tn), in_specs=in_specs, out_specs=out_spec,
        out_shape=out_shape, compiler_params=_params("parallel", "parallel"))


def _mm_win_fwd(h, w_g, name, deps=()):
    m, k = h.shape
    nj = w_g.shape[2]
    tm = _tile(m, 512)

    def body(a_ref, b_ref, o_ref):
        o_ref[...] = jnp.dot(a_ref[...], b_ref[...], preferred_element_type=F32).astype(o_ref.dtype)

    return _pallas(
        body, [h, w_g], deps, name=name, grid=(N_DEV, m // tm),
        in_specs=[pl.BlockSpec((tm, k), lambda j, i: (i, 0)),
                  pl.BlockSpec((None, k, nj), lambda j, i: (j, 0, 0))],
        out_specs=pl.BlockSpec((tm, nj), lambda j, i: (i, j)),
        out_shape=jax.ShapeDtypeStruct((m, N_DEV * nj), BF16),
        compiler_params=_params("parallel", "parallel"))


def _mm_win_bwd(dproj, w_g, name, deps=()):
    m = dproj.shape[0]
    d, nj = w_g.shape[1], w_g.shape[2]
    tm = _tile(m, 512)
    tn = _tile(d, 1024)

    per_step = 2

    def body(a_ref, b_ref, o_ref, acc_ref):
        j = pl.program_id(2)

        @pl.when(j == 0)
        def _():
            acc_ref[...] = jnp.zeros_like(acc_ref)

        part = None
        for k in range(per_step):
            term = lax.dot_general(a_ref[:, k * nj:(k + 1) * nj], b_ref[k], (((1,), (1,)), ((), ())),
                                   preferred_element_type=F32)
            part = term if part is None else part + term
        acc_ref[...] += part

        @pl.when(j == N_DEV // per_step - 1)
        def _():
            o_ref[...] = acc_ref[...]

    return _pallas(
        body, [dproj, w_g], deps, name=name, grid=(m // tm, d // tn, N_DEV // per_step),
        in_specs=[pl.BlockSpec((tm, per_step * nj), lambda i, n, j: (i, j)),
                  pl.BlockSpec((per_step, tn, nj), lambda i, n, j: (j, n, 0))],
        out_specs=pl.BlockSpec((tm, tn), lambda i, n, j: (i, n)),
        out_shape=jax.ShapeDtypeStruct((m, d), F32),
        scratch_shapes=[pltpu.VMEM((tm, tn), F32)],
        compiler_params=_params("parallel", "parallel", "arbitrary"))


def _row_spec(tr, w, col):
    return pl.BlockSpec((tr, w), lambda i: (i, col))


def _full_spec(shape):
    return pl.BlockSpec(shape, lambda i: (0,) * len(shape))


def _rmsnorm_fwd(x, g, name, deps=()):
    l, d = x.shape
    tr = _tile(l, 256)

    def body(x_ref, g_ref, o_ref):
        xv = x_ref[...]
        rstd = lax.rsqrt(jnp.mean(xv * xv, axis=-1, keepdims=True) + RMS_EPS)
        o_ref[...] = (xv * rstd * g_ref[...]).astype(o_ref.dtype)

    return _pallas(
        body, [x, g.reshape(1, d)], deps, name=name, grid=(l // tr,),
        in_specs=[_row_spec(tr, d, 0), _full_spec((1, d))],
        out_specs=_row_spec(tr, d, 0),
        out_shape=jax.ShapeDtypeStruct((l, d), BF16),
        compiler_params=_params("parallel"))


def _rmsnorm_bwd(x, g, dh, dxo, name, deps=()):
    l, d = x.shape
    tr = _tile(l, 256)

    def body(x_ref, g_ref, dh_ref, dxo_ref, dx_ref, dg_ref, dxb_ref):
        xv = x_ref[...]
        rstd = lax.rsqrt(jnp.mean(xv * xv, axis=-1, keepdims=True) + RMS_EPS)
        dhv = dh_ref[...]
        gdy = dhv * g_ref[...]
        dot = jnp.mean(gdy * xv, axis=-1, keepdims=True)
        dxv = dxo_ref[...] + rstd * gdy - xv * (rstd * rstd * rstd * dot)
        dx_ref[...] = dxv
        dxb_ref[...] = dxv.astype(dxb_ref.dtype)

        @pl.when(pl.program_id(0) == 0)
        def _():
            dg_ref[...] = jnp.zeros_like(dg_ref)

        dg_ref[...] += jnp.sum(dhv * xv * rstd, axis=0, keepdims=True)

    return _pallas(
        body, [x, g.reshape(1, d), dh, dxo], deps, name=name, grid=(l // tr,),
        in_specs=[_row_spec(tr, d, 0), _full_spec((1, d)), _row_spec(tr, d, 0), _row_spec(tr, d, 0)],
        out_specs=[_row_spec(tr, d, 0), _full_spec((1, d)), _row_spec(tr, d, 0)],
        out_shape=[jax.ShapeDtypeStruct((l, d), F32), jax.ShapeDtypeStruct((1, d), F32),
                   jax.ShapeDtypeStruct((l, d), BF16)],
        compiler_params=_params("arbitrary"))


def _final_loss(x, g, tgt, name):
    l, d = x.shape
    tr = _tile(l, 256)

    def body(x_ref, g_ref, t_ref, dx_ref, dg_ref, loss_ref, dxb_ref):
        xv = x_ref[...]
        gv = g_ref[...]
        rstd = lax.rsqrt(jnp.mean(xv * xv, axis=-1, keepdims=True) + RMS_EPS)
        xn = xv * rstd
        err = xn * gv - t_ref[...]
        dy = err * (1.0 / d)
        gdy = dy * gv
        dot = jnp.mean(gdy * xv, axis=-1, keepdims=True)
        dxv = rstd * gdy - xv * (rstd * rstd * rstd * dot)
        dx_ref[...] = dxv
        dxb_ref[...] = dxv.astype(dxb_ref.dtype)

        @pl.when(pl.program_id(0) == 0)
        def _():
            dg_ref[...] = jnp.zeros_like(dg_ref)
            loss_ref[...] = jnp.zeros_like(loss_ref)

        dg_ref[...] += jnp.sum(dy * xn, axis=0, keepdims=True)
        loss_ref[...] += (0.5 / d) * jnp.sum(err * err)

    return pl.pallas_call(
        body, name=name, grid=(l // tr,),
        in_specs=[_row_spec(tr, d, 0), _full_spec((1, d)), _row_spec(tr, d, 0)],
        out_specs=[_row_spec(tr, d, 0), _full_spec((1, d)), _full_spec((SUBLANES, LANES)), _row_spec(tr, d, 0)],
        out_shape=[jax.ShapeDtypeStruct((l, d), F32), jax.ShapeDtypeStruct((1, d), F32),
                   jax.ShapeDtypeStruct((SUBLANES, LANES), F32), jax.ShapeDtypeStruct((l, d), BF16)],
        compiler_params=_params("arbitrary"),
    )(x, g.reshape(1, d), tgt)


HALO = 2 * SUBLANES


def _halo_spec(tr, w, col, nblk, before):
    step = tr // HALO
    if before:
        return pl.BlockSpec((HALO, w), lambda i: (jnp.maximum(i * step - 1, 0), col))
    return pl.BlockSpec((HALO, w), lambda i: (jnp.minimum((i + 1) * step, nblk - 1), col))


def _shift_down(cur, before, k):
    ext = jnp.concatenate([before, cur], axis=0)
    return pltpu.roll(ext, k, axis=0)[HALO:, :]


def _shift_up(cur, after, k):
    tr = cur.shape[0]
    ext = jnp.concatenate([cur, after], axis=0)
    return pltpu.roll(ext, tr + HALO - k, axis=0)[:tr, :]


def _f32(ref):
    return ref[...].astype(F32)


def _branch_a_fwd(proj, conv_w, d, name, deps=()):
    l = proj.shape[0]
    tr = _tile(l, 256)
    nblk8 = l // HALO

    def body(v_ref, bg_ref, cg_ref, za_ref, vh_ref, cgh_ref, w_ref, o_ref):
        first = pl.program_id(0) == 0
        cv = _f32(cg_ref) * _f32(v_ref)
        cvh = jnp.where(first, 0.0, _f32(cgh_ref) * _f32(vh_ref))
        w0, w1, w2 = w_ref[0:1, :], w_ref[1:2, :], w_ref[2:3, :]
        q = w2 * cv + w1 * _shift_down(cv, cvh, 1) + w0 * _shift_down(cv, cvh, 2)
        za = _f32(za_ref)
        o_ref[...] = (_f32(bg_ref) * q * (za * _sigmoid(za))).astype(o_ref.dtype)

    return _pallas(
        body, [proj, proj, proj, proj, proj, proj, conv_w], deps, name=name, grid=(l // tr,),
        in_specs=[_row_spec(tr, d, 0), _row_spec(tr, d, 1), _row_spec(tr, d, 2), _row_spec(tr, d, 3),
                  _halo_spec(tr, d, 0, nblk8, True), _halo_spec(tr, d, 2, nblk8, True),
                  _full_spec((SUBLANES, d))],
        out_specs=_row_spec(tr, d, 0),
        out_shape=jax.ShapeDtypeStruct((l, d), BF16),
        compiler_params=_params("parallel"))


def _branch_a_bwd(proj, dpa, conv_w, dproj, d, name):
    l = proj.shape[0]
    tr = _tile(l, 128)
    nblk8 = l // HALO
    ntiles = l // tr

    def body(v_ref, bg_ref, cg_ref, za_ref, dpa_ref, vh_ref, cgh_ref, bgn_ref, zan_ref, dpan_ref,
             w_ref, _, o_ref, dw0_ref, dw1_ref, dw2_ref):
        dv_ref, dbg_ref, dcg_ref, dza_ref = [o_ref.at[:, pl.ds(k * d, d)] for k in range(4)]
        i = pl.program_id(0)
        v, bg, cg, za, dpa_v = _f32(v_ref), _f32(bg_ref), _f32(cg_ref), _f32(za_ref), _f32(dpa_ref)
        w0, w1, w2 = w_ref[0:1, :], w_ref[1:2, :], w_ref[2:3, :]
        cv = cg * v
        cvh = jnp.where(i == 0, 0.0, _f32(cgh_ref) * _f32(vh_ref))
        cv1 = _shift_down(cv, cvh, 1)
        cv2 = _shift_down(cv, cvh, 2)
        q = w2 * cv + w1 * cv1 + w0 * cv2
        sg = _sigmoid(za)
        s = za * sg
        dbg_ref[...] = (dpa_v * q * s).astype(dbg_ref.dtype)
        dza_ref[...] = (dpa_v * bg * q * (sg * (1.0 + za * (1.0 - sg)))).astype(dza_ref.dtype)
        dq = dpa_v * bg * s
        zan = _f32(zan_ref)
        dqn = jnp.where(i == ntiles - 1, 0.0, _f32(dpan_ref) * _f32(bgn_ref) * (zan * _sigmoid(zan)))
        dcv = w2 * dq + w1 * _shift_up(dq, dqn, 1) + w0 * _shift_up(dq, dqn, 2)
        dcg_ref[...] = (dcv * v).astype(dcg_ref.dtype)
        dv_ref[...] = (dcv * cg).astype(dv_ref.dtype)

        @pl.when(i == 0)
        def _():
            dw0_ref[...] = jnp.zeros_like(dw0_ref)
            dw1_ref[...] = jnp.zeros_like(dw1_ref)
            dw2_ref[...] = jnp.zeros_like(dw2_ref)

        dw0_ref[...] += jnp.sum(dq * cv2, axis=0, keepdims=True)
        dw1_ref[...] += jnp.sum(dq * cv1, axis=0, keepdims=True)
        dw2_ref[...] += jnp.sum(dq * cv, axis=0, keepdims=True)

    wsum = jax.ShapeDtypeStruct((1, d), F32)
    return pl.pallas_call(
        body, name=name, grid=(ntiles,),
        in_specs=[_row_spec(tr, d, 0), _row_spec(tr, d, 1), _row_spec(tr, d, 2), _row_spec(tr, d, 3),
                  _row_spec(tr, d, 0),
                  _halo_spec(tr, d, 0, nblk8, True), _halo_spec(tr, d, 2, nblk8, True),
                  _halo_spec(tr, d, 1, nblk8, False), _halo_spec(tr, d, 3, nblk8, False),
                  _halo_spec(tr, d, 0, nblk8, False),
                  _full_spec((SUBLANES, d)), ANY_SPEC],
        out_specs=[_row_spec(tr, 4 * d, 0)] + [_full_spec((1, d))] * 3,
        out_shape=[jax.ShapeDtypeStruct(dproj.shape, dproj.dtype)] + [wsum] * 3,
        input_output_aliases={11: 0},
        compiler_params=_params("arbitrary"),
    )(proj, proj, proj, proj, dpa, proj, proj, proj, proj, dpa, conv_w, dproj)


def _gelu_cast(y, name):
    l, w = y.shape
    tr = _tile(l, 512)

    def body(y_ref, o_ref):
        o_ref[...] = _gelu(y_ref[...]).astype(o_ref.dtype)

    return pl.pallas_call(
        body, name=name, grid=(l // tr,), in_specs=[_row_spec(tr, w, 0)],
        out_specs=_row_spec(tr, w, 0), out_shape=jax.ShapeDtypeStruct((l, w), BF16),
        compiler_params=_params("parallel"),
    )(y)


def _glu_post(y, gl, proj, b_glu, zb_col, name):
    l, w = y.shape
    tr = _tile(l, 512)

    def body(y_ref, gl_ref, zb_ref, b_ref, o_ref):
        zb = _f32(zb_ref)
        o_ref[...] = (_gelu(y_ref[...]) * _sigmoid(_f32(gl_ref) + b_ref[...])
                      * (zb * _sigmoid(zb))).astype(o_ref.dtype)

    return pl.pallas_call(
        body, name=name, grid=(l // tr,),
        in_specs=[_row_spec(tr, w, 0), _row_spec(tr, w, 0), _row_spec(tr, w, zb_col), _full_spec((1, w))],
        out_specs=_row_spec(tr, w, 0), out_shape=jax.ShapeDtypeStruct((l, w), BF16),
        compiler_params=_params("parallel"),
    )(y, gl, proj, b_glu.reshape(1, w))


def _glu_bwd1(y, gl, proj, b_glu, dpb, dproj, zb_col, name):
    l, w = y.shape
    tr = _tile(l, 512)

    def body(y_ref, gl_ref, zb_ref, b_ref, dpb_ref, _, dzb_ref, dgl_ref, t_ref, db_ref):
        zb = _f32(zb_ref)
        dpb_v = _f32(dpb_ref)
        yg = _gelu(y_ref[...])
        sgl = _sigmoid(_f32(gl_ref) + b_ref[...])
        szb = _sigmoid(zb)
        dzb_ref[...] = (dpb_v * yg * sgl * (szb * (1.0 + zb * (1.0 - szb)))).astype(dzb_ref.dtype)
        e = dpb_v * (zb * szb)
        dgl = e * yg * sgl * (1.0 - sgl)
        dgl_ref[...] = dgl.astype(dgl_ref.dtype)
        t_ref[...] = e * sgl

        @pl.when(pl.program_id(0) == 0)
        def _():
            db_ref[...] = jnp.zeros_like(db_ref)

        db_ref[...] += jnp.sum(dgl, axis=0, keepdims=True)

    return pl.pallas_call(
        body, name=name, grid=(l // tr,),
        in_specs=[_row_spec(tr, w, 0), _row_spec(tr, w, 0), _row_spec(tr, w, zb_col), _full_spec((1, w)),
                  _row_spec(tr, w, 0), ANY_SPEC],
        out_specs=[_row_spec(tr, w, zb_col)] + [_row_spec(tr, w, 0)] * 2 + [_full_spec((1, w))],
        out_shape=[jax.ShapeDtypeStruct(dproj.shape, dproj.dtype), jax.ShapeDtypeStruct((l, w), BF16),
                   jax.ShapeDtypeStruct((l, w), F32), jax.ShapeDtypeStruct((1, w), F32)],
        input_output_aliases={5: 0},
        compiler_params=_params("arbitrary"),
    )(y, gl, proj, b_glu.reshape(1, w), dpb, dproj)


def _write_cols(dproj, cols, col, name):
    l, w = cols.shape
    tr = _tile(l, 512)

    def body(c_ref, _, o_ref):
        o_ref[...] = c_ref[...].astype(o_ref.dtype)

    return pl.pallas_call(
        body, name=name, grid=(l // tr,), in_specs=[_row_spec(tr, w, 0), ANY_SPEC],
        out_specs=_row_spec(tr, w, col), out_shape=jax.ShapeDtypeStruct(dproj.shape, dproj.dtype),
        input_output_aliases={1: 0}, compiler_params=_params("parallel"),
    )(cols, dproj)


def _glu_bwd2(y, t1, dyg2, name, deps=()):
    l, w = y.shape
    tr = _tile(l, 512)

    def body(y_ref, t_ref, d_ref, o_ref):
        o_ref[...] = (t_ref[...] + _f32(d_ref)) * _gelu_grad(y_ref[...])

    return _pallas(
        body, [y, t1, dyg2], deps, name=name, grid=(l // tr,), in_specs=[_row_spec(tr, w, 0)] * 3,
        out_specs=_row_spec(tr, w, 0), out_shape=jax.ShapeDtypeStruct((l, w), F32),
        compiler_params=_params("parallel"))


def _merge_fwd(proj, ya, yb, d, ga_col, gb_col, name):
    l = proj.shape[0]
    tr = _tile(l, 256)

    def body(ga_ref, gb_ref, ya_ref, yb_ref, o_ref):
        o_ref[...] = (_sigmoid(_f32(ga_ref)) * _f32(ya_ref)
                      + _sigmoid(_f32(gb_ref)) * _f32(yb_ref)).astype(o_ref.dtype)

    return pl.pallas_call(
        body, name=name, grid=(l // tr,),
        in_specs=[_row_spec(tr, d, ga_col), _row_spec(tr, d, gb_col), _row_spec(tr, d, 0), _row_spec(tr, d, 0)],
        out_specs=_row_spec(tr, d, 0), out_shape=jax.ShapeDtypeStruct((l, d), BF16),
        compiler_params=_params("parallel"),
    )(proj, proj, ya, yb)


def _merge_bwd(proj, ya, yb, dm, d, ga_col, name):
    l, n = proj.shape
    tr = _tile(l, 256)

    def body(g_ref, ya_ref, yb_ref, dm_ref, dy_ref, dg_ref):
        dmv = _f32(dm_ref)
        sg = _sigmoid(_f32(g_ref))
        yv = jnp.where(pl.program_id(1) == 0, _f32(ya_ref), _f32(yb_ref))
        dy_ref[...] = (dmv * sg).astype(dy_ref.dtype)
        dg_ref[...] = (dmv * yv * sg * (1.0 - sg)).astype(dg_ref.dtype)

    row = pl.BlockSpec((tr, d), lambda i, j: (i, 0))
    return pl.pallas_call(
        body, name=name, grid=(l // tr, 2),
        in_specs=[pl.BlockSpec((tr, d), lambda i, j: (i, ga_col + j)), row, row, row],
        out_specs=[pl.BlockSpec((None, tr, d), lambda i, j: (j, i, 0)),
                   pl.BlockSpec((tr, d), lambda i, j: (i, ga_col + j))],
        out_shape=[jax.ShapeDtypeStruct((2, l, d), BF16), jax.ShapeDtypeStruct((l, n), BF16)],
        compiler_params=_params("parallel", "arbitrary"),
    )(proj, ya, yb, dm)


def _to_segments(a):
    l, w = a.shape
    return a.reshape(SUBLANES, l // SUBLANES, w).transpose(1, 0, 2).reshape(l, w)


def _from_segments(a):
    l, w = a.shape
    return a.reshape(l // SUBLANES, SUBLANES, w).transpose(1, 0, 2).reshape(l, w)


def _dense(z, shape):
    return jnp.broadcast_to(z, shape).reshape(-1, LANES)


def _s5_disc(are, aim, ldt):
    dt = jnp.exp(ldt)
    er = jnp.exp(are * dt)
    lbr = er * jnp.cos(aim * dt)
    lbi = er * jnp.sin(aim * dt)
    inv = 1.0 / (are * are + aim * aim)
    fr = ((lbr - 1.0) * are + lbi * aim) * inv
    fi = (lbi * are - (lbr - 1.0) * aim) * inv
    return dt, lbr, lbi, inv, fr, fi


def _s5_params(are, aim, ldt, bre, bim, input_matrix, name, deps=()):
    shape = are.shape

    def body(are_ref, aim_ref, ldt_ref, bre_ref, bim_ref, re_ref, im_ref):
        _, lbr, lbi, _, fr, fi = _s5_disc(are_ref[...], aim_ref[...], ldt_ref[...])
        if input_matrix:
            re_ref[...] = fr * bre_ref[...] - fi * bim_ref[...]
            im_ref[...] = fr * bim_ref[...] + fi * bre_ref[...]
        else:
            re_ref[...] = lbr
            im_ref[...] = lbi

    out = jax.ShapeDtypeStruct(shape, F32)
    return _pallas(body, [are, aim, ldt, bre, bim], deps, name=name,
                   in_specs=[pl.BlockSpec(memory_space=pltpu.VMEM)] * 5, out_shape=[out] * 2,
                   compiler_params=pltpu.CompilerParams(vmem_limit_bytes=VMEM_LIMIT_BYTES))


def _s5_params_bwd(are, aim, ldt, bre, bim, glbr, glbi, gbbr, gbbi, n_groups, name, deps=()):
    shape = are.shape
    rows_per_group = shape[0] // n_groups

    def body(are_ref, aim_ref, ldt_ref, bre_ref, bim_ref, glbr_ref, glbi_ref, gbbr_ref, gbbi_ref,
             gar_ref, gai_ref, gdt_ref, gbr_ref, gbi_ref):
        are_v, aim_v = are_ref[...], aim_ref[...]
        bre_v, bim_v = bre_ref[...], bim_ref[...]
        gbbr_v, gbbi_v = gbbr_ref[...], gbbi_ref[...]
        dt, lbr, lbi, inv, fr, fi = _s5_disc(are_v, aim_v, ldt_ref[...])
        gbr_ref[...] = fr * gbbr_v + fi * gbbi_v
        gbi_ref[...] = fr * gbbi_v - fi * gbbr_v
        lane_group = lax.broadcasted_iota(jnp.int32, (LANES, LANES), 0) // S5_GROUP
        same_group = (lane_group == lax.broadcasted_iota(jnp.int32, (LANES, LANES), 1) // S5_GROUP)
        ones = same_group.astype(F32)
        gfr = jnp.dot(bre_v * gbbr_v + bim_v * gbbi_v, ones, precision=HIGHEST, preferred_element_type=F32)
        gfi = jnp.dot(bre_v * gbbi_v - bim_v * gbbr_v, ones, precision=HIGHEST, preferred_element_type=F32)
        glr = glbr_ref[...] + (are_v * gfr - aim_v * gfi) * inv
        gli = glbi_ref[...] + (are_v * gfi + aim_v * gfr) * inv
        qr = (fr * are_v + fi * aim_v) * inv
        qi = (fi * are_v - fr * aim_v) * inv
        gzr = lbr * glr + lbi * gli
        gzi = lbr * gli - lbi * glr
        gar_ref[...] = dt * gzr - (qr * gfr + qi * gfi)
        gai_ref[...] = dt * gzi - (qr * gfi - qi * gfr)
        e = dt * (are_v * gzr + aim_v * gzi)
        per_group = jnp.sum(e.reshape(n_groups, rows_per_group, LANES), axis=1)
        total = jnp.sum(per_group, axis=1, keepdims=True) * (1.0 / S5_GROUP)
        gdt_ref[...] = jnp.broadcast_to(total, gdt_ref.shape)

    out = jax.ShapeDtypeStruct(shape, F32)
    return _pallas(
        body, [are, aim, ldt, bre, bim, glbr, glbi, gbbr, gbbi], deps, name=name,
        in_specs=[pl.BlockSpec(memory_space=pltpu.VMEM)] * 9,
        out_shape=[out, out, jax.ShapeDtypeStruct((n_groups, LANES), F32), out, out],
        compiler_params=pltpu.CompilerParams(vmem_limit_bytes=VMEM_LIMIT_BYTES))


def _cmul(ar, ai, br, bi):
    return ar * br - ai * bi, ar * bi + ai * br


def _scan_in_place(hr_ref, hi_ref, lr, li, reverse):
    l, wb = hr_ref.shape
    nt = l // SUBLANES
    shift = SUBLANES - 1 if reverse else 1
    unroll = 8 if nt % 8 == 0 else 1

    def rows(k):
        t = (nt - 1 - k) if reverse else k
        return pl.ds(pl.multiple_of(t * SUBLANES, SUBLANES), SUBLANES)

    zero = jnp.zeros((SUBLANES, wb), F32)

    def local_step(k, carry):
        hr, hi = carry
        r = rows(k)
        tr_, ti_ = _cmul(lr, li, hr, hi)
        hr, hi = tr_ + hr_ref[r, :], ti_ + hi_ref[r, :]
        hr_ref[r, :] = hr
        hi_ref[r, :] = hi
        return hr, hi

    er, ei = lax.fori_loop(0, nt, local_step, (zero, zero), unroll=unroll)

    lnr = lni = None
    br, bi, n = lr, li, nt
    while n:
        if n & 1:
            lnr, lni = (br, bi) if lnr is None else _cmul(lnr, lni, br, bi)
        n >>= 1
        if n:
            br, bi = _cmul(br, bi, br, bi)

    row = lax.broadcasted_iota(jnp.int32, (SUBLANES, wb), 0)
    tr_, ti_ = er, ei
    for j in range(1, SUBLANES):
        pr_, pi_ = _cmul(lnr, lni, pltpu.roll(tr_, shift, axis=0), pltpu.roll(ti_, shift, axis=0))
        at = row == ((SUBLANES - 1 - j) if reverse else j)
        tr_ = jnp.where(at, er + pr_, tr_)
        ti_ = jnp.where(at, ei + pi_, ti_)
    edge = row == ((SUBLANES - 1) if reverse else 0)
    cr = jnp.where(edge, 0.0, pltpu.roll(tr_, shift, axis=0))
    ci = jnp.where(edge, 0.0, pltpu.roll(ti_, shift, axis=0))

    def fix_step(k, carry):
        zr, zi = _cmul(lr, li, *carry)
        r = rows(k)
        hr_ref[r, :] = hr_ref[r, :] + zr
        hi_ref[r, :] = hi_ref[r, :] + zi
        return zr, zi

    lax.fori_loop(0, nt, fix_step, (cr, ci), unroll=unroll)


def _dot(a, b):
    return jnp.dot(a.astype(BF16), b.astype(BF16), preferred_element_type=F32)


def _s5_forward(u_seg, mb_re, mb_im, mc_re, mc_im, lam_re, lam_im, dvec, name, deps=()):
    l = u_seg.shape[0]
    nb, kin, kst = mb_re.shape

    def body(u_ref, mbr_ref, mbi_ref, mcr_ref, mci_ref, lr_ref, li_ref, d_ref, hr_ref, hi_ref, y_ref):
        u = u_ref[...]
        hr_ref[...] = _dot(u, mbr_ref[...])
        hi_ref[...] = _dot(u, mbi_ref[...])
        _scan_in_place(hr_ref, hi_ref, jnp.broadcast_to(lr_ref[...], (SUBLANES, kst)),
                       jnp.broadcast_to(li_ref[...], (SUBLANES, kst)), False)
        y_ref[...] = (_dot(hr_ref[...], mcr_ref[...]) - _dot(hi_ref[...], mci_ref[...])
                      + d_ref[...] * u.astype(F32))

    act = pl.BlockSpec((l, kin), lambda b: (0, b))
    state = pl.BlockSpec((l, kst), lambda b: (0, b))
    up = pl.BlockSpec((None, kin, kst), lambda b: (b, 0, 0))
    down = pl.BlockSpec((None, kst, kin), lambda b: (b, 0, 0))
    hshape = jax.ShapeDtypeStruct((l, nb * kst), F32)
    return _pallas(
        body, [u_seg, mb_re, mb_im, mc_re, mc_im, lam_re, lam_im, dvec], deps, name=name, grid=(nb,),
        in_specs=[act, up, up, down, down, pl.BlockSpec((1, kst), lambda b: (0, b)),
                  pl.BlockSpec((1, kst), lambda b: (0, b)), pl.BlockSpec((1, kin), lambda b: (0, b))],
        out_specs=[state, state, act],
        out_shape=[hshape, hshape, jax.ShapeDtypeStruct((l, nb * kin), F32)],
        compiler_params=_params("parallel"))


def _dot_ta(a, b):
    return lax.dot_general(a.astype(BF16), b.astype(BF16), (((0,), (0,)), ((), ())), preferred_element_type=F32)


def _dot_nt(a, b):
    return lax.dot_general(a.astype(BF16), b.astype(BF16), (((1,), (1,)), ((), ())), preferred_element_type=F32)


def _s5_backward(dy_seg, u_seg, h_re, h_im, mb_re, mb_im, mc_re, mc_im, lam_re, lam_im_neg, dvec, name):
    l = dy_seg.shape[0]
    nb, kin, kst = mb_re.shape
    nt = l // SUBLANES

    def body(dy_ref, u_ref, hr_ref, hi_ref, mbr_ref, mbi_ref, mcr_ref, mci_ref, lr_ref, li_ref,
             d_ref, du_ref, gcr_ref, gci_ref, gbr_ref, gbi_ref, glr_ref, gli_ref, dsk_ref, qr_ref, qi_ref):
        dy = dy_ref[...]
        qr_ref[...] = _dot_nt(dy, mcr_ref[...])
        qi_ref[...] = -_dot_nt(dy, mci_ref[...])
        _scan_in_place(qr_ref, qi_ref, jnp.broadcast_to(lr_ref[...], (SUBLANES, kst)),
                       jnp.broadcast_to(li_ref[...], (SUBLANES, kst)), True)
        du_ref[...] = _dot_nt(qr_ref[...], mbr_ref[...]) + _dot_nt(qi_ref[...], mbi_ref[...]) + d_ref[...] * dy
        dsk_ref[...] = jnp.sum(dy * _f32(u_ref), axis=0, keepdims=True)
        gcr_ref[...] = _dot_ta(dy, hr_ref[...])
        gci_ref[...] = _dot_ta(dy, hi_ref[...])
        gbr_ref[...] = _dot_ta(u_ref[...], qr_ref[...])
        gbi_ref[...] = _dot_ta(u_ref[...], qi_ref[...])

        row = lax.broadcasted_iota(jnp.int32, (SUBLANES, kst), 0)
        last = pl.ds((nt - 1) * SUBLANES, SUBLANES)
        first = pl.ds(0, SUBLANES)
        pr = jnp.where(row == 0, 0.0, pltpu.roll(hr_ref[last, :], 1, axis=0))
        pi = jnp.where(row == 0, 0.0, pltpu.roll(hi_ref[last, :], 1, axis=0))
        gr, gi = qr_ref[first, :], qi_ref[first, :]

        def step(t, carry):
            acc_r, acc_i = carry
            cur = pl.ds(pl.multiple_of(t * SUBLANES, SUBLANES), SUBLANES)
            prev = pl.ds(pl.multiple_of((t - 1) * SUBLANES, SUBLANES), SUBLANES)
            gr, gi = qr_ref[cur, :], qi_ref[cur, :]
            pr, pi = hr_ref[prev, :], hi_ref[prev, :]
            return acc_r + gr * pr + gi * pi, acc_i + gi * pr - gr * pi

        acc_r, acc_i = lax.fori_loop(1, nt, step, (gr * pr + gi * pi, gi * pr - gr * pi))
        glr_ref[...] = jnp.sum(acc_r, axis=0, keepdims=True)
        gli_ref[...] = jnp.sum(acc_i, axis=0, keepdims=True)

    act = pl.BlockSpec((l, kin), lambda b: (0, b))
    state = pl.BlockSpec((l, kst), lambda b: (0, b))
    up = pl.BlockSpec((None, kin, kst), lambda b: (b, 0, 0))
    down = pl.BlockSpec((None, kst, kin), lambda b: (b, 0, 0))
    vec_st = pl.BlockSpec((1, kst), lambda b: (0, b))
    vec_in = pl.BlockSpec((1, kin), lambda b: (0, b))
    outer = jax.ShapeDtypeStruct((nb, kin, kst), F32)
    lam_shape = jax.ShapeDtypeStruct((1, nb * kst), F32)
    return pl.pallas_call(
        body, name=name, grid=(nb,),
        in_specs=[act, act, state, state, up, up, down, down, vec_st, vec_st, vec_in],
        out_specs=[act, up, up, up, up, vec_st, vec_st, vec_in],
        out_shape=[jax.ShapeDtypeStruct((l, nb * kin), F32), outer, outer, outer, outer, lam_shape, lam_shape,
                   jax.ShapeDtypeStruct((1, nb * kin), F32)],
        scratch_shapes=[pltpu.VMEM((l, kst), F32), pltpu.VMEM((l, kst), F32)],
        compiler_params=_params("parallel"),
    )(dy_seg, u_seg, h_re, h_im, mb_re, mb_im, mc_re, mc_im, lam_re, lam_im_neg, dvec)


def _block_diag(m, nb):
    g, r, s = m.shape
    gb = g // nb
    tiled = jnp.tile(m.reshape(nb, gb * r, s), (1, 1, gb))
    row_group = lax.broadcasted_iota(jnp.int32, tiled.shape, 1) // r
    col_group = lax.broadcasted_iota(jnp.int32, tiled.shape, 2) // s
    return jnp.where(row_group == col_group, tiled, 0.0)


def _block_diag_extract(mat, g, r, s):
    nb = mat.shape[0]
    gb = g // nb
    eye = jnp.eye(gb, dtype=mat.dtype)
    m5 = mat.reshape(nb, gb, r, gb, s) * eye[None, :, None, :, None]
    return jnp.sum(m5, axis=3).reshape(g, r, s)


def _adamw(w, m, v, gslots, name, layer=0, prev=None, deps=()):
    layers, r, c = w.shape
    s = gslots.shape[0]
    tr = _tile(r, max(SUBLANES, 1 << int(math.log2(ADAMW_BLOCK_ELEMS // c))))
    bc1 = 1.0 / (1.0 - ADAM_B1 ** ADAM_STEP)
    bc2 = 1.0 / (1.0 - ADAM_B2 ** ADAM_STEP)

    def body(w_ref, m_ref, v_ref, g_ref, *rest):
        go_ref, d_ref, mo_ref, vo_ref = rest[-4:]
        g = g_ref[0].astype(F32)
        for k in range(1, s):
            g = g + g_ref[k].astype(F32)
        mn = ADAM_B1 * m_ref[...] + (1.0 - ADAM_B1) * g
        vn = ADAM_B2 * v_ref[...] + (1.0 - ADAM_B2) * (g * g)
        go_ref[...] = g
        mo_ref[...] = mn
        vo_ref[...] = vn
        d_ref[...] = -ADAM_LR * ((mn * bc1) / (jnp.sqrt(vn * bc2) + ADAM_EPS) + ADAM_WD * w_ref[...])

    spec = pl.BlockSpec((None, tr, c), lambda i: (layer, i, 0))
    out = jax.ShapeDtypeStruct((layers, r, c), F32)
    in_specs = [spec, spec, spec, pl.BlockSpec((s, tr, c), lambda i: (0, i, 0))]
    args = [w, m, v, gslots]
    aliases = {}
    if prev is not None:
        in_specs += [ANY_SPEC] * 4
        args += list(prev)
        aliases = {4 + q: q for q in range(4)}
    in_specs += [ANY_SPEC] * len(deps)
    args += list(deps)
    return pl.pallas_call(
        body, name=name, grid=(r // tr,), in_specs=in_specs,
        out_specs=[spec] * 4, out_shape=[out] * 4, input_output_aliases=aliases,
        compiler_params=_params("parallel"),
    )(*args)


def _pack(parts):
    flat = jnp.concatenate([p.reshape(-1) for p in parts])
    pad = (-flat.shape[0]) % (PACK_ROWS * LANES)
    return jnp.pad(flat, (0, pad)).reshape(-1, LANES)


def _unpack(packed, shapes):
    flat = packed.reshape(-1)
    out, off = [], 0
    for shp in shapes:
        size = math.prod(shp)
        out.append(flat[off:off + size].reshape(shp))
        off += size
    return out


def kernel(x, norm_g, w_in, conv_w, w_out_a, a_re, a_im, log_dt, b_re, b_im, c_re, c_im, d_skip, w_glu, b_glu, w_out_b, w_o, final_g, loss_target, m_norm_g, m_w_in, m_conv_w, m_w_out_a, m_a_re, m_a_im, m_log_dt, m_b_re, m_b_im, m_c_re, m_c_im, m_d_skip, m_w_glu, m_b_glu, m_w_out_b, m_w_o, m_final_g, v_norm_g, v_w_in, v_conv_w, v_w_out_a, v_a_re, v_a_im, v_log_dt, v_b_re, v_b_im, v_c_re, v_c_im, v_d_skip, v_w_glu, v_b_glu, v_w_out_b, v_w_o, v_final_g):
    depth = norm_g.shape[0]
    l, d = x.shape[1], x.shape[2]
    ws = w_glu.shape[2]
    n_groups, n_state = a_re.shape[1], a_re.shape[2]
    nb = ws // LANES
    assert S5_GROUP == b_re.shape[3] and n_state * S5_GB == 4 * LANES
    u_col, zb_col = 4 * d // ws, 4 * d // ws + 1
    ga_col, gb_col = (4 * d + 2 * ws) // d, (4 * d + 2 * ws) // d + 1
    me = 4 * lax.axis_index("x") + 2 * lax.axis_index("y") + lax.axis_index("c")

    xs = [x[0]]
    tgt = loss_target[0]

    big_names = ("w_in", "w_out_a", "w_glu", "w_out_b", "w_o")
    big = dict(w_in=(w_in, m_w_in, v_w_in), w_out_a=(w_out_a, m_w_out_a, v_w_out_a),
               w_glu=(w_glu, m_w_glu, v_w_glu), w_out_b=(w_out_b, m_w_out_b, v_w_out_b),
               w_o=(w_o, m_w_o, v_w_o))

    all_shards = [[None] * len(big_names) for _ in range(depth)]
    all_shards[0][0] = w_in[0].astype(BF16)

    def shards_bf16(i):
        return all_shards[i]

    main_names = ("a_re", "a_im", "log_dt", "b_re", "b_im", "c_re", "c_im", "d_skip", "b_glu", "conv_w")
    small_w = dict(a_re=(a_re, m_a_re, v_a_re), a_im=(a_im, m_a_im, v_a_im),
                   log_dt=(log_dt, m_log_dt, v_log_dt), b_re=(b_re, m_b_re, v_b_re), b_im=(b_im, m_b_im, v_b_im),
                   c_re=(c_re, m_c_re, v_c_re), c_im=(c_im, m_c_im, v_c_im), d_skip=(d_skip, m_d_skip, v_d_skip),
                   b_glu=(b_glu, m_b_glu, v_b_glu))
    main_shapes = [small_w[k][0].shape for k in main_names[:-1]] + [(depth, 3, d)]
    dc = d // N_DEV
    pad8 = lambda a: jnp.pad(a.reshape(depth * 3, dc), ((0, SUBLANES - depth * 3), (0, 0)))[None]

    def input_only_work(zero):
        for i in range(depth):
            for j, k in enumerate(big_names):
                if all_shards[i][j] is None:
                    all_shards[i][j] = (big[k][0][i] + zero).astype(BF16)
        zeros_conv = jnp.zeros((depth, 3, d), F32) + zero
        zero1 = jnp.zeros((1,), F32) + zero
        main = [_pack([small_w[k][q] for k in main_names[:-1]] + [zeros_conv])[None] for q in range(3)]
        gains = [_pack([g_, f_, zero1])[None]
                 for g_, f_ in ((norm_g, final_g), (m_norm_g, m_final_g), (v_norm_g, v_final_g))]
        return main, gains, [pad8(conv_w), pad8(m_conv_w), pad8(v_conv_w)]

    def gather_start(shards, name, deps=()):
        sems, srcs, lands, token = _exchange_start(
            _plan_gather_chips, shards, [_landing(s_, N_DEV, me) for s_ in shards], f"{name}_start", deps)
        return (name, sems, srcs, lands), token

    def gather_forward(state, after, deps=()):
        name, sems, srcs, lands = state
        _, lands = _exchange_wait(_plan_gather_chips, sems, srcs, lands, after, f"{name}_wait")
        sems, _, lands, token = _exchange_start(_plan_gather_forward, None, lands, f"{name}_forward_start", deps)
        return (name, sems, lands), token

    def gather_finish(state, after):
        name, sems, lands = state
        return _exchange_wait(_plan_gather_forward, sems, None, lands, after, f"{name}_forward_wait")[1]

    def halves_start(shards, name, deps=()):
        srcs = [s_.reshape(2, s_.shape[0] // 2, *s_.shape[1:]) for s_ in shards]
        lands = [lax.dynamic_update_slice(lax.empty((2 * N_DEV,) + h_.shape[1:], h_.dtype), h_, (2 * me, 0, 0))
                 for h_ in srcs]
        sems, srcs, lands, token = _exchange_start(_plan_halves_first, srcs, lands, f"{name}_start", deps)
        return (name, sems, srcs, lands), token

    def halves_second(state, after, deps=()):
        name, sems, srcs, lands = state
        _, lands = _exchange_wait(_plan_halves_first, sems, srcs, lands, after, f"{name}_wait")
        sems, _, lands, token = _exchange_start(_plan_halves_second, None, lands, f"{name}_second_start", deps)
        return (name, sems, lands), token

    def halves_forward(state, after, deps=()):
        name, sems, lands = state
        _, lands = _exchange_wait(_plan_halves_second, sems, None, lands, after, f"{name}_second_wait")
        sems, _, lands, token = _exchange_start(_plan_halves_forward, None, lands, f"{name}_forward_start", deps)
        return (name, sems, lands), token

    def halves_finish(state, after):
        name, sems, lands = state
        lands = _exchange_wait(_plan_halves_forward, sems, None, lands, after, f"{name}_forward_wait")[1]
        return [l_.reshape(N_DEV, 2 * l_.shape[1], *l_.shape[2:]) for l_ in lands]

    conv_shard = jnp.pad(conv_w.reshape(depth * 3, -1), ((0, HALO - depth * 3), (0, 0)))
    w_in_state, token = halves_start([shards_bf16(0)[0], conv_shard], "ag_w_in_0")
    s5 = []
    shape3 = (n_groups, n_state, S5_GROUP)
    for i in range(depth):
        dense_in = (_dense(a_re[i][:, :, None], shape3), _dense(a_im[i][:, :, None], shape3),
                    _dense(log_dt[i][:, None, None], shape3), b_re[i].reshape(-1, LANES), b_im[i].reshape(-1, LANES))
        lbr, lbi = _s5_params(*dense_in, False, f"s5_lam_{i}", deps=(token,))
        s5.append(dict(dense_in=dense_in, lam_re=lbr.reshape(shape3)[:, :, 0].reshape(1, -1),
                       lam_im=lbi.reshape(shape3)[:, :, 0].reshape(1, -1)))
    main_wmv, gains_wmv, conv_wmv = input_only_work(token[0, 0])
    w_in_state, token = halves_second(w_in_state, [p[k] for p in s5 for k in ("lam_re", "lam_im")] + main_wmv)
    for i in range(depth):
        bbr, bbi = _s5_params(*s5[i]["dense_in"], True, f"s5_input_matrix_{i}", deps=(token,))
        bbr3, bbi3 = bbr.reshape(shape3), bbi.reshape(shape3)
        diag = lambda m: _block_diag(m, nb).astype(BF16)
        s5[i].update(up=(diag(bbr3.transpose(0, 2, 1)), diag(bbi3.transpose(0, 2, 1))),
                     down=(diag((c_re[i] + token[0, 0]).transpose(0, 2, 1)),
                           diag((c_im[i] + token[0, 0]).transpose(0, 2, 1))))
    prologue = [m for p in s5 for k in ("up", "down") for m in p[k]]
    prologue += main_wmv + gains_wmv + conv_wmv + all_shards[0][1:] + [s_ for sh in all_shards[1:] for s_ in sh]
    w_in_state, token = halves_forward(w_in_state, prologue, deps=(token,))
    rest_state, token = halves_start(shards_bf16(0)[1:], "ag_rest_0", deps=(token,))
    next_state = None
    if depth > 1:
        next_state, token = halves_start([shards_bf16(1)[0]], "ag_w_in_1", deps=(token,))

    saved = []
    wg = [None] * depth
    conv_full = None
    for i in range(depth):
        xi = xs[-1]
        h = _rmsnorm_fwd(xi, norm_g[i], f"rmsnorm_fwd_{i}", deps=(token,))
        arrived = halves_finish(w_in_state, [h])
        if i == 0:
            conv_full = arrived[1].transpose(1, 0, 2).reshape(HALO, d)[:depth * 3].reshape(depth, 3, d)
        conv8 = jnp.pad(conv_full[i], ((0, SUBLANES - 3), (0, 0)))
        proj = _mm_win_fwd(h, arrived[0], f"mm_proj_{i}")
        rest_state, token = halves_second(rest_state, [proj])
        if next_state is not None:
            next_state, token = halves_second(next_state, [token], deps=(token,))
        u_seg = _to_segments(proj[:, 4 * d:4 * d + ws])
        h_re, h_im, y_seg = _s5_forward(u_seg, *s5[i]["up"], *s5[i]["down"], s5[i]["lam_re"], s5[i]["lam_im"],
                                        d_skip[i].reshape(1, ws), f"s5_forward_{i}", deps=(token,))
        rest_state, token = halves_forward(rest_state, [y_seg])
        pa = _branch_a_fwd(proj, conv8, d, f"branch_a_fwd_{i}", deps=(token,))
        rest = halves_finish(rest_state, [pa])
        wg[i] = g = dict(w_in=arrived[0], w_a=rest[0].reshape(d, d), w_glu=rest[1].reshape(ws, ws),
                         w_b=rest[2], w_o=rest[3].reshape(d, d))
        ya = _mm(pa, g["w_a"], name=f"mm_ya_{i}", out_dtype=BF16)
        y = _from_segments(y_seg)
        yg = _gelu_cast(y, f"gelu_{i}")
        gl = _mm(yg, g["w_glu"], name=f"mm_glu_{i}", out_dtype=BF16)
        pb = _glu_post(y, gl, proj, b_glu[i], zb_col, f"glu_post_{i}")
        w_b2d = g["w_b"].transpose(1, 0, 2).reshape(ws, d)
        yb = _mm(pb, w_b2d, name=f"mm_yb_{i}", out_dtype=BF16)
        mrg = _merge_fwd(proj, ya, yb, d, ga_col, gb_col, f"merge_fwd_{i}")
        deps = ()
        if i + 1 < depth:
            w_in_state, token = halves_forward(next_state, [mrg])
            rest_state, token = halves_start(shards_bf16(i + 1)[1:], f"ag_rest_{i + 1}", deps=(token,))
            next_state = None
            if i + 2 < depth:
                next_state, token = halves_start([shards_bf16(i + 2)[0]], f"ag_w_in_{i + 2}", deps=(token,))
            deps = (token,)
        xs.append(_mm(mrg, g["w_o"], name=f"mm_out_{i}", add=xi, deps=deps))
        saved.append(dict(h=h, proj=proj, pa=pa, ya=ya, yb=yb, y=y, yg=yg, gl=gl, pb=pb, mrg=mrg,
                          u_seg=u_seg, h_re=h_re, h_im=h_im, conv8=conv8, w_b2d=w_b2d))

    dx, g_final, loss_part, dxo_b = _final_loss(xs[-1], final_g, tgt, "final_loss")

    rs_pending = []
    small = {k: [None] * depth for k in ("norm_g", "a_re", "a_im", "log_dt", "b_re", "b_im", "c_re", "c_im",
                                         "d_skip", "b_glu", "conv_w")}

    my_chip = 2 * lax.axis_index("x") + lax.axis_index("y")

    def reduce_on_chip(pieces, tag):
        lands = [lax.empty((4,) + p.shape[1:], p.dtype) for p in pieces]
        sems, srcs, lands, token = _exchange_start(_plan_reduce_sibling, pieces, lands, f"rs_sibling_start_{tag}")
        return (sems, srcs, lands), token

    def reduce_across_chips(names_, state, layer, tag, after):
        sems, srcs, lands = state
        srcs, lands = _exchange_wait(_plan_reduce_sibling, sems, srcs, lands, after, f"rs_sibling_wait_{tag}")
        sums = [_chip_sums(p, l_, f"chip_sum_{k}_{layer}") for k, p, l_ in zip(names_, srcs, lands)]
        lands = [_landing(lax.dynamic_index_in_dim(s_, my_chip, 0, keepdims=False), 4, my_chip) for s_ in sums]
        sems, srcs, lands, token = _exchange_start(_plan_reduce_chips, sums, lands, f"rs_chips_start_{tag}")
        rs_pending.append((names_, layer, sems, srcs, lands, f"rs_chips_wait_{tag}"))
        return token

    for i in reversed(range(depth)):
        s, g = saved[i], wg[i]
        proj = s["proj"]
        dm = _mm(dxo_b, g["w_o"], name=f"mm_dm_{i}", nt=True, out_dtype=BF16)
        gw_o = _mm(s["mrg"], dxo_b, ta=True, name=f"mm_gw_o_{i}", out_dtype=BF16)
        dy2, dproj = _merge_bwd(proj, s["ya"], s["yb"], dm, d, ga_col, f"merge_bwd_{i}")
        dya, dyb = dy2[0], dy2[1]
        dpa = _mm(dya, g["w_a"], name=f"mm_dpa_{i}", nt=True, out_dtype=BF16)
        gw_a = _mm(s["pa"], dya, ta=True, name=f"mm_gw_a_{i}", out_dtype=BF16)
        dpb = _mm(dyb, s["w_b2d"], name=f"mm_dpb_{i}", nt=True, out_dtype=BF16)
        gw_b = _mm(s["pb"], dyb, ta=True, name=f"mm_gw_b_{i}", split_n=N_DEV, out_dtype=BF16)
        dproj, dw0, dw1, dw2 = _branch_a_bwd(proj, dpa, s["conv8"], dproj, d, f"branch_a_bwd_{i}")
        small["conv_w"][i] = jnp.concatenate([dw0, dw1, dw2], axis=0)
        dproj, dgl, t1, db_glu = _glu_bwd1(s["y"], s["gl"], proj, b_glu[i], dpb, dproj, zb_col, f"glu_bwd1_{i}")
        small["b_glu"][i] = db_glu.reshape(ws)
        dyg2 = _mm(dgl, g["w_glu"], name=f"mm_dyg_{i}", nt=True, out_dtype=BF16)
        gw_glu = _mm(s["yg"], dgl, ta=True, name=f"mm_gw_glu_{i}", out_dtype=BF16)
        small_names_ = ("w_out_a", "w_glu", "w_out_b", "w_o")
        state, token = reduce_on_chip(
            [gw_a.reshape(N_DEV, d // N_DEV, d), gw_glu.reshape(N_DEV, ws // N_DEV, ws), gw_b,
             gw_o.reshape(N_DEV, d // N_DEV, d)], f"small_{i}")
        dy = _glu_bwd2(s["y"], t1, dyg2, f"glu_bwd2_{i}", deps=(token,))
        dy_seg = _to_segments(dy)
        u_seg = s["u_seg"]
        du_seg, gc_re, gc_im, gbb_re, gbb_im, glam_re, glam_im, dskip = _s5_backward(
            dy_seg, u_seg, s["h_re"], s["h_im"], *s5[i]["up"], *s5[i]["down"],
            s5[i]["lam_re"], -s5[i]["lam_im"], d_skip[i].reshape(1, ws), f"s5_backward_{i}")
        token = reduce_across_chips(small_names_, state, i, f"small_{i}", [du_seg])
        dproj = _write_cols(dproj, _from_segments(du_seg), u_col, f"write_du_{i}")
        gw_in = _mm(s["h"], dproj, ta=True, name=f"mm_gw_in_{i}", split_n=N_DEV, tm=1024,
                    out_dtype=BF16, deps=(token,))
        state, token = reduce_on_chip([gw_in], f"w_in_{i}")
        small["d_skip"][i] = dskip.reshape(n_groups, S5_GROUP)
        small["c_re"][i] = _block_diag_extract(gc_re, n_groups, S5_GROUP, n_state)
        small["c_im"][i] = -_block_diag_extract(gc_im, n_groups, S5_GROUP, n_state)
        gbb_re = _block_diag_extract(gbb_re, n_groups, S5_GROUP, n_state).transpose(0, 2, 1)
        gbb_im = _block_diag_extract(gbb_im, n_groups, S5_GROUP, n_state).transpose(0, 2, 1)
        gar, gai, gdt, gbr, gbi = _s5_params_bwd(
            *s5[i]["dense_in"], _dense(glam_re.reshape(n_groups, n_state, 1), shape3),
            _dense(glam_im.reshape(n_groups, n_state, 1), shape3),
            gbb_re.reshape(-1, LANES), gbb_im.reshape(-1, LANES), n_groups, f"s5_params_bwd_{i}", deps=(token,))
        small["a_re"][i] = gar.reshape(shape3)[:, :, 0]
        small["a_im"][i] = gai.reshape(shape3)[:, :, 0]
        small["log_dt"][i] = gdt[:, 0]
        small["b_re"][i] = gbr.reshape(shape3)
        small["b_im"][i] = gbi.reshape(shape3)
        if i == 0:
            part = {k: jnp.stack(small[k]) for k in main_names}
            main_state, token = gather_start([_pack([part[k] for k in main_names]).astype(BF16)], "ag_small")
            token = reduce_across_chips(("w_in",), state, i, f"w_in_{i}", [token])
            main_state, token = gather_forward(main_state, [token])
            dh = _mm_win_bwd(dproj, g["w_in"], f"mm_dh_{i}", deps=(token,))
            main_slots = gather_finish(main_state, [dh])[0]
            deps = ()
        else:
            dh = _mm_win_bwd(dproj, g["w_in"], f"mm_dh_{i}", deps=(gar,))
            deps = (reduce_across_chips(("w_in",), state, i, f"w_in_{i}", [dh]),)
        dx, dng, dxo_b = _rmsnorm_bwd(xs[i], norm_g[i], dh, dx, f"rmsnorm_bwd_{i}", deps=deps)
        small["norm_g"][i] = dng.reshape(d)

    results = {}

    gain_grads = jnp.concatenate([jnp.stack(small["norm_g"]).reshape(-1), g_final.reshape(d)])
    gain_shapes = [(depth, d), (d,), (1,)]
    gains_state, token = gather_start([_pack([gain_grads, loss_part[0, :1]])], "ag_gains")

    sres = [_unpack(p[0], main_shapes) for p in _adamw(*main_wmv, main_slots, "adamw_small", deps=(token,))]
    for j, k in enumerate(main_names[:-1]):
        results[k] = [sres[q][j] for q in range(4)]
    gconv = lax.dynamic_slice_in_dim(sres[0][-1], me * dc, dc, axis=2)
    cres = _adamw(*conv_wmv, pad8(gconv), "adamw_conv_w")
    results["conv_w"] = [r_[0, :depth * 3].reshape(depth, 3, dc) for r_ in cres]

    after = [cres[0]]
    for names_, layer, sems, srcs, lands, wait_name in rs_pending:
        _, slots = _exchange_wait(_plan_reduce_chips, sems, srcs, lands, after, wait_name)
        for k, land in zip(names_, slots):
            w_, m_, v_ = big[k]
            results[k] = _adamw(w_, m_, v_, land, f"adamw_{k}_{layer}", layer=layer, prev=results.get(k))
            after = [results[k][0]]

    gains_state, token = gather_forward(gains_state, after)
    gpack = gather_finish(gains_state, [token])[0]
    gres = [_unpack(p[0], gain_shapes) for p in _adamw(*gains_wmv, gpack, "adamw_gains")]
    results["norm_g"] = [gres[q][0] for q in range(4)]
    results["final_g"] = [gres[q][1] for q in range(4)]
    loss = gres[0][2][0]

    names = ("norm_g", "w_in", "conv_w", "w_out_a", "a_re", "a_im", "log_dt", "b_re", "b_im", "c_re", "c_im",
             "d_skip", "w_glu", "b_glu", "w_out_b", "w_o", "final_g")
    outs = [loss, dx[None]]
    for q in range(4):
        outs += [results[k][q] for k in names]
    return tuple(outs)
```
